```python
import numpy as np
import jax
import jax.numpy as jnp
from jax import lax

D_MODEL = 1024
BATCH = 8
SEQ = 2048
DEPTH = 4

MEM_LEN = 256
HEAD_DIM = 64
N_MIXERS = 4
MIX_W = 256
Q_BLOCK = 128

NSA_HEADS = 4
NSA_KV_HEADS = 2
NSA_GROUP = NSA_HEADS // NSA_KV_HEADS
CMP_LEN = 32
CMP_STRIDE = 16
CMP_HID = 256
SEL_BLOCK = 64
SEL_TOPK = 8
WINDOW = 512
FORCE_SCORE = 1e4

SB_HEADS = 4

RNN_W = 256
RNN_BLOCKS = 4
RNN_BW = RNN_W // RNN_BLOCKS
CONV_W = 4
LRU_C = 8.0

MLA_HEADS = 4
MLA_Q_RANK = 192
MLA_KV_RANK = 128
MLA_NOPE = 64
MLA_ROPE = 32
MLA_V = 64
ROPE_THETA = 10000.0

X_HEADS = 4
X_HEAD_DIM = 128

N_GROUPS = 4
EXPERTS_PER_GROUP = 8
N_EXPERTS = N_GROUPS * EXPERTS_PER_GROUP
TOPK_IN_GROUP = 2
D_EXPERT = 512
EXPERT_CHUNK = 256

DN_ALPHA = (2.0 * DEPTH) ** 0.25
DN_BETA = (8.0 * DEPTH) ** -0.25
LN_EPS = 1e-5
RMS_EPS = 1e-6

IN_SPLITS = ((NSA_HEADS * HEAD_DIM,) + (NSA_KV_HEADS * HEAD_DIM,) * 6 + (NSA_HEADS * 3,)
             + (SB_HEADS * HEAD_DIM,) * 3
             + (RNN_W, RNN_W)
             + (MLA_Q_RANK, MLA_KV_RANK, MLA_ROPE)
             + (N_MIXERS * D_MODEL,))
IN_OFFSETS = tuple(int(o) for o in np.concatenate([[0], np.cumsum(IN_SPLITS)[:-1]]))
N_IN = sum(IN_SPLITS)

kernel_name = 'hybrid_nsa_sb_rglru_mla_hmoe_deepnorm'


def layer_norm(x, g, b):
    xf = x.astype(jnp.float32)
    mu = jnp.mean(xf, -1, keepdims=True)
    var = jnp.mean(jnp.square(xf - mu), -1, keepdims=True)
    return ((xf - mu) * lax.rsqrt(var + LN_EPS) * g + b).astype(x.dtype)


def rms_norm(x, g):
    xf = x.astype(jnp.float32)
    return (xf * lax.rsqrt(jnp.mean(jnp.square(xf), -1, keepdims=True) + RMS_EPS) * g).astype(x.dtype)


def masked_softmax(s, mask):
    s = jnp.where(mask, s.astype(jnp.float32), -jnp.inf)
    m = jnp.max(s, -1, keepdims=True)
    p = jnp.exp(s - jnp.where(jnp.isfinite(m), m, 0.0))
    return p / jnp.maximum(jnp.sum(p, -1, keepdims=True), 1e-30)


def alibi_slopes(n):
    return 2.0 ** (-8.0 * jnp.arange(1, n + 1, dtype=jnp.float32) / n)


def rope(x, pos):
    d = x.shape[-1]
    inv = ROPE_THETA ** (-jnp.arange(0, d, 2, dtype=jnp.float32) / d)
    ang = pos.astype(jnp.float32)[:, None] * inv[None, :]
    cos, sin = jnp.cos(ang)[:, None, :], jnp.sin(ang)[:, None, :]
    xf = x.astype(jnp.float32)
    x1, x2 = xf[..., : d // 2], xf[..., d // 2:]
    return jnp.concatenate([x1 * cos - x2 * sin, x1 * sin + x2 * cos], -1).astype(x.dtype)


def q_blocks(t):
    b, s = t.shape[:2]
    return jnp.moveaxis(t.reshape(b, s // Q_BLOCK, Q_BLOCK, *t.shape[2:]), 1, 0)


def unblock(t):
    nq, b, q = t.shape[:3]
    return jnp.moveaxis(t, 0, 1).reshape(b, nq * q, *t.shape[3:])


def nsa_attention(q, k_cmp, v_cmp, k_sel, v_sel, k_win, v_win, gate_logits, cmp_pos, cmp_w1, cmp_w2):
    B, S, H, dh = q.shape
    G = k_cmp.shape[2]
    f32 = jnp.float32
    scale = dh ** -0.5
    slopes = alibi_slopes(H).reshape(G, NSA_GROUP)
    t_pos = jnp.arange(S)
    nq = S // Q_BLOCK
    qpos = t_pos.reshape(nq, Q_BLOCK)
    qg = q.reshape(B, S, G, NSA_GROUP, dh)

    n_cmp = (S - CMP_LEN) // CMP_STRIDE + 1
    blk_idx = jnp.asarray(np.arange(n_cmp)[:, None] * CMP_STRIDE + np.arange(CMP_LEN)[None, :], jnp.int32)

    def compress(t, j):
        tb = t[:, blk_idx] + cmp_pos[j][None, None, :, None, :]
        tb = jnp.moveaxis(tb, 3, 2).reshape(B, n_cmp, G, CMP_LEN * dh)
        hid = jax.nn.gelu(jnp.einsum('bcgf,fh->bcgh', tb, cmp_w1[j]))
        return jnp.einsum('bcgh,hd->bcgd', hid, cmp_w2[j])

    kc, vc = compress(k_cmp, 0), compress(v_cmp, 1)
    blk_end = jnp.arange(n_cmp) * CMP_STRIDE + CMP_LEN - 1
    dist_c = (t_pos[:, None] - blk_end[None, :]).astype(f32)
    s_c = jnp.einsum('bsgnd,bcgd->bgnsc', qg, kc).astype(f32) * scale - slopes[None, :, :, None, None] * dist_c
    p_c = masked_softmax(s_c, dist_c >= 0)
    o_c = jnp.einsum('bgnsc,bcgd->bsgnd', p_c.astype(vc.dtype), vc)

    n_sel = S // SEL_BLOCK
    c0 = np.arange(n_cmp)[:, None] * CMP_STRIDE
    j0 = np.arange(n_sel)[None, :] * SEL_BLOCK
    cover = np.clip(np.minimum(c0 + CMP_LEN, j0 + SEL_BLOCK) - np.maximum(c0, j0), 0, None) / CMP_LEN
    imp = jnp.einsum('bgnsc,cj->bgsj', p_c, jnp.asarray(cover, f32))
    blk = jnp.arange(n_sel)[None, :]
    forced = (blk == 0) | (blk == (t_pos // SEL_BLOCK)[:, None])
    valid = blk * SEL_BLOCK <= t_pos[:, None]
    imp = jnp.where(forced, FORCE_SCORE, jnp.where(valid, imp, -1.0))
    n_top = min(SEL_TOPK, n_sel)
    _, sel_idx = lax.top_k(imp, n_top)

    kbt = jnp.moveaxis(k_sel.reshape(B, n_sel, SEL_BLOCK, G, dh), 3, 1)
    vbt = jnp.moveaxis(v_sel.reshape(B, n_sel, SEL_BLOCK, G, dh), 3, 1)
    bi = jnp.arange(B)[:, None, None, None]
    gi = jnp.arange(G)[None, :, None, None]
    n_keys = n_top * SEL_BLOCK

    def selected_block(args):
        qb, idx, qp = args
        kb = kbt[bi, gi, idx].reshape(B, G, Q_BLOCK, n_keys, dh)
        vb = vbt[bi, gi, idx].reshape(B, G, Q_BLOCK, n_keys, dh)
        kpos = (idx[..., None] * SEL_BLOCK + jnp.arange(SEL_BLOCK)).reshape(B, G, Q_BLOCK, n_keys)
        dist = (qp[None, None, :, None] - kpos).astype(f32)[:, :, None]
        s = jnp.einsum('bqgnd,bgqkd->bgnqk', qb, kb).astype(f32) * scale - slopes[None, :, :, None, None] * dist
        p = masked_softmax(s, dist >= 0)
        return jnp.einsum('bgnqk,bgqkd->bqgnd', p.astype(vb.dtype), vb)

    sel_blocks = jnp.moveaxis(sel_idx.reshape(B, G, nq, Q_BLOCK, n_top), 2, 0)
    o_s = unblock(lax.map(selected_block, (q_blocks(qg), sel_blocks, qpos)))

    band = Q_BLOCK + WINDOW
    band_np = np.arange(nq)[:, None] * Q_BLOCK + np.arange(band)[None, :]
    band_idx = jnp.asarray(band_np, jnp.int32)
    kpos_w = jnp.asarray(band_np - WINDOW, jnp.int32)
    pad = ((0, 0), (WINDOW, 0), (0, 0), (0, 0))
    kw = jnp.pad(k_win, pad)[:, band_idx]
    vw = jnp.pad(v_win, pad)[:, band_idx]
    dist_w = qpos[:, :, None] - kpos_w[:, None, :]
    mask_w = (dist_w >= 0) & (dist_w < WINDOW) & (kpos_w[:, None, :] >= 0)
    qw = qg.reshape(B, nq, Q_BLOCK, G, NSA_GROUP, dh)
    s_w = (jnp.einsum('bqtgnd,bqkgd->bgnqtk', qw, kw).astype(f32) * scale
           - slopes[None, :, :, None, None, None] * dist_w.astype(f32))
    p_w = masked_softmax(s_w, mask_w)
    o_w = jnp.einsum('bgnqtk,bqkgd->bqtgnd', p_w.astype(vw.dtype), vw).reshape(B, S, G, NSA_GROUP, dh)

    g = jax.nn.sigmoid(gate_logits.astype(f32)).reshape(B, S, G, NSA_GROUP, 3).astype(q.dtype)
    o = g[..., 0:1] * o_c + g[..., 1:2] * o_s + g[..., 2:3] * o_w
    return o.reshape(B, S, H * dh)


def stick_breaking_attention(q, k, v):
    B, S, H, dh = q.shape
    scale = dh ** -0.5
    key_pos = jnp.arange(S)

    def block(args):
        qb, qp = args
        z = jnp.einsum('bqhd,bkhd->bhqk', qb, k).astype(jnp.float32) * scale
        strict = key_pos[None, :] < qp[:, None]
        log_1m = jnp.where(strict, jax.nn.log_sigmoid(-z), 0.0)
        tail = lax.cumsum(log_1m, axis=3, reverse=True) - log_1m
        a = jnp.where(strict, jnp.exp(jax.nn.log_sigmoid(z) + tail), 0.0)
        return jnp.einsum('bhqk,bkhd->bqhd', a.astype(v.dtype), v)

    o = unblock(lax.map(block, (q_blocks(q), key_pos.reshape(-1, Q_BLOCK))))
    return o.reshape(B, S, H * dh)


def rglru_block(xr, xg, conv_w, conv_b, ga_w, ga_b, gx_w, gx_b, lru_lambda):
    B, S, W = xr.shape
    f32 = jnp.float32
    xp = jnp.pad(xr, ((0, 0), (CONV_W - 1, 0), (0, 0)))
    u = conv_b + xp[:, 0:S] * conv_w[0]
    for tap in range(1, CONV_W):
        u = u + xp[:, tap:tap + S] * conv_w[tap]
    ub = u.reshape(B, S, RNN_BLOCKS, RNN_BW)
    r = jax.nn.sigmoid(jnp.einsum('bsnc,ncd->bsnd', ub, ga_w).reshape(B, S, W) + ga_b)
    i = jax.nn.sigmoid(jnp.einsum('bsnc,ncd->bsnd', ub, gx_w).reshape(B, S, W) + gx_b)
    log_a = -LRU_C * r.astype(f32) * jax.nn.softplus(-lru_lambda.astype(f32))
    a = jnp.exp(log_a)
    b_in = jnp.sqrt(-jnp.expm1(2.0 * log_a)) * (i * u).astype(f32)

    def combine(left, right):
        a1, b1 = left
        a2, b2 = right
        return a1 * a2, a2 * b1 + b2

    _, h = lax.associative_scan(combine, (a, b_in), axis=1)
    return h.astype(xr.dtype) * jax.nn.gelu(xg)


def mla_attention(c_q_raw, c_kv_raw, k_rope_raw, q_norm, kv_norm, w_uq, w_ukv, pos):
    B, S, _ = c_q_raw.shape
    q = jnp.einsum('bsr,rf->bsf', rms_norm(c_q_raw, q_norm), w_uq).reshape(B, S, MLA_HEADS, MLA_NOPE + MLA_ROPE)
    q_nope, q_rope = q[..., :MLA_NOPE], rope(q[..., MLA_NOPE:], pos)
    kv = jnp.einsum('bsr,rf->bsf', rms_norm(c_kv_raw, kv_norm), w_ukv).reshape(B, S, MLA_HEADS, MLA_NOPE + MLA_V)
    k_nope, v = kv[..., :MLA_NOPE], kv[..., MLA_NOPE:]
    k_rope = rope(k_rope_raw[:, :, None, :], pos)[:, :, 0]
    scale = (MLA_NOPE + MLA_ROPE) ** -0.5
    key_pos = jnp.arange(S)

    def block(args):
        qn, qr, qp = args
        s = (jnp.einsum('bqhd,bkhd->bhqk', qn, k_nope) + jnp.einsum('bqhd,bkd->bhqk', qr, k_rope)) * scale
        p = masked_softmax(s, key_pos[None, :] <= qp[:, None])
        return jnp.einsum('bhqk,bkhd->bqhd', p.astype(v.dtype), v)

    o = unblock(lax.map(block, (q_blocks(q_nope), q_blocks(q_rope), key_pos.reshape(-1, Q_BLOCK))))
    return o.reshape(B, S, MLA_HEADS * MLA_V)


def hybrid_mixer(x, pos, w_in, cmp_pos, cmp_w1, cmp_w2, conv_w, conv_b, ga_w, ga_b, gx_w, gx_b, lru_lambda,
                 q_norm, kv_norm, w_uq, w_ukv, w_branch, w_out):
    B, S, D = x.shape
    h = jnp.einsum('bsd,df->bsf', x, w_in)
    (a_q, a_kc, a_vc, a_ks, a_vs, a_kw, a_vw, a_g, b_q, b_k, b_v, c_x, c_g, d_cq, d_ckv, d_kr,
     merge_g) = [h[..., o:o + w] for o, w in zip(IN_OFFSETS, IN_SPLITS)]

    def heads(t):
        return t.reshape(B, S, -1, HEAD_DIM)

    o_a = nsa_attention(heads(a_q), heads(a_kc), heads(a_vc), heads(a_ks), heads(a_vs), heads(a_kw), heads(a_vw),
                        a_g.reshape(B, S, NSA_HEADS, 3), cmp_pos, cmp_w1, cmp_w2)
    o_b = stick_breaking_attention(heads(b_q), heads(b_k), heads(b_v))
    o_c = rglru_block(c_x, c_g, conv_w, conv_b, ga_w, ga_b, gx_w, gx_b, lru_lambda)
    o_d = mla_attention(d_cq, d_ckv, d_kr, q_norm, kv_norm, w_uq, w_ukv, pos)
    branches = jnp.stack([o_a, o_b, o_c, o_d], axis=2)
    up = jnp.einsum('bsnc,ncd->bsnd', branches, w_branch)
    gates = jax.nn.sigmoid(merge_g.reshape(B, S, N_MIXERS, D))
    return jnp.einsum('bsd,de->bse', jnp.sum(gates * up, axis=2), w_out)


def memory_cross_attention(x, mem, wq, wkv, wo):
    B, S, _ = x.shape
    M = mem.shape[1]
    F = X_HEADS * X_HEAD_DIM
    q = jnp.einsum('bsd,df->bsf', x, wq).reshape(B, S, X_HEADS, X_HEAD_DIM)
    kv = jnp.einsum('bmd,df->bmf', mem, wkv)
    k = kv[..., :F].reshape(B, M, X_HEADS, X_HEAD_DIM)
    v = kv[..., F:].reshape(B, M, X_HEADS, X_HEAD_DIM)
    s = jnp.einsum('bshd,bmhd->bhsm', q, k).astype(jnp.float32) * X_HEAD_DIM ** -0.5
    p = jax.nn.softmax(s, axis=-1).astype(v.dtype)
    o = jnp.einsum('bhsm,bmhd->bshd', p, v).reshape(B, S, F)
    return jnp.einsum('bsf,fd->bsd', o, wo)


def grouped_expert_ffn(xt, expert_idx, weights, w_gu, w_down):
    N, D = xt.shape
    K = expert_idx.shape[1]
    E = w_gu.shape[0]
    F = w_down.shape[1]
    C = EXPERT_CHUNK
    A = N * K
    flat_e = expert_idx.reshape(A)
    flat_tok = jnp.arange(A, dtype=jnp.int32) // K
    order = jnp.argsort(flat_e)
    se, st = flat_e[order], flat_tok[order]
    counts = jax.ops.segment_sum(jnp.ones((A,), jnp.int32), flat_e, num_segments=E)
    padded = (counts + C - 1) // C * C
    pad_end = jnp.cumsum(padded)
    pad_start = pad_end - padded
    start = jnp.cumsum(counts) - counts
    dest = pad_start[se] + jnp.arange(A, dtype=jnp.int32) - start[se]
    n_chunks = -(-(A + E * (C - 1)) // C)
    P = n_chunks * C
    slot_tok = jnp.full((P,), N, jnp.int32).at[dest].set(st)
    x_pad = jnp.concatenate([xt, jnp.zeros((1, D), xt.dtype)], axis=0)
    xb = x_pad[slot_tok].reshape(n_chunks, C, D)
    chunk_start = jnp.arange(n_chunks, dtype=jnp.int32) * C
    chunk_e = jnp.minimum(jnp.sum((pad_end[None, :] <= chunk_start[:, None]).astype(jnp.int32), axis=1), E - 1)

    def run(args):
        xc, e = args
        gu = xc @ w_gu[e]
        return (jax.nn.silu(gu[:, :F]) * gu[:, F:]) @ w_down[e]

    yb = lax.map(run, (xb, chunk_e)).reshape(P, D)
    contrib = yb[dest] * weights.reshape(A)[order][:, None].astype(yb.dtype)
    return jnp.zeros((N, D), yb.dtype).at[st].add(contrib)


def hier_moe(x, rg_w, rg_b, re_w, re_b, w_gu, w_down):
    B, S, D = x.shape
    xt = x.reshape(B * S, D)
    N = xt.shape[0]
    rows = jnp.arange(N)
    p_grp = jax.nn.softmax((xt @ rg_w).astype(jnp.float32) + rg_b, axis=-1)
    grp = jnp.argmax(p_grp, axis=-1).astype(jnp.int32)
    p_g = p_grp[rows, grp]
    e_logits = ((xt @ re_w).astype(jnp.float32) + re_b).reshape(N, N_GROUPS, EXPERTS_PER_GROUP)
    p_e = jax.nn.softmax(e_logits[rows, grp], axis=-1)
    top_p, top_i = lax.top_k(p_e, TOPK_IN_GROUP)
    w = p_g[:, None] * top_p / jnp.sum(top_p, -1, keepdims=True)
    expert_idx = (grp[:, None] * EXPERTS_PER_GROUP + top_i).astype(jnp.int32)
    return grouped_expert_ffn(xt, expert_idx, w, w_gu, w_down).reshape(B, S, D)


def setup_inputs(seed: int = 0) -> dict:
    key = jax.random.key(seed)
    keys = jax.random.split(key, 48)
    ks = iter([keys[i] for i in range(48)])
    f32 = jnp.float32
    L = DEPTH

    def nrm(shape, scale):
        return jax.random.normal(next(ks), shape, f32) * scale

    def gain(n):
        return 1.0 + nrm((L, n), 0.02)

    u = jax.random.uniform(next(ks), (L, RNN_W), f32, 0.9, 0.999)
    a0 = u ** (1.0 / LRU_C)
    return {
        'x': nrm((BATCH, SEQ, D_MODEL), 1.0),
        'mem': nrm((BATCH, MEM_LEN, D_MODEL), 1.0),
        'w_in': nrm((L, D_MODEL, N_IN), D_MODEL ** -0.5),
        'nsa_cmp_pos': nrm((L, 2, CMP_LEN, HEAD_DIM), 0.1),
        'nsa_cmp_w1': nrm((L, 2, CMP_LEN * HEAD_DIM, CMP_HID), (CMP_LEN * HEAD_DIM) ** -0.5),
        'nsa_cmp_w2': nrm((L, 2, CMP_HID, HEAD_DIM), CMP_HID ** -0.5),
        'rnn_conv_w': nrm((L, CONV_W, RNN_W), CONV_W ** -0.5),
        'rnn_conv_b': nrm((L, RNN_W), 0.01),
        'rnn_ga_w': nrm((L, RNN_BLOCKS, RNN_BW, RNN_BW), RNN_BW ** -0.5),
        'rnn_ga_b': nrm((L, RNN_W), 0.01),
        'rnn_gx_w': nrm((L, RNN_BLOCKS, RNN_BW, RNN_BW), RNN_BW ** -0.5),
        'rnn_gx_b': nrm((L, RNN_W), 0.01),
        'rnn_lambda': jnp.log(a0) - jnp.log1p(-a0),
        'mla_q_norm': gain(MLA_Q_RANK),
        'mla_kv_norm': gain(MLA_KV_RANK),
        'mla_w_uq': nrm((L, MLA_Q_RANK, MLA_HEADS * (MLA_NOPE + MLA_ROPE)), MLA_Q_RANK ** -0.5),
        'mla_w_ukv': nrm((L, MLA_KV_RANK, MLA_HEADS * (MLA_NOPE + MLA_V)), MLA_KV_RANK ** -0.5),
        'w_branch': nrm((L, N_MIXERS, MIX_W, D_MODEL), MIX_W ** -0.5),
        'w_out': nrm((L, D_MODEL, D_MODEL), DN_BETA * D_MODEL ** -0.5),
        'ln1_g': gain(D_MODEL),
        'ln1_b': nrm((L, D_MODEL), 0.02),
        'x_wq': nrm((L, D_MODEL, X_HEADS * X_HEAD_DIM), D_MODEL ** -0.5),
        'x_wkv': nrm((L, D_MODEL, 2 * X_HEADS * X_HEAD_DIM), D_MODEL ** -0.5),
        'x_wo': nrm((L, X_HEADS * X_HEAD_DIM, D_MODEL), DN_BETA * (X_HEADS * X_HEAD_DIM) ** -0.5),
        'ln2_g': gain(D_MODEL),
        'ln2_b': nrm((L, D_MODEL), 0.02),
        'moe_rg_w': nrm((L, D_MODEL, N_GROUPS), D_MODEL ** -0.5),
        'moe_rg_b': nrm((L, N_GROUPS), 0.01),
        'moe_re_w': nrm((L, D_MODEL, N_EXPERTS), D_MODEL ** -0.5),
        'moe_re_b': nrm((L, N_EXPERTS), 0.01),
        'moe_w_gu': nrm((L, N_EXPERTS, D_MODEL, 2 * D_EXPERT), D_MODEL ** -0.5),
        'moe_w_down': nrm((L, N_EXPERTS, D_EXPERT, D_MODEL), DN_BETA * D_EXPERT ** -0.5),
        'ln3_g': gain(D_MODEL),
        'ln3_b': nrm((L, D_MODEL), 0.02),
    }


def reference(x, mem, w_in, nsa_cmp_pos, nsa_cmp_w1, nsa_cmp_w2, rnn_conv_w, rnn_conv_b, rnn_ga_w, rnn_ga_b,
              rnn_gx_w, rnn_gx_b, rnn_lambda, mla_q_norm, mla_kv_norm, mla_w_uq, mla_w_ukv, w_branch, w_out,
              ln1_g, ln1_b, x_wq, x_wkv, x_wo, ln2_g, ln2_b, moe_rg_w, moe_rg_b, moe_re_w, moe_re_b,
              moe_w_gu, moe_w_down, ln3_g, ln3_b):
    pos = jnp.arange(x.shape[1])
    for l in range(DEPTH):
        y = hybrid_mixer(x, pos, w_in[l], nsa_cmp_pos[l], nsa_cmp_w1[l], nsa_cmp_w2[l], rnn_conv_w[l], rnn_conv_b[l],
                         rnn_ga_w[l], rnn_ga_b[l], rnn_gx_w[l], rnn_gx_b[l], rnn_lambda[l], mla_q_norm[l],
                         mla_kv_norm[l], mla_w_uq[l], mla_w_ukv[l], w_branch[l], w_out[l])
        x = layer_norm(DN_ALPHA * x + y, ln1_g[l], ln1_b[l])
        y = memory_cross_attention(x, mem, x_wq[l], x_wkv[l], x_wo[l])
        x = layer_norm(DN_ALPHA * x + y, ln2_g[l], ln2_b[l])
        y = hier_moe(x, moe_rg_w[l], moe_rg_b[l], moe_re_w[l], moe_re_b[l], moe_w_gu[l], moe_w_down[l])
        x = layer_norm(DN_ALPHA * x + y, ln3_g[l], ln3_b[l])
    return x
```

```python
import functools

import numpy as np
import jax
import jax.numpy as jnp
from jax import lax
from jax.experimental import pallas as pl
from jax.experimental.pallas import tpu as pltpu

D_MODEL = 1024
DEPTH = 4
HEAD_DIM = 64
N_MIXERS = 4
MIX_W = 256
Q_BLOCK = 128
NSA_HEADS = 4
NSA_KV_HEADS = 2
NSA_GROUP = NSA_HEADS // NSA_KV_HEADS
CMP_LEN = 32
CMP_STRIDE = 16
CMP_HID = 256
SEL_BLOCK = 64
SEL_TOPK = 8
WINDOW = 512
FORCE_SCORE = 1e4
SB_HEADS = 4
RNN_W = 256
RNN_BLOCKS = 4
RNN_BW = RNN_W // RNN_BLOCKS
CONV_W = 4
LRU_C = 8.0
MLA_HEADS = 4
MLA_Q_RANK = 192
MLA_KV_RANK = 128
MLA_NOPE = 64
MLA_ROPE = 32
MLA_V = 64
ROPE_THETA = 10000.0
X_HEADS = 4
X_HEAD_DIM = 128
N_GROUPS = 4
EXPERTS_PER_GROUP = 8
N_EXPERTS = N_GROUPS * EXPERTS_PER_GROUP
TOPK_IN_GROUP = 2
D_EXPERT = 512
EXPERT_CHUNK = 256
DN_ALPHA = (2.0 * DEPTH) ** 0.25
LN_EPS = 1e-5
RMS_EPS = 1e-6

IN_SPLITS = ((NSA_HEADS * HEAD_DIM,) + (NSA_KV_HEADS * HEAD_DIM,) * 6 + (NSA_HEADS * 3,)
             + (SB_HEADS * HEAD_DIM,) * 3
             + (RNN_W, RNN_W)
             + (MLA_Q_RANK, MLA_KV_RANK, MLA_ROPE)
             + (N_MIXERS * D_MODEL,))
IN_OFFSETS = tuple(int(o) for o in np.concatenate([[0], np.cumsum(IN_SPLITS)[:-1]]))
N_IN = sum(IN_SPLITS)

LANE = 128
VMEM_LIMIT = 48 * 1024 * 1024

f32 = jnp.float32
bf16 = jnp.bfloat16


def _mm_kernel(a_ref, b_ref, o_ref):
    o_ref[...] = jnp.dot(a_ref[...].astype(bf16), b_ref[...], preferred_element_type=f32).astype(o_ref.dtype)


def matmul(a, b, tm, tn, out_dtype=f32):
    M, K = a.shape
    _, N = b.shape
    assert M % tm == 0 and N % tn == 0
    return pl.pallas_call(
        _mm_kernel,
        grid=(N // tn, M // tm),
        in_specs=[pl.BlockSpec((tm, K), lambda j, i: (i, 0)),
                  pl.BlockSpec((K, tn), lambda j, i: (0, j))],
        out_specs=pl.BlockSpec((tm, tn), lambda j, i: (i, j)),
        out_shape=jax.ShapeDtypeStruct((M, N), out_dtype),
        compiler_params=pltpu.CompilerParams(dimension_semantics=("arbitrary", "arbitrary"),
                                             vmem_limit_bytes=VMEM_LIMIT),
        name="matmul",
    )(a, b)


def layer_norm(x, g, b):
    mu = jnp.mean(x, -1, keepdims=True)
    var = jnp.mean(jnp.square(x - mu), -1, keepdims=True)
    return (x - mu) * lax.rsqrt(var + LN_EPS) * g + b


def rms_norm(x, g):
    return x * lax.rsqrt(jnp.mean(jnp.square(x), -1, keepdims=True) + RMS_EPS) * g


def masked_softmax(s, mask):
    s = jnp.where(mask, s.astype(f32), -jnp.inf)
    m = jnp.max(s, -1, keepdims=True)
    p = jnp.exp(s - jnp.where(jnp.isfinite(m), m, 0.0))
    return p / jnp.maximum(jnp.sum(p, -1, keepdims=True), 1e-30)


def alibi_slopes(n):
    return 2.0 ** (-8.0 * jnp.arange(1, n + 1, dtype=f32) / n)


def rope(x, pos):
    d = x.shape[-1]
    inv = ROPE_THETA ** (-jnp.arange(0, d, 2, dtype=f32) / d)
    ang = pos.astype(f32)[:, None] * inv[None, :]
    cos, sin = jnp.cos(ang)[:, None, :], jnp.sin(ang)[:, None, :]
    x1, x2 = x[..., : d // 2], x[..., d // 2:]
    return jnp.concatenate([x1 * cos - x2 * sin, x1 * sin + x2 * cos], -1)


def q_blocks(t):
    b, s = t.shape[:2]
    return jnp.moveaxis(t.reshape(b, s // Q_BLOCK, Q_BLOCK, *t.shape[2:]), 1, 0)


def unblock(t):
    nq, b, q = t.shape[:3]
    return jnp.moveaxis(t, 0, 1).reshape(b, nq * q, *t.shape[3:])


def nsa_attention(q, k_cmp, v_cmp, k_sel, v_sel, k_win, v_win, gate_logits, cmp_pos, cmp_w1, cmp_w2):
    B, S, H, dh = q.shape
    G = k_cmp.shape[2]
    scale = dh ** -0.5
    slopes = alibi_slopes(H).reshape(G, NSA_GROUP)
    t_pos = jnp.arange(S)
    nq = S // Q_BLOCK
    qpos = t_pos.reshape(nq, Q_BLOCK)
    qg = q.reshape(B, S, G, NSA_GROUP, dh)
    n_cmp = (S - CMP_LEN) // CMP_STRIDE + 1
    blk_idx = jnp.asarray(np.arange(n_cmp)[:, None] * CMP_STRIDE + np.arange(CMP_LEN)[None, :], jnp.int32)

    def compress(t, j):
        tb = t[:, blk_idx] + cmp_pos[j][None, None, :, None, :]
        tb = jnp.moveaxis(tb, 3, 2).reshape(B, n_cmp, G, CMP_LEN * dh)
        hid = jax.nn.gelu(jnp.einsum('bcgf,fh->bcgh', tb, cmp_w1[j]))
        return jnp.einsum('bcgh,hd->bcgd', hid, cmp_w2[j])

    kc, vc = compress(k_cmp, 0), compress(v_cmp, 1)
    blk_end = jnp.arange(n_cmp) * CMP_STRIDE + CMP_LEN - 1
    dist_c = (t_pos[:, None] - blk_end[None, :]).astype(f32)
    s_c = jnp.einsum('bsgnd,bcgd->bgnsc', qg, kc).astype(f32) * scale - slopes[None, :, :, None, None] * dist_c
    p_c = masked_softmax(s_c, dist_c >= 0)
    o_c = jnp.einsum('bgnsc,bcgd->bsgnd', p_c, vc)

    n_sel = S // SEL_BLOCK
    c0 = np.arange(n_cmp)[:, None] * CMP_STRIDE
    j0 = np.arange(n_sel)[None, :] * SEL_BLOCK
    cover = np.clip(np.minimum(c0 + CMP_LEN, j0 + SEL_BLOCK) - np.maximum(c0, j0), 0, None) / CMP_LEN
    imp = jnp.einsum('bgnsc,cj->bgsj', p_c, jnp.asarray(cover, f32))
    blk = jnp.arange(n_sel)[None, :]
    forced = (blk == 0) | (blk == (t_pos // SEL_BLOCK)[:, None])
    valid = blk * SEL_BLOCK <= t_pos[:, None]
    imp = jnp.where(forced, FORCE_SCORE, jnp.where(valid, imp, -1.0))
    n_top = min(SEL_TOPK, n_sel)
    _, sel_idx = lax.top_k(imp, n_top)

    kbt = jnp.moveaxis(k_sel.reshape(B, n_sel, SEL_BLOCK, G, dh), 3, 1)
    vbt = jnp.moveaxis(v_sel.reshape(B, n_sel, SEL_BLOCK, G, dh), 3, 1)
    bi = jnp.arange(B)[:, None, None, None]
    gi = jnp.arange(G)[None, :, None, None]
    n_keys = n_top * SEL_BLOCK

    def selected_block(args):
        qb, idx, qp = args
        kb = kbt[bi, gi, idx].reshape(B, G, Q_BLOCK, n_keys, dh)
        vb = vbt[bi, gi, idx].reshape(B, G, Q_BLOCK, n_keys, dh)
        kpos = (idx[..., None] * SEL_BLOCK + jnp.arange(SEL_BLOCK)).reshape(B, G, Q_BLOCK, n_keys)
        dist = (qp[None, None, :, None] - kpos).astype(f32)[:, :, None]
        s = jnp.einsum('bqgnd,bgqkd->bgnqk', qb, kb).astype(f32) * scale - slopes[None, :, :, None, None] * dist
        p = masked_softmax(s, dist >= 0)
        return jnp.einsum('bgnqk,bgqkd->bqgnd', p, vb)

    sel_blocks = jnp.moveaxis(sel_idx.reshape(B, G, nq, Q_BLOCK, n_top), 2, 0)
    o_s = unblock(lax.map(selected_block, (q_blocks(qg), sel_blocks, qpos)))

    band = Q_BLOCK + WINDOW
    band_np = np.arange(nq)[:, None] * Q_BLOCK + np.arange(band)[None, :]
    band_idx = jnp.asarray(band_np, jnp.int32)
    kpos_w = jnp.asarray(band_np - WINDOW, jnp.int32)
    pad = ((0, 0), (WINDOW, 0), (0, 0), (0, 0))
    kw = jnp.pad(k_win, pad)[:, band_idx]
    vw = jnp.pad(v_win, pad)[:, band_idx]
    dist_w = qpos[:, :, None] - kpos_w[:, None, :]
    mask_w = (dist_w >= 0) & (dist_w < WINDOW) & (kpos_w[:, None, :] >= 0)
    qw = qg.reshape(B, nq, Q_BLOCK, G, NSA_GROUP, dh)
    s_w = (jnp.einsum('bqtgnd,bqkgd->bgnqtk', qw, kw).astype(f32) * scale
           - slopes[None, :, :, None, None, None] * dist_w.astype(f32))
    p_w = masked_softmax(s_w, mask_w)
    o_w = jnp.einsum('bgnqtk,bqkgd->bqtgnd', p_w, vw).reshape(B, S, G, NSA_GROUP, dh)

    g = jax.nn.sigmoid(gate_logits.astype(f32)).reshape(B, S, G, NSA_GROUP, 3)
    o = g[..., 0:1] * o_c + g[..., 1:2] * o_s + g[..., 2:3] * o_w
    return o.reshape(B, S, H * dh)


def stick_breaking_attention(q, k, v):
    B, S, H, dh = q.shape
    scale = dh ** -0.5
    key_pos = jnp.arange(S)

    def block(args):
        qb, qp = args
        z = jnp.einsum('bqhd,bkhd->bhqk', qb, k).astype(f32) * scale
        strict = key_pos[None, :] < qp[:, None]
        log_1m = jnp.where(strict, jax.nn.log_sigmoid(-z), 0.0)
        tail = lax.cumsum(log_1m, axis=3, reverse=True) - log_1m
        a = jnp.where(strict, jnp.exp(jax.nn.log_sigmoid(z) + tail), 0.0)
        return jnp.einsum('bhqk,bkhd->bqhd', a, v)

    o = unblock(lax.map(block, (q_blocks(q), key_pos.reshape(-1, Q_BLOCK))))
    return o.reshape(B, S, H * dh)


def rglru_block(xr, xg, conv_w, conv_b, ga_w, ga_b, gx_w, gx_b, lru_lambda):
    B, S, W = xr.shape
    xp = jnp.pad(xr, ((0, 0), (CONV_W - 1, 0), (0, 0)))
    u = conv_b + xp[:, 0:S] * conv_w[0]
    for tap in range(1, CONV_W):
        u = u + xp[:, tap:tap + S] * conv_w[tap]
    ub = u.reshape(B, S, RNN_BLOCKS, RNN_BW)
    r = jax.nn.sigmoid(jnp.einsum('bsnc,ncd->bsnd', ub, ga_w).reshape(B, S, W) + ga_b)
    i = jax.nn.sigmoid(jnp.einsum('bsnc,ncd->bsnd', ub, gx_w).reshape(B, S, W) + gx_b)
    log_a = -LRU_C * r * jax.nn.softplus(-lru_lambda)
    a = jnp.exp(log_a)
    b_in = jnp.sqrt(-jnp.expm1(2.0 * log_a)) * (i * u)

    def combine(left, right):
        a1, b1 = left
        a2, b2 = right
        return a1 * a2, a2 * b1 + b2

    _, h = lax.associative_scan(combine, (a, b_in), axis=1)
    return h * jax.nn.gelu(xg)


def mla_attention(c_q_raw, c_kv_raw, k_rope_raw, q_norm, kv_norm, w_uq, w_ukv, pos):
    B, S, _ = c_q_raw.shape
    q = jnp.einsum('bsr,rf->bsf', rms_norm(c_q_raw, q_norm), w_uq).reshape(B, S, MLA_HEADS, MLA_NOPE + MLA_ROPE)
    q_nope, q_rope = q[..., :MLA_NOPE], rope(q[..., MLA_NOPE:], pos)
    kv = jnp.einsum('bsr,rf->bsf', rms_norm(c_kv_raw, kv_norm), w_ukv).reshape(B, S, MLA_HEADS, MLA_NOPE + MLA_V)
    k_nope, v = kv[..., :MLA_NOPE], kv[..., MLA_NOPE:]
    k_rope = rope(k_rope_raw[:, :, None, :], pos)[:, :, 0]
    scale = (MLA_NOPE + MLA_ROPE) ** -0.5
    key_pos = jnp.arange(S)

    def block(args):
        qn, qr, qp = args
        s = (jnp.einsum('bqhd,bkhd->bhqk', qn, k_nope) + jnp.einsum('bqhd,bkd->bhqk', qr, k_rope)) * scale
        p = masked_softmax(s, key_pos[None, :] <= qp[:, None])
        return jnp.einsum('bhqk,bkhd->bqhd', p, v)

    o = unblock(lax.map(block, (q_blocks(q_nope), q_blocks(q_rope), key_pos.reshape(-1, Q_BLOCK))))
    return o.reshape(B, S, MLA_HEADS * MLA_V)


def hybrid_mixer(x, pos, w_in, cmp_pos, cmp_w1, cmp_w2, conv_w, conv_b, ga_w, ga_b, gx_w, gx_b, lru_lambda,
                 q_norm, kv_norm, w_uq, w_ukv, w_branch, w_out):
    B, S, D = x.shape
    n_pad = -N_IN % (2 * LANE)
    w_pad = jnp.pad(w_in, ((0, 0), (0, n_pad))).astype(bf16)
    h = matmul(x.reshape(B * S, D), w_pad, 512, (N_IN + n_pad) // 2).reshape(B, S, N_IN + n_pad)
    (a_q, a_kc, a_vc, a_ks, a_vs, a_kw, a_vw, a_g, b_q, b_k, b_v, c_x, c_g, d_cq, d_ckv, d_kr,
     merge_g) = [h[..., o:o + w] for o, w in zip(IN_OFFSETS, IN_SPLITS)]

    def heads(t):
        return t.reshape(B, S, -1, HEAD_DIM)

    o_a = nsa_attention(heads(a_q), heads(a_kc), heads(a_vc), heads(a_ks), heads(a_vs), heads(a_kw), heads(a_vw),
                        a_g.reshape(B, S, NSA_HEADS, 3), cmp_pos, cmp_w1, cmp_w2)
    o_b = stick_breaking_attention(heads(b_q), heads(b_k), heads(b_v))
    o_c = rglru_block(c_x, c_g, conv_w, conv_b, ga_w, ga_b, gx_w, gx_b, lru_lambda)
    o_d = mla_attention(d_cq, d_ckv, d_kr, q_norm, kv_norm, w_uq, w_ukv, pos)
    branches = jnp.stack([o_a, o_b, o_c, o_d], axis=2)
    up = jnp.einsum('bsnc,ncd->bsnd', branches, w_branch)
    gates = jax.nn.sigmoid(merge_g.reshape(B, S, N_MIXERS, D))
    return jnp.einsum('bsd,de->bse', jnp.sum(gates * up, axis=2), w_out)


def memory_cross_attention(x, mem, wq, wkv, wo):
    B, S, _ = x.shape
    M = mem.shape[1]
    F = X_HEADS * X_HEAD_DIM
    q = jnp.einsum('bsd,df->bsf', x, wq).reshape(B, S, X_HEADS, X_HEAD_DIM)
    kv = jnp.einsum('bmd,df->bmf', mem, wkv)
    k = kv[..., :F].reshape(B, M, X_HEADS, X_HEAD_DIM)
    v = kv[..., F:].reshape(B, M, X_HEADS, X_HEAD_DIM)
    s = jnp.einsum('bshd,bmhd->bhsm', q, k).astype(f32) * X_HEAD_DIM ** -0.5
    p = jax.nn.softmax(s, axis=-1)
    o = jnp.einsum('bhsm,bmhd->bshd', p, v).reshape(B, S, F)
    return jnp.einsum('bsf,fd->bsd', o, wo)


def grouped_expert_ffn(xt, expert_idx, weights, w_gu, w_down):
    N, D = xt.shape
    K = expert_idx.shape[1]
    E = w_gu.shape[0]
    F = w_down.shape[1]
    C = EXPERT_CHUNK
    A = N * K
    flat_e = expert_idx.reshape(A)
    flat_tok = jnp.arange(A, dtype=jnp.int32) // K
    order = jnp.argsort(flat_e)
    se, st = flat_e[order], flat_tok[order]
    counts = jax.ops.segment_sum(jnp.ones((A,), jnp.int32), flat_e, num_segments=E)
    padded = (counts + C - 1) // C * C
    pad_end = jnp.cumsum(padded)
    pad_start = pad_end - padded
    start = jnp.cumsum(counts) - counts
    dest = pad_start[se] + jnp.arange(A, dtype=jnp.int32) - start[se]
    n_chunks = -(-(A + E * (C - 1)) // C)
    P = n_chunks * C
    slot_tok = jnp.full((P,), N, jnp.int32).at[dest].set(st)
    x_pad = jnp.concatenate([xt, jnp.zeros((1, D), xt.dtype)], axis=0)
    xb = x_pad[slot_tok].reshape(n_chunks, C, D)
    chunk_start = jnp.arange(n_chunks, dtype=jnp.int32) * C
    chunk_e = jnp.minimum(jnp.sum((pad_end[None, :] <= chunk_start[:, None]).astype(jnp.int32), axis=1), E - 1)

    def run(args):
        xc, e = args
        gu = xc @ w_gu[e]
        return (jax.nn.silu(gu[:, :F]) * gu[:, F:]) @ w_down[e]

    yb = lax.map(run, (xb, chunk_e)).reshape(P, D)
    contrib = yb[dest] * weights.reshape(A)[order][:, None]
    return jnp.zeros((N, D), yb.dtype).at[st].add(contrib)


def hier_moe(x, rg_w, rg_b, re_w, re_b, w_gu, w_down):
    B, S, D = x.shape
    xt = x.reshape(B * S, D)
    N = xt.shape[0]
    rows = jnp.arange(N)
    p_grp = jax.nn.softmax((xt @ rg_w).astype(f32) + rg_b, axis=-1)
    grp = jnp.argmax(p_grp, axis=-1).astype(jnp.int32)
    p_g = p_grp[rows, grp]
    e_logits = ((xt @ re_w).astype(f32) + re_b).reshape(N, N_GROUPS, EXPERTS_PER_GROUP)
    p_e = jax.nn.softmax(e_logits[rows, grp], axis=-1)
    top_p, top_i = lax.top_k(p_e, TOPK_IN_GROUP)
    w = p_g[:, None] * top_p / jnp.sum(top_p, -1, keepdims=True)
    expert_idx = (grp[:, None] * EXPERTS_PER_GROUP + top_i).astype(jnp.int32)
    return grouped_expert_ffn(xt, expert_idx, w, w_gu, w_down).reshape(B, S, D)


def kernel(x, mem, w_in, nsa_cmp_pos, nsa_cmp_w1, nsa_cmp_w2, rnn_conv_w, rnn_conv_b, rnn_ga_w, rnn_ga_b,
           rnn_gx_w, rnn_gx_b, rnn_lambda, mla_q_norm, mla_kv_norm, mla_w_uq, mla_w_ukv, w_branch, w_out,
           ln1_g, ln1_b, x_wq, x_wkv, x_wo, ln2_g, ln2_b, moe_rg_w, moe_rg_b, moe_re_w, moe_re_b,
           moe_w_gu, moe_w_down, ln3_g, ln3_b):
    pos = jnp.arange(x.shape[1])
    for l in range(DEPTH):
        y = hybrid_mixer(x, pos, w_in[l], nsa_cmp_pos[l], nsa_cmp_w1[l], nsa_cmp_w2[l], rnn_conv_w[l], rnn_conv_b[l],
                         rnn_ga_w[l], rnn_ga_b[l], rnn_gx_w[l], rnn_gx_b[l], rnn_lambda[l], mla_q_norm[l],
                         mla_kv_norm[l], mla_w_uq[l], mla_w_ukv[l], w_branch[l], w_out[l])
        x = layer_norm(DN_ALPHA * x + y, ln1_g[l], ln1_b[l])
        y = memory_cross_attention(x, mem, x_wq[l], x_wkv[l], x_wo[l])
        x = layer_norm(DN_ALPHA * x + y, ln2_g[l], ln2_b[l])
        y = hier_moe(x, moe_rg_w[l], moe_rg_b[l], moe_re_w[l], moe_re_b[l], moe_w_gu[l], moe_w_down[l])
        x = layer_norm(DN_ALPHA * x + y, ln3_g[l], ln3_b[l])
    return x
```

```python
import functools

import numpy as np
import jax
import jax.numpy as jnp
from jax import lax
from jax.experimental import pallas as pl
from jax.experimental.pallas import tpu as pltpu

D_MODEL = 1024
DEPTH = 4
HEAD_DIM = 64
N_MIXERS = 4
MIX_W = 256
Q_BLOCK = 128
NSA_HEADS = 4
NSA_KV_HEADS = 2
NSA_GROUP = NSA_HEADS // NSA_KV_HEADS
CMP_LEN = 32
CMP_STRIDE = 16
CMP_HID = 256
SEL_BLOCK = 64
SEL_TOPK = 8
WINDOW = 512
FORCE_SCORE = 1e4
SB_HEADS = 4
RNN_W = 256
RNN_BLOCKS = 4
RNN_BW = RNN_W // RNN_BLOCKS
CONV_W = 4
LRU_C = 8.0
MLA_HEADS = 4
MLA_Q_RANK = 192
MLA_KV_RANK = 128
MLA_NOPE = 64
MLA_ROPE = 32
MLA_V = 64
ROPE_THETA = 10000.0
X_HEADS = 4
X_HEAD_DIM = 128
N_GROUPS = 4
EXPERTS_PER_GROUP = 8
N_EXPERTS = N_GROUPS * EXPERTS_PER_GROUP
TOPK_IN_GROUP = 2
D_EXPERT = 512
EXPERT_CHUNK = 256
DN_ALPHA = (2.0 * DEPTH) ** 0.25
LN_EPS = 1e-5
RMS_EPS = 1e-6

IN_SPLITS = ((NSA_HEADS * HEAD_DIM,) + (NSA_KV_HEADS * HEAD_DIM,) * 6 + (NSA_HEADS * 3,)
             + (SB_HEADS * HEAD_DIM,) * 3
             + (RNN_W, RNN_W)
             + (MLA_Q_RANK, MLA_KV_RANK, MLA_ROPE)
             + (N_MIXERS * D_MODEL,))
IN_OFFSETS = tuple(int(o) for o in np.concatenate([[0], np.cumsum(IN_SPLITS)[:-1]]))
N_IN = sum(IN_SPLITS)

LANE = 128
VMEM_LIMIT = 48 * 1024 * 1024
NEG = -1e30

f32 = jnp.float32
MXU = jnp.bfloat16

_SECTIONS = (("mg", 4096), ("aq", 256), ("ks", 256), ("vs", 256), ("kw", 256), ("vw", 256), ("bq", 256),
             ("bk", 256), ("bv", 256), ("cx", 256), ("cg", 256), ("dcq", 256), ("kc", 128), ("vc", 128),
             ("ag", 128), ("dckv", 128), ("dkr", 128), ("dkrs", 128))
_IN_TILE = 3840
OFF = {}
_o = 0
for _n, _w in _SECTIONS:
    OFF[_n] = _o
    _o += _w
N_CAT = -(-_o // _IN_TILE) * _IN_TILE


def _cat_w_in(w):
    def sec(i):
        return w[:, IN_OFFSETS[i]:IN_OFFSETS[i] + IN_SPLITS[i]]

    def dup(t):
        g0, g1 = t[:, :HEAD_DIM], t[:, HEAD_DIM:]
        return jnp.concatenate([g0, g0, g1, g1], axis=1)

    def padc(t, n):
        return jnp.pad(t, ((0, 0), (0, n - t.shape[1])))

    def rope_slot(t):
        return jnp.pad(t, ((0, 0), (MLA_NOPE, LANE - MLA_NOPE - MLA_ROPE)))

    parts = {"mg": sec(16), "aq": sec(0), "kc": sec(1), "vc": sec(2), "ks": dup(sec(3)), "vs": dup(sec(4)),
             "kw": dup(sec(5)), "vw": dup(sec(6)), "ag": padc(sec(7), 128), "bq": sec(8), "bk": sec(9),
             "bv": sec(10), "cx": sec(11), "cg": sec(12), "dcq": padc(sec(13), 256), "dckv": sec(14),
             "dkr": rope_slot(sec(15)), "dkrs": rope_slot(_rot_half(sec(15)))}
    cat = jnp.concatenate([parts[n] for n, _ in _SECTIONS], axis=1)
    return padc(cat, N_CAT).astype(MXU)


def _dot(a, b):
    return jnp.dot(a, b, preferred_element_type=f32)


def _dot_nt(a, b):
    return lax.dot_general(a, b, (((1,), (1,)), ((), ())), preferred_element_type=f32)


def _mm_kernel(a_ref, b_ref, o_ref):
    o_ref[...] = _dot(a_ref[...].astype(MXU), b_ref[...]).astype(o_ref.dtype)


def matmul(a, b, tm, tn, out_dtype=f32):
    M, K = a.shape
    _, N = b.shape
    assert M % tm == 0 and N % tn == 0
    return pl.pallas_call(
        _mm_kernel,
        grid=(N // tn, M // tm),
        in_specs=[pl.BlockSpec((tm, K), lambda j, i: (i, 0)),
                  pl.BlockSpec((K, tn), lambda j, i: (0, j))],
        out_specs=pl.BlockSpec((tm, tn), lambda j, i: (i, j)),
        out_shape=jax.ShapeDtypeStruct((M, N), out_dtype),
        compiler_params=pltpu.CompilerParams(dimension_semantics=("arbitrary", "arbitrary"),
                                             vmem_limit_bytes=VMEM_LIMIT),
        name="matmul",
    )(a, b)


def _cmp_kernel(t_ref, w1_ref, pos_ref, w2_ref, o_ref):
    half = CMP_STRIDE * HEAD_DIM
    t = t_ref[0, 0].astype(MXU)
    y1 = _dot(t, w1_ref[0, :half, :])
    y2 = _dot(t, w1_ref[0, half:, :])
    pos = jnp.broadcast_to(pos_ref[0], (8, 2 * half)).astype(MXU)
    pc = _dot(pos, w1_ref[0])[0:1]
    nc = y2.shape[0]
    hid = y1 + pltpu.roll(y2, nc - 1, 0) + pc
    o_ref[0, 0] = _dot(jax.nn.gelu(hid).astype(MXU), w2_ref[0])


def nsa_compress(t, w1, pos, w2):
    _, BG, NC, F = t.shape
    return pl.pallas_call(
        _cmp_kernel,
        grid=(2, BG),
        in_specs=[pl.BlockSpec((1, 1, NC, F), lambda j, i: (j, i, 0, 0)),
                  pl.BlockSpec((1, 2 * F, CMP_HID), lambda j, i: (j, 0, 0)),
                  pl.BlockSpec((1, 1, 2 * F), lambda j, i: (j, 0, 0)),
                  pl.BlockSpec((1, CMP_HID, LANE), lambda j, i: (j, 0, 0))],
        out_specs=pl.BlockSpec((1, 1, NC, LANE), lambda j, i: (j, i, 0, 0)),
        out_shape=jax.ShapeDtypeStruct((2, BG, NC, LANE), f32),
        compiler_params=pltpu.CompilerParams(dimension_semantics=("arbitrary", "arbitrary"),
                                             vmem_limit_bytes=VMEM_LIMIT),
        name="nsa_compress",
    )(t, w1, pos, w2)


def _nsa_kernel(q_ref, kc_ref, vc_ref, ks_ref, vs_ref, kw_ref, vw_ref, gl_ref, cover_ref, e_ref, o_ref,
                *, tq, n_cmp, n_sel, n_top):
    tk = tq
    g = pl.program_id(1)
    qi = pl.program_id(2)
    q0 = qi * tq
    lane = lax.broadcasted_iota(jnp.int32, (tq, LANE), 1)
    lo_half = lane < HEAD_DIM
    q = q_ref[0] * (HEAD_DIM ** -0.5)
    qh = (jnp.where(lo_half, q, 0.0).astype(MXU), jnp.where(lo_half, 0.0, q).astype(MXU))
    alibi = [2.0 ** (-8.0 * (h + 1) / NSA_HEADS) for h in range(NSA_HEADS)]
    slopes = [jnp.where(g == 0, alibi[n], alibi[NSA_GROUP + n]) for n in range(NSA_GROUP)]
    tpos = q0 + lax.broadcasted_iota(jnp.int32, (tq, 1), 0)

    nc = kc_ref.shape[2]
    cidx = lax.broadcasted_iota(jnp.int32, (1, nc), 1)
    dist_c = tpos - (cidx * CMP_STRIDE + (CMP_LEN - 1))
    mask_c = (dist_c >= 0) & (cidx < n_cmp)
    dist_cf = dist_c.astype(f32)
    kc = kc_ref[0, 0].astype(MXU)
    vc = vc_ref[0, 0].astype(MXU)
    o_cmp = []
    imp = jnp.zeros((tq, LANE), f32)
    for n in range(NSA_GROUP):
        s = _dot_nt(qh[n], kc) - slopes[n] * dist_cf
        sm = jnp.where(mask_c, s, NEG)
        m = jnp.max(sm, axis=1, keepdims=True)
        p = jnp.where(mask_c, jnp.exp(sm - m), 0.0)
        p = (p / jnp.maximum(jnp.sum(p, axis=1, keepdims=True), 1e-30)).astype(MXU)
        o_cmp.append(_dot(p, vc))
        imp = imp + _dot(p, cover_ref[...])

    blk = lane.astype(f32)
    cur = jnp.right_shift(tpos, 6)
    forced = (lane == 0) | (lane == cur)
    valid = lane * SEL_BLOCK <= tpos
    imp = jnp.where(forced, FORCE_SCORE, jnp.where(valid, imp, -1.0))
    imp = jnp.where(lane < n_sel, imp, NEG)
    sel = jnp.zeros((tq, LANE), f32)
    for _ in range(n_top):
        m = jnp.max(imp, axis=1, keepdims=True)
        first = jnp.min(jnp.where(imp == m, blk, float(LANE)), axis=1, keepdims=True)
        pick = blk == first
        sel = jnp.where(pick, 1.0, sel)
        imp = jnp.where(pick, 2 * NEG, imp)
    sel = sel.astype(MXU)

    def flash(k_ref, v_ref, lo, hi, mask_fn):
        def body(kt, carry):
            k0 = pl.multiple_of(kt * tk, tk)
            k = k_ref[0, pl.ds(k0, tk), :].astype(MXU)
            v = v_ref[0, pl.ds(k0, tk), :].astype(MXU)
            dist = tpos - (k0 + lax.broadcasted_iota(jnp.int32, (1, tk), 1))
            mask = mask_fn(kt, dist)
            dist_f = dist.astype(f32)
            out = []
            for n in range(NSA_GROUP):
                m, l, acc = carry[n]
                sm = jnp.where(mask, _dot_nt(qh[n], k) - slopes[n] * dist_f, NEG)
                m_new = jnp.maximum(m, jnp.max(sm, axis=1, keepdims=True))
                alpha = jnp.exp(m - m_new)
                p = jnp.where(mask, jnp.exp(sm - m_new), 0.0)
                l = alpha * l + jnp.sum(p, axis=1, keepdims=True)
                acc = alpha * acc + _dot(p.astype(MXU), v)
                out.append((m_new, l, acc))
            return tuple(out)

        init = tuple((jnp.full((tq, 1), NEG, f32), jnp.zeros((tq, 1), f32), jnp.zeros((tq, LANE), f32))
                     for _ in range(NSA_GROUP))
        res = lax.fori_loop(lo, hi, body, init)
        return [acc / jnp.maximum(l, 1e-30) for (_, l, acc) in res]

    o_sel = flash(ks_ref, vs_ref, 0, qi + 1,
                  lambda kt, dist: (_dot(sel, e_ref[kt]) > 0.5) & (dist >= 0))
    o_win = flash(kw_ref, vw_ref, jnp.maximum(qi - WINDOW // tk, 0), qi + 1,
                  lambda kt, dist: (dist >= 0) & (dist < WINDOW))

    sig = jax.nn.sigmoid(gl_ref[0])

    def gate(n, j):
        col = 3 * (NSA_GROUP * g + n) + j
        return jnp.sum(jnp.where(lane == col, sig, 0.0), axis=1, keepdims=True)

    o = [gate(n, 0) * o_cmp[n] + gate(n, 1) * o_sel[n] + gate(n, 2) * o_win[n] for n in range(NSA_GROUP)]
    o_ref[0] = jnp.where(lo_half, o[0], o[1])


def nsa_attention(h, cmp_pos, cmp_w1, cmp_w2, tq=256):
    B, S, _ = h.shape
    G = NSA_KV_HEADS
    NC = S // CMP_STRIDE
    n_cmp = (S - CMP_LEN) // CMP_STRIDE + 1
    n_sel = S // SEL_BLOCK
    n_top = min(SEL_TOPK, n_sel)
    F = CMP_STRIDE * HEAD_DIM
    assert S % tq == 0 and WINDOW % tq == 0 and tq % SEL_BLOCK == 0 and n_sel <= LANE

    def chunks(off):
        t = h[:, :, off:off + G * HEAD_DIM].reshape(B, NC, CMP_STRIDE, G, HEAD_DIM)
        return t.transpose(0, 3, 1, 2, 4).reshape(B * G, NC, F)

    t = jnp.stack([chunks(OFF["kc"]), chunks(OFF["vc"])])
    w2 = jnp.concatenate([cmp_w2, cmp_w2], axis=-1).astype(MXU)
    kvc = nsa_compress(t, cmp_w1.astype(MXU), cmp_pos.reshape(2, 1, 2 * F), w2)

    c0 = np.arange(n_cmp)[:, None] * CMP_STRIDE
    j0 = np.arange(n_sel)[None, :] * SEL_BLOCK
    cover = np.clip(np.minimum(c0 + CMP_LEN, j0 + SEL_BLOCK) - np.maximum(c0, j0), 0, None) / CMP_LEN
    cover_p = np.zeros((NC, LANE), np.float32)
    cover_p[:n_cmp, :n_sel] = cover
    expand = (np.arange(S)[None, :] // SEL_BLOCK == np.arange(LANE)[:, None]).astype(np.float32)
    expand = expand.reshape(LANE, S // tq, tq).transpose(1, 0, 2)

    def col(name, g_stride=1):
        return OFF[name] // LANE

    q_spec = pl.BlockSpec((1, tq, LANE), lambda b, g, i: (b, i, col("aq") + g))
    c_specs = [pl.BlockSpec((1, 1, NC, LANE), lambda b, g, i, j=j: (j, b * G + g, 0, 0)) for j in range(2)]
    kv_specs = [pl.BlockSpec((1, S, LANE), lambda b, g, i, c=col(n): (b, 0, c + g)) for n in ("ks", "vs", "kw", "vw")]
    gl_spec = pl.BlockSpec((1, tq, LANE), lambda b, g, i: (b, i, col("ag")))
    cover_spec = pl.BlockSpec((NC, LANE), lambda b, g, i: (0, 0))
    e_spec = pl.BlockSpec((S // tq, LANE, tq), lambda b, g, i: (0, 0, 0))
    return pl.pallas_call(
        functools.partial(_nsa_kernel, tq=tq, n_cmp=n_cmp, n_sel=n_sel, n_top=n_top),
        grid=(B, G, S // tq),
        in_specs=[q_spec] + c_specs + kv_specs + [gl_spec, cover_spec, e_spec],
        out_specs=pl.BlockSpec((1, tq, LANE), lambda b, g, i: (b, i, g)),
        out_shape=jax.ShapeDtypeStruct((B, S, NSA_HEADS * HEAD_DIM), f32),
        compiler_params=pltpu.CompilerParams(dimension_semantics=("arbitrary", "arbitrary", "arbitrary"),
                                             vmem_limit_bytes=VMEM_LIMIT),
        name="nsa_attention",
    )(h, kvc, kvc, h, h, h, h, h, jnp.asarray(cover_p, MXU), jnp.asarray(expand, MXU))


def _log_sigmoid(z):
    return jnp.minimum(z, 0.0) - jnp.log1p(jnp.exp(-jnp.abs(z)))


def _split3(x):
    hi = x.astype(MXU)
    r = x - hi.astype(f32)
    mid = r.astype(MXU)
    return hi, mid, (r - mid.astype(f32)).astype(MXU)


def _sb_kernel(q_ref, k_ref, v_ref, u_ref, o_ref, *, tq):
    tk = tq
    qi = pl.program_id(2)
    lane = lax.broadcasted_iota(jnp.int32, (tq, LANE), 1)
    lo_half = lane < HEAD_DIM
    q = q_ref[0] * (HEAD_DIM ** -0.5)
    qh = (jnp.where(lo_half, q, 0.0).astype(MXU), jnp.where(lo_half, 0.0, q).astype(MXU))
    tpos = qi * tq + lax.broadcasted_iota(jnp.int32, (tq, 1), 0)
    u = u_ref[...]

    def body(i, carry):
        kt = qi - i
        k0 = pl.multiple_of(kt * tk, tk)
        k = k_ref[0, pl.ds(k0, tk), :].astype(MXU)
        v = v_ref[0, pl.ds(k0, tk), :].astype(MXU)
        strict = (k0 + lax.broadcasted_iota(jnp.int32, (1, tk), 1)) < tpos
        out = []
        for n in range(2):
            c, acc = carry[n]
            z = _dot_nt(qh[n], k)
            ls = _log_sigmoid(z)
            log_1m = jnp.where(strict, ls - z, 0.0)
            hi, mid, lo = _split3(log_1m)
            tail = (_dot(hi, u) + _dot(mid, u)) + _dot(lo, u) + c
            a = jnp.where(strict, jnp.exp(ls + tail), 0.0)
            acc = acc + _dot(a.astype(MXU), v)
            c = c + jnp.sum(log_1m, axis=1, keepdims=True)
            out.append((c, acc))
        return tuple(out)

    init = tuple((jnp.zeros((tq, 1), f32), jnp.zeros((tq, LANE), f32)) for _ in range(2))
    res = lax.fori_loop(0, qi + 1, body, init)
    o_ref[0] = jnp.where(lo_half, res[0][1], res[1][1])


def stick_breaking_attention(h, tq=256):
    B, S, _ = h.shape
    assert S % tq == 0
    tri = jnp.asarray(np.tril(np.ones((tq, tq), np.float32), -1), MXU)
    cq, ck, cv = (OFF[n] // LANE for n in ("bq", "bk", "bv"))
    return pl.pallas_call(
        functools.partial(_sb_kernel, tq=tq),
        grid=(B, SB_HEADS // 2, S // tq),
        in_specs=[pl.BlockSpec((1, tq, LANE), lambda b, p, i: (b, i, cq + p)),
                  pl.BlockSpec((1, S, LANE), lambda b, p, i: (b, 0, ck + p)),
                  pl.BlockSpec((1, S, LANE), lambda b, p, i: (b, 0, cv + p)),
                  pl.BlockSpec((tq, tq), lambda b, p, i: (0, 0))],
        out_specs=pl.BlockSpec((1, tq, LANE), lambda b, p, i: (b, i, p)),
        out_shape=jax.ShapeDtypeStruct((B, S, SB_HEADS * HEAD_DIM), f32),
        compiler_params=pltpu.CompilerParams(dimension_semantics=("arbitrary", "arbitrary", "arbitrary"),
                                             vmem_limit_bytes=VMEM_LIMIT),
        name="sb_attention",
    )(h, h, h, tri)


def _neg_expm1(y):
    series = -y * (1.0 + y * (1.0 / 2 + y * (1.0 / 6 + y * (1.0 / 24 + y * (1.0 / 120)))))
    return jnp.where(y > -0.1, series, 1.0 - jnp.exp(y))


def _rglru_kernel(x_ref, xg_ref, cw_ref, cb_ref, gaw_ref, gab_ref, gxw_ref, gxb_ref, lam_ref, o_ref):
    x = x_ref[0]
    S = x.shape[0]
    row = lax.broadcasted_iota(jnp.int32, (S, 1), 0)

    def shifted(t, d, fill):
        return jnp.where(row >= d, pltpu.roll(t, d, 0), fill)

    u = cb_ref[...] + x * cw_ref[CONV_W - 1:CONV_W, :]
    for d in range(1, CONV_W):
        u = u + shifted(x, d, 0.0) * cw_ref[CONV_W - 1 - d:CONV_W - d, :]
    ub = u.astype(MXU)
    r = jax.nn.sigmoid(_dot(ub, gaw_ref[...]) + gab_ref[...])
    i = jax.nn.sigmoid(_dot(ub, gxw_ref[...]) + gxb_ref[...])
    lam = lam_ref[...]
    softplus_neg = jnp.maximum(-lam, 0.0) + jnp.log1p(jnp.exp(-jnp.abs(lam)))
    log_a = -LRU_C * r * softplus_neg
    a = jnp.exp(log_a)
    b = jnp.sqrt(_neg_expm1(2.0 * log_a)) * (i * u)
    d = 1
    while d < S:
        b = a * shifted(b, d, 0.0) + b
        a = a * shifted(a, d, 1.0)
        d *= 2
    o_ref[0] = b * jax.nn.gelu(xg_ref[0])


def _block_diag(w):
    n, c, _ = w.shape
    out = jnp.zeros((n * c, n * c), w.dtype)
    for j in range(n):
        out = out.at[j * c:(j + 1) * c, j * c:(j + 1) * c].set(w[j])
    return out


def rglru_block(h, conv_w, conv_b, ga_w, ga_b, gx_w, gx_b, lru_lambda):
    B, S, _ = h.shape
    W = RNN_W
    cx, cg = OFF["cx"] // W, OFF["cg"] // W
    vec = pl.BlockSpec((1, W), lambda b: (0, 0))
    mat = pl.BlockSpec((W, W), lambda b: (0, 0))
    return pl.pallas_call(
        _rglru_kernel,
        grid=(B,),
        in_specs=[pl.BlockSpec((1, S, W), lambda b: (b, 0, cx)), pl.BlockSpec((1, S, W), lambda b: (b, 0, cg)),
                  pl.BlockSpec((CONV_W, W), lambda b: (0, 0)), vec, mat, vec, mat, vec, vec],
        out_specs=pl.BlockSpec((1, S, W), lambda b: (b, 0, 0)),
        out_shape=jax.ShapeDtypeStruct((B, S, W), f32),
        compiler_params=pltpu.CompilerParams(dimension_semantics=("arbitrary",), vmem_limit_bytes=VMEM_LIMIT),
        name="rglru",
    )(h, h, conv_w, conv_b.reshape(1, W), _block_diag(ga_w).astype(MXU), ga_b.reshape(1, W),
      _block_diag(gx_w).astype(MXU), gx_b.reshape(1, W), lru_lambda.reshape(1, W))


def _rot_half(t):
    d = t.shape[-1]
    return jnp.concatenate([-t[..., d // 2:], t[..., :d // 2]], axis=-1)


def _rms(x, g, width):
    return x * lax.rsqrt(jnp.sum(x * x, axis=-1, keepdims=True) * (1.0 / width) + RMS_EPS) * g


def _mla_prep_kernel(cq_ref, ckv_ref, kr_ref, krs_ref, gq_ref, gkv_ref, wq_ref, wqs_ref, wk_ref, wv_ref,
                     cosq_ref, sinq_ref, cosk_ref, sink_ref, q_ref, k_ref, v_ref):
    cq = _rms(cq_ref[0], gq_ref[...], MLA_Q_RANK).astype(MXU)
    ckv = _rms(ckv_ref[0], gkv_ref[...], MLA_KV_RANK).astype(MXU)
    scale = (MLA_NOPE + MLA_ROPE) ** -0.5
    q = _dot(cq, wq_ref[...]) * cosq_ref[...] + _dot(cq, wqs_ref[...]) * sinq_ref[...]
    q_ref[0] = (q * scale).astype(q_ref.dtype)
    k_rope = kr_ref[0] * cosk_ref[...] + krs_ref[0] * sink_ref[...]
    k = _dot(ckv, wk_ref[...])
    k_ref[0] = (k + jnp.concatenate([k_rope] * MLA_HEADS, axis=1)).astype(k_ref.dtype)
    v_ref[0] = _dot(ckv, wv_ref[...]).astype(v_ref.dtype)


def _mla_attn_kernel(q_ref, k_ref, v_ref, o_ref, *, tq):
    tk = tq
    qi = pl.program_id(2)
    tpos = qi * tq + lax.broadcasted_iota(jnp.int32, (tq, 1), 0)
    q = q_ref[0]

    def body(kt, carry):
        k0 = pl.multiple_of(kt * tk, tk)
        k = k_ref[0, pl.ds(k0, tk), :]
        v = v_ref[0, pl.ds(k0, tk), :]
        mask = (k0 + lax.broadcasted_iota(jnp.int32, (1, tk), 1)) <= tpos
        out = []
        for n in range(2):
            m, l, acc = carry[n]
            sm = jnp.where(mask, _dot_nt(q[:, n * LANE:(n + 1) * LANE], k[:, n * LANE:(n + 1) * LANE]), NEG)
            m_new = jnp.maximum(m, jnp.max(sm, axis=1, keepdims=True))
            alpha = jnp.exp(m - m_new)
            p = jnp.where(mask, jnp.exp(sm - m_new), 0.0)
            l = alpha * l + jnp.sum(p, axis=1, keepdims=True)
            acc = alpha * acc + _dot(p.astype(MXU), v)
            out.append((m_new, l, acc))
        return tuple(out)

    init = tuple((jnp.full((tq, 1), NEG, f32), jnp.zeros((tq, 1), f32), jnp.zeros((tq, LANE), f32))
                 for _ in range(2))
    res = lax.fori_loop(0, qi + 1, body, init)
    o = [acc / jnp.maximum(l, 1e-30) for (_, l, acc) in res]
    lane = lax.broadcasted_iota(jnp.int32, (tq, LANE), 1)
    o_ref[0] = jnp.where(lane < MLA_V, o[0], o[1])


def mla_attention(h, q_norm, kv_norm, w_uq, w_ukv, tr=512, tq=256):
    B, S, _ = h.shape
    H = MLA_HEADS
    dq = MLA_NOPE + MLA_ROPE
    HW = H * LANE
    wq3 = w_uq.reshape(MLA_Q_RANK, H, dq)
    wq_rot = jnp.concatenate([jnp.zeros_like(wq3[..., :MLA_NOPE]), _rot_half(wq3[..., MLA_NOPE:])], axis=-1)

    def pad_q(w3):
        w3 = jnp.pad(w3, ((0, 256 - MLA_Q_RANK), (0, 0), (0, LANE - dq)))
        return w3.reshape(256, HW).astype(MXU)

    wkv3 = w_ukv.reshape(MLA_KV_RANK, H, MLA_NOPE + MLA_V)
    wk = jnp.pad(wkv3[..., :MLA_NOPE], ((0, 0), (0, 0), (0, LANE - MLA_NOPE))).reshape(MLA_KV_RANK, HW).astype(MXU)
    wv = wkv3[..., MLA_NOPE:].reshape(MLA_KV_RANK, H * MLA_V).astype(MXU)
    gq = jnp.pad(q_norm, (0, 256 - MLA_Q_RANK)).reshape(1, 256)
    gkv = kv_norm.reshape(1, MLA_KV_RANK)
    inv = ROPE_THETA ** (-jnp.arange(0, MLA_ROPE, 2, dtype=f32) / MLA_ROPE)
    ang = jnp.arange(S, dtype=f32)[:, None] * inv[None, :]
    cos2 = jnp.concatenate([jnp.cos(ang), jnp.cos(ang)], axis=1)
    sin2 = jnp.concatenate([jnp.sin(ang), jnp.sin(ang)], axis=1)
    tail = LANE - dq
    cos_k = jnp.concatenate([jnp.zeros((S, MLA_NOPE), f32), cos2, jnp.zeros((S, tail), f32)], axis=1)
    sin_k = jnp.concatenate([jnp.zeros((S, MLA_NOPE), f32), sin2, jnp.zeros((S, tail), f32)], axis=1)
    cos_q = jnp.tile(jnp.concatenate([jnp.ones((S, MLA_NOPE), f32), cos2, jnp.zeros((S, tail), f32)], axis=1), (1, H))
    sin_q = jnp.tile(sin_k, (1, H))

    c_cq, c_ckv, c_kr, c_krs = OFF["dcq"] // 256, OFF["dckv"] // LANE, OFF["dkr"] // LANE, OFF["dkrs"] // LANE
    full = lambda shape: pl.BlockSpec(shape, lambda b, i: (0, 0))
    tab = lambda w: pl.BlockSpec((tr, w), lambda b, i: (i, 0))
    q, k, v = pl.pallas_call(
        _mla_prep_kernel,
        grid=(B, S // tr),
        in_specs=[pl.BlockSpec((1, tr, 256), lambda b, i: (b, i, c_cq)),
                  pl.BlockSpec((1, tr, LANE), lambda b, i: (b, i, c_ckv)),
                  pl.BlockSpec((1, tr, LANE), lambda b, i: (b, i, c_kr)),
                  pl.BlockSpec((1, tr, LANE), lambda b, i: (b, i, c_krs)),
                  full((1, 256)), full((1, MLA_KV_RANK)), full((256, HW)), full((256, HW)),
                  full((MLA_KV_RANK, HW)), full((MLA_KV_RANK, H * MLA_V)),
                  tab(HW), tab(HW), tab(LANE), tab(LANE)],
        out_specs=[pl.BlockSpec((1, tr, HW), lambda b, i: (b, i, 0)),
                   pl.BlockSpec((1, tr, HW), lambda b, i: (b, i, 0)),
                   pl.BlockSpec((1, tr, H * MLA_V), lambda b, i: (b, i, 0))],
        out_shape=[jax.ShapeDtypeStruct((B, S, HW), MXU), jax.ShapeDtypeStruct((B, S, HW), MXU),
                   jax.ShapeDtypeStruct((B, S, H * MLA_V), MXU)],
        compiler_params=pltpu.CompilerParams(dimension_semantics=("arbitrary", "arbitrary"),
                                             vmem_limit_bytes=VMEM_LIMIT),
        name="mla_prep",
    )(h, h, h, h, gq, gkv, pad_q(wq3), pad_q(wq_rot), wk, wv, cos_q, sin_q, cos_k, sin_k)
    return pl.pallas_call(
        functools.partial(_mla_attn_kernel, tq=tq),
        grid=(B, H // 2, S // tq),
        in_specs=[pl.BlockSpec((1, tq, 2 * LANE), lambda b, p, i: (b, i, p)),
                  pl.BlockSpec((1, S, 2 * LANE), lambda b, p, i: (b, 0, p)),
                  pl.BlockSpec((1, S, LANE), lambda b, p, i: (b, 0, p))],
        out_specs=pl.BlockSpec((1, tq, LANE), lambda b, p, i: (b, i, p)),
        out_shape=jax.ShapeDtypeStruct((B, S, H * MLA_V), f32),
        compiler_params=pltpu.CompilerParams(dimension_semantics=("arbitrary", "arbitrary", "arbitrary"),
                                             vmem_limit_bytes=VMEM_LIMIT),
        name="mla_attention",
    )(q, k, v)


def layer_norm(x, g, b):
    mu = jnp.mean(x, -1, keepdims=True)
    var = jnp.mean(jnp.square(x - mu), -1, keepdims=True)
    return (x - mu) * lax.rsqrt(var + LN_EPS) * g + b


def rms_norm(x, g):
    return x * lax.rsqrt(jnp.mean(jnp.square(x), -1, keepdims=True) + RMS_EPS) * g


def masked_softmax(s, mask):
    s = jnp.where(mask, s.astype(f32), -jnp.inf)
    m = jnp.max(s, -1, keepdims=True)
    p = jnp.exp(s - jnp.where(jnp.isfinite(m), m, 0.0))
    return p / jnp.maximum(jnp.sum(p, -1, keepdims=True), 1e-30)


def rope(x, pos):
    d = x.shape[-1]
    inv = ROPE_THETA ** (-jnp.arange(0, d, 2, dtype=f32) / d)
    ang = pos.astype(f32)[:, None] * inv[None, :]
    cos, sin = jnp.cos(ang)[:, None, :], jnp.sin(ang)[:, None, :]
    x1, x2 = x[..., : d // 2], x[..., d // 2:]
    return jnp.concatenate([x1 * cos - x2 * sin, x1 * sin + x2 * cos], -1)


def q_blocks(t):
    b, s = t.shape[:2]
    return jnp.moveaxis(t.reshape(b, s // Q_BLOCK, Q_BLOCK, *t.shape[2:]), 1, 0)


def unblock(t):
    nq, b, q = t.shape[:3]
    return jnp.moveaxis(t, 0, 1).reshape(b, nq * q, *t.shape[3:])


def hybrid_mixer(x, pos, w_in, cmp_pos, cmp_w1, cmp_w2, conv_w, conv_b, ga_w, ga_b, gx_w, gx_b, lru_lambda,
                 q_norm, kv_norm, w_uq, w_ukv, w_branch, w_out):
    B, S, D = x.shape
    h = matmul(x.reshape(B * S, D), _cat_w_in(w_in), 512, _IN_TILE).reshape(B, S, N_CAT)

    def sec(name, w):
        return h[..., OFF[name]:OFF[name] + w]

    def heads(t):
        return t.reshape(B, S, -1, HEAD_DIM)

    o_a = nsa_attention(h, cmp_pos, cmp_w1, cmp_w2)
    o_b = stick_breaking_attention(h)
    o_c = rglru_block(h, conv_w, conv_b, ga_w, ga_b, gx_w, gx_b, lru_lambda)
    o_d = mla_attention(h, q_norm, kv_norm, w_uq, w_ukv)
    branches = jnp.stack([o_a, o_b, o_c, o_d], axis=2)
    up = jnp.einsum('bsnc,ncd->bsnd', branches, w_branch)
    gates = jax.nn.sigmoid(sec("mg", N_MIXERS * D).reshape(B, S, N_MIXERS, D))
    return jnp.einsum('bsd,de->bse', jnp.sum(gates * up, axis=2), w_out)


def memory_cross_attention(x, mem, wq, wkv, wo):
    B, S, _ = x.shape
    M = mem.shape[1]
    F = X_HEADS * X_HEAD_DIM
    q = jnp.einsum('bsd,df->bsf', x, wq).reshape(B, S, X_HEADS, X_HEAD_DIM)
    kv = jnp.einsum('bmd,df->bmf', mem, wkv)
    k = kv[..., :F].reshape(B, M, X_HEADS, X_HEAD_DIM)
    v = kv[..., F:].reshape(B, M, X_HEADS, X_HEAD_DIM)
    s = jnp.einsum('bshd,bmhd->bhsm', q, k).astype(f32) * X_HEAD_DIM ** -0.5
    p = jax.nn.softmax(s, axis=-1)
    o = jnp.einsum('bhsm,bmhd->bshd', p, v).reshape(B, S, F)
    return jnp.einsum('bsf,fd->bsd', o, wo)


def grouped_expert_ffn(xt, expert_idx, weights, w_gu, w_down):
    N, D = xt.shape
    K = expert_idx.shape[1]
    E = w_gu.shape[0]
    F = w_down.shape[1]
    C = EXPERT_CHUNK
    A = N * K
    flat_e = expert_idx.reshape(A)
    flat_tok = jnp.arange(A, dtype=jnp.int32) // K
    order = jnp.argsort(flat_e)
    se, st = flat_e[order], flat_tok[order]
    counts = jax.ops.segment_sum(jnp.ones((A,), jnp.int32), flat_e, num_segments=E)
    padded = (counts + C - 1) // C * C
    pad_end = jnp.cumsum(padded)
    pad_start = pad_end - padded
    start = jnp.cumsum(counts) - counts
    dest = pad_start[se] + jnp.arange(A, dtype=jnp.int32) - start[se]
    n_chunks = -(-(A + E * (C - 1)) // C)
    P = n_chunks * C
    slot_tok = jnp.full((P,), N, jnp.int32).at[dest].set(st)
    x_pad = jnp.concatenate([xt, jnp.zeros((1, D), xt.dtype)], axis=0)
    xb = x_pad[slot_tok].reshape(n_chunks, C, D)
    chunk_start = jnp.arange(n_chunks, dtype=jnp.int32) * C
    chunk_e = jnp.minimum(jnp.sum((pad_end[None, :] <= chunk_start[:, None]).astype(jnp.int32), axis=1), E - 1)

    def run(args):
        xc, e = args
        gu = xc @ w_gu[e]
        return (jax.nn.silu(gu[:, :F]) * gu[:, F:]) @ w_down[e]

    yb = lax.map(run, (xb, chunk_e)).reshape(P, D)
    contrib = yb[dest] * weights.reshape(A)[order][:, None]
    return jnp.zeros((N, D), yb.dtype).at[st].add(contrib)


def hier_moe(x, rg_w, rg_b, re_w, re_b, w_gu, w_down):
    B, S, D = x.shape
    xt = x.reshape(B * S, D)
    N = xt.shape[0]
    rows = jnp.arange(N)
    p_grp = jax.nn.softmax((xt @ rg_w).astype(f32) + rg_b, axis=-1)
    grp = jnp.argmax(p_grp, axis=-1).astype(jnp.int32)
    p_g = p_grp[rows, grp]
    e_logits = ((xt @ re_w).astype(f32) + re_b).reshape(N, N_GROUPS, EXPERTS_PER_GROUP)
    p_e = jax.nn.softmax(e_logits[rows, grp], axis=-1)
    top_p, top_i = lax.top_k(p_e, TOPK_IN_GROUP)
    w = p_g[:, None] * top_p / jnp.sum(top_p, -1, keepdims=True)
    expert_idx = (grp[:, None] * EXPERTS_PER_GROUP + top_i).astype(jnp.int32)
    return grouped_expert_ffn(xt, expert_idx, w, w_gu, w_down).reshape(B, S, D)


def kernel(x, mem, w_in, nsa_cmp_pos, nsa_cmp_w1, nsa_cmp_w2, rnn_conv_w, rnn_conv_b, rnn_ga_w, rnn_ga_b,
           rnn_gx_w, rnn_gx_b, rnn_lambda, mla_q_norm, mla_kv_norm, mla_w_uq, mla_w_ukv, w_branch, w_out,
           ln1_g, ln1_b, x_wq, x_wkv, x_wo, ln2_g, ln2_b, moe_rg_w, moe_rg_b, moe_re_w, moe_re_b,
           moe_w_gu, moe_w_down, ln3_g, ln3_b):
    pos = jnp.arange(x.shape[1])
    for l in range(DEPTH):
        y = hybrid_mixer(x, pos, w_in[l], nsa_cmp_pos[l], nsa_cmp_w1[l], nsa_cmp_w2[l], rnn_conv_w[l], rnn_conv_b[l],
                         rnn_ga_w[l], rnn_ga_b[l], rnn_gx_w[l], rnn_gx_b[l], rnn_lambda[l], mla_q_norm[l],
                         mla_kv_norm[l], mla_w_uq[l], mla_w_ukv[l], w_branch[l], w_out[l])
        x = layer_norm(DN_ALPHA * x + y, ln1_g[l], ln1_b[l])
        y = memory_cross_attention(x, mem, x_wq[l], x_wkv[l], x_wo[l])
        x = layer_norm(DN_ALPHA * x + y, ln2_g[l], ln2_b[l])
        y = hier_moe(x, moe_rg_w[l], moe_rg_b[l], moe_re_w[l], moe_re_b[l], moe_w_gu[l], moe_w_down[l])
        x = layer_norm(DN_ALPHA * x + y, ln3_g[l], ln3_b[l])
    return x
```

```python
import functools

import numpy as np
import jax
import jax.numpy as jnp
from jax import lax
from jax.experimental import pallas as pl
from jax.experimental.pallas import tpu as pltpu

D_MODEL = 1024
DEPTH = 4
HEAD_DIM = 64
N_MIXERS = 4
MIX_W = 256
Q_BLOCK = 128
NSA_HEADS = 4
NSA_KV_HEADS = 2
NSA_GROUP = NSA_HEADS // NSA_KV_HEADS
CMP_LEN = 32
CMP_STRIDE = 16
CMP_HID = 256
SEL_BLOCK = 64
SEL_TOPK = 8
WINDOW = 512
FORCE_SCORE = 1e4
SB_HEADS = 4
RNN_W = 256
RNN_BLOCKS = 4
RNN_BW = RNN_W // RNN_BLOCKS
CONV_W = 4
LRU_C = 8.0
MLA_HEADS = 4
MLA_Q_RANK = 192
MLA_KV_RANK = 128
MLA_NOPE = 64
MLA_ROPE = 32
MLA_V = 64
ROPE_THETA = 10000.0
X_HEADS = 4
X_HEAD_DIM = 128
N_GROUPS = 4
EXPERTS_PER_GROUP = 8
N_EXPERTS = N_GROUPS * EXPERTS_PER_GROUP
TOPK_IN_GROUP = 2
D_EXPERT = 512
EXPERT_CHUNK = 256
DN_ALPHA = (2.0 * DEPTH) ** 0.25
LN_EPS = 1e-5
RMS_EPS = 1e-6

IN_SPLITS = ((NSA_HEADS * HEAD_DIM,) + (NSA_KV_HEADS * HEAD_DIM,) * 6 + (NSA_HEADS * 3,)
             + (SB_HEADS * HEAD_DIM,) * 3
             + (RNN_W, RNN_W)
             + (MLA_Q_RANK, MLA_KV_RANK, MLA_ROPE)
             + (N_MIXERS * D_MODEL,))
IN_OFFSETS = tuple(int(o) for o in np.concatenate([[0], np.cumsum(IN_SPLITS)[:-1]]))
N_IN = sum(IN_SPLITS)

LANE = 128
VMEM_LIMIT = 48 * 1024 * 1024
NEG = -1e30

f32 = jnp.float32
MXU = jnp.bfloat16

_SECTIONS = (("mg", 4096), ("aq", 256), ("ks", 256), ("vs", 256), ("kw", 256), ("vw", 256), ("bq", 256),
             ("bk", 256), ("bv", 256), ("cx", 256), ("cg", 256), ("dcq", 256), ("kc", 128), ("vc", 128),
             ("ag", 128), ("dckv", 128), ("dkr", 128), ("dkrs", 128))
_IN_TILE = 3840
OFF = {}
_o = 0
for _n, _w in _SECTIONS:
    OFF[_n] = _o
    _o += _w
N_CAT = -(-_o // _IN_TILE) * _IN_TILE


def _cat_w_in(w):
    def sec(i):
        return w[:, IN_OFFSETS[i]:IN_OFFSETS[i] + IN_SPLITS[i]]

    def dup(t):
        g0, g1 = t[:, :HEAD_DIM], t[:, HEAD_DIM:]
        return jnp.concatenate([g0, g0, g1, g1], axis=1)

    def padc(t, n):
        return jnp.pad(t, ((0, 0), (0, n - t.shape[1])))

    def rope_slot(t):
        return jnp.pad(t, ((0, 0), (MLA_NOPE, LANE - MLA_NOPE - MLA_ROPE)))

    parts = {"mg": sec(16), "aq": sec(0), "kc": sec(1), "vc": sec(2), "ks": dup(sec(3)), "vs": dup(sec(4)),
             "kw": dup(sec(5)), "vw": dup(sec(6)), "ag": padc(sec(7), 128), "bq": sec(8), "bk": sec(9),
             "bv": sec(10), "cx": sec(11), "cg": sec(12), "dcq": padc(sec(13), 256), "dckv": sec(14),
             "dkr": rope_slot(sec(15)), "dkrs": rope_slot(_rot_half(sec(15)))}
    cat = jnp.concatenate([parts[n] for n, _ in _SECTIONS], axis=1)
    return padc(cat, N_CAT).astype(MXU)


def _dot(a, b):
    return jnp.dot(a, b, preferred_element_type=f32)


def _dot_nt(a, b):
    return lax.dot_general(a, b, (((1,), (1,)), ((), ())), preferred_element_type=f32)


def _mm_kernel(a_ref, b_ref, o_ref):
    o_ref[...] = _dot(a_ref[...].astype(MXU), b_ref[...]).astype(o_ref.dtype)


def matmul(a, b, tm, tn, out_dtype=f32):
    M, K = a.shape
    _, N = b.shape
    assert M % tm == 0 and N % tn == 0
    return pl.pallas_call(
        _mm_kernel,
        grid=(N // tn, M // tm),
        in_specs=[pl.BlockSpec((tm, K), lambda j, i: (i, 0)),
                  pl.BlockSpec((K, tn), lambda j, i: (0, j))],
        out_specs=pl.BlockSpec((tm, tn), lambda j, i: (i, j)),
        out_shape=jax.ShapeDtypeStruct((M, N), out_dtype),
        compiler_params=pltpu.CompilerParams(dimension_semantics=("arbitrary", "arbitrary"),
                                             vmem_limit_bytes=VMEM_LIMIT),
        name="matmul",
    )(a, b)


def _cmp_kernel(t_ref, w1_ref, pos_ref, w2_ref, o_ref):
    half = CMP_STRIDE * HEAD_DIM
    t = t_ref[0, 0].astype(MXU)
    y1 = _dot(t, w1_ref[0, :half, :])
    y2 = _dot(t, w1_ref[0, half:, :])
    pos = jnp.broadcast_to(pos_ref[0], (8, 2 * half)).astype(MXU)
    pc = _dot(pos, w1_ref[0])[0:1]
    nc = y2.shape[0]
    hid = y1 + pltpu.roll(y2, nc - 1, 0) + pc
    o_ref[0, 0] = _dot(jax.nn.gelu(hid).astype(MXU), w2_ref[0])


def nsa_compress(t, w1, pos, w2):
    _, BG, NC, F = t.shape
    return pl.pallas_call(
        _cmp_kernel,
        grid=(2, BG),
        in_specs=[pl.BlockSpec((1, 1, NC, F), lambda j, i: (j, i, 0, 0)),
                  pl.BlockSpec((1, 2 * F, CMP_HID), lambda j, i: (j, 0, 0)),
                  pl.BlockSpec((1, 1, 2 * F), lambda j, i: (j, 0, 0)),
                  pl.BlockSpec((1, CMP_HID, LANE), lambda j, i: (j, 0, 0))],
        out_specs=pl.BlockSpec((1, 1, NC, LANE), lambda j, i: (j, i, 0, 0)),
        out_shape=jax.ShapeDtypeStruct((2, BG, NC, LANE), f32),
        compiler_params=pltpu.CompilerParams(dimension_semantics=("arbitrary", "arbitrary"),
                                             vmem_limit_bytes=VMEM_LIMIT),
        name="nsa_compress",
    )(t, w1, pos, w2)


def _nsa_kernel(q_ref, kc_ref, vc_ref, ks_ref, vs_ref, kw_ref, vw_ref, gl_ref, cover_ref, e_ref, o_ref,
                *, tq, n_cmp, n_sel, n_top):
    tk = tq
    g = pl.program_id(1)
    qi = pl.program_id(2)
    q0 = qi * tq
    lane = lax.broadcasted_iota(jnp.int32, (tq, LANE), 1)
    lo_half = lane < HEAD_DIM
    q = q_ref[0] * (HEAD_DIM ** -0.5)
    qh = (jnp.where(lo_half, q, 0.0).astype(MXU), jnp.where(lo_half, 0.0, q).astype(MXU))
    alibi = [2.0 ** (-8.0 * (h + 1) / NSA_HEADS) for h in range(NSA_HEADS)]
    slopes = [jnp.where(g == 0, alibi[n], alibi[NSA_GROUP + n]) for n in range(NSA_GROUP)]
    tpos = q0 + lax.broadcasted_iota(jnp.int32, (tq, 1), 0)

    nc = kc_ref.shape[2]
    cidx = lax.broadcasted_iota(jnp.int32, (1, nc), 1)
    dist_c = tpos - (cidx * CMP_STRIDE + (CMP_LEN - 1))
    mask_c = (dist_c >= 0) & (cidx < n_cmp)
    dist_cf = dist_c.astype(f32)
    kc = kc_ref[0, 0].astype(MXU)
    vc = vc_ref[0, 0].astype(MXU)
    o_cmp = []
    imp = jnp.zeros((tq, LANE), f32)
    for n in range(NSA_GROUP):
        s = _dot_nt(qh[n], kc) - slopes[n] * dist_cf
        sm = jnp.where(mask_c, s, NEG)
        m = jnp.max(sm, axis=1, keepdims=True)
        p = jnp.where(mask_c, jnp.exp(sm - m), 0.0)
        p = (p / jnp.maximum(jnp.sum(p, axis=1, keepdims=True), 1e-30)).astype(MXU)
        o_cmp.append(_dot(p, vc))
        imp = imp + _dot(p, cover_ref[...])

    blk = lane.astype(f32)
    cur = jnp.right_shift(tpos, 6)
    forced = (lane == 0) | (lane == cur)
    valid = lane * SEL_BLOCK <= tpos
    imp = jnp.where(forced, FORCE_SCORE, jnp.where(valid, imp, -1.0))
    imp = jnp.where(lane < n_sel, imp, NEG)
    sel = jnp.zeros((tq, LANE), f32)
    for _ in range(n_top):
        m = jnp.max(imp, axis=1, keepdims=True)
        first = jnp.min(jnp.where(imp == m, blk, float(LANE)), axis=1, keepdims=True)
        pick = blk == first
        sel = jnp.where(pick, 1.0, sel)
        imp = jnp.where(pick, 2 * NEG, imp)
    sel = sel.astype(MXU)

    def flash(k_ref, v_ref, lo, hi, mask_fn):
        def body(kt, carry):
            k0 = pl.multiple_of(kt * tk, tk)
            k = k_ref[0, pl.ds(k0, tk), :].astype(MXU)
            v = v_ref[0, pl.ds(k0, tk), :].astype(MXU)
            dist = tpos - (k0 + lax.broadcasted_iota(jnp.int32, (1, tk), 1))
            mask = mask_fn(kt, dist)
            dist_f = dist.astype(f32)
            out = []
            for n in range(NSA_GROUP):
                m, l, acc = carry[n]
                sm = jnp.where(mask, _dot_nt(qh[n], k) - slopes[n] * dist_f, NEG)
                m_new = jnp.maximum(m, jnp.max(sm, axis=1, keepdims=True))
                alpha = jnp.exp(m - m_new)
                p = jnp.where(mask, jnp.exp(sm - m_new), 0.0)
                l = alpha * l + jnp.sum(p, axis=1, keepdims=True)
                acc = alpha * acc + _dot(p.astype(MXU), v)
                out.append((m_new, l, acc))
            return tuple(out)

        init = tuple((jnp.full((tq, 1), NEG, f32), jnp.zeros((tq, 1), f32), jnp.zeros((tq, LANE), f32))
                     for _ in range(NSA_GROUP))
        res = lax.fori_loop(lo, hi, body, init)
        return [acc / jnp.maximum(l, 1e-30) for (_, l, acc) in res]

    o_sel = flash(ks_ref, vs_ref, 0, qi + 1,
                  lambda kt, dist: (_dot(sel, e_ref[kt]) > 0.5) & (dist >= 0))
    o_win = flash(kw_ref, vw_ref, jnp.maximum(qi - WINDOW // tk, 0), qi + 1,
                  lambda kt, dist: (dist >= 0) & (dist < WINDOW))

    sig = jax.nn.sigmoid(gl_ref[0])

    def gate(n, j):
        col = 3 * (NSA_GROUP * g + n) + j
        return jnp.sum(jnp.where(lane == col, sig, 0.0), axis=1, keepdims=True)

    o = [gate(n, 0) * o_cmp[n] + gate(n, 1) * o_sel[n] + gate(n, 2) * o_win[n] for n in range(NSA_GROUP)]
    o_ref[0] = jnp.where(lo_half, o[0], o[1])


def nsa_attention(h, cmp_pos, cmp_w1, cmp_w2, tq=256):
    B, S, _ = h.shape
    G = NSA_KV_HEADS
    NC = S // CMP_STRIDE
    n_cmp = (S - CMP_LEN) // CMP_STRIDE + 1
    n_sel = S // SEL_BLOCK
    n_top = min(SEL_TOPK, n_sel)
    F = CMP_STRIDE * HEAD_DIM
    assert S % tq == 0 and WINDOW % tq == 0 and tq % SEL_BLOCK == 0 and n_sel <= LANE

    def chunks(off):
        t = h[:, :, off:off + G * HEAD_DIM].reshape(B, NC, CMP_STRIDE, G, HEAD_DIM)
        return t.transpose(0, 3, 1, 2, 4).reshape(B * G, NC, F)

    t = jnp.stack([chunks(OFF["kc"]), chunks(OFF["vc"])])
    w2 = jnp.concatenate([cmp_w2, cmp_w2], axis=-1).astype(MXU)
    kvc = nsa_compress(t, cmp_w1.astype(MXU), cmp_pos.reshape(2, 1, 2 * F), w2)

    c0 = np.arange(n_cmp)[:, None] * CMP_STRIDE
    j0 = np.arange(n_sel)[None, :] * SEL_BLOCK
    cover = np.clip(np.minimum(c0 + CMP_LEN, j0 + SEL_BLOCK) - np.maximum(c0, j0), 0, None) / CMP_LEN
    cover_p = np.zeros((NC, LANE), np.float32)
    cover_p[:n_cmp, :n_sel] = cover
    expand = (np.arange(S)[None, :] // SEL_BLOCK == np.arange(LANE)[:, None]).astype(np.float32)
    expand = expand.reshape(LANE, S // tq, tq).transpose(1, 0, 2)

    def col(name, g_stride=1):
        return OFF[name] // LANE

    q_spec = pl.BlockSpec((1, tq, LANE), lambda b, g, i: (b, i, col("aq") + g))
    c_specs = [pl.BlockSpec((1, 1, NC, LANE), lambda b, g, i, j=j: (j, b * G + g, 0, 0)) for j in range(2)]
    kv_specs = [pl.BlockSpec((1, S, LANE), lambda b, g, i, c=col(n): (b, 0, c + g)) for n in ("ks", "vs", "kw", "vw")]
    gl_spec = pl.BlockSpec((1, tq, LANE), lambda b, g, i: (b, i, col("ag")))
    cover_spec = pl.BlockSpec((NC, LANE), lambda b, g, i: (0, 0))
    e_spec = pl.BlockSpec((S // tq, LANE, tq), lambda b, g, i: (0, 0, 0))
    return pl.pallas_call(
        functools.partial(_nsa_kernel, tq=tq, n_cmp=n_cmp, n_sel=n_sel, n_top=n_top),
        grid=(B, G, S // tq),
        in_specs=[q_spec] + c_specs + kv_specs + [gl_spec, cover_spec, e_spec],
        out_specs=pl.BlockSpec((1, tq, LANE), lambda b, g, i: (b, i, g)),
        out_shape=jax.ShapeDtypeStruct((B, S, NSA_HEADS * HEAD_DIM), f32),
        compiler_params=pltpu.CompilerParams(dimension_semantics=("arbitrary", "arbitrary", "arbitrary"),
                                             vmem_limit_bytes=VMEM_LIMIT),
        name="nsa_attention",
    )(h, kvc, kvc, h, h, h, h, h, jnp.asarray(cover_p, MXU), jnp.asarray(expand, MXU))


def _log_sigmoid(z):
    return jnp.minimum(z, 0.0) - jnp.log1p(jnp.exp(-jnp.abs(z)))


def _split3(x):
    hi = x.astype(MXU)
    r = x - hi.astype(f32)
    mid = r.astype(MXU)
    return hi, mid, (r - mid.astype(f32)).astype(MXU)


def _sb_kernel(q_ref, k_ref, v_ref, u_ref, o_ref, *, tq):
    tk = tq
    qi = pl.program_id(2)
    lane = lax.broadcasted_iota(jnp.int32, (tq, LANE), 1)
    lo_half = lane < HEAD_DIM
    q = q_ref[0] * (HEAD_DIM ** -0.5)
    qh = (jnp.where(lo_half, q, 0.0).astype(MXU), jnp.where(lo_half, 0.0, q).astype(MXU))
    tpos = qi * tq + lax.broadcasted_iota(jnp.int32, (tq, 1), 0)
    u = u_ref[...]

    def body(i, carry):
        kt = qi - i
        k0 = pl.multiple_of(kt * tk, tk)
        k = k_ref[0, pl.ds(k0, tk), :].astype(MXU)
        v = v_ref[0, pl.ds(k0, tk), :].astype(MXU)
        strict = (k0 + lax.broadcasted_iota(jnp.int32, (1, tk), 1)) < tpos
        out = []
        for n in range(2):
            c, acc = carry[n]
            z = _dot_nt(qh[n], k)
            ls = _log_sigmoid(z)
            log_1m = jnp.where(strict, ls - z, 0.0)
            hi, mid, lo = _split3(log_1m)
            tail = (_dot(hi, u) + _dot(mid, u)) + _dot(lo, u) + c
            a = jnp.where(strict, jnp.exp(ls + tail), 0.0)
            acc = acc + _dot(a.astype(MXU), v)
            c = c + jnp.sum(log_1m, axis=1, keepdims=True)
            out.append((c, acc))
        return tuple(out)

    init = tuple((jnp.zeros((tq, 1), f32), jnp.zeros((tq, LANE), f32)) for _ in range(2))
    res = lax.fori_loop(0, qi + 1, body, init)
    o_ref[0] = jnp.where(lo_half, res[0][1], res[1][1])


def stick_breaking_attention(h, tq=256):
    B, S, _ = h.shape
    assert S % tq == 0
    tri = jnp.asarray(np.tril(np.ones((tq, tq), np.float32), -1), MXU)
    cq, ck, cv = (OFF[n] // LANE for n in ("bq", "bk", "bv"))
    return pl.pallas_call(
        functools.partial(_sb_kernel, tq=tq),
        grid=(B, SB_HEADS // 2, S // tq),
        in_specs=[pl.BlockSpec((1, tq, LANE), lambda b, p, i: (b, i, cq + p)),
                  pl.BlockSpec((1, S, LANE), lambda b, p, i: (b, 0, ck + p)),
                  pl.BlockSpec((1, S, LANE), lambda b, p, i: (b, 0, cv + p)),
                  pl.BlockSpec((tq, tq), lambda b, p, i: (0, 0))],
        out_specs=pl.BlockSpec((1, tq, LANE), lambda b, p, i: (b, i, p)),
        out_shape=jax.ShapeDtypeStruct((B, S, SB_HEADS * HEAD_DIM), f32),
        compiler_params=pltpu.CompilerParams(dimension_semantics=("arbitrary", "arbitrary", "arbitrary"),
                                             vmem_limit_bytes=VMEM_LIMIT),
        name="sb_attention",
    )(h, h, h, tri)


def _neg_expm1(y):
    series = -y * (1.0 + y * (1.0 / 2 + y * (1.0 / 6 + y * (1.0 / 24 + y * (1.0 / 120)))))
    return jnp.where(y > -0.1, series, 1.0 - jnp.exp(y))


def _rglru_kernel(x_ref, xg_ref, cw_ref, cb_ref, gaw_ref, gab_ref, gxw_ref, gxb_ref, lam_ref, o_ref):
    x = x_ref[0]
    S = x.shape[0]
    row = lax.broadcasted_iota(jnp.int32, (S, 1), 0)

    def shifted(t, d, fill):
        return jnp.where(row >= d, pltpu.roll(t, d, 0), fill)

    u = cb_ref[...] + x * cw_ref[CONV_W - 1:CONV_W, :]
    for d in range(1, CONV_W):
        u = u + shifted(x, d, 0.0) * cw_ref[CONV_W - 1 - d:CONV_W - d, :]
    ub = u.astype(MXU)
    r = jax.nn.sigmoid(_dot(ub, gaw_ref[...]) + gab_ref[...])
    i = jax.nn.sigmoid(_dot(ub, gxw_ref[...]) + gxb_ref[...])
    lam = lam_ref[...]
    softplus_neg = jnp.maximum(-lam, 0.0) + jnp.log1p(jnp.exp(-jnp.abs(lam)))
    log_a = -LRU_C * r * softplus_neg
    a = jnp.exp(log_a)
    b = jnp.sqrt(_neg_expm1(2.0 * log_a)) * (i * u)
    d = 1
    while d < S:
        b = a * shifted(b, d, 0.0) + b
        a = a * shifted(a, d, 1.0)
        d *= 2
    o_ref[0] = b * jax.nn.gelu(xg_ref[0])


def _block_diag(w):
    n, c, _ = w.shape
    out = jnp.zeros((n * c, n * c), w.dtype)
    for j in range(n):
        out = out.at[j * c:(j + 1) * c, j * c:(j + 1) * c].set(w[j])
    return out


def rglru_block(h, conv_w, conv_b, ga_w, ga_b, gx_w, gx_b, lru_lambda):
    B, S, _ = h.shape
    W = RNN_W
    cx, cg = OFF["cx"] // W, OFF["cg"] // W
    vec = pl.BlockSpec((1, W), lambda b: (0, 0))
    mat = pl.BlockSpec((W, W), lambda b: (0, 0))
    return pl.pallas_call(
        _rglru_kernel,
        grid=(B,),
        in_specs=[pl.BlockSpec((1, S, W), lambda b: (b, 0, cx)), pl.BlockSpec((1, S, W), lambda b: (b, 0, cg)),
                  pl.BlockSpec((CONV_W, W), lambda b: (0, 0)), vec, mat, vec, mat, vec, vec],
        out_specs=pl.BlockSpec((1, S, W), lambda b: (b, 0, 0)),
        out_shape=jax.ShapeDtypeStruct((B, S, W), f32),
        compiler_params=pltpu.CompilerParams(dimension_semantics=("arbitrary",), vmem_limit_bytes=VMEM_LIMIT),
        name="rglru",
    )(h, h, conv_w, conv_b.reshape(1, W), _block_diag(ga_w).astype(MXU), ga_b.reshape(1, W),
      _block_diag(gx_w).astype(MXU), gx_b.reshape(1, W), lru_lambda.reshape(1, W))


def _rot_half(t):
    d = t.shape[-1]
    return jnp.concatenate([-t[..., d // 2:], t[..., :d // 2]], axis=-1)


def _rms(x, g, width):
    return x * lax.rsqrt(jnp.sum(x * x, axis=-1, keepdims=True) * (1.0 / width) + RMS_EPS) * g


def _mla_prep_kernel(cq_ref, ckv_ref, kr_ref, krs_ref, gq_ref, gkv_ref, wq_ref, wqs_ref, wk_ref, wv_ref,
                     cosq_ref, sinq_ref, cosk_ref, sink_ref, q_ref, k_ref, v_ref):
    cq = _rms(cq_ref[0], gq_ref[...], MLA_Q_RANK).astype(MXU)
    ckv = _rms(ckv_ref[0], gkv_ref[...], MLA_KV_RANK).astype(MXU)
    scale = (MLA_NOPE + MLA_ROPE) ** -0.5
    q = _dot(cq, wq_ref[...]) * cosq_ref[...] + _dot(cq, wqs_ref[...]) * sinq_ref[...]
    q_ref[0] = (q * scale).astype(q_ref.dtype)
    k_rope = kr_ref[0] * cosk_ref[...] + krs_ref[0] * sink_ref[...]
    k = _dot(ckv, wk_ref[...])
    k_ref[0] = (k + jnp.concatenate([k_rope] * MLA_HEADS, axis=1)).astype(k_ref.dtype)
    v_ref[0] = _dot(ckv, wv_ref[...]).astype(v_ref.dtype)


def _mla_attn_kernel(q_ref, k_ref, v_ref, o_ref, *, tq):
    tk = tq
    qi = pl.program_id(2)
    tpos = qi * tq + lax.broadcasted_iota(jnp.int32, (tq, 1), 0)
    q = q_ref[0]

    def body(kt, carry):
        k0 = pl.multiple_of(kt * tk, tk)
        k = k_ref[0, pl.ds(k0, tk), :]
        v = v_ref[0, pl.ds(k0, tk), :]
        mask = (k0 + lax.broadcasted_iota(jnp.int32, (1, tk), 1)) <= tpos
        out = []
        for n in range(2):
            m, l, acc = carry[n]
            sm = jnp.where(mask, _dot_nt(q[:, n * LANE:(n + 1) * LANE], k[:, n * LANE:(n + 1) * LANE]), NEG)
            m_new = jnp.maximum(m, jnp.max(sm, axis=1, keepdims=True))
            alpha = jnp.exp(m - m_new)
            p = jnp.where(mask, jnp.exp(sm - m_new), 0.0)
            l = alpha * l + jnp.sum(p, axis=1, keepdims=True)
            acc = alpha * acc + _dot(p.astype(MXU), v)
            out.append((m_new, l, acc))
        return tuple(out)

    init = tuple((jnp.full((tq, 1), NEG, f32), jnp.zeros((tq, 1), f32), jnp.zeros((tq, LANE), f32))
                 for _ in range(2))
    res = lax.fori_loop(0, qi + 1, body, init)
    o = [acc / jnp.maximum(l, 1e-30) for (_, l, acc) in res]
    lane = lax.broadcasted_iota(jnp.int32, (tq, LANE), 1)
    o_ref[0] = jnp.where(lane < MLA_V, o[0], o[1])


def mla_attention(h, q_norm, kv_norm, w_uq, w_ukv, tr=512, tq=256):
    B, S, _ = h.shape
    H = MLA_HEADS
    dq = MLA_NOPE + MLA_ROPE
    HW = H * LANE
    wq3 = w_uq.reshape(MLA_Q_RANK, H, dq)
    wq_rot = jnp.concatenate([jnp.zeros_like(wq3[..., :MLA_NOPE]), _rot_half(wq3[..., MLA_NOPE:])], axis=-1)

    def pad_q(w3):
        w3 = jnp.pad(w3, ((0, 256 - MLA_Q_RANK), (0, 0), (0, LANE - dq)))
        return w3.reshape(256, HW).astype(MXU)

    wkv3 = w_ukv.reshape(MLA_KV_RANK, H, MLA_NOPE + MLA_V)
    wk = jnp.pad(wkv3[..., :MLA_NOPE], ((0, 0), (0, 0), (0, LANE - MLA_NOPE))).reshape(MLA_KV_RANK, HW).astype(MXU)
    wv = wkv3[..., MLA_NOPE:].reshape(MLA_KV_RANK, H * MLA_V).astype(MXU)
    gq = jnp.pad(q_norm, (0, 256 - MLA_Q_RANK)).reshape(1, 256)
    gkv = kv_norm.reshape(1, MLA_KV_RANK)
    inv = ROPE_THETA ** (-jnp.arange(0, MLA_ROPE, 2, dtype=f32) / MLA_ROPE)
    ang = jnp.arange(S, dtype=f32)[:, None] * inv[None, :]
    cos2 = jnp.concatenate([jnp.cos(ang), jnp.cos(ang)], axis=1)
    sin2 = jnp.concatenate([jnp.sin(ang), jnp.sin(ang)], axis=1)
    tail = LANE - dq
    cos_k = jnp.concatenate([jnp.zeros((S, MLA_NOPE), f32), cos2, jnp.zeros((S, tail), f32)], axis=1)
    sin_k = jnp.concatenate([jnp.zeros((S, MLA_NOPE), f32), sin2, jnp.zeros((S, tail), f32)], axis=1)
    cos_q = jnp.tile(jnp.concatenate([jnp.ones((S, MLA_NOPE), f32), cos2, jnp.zeros((S, tail), f32)], axis=1), (1, H))
    sin_q = jnp.tile(sin_k, (1, H))

    c_cq, c_ckv, c_kr, c_krs = OFF["dcq"] // 256, OFF["dckv"] // LANE, OFF["dkr"] // LANE, OFF["dkrs"] // LANE
    full = lambda shape: pl.BlockSpec(shape, lambda b, i: (0, 0))
    tab = lambda w: pl.BlockSpec((tr, w), lambda b, i: (i, 0))
    q, k, v = pl.pallas_call(
        _mla_prep_kernel,
        grid=(B, S // tr),
        in_specs=[pl.BlockSpec((1, tr, 256), lambda b, i: (b, i, c_cq)),
                  pl.BlockSpec((1, tr, LANE), lambda b, i: (b, i, c_ckv)),
                  pl.BlockSpec((1, tr, LANE), lambda b, i: (b, i, c_kr)),
                  pl.BlockSpec((1, tr, LANE), lambda b, i: (b, i, c_krs)),
                  full((1, 256)), full((1, MLA_KV_RANK)), full((256, HW)), full((256, HW)),
                  full((MLA_KV_RANK, HW)), full((MLA_KV_RANK, H * MLA_V)),
                  tab(HW), tab(HW), tab(LANE), tab(LANE)],
        out_specs=[pl.BlockSpec((1, tr, HW), lambda b, i: (b, i, 0)),
                   pl.BlockSpec((1, tr, HW), lambda b, i: (b, i, 0)),
                   pl.BlockSpec((1, tr, H * MLA_V), lambda b, i: (b, i, 0))],
        out_shape=[jax.ShapeDtypeStruct((B, S, HW), MXU), jax.ShapeDtypeStruct((B, S, HW), MXU),
                   jax.ShapeDtypeStruct((B, S, H * MLA_V), MXU)],
        compiler_params=pltpu.CompilerParams(dimension_semantics=("arbitrary", "arbitrary"),
                                             vmem_limit_bytes=VMEM_LIMIT),
        name="mla_prep",
    )(h, h, h, h, gq, gkv, pad_q(wq3), pad_q(wq_rot), wk, wv, cos_q, sin_q, cos_k, sin_k)
    return pl.pallas_call(
        functools.partial(_mla_attn_kernel, tq=tq),
        grid=(B, H // 2, S // tq),
        in_specs=[pl.BlockSpec((1, tq, 2 * LANE), lambda b, p, i: (b, i, p)),
                  pl.BlockSpec((1, S, 2 * LANE), lambda b, p, i: (b, 0, p)),
                  pl.BlockSpec((1, S, LANE), lambda b, p, i: (b, 0, p))],
        out_specs=pl.BlockSpec((1, tq, LANE), lambda b, p, i: (b, i, p)),
        out_shape=jax.ShapeDtypeStruct((B, S, H * MLA_V), f32),
        compiler_params=pltpu.CompilerParams(dimension_semantics=("arbitrary", "arbitrary", "arbitrary"),
                                             vmem_limit_bytes=VMEM_LIMIT),
        name="mla_attention",
    )(q, k, v)


def _ln(z, g, b):
    mu = jnp.mean(z, axis=-1, keepdims=True)
    zc = z - mu
    var = jnp.mean(zc * zc, axis=-1, keepdims=True)
    return zc * lax.rsqrt(var + LN_EPS) * g + b


def _merge_kernel(mg_ref, oa_ref, ob_ref, oc_ref, od_ref, wb_ref, wo_ref, x_ref, g_ref, b_ref, o_ref):
    acc = None
    for n, br in enumerate((oa_ref, ob_ref, oc_ref, od_ref)):
        up = _dot(br[...].astype(MXU), wb_ref[n])
        term = jax.nn.sigmoid(mg_ref[:, n * D_MODEL:(n + 1) * D_MODEL]) * up
        acc = term if acc is None else acc + term
    y = _dot(acc.astype(MXU), wo_ref[...])
    o_ref[...] = _ln(DN_ALPHA * x_ref[...] + y, g_ref[...], b_ref[...])


def merge_ln(h2, branches, w_branch, w_out, x2, g, b, tm=256):
    N, D = x2.shape
    assert OFF["mg"] == 0 and N % tm == 0
    row = lambda w: pl.BlockSpec((tm, w), lambda i: (i, 0))
    return pl.pallas_call(
        _merge_kernel,
        grid=(N // tm,),
        in_specs=[row(N_MIXERS * D)] + [row(MIX_W)] * N_MIXERS
        + [pl.BlockSpec((N_MIXERS, MIX_W, D), lambda i: (0, 0, 0)), pl.BlockSpec((D, D), lambda i: (0, 0)),
           row(D), pl.BlockSpec((1, D), lambda i: (0, 0)), pl.BlockSpec((1, D), lambda i: (0, 0))],
        out_specs=row(D),
        out_shape=jax.ShapeDtypeStruct((N, D), f32),
        compiler_params=pltpu.CompilerParams(dimension_semantics=("arbitrary",), vmem_limit_bytes=VMEM_LIMIT),
        name="merge_ln",
    )(h2, *branches, w_branch.astype(MXU), w_out.astype(MXU), x2, g.reshape(1, D), b.reshape(1, D))


def _xattn_kernel(x_ref, wq_ref, k_ref, v_ref, wo_ref, g_ref, b_ref, o_ref):
    x = x_ref[0]
    q = _dot(x.astype(MXU), wq_ref[...]).astype(MXU)
    k = k_ref[0]
    v = v_ref[0]
    heads = []
    for hd in range(X_HEADS):
        sl = slice(hd * X_HEAD_DIM, (hd + 1) * X_HEAD_DIM)
        s = _dot_nt(q[:, sl], k[:, sl]) * (X_HEAD_DIM ** -0.5)
        e = jnp.exp(s - jnp.max(s, axis=1, keepdims=True))
        p = e / jnp.sum(e, axis=1, keepdims=True)
        heads.append(_dot(p.astype(MXU), v[:, sl]).astype(MXU))
    y = _dot(jnp.concatenate(heads, axis=1), wo_ref[...])
    o_ref[0] = _ln(DN_ALPHA * x + y, g_ref[...], b_ref[...])


def cross_attention_ln(x, mem, wq, wkv, wo, g, b, tq=256):
    B, S, D = x.shape
    M = mem.shape[1]
    F = X_HEADS * X_HEAD_DIM
    kv = matmul(mem.reshape(B * M, D), wkv.astype(MXU), 512, 2 * F, out_dtype=MXU).reshape(B, M, 2 * F)
    full = lambda shape: pl.BlockSpec(shape, lambda bi, i: (0,) * len(shape))
    return pl.pallas_call(
        _xattn_kernel,
        grid=(B, S // tq),
        in_specs=[pl.BlockSpec((1, tq, D), lambda bi, i: (bi, i, 0)), full((D, F)),
                  pl.BlockSpec((1, M, F), lambda bi, i: (bi, 0, 0)), pl.BlockSpec((1, M, F), lambda bi, i: (bi, 0, 1)),
                  full((F, D)), full((1, D)), full((1, D))],
        out_specs=pl.BlockSpec((1, tq, D), lambda bi, i: (bi, i, 0)),
        out_shape=jax.ShapeDtypeStruct((B, S, D), f32),
        compiler_params=pltpu.CompilerParams(dimension_semantics=("arbitrary", "arbitrary"),
                                             vmem_limit_bytes=VMEM_LIMIT),
        name="cross_attention_ln",
    )(x, wq.astype(MXU), kv, kv, wo.astype(MXU), g.reshape(1, D), b.reshape(1, D))


_R_E0, _R_E1, _R_W0, _R_W1, _R_RANK0, _R_RANK1 = range(6)
_GRP_LANE0 = N_EXPERTS


def _router_kernel(x_ref, w_ref, b_ref, tri_ref, r_ref, cnt_ref):
    i = pl.program_id(0)
    tm = x_ref.shape[0]
    logits = _dot(x_ref[...].astype(MXU), w_ref[...]) + b_ref[...]
    lane = lax.broadcasted_iota(jnp.int32, (tm, LANE), 1)
    lane_f = lane.astype(f32)
    big = float(LANE)

    def rmax(t):
        return jnp.max(t, axis=1, keepdims=True)

    def first_lane(cond):
        return jnp.min(jnp.where(cond, lane_f, big), axis=1, keepdims=True)

    def softmax_on(mask):
        lm = jnp.where(mask, logits, NEG)
        e = jnp.where(mask, jnp.exp(lm - rmax(lm)), 0.0)
        return e / jnp.sum(e, axis=1, keepdims=True)

    is_g = (lane >= _GRP_LANE0) & (lane < _GRP_LANE0 + N_GROUPS)
    p_grp = softmax_on(is_g)
    p_g = rmax(p_grp)
    grp = first_lane(is_g & (p_grp == p_g)) - float(_GRP_LANE0)
    in_grp = (lane < N_EXPERTS) & (jnp.right_shift(lane, 3).astype(f32) == grp)
    p_e = softmax_on(in_grp)
    p1 = rmax(jnp.where(in_grp, p_e, -1.0))
    e1 = first_lane(in_grp & (p_e == p1))
    rest = in_grp & (lane_f != e1)
    p2 = rmax(jnp.where(rest, p_e, -1.0))
    e2 = first_lane(rest & (p_e == p2))
    w1 = p_g * p1 / (p1 + p2)
    w2 = p_g * p2 / (p1 + p2)

    @pl.when(i == 0)
    def _():
        cnt_ref[...] = jnp.zeros_like(cnt_ref)

    oh1 = lane_f == e1
    oh2 = lane_f == e2
    both = (oh1 | oh2).astype(MXU)
    before = _dot(tri_ref[...], both) + cnt_ref[0:1, :]
    rank1 = jnp.sum(jnp.where(oh1, before, 0.0), axis=1, keepdims=True)
    rank2 = jnp.sum(jnp.where(oh2, before, 0.0), axis=1, keepdims=True)
    cnt_ref[...] = cnt_ref[...] + jnp.sum(both.astype(f32), axis=0, keepdims=True)

    out = jnp.zeros((tm, LANE), f32)
    for slot, val in ((_R_E0, e1), (_R_E1, e2), (_R_W0, w1), (_R_W1, w2), (_R_RANK0, rank1), (_R_RANK1, rank2)):
        out = jnp.where(lane == slot, val, out)
    r_ref[...] = out


def moe_router(x2, rg_w, rg_b, re_w, re_b, tm=256):
    N, D = x2.shape
    assert EXPERTS_PER_GROUP == 8 and N_EXPERTS + N_GROUPS <= LANE
    w = jnp.pad(jnp.concatenate([re_w, rg_w], axis=1), ((0, 0), (0, LANE - N_EXPERTS - N_GROUPS))).astype(MXU)
    b = jnp.pad(jnp.concatenate([re_b, rg_b]), (0, LANE - N_EXPERTS - N_GROUPS)).reshape(1, LANE)
    tri = jnp.asarray(np.tril(np.ones((tm, tm), np.float32), -1), MXU)
    return pl.pallas_call(
        _router_kernel,
        grid=(N // tm,),
        in_specs=[pl.BlockSpec((tm, D), lambda i: (i, 0)), pl.BlockSpec((D, LANE), lambda i: (0, 0)),
                  pl.BlockSpec((1, LANE), lambda i: (0, 0)), pl.BlockSpec((tm, tm), lambda i: (0, 0))],
        out_specs=[pl.BlockSpec((tm, LANE), lambda i: (i, 0)), pl.BlockSpec((8, LANE), lambda i: (0, 0))],
        out_shape=[jax.ShapeDtypeStruct((N, LANE), f32), jax.ShapeDtypeStruct((8, LANE), f32)],
        compiler_params=pltpu.CompilerParams(dimension_semantics=("arbitrary",), vmem_limit_bytes=VMEM_LIMIT),
        name="moe_router",
    )(x2, w, b, tri)


def _ffn_kernel(ce_ref, nu_ref, x_ref, wgu_ref, wd_ref, o_ref):
    c = pl.program_id(0)

    @pl.when(c < nu_ref[0])
    def _():
        gu = _dot(x_ref[...].astype(MXU), wgu_ref[0].astype(MXU))
        hid = jax.nn.silu(gu[:, :D_EXPERT]) * gu[:, D_EXPERT:]
        o_ref[...] = _dot(hid.astype(MXU), wd_ref[0].astype(MXU))

    @pl.when(c >= nu_ref[0])
    def _():
        o_ref[...] = jnp.zeros_like(o_ref)


def expert_ffn(xb, chunk_e, n_used, w_gu, w_down):
    P, D = xb.shape
    C = EXPERT_CHUNK
    grid_spec = pltpu.PrefetchScalarGridSpec(
        num_scalar_prefetch=2,
        grid=(P // C,),
        in_specs=[pl.BlockSpec((C, D), lambda c, ce, nu: (c, 0)),
                  pl.BlockSpec((1, D, 2 * D_EXPERT), lambda c, ce, nu: (ce[c], 0, 0)),
                  pl.BlockSpec((1, D_EXPERT, D), lambda c, ce, nu: (ce[c], 0, 0))],
        out_specs=pl.BlockSpec((C, D), lambda c, ce, nu: (c, 0)),
    )
    return pl.pallas_call(
        _ffn_kernel,
        grid_spec=grid_spec,
        out_shape=jax.ShapeDtypeStruct((P, D), f32),
        compiler_params=pltpu.CompilerParams(dimension_semantics=("arbitrary",), vmem_limit_bytes=VMEM_LIMIT),
        name="expert_ffn",
    )(chunk_e, n_used, xb, w_gu, w_down)


def _combine_kernel(x_ref, y0_ref, y1_ref, r_ref, g_ref, b_ref, o_ref):
    r = r_ref[...]
    lane = lax.broadcasted_iota(jnp.int32, r.shape, 1)
    w0 = jnp.sum(jnp.where(lane == _R_W0, r, 0.0), axis=1, keepdims=True)
    w1 = jnp.sum(jnp.where(lane == _R_W1, r, 0.0), axis=1, keepdims=True)
    y = y0_ref[...] * w0 + y1_ref[...] * w1
    o_ref[...] = _ln(DN_ALPHA * x_ref[...] + y, g_ref[...], b_ref[...])


def combine_ln(x2, y0, y1, r, g, b, tm=512):
    N, D = x2.shape
    row = lambda w: pl.BlockSpec((tm, w), lambda i: (i, 0))
    vec = pl.BlockSpec((1, D), lambda i: (0, 0))
    return pl.pallas_call(
        _combine_kernel,
        grid=(N // tm,),
        in_specs=[row(D), row(D), row(D), row(LANE), vec, vec],
        out_specs=row(D),
        out_shape=jax.ShapeDtypeStruct((N, D), f32),
        compiler_params=pltpu.CompilerParams(dimension_semantics=("arbitrary",), vmem_limit_bytes=VMEM_LIMIT),
        name="moe_combine_ln",
    )(x2, y0, y1, r, g.reshape(1, D), b.reshape(1, D))


def hier_moe_ln(x2, rg_w, rg_b, re_w, re_b, w_gu, w_down, g, b):
    N, D = x2.shape
    E, C, K = N_EXPERTS, EXPERT_CHUNK, TOPK_IN_GROUP
    A = N * K
    r, cnt = moe_router(x2, rg_w, rg_b, re_w, re_b)
    e = r[:, _R_E0:_R_E1 + 1].astype(jnp.int32)
    rank = r[:, _R_RANK0:_R_RANK1 + 1].astype(jnp.int32)
    counts = cnt[0, :E].astype(jnp.int32)
    padded = (counts + C - 1) // C * C
    pad_end = jnp.cumsum(padded)
    pad_start = pad_end - padded
    dest = pad_start[e] + rank
    n_chunks = -(-(A + E * (C - 1)) // C)
    P = n_chunks * C
    chunk_start = jnp.arange(n_chunks, dtype=jnp.int32) * C
    chunk_e = jnp.minimum(jnp.sum((pad_end[None, :] <= chunk_start[:, None]).astype(jnp.int32), axis=1), E - 1)
    n_used = (pad_end[-1] // C).reshape(1).astype(jnp.int32)
    slot_tok = jnp.full((P,), N, jnp.int32).at[dest.reshape(A)].set(jnp.arange(A, dtype=jnp.int32) // K)
    xb = jnp.take(x2, slot_tok, axis=0, mode="fill", fill_value=0.0)
    yb = expert_ffn(xb, chunk_e.astype(jnp.int32), n_used, w_gu, w_down)
    return combine_ln(x2, yb[dest[:, 0]], yb[dest[:, 1]], r, g, b)


def layer_norm(x, g, b):
    mu = jnp.mean(x, -1, keepdims=True)
    var = jnp.mean(jnp.square(x - mu), -1, keepdims=True)
    return (x - mu) * lax.rsqrt(var + LN_EPS) * g + b


def rms_norm(x, g):
    return x * lax.rsqrt(jnp.mean(jnp.square(x), -1, keepdims=True) + RMS_EPS) * g


def masked_softmax(s, mask):
    s = jnp.where(mask, s.astype(f32), -jnp.inf)
    m = jnp.max(s, -1, keepdims=True)
    p = jnp.exp(s - jnp.where(jnp.isfinite(m), m, 0.0))
    return p / jnp.maximum(jnp.sum(p, -1, keepdims=True), 1e-30)


def rope(x, pos):
    d = x.shape[-1]
    inv = ROPE_THETA ** (-jnp.arange(0, d, 2, dtype=f32) / d)
    ang = pos.astype(f32)[:, None] * inv[None, :]
    cos, sin = jnp.cos(ang)[:, None, :], jnp.sin(ang)[:, None, :]
    x1, x2 = x[..., : d // 2], x[..., d // 2:]
    return jnp.concatenate([x1 * cos - x2 * sin, x1 * sin + x2 * cos], -1)


def q_blocks(t):
    b, s = t.shape[:2]
    return jnp.moveaxis(t.reshape(b, s // Q_BLOCK, Q_BLOCK, *t.shape[2:]), 1, 0)


def unblock(t):
    nq, b, q = t.shape[:3]
    return jnp.moveaxis(t, 0, 1).reshape(b, nq * q, *t.shape[3:])


def hybrid_mixer(x, pos, w_in, cmp_pos, cmp_w1, cmp_w2, conv_w, conv_b, ga_w, ga_b, gx_w, gx_b, lru_lambda,
                 q_norm, kv_norm, w_uq, w_ukv, w_branch, w_out):
    B, S, D = x.shape
    h = matmul(x.reshape(B * S, D), _cat_w_in(w_in), 512, _IN_TILE).reshape(B, S, N_CAT)

    def sec(name, w):
        return h[..., OFF[name]:OFF[name] + w]

    def heads(t):
        return t.reshape(B, S, -1, HEAD_DIM)

    o_a = nsa_attention(h, cmp_pos, cmp_w1, cmp_w2)
    o_b = stick_breaking_attention(h)
    o_c = rglru_block(h, conv_w, conv_b, ga_w, ga_b, gx_w, gx_b, lru_lambda)
    o_d = mla_attention(h, q_norm, kv_norm, w_uq, w_ukv)
    branches = jnp.stack([o_a, o_b, o_c, o_d], axis=2)
    up = jnp.einsum('bsnc,ncd->bsnd', branches, w_branch)
    gates = jax.nn.sigmoid(sec("mg", N_MIXERS * D).reshape(B, S, N_MIXERS, D))
    return jnp.einsum('bsd,de->bse', jnp.sum(gates * up, axis=2), w_out)


def memory_cross_attention(x, mem, wq, wkv, wo):
    B, S, _ = x.shape
    M = mem.shape[1]
    F = X_HEADS * X_HEAD_DIM
    q = jnp.einsum('bsd,df->bsf', x, wq).reshape(B, S, X_HEADS, X_HEAD_DIM)
    kv = jnp.einsum('bmd,df->bmf', mem, wkv)
    k = kv[..., :F].reshape(B, M, X_HEADS, X_HEAD_DIM)
    v = kv[..., F:].reshape(B, M, X_HEADS, X_HEAD_DIM)
    s = jnp.einsum('bshd,bmhd->bhsm', q, k).astype(f32) * X_HEAD_DIM ** -0.5
    p = jax.nn.softmax(s, axis=-1)
    o = jnp.einsum('bhsm,bmhd->bshd', p, v).reshape(B, S, F)
    return jnp.einsum('bsf,fd->bsd', o, wo)


def grouped_expert_ffn(xt, expert_idx, weights, w_gu, w_down):
    N, D = xt.shape
    K = expert_idx.shape[1]
    E = w_gu.shape[0]
    F = w_down.shape[1]
    C = EXPERT_CHUNK
    A = N * K
    flat_e = expert_idx.reshape(A)
    flat_tok = jnp.arange(A, dtype=jnp.int32) // K
    order = jnp.argsort(flat_e)
    se, st = flat_e[order], flat_tok[order]
    counts = jax.ops.segment_sum(jnp.ones((A,), jnp.int32), flat_e, num_segments=E)
    padded = (counts + C - 1) // C * C
    pad_end = jnp.cumsum(padded)
    pad_start = pad_end - padded
    start = jnp.cumsum(counts) - counts
    dest = pad_start[se] + jnp.arange(A, dtype=jnp.int32) - start[se]
    n_chunks = -(-(A + E * (C - 1)) // C)
    P = n_chunks * C
    slot_tok = jnp.full((P,), N, jnp.int32).at[dest].set(st)
    x_pad = jnp.concatenate([xt, jnp.zeros((1, D), xt.dtype)], axis=0)
    xb = x_pad[slot_tok].reshape(n_chunks, C, D)
    chunk_start = jnp.arange(n_chunks, dtype=jnp.int32) * C
    chunk_e = jnp.minimum(jnp.sum((pad_end[None, :] <= chunk_start[:, None]).astype(jnp.int32), axis=1), E - 1)

    def run(args):
        xc, e = args
        gu = xc @ w_gu[e]
        return (jax.nn.silu(gu[:, :F]) * gu[:, F:]) @ w_down[e]

    yb = lax.map(run, (xb, chunk_e)).reshape(P, D)
    contrib = yb[dest] * weights.reshape(A)[order][:, None]
    return jnp.zeros((N, D), yb.dtype).at[st].add(contrib)


def hier_moe(x, rg_w, rg_b, re_w, re_b, w_gu, w_down):
    B, S, D = x.shape
    xt = x.reshape(B * S, D)
    N = xt.shape[0]
    rows = jnp.arange(N)
    p_grp = jax.nn.softmax((xt @ rg_w).astype(f32) + rg_b, axis=-1)
    grp = jnp.argmax(p_grp, axis=-1).astype(jnp.int32)
    p_g = p_grp[rows, grp]
    e_logits = ((xt @ re_w).astype(f32) + re_b).reshape(N, N_GROUPS, EXPERTS_PER_GROUP)
    p_e = jax.nn.softmax(e_logits[rows, grp], axis=-1)
    top_p, top_i = lax.top_k(p_e, TOPK_IN_GROUP)
    w = p_g[:, None] * top_p / jnp.sum(top_p, -1, keepdims=True)
    expert_idx = (grp[:, None] * EXPERTS_PER_GROUP + top_i).astype(jnp.int32)
    return grouped_expert_ffn(xt, expert_idx, w, w_gu, w_down).reshape(B, S, D)


def kernel(x, mem, w_in, nsa_cmp_pos, nsa_cmp_w1, nsa_cmp_w2, rnn_conv_w, rnn_conv_b, rnn_ga_w, rnn_ga_b,
           rnn_gx_w, rnn_gx_b, rnn_lambda, mla_q_norm, mla_kv_norm, mla_w_uq, mla_w_ukv, w_branch, w_out,
           ln1_g, ln1_b, x_wq, x_wkv, x_wo, ln2_g, ln2_b, moe_rg_w, moe_rg_b, moe_re_w, moe_re_b,
           moe_w_gu, moe_w_down, ln3_g, ln3_b):
    B, S, D = x.shape
    N = B * S
    x2 = x.reshape(N, D)
    for l in range(DEPTH):
        h2 = matmul(x2, _cat_w_in(w_in[l]), 512, _IN_TILE)
        h = h2.reshape(B, S, N_CAT)
        o_a = nsa_attention(h, nsa_cmp_pos[l], nsa_cmp_w1[l], nsa_cmp_w2[l])
        o_b = stick_breaking_attention(h)
        o_c = rglru_block(h, rnn_conv_w[l], rnn_conv_b[l], rnn_ga_w[l], rnn_ga_b[l], rnn_gx_w[l], rnn_gx_b[l],
                          rnn_lambda[l])
        o_d = mla_attention(h, mla_q_norm[l], mla_kv_norm[l], mla_w_uq[l], mla_w_ukv[l])
        branches = [o.reshape(N, MIX_W) for o in (o_a, o_b, o_c, o_d)]
        x2 = merge_ln(h2, branches, w_branch[l], w_out[l], x2, ln1_g[l], ln1_b[l])
        x2 = cross_attention_ln(x2.reshape(B, S, D), mem, x_wq[l], x_wkv[l], x_wo[l], ln2_g[l], ln2_b[l]).reshape(N, D)
        x2 = hier_moe_ln(x2, moe_rg_w[l], moe_rg_b[l], moe_re_w[l], moe_re_b[l], moe_w_gu[l], moe_w_down[l],
                         ln3_g[l], ln3_b[l])
    return x2.reshape(B, S, D)
```

```python
import functools

import numpy as np
import jax
import jax.numpy as jnp
from jax import lax
from jax.experimental import pallas as pl
from jax.experimental.pallas import tpu as pltpu

D_MODEL = 1024
DEPTH = 4
HEAD_DIM = 64
N_MIXERS = 4
MIX_W = 256
NSA_HEADS = 4
NSA_KV_HEADS = 2
NSA_GROUP = NSA_HEADS // NSA_KV_HEADS
CMP_LEN = 32
CMP_STRIDE = 16
CMP_HID = 256
SEL_BLOCK = 64
SEL_TOPK = 8
WINDOW = 512
FORCE_SCORE = 1e4
SB_HEADS = 4
RNN_W = 256
CONV_W = 4
LRU_C = 8.0
MLA_HEADS = 4
MLA_Q_RANK = 192
MLA_KV_RANK = 128
MLA_NOPE = 64
MLA_ROPE = 32
MLA_V = 64
ROPE_THETA = 10000.0
X_HEADS = 4
X_HEAD_DIM = 128
N_GROUPS = 4
EXPERTS_PER_GROUP = 8
N_EXPERTS = N_GROUPS * EXPERTS_PER_GROUP
TOPK_IN_GROUP = 2
D_EXPERT = 512
EXPERT_CHUNK = 256
DN_ALPHA = (2.0 * DEPTH) ** 0.25
LN_EPS = 1e-5
RMS_EPS = 1e-6

IN_SPLITS = ((NSA_HEADS * HEAD_DIM,) + (NSA_KV_HEADS * HEAD_DIM,) * 6 + (NSA_HEADS * 3,)
             + (SB_HEADS * HEAD_DIM,) * 3
             + (RNN_W, RNN_W)
             + (MLA_Q_RANK, MLA_KV_RANK, MLA_ROPE)
             + (N_MIXERS * D_MODEL,))
IN_OFFSETS = tuple(int(o) for o in np.concatenate([[0], np.cumsum(IN_SPLITS)[:-1]]))

LANE = 128
VMEM_LIMIT = 48 * 1024 * 1024
NEG = -1e30
BIG_NEG = -2.0 ** 100

f32 = jnp.float32
MXU = jnp.bfloat16

_SECTIONS = (("mg", 4096), ("aq", 256), ("ks", 256), ("vs", 256), ("kw", 256), ("vw", 256), ("bq", 256),
             ("bk", 256), ("bv", 256), ("cx", 256), ("cg", 256), ("dcq", 256), ("kc", 128), ("vc", 128),
             ("ag", 128), ("dckv", 128), ("dkr", 128), ("dkrs", 128))
_IN_TILE = 3840
OFF = {}
_o = 0
for _n, _w in _SECTIONS:
    OFF[_n] = _o
    _o += _w
N_CAT = -(-_o // _IN_TILE) * _IN_TILE

_AUG_SEL = HEAD_DIM
_AUG_POS = HEAD_DIM + 32
_POS_SPLIT = 256


def _rot_half(t):
    d = t.shape[-1]
    return jnp.concatenate([-t[..., d // 2:], t[..., :d // 2]], axis=-1)


def _cat_w_in(w):
    def sec(i):
        return w[:, IN_OFFSETS[i]:IN_OFFSETS[i] + IN_SPLITS[i]]

    def spread(t):
        z = jnp.zeros_like(t[:, :HEAD_DIM])
        return jnp.concatenate([t[:, :HEAD_DIM], z, t[:, HEAD_DIM:], z], axis=1)

    def padc(t, n):
        return jnp.pad(t, ((0, 0), (0, n - t.shape[1])))

    def rope_slot(t):
        return jnp.pad(t, ((0, 0), (MLA_NOPE, LANE - MLA_NOPE - MLA_ROPE)))

    parts = {"mg": sec(16), "aq": sec(0), "kc": sec(1), "vc": sec(2), "ks": spread(sec(3)), "vs": spread(sec(4)),
             "kw": spread(sec(5)), "vw": spread(sec(6)), "ag": padc(sec(7), 128), "bq": sec(8), "bk": sec(9),
             "bv": sec(10), "cx": sec(11), "cg": sec(12), "dcq": padc(sec(13), 256), "dckv": sec(14),
             "dkr": rope_slot(sec(15)), "dkrs": rope_slot(_rot_half(sec(15)))}
    cat = jnp.concatenate([parts[n] for n, _ in _SECTIONS], axis=1)
    return padc(cat, N_CAT).astype(MXU)


def _dot(a, b):
    return jnp.dot(a, b, preferred_element_type=f32)


def _dot_nt(a, b):
    return lax.dot_general(a, b, (((1,), (1,)), ((), ())), preferred_element_type=f32)


def _params(n_axes):
    return pltpu.CompilerParams(dimension_semantics=("arbitrary",) * n_axes, vmem_limit_bytes=VMEM_LIMIT)


def _mm_kernel(a_ref, b_ref, o_ref):
    o_ref[...] = _dot(a_ref[...].astype(MXU), b_ref[...]).astype(o_ref.dtype)


def matmul(a, b, tm, tn, out_dtype=f32):
    M, K = a.shape
    _, N = b.shape
    assert M % tm == 0 and N % tn == 0
    return pl.pallas_call(
        _mm_kernel,
        grid=(N // tn, M // tm),
        in_specs=[pl.BlockSpec((tm, K), lambda j, i: (i, 0)),
                  pl.BlockSpec((K, tn), lambda j, i: (0, j))],
        out_specs=pl.BlockSpec((tm, tn), lambda j, i: (i, j)),
        out_shape=jax.ShapeDtypeStruct((M, N), out_dtype),
        compiler_params=_params(2),
        name="matmul",
    )(a, b)


def _cmp_kernel(t_ref, w1_ref, pos_ref, w2_ref, o_ref):
    half = CMP_STRIDE * HEAD_DIM
    t = t_ref[0, 0].astype(MXU)
    y1 = _dot(t, w1_ref[0, :half, :])
    y2 = _dot(t, w1_ref[0, half:, :])
    pos = jnp.broadcast_to(pos_ref[0], (8, 2 * half)).astype(MXU)
    pc = _dot(pos, w1_ref[0])[0:1]
    nc = y2.shape[0]
    hid = y1 + pltpu.roll(y2, nc - 1, 0) + pc
    o_ref[0, 0] = _dot(jax.nn.gelu(hid).astype(MXU), w2_ref[0])


def nsa_compress(t, w1, pos, w2):
    _, BG, NC, F = t.shape
    return pl.pallas_call(
        _cmp_kernel,
        grid=(2, BG),
        in_specs=[pl.BlockSpec((1, 1, NC, F), lambda j, i: (j, i, 0, 0)),
                  pl.BlockSpec((1, 2 * F, CMP_HID), lambda j, i: (j, 0, 0)),
                  pl.BlockSpec((1, 1, 2 * F), lambda j, i: (j, 0, 0)),
                  pl.BlockSpec((1, CMP_HID, LANE), lambda j, i: (j, 0, 0))],
        out_specs=pl.BlockSpec((1, 1, NC, LANE), lambda j, i: (j, i, 0, 0)),
        out_shape=jax.ShapeDtypeStruct((2, BG, NC, LANE), f32),
        compiler_params=_params(2),
        name="nsa_compress",
    )(t, w1, pos, w2)


def _nsa_kernel(q_ref, kc_ref, vc_ref, ks_ref, vs_ref, kw_ref, vw_ref, gl_ref, cover_ref, aug_ref, cb_ref, wb_ref,
                o_ref, ksa, vsa, kwa, vwa, *, tq, n_cmp, n_sel, n_top):
    tk = tq
    g = pl.program_id(1)
    qi = pl.program_id(2)
    q0 = pl.multiple_of(qi * tq, tq)
    lane = lax.broadcasted_iota(jnp.int32, (tq, LANE), 1)
    lo_half = lane < HEAD_DIM

    @pl.when(qi == 0)
    def _():
        real = lax.broadcasted_iota(jnp.int32, ksa.shape, 1) < HEAD_DIM
        aug = aug_ref[...]
        ones = jnp.ones(ksa.shape, MXU)
        ksa[...] = jnp.where(real, ks_ref[0].astype(MXU), aug)
        kwa[...] = jnp.where(real, kw_ref[0].astype(MXU), aug)
        vsa[...] = jnp.where(real, vs_ref[0].astype(MXU), ones)
        vwa[...] = jnp.where(real, vw_ref[0].astype(MXU), ones)

    q = q_ref[0] * (HEAD_DIM ** -0.5)
    q_heads = (jnp.where(lo_half, q, 0.0), jnp.where(lo_half, pltpu.roll(q, HEAD_DIM, 1), 0.0))
    alibi = [2.0 ** (-8.0 * (h + 1) / NSA_HEADS) for h in range(NSA_HEADS)]
    slopes = [jnp.where(g == 0, alibi[n], alibi[NSA_GROUP + n]) for n in range(NSA_GROUP)]
    pos_cols = [jnp.where(lane == _AUG_POS, slopes[n] * _POS_SPLIT, jnp.where(lane == _AUG_POS + 1, slopes[n], 0.0))
                for n in range(NSA_GROUP)]
    tpos = q0 + lax.broadcasted_iota(jnp.int32, (tq, 1), 0)

    nc = kc_ref.shape[2]
    cidx = lax.broadcasted_iota(jnp.int32, (1, nc), 1)
    dist_c = tpos - (cidx * CMP_STRIDE + (CMP_LEN - 1))
    mask_c = (dist_c >= 0) & (cidx < n_cmp)
    dist_cf = dist_c.astype(f32)
    kc = kc_ref[0, 0].astype(MXU)
    vc = vc_ref[0, 0].astype(MXU)
    o_cmp = []
    imp_t = jnp.zeros((LANE, tq), f32)
    for n in range(NSA_GROUP):
        s = _dot_nt(q_heads[n].astype(MXU), kc) - slopes[n] * dist_cf
        sm = jnp.where(mask_c, s, NEG)
        m = jnp.max(sm, axis=1, keepdims=True)
        p = jnp.where(mask_c, jnp.exp(sm - m), 0.0)
        p = (p / jnp.maximum(jnp.sum(p, axis=1, keepdims=True), 1e-30)).astype(MXU)
        o_cmp.append(_dot(p, vc))
        imp_t = imp_t + _dot_nt(cover_ref[...], p)

    rows = 32
    imp = imp_t[:rows]
    blk = lax.broadcasted_iota(jnp.int32, (rows, tq), 0)
    blk_f = blk.astype(f32)
    tpos_t = q0 + lax.broadcasted_iota(jnp.int32, (1, tq), 1)
    forced = (blk == 0) | (blk == jnp.right_shift(tpos_t, 6))
    valid = blk * SEL_BLOCK <= tpos_t
    imp = jnp.where(forced, FORCE_SCORE, jnp.where(valid, imp, -1.0))
    imp = jnp.where(blk < n_sel, imp, NEG)
    sel_t = jnp.zeros((rows, tq), f32)
    for _ in range(n_top):
        m = jnp.max(imp, axis=0, keepdims=True)
        first = jnp.min(jnp.where(imp == m, blk_f, float(LANE)), axis=0, keepdims=True)
        pick = blk_f == first
        sel_t = jnp.where(pick, 1.0, sel_t)
        imp = jnp.where(pick, 2 * NEG, imp)
    sel = jnp.concatenate([sel_t, jnp.zeros((LANE - rows, tq), f32)], axis=0).T
    sel_bias = pltpu.roll(jnp.where(sel > 0.5, 0.0, BIG_NEG), _AUG_SEL, 1)
    sel_cols = jnp.where((lane >= _AUG_SEL) & (lane < _AUG_SEL + rows), sel_bias, 0.0)
    q_sel = [(q_heads[n] + sel_cols + pos_cols[n]).astype(MXU) for n in range(NSA_GROUP)]
    q_win = [(q_heads[n] + pos_cols[n]).astype(MXU) for n in range(NSA_GROUP)]

    def update(carry, qn, k, v, bias):
        m, acc = carry
        s = _dot_nt(qn, k)
        if bias is not None:
            s = s + bias
        m_new = jnp.maximum(m, jnp.max(s, axis=1, keepdims=True))
        p = jnp.exp(s - m_new)
        return m_new, jnp.exp(m - m_new) * acc + _dot(p.astype(MXU), v)

    def finish(carry):
        _, acc = carry
        denom = jnp.sum(jnp.where(lane == HEAD_DIM, acc, 0.0), axis=1, keepdims=True)
        return acc / jnp.maximum(denom, 1e-30)

    init = tuple((jnp.full((tq, 1), NEG, f32), jnp.zeros((tq, LANE), f32)) for _ in range(NSA_GROUP))
    causal = cb_ref[...]

    def sel_body(kt, carry):
        k0 = pl.multiple_of(kt * tk, tk)
        k = ksa[pl.ds(k0, tk), :]
        v = vsa[pl.ds(k0, tk), :]
        return tuple(update(carry[n], q_sel[n], k, v, None) for n in range(NSA_GROUP))

    carry = lax.fori_loop(0, qi, sel_body, init)
    k = ksa[pl.ds(q0, tk), :]
    v = vsa[pl.ds(q0, tk), :]
    o_sel = [finish(update(carry[n], q_sel[n], k, v, causal)) for n in range(NSA_GROUP)]

    carry = init
    for back, pattern in ((2, wb_ref[...]), (1, None), (0, causal)):
        k0 = pl.multiple_of(jnp.maximum(qi - back, 0) * tk, tk)
        k = kwa[pl.ds(k0, tk), :]
        v = vwa[pl.ds(k0, tk), :]
        if back:
            off = jnp.where(qi >= back, 0.0, BIG_NEG)
            bias = off if pattern is None else pattern + off
        else:
            bias = pattern
        carry = tuple(update(carry[n], q_win[n], k, v, bias) for n in range(NSA_GROUP))
    o_win = [finish(c) for c in carry]

    sig = jax.nn.sigmoid(gl_ref[0])

    def gate(n, j):
        col = 3 * (NSA_GROUP * g + n) + j
        return jnp.sum(jnp.where(lane == col, sig, 0.0), axis=1, keepdims=True)

    o = [gate(n, 0) * o_cmp[n] + gate(n, 1) * o_sel[n] + gate(n, 2) * o_win[n] for n in range(NSA_GROUP)]
    o_ref[0] = jnp.where(lo_half, o[0], pltpu.roll(o[1], HEAD_DIM, 1))


def nsa_attention(h, cmp_pos, cmp_w1, cmp_w2, tq=256):
    B, S, _ = h.shape
    G = NSA_KV_HEADS
    NC = S // CMP_STRIDE
    n_cmp = (S - CMP_LEN) // CMP_STRIDE + 1
    n_sel = S // SEL_BLOCK
    n_top = min(SEL_TOPK, n_sel)
    F = CMP_STRIDE * HEAD_DIM
    assert S % tq == 0 and WINDOW == 2 * tq and tq % SEL_BLOCK == 0 and n_sel <= 32 and tq == _POS_SPLIT

    def chunks(off):
        t = h[:, :, off:off + G * HEAD_DIM].reshape(B, NC, CMP_STRIDE, G, HEAD_DIM)
        return t.transpose(0, 3, 1, 2, 4).reshape(B * G, NC, F)

    t = jnp.stack([chunks(OFF["kc"]), chunks(OFF["vc"])])
    w2 = jnp.concatenate([cmp_w2, cmp_w2], axis=-1).astype(MXU)
    kvc = nsa_compress(t, cmp_w1.astype(MXU), cmp_pos.reshape(2, 1, 2 * F), w2)

    c0 = np.arange(n_cmp)[:, None] * CMP_STRIDE
    j0 = np.arange(n_sel)[None, :] * SEL_BLOCK
    cover = np.clip(np.minimum(c0 + CMP_LEN, j0 + SEL_BLOCK) - np.maximum(c0, j0), 0, None) / CMP_LEN
    cover_t = np.zeros((LANE, NC), np.float32)
    cover_t[:n_sel, :n_cmp] = cover.T
    pos = np.arange(S)
    aug = np.zeros((S, LANE), np.float32)
    aug[pos, _AUG_SEL + pos // SEL_BLOCK] = 1.0
    aug[:, _AUG_POS] = pos // _POS_SPLIT
    aug[:, _AUG_POS + 1] = pos % _POS_SPLIT
    rel = np.arange(tq)[:, None] - np.arange(tq)[None, :]
    causal = np.where(rel >= 0, 0.0, BIG_NEG).astype(np.float32)
    win_lo = np.where(rel < 0, 0.0, BIG_NEG).astype(np.float32)

    col = lambda name: OFF[name] // LANE
    full2 = lambda shape: pl.BlockSpec(shape, lambda b, g, i: (0, 0))
    q_spec = pl.BlockSpec((1, tq, LANE), lambda b, g, i: (b, i, col("aq") + g))
    c_specs = [pl.BlockSpec((1, 1, NC, LANE), lambda b, g, i, j=j: (j, b * G + g, 0, 0)) for j in range(2)]
    kv_specs = [pl.BlockSpec((1, S, LANE), lambda b, g, i, c=col(n): (b, 0, c + g))
                for n in ("ks", "vs", "kw", "vw")]
    gl_spec = pl.BlockSpec((1, tq, LANE), lambda b, g, i: (b, i, col("ag")))
    return pl.pallas_call(
        functools.partial(_nsa_kernel, tq=tq, n_cmp=n_cmp, n_sel=n_sel, n_top=n_top),
        grid=(B, G, S // tq),
        in_specs=[q_spec] + c_specs + kv_specs + [gl_spec, full2((LANE, NC)), full2((S, LANE)), full2((tq, tq)),
                                                  full2((tq, tq))],
        out_specs=pl.BlockSpec((1, tq, LANE), lambda b, g, i: (b, i, g)),
        out_shape=jax.ShapeDtypeStruct((B, S, NSA_HEADS * HEAD_DIM), f32),
        scratch_shapes=[pltpu.VMEM((S, LANE), MXU)] * 4,
        compiler_params=_params(3),
        name="nsa_attention",
    )(h, kvc, kvc, h, h, h, h, h, jnp.asarray(cover_t, MXU), jnp.asarray(aug, MXU), jnp.asarray(causal),
      jnp.asarray(win_lo))


_SB_DEAD = -104.0


def _log_sigmoid(z):
    return jnp.minimum(z, 0.0) - jnp.log1p(jnp.exp(-jnp.abs(z)))


def _sb_kernel(q_ref, k_ref, v_ref, u_ref, o_ref, kb, vb, *, tq):
    tk = tq
    qi = pl.program_id(2)
    q0 = pl.multiple_of(qi * tq, tq)
    lane = lax.broadcasted_iota(jnp.int32, (tq, LANE), 1)
    lo_half = lane < HEAD_DIM

    @pl.when(qi == 0)
    def _():
        kb[...] = k_ref[0].astype(MXU)
        vb[...] = v_ref[0].astype(MXU)

    q = q_ref[0] * (HEAD_DIM ** -0.5)
    qh = (jnp.where(lo_half, q, 0.0).astype(MXU), jnp.where(lo_half, 0.0, q).astype(MXU))
    u = u_ref[...]

    def tile(carry, qn, k, v, strict):
        c, acc = carry
        z = _dot_nt(qn, k)
        ls = _log_sigmoid(z)
        log_1m = ls - z
        if strict is not None:
            log_1m = jnp.where(strict, log_1m, 0.0)
        hi = log_1m.astype(MXU)
        lo = (log_1m - hi.astype(f32)).astype(MXU)
        tail = _dot(hi, u) + _dot(lo, u) + c
        a = jnp.exp(ls + tail)
        if strict is not None:
            a = jnp.where(strict, a, 0.0)
        return c + jnp.sum(log_1m, axis=1, keepdims=True), acc + _dot(a.astype(MXU), v)

    rel = lax.broadcasted_iota(jnp.int32, (tq, tk), 0) - lax.broadcasted_iota(jnp.int32, (tq, tk), 1)
    k = kb[pl.ds(q0, tk), :]
    v = vb[pl.ds(q0, tk), :]
    zero = (jnp.zeros((tq, 1), f32), jnp.zeros((tq, LANE), f32))
    heads = tuple(tile(zero, qh[n], k, v, rel > 0) for n in range(2))

    def alive(heads):
        return (jnp.max(jnp.maximum(heads[0][0], heads[1][0])) > _SB_DEAD).astype(jnp.int32)

    def cond(state):
        kt, live, _ = state
        return (kt >= 0) & (live > 0)

    def body(state):
        kt, _, heads = state
        k0 = pl.multiple_of(kt * tk, tk)
        k = kb[pl.ds(k0, tk), :]
        v = vb[pl.ds(k0, tk), :]
        heads = tuple(tile(heads[n], qh[n], k, v, None) for n in range(2))
        return kt - 1, alive(heads), heads

    _, _, heads = lax.while_loop(cond, body, (qi - 1, alive(heads), heads))
    o_ref[0] = jnp.where(lo_half, heads[0][1], heads[1][1])


def stick_breaking_attention(h, tq=256):
    B, S, _ = h.shape
    assert S % tq == 0
    tri = jnp.asarray(np.tril(np.ones((tq, tq), np.float32), -1), MXU)
    cq, ck, cv = (OFF[n] // LANE for n in ("bq", "bk", "bv"))
    return pl.pallas_call(
        functools.partial(_sb_kernel, tq=tq),
        grid=(B, SB_HEADS // 2, S // tq),
        in_specs=[pl.BlockSpec((1, tq, LANE), lambda b, p, i: (b, i, cq + p)),
                  pl.BlockSpec((1, S, LANE), lambda b, p, i: (b, 0, ck + p)),
                  pl.BlockSpec((1, S, LANE), lambda b, p, i: (b, 0, cv + p)),
                  pl.BlockSpec((tq, tq), lambda b, p, i: (0, 0))],
        out_specs=pl.BlockSpec((1, tq, LANE), lambda b, p, i: (b, i, p)),
        out_shape=jax.ShapeDtypeStruct((B, S, SB_HEADS * HEAD_DIM), f32),
        scratch_shapes=[pltpu.VMEM((S, LANE), MXU)] * 2,
        compiler_params=_params(3),
        name="sb_attention",
    )(h, h, h, tri)


def _neg_expm1(y):
    series = -y * (1.0 + y * (1.0 / 2 + y * (1.0 / 6 + y * (1.0 / 24 + y * (1.0 / 120)))))
    return jnp.where(y > -0.1, series, 1.0 - jnp.exp(y))


def _rglru_kernel(x_ref, xg_ref, cw_ref, cb_ref, gaw_ref, gab_ref, gxw_ref, gxb_ref, lam_ref, o_ref):
    x = x_ref[0]
    S = x.shape[0]
    row = lax.broadcasted_iota(jnp.int32, (S, 1), 0)

    def shifted(t, d, fill):
        return jnp.where(row >= d, pltpu.roll(t, d, 0), fill)

    u = cb_ref[...] + x * cw_ref[CONV_W - 1:CONV_W, :]
    for d in range(1, CONV_W):
        u = u + shifted(x, d, 0.0) * cw_ref[CONV_W - 1 - d:CONV_W - d, :]
    ub = u.astype(MXU)
    r = jax.nn.sigmoid(_dot(ub, gaw_ref[...]) + gab_ref[...])
    i = jax.nn.sigmoid(_dot(ub, gxw_ref[...]) + gxb_ref[...])
    lam = lam_ref[...]
    softplus_neg = jnp.maximum(-lam, 0.0) + jnp.log1p(jnp.exp(-jnp.abs(lam)))
    log_a = -LRU_C * r * softplus_neg
    a = jnp.exp(log_a)
    b = jnp.sqrt(_neg_expm1(2.0 * log_a)) * (i * u)
    d = 1
    while d < S:
        b = a * shifted(b, d, 0.0) + b
        a = a * shifted(a, d, 1.0)
        d *= 2
    o_ref[0] = b * jax.nn.gelu(xg_ref[0])


def _block_diag(w):
    n, c, _ = w.shape
    out = jnp.zeros((n * c, n * c), w.dtype)
    for j in range(n):
        out = out.at[j * c:(j + 1) * c, j * c:(j + 1) * c].set(w[j])
    return out


def rglru_block(h, conv_w, conv_b, ga_w, ga_b, gx_w, gx_b, lru_lambda):
    B, S, _ = h.shape
    W = RNN_W
    cx, cg = OFF["cx"] // W, OFF["cg"] // W
    vec = pl.BlockSpec((1, W), lambda b: (0, 0))
    mat = pl.BlockSpec((W, W), lambda b: (0, 0))
    return pl.pallas_call(
        _rglru_kernel,
        grid=(B,),
        in_specs=[pl.BlockSpec((1, S, W), lambda b: (b, 0, cx)), pl.BlockSpec((1, S, W), lambda b: (b, 0, cg)),
                  pl.BlockSpec((CONV_W, W), lambda b: (0, 0)), vec, mat, vec, mat, vec, vec],
        out_specs=pl.BlockSpec((1, S, W), lambda b: (b, 0, 0)),
        out_shape=jax.ShapeDtypeStruct((B, S, W), f32),
        compiler_params=_params(1),
        name="rglru",
    )(h, h, conv_w, conv_b.reshape(1, W), _block_diag(ga_w).astype(MXU), ga_b.reshape(1, W),
      _block_diag(gx_w).astype(MXU), gx_b.reshape(1, W), lru_lambda.reshape(1, W))


def _rms(x, g, width):
    return x * lax.rsqrt(jnp.sum(x * x, axis=-1, keepdims=True) * (1.0 / width) + RMS_EPS) * g


def _mla_prep_kernel(cq_ref, ckv_ref, kr_ref, krs_ref, gq_ref, gkv_ref, wq_ref, wqs_ref, wk_ref, wv_ref,
                     cosq_ref, sinq_ref, cosk_ref, sink_ref, vone_ref, q_ref, k_ref, v_ref):
    cq = _rms(cq_ref[0], gq_ref[...], MLA_Q_RANK).astype(MXU)
    ckv = _rms(ckv_ref[0], gkv_ref[...], MLA_KV_RANK).astype(MXU)
    scale = (MLA_NOPE + MLA_ROPE) ** -0.5
    q = _dot(cq, wq_ref[...]) * cosq_ref[...] + _dot(cq, wqs_ref[...]) * sinq_ref[...]
    q_ref[0] = (q * scale).astype(q_ref.dtype)
    k_rope = kr_ref[0] * cosk_ref[...] + krs_ref[0] * sink_ref[...]
    k = _dot(ckv, wk_ref[...])
    k_ref[0] = (k + jnp.concatenate([k_rope] * MLA_HEADS, axis=1)).astype(k_ref.dtype)
    v_ref[0] = (_dot(ckv, wv_ref[...]) + vone_ref[...]).astype(v_ref.dtype)


def _mla_attn_kernel(q_ref, k_ref, v_ref, cb_ref, o_ref, *, tq):
    tk = tq
    qi = pl.program_id(2)
    q0 = pl.multiple_of(qi * tq, tq)
    q = q_ref[0]
    lane = lax.broadcasted_iota(jnp.int32, (tq, LANE), 1)

    def update(carry, n, k, v, bias):
        m, acc = carry
        sl = slice(n * LANE, (n + 1) * LANE)
        s = _dot_nt(q[:, sl], k[:, sl])
        if bias is not None:
            s = s + bias
        m_new = jnp.maximum(m, jnp.max(s, axis=1, keepdims=True))
        p = jnp.exp(s - m_new)
        return m_new, jnp.exp(m - m_new) * acc + _dot(p.astype(MXU), v[:, sl])

    def body(kt, carry):
        k0 = pl.multiple_of(kt * tk, tk)
        k = k_ref[0, pl.ds(k0, tk), :]
        v = v_ref[0, pl.ds(k0, tk), :]
        return tuple(update(carry[n], n, k, v, None) for n in range(2))

    init = tuple((jnp.full((tq, 1), NEG, f32), jnp.zeros((tq, LANE), f32)) for _ in range(2))
    carry = lax.fori_loop(0, qi, body, init)
    k = k_ref[0, pl.ds(q0, tk), :]
    v = v_ref[0, pl.ds(q0, tk), :]
    o = []
    for n in range(2):
        _, acc = update(carry[n], n, k, v, cb_ref[...])
        denom = jnp.sum(jnp.where(lane == MLA_V, acc, 0.0), axis=1, keepdims=True)
        o.append(acc / jnp.maximum(denom, 1e-30))
    o_ref[0] = jnp.where(lane < MLA_V, o[0], pltpu.roll(o[1], MLA_V, 1))


def mla_attention(h, q_norm, kv_norm, w_uq, w_ukv, tr=512, tq=256):
    B, S, _ = h.shape
    H = MLA_HEADS
    dq = MLA_NOPE + MLA_ROPE
    HW = H * LANE
    wq3 = w_uq.reshape(MLA_Q_RANK, H, dq)
    wq_rot = jnp.concatenate([jnp.zeros_like(wq3[..., :MLA_NOPE]), _rot_half(wq3[..., MLA_NOPE:])], axis=-1)

    def pad_q(w3):
        w3 = jnp.pad(w3, ((0, 256 - MLA_Q_RANK), (0, 0), (0, LANE - dq)))
        return w3.reshape(256, HW).astype(MXU)

    wkv3 = w_ukv.reshape(MLA_KV_RANK, H, MLA_NOPE + MLA_V)
    wk = jnp.pad(wkv3[..., :MLA_NOPE], ((0, 0), (0, 0), (0, LANE - MLA_NOPE))).reshape(MLA_KV_RANK, HW).astype(MXU)
    wv = jnp.pad(wkv3[..., MLA_NOPE:], ((0, 0), (0, 0), (0, LANE - MLA_V))).reshape(MLA_KV_RANK, HW).astype(MXU)
    v_one = jnp.tile(jnp.concatenate([jnp.zeros((1, MLA_V), f32), jnp.ones((1, LANE - MLA_V), f32)], axis=1), (1, H))
    gq = jnp.pad(q_norm, (0, 256 - MLA_Q_RANK)).reshape(1, 256)
    gkv = kv_norm.reshape(1, MLA_KV_RANK)
    inv = ROPE_THETA ** (-jnp.arange(0, MLA_ROPE, 2, dtype=f32) / MLA_ROPE)
    ang = jnp.arange(S, dtype=f32)[:, None] * inv[None, :]
    cos2 = jnp.concatenate([jnp.cos(ang), jnp.cos(ang)], axis=1)
    sin2 = jnp.concatenate([jnp.sin(ang), jnp.sin(ang)], axis=1)
    tail = LANE - dq
    cos_k = jnp.concatenate([jnp.zeros((S, MLA_NOPE), f32), cos2, jnp.zeros((S, tail), f32)], axis=1)
    sin_k = jnp.concatenate([jnp.zeros((S, MLA_NOPE), f32), sin2, jnp.zeros((S, tail), f32)], axis=1)
    cos_q = jnp.tile(jnp.concatenate([jnp.ones((S, MLA_NOPE), f32), cos2, jnp.zeros((S, tail), f32)], axis=1), (1, H))
    sin_q = jnp.tile(sin_k, (1, H))
    rel = np.arange(tq)[:, None] - np.arange(tq)[None, :]
    causal = jnp.asarray(np.where(rel >= 0, 0.0, BIG_NEG).astype(np.float32))

    c_cq, c_ckv, c_kr, c_krs = OFF["dcq"] // 256, OFF["dckv"] // LANE, OFF["dkr"] // LANE, OFF["dkrs"] // LANE
    full = lambda shape: pl.BlockSpec(shape, lambda b, i: (0, 0))
    tab = lambda w: pl.BlockSpec((tr, w), lambda b, i: (i, 0))
    out3 = pl.BlockSpec((1, tr, HW), lambda b, i: (b, i, 0))
    q, k, v = pl.pallas_call(
        _mla_prep_kernel,
        grid=(B, S // tr),
        in_specs=[pl.BlockSpec((1, tr, 256), lambda b, i: (b, i, c_cq)),
                  pl.BlockSpec((1, tr, LANE), lambda b, i: (b, i, c_ckv)),
                  pl.BlockSpec((1, tr, LANE), lambda b, i: (b, i, c_kr)),
                  pl.BlockSpec((1, tr, LANE), lambda b, i: (b, i, c_krs)),
                  full((1, 256)), full((1, MLA_KV_RANK)), full((256, HW)), full((256, HW)),
                  full((MLA_KV_RANK, HW)), full((MLA_KV_RANK, HW)),
                  tab(HW), tab(HW), tab(LANE), tab(LANE), full((1, HW))],
        out_specs=[out3, out3, out3],
        out_shape=[jax.ShapeDtypeStruct((B, S, HW), MXU)] * 3,
        compiler_params=_params(2),
        name="mla_prep",
    )(h, h, h, h, gq, gkv, pad_q(wq3), pad_q(wq_rot), wk, wv, cos_q, sin_q, cos_k, sin_k, v_one)
    pair = lambda rows: pl.BlockSpec((1, rows, 2 * LANE), lambda b, p, i: (b, i if rows == tq else 0, p))
    return pl.pallas_call(
        functools.partial(_mla_attn_kernel, tq=tq),
        grid=(B, H // 2, S // tq),
        in_specs=[pair(tq), pair(S), pair(S), pl.BlockSpec((tq, tq), lambda b, p, i: (0, 0))],
        out_specs=pl.BlockSpec((1, tq, LANE), lambda b, p, i: (b, i, p)),
        out_shape=jax.ShapeDtypeStruct((B, S, H * MLA_V), f32),
        compiler_params=_params(3),
        name="mla_attention",
    )(q, k, v, causal)


def _ln(z, g, b):
    mu = jnp.mean(z, axis=-1, keepdims=True)
    zc = z - mu
    var = jnp.mean(zc * zc, axis=-1, keepdims=True)
    return zc * lax.rsqrt(var + LN_EPS) * g + b


def _merge_kernel(mg_ref, oa_ref, ob_ref, oc_ref, od_ref, wb_ref, wo_ref, x_ref, g_ref, b_ref, o_ref):
    acc = None
    for n, br in enumerate((oa_ref, ob_ref, oc_ref, od_ref)):
        up = _dot(br[...].astype(MXU), wb_ref[n])
        term = jax.nn.sigmoid(mg_ref[:, n * D_MODEL:(n + 1) * D_MODEL]) * up
        acc = term if acc is None else acc + term
    y = _dot(acc.astype(MXU), wo_ref[...])
    o_ref[...] = _ln(DN_ALPHA * x_ref[...] + y, g_ref[...], b_ref[...])


def merge_ln(h2, branches, w_branch, w_out, x2, g, b, tm=256):
    N, D = x2.shape
    assert OFF["mg"] == 0 and N % tm == 0
    row = lambda w: pl.BlockSpec((tm, w), lambda i: (i, 0))
    return pl.pallas_call(
        _merge_kernel,
        grid=(N // tm,),
        in_specs=[row(N_MIXERS * D)] + [row(MIX_W)] * N_MIXERS
        + [pl.BlockSpec((N_MIXERS, MIX_W, D), lambda i: (0, 0, 0)), pl.BlockSpec((D, D), lambda i: (0, 0)),
           row(D), pl.BlockSpec((1, D), lambda i: (0, 0)), pl.BlockSpec((1, D), lambda i: (0, 0))],
        out_specs=row(D),
        out_shape=jax.ShapeDtypeStruct((N, D), f32),
        compiler_params=_params(1),
        name="merge_ln",
    )(h2, *branches, w_branch.astype(MXU), w_out.astype(MXU), x2, g.reshape(1, D), b.reshape(1, D))


def _xattn_kernel(x_ref, wq_ref, k_ref, v_ref, wo_ref, g_ref, b_ref, o_ref):
    x = x_ref[0]
    q = _dot(x.astype(MXU), wq_ref[...]).astype(MXU)
    k = k_ref[0]
    v = v_ref[0]
    heads = []
    for hd in range(X_HEADS):
        sl = slice(hd * X_HEAD_DIM, (hd + 1) * X_HEAD_DIM)
        s = _dot_nt(q[:, sl], k[:, sl]) * (X_HEAD_DIM ** -0.5)
        e = jnp.exp(s - jnp.max(s, axis=1, keepdims=True))
        p = e / jnp.sum(e, axis=1, keepdims=True)
        heads.append(_dot(p.astype(MXU), v[:, sl]).astype(MXU))
    y = _dot(jnp.concatenate(heads, axis=1), wo_ref[...])
    o_ref[0] = _ln(DN_ALPHA * x + y, g_ref[...], b_ref[...])


def cross_attention_ln(x, mem, wq, wkv, wo, g, b, tq=256):
    B, S, D = x.shape
    M = mem.shape[1]
    F = X_HEADS * X_HEAD_DIM
    kv = matmul(mem.reshape(B * M, D), wkv.astype(MXU), 512, 2 * F, out_dtype=MXU).reshape(B, M, 2 * F)
    full = lambda shape: pl.BlockSpec(shape, lambda bi, i: (0,) * len(shape))
    return pl.pallas_call(
        _xattn_kernel,
        grid=(B, S // tq),
        in_specs=[pl.BlockSpec((1, tq, D), lambda bi, i: (bi, i, 0)), full((D, F)),
                  pl.BlockSpec((1, M, F), lambda bi, i: (bi, 0, 0)), pl.BlockSpec((1, M, F), lambda bi, i: (bi, 0, 1)),
                  full((F, D)), full((1, D)), full((1, D))],
        out_specs=pl.BlockSpec((1, tq, D), lambda bi, i: (bi, i, 0)),
        out_shape=jax.ShapeDtypeStruct((B, S, D), f32),
        compiler_params=_params(2),
        name="cross_attention_ln",
    )(x, wq.astype(MXU), kv, kv, wo.astype(MXU), g.reshape(1, D), b.reshape(1, D))


_R_E0, _R_E1, _R_W0, _R_W1, _R_RANK0, _R_RANK1 = range(6)
_GRP_LANE0 = N_EXPERTS


def _router_kernel(x_ref, w_ref, b_ref, tri_ref, r_ref, cnt_ref):
    i = pl.program_id(0)
    tm = x_ref.shape[0]
    logits = _dot(x_ref[...].astype(MXU), w_ref[...]) + b_ref[...]
    lane = lax.broadcasted_iota(jnp.int32, (tm, LANE), 1)
    lane_f = lane.astype(f32)
    big = float(LANE)

    def rmax(t):
        return jnp.max(t, axis=1, keepdims=True)

    def first_lane(cond):
        return jnp.min(jnp.where(cond, lane_f, big), axis=1, keepdims=True)

    def softmax_on(mask):
        lm = jnp.where(mask, logits, NEG)
        e = jnp.where(mask, jnp.exp(lm - rmax(lm)), 0.0)
        return e / jnp.sum(e, axis=1, keepdims=True)

    is_g = (lane >= _GRP_LANE0) & (lane < _GRP_LANE0 + N_GROUPS)
    p_grp = softmax_on(is_g)
    p_g = rmax(p_grp)
    grp = first_lane(is_g & (p_grp == p_g)) - float(_GRP_LANE0)
    in_grp = (lane < N_EXPERTS) & (jnp.right_shift(lane, 3).astype(f32) == grp)
    p_e = softmax_on(in_grp)
    p1 = rmax(jnp.where(in_grp, p_e, -1.0))
    e1 = first_lane(in_grp & (p_e == p1))
    rest = in_grp & (lane_f != e1)
    p2 = rmax(jnp.where(rest, p_e, -1.0))
    e2 = first_lane(rest & (p_e == p2))
    w1 = p_g * p1 / (p1 + p2)
    w2 = p_g * p2 / (p1 + p2)

    @pl.when(i == 0)
    def _():
        cnt_ref[...] = jnp.zeros_like(cnt_ref)

    oh1 = lane_f == e1
    oh2 = lane_f == e2
    both = (oh1 | oh2).astype(MXU)
    before = _dot(tri_ref[...], both) + cnt_ref[0:1, :]
    rank1 = jnp.sum(jnp.where(oh1, before, 0.0), axis=1, keepdims=True)
    rank2 = jnp.sum(jnp.where(oh2, before, 0.0), axis=1, keepdims=True)
    cnt_ref[...] = cnt_ref[...] + jnp.sum(both.astype(f32), axis=0, keepdims=True)

    out = jnp.zeros((tm, LANE), f32)
    for slot, val in ((_R_E0, e1), (_R_E1, e2), (_R_W0, w1), (_R_W1, w2), (_R_RANK0, rank1), (_R_RANK1, rank2)):
        out = jnp.where(lane == slot, val, out)
    r_ref[...] = out


def moe_router(x2, rg_w, rg_b, re_w, re_b, tm=256):
    N, D = x2.shape
    assert EXPERTS_PER_GROUP == 8 and N_EXPERTS + N_GROUPS <= LANE
    w = jnp.pad(jnp.concatenate([re_w, rg_w], axis=1), ((0, 0), (0, LANE - N_EXPERTS - N_GROUPS))).astype(MXU)
    b = jnp.pad(jnp.concatenate([re_b, rg_b]), (0, LANE - N_EXPERTS - N_GROUPS)).reshape(1, LANE)
    tri = jnp.asarray(np.tril(np.ones((tm, tm), np.float32), -1), MXU)
    return pl.pallas_call(
        _router_kernel,
        grid=(N // tm,),
        in_specs=[pl.BlockSpec((tm, D), lambda i: (i, 0)), pl.BlockSpec((D, LANE), lambda i: (0, 0)),
                  pl.BlockSpec((1, LANE), lambda i: (0, 0)), pl.BlockSpec((tm, tm), lambda i: (0, 0))],
        out_specs=[pl.BlockSpec((tm, LANE), lambda i: (i, 0)), pl.BlockSpec((8, LANE), lambda i: (0, 0))],
        out_shape=[jax.ShapeDtypeStruct((N, LANE), f32), jax.ShapeDtypeStruct((8, LANE), f32)],
        compiler_params=_params(1),
        name="moe_router",
    )(x2, w, b, tri)


def _ffn_kernel(ce_ref, nu_ref, x_ref, wgu_ref, wd_ref, o_ref):
    c = pl.program_id(0)

    @pl.when(c < nu_ref[0])
    def _():
        gu = _dot(x_ref[...].astype(MXU), wgu_ref[0, 0].astype(MXU))
        hid = jax.nn.silu(gu[:, :D_EXPERT]) * gu[:, D_EXPERT:]
        o_ref[...] = _dot(hid.astype(MXU), wd_ref[0, 0].astype(MXU))

    @pl.when(c >= nu_ref[0])
    def _():
        o_ref[...] = jnp.zeros_like(o_ref)


def expert_ffn(xb, chunk_e, n_used, w_gu, w_down, layer):
    P, D = xb.shape
    C = EXPERT_CHUNK
    grid_spec = pltpu.PrefetchScalarGridSpec(
        num_scalar_prefetch=2,
        grid=(P // C,),
        in_specs=[pl.BlockSpec((C, D), lambda c, ce, nu: (c, 0)),
                  pl.BlockSpec((1, 1, D, 2 * D_EXPERT), lambda c, ce, nu: (layer, ce[c], 0, 0)),
                  pl.BlockSpec((1, 1, D_EXPERT, D), lambda c, ce, nu: (layer, ce[c], 0, 0))],
        out_specs=pl.BlockSpec((C, D), lambda c, ce, nu: (c, 0)),
    )
    return pl.pallas_call(
        _ffn_kernel,
        grid_spec=grid_spec,
        out_shape=jax.ShapeDtypeStruct((P, D), f32),
        compiler_params=_params(1),
        name="expert_ffn",
    )(chunk_e, n_used, xb, w_gu, w_down)


def _combine_kernel(x_ref, y0_ref, y1_ref, r_ref, g_ref, b_ref, o_ref):
    r = r_ref[...]
    lane = lax.broadcasted_iota(jnp.int32, r.shape, 1)
    w0 = jnp.sum(jnp.where(lane == _R_W0, r, 0.0), axis=1, keepdims=True)
    w1 = jnp.sum(jnp.where(lane == _R_W1, r, 0.0), axis=1, keepdims=True)
    y = y0_ref[...] * w0 + y1_ref[...] * w1
    o_ref[...] = _ln(DN_ALPHA * x_ref[...] + y, g_ref[...], b_ref[...])


def combine_ln(x2, y0, y1, r, g, b, tm=512):
    N, D = x2.shape
    row = lambda w: pl.BlockSpec((tm, w), lambda i: (i, 0))
    vec = pl.BlockSpec((1, D), lambda i: (0, 0))
    return pl.pallas_call(
        _combine_kernel,
        grid=(N // tm,),
        in_specs=[row(D), row(D), row(D), row(LANE), vec, vec],
        out_specs=row(D),
        out_shape=jax.ShapeDtypeStruct((N, D), f32),
        compiler_params=_params(1),
        name="moe_combine_ln",
    )(x2, y0, y1, r, g.reshape(1, D), b.reshape(1, D))


def hier_moe_ln(x2, rg_w, rg_b, re_w, re_b, w_gu, w_down, layer, g, b):
    N, D = x2.shape
    E, C, K = N_EXPERTS, EXPERT_CHUNK, TOPK_IN_GROUP
    A = N * K
    r, cnt = moe_router(x2, rg_w, rg_b, re_w, re_b)
    e = r[:, _R_E0:_R_E1 + 1].astype(jnp.int32)
    rank = r[:, _R_RANK0:_R_RANK1 + 1].astype(jnp.int32)
    counts = cnt[0, :E].astype(jnp.int32)
    padded = (counts + C - 1) // C * C
    pad_end = jnp.cumsum(padded)
    pad_start = pad_end - padded
    dest = pad_start[e] + rank
    n_chunks = -(-(A + E * (C - 1)) // C)
    P = n_chunks * C
    chunk_start = jnp.arange(n_chunks, dtype=jnp.int32) * C
    chunk_e = jnp.minimum(jnp.sum((pad_end[None, :] <= chunk_start[:, None]).astype(jnp.int32), axis=1), E - 1)
    n_used = (pad_end[-1] // C).reshape(1).astype(jnp.int32)
    slot_tok = jnp.full((P,), N, jnp.int32).at[dest.reshape(A)].set(jnp.arange(A, dtype=jnp.int32) // K)
    xb = jnp.take(x2, slot_tok, axis=0, mode="fill", fill_value=0.0)
    yb = expert_ffn(xb, chunk_e.astype(jnp.int32), n_used, w_gu, w_down, layer)
    return combine_ln(x2, yb[dest[:, 0]], yb[dest[:, 1]], r, g, b)


def kernel(x, mem, w_in, nsa_cmp_pos, nsa_cmp_w1, nsa_cmp_w2, rnn_conv_w, rnn_conv_b, rnn_ga_w, rnn_ga_b,
           rnn_gx_w, rnn_gx_b, rnn_lambda, mla_q_norm, mla_kv_norm, mla_w_uq, mla_w_ukv, w_branch, w_out,
           ln1_g, ln1_b, x_wq, x_wkv, x_wo, ln2_g, ln2_b, moe_rg_w, moe_rg_b, moe_re_w, moe_re_b,
           moe_w_gu, moe_w_down, ln3_g, ln3_b):
    B, S, D = x.shape
    N = B * S
    x2 = x.reshape(N, D)
    for l in range(DEPTH):
        h2 = matmul(x2, _cat_w_in(w_in[l]), 512, _IN_TILE)
        h = h2.reshape(B, S, N_CAT)
        o_a = nsa_attention(h, nsa_cmp_pos[l], nsa_cmp_w1[l], nsa_cmp_w2[l])
        o_b = stick_breaking_attention(h)
        o_c = rglru_block(h, rnn_conv_w[l], rnn_conv_b[l], rnn_ga_w[l], rnn_ga_b[l], rnn_gx_w[l], rnn_gx_b[l],
                          rnn_lambda[l])
        o_d = mla_attention(h, mla_q_norm[l], mla_kv_norm[l], mla_w_uq[l], mla_w_ukv[l])
        branches = [o.reshape(N, MIX_W) for o in (o_a, o_b, o_c, o_d)]
        x2 = merge_ln(h2, branches, w_branch[l], w_out[l], x2, ln1_g[l], ln1_b[l])
        x2 = cross_attention_ln(x2.reshape(B, S, D), mem, x_wq[l], x_wkv[l], x_wo[l], ln2_g[l], ln2_b[l]).reshape(N, D)
        x2 = hier_moe_ln(x2, moe_rg_w[l], moe_rg_b[l], moe_re_w[l], moe_re_b[l], moe_w_gu, moe_w_down, l,
                         ln3_g[l], ln3_b[l])
    return x2.reshape(B, S, D)
```

```python
import functools

import numpy as np
import jax
import jax.numpy as jnp
from jax import lax
from jax.experimental import pallas as pl
from jax.experimental.pallas import tpu as pltpu

D_MODEL = 1024
DEPTH = 4
HEAD_DIM = 64
N_MIXERS = 4
MIX_W = 256
NSA_HEADS = 4
NSA_KV_HEADS = 2
NSA_GROUP = NSA_HEADS // NSA_KV_HEADS
CMP_LEN = 32
CMP_STRIDE = 16
CMP_HID = 256
SEL_BLOCK = 64
SEL_TOPK = 8
WINDOW = 512
FORCE_SCORE = 1e4
SB_HEADS = 4
RNN_W = 256
CONV_W = 4
LRU_C = 8.0
MLA_HEADS = 4
MLA_Q_RANK = 192
MLA_KV_RANK = 128
MLA_NOPE = 64
MLA_ROPE = 32
MLA_V = 64
ROPE_THETA = 10000.0
X_HEADS = 4
X_HEAD_DIM = 128
N_GROUPS = 4
EXPERTS_PER_GROUP = 8
N_EXPERTS = N_GROUPS * EXPERTS_PER_GROUP
TOPK_IN_GROUP = 2
D_EXPERT = 512
EXPERT_CHUNK = 256
DN_ALPHA = (2.0 * DEPTH) ** 0.25
LN_EPS = 1e-5
RMS_EPS = 1e-6

IN_SPLITS = ((NSA_HEADS * HEAD_DIM,) + (NSA_KV_HEADS * HEAD_DIM,) * 6 + (NSA_HEADS * 3,)
             + (SB_HEADS * HEAD_DIM,) * 3
             + (RNN_W, RNN_W)
             + (MLA_Q_RANK, MLA_KV_RANK, MLA_ROPE)
             + (N_MIXERS * D_MODEL,))
IN_OFFSETS = tuple(int(o) for o in np.concatenate([[0], np.cumsum(IN_SPLITS)[:-1]]))

LANE = 128
VMEM_LIMIT = 48 * 1024 * 1024
NEG = -1e30
BIG_NEG = -2.0 ** 100

f32 = jnp.float32
MXU = jnp.bfloat16

_SECTIONS = (("mg", 4096), ("aq", 256), ("ks", 256), ("vs", 256), ("kw", 256), ("vw", 256), ("bq", 256),
             ("bk", 256), ("bv", 256), ("cx", 256), ("cg", 256), ("dcq", 256), ("kc", 128), ("vc", 128),
             ("ag", 128), ("dckv", 128), ("dkr", 128), ("dkrs", 128))
_IN_TILE = 3840
OFF = {}
_o = 0
for _n, _w in _SECTIONS:
    OFF[_n] = _o
    _o += _w
N_CAT = -(-_o // _IN_TILE) * _IN_TILE

_AUG_SEL = HEAD_DIM
_AUG_POS = HEAD_DIM + 32
_POS_SPLIT = 256


def _rot_half(t):
    d = t.shape[-1]
    return jnp.concatenate([-t[..., d // 2:], t[..., :d // 2]], axis=-1)


def _cat_w_in(w):
    def sec(i):
        return w[:, IN_OFFSETS[i]:IN_OFFSETS[i] + IN_SPLITS[i]]

    def spread(t):
        z = jnp.zeros_like(t[:, :HEAD_DIM])
        return jnp.concatenate([t[:, :HEAD_DIM], z, t[:, HEAD_DIM:], z], axis=1)

    def padc(t, n):
        return jnp.pad(t, ((0, 0), (0, n - t.shape[1])))

    def rope_slot(t):
        return jnp.pad(t, ((0, 0), (MLA_NOPE, LANE - MLA_NOPE - MLA_ROPE)))

    parts = {"mg": sec(16), "aq": sec(0), "kc": sec(1), "vc": sec(2), "ks": spread(sec(3)), "vs": spread(sec(4)),
             "kw": spread(sec(5)), "vw": spread(sec(6)), "ag": padc(sec(7), 128), "bq": sec(8), "bk": sec(9),
             "bv": sec(10), "cx": sec(11), "cg": sec(12), "dcq": padc(sec(13), 256), "dckv": sec(14),
             "dkr": rope_slot(sec(15)), "dkrs": rope_slot(_rot_half(sec(15)))}
    cat = jnp.concatenate([parts[n] for n, _ in _SECTIONS], axis=1)
    return padc(cat, N_CAT).astype(MXU)


def _dot(a, b):
    return jnp.dot(a, b, preferred_element_type=f32)


def _dot_nt(a, b):
    return lax.dot_general(a, b, (((1,), (1,)), ((), ())), preferred_element_type=f32)


def _params(n_axes):
    return pltpu.CompilerParams(dimension_semantics=("arbitrary",) * n_axes, vmem_limit_bytes=VMEM_LIMIT)


def _mm_kernel(a_ref, b_ref, o_ref):
    o_ref[...] = _dot(a_ref[...].astype(MXU), b_ref[...]).astype(o_ref.dtype)


def matmul(a, b, tm, tn, out_dtype=f32):
    M, K = a.shape
    _, N = b.shape
    assert M % tm == 0 and N % tn == 0
    return pl.pallas_call(
        _mm_kernel,
        grid=(N // tn, M // tm),
        in_specs=[pl.BlockSpec((tm, K), lambda j, i: (i, 0)),
                  pl.BlockSpec((K, tn), lambda j, i: (0, j))],
        out_specs=pl.BlockSpec((tm, tn), lambda j, i: (i, j)),
        out_shape=jax.ShapeDtypeStruct((M, N), out_dtype),
        compiler_params=_params(2),
        name="matmul",
    )(a, b)


def _cmp_kernel(t_ref, w1_ref, pos_ref, w2_ref, o_ref):
    half = CMP_STRIDE * HEAD_DIM
    t = t_ref[0, 0].astype(MXU)
    y1 = _dot(t, w1_ref[0, :half, :])
    y2 = _dot(t, w1_ref[0, half:, :])
    pos = jnp.broadcast_to(pos_ref[0], (8, 2 * half)).astype(MXU)
    pc = _dot(pos, w1_ref[0])[0:1]
    nc = y2.shape[0]
    hid = y1 + pltpu.roll(y2, nc - 1, 0) + pc
    o_ref[0, 0] = _dot(jax.nn.gelu(hid).astype(MXU), w2_ref[0])


def nsa_compress(t, w1, pos, w2):
    _, BG, NC, F = t.shape
    return pl.pallas_call(
        _cmp_kernel,
        grid=(2, BG),
        in_specs=[pl.BlockSpec((1, 1, NC, F), lambda j, i: (j, i, 0, 0)),
                  pl.BlockSpec((1, 2 * F, CMP_HID), lambda j, i: (j, 0, 0)),
                  pl.BlockSpec((1, 1, 2 * F), lambda j, i: (j, 0, 0)),
                  pl.BlockSpec((1, CMP_HID, LANE), lambda j, i: (j, 0, 0))],
        out_specs=pl.BlockSpec((1, 1, NC, LANE), lambda j, i: (j, i, 0, 0)),
        out_shape=jax.ShapeDtypeStruct((2, BG, NC, LANE), f32),
        compiler_params=_params(2),
        name="nsa_compress",
    )(t, w1, pos, w2)


def _nsa_kernel(q_ref, kc_ref, vc_ref, ks_ref, vs_ref, kw_ref, vw_ref, gl_ref, cover_ref, aug_ref, cb_ref, wb_ref,
                o_ref, ksa, vsa, kwa, vwa, *, tq, n_cmp, n_sel, n_top):
    tk = tq
    g = pl.program_id(1)
    qi = pl.program_id(2)
    q0 = pl.multiple_of(qi * tq, tq)
    lane = lax.broadcasted_iota(jnp.int32, (tq, LANE), 1)
    lo_half = lane < HEAD_DIM

    @pl.when(qi == 0)
    def _():
        real = lax.broadcasted_iota(jnp.int32, ksa.shape, 1) < HEAD_DIM
        aug = aug_ref[...]
        ones = jnp.ones(ksa.shape, MXU)
        ksa[...] = jnp.where(real, ks_ref[0].astype(MXU), aug)
        kwa[...] = jnp.where(real, kw_ref[0].astype(MXU), aug)
        vsa[...] = jnp.where(real, vs_ref[0].astype(MXU), ones)
        vwa[...] = jnp.where(real, vw_ref[0].astype(MXU), ones)

    q = q_ref[0] * (HEAD_DIM ** -0.5)
    q_heads = (jnp.where(lo_half, q, 0.0), jnp.where(lo_half, pltpu.roll(q, HEAD_DIM, 1), 0.0))
    alibi = [2.0 ** (-8.0 * (h + 1) / NSA_HEADS) for h in range(NSA_HEADS)]
    slopes = [jnp.where(g == 0, alibi[n], alibi[NSA_GROUP + n]) for n in range(NSA_GROUP)]
    pos_cols = [jnp.where(lane == _AUG_POS, slopes[n] * _POS_SPLIT, jnp.where(lane == _AUG_POS + 1, slopes[n], 0.0))
                for n in range(NSA_GROUP)]
    tpos = q0 + lax.broadcasted_iota(jnp.int32, (tq, 1), 0)

    nc = kc_ref.shape[2]
    cidx = lax.broadcasted_iota(jnp.int32, (1, nc), 1)
    dist_c = tpos - (cidx * CMP_STRIDE + (CMP_LEN - 1))
    mask_c = (dist_c >= 0) & (cidx < n_cmp)
    dist_cf = dist_c.astype(f32)
    kc = kc_ref[0, 0].astype(MXU)
    vc = vc_ref[0, 0].astype(MXU)
    o_cmp = []
    imp_t = jnp.zeros((LANE, tq), f32)
    for n in range(NSA_GROUP):
        s = _dot_nt(q_heads[n].astype(MXU), kc) - slopes[n] * dist_cf
        sm = jnp.where(mask_c, s, NEG)
        m = jnp.max(sm, axis=1, keepdims=True)
        p = jnp.where(mask_c, jnp.exp(sm - m), 0.0)
        p = (p / jnp.maximum(jnp.sum(p, axis=1, keepdims=True), 1e-30)).astype(MXU)
        o_cmp.append(_dot(p, vc))
        imp_t = imp_t + _dot_nt(cover_ref[...], p)

    rows = 32
    imp = imp_t[:rows]
    blk = lax.broadcasted_iota(jnp.int32, (rows, tq), 0)
    blk_f = blk.astype(f32)
    tpos_t = q0 + lax.broadcasted_iota(jnp.int32, (1, tq), 1)
    forced = (blk == 0) | (blk == jnp.right_shift(tpos_t, 6))
    valid = blk * SEL_BLOCK <= tpos_t
    imp = jnp.where(forced, FORCE_SCORE, jnp.where(valid, imp, -1.0))
    imp = jnp.where(blk < n_sel, imp, NEG)
    sel_t = jnp.zeros((rows, tq), f32)
    for _ in range(n_top):
        m = jnp.max(imp, axis=0, keepdims=True)
        first = jnp.min(jnp.where(imp == m, blk_f, float(LANE)), axis=0, keepdims=True)
        pick = blk_f == first
        sel_t = jnp.where(pick, 1.0, sel_t)
        imp = jnp.where(pick, 2 * NEG, imp)
    sel = jnp.concatenate([sel_t, jnp.zeros((LANE - rows, tq), f32)], axis=0).T
    sel_bias = pltpu.roll(jnp.where(sel > 0.5, 0.0, BIG_NEG), _AUG_SEL, 1)
    sel_cols = jnp.where((lane >= _AUG_SEL) & (lane < _AUG_SEL + rows), sel_bias, 0.0)
    q_sel = jnp.concatenate([q_heads[n] + sel_cols + pos_cols[n] for n in range(NSA_GROUP)], axis=0).astype(MXU)
    q_win = jnp.concatenate([q_heads[n] + pos_cols[n] for n in range(NSA_GROUP)], axis=0).astype(MXU)
    rows_q = NSA_GROUP * tq
    denom_lane = lax.broadcasted_iota(jnp.int32, (rows_q, LANE), 1) == HEAD_DIM

    def update(carry, qs, k, v, bias):
        m, acc = carry
        s = _dot_nt(qs, k)
        if bias is not None:
            s = s + bias
        m_new = jnp.maximum(m, jnp.max(s, axis=1, keepdims=True))
        p = jnp.exp(s - m_new)
        return m_new, jnp.exp(m - m_new) * acc + _dot(p.astype(MXU), v)

    def finish(carry):
        _, acc = carry
        denom = jnp.sum(jnp.where(denom_lane, acc, 0.0), axis=1, keepdims=True)
        o = acc / jnp.maximum(denom, 1e-30)
        return [o[n * tq:(n + 1) * tq] for n in range(NSA_GROUP)]

    init = (jnp.full((rows_q, 1), NEG, f32), jnp.zeros((rows_q, LANE), f32))
    causal = cb_ref[...]

    def sel_body(kt, carry):
        k0 = pl.multiple_of(kt * tk, tk)
        return update(carry, q_sel, ksa[pl.ds(k0, tk), :], vsa[pl.ds(k0, tk), :], None)

    carry = lax.fori_loop(0, qi, sel_body, init)
    o_sel = finish(update(carry, q_sel, ksa[pl.ds(q0, tk), :], vsa[pl.ds(q0, tk), :], causal))

    carry = init
    n_back = WINDOW // tk
    for back in range(n_back, -1, -1):
        k0 = pl.multiple_of(jnp.maximum(qi - back, 0) * tk, tk)
        if back:
            off = jnp.where(qi >= back, 0.0, BIG_NEG)
            bias = wb_ref[...] + off if back == n_back else off
        else:
            bias = causal
        carry = update(carry, q_win, kwa[pl.ds(k0, tk), :], vwa[pl.ds(k0, tk), :], bias)
    o_win = finish(carry)

    sig = jax.nn.sigmoid(gl_ref[0])

    def gate(n, j):
        col = 3 * (NSA_GROUP * g + n) + j
        return jnp.sum(jnp.where(lane == col, sig, 0.0), axis=1, keepdims=True)

    o = [gate(n, 0) * o_cmp[n] + gate(n, 1) * o_sel[n] + gate(n, 2) * o_win[n] for n in range(NSA_GROUP)]
    o_ref[0] = jnp.where(lo_half, o[0], pltpu.roll(o[1], HEAD_DIM, 1))


def nsa_attention(h, cmp_pos, cmp_w1, cmp_w2, tq=512):
    B, S, _ = h.shape
    G = NSA_KV_HEADS
    NC = S // CMP_STRIDE
    n_cmp = (S - CMP_LEN) // CMP_STRIDE + 1
    n_sel = S // SEL_BLOCK
    n_top = min(SEL_TOPK, n_sel)
    F = CMP_STRIDE * HEAD_DIM
    assert S % tq == 0 and WINDOW % tq == 0 and tq % SEL_BLOCK == 0 and n_sel <= 32 and S <= _POS_SPLIT ** 2

    def chunks(off):
        t = h[:, :, off:off + G * HEAD_DIM].reshape(B, NC, CMP_STRIDE, G, HEAD_DIM)
        return t.transpose(0, 3, 1, 2, 4).reshape(B * G, NC, F)

    t = jnp.stack([chunks(OFF["kc"]), chunks(OFF["vc"])])
    w2 = jnp.concatenate([cmp_w2, cmp_w2], axis=-1).astype(MXU)
    kvc = nsa_compress(t, cmp_w1.astype(MXU), cmp_pos.reshape(2, 1, 2 * F), w2)

    c0 = np.arange(n_cmp)[:, None] * CMP_STRIDE
    j0 = np.arange(n_sel)[None, :] * SEL_BLOCK
    cover = np.clip(np.minimum(c0 + CMP_LEN, j0 + SEL_BLOCK) - np.maximum(c0, j0), 0, None) / CMP_LEN
    cover_t = np.zeros((LANE, NC), np.float32)
    cover_t[:n_sel, :n_cmp] = cover.T
    pos = np.arange(S)
    aug = np.zeros((S, LANE), np.float32)
    aug[pos, _AUG_SEL + pos // SEL_BLOCK] = 1.0
    aug[:, _AUG_POS] = pos // _POS_SPLIT
    aug[:, _AUG_POS + 1] = pos % _POS_SPLIT
    rel = np.arange(tq)[:, None] - np.arange(tq)[None, :]
    rel = np.tile(rel, (NSA_GROUP, 1))
    causal = np.where(rel >= 0, 0.0, BIG_NEG).astype(np.float32)
    win_lo = np.where(rel < 0, 0.0, BIG_NEG).astype(np.float32)

    col = lambda name: OFF[name] // LANE
    full2 = lambda shape: pl.BlockSpec(shape, lambda b, g, i: (0, 0))
    q_spec = pl.BlockSpec((1, tq, LANE), lambda b, g, i: (b, i, col("aq") + g))
    c_specs = [pl.BlockSpec((1, 1, NC, LANE), lambda b, g, i, j=j: (j, b * G + g, 0, 0)) for j in range(2)]
    kv_specs = [pl.BlockSpec((1, S, LANE), lambda b, g, i, c=col(n): (b, 0, c + g))
                for n in ("ks", "vs", "kw", "vw")]
    gl_spec = pl.BlockSpec((1, tq, LANE), lambda b, g, i: (b, i, col("ag")))
    return pl.pallas_call(
        functools.partial(_nsa_kernel, tq=tq, n_cmp=n_cmp, n_sel=n_sel, n_top=n_top),
        grid=(B, G, S // tq),
        in_specs=[q_spec] + c_specs + kv_specs + [gl_spec, full2((LANE, NC)), full2((S, LANE)),
                                                  full2((NSA_GROUP * tq, tq)), full2((NSA_GROUP * tq, tq))],
        out_specs=pl.BlockSpec((1, tq, LANE), lambda b, g, i: (b, i, g)),
        out_shape=jax.ShapeDtypeStruct((B, S, NSA_HEADS * HEAD_DIM), f32),
        scratch_shapes=[pltpu.VMEM((S, LANE), MXU)] * 4,
        compiler_params=_params(3),
        name="nsa_attention",
    )(h, kvc, kvc, h, h, h, h, h, jnp.asarray(cover_t, MXU), jnp.asarray(aug, MXU), jnp.asarray(causal),
      jnp.asarray(win_lo))


_SB_DEAD = -104.0


def _log_sigmoid(z):
    return jnp.minimum(z, 0.0) - jnp.log1p(jnp.exp(-jnp.abs(z)))


def _sb_kernel(q_ref, k_ref, v_ref, u_ref, o_ref, kb, vb, *, tq):
    tk = tq
    qi = pl.program_id(2)
    q0 = pl.multiple_of(qi * tq, tq)
    lane = lax.broadcasted_iota(jnp.int32, (tq, LANE), 1)
    lo_half = lane < HEAD_DIM

    @pl.when(qi == 0)
    def _():
        kb[...] = k_ref[0].astype(MXU)
        vb[...] = v_ref[0].astype(MXU)

    q = q_ref[0] * (HEAD_DIM ** -0.5)
    qh = (jnp.where(lo_half, q, 0.0).astype(MXU), jnp.where(lo_half, 0.0, q).astype(MXU))
    u = u_ref[...]

    def tile(carry, qn, k, v, strict):
        c, acc = carry
        z = _dot_nt(qn, k)
        ls = _log_sigmoid(z)
        log_1m = ls - z
        if strict is not None:
            log_1m = jnp.where(strict, log_1m, 0.0)
        hi = log_1m.astype(MXU)
        lo = (log_1m - hi.astype(f32)).astype(MXU)
        tail = _dot(hi, u) + _dot(lo, u) + c
        a = jnp.exp(ls + tail)
        if strict is not None:
            a = jnp.where(strict, a, 0.0)
        return c + jnp.sum(log_1m, axis=1, keepdims=True), acc + _dot(a.astype(MXU), v)

    rel = lax.broadcasted_iota(jnp.int32, (tq, tk), 0) - lax.broadcasted_iota(jnp.int32, (tq, tk), 1)
    k = kb[pl.ds(q0, tk), :]
    v = vb[pl.ds(q0, tk), :]
    zero = (jnp.zeros((tq, 1), f32), jnp.zeros((tq, LANE), f32))
    heads = tuple(tile(zero, qh[n], k, v, rel > 0) for n in range(2))

    def alive(heads):
        return (jnp.max(jnp.maximum(heads[0][0], heads[1][0])) > _SB_DEAD).astype(jnp.int32)

    def cond(state):
        kt, live, _ = state
        return (kt >= 0) & (live > 0)

    def body(state):
        kt, _, heads = state
        k0 = pl.multiple_of(kt * tk, tk)
        k = kb[pl.ds(k0, tk), :]
        v = vb[pl.ds(k0, tk), :]
        heads = tuple(tile(heads[n], qh[n], k, v, None) for n in range(2))
        return kt - 1, alive(heads), heads

    _, _, heads = lax.while_loop(cond, body, (qi - 1, alive(heads), heads))
    o_ref[0] = jnp.where(lo_half, heads[0][1], heads[1][1])


def stick_breaking_attention(h, tq=256):
    B, S, _ = h.shape
    assert S % tq == 0
    tri = jnp.asarray(np.tril(np.ones((tq, tq), np.float32), -1), MXU)
    cq, ck, cv = (OFF[n] // LANE for n in ("bq", "bk", "bv"))
    return pl.pallas_call(
        functools.partial(_sb_kernel, tq=tq),
        grid=(B, SB_HEADS // 2, S // tq),
        in_specs=[pl.BlockSpec((1, tq, LANE), lambda b, p, i: (b, i, cq + p)),
                  pl.BlockSpec((1, S, LANE), lambda b, p, i: (b, 0, ck + p)),
                  pl.BlockSpec((1, S, LANE), lambda b, p, i: (b, 0, cv + p)),
                  pl.BlockSpec((tq, tq), lambda b, p, i: (0, 0))],
        out_specs=pl.BlockSpec((1, tq, LANE), lambda b, p, i: (b, i, p)),
        out_shape=jax.ShapeDtypeStruct((B, S, SB_HEADS * HEAD_DIM), f32),
        scratch_shapes=[pltpu.VMEM((S, LANE), MXU)] * 2,
        compiler_params=_params(3),
        name="sb_attention",
    )(h, h, h, tri)


def _neg_expm1(y):
    series = -y * (1.0 + y * (1.0 / 2 + y * (1.0 / 6 + y * (1.0 / 24 + y * (1.0 / 120)))))
    return jnp.where(y > -0.1, series, 1.0 - jnp.exp(y))


def _rglru_kernel(x_ref, xg_ref, cw_ref, cb_ref, gaw_ref, gab_ref, gxw_ref, gxb_ref, lam_ref, o_ref):
    x = x_ref[0]
    S = x.shape[0]
    row = lax.broadcasted_iota(jnp.int32, (S, 1), 0)

    def shifted(t, d, fill):
        return jnp.where(row >= d, pltpu.roll(t, d, 0), fill)

    u = cb_ref[...] + x * cw_ref[CONV_W - 1:CONV_W, :]
    for d in range(1, CONV_W):
        u = u + shifted(x, d, 0.0) * cw_ref[CONV_W - 1 - d:CONV_W - d, :]
    ub = u.astype(MXU)
    r = jax.nn.sigmoid(_dot(ub, gaw_ref[...]) + gab_ref[...])
    i = jax.nn.sigmoid(_dot(ub, gxw_ref[...]) + gxb_ref[...])
    lam = lam_ref[...]
    softplus_neg = jnp.maximum(-lam, 0.0) + jnp.log1p(jnp.exp(-jnp.abs(lam)))
    log_a = -LRU_C * r * softplus_neg
    a = jnp.exp(log_a)
    b = jnp.sqrt(_neg_expm1(2.0 * log_a)) * (i * u)
    d = 1
    while d < S:
        b = a * shifted(b, d, 0.0) + b
        a = a * shifted(a, d, 1.0)
        d *= 2
    o_ref[0] = b * jax.nn.gelu(xg_ref[0])


def _block_diag(w):
    n, c, _ = w.shape
    out = jnp.zeros((n * c, n * c), w.dtype)
    for j in range(n):
        out = out.at[j * c:(j + 1) * c, j * c:(j + 1) * c].set(w[j])
    return out


def rglru_block(h, conv_w, conv_b, ga_w, ga_b, gx_w, gx_b, lru_lambda):
    B, S, _ = h.shape
    W = RNN_W
    cx, cg = OFF["cx"] // W, OFF["cg"] // W
    vec = pl.BlockSpec((1, W), lambda b: (0, 0))
    mat = pl.BlockSpec((W, W), lambda b: (0, 0))
    return pl.pallas_call(
        _rglru_kernel,
        grid=(B,),
        in_specs=[pl.BlockSpec((1, S, W), lambda b: (b, 0, cx)), pl.BlockSpec((1, S, W), lambda b: (b, 0, cg)),
                  pl.BlockSpec((CONV_W, W), lambda b: (0, 0)), vec, mat, vec, mat, vec, vec],
        out_specs=pl.BlockSpec((1, S, W), lambda b: (b, 0, 0)),
        out_shape=jax.ShapeDtypeStruct((B, S, W), f32),
        compiler_params=_params(1),
        name="rglru",
    )(h, h, conv_w, conv_b.reshape(1, W), _block_diag(ga_w).astype(MXU), ga_b.reshape(1, W),
      _block_diag(gx_w).astype(MXU), gx_b.reshape(1, W), lru_lambda.reshape(1, W))


def _rms(x, g, width):
    return x * lax.rsqrt(jnp.sum(x * x, axis=-1, keepdims=True) * (1.0 / width) + RMS_EPS) * g


def _mla_prep_kernel(cq_ref, ckv_ref, kr_ref, krs_ref, gq_ref, gkv_ref, wq_ref, wqs_ref, wk_ref, wv_ref,
                     cosq_ref, sinq_ref, cosk_ref, sink_ref, vone_ref, q_ref, k_ref, v_ref):
    cq = _rms(cq_ref[0], gq_ref[...], MLA_Q_RANK).astype(MXU)
    ckv = _rms(ckv_ref[0], gkv_ref[...], MLA_KV_RANK).astype(MXU)
    scale = (MLA_NOPE + MLA_ROPE) ** -0.5
    q = _dot(cq, wq_ref[...]) * cosq_ref[...] + _dot(cq, wqs_ref[...]) * sinq_ref[...]
    q_ref[0] = (q * scale).astype(q_ref.dtype)
    k_rope = kr_ref[0] * cosk_ref[...] + krs_ref[0] * sink_ref[...]
    k = _dot(ckv, wk_ref[...])
    k_ref[0] = (k + jnp.concatenate([k_rope] * MLA_HEADS, axis=1)).astype(k_ref.dtype)
    v_ref[0] = (_dot(ckv, wv_ref[...]) + vone_ref[...]).astype(v_ref.dtype)


def _mla_attn_kernel(q_ref, k_ref, v_ref, cb_ref, o_ref, *, tq):
    tk = tq
    qi = pl.program_id(2)
    q0 = pl.multiple_of(qi * tq, tq)
    q = q_ref[0]
    lane = lax.broadcasted_iota(jnp.int32, (tq, LANE), 1)

    def update(carry, n, k, v, bias):
        m, acc = carry
        sl = slice(n * LANE, (n + 1) * LANE)
        s = _dot_nt(q[:, sl], k[:, sl])
        if bias is not None:
            s = s + bias
        m_new = jnp.maximum(m, jnp.max(s, axis=1, keepdims=True))
        p = jnp.exp(s - m_new)
        return m_new, jnp.exp(m - m_new) * acc + _dot(p.astype(MXU), v[:, sl])

    def body(kt, carry):
        k0 = pl.multiple_of(kt * tk, tk)
        k = k_ref[0, pl.ds(k0, tk), :]
        v = v_ref[0, pl.ds(k0, tk), :]
        return tuple(update(carry[n], n, k, v, None) for n in range(2))

    init = tuple((jnp.full((tq, 1), NEG, f32), jnp.zeros((tq, LANE), f32)) for _ in range(2))
    carry = lax.fori_loop(0, qi, body, init)
    k = k_ref[0, pl.ds(q0, tk), :]
    v = v_ref[0, pl.ds(q0, tk), :]
    o = []
    for n in range(2):
        _, acc = update(carry[n], n, k, v, cb_ref[...])
        denom = jnp.sum(jnp.where(lane == MLA_V, acc, 0.0), axis=1, keepdims=True)
        o.append(acc / jnp.maximum(denom, 1e-30))
    o_ref[0] = jnp.where(lane < MLA_V, o[0], pltpu.roll(o[1], MLA_V, 1))


def mla_attention(h, q_norm, kv_norm, w_uq, w_ukv, tr=512, tq=512):
    B, S, _ = h.shape
    H = MLA_HEADS
    dq = MLA_NOPE + MLA_ROPE
    HW = H * LANE
    wq3 = w_uq.reshape(MLA_Q_RANK, H, dq)
    wq_rot = jnp.concatenate([jnp.zeros_like(wq3[..., :MLA_NOPE]), _rot_half(wq3[..., MLA_NOPE:])], axis=-1)

    def pad_q(w3):
        w3 = jnp.pad(w3, ((0, 256 - MLA_Q_RANK), (0, 0), (0, LANE - dq)))
        return w3.reshape(256, HW).astype(MXU)

    wkv3 = w_ukv.reshape(MLA_KV_RANK, H, MLA_NOPE + MLA_V)
    wk = jnp.pad(wkv3[..., :MLA_NOPE], ((0, 0), (0, 0), (0, LANE - MLA_NOPE))).reshape(MLA_KV_RANK, HW).astype(MXU)
    wv = jnp.pad(wkv3[..., MLA_NOPE:], ((0, 0), (0, 0), (0, LANE - MLA_V))).reshape(MLA_KV_RANK, HW).astype(MXU)
    v_one = jnp.tile(jnp.concatenate([jnp.zeros((1, MLA_V), f32), jnp.ones((1, LANE - MLA_V), f32)], axis=1), (1, H))
    gq = jnp.pad(q_norm, (0, 256 - MLA_Q_RANK)).reshape(1, 256)
    gkv = kv_norm.reshape(1, MLA_KV_RANK)
    inv = ROPE_THETA ** (-jnp.arange(0, MLA_ROPE, 2, dtype=f32) / MLA_ROPE)
    ang = jnp.arange(S, dtype=f32)[:, None] * inv[None, :]
    cos2 = jnp.concatenate([jnp.cos(ang), jnp.cos(ang)], axis=1)
    sin2 = jnp.concatenate([jnp.sin(ang), jnp.sin(ang)], axis=1)
    tail = LANE - dq
    cos_k = jnp.concatenate([jnp.zeros((S, MLA_NOPE), f32), cos2, jnp.zeros((S, tail), f32)], axis=1)
    sin_k = jnp.concatenate([jnp.zeros((S, MLA_NOPE), f32), sin2, jnp.zeros((S, tail), f32)], axis=1)
    cos_q = jnp.tile(jnp.concatenate([jnp.ones((S, MLA_NOPE), f32), cos2, jnp.zeros((S, tail), f32)], axis=1), (1, H))
    sin_q = jnp.tile(sin_k, (1, H))
    rel = np.arange(tq)[:, None] - np.arange(tq)[None, :]
    causal = jnp.asarray(np.where(rel >= 0, 0.0, BIG_NEG).astype(np.float32))

    c_cq, c_ckv, c_kr, c_krs = OFF["dcq"] // 256, OFF["dckv"] // LANE, OFF["dkr"] // LANE, OFF["dkrs"] // LANE
    full = lambda shape: pl.BlockSpec(shape, lambda b, i: (0, 0))
    tab = lambda w: pl.BlockSpec((tr, w), lambda b, i: (i, 0))
    out3 = pl.BlockSpec((1, tr, HW), lambda b, i: (b, i, 0))
    q, k, v = pl.pallas_call(
        _mla_prep_kernel,
        grid=(B, S // tr),
        in_specs=[pl.BlockSpec((1, tr, 256), lambda b, i: (b, i, c_cq)),
                  pl.BlockSpec((1, tr, LANE), lambda b, i: (b, i, c_ckv)),
                  pl.BlockSpec((1, tr, LANE), lambda b, i: (b, i, c_kr)),
                  pl.BlockSpec((1, tr, LANE), lambda b, i: (b, i, c_krs)),
                  full((1, 256)), full((1, MLA_KV_RANK)), full((256, HW)), full((256, HW)),
                  full((MLA_KV_RANK, HW)), full((MLA_KV_RANK, HW)),
                  tab(HW), tab(HW), tab(LANE), tab(LANE), full((1, HW))],
        out_specs=[out3, out3, out3],
        out_shape=[jax.ShapeDtypeStruct((B, S, HW), MXU)] * 3,
        compiler_params=_params(2),
        name="mla_prep",
    )(h, h, h, h, gq, gkv, pad_q(wq3), pad_q(wq_rot), wk, wv, cos_q, sin_q, cos_k, sin_k, v_one)
    pair = lambda rows: pl.BlockSpec((1, rows, 2 * LANE), lambda b, p, i: (b, i if rows == tq else 0, p))
    return pl.pallas_call(
        functools.partial(_mla_attn_kernel, tq=tq),
        grid=(B, H // 2, S // tq),
        in_specs=[pair(tq), pair(S), pair(S), pl.BlockSpec((tq, tq), lambda b, p, i: (0, 0))],
        out_specs=pl.BlockSpec((1, tq, LANE), lambda b, p, i: (b, i, p)),
        out_shape=jax.ShapeDtypeStruct((B, S, H * MLA_V), f32),
        compiler_params=_params(3),
        name="mla_attention",
    )(q, k, v, causal)


def _ln(z, g, b):
    mu = jnp.mean(z, axis=-1, keepdims=True)
    zc = z - mu
    var = jnp.mean(zc * zc, axis=-1, keepdims=True)
    return zc * lax.rsqrt(var + LN_EPS) * g + b


def _merge_kernel(mg_ref, oa_ref, ob_ref, oc_ref, od_ref, wb_ref, wo_ref, x_ref, g_ref, b_ref, o_ref):
    acc = None
    for n, br in enumerate((oa_ref, ob_ref, oc_ref, od_ref)):
        up = _dot(br[...].astype(MXU), wb_ref[n])
        term = jax.nn.sigmoid(mg_ref[:, n * D_MODEL:(n + 1) * D_MODEL]) * up
        acc = term if acc is None else acc + term
    y = _dot(acc.astype(MXU), wo_ref[...])
    o_ref[...] = _ln(DN_ALPHA * x_ref[...] + y, g_ref[...], b_ref[...])


def merge_ln(h2, branches, w_branch, w_out, x2, g, b, tm=256):
    N, D = x2.shape
    assert OFF["mg"] == 0 and N % tm == 0
    row = lambda w: pl.BlockSpec((tm, w), lambda i: (i, 0))
    return pl.pallas_call(
        _merge_kernel,
        grid=(N // tm,),
        in_specs=[row(N_MIXERS * D)] + [row(MIX_W)] * N_MIXERS
        + [pl.BlockSpec((N_MIXERS, MIX_W, D), lambda i: (0, 0, 0)), pl.BlockSpec((D, D), lambda i: (0, 0)),
           row(D), pl.BlockSpec((1, D), lambda i: (0, 0)), pl.BlockSpec((1, D), lambda i: (0, 0))],
        out_specs=row(D),
        out_shape=jax.ShapeDtypeStruct((N, D), f32),
        compiler_params=_params(1),
        name="merge_ln",
    )(h2, *branches, w_branch.astype(MXU), w_out.astype(MXU), x2, g.reshape(1, D), b.reshape(1, D))


ROW_TILE = 8


def _to_token_tiles(ref, val):
    rows = val.shape[0]
    for j in range(ROW_TILE):
        ref[pl.ds(j, rows, stride=ROW_TILE), :] = val[:, j * LANE:(j + 1) * LANE]


def _from_token_tiles(ref, rows, first=0, stride=ROW_TILE):
    return jnp.concatenate([ref[pl.ds(first + j, rows, stride=stride), :] for j in range(ROW_TILE)], axis=1)


def _xattn_kernel(x_ref, wq_ref, k_ref, v_ref, wo_ref, g_ref, b_ref, o_ref, o8_ref):
    x = x_ref[0]
    q = _dot(x.astype(MXU), wq_ref[...]).astype(MXU)
    k = k_ref[0]
    v = v_ref[0]
    heads = []
    for hd in range(X_HEADS):
        sl = slice(hd * X_HEAD_DIM, (hd + 1) * X_HEAD_DIM)
        s = _dot_nt(q[:, sl], k[:, sl]) * (X_HEAD_DIM ** -0.5)
        e = jnp.exp(s - jnp.max(s, axis=1, keepdims=True))
        p = e / jnp.sum(e, axis=1, keepdims=True)
        heads.append(_dot(p.astype(MXU), v[:, sl]).astype(MXU))
    y = _dot(jnp.concatenate(heads, axis=1), wo_ref[...])
    out = _ln(DN_ALPHA * x + y, g_ref[...], b_ref[...])
    o_ref[0] = out
    _to_token_tiles(o8_ref.at[0], out)


def cross_attention_ln(x, mem, wq, wkv, wo, g, b, tq=256):
    B, S, D = x.shape
    assert D == ROW_TILE * LANE
    M = mem.shape[1]
    F = X_HEADS * X_HEAD_DIM
    kv = matmul(mem.reshape(B * M, D), wkv.astype(MXU), 512, 2 * F, out_dtype=MXU).reshape(B, M, 2 * F)
    full = lambda shape: pl.BlockSpec(shape, lambda bi, i: (0,) * len(shape))
    return pl.pallas_call(
        _xattn_kernel,
        grid=(B, S // tq),
        in_specs=[pl.BlockSpec((1, tq, D), lambda bi, i: (bi, i, 0)), full((D, F)),
                  pl.BlockSpec((1, M, F), lambda bi, i: (bi, 0, 0)), pl.BlockSpec((1, M, F), lambda bi, i: (bi, 0, 1)),
                  full((F, D)), full((1, D)), full((1, D))],
        out_specs=[pl.BlockSpec((1, tq, D), lambda bi, i: (bi, i, 0)),
                   pl.BlockSpec((1, tq * ROW_TILE, LANE), lambda bi, i: (bi, i, 0))],
        out_shape=[jax.ShapeDtypeStruct((B, S, D), f32), jax.ShapeDtypeStruct((B, S * ROW_TILE, LANE), f32)],
        compiler_params=_params(2),
        name="cross_attention_ln",
    )(x, wq.astype(MXU), kv, kv, wo.astype(MXU), g.reshape(1, D), b.reshape(1, D))


_R_E0, _R_E1, _R_W0, _R_W1, _R_RANK0, _R_RANK1 = range(6)
_GRP_LANE0 = N_EXPERTS


def _router_kernel(x_ref, w_ref, b_ref, tri_ref, r_ref, cnt_ref):
    i = pl.program_id(0)
    tm = x_ref.shape[0]
    logits = _dot(x_ref[...].astype(MXU), w_ref[...]) + b_ref[...]
    lane = lax.broadcasted_iota(jnp.int32, (tm, LANE), 1)
    lane_f = lane.astype(f32)
    big = float(LANE)

    def rmax(t):
        return jnp.max(t, axis=1, keepdims=True)

    def first_lane(cond):
        return jnp.min(jnp.where(cond, lane_f, big), axis=1, keepdims=True)

    def softmax_on(mask):
        lm = jnp.where(mask, logits, NEG)
        e = jnp.where(mask, jnp.exp(lm - rmax(lm)), 0.0)
        return e / jnp.sum(e, axis=1, keepdims=True)

    is_g = (lane >= _GRP_LANE0) & (lane < _GRP_LANE0 + N_GROUPS)
    p_grp = softmax_on(is_g)
    p_g = rmax(p_grp)
    grp = first_lane(is_g & (p_grp == p_g)) - float(_GRP_LANE0)
    in_grp = (lane < N_EXPERTS) & (jnp.right_shift(lane, 3).astype(f32) == grp)
    p_e = softmax_on(in_grp)
    p1 = rmax(jnp.where(in_grp, p_e, -1.0))
    e1 = first_lane(in_grp & (p_e == p1))
    rest = in_grp & (lane_f != e1)
    p2 = rmax(jnp.where(rest, p_e, -1.0))
    e2 = first_lane(rest & (p_e == p2))
    w1 = p_g * p1 / (p1 + p2)
    w2 = p_g * p2 / (p1 + p2)

    @pl.when(i == 0)
    def _():
        cnt_ref[...] = jnp.zeros_like(cnt_ref)

    oh1 = lane_f == e1
    oh2 = lane_f == e2
    both = (oh1 | oh2).astype(MXU)
    before = _dot(tri_ref[...], both) + cnt_ref[0:1, :]
    rank1 = jnp.sum(jnp.where(oh1, before, 0.0), axis=1, keepdims=True)
    rank2 = jnp.sum(jnp.where(oh2, before, 0.0), axis=1, keepdims=True)
    cnt_ref[...] = cnt_ref[...] + jnp.sum(both.astype(f32), axis=0, keepdims=True)

    out = jnp.zeros((tm, LANE), f32)
    for slot, val in ((_R_E0, e1), (_R_E1, e2), (_R_W0, w1), (_R_W1, w2), (_R_RANK0, rank1), (_R_RANK1, rank2)):
        out = jnp.where(lane == slot, val, out)
    r_ref[...] = out


def moe_router(x2, rg_w, rg_b, re_w, re_b, tm=256):
    N, D = x2.shape
    assert EXPERTS_PER_GROUP == 8 and N_EXPERTS + N_GROUPS <= LANE
    w = jnp.pad(jnp.concatenate([re_w, rg_w], axis=1), ((0, 0), (0, LANE - N_EXPERTS - N_GROUPS))).astype(MXU)
    b = jnp.pad(jnp.concatenate([re_b, rg_b]), (0, LANE - N_EXPERTS - N_GROUPS)).reshape(1, LANE)
    tri = jnp.asarray(np.tril(np.ones((tm, tm), np.float32), -1), MXU)
    return pl.pallas_call(
        _router_kernel,
        grid=(N // tm,),
        in_specs=[pl.BlockSpec((tm, D), lambda i: (i, 0)), pl.BlockSpec((D, LANE), lambda i: (0, 0)),
                  pl.BlockSpec((1, LANE), lambda i: (0, 0)), pl.BlockSpec((tm, tm), lambda i: (0, 0))],
        out_specs=[pl.BlockSpec((tm, LANE), lambda i: (i, 0)), pl.BlockSpec((8, LANE), lambda i: (0, 0))],
        out_shape=[jax.ShapeDtypeStruct((N, LANE), f32), jax.ShapeDtypeStruct((8, LANE), f32)],
        compiler_params=_params(1),
        name="moe_router",
    )(x2, w, b, tri)


def _ffn_kernel(ce_ref, nu_ref, x_ref, wgu_ref, wd_ref, o_ref):
    c = pl.program_id(0)

    @pl.when(c < nu_ref[0])
    def _():
        x = _from_token_tiles(x_ref, EXPERT_CHUNK)
        gu = _dot(x.astype(MXU), wgu_ref[0, 0].astype(MXU))
        hid = jax.nn.silu(gu[:, :D_EXPERT]) * gu[:, D_EXPERT:]
        _to_token_tiles(o_ref, _dot(hid.astype(MXU), wd_ref[0, 0].astype(MXU)))

    @pl.when(c >= nu_ref[0])
    def _():
        o_ref[...] = jnp.zeros_like(o_ref)


def expert_ffn(xb8, chunk_e, n_used, w_gu, w_down, layer):
    C = EXPERT_CHUNK
    n_chunks = xb8.shape[0] // (C * ROW_TILE)
    D = ROW_TILE * LANE
    tile_spec = pl.BlockSpec((C * ROW_TILE, LANE), lambda c, ce, nu: (c, 0))
    grid_spec = pltpu.PrefetchScalarGridSpec(
        num_scalar_prefetch=2,
        grid=(n_chunks,),
        in_specs=[tile_spec,
                  pl.BlockSpec((1, 1, D, 2 * D_EXPERT), lambda c, ce, nu: (layer, ce[c], 0, 0)),
                  pl.BlockSpec((1, 1, D_EXPERT, D), lambda c, ce, nu: (layer, ce[c], 0, 0))],
        out_specs=tile_spec,
    )
    return pl.pallas_call(
        _ffn_kernel,
        grid_spec=grid_spec,
        out_shape=jax.ShapeDtypeStruct(xb8.shape, f32),
        compiler_params=_params(1),
        name="expert_ffn",
    )(chunk_e, n_used, xb8, w_gu, w_down)


_COPY_WINDOW = 256


def _row_copy_kernel(src_idx, dst_idx, src_hbm, *rest, n, has_init):
    dst_hbm, sems = rest[-2:]

    def copy(a):
        s = pl.multiple_of(src_idx[a] * ROW_TILE, ROW_TILE)
        d = pl.multiple_of(dst_idx[a] * ROW_TILE, ROW_TILE)
        return pltpu.make_async_copy(src_hbm.at[pl.ds(s, ROW_TILE)], dst_hbm.at[pl.ds(d, ROW_TILE)],
                                     sems.at[a % _COPY_WINDOW])

    def wait(a):
        pltpu.make_async_copy(src_hbm.at[pl.ds(0, ROW_TILE)], dst_hbm.at[pl.ds(0, ROW_TILE)],
                              sems.at[a % _COPY_WINDOW]).wait()

    def fill(a, carry):
        copy(a).start()
        return carry

    def steady(a, carry):
        wait(a - _COPY_WINDOW)
        copy(a).start()
        return carry

    def drain(a, carry):
        wait(a)
        return carry

    lax.fori_loop(0, _COPY_WINDOW, fill, 0, unroll=8)
    lax.fori_loop(_COPY_WINDOW, n, steady, 0, unroll=8)
    lax.fori_loop(n - _COPY_WINDOW, n, drain, 0, unroll=8)


def row_copy(src8, src_idx, dst_idx, dst_rows, init=None):
    n = src_idx.shape[0]
    assert n >= _COPY_WINDOW
    any_spec = pl.BlockSpec(memory_space=pl.ANY)
    operands = [src_idx, dst_idx, src8] + ([init] if init is not None else [])
    grid_spec = pltpu.PrefetchScalarGridSpec(
        num_scalar_prefetch=2,
        grid=(1,),
        in_specs=[any_spec] * (len(operands) - 2),
        out_specs=any_spec,
        scratch_shapes=[pltpu.SemaphoreType.DMA((_COPY_WINDOW,))],
    )
    return pl.pallas_call(
        functools.partial(_row_copy_kernel, n=n, has_init=init is not None),
        grid_spec=grid_spec,
        out_shape=jax.ShapeDtypeStruct((dst_rows * ROW_TILE, LANE), src8.dtype),
        input_output_aliases={3: 0} if init is not None else {},
        compiler_params=_params(1),
        name="moe_row_copy",
    )(*operands)


def _combine_kernel(x_ref, y_ref, r_ref, g_ref, b_ref, o_ref):
    tm = x_ref.shape[0]
    r = r_ref[...]
    lane = lax.broadcasted_iota(jnp.int32, r.shape, 1)
    w0 = jnp.sum(jnp.where(lane == _R_W0, r, 0.0), axis=1, keepdims=True)
    w1 = jnp.sum(jnp.where(lane == _R_W1, r, 0.0), axis=1, keepdims=True)
    pair = TOPK_IN_GROUP * ROW_TILE
    y = (_from_token_tiles(y_ref, tm, 0, pair) * w0 + _from_token_tiles(y_ref, tm, ROW_TILE, pair) * w1)
    o_ref[...] = _ln(DN_ALPHA * x_ref[...] + y, g_ref[...], b_ref[...])


def combine_ln(x2, y8, r, g, b, tm=256):
    N, D = x2.shape
    row = lambda w: pl.BlockSpec((tm, w), lambda i: (i, 0))
    vec = pl.BlockSpec((1, D), lambda i: (0, 0))
    return pl.pallas_call(
        _combine_kernel,
        grid=(N // tm,),
        in_specs=[row(D), pl.BlockSpec((tm * TOPK_IN_GROUP * ROW_TILE, LANE), lambda i: (i, 0)), row(LANE), vec, vec],
        out_specs=row(D),
        out_shape=jax.ShapeDtypeStruct((N, D), f32),
        compiler_params=_params(1),
        name="moe_combine_ln",
    )(x2, y8, r, g.reshape(1, D), b.reshape(1, D))


def hier_moe_ln(x2, x8, rg_w, rg_b, re_w, re_b, w_gu, w_down, layer, g, b):
    N, D = x2.shape
    E, C, K = N_EXPERTS, EXPERT_CHUNK, TOPK_IN_GROUP
    A = N * K
    r, cnt = moe_router(x2, rg_w, rg_b, re_w, re_b)
    e = r[:, _R_E0:_R_E1 + 1].astype(jnp.int32)
    rank = r[:, _R_RANK0:_R_RANK1 + 1].astype(jnp.int32)
    counts = cnt[0, :E].astype(jnp.int32)
    padded = (counts + C - 1) // C * C
    pad_end = jnp.cumsum(padded)
    pad_start = pad_end - padded
    dest = pad_start[e] + rank
    n_chunks = -(-(A + E * (C - 1)) // C)
    P = n_chunks * C
    chunk_start = jnp.arange(n_chunks, dtype=jnp.int32) * C
    chunk_e = jnp.minimum(jnp.sum((pad_end[None, :] <= chunk_start[:, None]).astype(jnp.int32), axis=1), E - 1)
    n_used = (pad_end[-1] // C).reshape(1).astype(jnp.int32)
    dest = dest.reshape(A).astype(jnp.int32)
    assign = jnp.arange(A, dtype=jnp.int32)
    xb8 = row_copy(x8, assign // K, dest, P, init=jnp.zeros((P * ROW_TILE, LANE), f32))
    yb8 = expert_ffn(xb8, chunk_e.astype(jnp.int32), n_used, w_gu, w_down, layer)
    y8 = row_copy(yb8, dest, assign, A)
    return combine_ln(x2, y8, r, g, b)


def kernel(x, mem, w_in, nsa_cmp_pos, nsa_cmp_w1, nsa_cmp_w2, rnn_conv_w, rnn_conv_b, rnn_ga_w, rnn_ga_b,
           rnn_gx_w, rnn_gx_b, rnn_lambda, mla_q_norm, mla_kv_norm, mla_w_uq, mla_w_ukv, w_branch, w_out,
           ln1_g, ln1_b, x_wq, x_wkv, x_wo, ln2_g, ln2_b, moe_rg_w, moe_rg_b, moe_re_w, moe_re_b,
           moe_w_gu, moe_w_down, ln3_g, ln3_b):
    B, S, D = x.shape
    N = B * S
    x2 = x.reshape(N, D)
    for l in range(DEPTH):
        h2 = matmul(x2, _cat_w_in(w_in[l]), 512, _IN_TILE)
        h = h2.reshape(B, S, N_CAT)
        o_a = nsa_attention(h, nsa_cmp_pos[l], nsa_cmp_w1[l], nsa_cmp_w2[l])
        o_b = stick_breaking_attention(h)
        o_c = rglru_block(h, rnn_conv_w[l], rnn_conv_b[l], rnn_ga_w[l], rnn_ga_b[l], rnn_gx_w[l], rnn_gx_b[l],
                          rnn_lambda[l])
        o_d = mla_attention(h, mla_q_norm[l], mla_kv_norm[l], mla_w_uq[l], mla_w_ukv[l])
        branches = [o.reshape(N, MIX_W) for o in (o_a, o_b, o_c, o_d)]
        x2 = merge_ln(h2, branches, w_branch[l], w_out[l], x2, ln1_g[l], ln1_b[l])
        x3, x8 = cross_attention_ln(x2.reshape(B, S, D), mem, x_wq[l], x_wkv[l], x_wo[l], ln2_g[l], ln2_b[l])
        x2 = hier_moe_ln(x3.reshape(N, D), x8.reshape(N * ROW_TILE, LANE), moe_rg_w[l], moe_rg_b[l], moe_re_w[l],
                         moe_re_b[l], moe_w_gu, moe_w_down, l, ln3_g[l], ln3_b[l])
    return x2.reshape(B, S, D)
```

```python
import functools

import numpy as np
import jax
import jax.numpy as jnp
from jax import lax
from jax.experimental import pallas as pl
from jax.experimental.pallas import tpu as pltpu

D_MODEL = 1024
DEPTH = 4
HEAD_DIM = 64
N_MIXERS = 4
MIX_W = 256
NSA_HEADS = 4
NSA_KV_HEADS = 2
NSA_GROUP = NSA_HEADS // NSA_KV_HEADS
CMP_LEN = 32
CMP_STRIDE = 16
CMP_HID = 256
SEL_BLOCK = 64
SEL_TOPK = 8
WINDOW = 512
FORCE_SCORE = 1e4
SB_HEADS = 4
RNN_W = 256
CONV_W = 4
LRU_C = 8.0
MLA_HEADS = 4
MLA_Q_RANK = 192
MLA_KV_RANK = 128
MLA_NOPE = 64
MLA_ROPE = 32
MLA_V = 64
ROPE_THETA = 10000.0
X_HEADS = 4
X_HEAD_DIM = 128
N_GROUPS = 4
EXPERTS_PER_GROUP = 8
N_EXPERTS = N_GROUPS * EXPERTS_PER_GROUP
TOPK_IN_GROUP = 2
D_EXPERT = 512
EXPERT_CHUNK = 256
DN_ALPHA = (2.0 * DEPTH) ** 0.25
LN_EPS = 1e-5
RMS_EPS = 1e-6

IN_SPLITS = ((NSA_HEADS * HEAD_DIM,) + (NSA_KV_HEADS * HEAD_DIM,) * 6 + (NSA_HEADS * 3,)
             + (SB_HEADS * HEAD_DIM,) * 3
             + (RNN_W, RNN_W)
             + (MLA_Q_RANK, MLA_KV_RANK, MLA_ROPE)
             + (N_MIXERS * D_MODEL,))
IN_OFFSETS = tuple(int(o) for o in np.concatenate([[0], np.cumsum(IN_SPLITS)[:-1]]))

LANE = 128
VMEM_LIMIT = 48 * 1024 * 1024
NEG = -1e30
BIG_NEG = -2.0 ** 100

f32 = jnp.float32
MXU = jnp.bfloat16

_SECTIONS = (("mg", 4096), ("aq", 256), ("ks", 256), ("vs", 256), ("kw", 256), ("vw", 256), ("bq", 256),
             ("bk", 256), ("bv", 256), ("cx", 256), ("cg", 256), ("dcq", 256), ("kc", 128), ("vc", 128),
             ("ag", 128), ("dckv", 128), ("dkr", 128), ("dkrs", 128))
_IN_TILE = 3840
OFF = {}
_o = 0
for _n, _w in _SECTIONS:
    OFF[_n] = _o
    _o += _w
N_CAT = -(-_o // _IN_TILE) * _IN_TILE

_AUG_SEL = HEAD_DIM
_AUG_POS = HEAD_DIM + 32
_POS_SPLIT = 256


def _rot_half(t):
    d = t.shape[-1]
    return jnp.concatenate([-t[..., d // 2:], t[..., :d // 2]], axis=-1)


def _cat_w_in(w):
    def sec(i):
        return w[:, IN_OFFSETS[i]:IN_OFFSETS[i] + IN_SPLITS[i]]

    def spread(t):
        z = jnp.zeros_like(t[:, :HEAD_DIM])
        return jnp.concatenate([t[:, :HEAD_DIM], z, t[:, HEAD_DIM:], z], axis=1)

    def padc(t, n):
        return jnp.pad(t, ((0, 0), (0, n - t.shape[1])))

    def rope_slot(t):
        return jnp.pad(t, ((0, 0), (MLA_NOPE, LANE - MLA_NOPE - MLA_ROPE)))

    parts = {"mg": sec(16), "aq": sec(0), "kc": sec(1), "vc": sec(2), "ks": spread(sec(3)), "vs": spread(sec(4)),
             "kw": spread(sec(5)), "vw": spread(sec(6)), "ag": padc(sec(7), 128), "bq": sec(8), "bk": sec(9),
             "bv": sec(10), "cx": sec(11), "cg": sec(12), "dcq": padc(sec(13), 256), "dckv": sec(14),
             "dkr": rope_slot(sec(15)), "dkrs": rope_slot(_rot_half(sec(15)))}
    cat = jnp.concatenate([parts[n] for n, _ in _SECTIONS], axis=1)
    return padc(cat, N_CAT).astype(MXU)


def _dot(a, b):
    return jnp.dot(a, b, preferred_element_type=f32)


def _dot_nt(a, b):
    return lax.dot_general(a, b, (((1,), (1,)), ((), ())), preferred_element_type=f32)


def _params(n_axes):
    return pltpu.CompilerParams(dimension_semantics=("arbitrary",) * n_axes, vmem_limit_bytes=VMEM_LIMIT)


def _mm_kernel(a_ref, b_ref, o_ref):
    o_ref[...] = _dot(a_ref[...].astype(MXU), b_ref[...]).astype(o_ref.dtype)


def matmul(a, b, tm, tn, out_dtype=f32):
    M, K = a.shape
    _, N = b.shape
    assert M % tm == 0 and N % tn == 0
    return pl.pallas_call(
        _mm_kernel,
        grid=(N // tn, M // tm),
        in_specs=[pl.BlockSpec((tm, K), lambda j, i: (i, 0)),
                  pl.BlockSpec((K, tn), lambda j, i: (0, j))],
        out_specs=pl.BlockSpec((tm, tn), lambda j, i: (i, j)),
        out_shape=jax.ShapeDtypeStruct((M, N), out_dtype),
        compiler_params=_params(2),
        name="matmul",
    )(a, b)


def _cmp_kernel(t_ref, w1_ref, pos_ref, w2_ref, o_ref):
    half = CMP_STRIDE * HEAD_DIM
    t = t_ref[0, 0].astype(MXU)
    y1 = _dot(t, w1_ref[0, :half, :])
    y2 = _dot(t, w1_ref[0, half:, :])
    pos = jnp.broadcast_to(pos_ref[0], (8, 2 * half)).astype(MXU)
    pc = _dot(pos, w1_ref[0])[0:1]
    nc = y2.shape[0]
    hid = y1 + pltpu.roll(y2, nc - 1, 0) + pc
    o_ref[0, 0] = _dot(jax.nn.gelu(hid).astype(MXU), w2_ref[0])


def nsa_compress(t, w1, pos, w2):
    _, BG, NC, F = t.shape
    return pl.pallas_call(
        _cmp_kernel,
        grid=(2, BG),
        in_specs=[pl.BlockSpec((1, 1, NC, F), lambda j, i: (j, i, 0, 0)),
                  pl.BlockSpec((1, 2 * F, CMP_HID), lambda j, i: (j, 0, 0)),
                  pl.BlockSpec((1, 1, 2 * F), lambda j, i: (j, 0, 0)),
                  pl.BlockSpec((1, CMP_HID, LANE), lambda j, i: (j, 0, 0))],
        out_specs=pl.BlockSpec((1, 1, NC, LANE), lambda j, i: (j, i, 0, 0)),
        out_shape=jax.ShapeDtypeStruct((2, BG, NC, LANE), f32),
        compiler_params=_params(2),
        name="nsa_compress",
    )(t, w1, pos, w2)


def _nsa_kernel(q_ref, kc_ref, vc_ref, ks_ref, vs_ref, kw_ref, vw_ref, gl_ref, cover_ref, aug_ref, cb_ref, wb_ref,
                o_ref, ksa, vsa, kwa, vwa, *, tq, n_cmp, n_sel, n_top):
    tk = tq
    g = pl.program_id(1)
    qi = pl.program_id(2)
    q0 = pl.multiple_of(qi * tq, tq)
    lane = lax.broadcasted_iota(jnp.int32, (tq, LANE), 1)
    lo_half = lane < HEAD_DIM

    @pl.when(qi == 0)
    def _():
        real = lax.broadcasted_iota(jnp.int32, ksa.shape, 1) < HEAD_DIM
        aug = aug_ref[...]
        ones = jnp.ones(ksa.shape, MXU)
        ksa[...] = jnp.where(real, ks_ref[0].astype(MXU), aug)
        kwa[...] = jnp.where(real, kw_ref[0].astype(MXU), aug)
        vsa[...] = jnp.where(real, vs_ref[0].astype(MXU), ones)
        vwa[...] = jnp.where(real, vw_ref[0].astype(MXU), ones)

    q = q_ref[0] * (HEAD_DIM ** -0.5)
    q_heads = (jnp.where(lo_half, q, 0.0), jnp.where(lo_half, pltpu.roll(q, HEAD_DIM, 1), 0.0))
    alibi = [2.0 ** (-8.0 * (h + 1) / NSA_HEADS) for h in range(NSA_HEADS)]
    slopes = [jnp.where(g == 0, alibi[n], alibi[NSA_GROUP + n]) for n in range(NSA_GROUP)]
    pos_cols = [jnp.where(lane == _AUG_POS, slopes[n] * _POS_SPLIT, jnp.where(lane == _AUG_POS + 1, slopes[n], 0.0))
                for n in range(NSA_GROUP)]
    tpos = q0 + lax.broadcasted_iota(jnp.int32, (tq, 1), 0)

    nc = kc_ref.shape[2]
    cidx = lax.broadcasted_iota(jnp.int32, (1, nc), 1)
    dist_c = tpos - (cidx * CMP_STRIDE + (CMP_LEN - 1))
    mask_c = (dist_c >= 0) & (cidx < n_cmp)
    dist_cf = dist_c.astype(f32)
    kc = kc_ref[0, 0].astype(MXU)
    vc = vc_ref[0, 0].astype(MXU)
    o_cmp = []
    imp_t = jnp.zeros((LANE, tq), f32)
    for n in range(NSA_GROUP):
        s = _dot_nt(q_heads[n].astype(MXU), kc) - slopes[n] * dist_cf
        sm = jnp.where(mask_c, s, NEG)
        m = jnp.max(sm, axis=1, keepdims=True)
        p = jnp.where(mask_c, jnp.exp(sm - m), 0.0)
        p = (p / jnp.maximum(jnp.sum(p, axis=1, keepdims=True), 1e-30)).astype(MXU)
        o_cmp.append(_dot(p, vc))
        imp_t = imp_t + _dot_nt(cover_ref[...], p)

    rows = 32
    imp = imp_t[:rows]
    blk = lax.broadcasted_iota(jnp.int32, (rows, tq), 0)
    blk_f = blk.astype(f32)
    tpos_t = q0 + lax.broadcasted_iota(jnp.int32, (1, tq), 1)
    forced = (blk == 0) | (blk == jnp.right_shift(tpos_t, 6))
    valid = blk * SEL_BLOCK <= tpos_t
    imp = jnp.where(forced, FORCE_SCORE, jnp.where(valid, imp, -1.0))
    imp = jnp.where(blk < n_sel, imp, NEG)
    sel_t = jnp.zeros((rows, tq), f32)
    for _ in range(n_top):
        m = jnp.max(imp, axis=0, keepdims=True)
        first = jnp.min(jnp.where(imp == m, blk_f, float(LANE)), axis=0, keepdims=True)
        pick = blk_f == first
        sel_t = jnp.where(pick, 1.0, sel_t)
        imp = jnp.where(pick, 2 * NEG, imp)
    sel = jnp.concatenate([sel_t, jnp.zeros((LANE - rows, tq), f32)], axis=0).T
    sel_bias = pltpu.roll(jnp.where(sel > 0.5, 0.0, BIG_NEG), _AUG_SEL, 1)
    sel_cols = jnp.where((lane >= _AUG_SEL) & (lane < _AUG_SEL + rows), sel_bias, 0.0)
    q_sel = jnp.concatenate([q_heads[n] + sel_cols + pos_cols[n] for n in range(NSA_GROUP)], axis=0).astype(MXU)
    q_win = jnp.concatenate([q_heads[n] + pos_cols[n] for n in range(NSA_GROUP)], axis=0).astype(MXU)
    rows_q = NSA_GROUP * tq
    denom_lane = lax.broadcasted_iota(jnp.int32, (rows_q, LANE), 1) == HEAD_DIM

    def update(carry, qs, k, v, bias):
        m, acc = carry
        s = _dot_nt(qs, k)
        if bias is not None:
            s = s + bias
        m_new = jnp.maximum(m, jnp.max(s, axis=1, keepdims=True))
        p = jnp.exp(s - m_new)
        return m_new, jnp.exp(m - m_new) * acc + _dot(p.astype(MXU), v)

    def finish(carry):
        _, acc = carry
        denom = jnp.sum(jnp.where(denom_lane, acc, 0.0), axis=1, keepdims=True)
        o = acc / jnp.maximum(denom, 1e-30)
        return [o[n * tq:(n + 1) * tq] for n in range(NSA_GROUP)]

    init = (jnp.full((rows_q, 1), NEG, f32), jnp.zeros((rows_q, LANE), f32))
    causal = cb_ref[...]

    def sel_body(kt, carry):
        k0 = pl.multiple_of(kt * tk, tk)
        return update(carry, q_sel, ksa[pl.ds(k0, tk), :], vsa[pl.ds(k0, tk), :], None)

    carry = lax.fori_loop(0, qi, sel_body, init)
    o_sel = finish(update(carry, q_sel, ksa[pl.ds(q0, tk), :], vsa[pl.ds(q0, tk), :], causal))

    carry = init
    n_back = WINDOW // tk
    for back in range(n_back, -1, -1):
        k0 = pl.multiple_of(jnp.maximum(qi - back, 0) * tk, tk)
        if back:
            off = jnp.where(qi >= back, 0.0, BIG_NEG)
            bias = wb_ref[...] + off if back == n_back else off
        else:
            bias = causal
        carry = update(carry, q_win, kwa[pl.ds(k0, tk), :], vwa[pl.ds(k0, tk), :], bias)
    o_win = finish(carry)

    sig = jax.nn.sigmoid(gl_ref[0])

    def gate(n, j):
        col = 3 * (NSA_GROUP * g + n) + j
        return jnp.sum(jnp.where(lane == col, sig, 0.0), axis=1, keepdims=True)

    o = [gate(n, 0) * o_cmp[n] + gate(n, 1) * o_sel[n] + gate(n, 2) * o_win[n] for n in range(NSA_GROUP)]
    o_ref[0] = jnp.where(lo_half, o[0], pltpu.roll(o[1], HEAD_DIM, 1))


def nsa_attention(h, cmp_pos, cmp_w1, cmp_w2, tq=512):
    B, S, _ = h.shape
    G = NSA_KV_HEADS
    NC = S // CMP_STRIDE
    n_cmp = (S - CMP_LEN) // CMP_STRIDE + 1
    n_sel = S // SEL_BLOCK
    n_top = min(SEL_TOPK, n_sel)
    F = CMP_STRIDE * HEAD_DIM
    assert S % tq == 0 and WINDOW % tq == 0 and tq % SEL_BLOCK == 0 and n_sel <= 32 and S <= _POS_SPLIT ** 2

    def chunks(off):
        t = h[:, :, off:off + G * HEAD_DIM].reshape(B, NC, CMP_STRIDE, G, HEAD_DIM)
        return t.transpose(0, 3, 1, 2, 4).reshape(B * G, NC, F)

    t = jnp.stack([chunks(OFF["kc"]), chunks(OFF["vc"])])
    w2 = jnp.concatenate([cmp_w2, cmp_w2], axis=-1).astype(MXU)
    kvc = nsa_compress(t, cmp_w1.astype(MXU), cmp_pos.reshape(2, 1, 2 * F), w2)

    c0 = np.arange(n_cmp)[:, None] * CMP_STRIDE
    j0 = np.arange(n_sel)[None, :] * SEL_BLOCK
    cover = np.clip(np.minimum(c0 + CMP_LEN, j0 + SEL_BLOCK) - np.maximum(c0, j0), 0, None) / CMP_LEN
    cover_t = np.zeros((LANE, NC), np.float32)
    cover_t[:n_sel, :n_cmp] = cover.T
    pos = np.arange(S)
    aug = np.zeros((S, LANE), np.float32)
    aug[pos, _AUG_SEL + pos // SEL_BLOCK] = 1.0
    aug[:, _AUG_POS] = pos // _POS_SPLIT
    aug[:, _AUG_POS + 1] = pos % _POS_SPLIT
    rel = np.arange(tq)[:, None] - np.arange(tq)[None, :]
    rel = np.tile(rel, (NSA_GROUP, 1))
    causal = np.where(rel >= 0, 0.0, BIG_NEG).astype(np.float32)
    win_lo = np.where(rel < 0, 0.0, BIG_NEG).astype(np.float32)

    col = lambda name: OFF[name] // LANE
    full2 = lambda shape: pl.BlockSpec(shape, lambda b, g, i: (0, 0))
    q_spec = pl.BlockSpec((1, tq, LANE), lambda b, g, i: (b, i, col("aq") + g))
    c_specs = [pl.BlockSpec((1, 1, NC, LANE), lambda b, g, i, j=j: (j, b * G + g, 0, 0)) for j in range(2)]
    kv_specs = [pl.BlockSpec((1, S, LANE), lambda b, g, i, c=col(n): (b, 0, c + g))
                for n in ("ks", "vs", "kw", "vw")]
    gl_spec = pl.BlockSpec((1, tq, LANE), lambda b, g, i: (b, i, col("ag")))
    return pl.pallas_call(
        functools.partial(_nsa_kernel, tq=tq, n_cmp=n_cmp, n_sel=n_sel, n_top=n_top),
        grid=(B, G, S // tq),
        in_specs=[q_spec] + c_specs + kv_specs + [gl_spec, full2((LANE, NC)), full2((S, LANE)),
                                                  full2((NSA_GROUP * tq, tq)), full2((NSA_GROUP * tq, tq))],
        out_specs=pl.BlockSpec((1, tq, LANE), lambda b, g, i: (b, i, g)),
        out_shape=jax.ShapeDtypeStruct((B, S, NSA_HEADS * HEAD_DIM), f32),
        scratch_shapes=[pltpu.VMEM((S, LANE), MXU)] * 4,
        compiler_params=_params(3),
        name="nsa_attention",
    )(h, kvc, kvc, h, h, h, h, h, jnp.asarray(cover_t, MXU), jnp.asarray(aug, MXU), jnp.asarray(causal),
      jnp.asarray(win_lo))


_SB_DEAD = -104.0


def _log_sigmoid(z):
    return jnp.minimum(z, 0.0) - jnp.log1p(jnp.exp(-jnp.abs(z)))


def _sb_kernel(q_ref, k_ref, v_ref, u_ref, o_ref, kb, vb, *, tq):
    tk = tq
    qi = pl.program_id(2)
    q0 = pl.multiple_of(qi * tq, tq)
    lane = lax.broadcasted_iota(jnp.int32, (tq, LANE), 1)
    lo_half = lane < HEAD_DIM

    @pl.when(qi == 0)
    def _():
        kb[...] = k_ref[0].astype(MXU)
        vb[...] = v_ref[0].astype(MXU)

    q = q_ref[0] * (HEAD_DIM ** -0.5)
    qh = (jnp.where(lo_half, q, 0.0).astype(MXU), jnp.where(lo_half, 0.0, q).astype(MXU))
    u = u_ref[...]

    def tile(carry, qn, k, v, strict):
        c, acc = carry
        z = _dot_nt(qn, k)
        ls = _log_sigmoid(z)
        log_1m = ls - z
        if strict is not None:
            log_1m = jnp.where(strict, log_1m, 0.0)
        hi = log_1m.astype(MXU)
        lo = (log_1m - hi.astype(f32)).astype(MXU)
        tail = _dot(hi, u) + _dot(lo, u) + c
        a = jnp.exp(ls + tail)
        if strict is not None:
            a = jnp.where(strict, a, 0.0)
        return c + jnp.sum(log_1m, axis=1, keepdims=True), acc + _dot(a.astype(MXU), v)

    rel = lax.broadcasted_iota(jnp.int32, (tq, tk), 0) - lax.broadcasted_iota(jnp.int32, (tq, tk), 1)
    k = kb[pl.ds(q0, tk), :]
    v = vb[pl.ds(q0, tk), :]
    zero = (jnp.zeros((tq, 1), f32), jnp.zeros((tq, LANE), f32))
    heads = tuple(tile(zero, qh[n], k, v, rel > 0) for n in range(2))

    def alive(heads):
        return (jnp.max(jnp.maximum(heads[0][0], heads[1][0])) > _SB_DEAD).astype(jnp.int32)

    def cond(state):
        kt, live, _ = state
        return (kt >= 0) & (live > 0)

    def body(state):
        kt, _, heads = state
        k0 = pl.multiple_of(kt * tk, tk)
        k = kb[pl.ds(k0, tk), :]
        v = vb[pl.ds(k0, tk), :]
        heads = tuple(tile(heads[n], qh[n], k, v, None) for n in range(2))
        return kt - 1, alive(heads), heads

    _, _, heads = lax.while_loop(cond, body, (qi - 1, alive(heads), heads))
    o_ref[0] = jnp.where(lo_half, heads[0][1], heads[1][1])


def stick_breaking_attention(h, tq=256):
    B, S, _ = h.shape
    assert S % tq == 0
    tri = jnp.asarray(np.tril(np.ones((tq, tq), np.float32), -1), MXU)
    cq, ck, cv = (OFF[n] // LANE for n in ("bq", "bk", "bv"))
    return pl.pallas_call(
        functools.partial(_sb_kernel, tq=tq),
        grid=(B, SB_HEADS // 2, S // tq),
        in_specs=[pl.BlockSpec((1, tq, LANE), lambda b, p, i: (b, i, cq + p)),
                  pl.BlockSpec((1, S, LANE), lambda b, p, i: (b, 0, ck + p)),
                  pl.BlockSpec((1, S, LANE), lambda b, p, i: (b, 0, cv + p)),
                  pl.BlockSpec((tq, tq), lambda b, p, i: (0, 0))],
        out_specs=pl.BlockSpec((1, tq, LANE), lambda b, p, i: (b, i, p)),
        out_shape=jax.ShapeDtypeStruct((B, S, SB_HEADS * HEAD_DIM), f32),
        scratch_shapes=[pltpu.VMEM((S, LANE), MXU)] * 2,
        compiler_params=_params(3),
        name="sb_attention",
    )(h, h, h, tri)


def _neg_expm1(y):
    series = -y * (1.0 + y * (1.0 / 2 + y * (1.0 / 6 + y * (1.0 / 24 + y * (1.0 / 120)))))
    return jnp.where(y > -0.1, series, 1.0 - jnp.exp(y))


def _rglru_kernel(x_ref, xg_ref, cw_ref, cb_ref, gaw_ref, gab_ref, gxw_ref, gxb_ref, lam_ref, o_ref):
    x = x_ref[0]
    S = x.shape[0]
    row = lax.broadcasted_iota(jnp.int32, (S, 1), 0)

    def shifted(t, d, fill):
        return jnp.where(row >= d, pltpu.roll(t, d, 0), fill)

    u = cb_ref[...] + x * cw_ref[CONV_W - 1:CONV_W, :]
    for d in range(1, CONV_W):
        u = u + shifted(x, d, 0.0) * cw_ref[CONV_W - 1 - d:CONV_W - d, :]
    ub = u.astype(MXU)
    r = jax.nn.sigmoid(_dot(ub, gaw_ref[...]) + gab_ref[...])
    i = jax.nn.sigmoid(_dot(ub, gxw_ref[...]) + gxb_ref[...])
    lam = lam_ref[...]
    softplus_neg = jnp.maximum(-lam, 0.0) + jnp.log1p(jnp.exp(-jnp.abs(lam)))
    log_a = -LRU_C * r * softplus_neg
    a = jnp.exp(log_a)
    b = jnp.sqrt(_neg_expm1(2.0 * log_a)) * (i * u)
    d = 1
    while d < S:
        b = a * shifted(b, d, 0.0) + b
        a = a * shifted(a, d, 1.0)
        d *= 2
    o_ref[0] = b * jax.nn.gelu(xg_ref[0])


def _block_diag(w):
    n, c, _ = w.shape
    out = jnp.zeros((n * c, n * c), w.dtype)
    for j in range(n):
        out = out.at[j * c:(j + 1) * c, j * c:(j + 1) * c].set(w[j])
    return out


def rglru_block(h, conv_w, conv_b, ga_w, ga_b, gx_w, gx_b, lru_lambda):
    B, S, _ = h.shape
    W = RNN_W
    cx, cg = OFF["cx"] // W, OFF["cg"] // W
    vec = pl.BlockSpec((1, W), lambda b: (0, 0))
    mat = pl.BlockSpec((W, W), lambda b: (0, 0))
    return pl.pallas_call(
        _rglru_kernel,
        grid=(B,),
        in_specs=[pl.BlockSpec((1, S, W), lambda b: (b, 0, cx)), pl.BlockSpec((1, S, W), lambda b: (b, 0, cg)),
                  pl.BlockSpec((CONV_W, W), lambda b: (0, 0)), vec, mat, vec, mat, vec, vec],
        out_specs=pl.BlockSpec((1, S, W), lambda b: (b, 0, 0)),
        out_shape=jax.ShapeDtypeStruct((B, S, W), f32),
        compiler_params=_params(1),
        name="rglru",
    )(h, h, conv_w, conv_b.reshape(1, W), _block_diag(ga_w).astype(MXU), ga_b.reshape(1, W),
      _block_diag(gx_w).astype(MXU), gx_b.reshape(1, W), lru_lambda.reshape(1, W))


def _rms(x, g, width):
    return x * lax.rsqrt(jnp.sum(x * x, axis=-1, keepdims=True) * (1.0 / width) + RMS_EPS) * g


def _mla_prep_kernel(cq_ref, ckv_ref, kr_ref, krs_ref, gq_ref, gkv_ref, wq_ref, wqs_ref, wk_ref, wv_ref,
                     cosq_ref, sinq_ref, cosk_ref, sink_ref, vone_ref, q_ref, k_ref, v_ref):
    cq = _rms(cq_ref[0], gq_ref[...], MLA_Q_RANK).astype(MXU)
    ckv = _rms(ckv_ref[0], gkv_ref[...], MLA_KV_RANK).astype(MXU)
    scale = (MLA_NOPE + MLA_ROPE) ** -0.5
    q = _dot(cq, wq_ref[...]) * cosq_ref[...] + _dot(cq, wqs_ref[...]) * sinq_ref[...]
    q_ref[0] = (q * scale).astype(q_ref.dtype)
    k_rope = kr_ref[0] * cosk_ref[...] + krs_ref[0] * sink_ref[...]
    k = _dot(ckv, wk_ref[...])
    k_ref[0] = (k + jnp.concatenate([k_rope] * MLA_HEADS, axis=1)).astype(k_ref.dtype)
    v_ref[0] = (_dot(ckv, wv_ref[...]) + vone_ref[...]).astype(v_ref.dtype)


def _mla_attn_kernel(q_ref, k_ref, v_ref, cb_ref, o_ref, *, tq):
    tk = tq
    qi = pl.program_id(2)
    q0 = pl.multiple_of(qi * tq, tq)
    q = q_ref[0]
    lane = lax.broadcasted_iota(jnp.int32, (tq, LANE), 1)

    def update(carry, n, k, v, bias):
        m, acc = carry
        sl = slice(n * LANE, (n + 1) * LANE)
        s = _dot_nt(q[:, sl], k[:, sl])
        if bias is not None:
            s = s + bias
        m_new = jnp.maximum(m, jnp.max(s, axis=1, keepdims=True))
        p = jnp.exp(s - m_new)
        return m_new, jnp.exp(m - m_new) * acc + _dot(p.astype(MXU), v[:, sl])

    def body(kt, carry):
        k0 = pl.multiple_of(kt * tk, tk)
        k = k_ref[0, pl.ds(k0, tk), :]
        v = v_ref[0, pl.ds(k0, tk), :]
        return tuple(update(carry[n], n, k, v, None) for n in range(2))

    init = tuple((jnp.full((tq, 1), NEG, f32), jnp.zeros((tq, LANE), f32)) for _ in range(2))
    carry = lax.fori_loop(0, qi, body, init)
    k = k_ref[0, pl.ds(q0, tk), :]
    v = v_ref[0, pl.ds(q0, tk), :]
    o = []
    for n in range(2):
        _, acc = update(carry[n], n, k, v, cb_ref[...])
        denom = jnp.sum(jnp.where(lane == MLA_V, acc, 0.0), axis=1, keepdims=True)
        o.append(acc / jnp.maximum(denom, 1e-30))
    o_ref[0] = jnp.where(lane < MLA_V, o[0], pltpu.roll(o[1], MLA_V, 1))


def mla_attention(h, q_norm, kv_norm, w_uq, w_ukv, tr=512, tq=512):
    B, S, _ = h.shape
    H = MLA_HEADS
    dq = MLA_NOPE + MLA_ROPE
    HW = H * LANE
    wq3 = w_uq.reshape(MLA_Q_RANK, H, dq)
    wq_rot = jnp.concatenate([jnp.zeros_like(wq3[..., :MLA_NOPE]), _rot_half(wq3[..., MLA_NOPE:])], axis=-1)

    def pad_q(w3):
        w3 = jnp.pad(w3, ((0, 256 - MLA_Q_RANK), (0, 0), (0, LANE - dq)))
        return w3.reshape(256, HW).astype(MXU)

    wkv3 = w_ukv.reshape(MLA_KV_RANK, H, MLA_NOPE + MLA_V)
    wk = jnp.pad(wkv3[..., :MLA_NOPE], ((0, 0), (0, 0), (0, LANE - MLA_NOPE))).reshape(MLA_KV_RANK, HW).astype(MXU)
    wv = jnp.pad(wkv3[..., MLA_NOPE:], ((0, 0), (0, 0), (0, LANE - MLA_V))).reshape(MLA_KV_RANK, HW).astype(MXU)
    v_one = jnp.tile(jnp.concatenate([jnp.zeros((1, MLA_V), f32), jnp.ones((1, LANE - MLA_V), f32)], axis=1), (1, H))
    gq = jnp.pad(q_norm, (0, 256 - MLA_Q_RANK)).reshape(1, 256)
    gkv = kv_norm.reshape(1, MLA_KV_RANK)
    inv = ROPE_THETA ** (-jnp.arange(0, MLA_ROPE, 2, dtype=f32) / MLA_ROPE)
    ang = jnp.arange(S, dtype=f32)[:, None] * inv[None, :]
    cos2 = jnp.concatenate([jnp.cos(ang), jnp.cos(ang)], axis=1)
    sin2 = jnp.concatenate([jnp.sin(ang), jnp.sin(ang)], axis=1)
    tail = LANE - dq
    cos_k = jnp.concatenate([jnp.zeros((S, MLA_NOPE), f32), cos2, jnp.zeros((S, tail), f32)], axis=1)
    sin_k = jnp.concatenate([jnp.zeros((S, MLA_NOPE), f32), sin2, jnp.zeros((S, tail), f32)], axis=1)
    cos_q = jnp.tile(jnp.concatenate([jnp.ones((S, MLA_NOPE), f32), cos2, jnp.zeros((S, tail), f32)], axis=1), (1, H))
    sin_q = jnp.tile(sin_k, (1, H))
    rel = np.arange(tq)[:, None] - np.arange(tq)[None, :]
    causal = jnp.asarray(np.where(rel >= 0, 0.0, BIG_NEG).astype(np.float32))

    c_cq, c_ckv, c_kr, c_krs = OFF["dcq"] // 256, OFF["dckv"] // LANE, OFF["dkr"] // LANE, OFF["dkrs"] // LANE
    full = lambda shape: pl.BlockSpec(shape, lambda b, i: (0, 0))
    tab = lambda w: pl.BlockSpec((tr, w), lambda b, i: (i, 0))
    out3 = pl.BlockSpec((1, tr, HW), lambda b, i: (b, i, 0))
    q, k, v = pl.pallas_call(
        _mla_prep_kernel,
        grid=(B, S // tr),
        in_specs=[pl.BlockSpec((1, tr, 256), lambda b, i: (b, i, c_cq)),
                  pl.BlockSpec((1, tr, LANE), lambda b, i: (b, i, c_ckv)),
                  pl.BlockSpec((1, tr, LANE), lambda b, i: (b, i, c_kr)),
                  pl.BlockSpec((1, tr, LANE), lambda b, i: (b, i, c_krs)),
                  full((1, 256)), full((1, MLA_KV_RANK)), full((256, HW)), full((256, HW)),
                  full((MLA_KV_RANK, HW)), full((MLA_KV_RANK, HW)),
                  tab(HW), tab(HW), tab(LANE), tab(LANE), full((1, HW))],
        out_specs=[out3, out3, out3],
        out_shape=[jax.ShapeDtypeStruct((B, S, HW), MXU)] * 3,
        compiler_params=_params(2),
        name="mla_prep",
    )(h, h, h, h, gq, gkv, pad_q(wq3), pad_q(wq_rot), wk, wv, cos_q, sin_q, cos_k, sin_k, v_one)
    pair = lambda rows: pl.BlockSpec((1, rows, 2 * LANE), lambda b, p, i: (b, i if rows == tq else 0, p))
    return pl.pallas_call(
        functools.partial(_mla_attn_kernel, tq=tq),
        grid=(B, H // 2, S // tq),
        in_specs=[pair(tq), pair(S), pair(S), pl.BlockSpec((tq, tq), lambda b, p, i: (0, 0))],
        out_specs=pl.BlockSpec((1, tq, LANE), lambda b, p, i: (b, i, p)),
        out_shape=jax.ShapeDtypeStruct((B, S, H * MLA_V), f32),
        compiler_params=_params(3),
        name="mla_attention",
    )(q, k, v, causal)


def _ln(z, g, b):
    mu = jnp.mean(z, axis=-1, keepdims=True)
    zc = z - mu
    var = jnp.mean(zc * zc, axis=-1, keepdims=True)
    return zc * lax.rsqrt(var + LN_EPS) * g + b


def _merge_kernel(mg_ref, oa_ref, ob_ref, oc_ref, od_ref, wb_ref, wo_ref, x_ref, g_ref, b_ref, o_ref):
    acc = None
    for n, br in enumerate((oa_ref, ob_ref, oc_ref, od_ref)):
        up = _dot(br[...].astype(MXU), wb_ref[n])
        term = jax.nn.sigmoid(mg_ref[:, n * D_MODEL:(n + 1) * D_MODEL]) * up
        acc = term if acc is None else acc + term
    y = _dot(acc.astype(MXU), wo_ref[...])
    o_ref[...] = _ln(DN_ALPHA * x_ref[...] + y, g_ref[...], b_ref[...])


def merge_ln(h2, branches, w_branch, w_out, x2, g, b, tm=256):
    N, D = x2.shape
    assert OFF["mg"] == 0 and N % tm == 0
    row = lambda w: pl.BlockSpec((tm, w), lambda i: (i, 0))
    return pl.pallas_call(
        _merge_kernel,
        grid=(N // tm,),
        in_specs=[row(N_MIXERS * D)] + [row(MIX_W)] * N_MIXERS
        + [pl.BlockSpec((N_MIXERS, MIX_W, D), lambda i: (0, 0, 0)), pl.BlockSpec((D, D), lambda i: (0, 0)),
           row(D), pl.BlockSpec((1, D), lambda i: (0, 0)), pl.BlockSpec((1, D), lambda i: (0, 0))],
        out_specs=row(D),
        out_shape=jax.ShapeDtypeStruct((N, D), f32),
        compiler_params=_params(1),
        name="merge_ln",
    )(h2, *branches, w_branch.astype(MXU), w_out.astype(MXU), x2, g.reshape(1, D), b.reshape(1, D))


ROW_TILE = 8


def _to_token_tiles(ref, val):
    rows = val.shape[0]
    for j in range(ROW_TILE):
        ref[pl.ds(j, rows, stride=ROW_TILE), :] = val[:, j * LANE:(j + 1) * LANE]


def _from_token_tiles(ref, rows, first=0, stride=ROW_TILE):
    return jnp.concatenate([ref[pl.ds(first + j, rows, stride=stride), :] for j in range(ROW_TILE)], axis=1)


def _xattn_kernel(x_ref, wq_ref, k_ref, v_ref, wo_ref, g_ref, b_ref, o_ref, o8_ref):
    x = x_ref[0]
    q = _dot(x.astype(MXU), wq_ref[...]).astype(MXU)
    k = k_ref[0]
    v = v_ref[0]
    heads = []
    for hd in range(X_HEADS):
        sl = slice(hd * X_HEAD_DIM, (hd + 1) * X_HEAD_DIM)
        s = _dot_nt(q[:, sl], k[:, sl]) * (X_HEAD_DIM ** -0.5)
        e = jnp.exp(s - jnp.max(s, axis=1, keepdims=True))
        p = e / jnp.sum(e, axis=1, keepdims=True)
        heads.append(_dot(p.astype(MXU), v[:, sl]).astype(MXU))
    y = _dot(jnp.concatenate(heads, axis=1), wo_ref[...])
    out = _ln(DN_ALPHA * x + y, g_ref[...], b_ref[...])
    o_ref[0] = out
    _to_token_tiles(o8_ref.at[0], out)


def cross_attention_ln(x, mem, wq, wkv, wo, g, b, tq=256):
    B, S, D = x.shape
    assert D == ROW_TILE * LANE
    M = mem.shape[1]
    F = X_HEADS * X_HEAD_DIM
    kv = matmul(mem.reshape(B * M, D), wkv.astype(MXU), 512, 2 * F, out_dtype=MXU).reshape(B, M, 2 * F)
    full = lambda shape: pl.BlockSpec(shape, lambda bi, i: (0,) * len(shape))
    return pl.pallas_call(
        _xattn_kernel,
        grid=(B, S // tq),
        in_specs=[pl.BlockSpec((1, tq, D), lambda bi, i: (bi, i, 0)), full((D, F)),
                  pl.BlockSpec((1, M, F), lambda bi, i: (bi, 0, 0)), pl.BlockSpec((1, M, F), lambda bi, i: (bi, 0, 1)),
                  full((F, D)), full((1, D)), full((1, D))],
        out_specs=[pl.BlockSpec((1, tq, D), lambda bi, i: (bi, i, 0)),
                   pl.BlockSpec((1, tq * ROW_TILE, LANE), lambda bi, i: (bi, i, 0))],
        out_shape=[jax.ShapeDtypeStruct((B, S, D), f32), jax.ShapeDtypeStruct((B, S * ROW_TILE, LANE), f32)],
        compiler_params=_params(2),
        name="cross_attention_ln",
    )(x, wq.astype(MXU), kv, kv, wo.astype(MXU), g.reshape(1, D), b.reshape(1, D))


_R_E0, _R_E1, _R_W0, _R_W1, _R_RANK0, _R_RANK1 = range(6)
_GRP_LANE0 = N_EXPERTS


def _router_kernel(x_ref, w_ref, b_ref, tri_ref, r_ref, cnt_ref):
    i = pl.program_id(0)
    tm = x_ref.shape[0]
    logits = _dot(x_ref[...].astype(MXU), w_ref[...]) + b_ref[...]
    lane = lax.broadcasted_iota(jnp.int32, (tm, LANE), 1)
    lane_f = lane.astype(f32)
    big = float(LANE)

    def rmax(t):
        return jnp.max(t, axis=1, keepdims=True)

    def first_lane(cond):
        return jnp.min(jnp.where(cond, lane_f, big), axis=1, keepdims=True)

    def softmax_on(mask):
        lm = jnp.where(mask, logits, NEG)
        e = jnp.where(mask, jnp.exp(lm - rmax(lm)), 0.0)
        return e / jnp.sum(e, axis=1, keepdims=True)

    is_g = (lane >= _GRP_LANE0) & (lane < _GRP_LANE0 + N_GROUPS)
    p_grp = softmax_on(is_g)
    p_g = rmax(p_grp)
    grp = first_lane(is_g & (p_grp == p_g)) - float(_GRP_LANE0)
    in_grp = (lane < N_EXPERTS) & (jnp.right_shift(lane, 3).astype(f32) == grp)
    p_e = softmax_on(in_grp)
    p1 = rmax(jnp.where(in_grp, p_e, -1.0))
    e1 = first_lane(in_grp & (p_e == p1))
    rest = in_grp & (lane_f != e1)
    p2 = rmax(jnp.where(rest, p_e, -1.0))
    e2 = first_lane(rest & (p_e == p2))
    w1 = p_g * p1 / (p1 + p2)
    w2 = p_g * p2 / (p1 + p2)

    @pl.when(i == 0)
    def _():
        cnt_ref[...] = jnp.zeros_like(cnt_ref)

    oh1 = lane_f == e1
    oh2 = lane_f == e2
    both = (oh1 | oh2).astype(MXU)
    before = _dot(tri_ref[...], both) + cnt_ref[0:1, :]
    rank1 = jnp.sum(jnp.where(oh1, before, 0.0), axis=1, keepdims=True)
    rank2 = jnp.sum(jnp.where(oh2, before, 0.0), axis=1, keepdims=True)
    cnt_ref[...] = cnt_ref[...] + jnp.sum(both.astype(f32), axis=0, keepdims=True)

    out = jnp.zeros((tm, LANE), f32)
    for slot, val in ((_R_E0, e1), (_R_E1, e2), (_R_W0, w1), (_R_W1, w2), (_R_RANK0, rank1), (_R_RANK1, rank2)):
        out = jnp.where(lane == slot, val, out)
    r_ref[...] = out


def moe_router(x2, rg_w, rg_b, re_w, re_b, tm=256):
    N, D = x2.shape
    assert EXPERTS_PER_GROUP == 8 and N_EXPERTS + N_GROUPS <= LANE
    w = jnp.pad(jnp.concatenate([re_w, rg_w], axis=1), ((0, 0), (0, LANE - N_EXPERTS - N_GROUPS))).astype(MXU)
    b = jnp.pad(jnp.concatenate([re_b, rg_b]), (0, LANE - N_EXPERTS - N_GROUPS)).reshape(1, LANE)
    tri = jnp.asarray(np.tril(np.ones((tm, tm), np.float32), -1), MXU)
    return pl.pallas_call(
        _router_kernel,
        grid=(N // tm,),
        in_specs=[pl.BlockSpec((tm, D), lambda i: (i, 0)), pl.BlockSpec((D, LANE), lambda i: (0, 0)),
                  pl.BlockSpec((1, LANE), lambda i: (0, 0)), pl.BlockSpec((tm, tm), lambda i: (0, 0))],
        out_specs=[pl.BlockSpec((tm, LANE), lambda i: (i, 0)), pl.BlockSpec((8, LANE), lambda i: (0, 0))],
        out_shape=[jax.ShapeDtypeStruct((N, LANE), f32), jax.ShapeDtypeStruct((8, LANE), f32)],
        compiler_params=_params(1),
        name="moe_router",
    )(x2, w, b, tri)


def _ffn_kernel(ce_ref, nu_ref, x_ref, wgu_ref, wd_ref, o_ref):
    c = pl.program_id(0)

    @pl.when(c < nu_ref[0])
    def _():
        x = _from_token_tiles(x_ref, EXPERT_CHUNK)
        gu = _dot(x.astype(MXU), wgu_ref[0, 0].astype(MXU))
        hid = jax.nn.silu(gu[:, :D_EXPERT]) * gu[:, D_EXPERT:]
        _to_token_tiles(o_ref, _dot(hid.astype(MXU), wd_ref[0, 0].astype(MXU)))

    @pl.when(c >= nu_ref[0])
    def _():
        o_ref[...] = jnp.zeros_like(o_ref)


def expert_ffn(xb8, chunk_e, n_used, w_gu, w_down, layer):
    C = EXPERT_CHUNK
    n_chunks = xb8.shape[0] // (C * ROW_TILE)
    D = ROW_TILE * LANE
    tile_spec = pl.BlockSpec((C * ROW_TILE, LANE), lambda c, ce, nu: (c, 0))
    grid_spec = pltpu.PrefetchScalarGridSpec(
        num_scalar_prefetch=2,
        grid=(n_chunks,),
        in_specs=[tile_spec,
                  pl.BlockSpec((1, 1, D, 2 * D_EXPERT), lambda c, ce, nu: (layer, ce[c], 0, 0)),
                  pl.BlockSpec((1, 1, D_EXPERT, D), lambda c, ce, nu: (layer, ce[c], 0, 0))],
        out_specs=tile_spec,
    )
    return pl.pallas_call(
        _ffn_kernel,
        grid_spec=grid_spec,
        out_shape=jax.ShapeDtypeStruct(xb8.shape, f32),
        compiler_params=_params(1),
        name="expert_ffn",
    )(chunk_e, n_used, xb8, w_gu, w_down)


_COPY_WINDOW = 256


def _windowed_copies(n, copy, wait):
    def fill(j, carry):
        copy(j).start()
        return carry

    def steady(j, carry):
        wait(j - _COPY_WINDOW)
        copy(j).start()
        return carry

    def drain(j, carry):
        wait(j)
        return carry

    lax.fori_loop(0, _COPY_WINDOW, fill, 0, unroll=8)
    lax.fori_loop(_COPY_WINDOW, n, steady, 0, unroll=8)
    lax.fori_loop(n - _COPY_WINDOW, n, drain, 0, unroll=8)


def _dispatch_kernel(dest_ref, x8_ref, init_hbm, xb_hbm, sems, *, tm):
    n = tm * TOPK_IN_GROUP
    base = pl.program_id(0) * n

    def copy(j):
        t = pl.multiple_of((j // TOPK_IN_GROUP) * ROW_TILE, ROW_TILE)
        d = pl.multiple_of(dest_ref[base + j] * ROW_TILE, ROW_TILE)
        return pltpu.make_async_copy(x8_ref.at[pl.ds(t, ROW_TILE)], xb_hbm.at[pl.ds(d, ROW_TILE)],
                                     sems.at[j % _COPY_WINDOW])

    def wait(j):
        pltpu.make_async_copy(x8_ref.at[pl.ds(0, ROW_TILE)], xb_hbm.at[pl.ds(0, ROW_TILE)],
                              sems.at[j % _COPY_WINDOW]).wait()

    _windowed_copies(n, copy, wait)


def moe_dispatch(x8, dest, n_slots, tm=256):
    n_tok = x8.shape[0] // ROW_TILE
    assert n_tok % tm == 0 and tm * TOPK_IN_GROUP >= _COPY_WINDOW
    grid_spec = pltpu.PrefetchScalarGridSpec(
        num_scalar_prefetch=1,
        grid=(n_tok // tm,),
        in_specs=[pl.BlockSpec((tm * ROW_TILE, LANE), lambda i, dest: (i, 0)), pl.BlockSpec(memory_space=pl.ANY)],
        out_specs=pl.BlockSpec(memory_space=pl.ANY),
        scratch_shapes=[pltpu.SemaphoreType.DMA((_COPY_WINDOW,))],
    )
    return pl.pallas_call(
        functools.partial(_dispatch_kernel, tm=tm),
        grid_spec=grid_spec,
        out_shape=jax.ShapeDtypeStruct((n_slots * ROW_TILE, LANE), x8.dtype),
        input_output_aliases={2: 0},
        compiler_params=_params(1),
        name="moe_dispatch",
    )(dest, x8, jnp.zeros((n_slots * ROW_TILE, LANE), x8.dtype))


def _combine_kernel(dest_ref, x_ref, r_ref, g_ref, b_ref, yb_hbm, o_ref, buf, sems):
    tm = x_ref.shape[0]
    n = tm * TOPK_IN_GROUP
    base = pl.program_id(0) * n

    def copy(j):
        s = pl.multiple_of(dest_ref[base + j] * ROW_TILE, ROW_TILE)
        return pltpu.make_async_copy(yb_hbm.at[pl.ds(s, ROW_TILE)], buf.at[pl.ds(pl.multiple_of(j * ROW_TILE, ROW_TILE),
                                                                                 ROW_TILE)], sems.at[j % _COPY_WINDOW])

    def wait(j):
        pltpu.make_async_copy(yb_hbm.at[pl.ds(0, ROW_TILE)], buf.at[pl.ds(0, ROW_TILE)],
                              sems.at[j % _COPY_WINDOW]).wait()

    _windowed_copies(n, copy, wait)
    r = r_ref[...]
    lane = lax.broadcasted_iota(jnp.int32, r.shape, 1)
    w0 = jnp.sum(jnp.where(lane == _R_W0, r, 0.0), axis=1, keepdims=True)
    w1 = jnp.sum(jnp.where(lane == _R_W1, r, 0.0), axis=1, keepdims=True)
    pair = TOPK_IN_GROUP * ROW_TILE
    y = _from_token_tiles(buf, tm, 0, pair) * w0 + _from_token_tiles(buf, tm, ROW_TILE, pair) * w1
    o_ref[...] = _ln(DN_ALPHA * x_ref[...] + y, g_ref[...], b_ref[...])


def combine_ln(x2, yb8, dest, r, g, b, tm=256):
    N, D = x2.shape
    assert N % tm == 0 and tm * TOPK_IN_GROUP >= _COPY_WINDOW
    row = lambda w: pl.BlockSpec((tm, w), lambda i, dest: (i, 0))
    vec = pl.BlockSpec((1, D), lambda i, dest: (0, 0))
    grid_spec = pltpu.PrefetchScalarGridSpec(
        num_scalar_prefetch=1,
        grid=(N // tm,),
        in_specs=[row(D), row(LANE), vec, vec, pl.BlockSpec(memory_space=pl.ANY)],
        out_specs=row(D),
        scratch_shapes=[pltpu.VMEM((tm * TOPK_IN_GROUP * ROW_TILE, LANE), f32),
                        pltpu.SemaphoreType.DMA((_COPY_WINDOW,))],
    )
    return pl.pallas_call(
        _combine_kernel,
        grid_spec=grid_spec,
        out_shape=jax.ShapeDtypeStruct((N, D), f32),
        compiler_params=_params(1),
        name="moe_combine_ln",
    )(dest, x2, r, g.reshape(1, D), b.reshape(1, D), yb8)


def hier_moe_ln(x2, x8, rg_w, rg_b, re_w, re_b, w_gu, w_down, layer, g, b):
    N, D = x2.shape
    E, C, K = N_EXPERTS, EXPERT_CHUNK, TOPK_IN_GROUP
    A = N * K
    r, cnt = moe_router(x2, rg_w, rg_b, re_w, re_b)
    e = r[:, _R_E0:_R_E1 + 1].astype(jnp.int32)
    rank = r[:, _R_RANK0:_R_RANK1 + 1].astype(jnp.int32)
    counts = cnt[0, :E].astype(jnp.int32)
    padded = (counts + C - 1) // C * C
    pad_end = jnp.cumsum(padded)
    pad_start = pad_end - padded
    dest = pad_start[e] + rank
    n_chunks = -(-(A + E * (C - 1)) // C)
    P = n_chunks * C
    chunk_start = jnp.arange(n_chunks, dtype=jnp.int32) * C
    chunk_e = jnp.minimum(jnp.sum((pad_end[None, :] <= chunk_start[:, None]).astype(jnp.int32), axis=1), E - 1)
    n_used = (pad_end[-1] // C).reshape(1).astype(jnp.int32)
    dest = dest.reshape(A).astype(jnp.int32)
    xb8 = moe_dispatch(x8, dest, P)
    yb8 = expert_ffn(xb8, chunk_e.astype(jnp.int32), n_used, w_gu, w_down, layer)
    return combine_ln(x2, yb8, dest, r, g, b)


def kernel(x, mem, w_in, nsa_cmp_pos, nsa_cmp_w1, nsa_cmp_w2, rnn_conv_w, rnn_conv_b, rnn_ga_w, rnn_ga_b,
           rnn_gx_w, rnn_gx_b, rnn_lambda, mla_q_norm, mla_kv_norm, mla_w_uq, mla_w_ukv, w_branch, w_out,
           ln1_g, ln1_b, x_wq, x_wkv, x_wo, ln2_g, ln2_b, moe_rg_w, moe_rg_b, moe_re_w, moe_re_b,
           moe_w_gu, moe_w_down, ln3_g, ln3_b):
    B, S, D = x.shape
    N = B * S
    x2 = x.reshape(N, D)
    for l in range(DEPTH):
        h2 = matmul(x2, _cat_w_in(w_in[l]), 512, _IN_TILE)
        h = h2.reshape(B, S, N_CAT)
        o_a = nsa_attention(h, nsa_cmp_pos[l], nsa_cmp_w1[l], nsa_cmp_w2[l])
        o_b = stick_breaking_attention(h)
        o_c = rglru_block(h, rnn_conv_w[l], rnn_conv_b[l], rnn_ga_w[l], rnn_ga_b[l], rnn_gx_w[l], rnn_gx_b[l],
                          rnn_lambda[l])
        o_d = mla_attention(h, mla_q_norm[l], mla_kv_norm[l], mla_w_uq[l], mla_w_ukv[l])
        branches = [o.reshape(N, MIX_W) for o in (o_a, o_b, o_c, o_d)]
        x2 = merge_ln(h2, branches, w_branch[l], w_out[l], x2, ln1_g[l], ln1_b[l])
        x3, x8 = cross_attention_ln(x2.reshape(B, S, D), mem, x_wq[l], x_wkv[l], x_wo[l], ln2_g[l], ln2_b[l])
        x2 = hier_moe_ln(x3.reshape(N, D), x8.reshape(N * ROW_TILE, LANE), moe_rg_w[l], moe_rg_b[l], moe_re_w[l],
                         moe_re_b[l], moe_w_gu, moe_w_down, l, ln3_g[l], ln3_b[l])
    return x2.reshape(B, S, D)
```

```python
import functools

import numpy as np
import jax
import jax.numpy as jnp
from jax import lax
from jax.experimental import pallas as pl
from jax.experimental.pallas import tpu as pltpu

D_MODEL = 1024
DEPTH = 4
HEAD_DIM = 64
N_MIXERS = 4
MIX_W = 256
NSA_HEADS = 4
NSA_KV_HEADS = 2
NSA_GROUP = NSA_HEADS // NSA_KV_HEADS
CMP_LEN = 32
CMP_STRIDE = 16
CMP_HID = 256
SEL_BLOCK = 64
SEL_TOPK = 8
WINDOW = 512
FORCE_SCORE = 1e4
SB_HEADS = 4
RNN_W = 256
CONV_W = 4
LRU_C = 8.0
MLA_HEADS = 4
MLA_Q_RANK = 192
MLA_KV_RANK = 128
MLA_NOPE = 64
MLA_ROPE = 32
MLA_V = 64
ROPE_THETA = 10000.0
X_HEADS = 4
X_HEAD_DIM = 128
N_GROUPS = 4
EXPERTS_PER_GROUP = 8
N_EXPERTS = N_GROUPS * EXPERTS_PER_GROUP
TOPK_IN_GROUP = 2
D_EXPERT = 512
EXPERT_CHUNK = 256
DN_ALPHA = (2.0 * DEPTH) ** 0.25
LN_EPS = 1e-5
RMS_EPS = 1e-6

IN_SPLITS = ((NSA_HEADS * HEAD_DIM,) + (NSA_KV_HEADS * HEAD_DIM,) * 6 + (NSA_HEADS * 3,)
             + (SB_HEADS * HEAD_DIM,) * 3
             + (RNN_W, RNN_W)
             + (MLA_Q_RANK, MLA_KV_RANK, MLA_ROPE)
             + (N_MIXERS * D_MODEL,))
IN_OFFSETS = tuple(int(o) for o in np.concatenate([[0], np.cumsum(IN_SPLITS)[:-1]]))

LANE = 128
VMEM_LIMIT = 48 * 1024 * 1024
NEG = -1e30
BIG_NEG = -2.0 ** 100

f32 = jnp.float32
MXU = jnp.bfloat16

_SECTIONS = (("mg", 4096), ("aq", 256), ("ks", 256), ("vs", 256), ("kw", 256), ("vw", 256), ("bq", 256),
             ("bk", 256), ("bv", 256), ("cx", 256), ("cg", 256), ("dcq", 256), ("kc", 128), ("vc", 128),
             ("ag", 128), ("dckv", 128), ("dkr", 128), ("dkrs", 128))
_IN_TILE = 3840
OFF = {}
_o = 0
for _n, _w in _SECTIONS:
    OFF[_n] = _o
    _o += _w
N_CAT = -(-_o // _IN_TILE) * _IN_TILE

_AUG_SEL = HEAD_DIM
_AUG_POS = HEAD_DIM + 32
_POS_SPLIT = 256


def _rot_half(t):
    d = t.shape[-1]
    return jnp.concatenate([-t[..., d // 2:], t[..., :d // 2]], axis=-1)


def _cat_w_in(w):
    def sec(i):
        return w[:, IN_OFFSETS[i]:IN_OFFSETS[i] + IN_SPLITS[i]]

    def spread(t):
        z = jnp.zeros_like(t[:, :HEAD_DIM])
        return jnp.concatenate([t[:, :HEAD_DIM], z, t[:, HEAD_DIM:], z], axis=1)

    def padc(t, n):
        return jnp.pad(t, ((0, 0), (0, n - t.shape[1])))

    def rope_slot(t):
        return jnp.pad(t, ((0, 0), (MLA_NOPE, LANE - MLA_NOPE - MLA_ROPE)))

    parts = {"mg": sec(16), "aq": sec(0), "kc": sec(1), "vc": sec(2), "ks": spread(sec(3)), "vs": spread(sec(4)),
             "kw": spread(sec(5)), "vw": spread(sec(6)), "ag": padc(sec(7), 128), "bq": sec(8), "bk": sec(9),
             "bv": sec(10), "cx": sec(11), "cg": sec(12), "dcq": padc(sec(13), 256), "dckv": sec(14),
             "dkr": rope_slot(sec(15)), "dkrs": rope_slot(_rot_half(sec(15)))}
    cat = jnp.concatenate([parts[n] for n, _ in _SECTIONS], axis=1)
    return padc(cat, N_CAT).astype(MXU)


def _dot(a, b):
    return jnp.dot(a, b, preferred_element_type=f32)


def _dot_nt(a, b):
    return lax.dot_general(a, b, (((1,), (1,)), ((), ())), preferred_element_type=f32)


def _params(n_axes):
    return pltpu.CompilerParams(dimension_semantics=("arbitrary",) * n_axes, vmem_limit_bytes=VMEM_LIMIT)


def _mm_kernel(a_ref, b_ref, o_ref):
    o_ref[...] = _dot(a_ref[...].astype(MXU), b_ref[...]).astype(o_ref.dtype)


def matmul(a, b, tm, tn, out_dtype=f32):
    M, K = a.shape
    _, N = b.shape
    assert M % tm == 0 and N % tn == 0
    return pl.pallas_call(
        _mm_kernel,
        grid=(N // tn, M // tm),
        in_specs=[pl.BlockSpec((tm, K), lambda j, i: (i, 0)),
                  pl.BlockSpec((K, tn), lambda j, i: (0, j))],
        out_specs=pl.BlockSpec((tm, tn), lambda j, i: (i, j)),
        out_shape=jax.ShapeDtypeStruct((M, N), out_dtype),
        compiler_params=_params(2),
        name="matmul",
    )(a, b)


def _cmp_kernel(t_ref, w1_ref, pos_ref, w2_ref, o_ref):
    half = CMP_STRIDE * HEAD_DIM
    t = t_ref[0, 0].astype(MXU)
    y1 = _dot(t, w1_ref[0, :half, :])
    y2 = _dot(t, w1_ref[0, half:, :])
    pos = jnp.broadcast_to(pos_ref[0], (8, 2 * half)).astype(MXU)
    pc = _dot(pos, w1_ref[0])[0:1]
    nc = y2.shape[0]
    hid = y1 + pltpu.roll(y2, nc - 1, 0) + pc
    o_ref[0, 0] = _dot(jax.nn.gelu(hid).astype(MXU), w2_ref[0])


def nsa_compress(t, w1, pos, w2):
    _, BG, NC, F = t.shape
    return pl.pallas_call(
        _cmp_kernel,
        grid=(2, BG),
        in_specs=[pl.BlockSpec((1, 1, NC, F), lambda j, i: (j, i, 0, 0)),
                  pl.BlockSpec((1, 2 * F, CMP_HID), lambda j, i: (j, 0, 0)),
                  pl.BlockSpec((1, 1, 2 * F), lambda j, i: (j, 0, 0)),
                  pl.BlockSpec((1, CMP_HID, LANE), lambda j, i: (j, 0, 0))],
        out_specs=pl.BlockSpec((1, 1, NC, LANE), lambda j, i: (j, i, 0, 0)),
        out_shape=jax.ShapeDtypeStruct((2, BG, NC, LANE), f32),
        compiler_params=_params(2),
        name="nsa_compress",
    )(t, w1, pos, w2)


def _nsa_kernel(q_ref, kc_ref, vc_ref, ks_ref, vs_ref, kw_ref, vw_ref, gl_ref, cover_ref, aug_ref, cb_ref, wb_ref,
                o_ref, ksa, vsa, kwa, vwa, *, tq, n_cmp, n_sel, n_top):
    tk = tq
    g = pl.program_id(1)
    qi = pl.program_id(2)
    q0 = pl.multiple_of(qi * tq, tq)
    lane = lax.broadcasted_iota(jnp.int32, (tq, LANE), 1)
    lo_half = lane < HEAD_DIM

    @pl.when(qi == 0)
    def _():
        real = lax.broadcasted_iota(jnp.int32, ksa.shape, 1) < HEAD_DIM
        aug = aug_ref[...]
        ones = jnp.ones(ksa.shape, MXU)
        ksa[...] = jnp.where(real, ks_ref[0].astype(MXU), aug)
        kwa[...] = jnp.where(real, kw_ref[0].astype(MXU), aug)
        vsa[...] = jnp.where(real, vs_ref[0].astype(MXU), ones)
        vwa[...] = jnp.where(real, vw_ref[0].astype(MXU), ones)

    q = q_ref[0] * (HEAD_DIM ** -0.5)
    q_heads = (jnp.where(lo_half, q, 0.0), jnp.where(lo_half, pltpu.roll(q, HEAD_DIM, 1), 0.0))
    alibi = [2.0 ** (-8.0 * (h + 1) / NSA_HEADS) for h in range(NSA_HEADS)]
    slopes = [jnp.where(g == 0, alibi[n], alibi[NSA_GROUP + n]) for n in range(NSA_GROUP)]
    pos_cols = [jnp.where(lane == _AUG_POS, slopes[n] * _POS_SPLIT, jnp.where(lane == _AUG_POS + 1, slopes[n], 0.0))
                for n in range(NSA_GROUP)]
    tpos = q0 + lax.broadcasted_iota(jnp.int32, (tq, 1), 0)

    nc = kc_ref.shape[2]
    cidx = lax.broadcasted_iota(jnp.int32, (1, nc), 1)
    dist_c = tpos - (cidx * CMP_STRIDE + (CMP_LEN - 1))
    mask_c = (dist_c >= 0) & (cidx < n_cmp)
    dist_cf = dist_c.astype(f32)
    kc = kc_ref[0, 0].astype(MXU)
    vc = vc_ref[0, 0].astype(MXU)
    o_cmp = []
    imp_t = jnp.zeros((LANE, tq), f32)
    for n in range(NSA_GROUP):
        s = _dot_nt(q_heads[n].astype(MXU), kc) - slopes[n] * dist_cf
        sm = jnp.where(mask_c, s, NEG)
        m = jnp.max(sm, axis=1, keepdims=True)
        p = jnp.where(mask_c, jnp.exp(sm - m), 0.0)
        p = (p / jnp.maximum(jnp.sum(p, axis=1, keepdims=True), 1e-30)).astype(MXU)
        o_cmp.append(_dot(p, vc))
        imp_t = imp_t + _dot_nt(cover_ref[...], p)

    rows = 32
    imp = imp_t[:rows]
    blk = lax.broadcasted_iota(jnp.int32, (rows, tq), 0)
    blk_f = blk.astype(f32)
    tpos_t = q0 + lax.broadcasted_iota(jnp.int32, (1, tq), 1)
    forced = (blk == 0) | (blk == jnp.right_shift(tpos_t, 6))
    valid = blk * SEL_BLOCK <= tpos_t
    imp = jnp.where(forced, FORCE_SCORE, jnp.where(valid, imp, -1.0))
    imp = jnp.where(blk < n_sel, imp, NEG)
    sel_t = jnp.zeros((rows, tq), f32)
    for _ in range(n_top):
        m = jnp.max(imp, axis=0, keepdims=True)
        first = jnp.min(jnp.where(imp == m, blk_f, float(LANE)), axis=0, keepdims=True)
        pick = blk_f == first
        sel_t = jnp.where(pick, 1.0, sel_t)
        imp = jnp.where(pick, 2 * NEG, imp)
    sel = jnp.concatenate([sel_t, jnp.zeros((LANE - rows, tq), f32)], axis=0).T
    sel_bias = pltpu.roll(jnp.where(sel > 0.5, 0.0, BIG_NEG), _AUG_SEL, 1)
    sel_cols = jnp.where((lane >= _AUG_SEL) & (lane < _AUG_SEL + rows), sel_bias, 0.0)
    q_sel = jnp.concatenate([q_heads[n] + sel_cols + pos_cols[n] for n in range(NSA_GROUP)], axis=0).astype(MXU)
    q_win = jnp.concatenate([q_heads[n] + pos_cols[n] for n in range(NSA_GROUP)], axis=0).astype(MXU)
    rows_q = NSA_GROUP * tq
    denom_lane = lax.broadcasted_iota(jnp.int32, (rows_q, LANE), 1) == HEAD_DIM

    def update(carry, qs, k, v, bias):
        m, acc = carry
        s = _dot_nt(qs, k)
        if bias is not None:
            s = s + bias
        m_new = jnp.maximum(m, jnp.max(s, axis=1, keepdims=True))
        p = jnp.exp(s - m_new)
        return m_new, jnp.exp(m - m_new) * acc + _dot(p.astype(MXU), v)

    def finish(carry):
        _, acc = carry
        denom = jnp.sum(jnp.where(denom_lane, acc, 0.0), axis=1, keepdims=True)
        o = acc / jnp.maximum(denom, 1e-30)
        return [o[n * tq:(n + 1) * tq] for n in range(NSA_GROUP)]

    init = (jnp.full((rows_q, 1), NEG, f32), jnp.zeros((rows_q, LANE), f32))
    causal = cb_ref[...]

    def sel_body(kt, carry):
        k0 = pl.multiple_of(kt * tk, tk)
        return update(carry, q_sel, ksa[pl.ds(k0, tk), :], vsa[pl.ds(k0, tk), :], None)

    carry = lax.fori_loop(0, qi, sel_body, init)
    o_sel = finish(update(carry, q_sel, ksa[pl.ds(q0, tk), :], vsa[pl.ds(q0, tk), :], causal))

    carry = init
    n_back = WINDOW // tk
    for back in range(n_back, -1, -1):
        k0 = pl.multiple_of(jnp.maximum(qi - back, 0) * tk, tk)
        if back:
            off = jnp.where(qi >= back, 0.0, BIG_NEG)
            bias = wb_ref[...] + off if back == n_back else off
        else:
            bias = causal
        carry = update(carry, q_win, kwa[pl.ds(k0, tk), :], vwa[pl.ds(k0, tk), :], bias)
    o_win = finish(carry)

    sig = jax.nn.sigmoid(gl_ref[0])

    def gate(n, j):
        col = 3 * (NSA_GROUP * g + n) + j
        return jnp.sum(jnp.where(lane == col, sig, 0.0), axis=1, keepdims=True)

    o = [gate(n, 0) * o_cmp[n] + gate(n, 1) * o_sel[n] + gate(n, 2) * o_win[n] for n in range(NSA_GROUP)]
    o_ref[0] = jnp.where(lo_half, o[0], pltpu.roll(o[1], HEAD_DIM, 1))


def nsa_attention(h, cmp_pos, cmp_w1, cmp_w2, tq=512):
    B, S, _ = h.shape
    G = NSA_KV_HEADS
    NC = S // CMP_STRIDE
    n_cmp = (S - CMP_LEN) // CMP_STRIDE + 1
    n_sel = S // SEL_BLOCK
    n_top = min(SEL_TOPK, n_sel)
    F = CMP_STRIDE * HEAD_DIM
    assert S % tq == 0 and WINDOW % tq == 0 and tq % SEL_BLOCK == 0 and n_sel <= 32 and S <= _POS_SPLIT ** 2

    def chunks(off):
        t = h[:, :, off:off + G * HEAD_DIM].reshape(B, NC, CMP_STRIDE, G, HEAD_DIM)
        return t.transpose(0, 3, 1, 2, 4).reshape(B * G, NC, F)

    t = jnp.stack([chunks(OFF["kc"]), chunks(OFF["vc"])])
    w2 = jnp.concatenate([cmp_w2, cmp_w2], axis=-1).astype(MXU)
    kvc = nsa_compress(t, cmp_w1.astype(MXU), cmp_pos.reshape(2, 1, 2 * F), w2)

    c0 = np.arange(n_cmp)[:, None] * CMP_STRIDE
    j0 = np.arange(n_sel)[None, :] * SEL_BLOCK
    cover = np.clip(np.minimum(c0 + CMP_LEN, j0 + SEL_BLOCK) - np.maximum(c0, j0), 0, None) / CMP_LEN
    cover_t = np.zeros((LANE, NC), np.float32)
    cover_t[:n_sel, :n_cmp] = cover.T
    pos = np.arange(S)
    aug = np.zeros((S, LANE), np.float32)
    aug[pos, _AUG_SEL + pos // SEL_BLOCK] = 1.0
    aug[:, _AUG_POS] = pos // _POS_SPLIT
    aug[:, _AUG_POS + 1] = pos % _POS_SPLIT
    rel = np.arange(tq)[:, None] - np.arange(tq)[None, :]
    rel = np.tile(rel, (NSA_GROUP, 1))
    causal = np.where(rel >= 0, 0.0, BIG_NEG).astype(np.float32)
    win_lo = np.where(rel < 0, 0.0, BIG_NEG).astype(np.float32)

    col = lambda name: OFF[name] // LANE
    full2 = lambda shape: pl.BlockSpec(shape, lambda b, g, i: (0, 0))
    q_spec = pl.BlockSpec((1, tq, LANE), lambda b, g, i: (b, i, col("aq") + g))
    c_specs = [pl.BlockSpec((1, 1, NC, LANE), lambda b, g, i, j=j: (j, b * G + g, 0, 0)) for j in range(2)]
    kv_specs = [pl.BlockSpec((1, S, LANE), lambda b, g, i, c=col(n): (b, 0, c + g))
                for n in ("ks", "vs", "kw", "vw")]
    gl_spec = pl.BlockSpec((1, tq, LANE), lambda b, g, i: (b, i, col("ag")))
    return pl.pallas_call(
        functools.partial(_nsa_kernel, tq=tq, n_cmp=n_cmp, n_sel=n_sel, n_top=n_top),
        grid=(B, G, S // tq),
        in_specs=[q_spec] + c_specs + kv_specs + [gl_spec, full2((LANE, NC)), full2((S, LANE)),
                                                  full2((NSA_GROUP * tq, tq)), full2((NSA_GROUP * tq, tq))],
        out_specs=pl.BlockSpec((1, tq, LANE), lambda b, g, i: (b, i, g)),
        out_shape=jax.ShapeDtypeStruct((B, S, NSA_HEADS * HEAD_DIM), f32),
        scratch_shapes=[pltpu.VMEM((S, LANE), MXU)] * 4,
        compiler_params=_params(3),
        name="nsa_attention",
    )(h, kvc, kvc, h, h, h, h, h, jnp.asarray(cover_t, MXU), jnp.asarray(aug, MXU), jnp.asarray(causal),
      jnp.asarray(win_lo))


_SB_DEAD = -104.0


def _log_sigmoid(z):
    return jnp.minimum(z, 0.0) - jnp.log1p(jnp.exp(-jnp.abs(z)))


def _sb_kernel(q_ref, k_ref, v_ref, u_ref, o_ref, kb, vb, *, tq):
    tk = tq
    qi = pl.program_id(2)
    q0 = pl.multiple_of(qi * tq, tq)
    lane = lax.broadcasted_iota(jnp.int32, (tq, LANE), 1)
    lo_half = lane < HEAD_DIM

    @pl.when(qi == 0)
    def _():
        kb[...] = k_ref[0].astype(MXU)
        vb[...] = v_ref[0].astype(MXU)

    q = q_ref[0] * (HEAD_DIM ** -0.5)
    q2 = jnp.concatenate([jnp.where(lo_half, q, 0.0), jnp.where(lo_half, 0.0, q)], axis=0).astype(MXU)
    u = u_ref[...]

    def tile(carry, k, v, strict):
        c, acc = carry
        z = _dot_nt(q2, k)
        ls = _log_sigmoid(z)
        log_1m = ls - z
        if strict is not None:
            log_1m = jnp.where(strict, log_1m, 0.0)
        hi = log_1m.astype(MXU)
        lo = (log_1m - hi.astype(f32)).astype(MXU)
        tail = _dot(hi, u) + _dot(lo, u) + c
        a = jnp.exp(ls + tail)
        if strict is not None:
            a = jnp.where(strict, a, 0.0)
        return c + jnp.sum(log_1m, axis=1, keepdims=True), acc + _dot(a.astype(MXU), v)

    row = lax.broadcasted_iota(jnp.int32, (2 * tq, tk), 0)
    rel = jnp.where(row >= tq, row - tq, row) - lax.broadcasted_iota(jnp.int32, (2 * tq, tk), 1)
    zero = (jnp.zeros((2 * tq, 1), f32), jnp.zeros((2 * tq, LANE), f32))
    state = tile(zero, kb[pl.ds(q0, tk), :], vb[pl.ds(q0, tk), :], rel > 0)

    def alive(state):
        return (jnp.max(state[0]) > _SB_DEAD).astype(jnp.int32)

    def cond(loop):
        kt, live, _ = loop
        return (kt >= 0) & (live > 0)

    def body(loop):
        kt, _, state = loop
        k0 = pl.multiple_of(kt * tk, tk)
        state = tile(state, kb[pl.ds(k0, tk), :], vb[pl.ds(k0, tk), :], None)
        return kt - 1, alive(state), state

    _, _, (_, acc) = lax.while_loop(cond, body, (qi - 1, alive(state), state))
    o_ref[0] = jnp.where(lo_half, acc[:tq], acc[tq:])


def stick_breaking_attention(h, tq=256):
    B, S, _ = h.shape
    assert S % tq == 0
    tri = jnp.asarray(np.tril(np.ones((tq, tq), np.float32), -1), MXU)
    cq, ck, cv = (OFF[n] // LANE for n in ("bq", "bk", "bv"))
    return pl.pallas_call(
        functools.partial(_sb_kernel, tq=tq),
        grid=(B, SB_HEADS // 2, S // tq),
        in_specs=[pl.BlockSpec((1, tq, LANE), lambda b, p, i: (b, i, cq + p)),
                  pl.BlockSpec((1, S, LANE), lambda b, p, i: (b, 0, ck + p)),
                  pl.BlockSpec((1, S, LANE), lambda b, p, i: (b, 0, cv + p)),
                  pl.BlockSpec((tq, tq), lambda b, p, i: (0, 0))],
        out_specs=pl.BlockSpec((1, tq, LANE), lambda b, p, i: (b, i, p)),
        out_shape=jax.ShapeDtypeStruct((B, S, SB_HEADS * HEAD_DIM), f32),
        scratch_shapes=[pltpu.VMEM((S, LANE), MXU)] * 2,
        compiler_params=_params(3),
        name="sb_attention",
    )(h, h, h, tri)


def _neg_expm1(y):
    series = -y * (1.0 + y * (1.0 / 2 + y * (1.0 / 6 + y * (1.0 / 24 + y * (1.0 / 120)))))
    return jnp.where(y > -0.1, series, 1.0 - jnp.exp(y))


def _rglru_kernel(x_ref, xg_ref, cw_ref, cb_ref, gaw_ref, gab_ref, gxw_ref, gxb_ref, lam_ref, o_ref):
    x = x_ref[0]
    S = x.shape[0]
    row = lax.broadcasted_iota(jnp.int32, (S, 1), 0)

    def shifted(t, d, fill):
        return jnp.where(row >= d, pltpu.roll(t, d, 0), fill)

    u = cb_ref[...] + x * cw_ref[CONV_W - 1:CONV_W, :]
    for d in range(1, CONV_W):
        u = u + shifted(x, d, 0.0) * cw_ref[CONV_W - 1 - d:CONV_W - d, :]
    ub = u.astype(MXU)
    r = jax.nn.sigmoid(_dot(ub, gaw_ref[...]) + gab_ref[...])
    i = jax.nn.sigmoid(_dot(ub, gxw_ref[...]) + gxb_ref[...])
    lam = lam_ref[...]
    softplus_neg = jnp.maximum(-lam, 0.0) + jnp.log1p(jnp.exp(-jnp.abs(lam)))
    log_a = -LRU_C * r * softplus_neg
    a = jnp.exp(log_a)
    b = jnp.sqrt(_neg_expm1(2.0 * log_a)) * (i * u)
    d = 1
    while d < S:
        b = a * shifted(b, d, 0.0) + b
        a = a * shifted(a, d, 1.0)
        d *= 2
    o_ref[0] = b * jax.nn.gelu(xg_ref[0])


def _block_diag(w):
    n, c, _ = w.shape
    out = jnp.zeros((n * c, n * c), w.dtype)
    for j in range(n):
        out = out.at[j * c:(j + 1) * c, j * c:(j + 1) * c].set(w[j])
    return out


def rglru_block(h, conv_w, conv_b, ga_w, ga_b, gx_w, gx_b, lru_lambda):
    B, S, _ = h.shape
    W = RNN_W
    cx, cg = OFF["cx"] // W, OFF["cg"] // W
    vec = pl.BlockSpec((1, W), lambda b: (0, 0))
    mat = pl.BlockSpec((W, W), lambda b: (0, 0))
    return pl.pallas_call(
        _rglru_kernel,
        grid=(B,),
        in_specs=[pl.BlockSpec((1, S, W), lambda b: (b, 0, cx)), pl.BlockSpec((1, S, W), lambda b: (b, 0, cg)),
                  pl.BlockSpec((CONV_W, W), lambda b: (0, 0)), vec, mat, vec, mat, vec, vec],
        out_specs=pl.BlockSpec((1, S, W), lambda b: (b, 0, 0)),
        out_shape=jax.ShapeDtypeStruct((B, S, W), f32),
        compiler_params=_params(1),
        name="rglru",
    )(h, h, conv_w, conv_b.reshape(1, W), _block_diag(ga_w).astype(MXU), ga_b.reshape(1, W),
      _block_diag(gx_w).astype(MXU), gx_b.reshape(1, W), lru_lambda.reshape(1, W))


def _rms(x, g, width):
    return x * lax.rsqrt(jnp.sum(x * x, axis=-1, keepdims=True) * (1.0 / width) + RMS_EPS) * g


def _mla_prep_kernel(cq_ref, ckv_ref, kr_ref, krs_ref, gq_ref, gkv_ref, wq_ref, wqs_ref, wk_ref, wv_ref,
                     cosq_ref, sinq_ref, cosk_ref, sink_ref, vone_ref, q_ref, k_ref, v_ref):
    cq = _rms(cq_ref[0], gq_ref[...], MLA_Q_RANK).astype(MXU)
    ckv = _rms(ckv_ref[0], gkv_ref[...], MLA_KV_RANK).astype(MXU)
    scale = (MLA_NOPE + MLA_ROPE) ** -0.5
    q = _dot(cq, wq_ref[...]) * cosq_ref[...] + _dot(cq, wqs_ref[...]) * sinq_ref[...]
    q_ref[0] = (q * scale).astype(q_ref.dtype)
    k_rope = kr_ref[0] * cosk_ref[...] + krs_ref[0] * sink_ref[...]
    k = _dot(ckv, wk_ref[...])
    k_ref[0] = (k + jnp.concatenate([k_rope] * MLA_HEADS, axis=1)).astype(k_ref.dtype)
    v_ref[0] = (_dot(ckv, wv_ref[...]) + vone_ref[...]).astype(v_ref.dtype)


def _mla_attn_kernel(q_ref, k_ref, v_ref, cb_ref, o_ref, *, tq):
    tk = tq
    qi = pl.program_id(2)
    q0 = pl.multiple_of(qi * tq, tq)
    q = q_ref[0]
    lane = lax.broadcasted_iota(jnp.int32, (tq, LANE), 1)

    def update(carry, n, k, v, bias):
        m, acc = carry
        sl = slice(n * LANE, (n + 1) * LANE)
        s = _dot_nt(q[:, sl], k[:, sl])
        if bias is not None:
            s = s + bias
        m_new = jnp.maximum(m, jnp.max(s, axis=1, keepdims=True))
        p = jnp.exp(s - m_new)
        return m_new, jnp.exp(m - m_new) * acc + _dot(p.astype(MXU), v[:, sl])

    def body(kt, carry):
        k0 = pl.multiple_of(kt * tk, tk)
        k = k_ref[0, pl.ds(k0, tk), :]
        v = v_ref[0, pl.ds(k0, tk), :]
        return tuple(update(carry[n], n, k, v, None) for n in range(2))

    init = tuple((jnp.full((tq, 1), NEG, f32), jnp.zeros((tq, LANE), f32)) for _ in range(2))
    carry = lax.fori_loop(0, qi, body, init)
    k = k_ref[0, pl.ds(q0, tk), :]
    v = v_ref[0, pl.ds(q0, tk), :]
    o = []
    for n in range(2):
        _, acc = update(carry[n], n, k, v, cb_ref[...])
        denom = jnp.sum(jnp.where(lane == MLA_V, acc, 0.0), axis=1, keepdims=True)
        o.append(acc / jnp.maximum(denom, 1e-30))
    o_ref[0] = jnp.where(lane < MLA_V, o[0], pltpu.roll(o[1], MLA_V, 1))


def mla_attention(h, q_norm, kv_norm, w_uq, w_ukv, tr=512, tq=512):
    B, S, _ = h.shape
    H = MLA_HEADS
    dq = MLA_NOPE + MLA_ROPE
    HW = H * LANE
    wq3 = w_uq.reshape(MLA_Q_RANK, H, dq)
    wq_rot = jnp.concatenate([jnp.zeros_like(wq3[..., :MLA_NOPE]), _rot_half(wq3[..., MLA_NOPE:])], axis=-1)

    def pad_q(w3):
        w3 = jnp.pad(w3, ((0, 256 - MLA_Q_RANK), (0, 0), (0, LANE - dq)))
        return w3.reshape(256, HW).astype(MXU)

    wkv3 = w_ukv.reshape(MLA_KV_RANK, H, MLA_NOPE + MLA_V)
    wk = jnp.pad(wkv3[..., :MLA_NOPE], ((0, 0), (0, 0), (0, LANE - MLA_NOPE))).reshape(MLA_KV_RANK, HW).astype(MXU)
    wv = jnp.pad(wkv3[..., MLA_NOPE:], ((0, 0), (0, 0), (0, LANE - MLA_V))).reshape(MLA_KV_RANK, HW).astype(MXU)
    v_one = jnp.tile(jnp.concatenate([jnp.zeros((1, MLA_V), f32), jnp.ones((1, LANE - MLA_V), f32)], axis=1), (1, H))
    gq = jnp.pad(q_norm, (0, 256 - MLA_Q_RANK)).reshape(1, 256)
    gkv = kv_norm.reshape(1, MLA_KV_RANK)
    inv = ROPE_THETA ** (-jnp.arange(0, MLA_ROPE, 2, dtype=f32) / MLA_ROPE)
    ang = jnp.arange(S, dtype=f32)[:, None] * inv[None, :]
    cos2 = jnp.concatenate([jnp.cos(ang), jnp.cos(ang)], axis=1)
    sin2 = jnp.concatenate([jnp.sin(ang), jnp.sin(ang)], axis=1)
    tail = LANE - dq
    cos_k = jnp.concatenate([jnp.zeros((S, MLA_NOPE), f32), cos2, jnp.zeros((S, tail), f32)], axis=1)
    sin_k = jnp.concatenate([jnp.zeros((S, MLA_NOPE), f32), sin2, jnp.zeros((S, tail), f32)], axis=1)
    cos_q = jnp.tile(jnp.concatenate([jnp.ones((S, MLA_NOPE), f32), cos2, jnp.zeros((S, tail), f32)], axis=1), (1, H))
    sin_q = jnp.tile(sin_k, (1, H))
    rel = np.arange(tq)[:, None] - np.arange(tq)[None, :]
    causal = jnp.asarray(np.where(rel >= 0, 0.0, BIG_NEG).astype(np.float32))

    c_cq, c_ckv, c_kr, c_krs = OFF["dcq"] // 256, OFF["dckv"] // LANE, OFF["dkr"] // LANE, OFF["dkrs"] // LANE
    full = lambda shape: pl.BlockSpec(shape, lambda b, i: (0, 0))
    tab = lambda w: pl.BlockSpec((tr, w), lambda b, i: (i, 0))
    out3 = pl.BlockSpec((1, tr, HW), lambda b, i: (b, i, 0))
    q, k, v = pl.pallas_call(
        _mla_prep_kernel,
        grid=(B, S // tr),
        in_specs=[pl.BlockSpec((1, tr, 256), lambda b, i: (b, i, c_cq)),
                  pl.BlockSpec((1, tr, LANE), lambda b, i: (b, i, c_ckv)),
                  pl.BlockSpec((1, tr, LANE), lambda b, i: (b, i, c_kr)),
                  pl.BlockSpec((1, tr, LANE), lambda b, i: (b, i, c_krs)),
                  full((1, 256)), full((1, MLA_KV_RANK)), full((256, HW)), full((256, HW)),
                  full((MLA_KV_RANK, HW)), full((MLA_KV_RANK, HW)),
                  tab(HW), tab(HW), tab(LANE), tab(LANE), full((1, HW))],
        out_specs=[out3, out3, out3],
        out_shape=[jax.ShapeDtypeStruct((B, S, HW), MXU)] * 3,
        compiler_params=_params(2),
        name="mla_prep",
    )(h, h, h, h, gq, gkv, pad_q(wq3), pad_q(wq_rot), wk, wv, cos_q, sin_q, cos_k, sin_k, v_one)
    pair = lambda rows: pl.BlockSpec((1, rows, 2 * LANE), lambda b, p, i: (b, i if rows == tq else 0, p))
    return pl.pallas_call(
        functools.partial(_mla_attn_kernel, tq=tq),
        grid=(B, H // 2, S // tq),
        in_specs=[pair(tq), pair(S), pair(S), pl.BlockSpec((tq, tq), lambda b, p, i: (0, 0))],
        out_specs=pl.BlockSpec((1, tq, LANE), lambda b, p, i: (b, i, p)),
        out_shape=jax.ShapeDtypeStruct((B, S, H * MLA_V), f32),
        compiler_params=_params(3),
        name="mla_attention",
    )(q, k, v, causal)


def _ln(z, g, b):
    mu = jnp.mean(z, axis=-1, keepdims=True)
    zc = z - mu
    var = jnp.mean(zc * zc, axis=-1, keepdims=True)
    return zc * lax.rsqrt(var + LN_EPS) * g + b


def _merge_kernel(mg_ref, oa_ref, ob_ref, oc_ref, od_ref, wb_ref, wo_ref, x_ref, g_ref, b_ref, o_ref):
    acc = None
    for n, br in enumerate((oa_ref, ob_ref, oc_ref, od_ref)):
        up = _dot(br[...].astype(MXU), wb_ref[n])
        term = jax.nn.sigmoid(mg_ref[:, n * D_MODEL:(n + 1) * D_MODEL]) * up
        acc = term if acc is None else acc + term
    y = _dot(acc.astype(MXU), wo_ref[...])
    o_ref[...] = _ln(DN_ALPHA * x_ref[...] + y, g_ref[...], b_ref[...])


def merge_ln(h2, branches, w_branch, w_out, x2, g, b, tm=256):
    N, D = x2.shape
    assert OFF["mg"] == 0 and N % tm == 0
    row = lambda w: pl.BlockSpec((tm, w), lambda i: (i, 0))
    return pl.pallas_call(
        _merge_kernel,
        grid=(N // tm,),
        in_specs=[row(N_MIXERS * D)] + [row(MIX_W)] * N_MIXERS
        + [pl.BlockSpec((N_MIXERS, MIX_W, D), lambda i: (0, 0, 0)), pl.BlockSpec((D, D), lambda i: (0, 0)),
           row(D), pl.BlockSpec((1, D), lambda i: (0, 0)), pl.BlockSpec((1, D), lambda i: (0, 0))],
        out_specs=row(D),
        out_shape=jax.ShapeDtypeStruct((N, D), f32),
        compiler_params=_params(1),
        name="merge_ln",
    )(h2, *branches, w_branch.astype(MXU), w_out.astype(MXU), x2, g.reshape(1, D), b.reshape(1, D))


ROW_TILE = 8


def _to_token_tiles(ref, val):
    rows = val.shape[0]
    for j in range(ROW_TILE):
        ref[pl.ds(j, rows, stride=ROW_TILE), :] = val[:, j * LANE:(j + 1) * LANE]


def _from_token_tiles(ref, rows, first=0, stride=ROW_TILE):
    return jnp.concatenate([ref[pl.ds(first + j, rows, stride=stride), :] for j in range(ROW_TILE)], axis=1)


def _xattn_kernel(x_ref, wq_ref, k_ref, v_ref, wo_ref, g_ref, b_ref, o_ref, o8_ref):
    x = x_ref[0]
    q = _dot(x.astype(MXU), wq_ref[...]).astype(MXU)
    k = k_ref[0]
    v = v_ref[0]
    heads = []
    for hd in range(X_HEADS):
        sl = slice(hd * X_HEAD_DIM, (hd + 1) * X_HEAD_DIM)
        s = _dot_nt(q[:, sl], k[:, sl]) * (X_HEAD_DIM ** -0.5)
        e = jnp.exp(s - jnp.max(s, axis=1, keepdims=True))
        p = e / jnp.sum(e, axis=1, keepdims=True)
        heads.append(_dot(p.astype(MXU), v[:, sl]).astype(MXU))
    y = _dot(jnp.concatenate(heads, axis=1), wo_ref[...])
    out = _ln(DN_ALPHA * x + y, g_ref[...], b_ref[...])
    o_ref[0] = out
    _to_token_tiles(o8_ref.at[0], out)


def cross_attention_ln(x, mem, wq, wkv, wo, g, b, tq=256):
    B, S, D = x.shape
    assert D == ROW_TILE * LANE
    M = mem.shape[1]
    F = X_HEADS * X_HEAD_DIM
    kv = matmul(mem.reshape(B * M, D), wkv.astype(MXU), 512, 2 * F, out_dtype=MXU).reshape(B, M, 2 * F)
    full = lambda shape: pl.BlockSpec(shape, lambda bi, i: (0,) * len(shape))
    return pl.pallas_call(
        _xattn_kernel,
        grid=(B, S // tq),
        in_specs=[pl.BlockSpec((1, tq, D), lambda bi, i: (bi, i, 0)), full((D, F)),
                  pl.BlockSpec((1, M, F), lambda bi, i: (bi, 0, 0)), pl.BlockSpec((1, M, F), lambda bi, i: (bi, 0, 1)),
                  full((F, D)), full((1, D)), full((1, D))],
        out_specs=[pl.BlockSpec((1, tq, D), lambda bi, i: (bi, i, 0)),
                   pl.BlockSpec((1, tq * ROW_TILE, LANE), lambda bi, i: (bi, i, 0))],
        out_shape=[jax.ShapeDtypeStruct((B, S, D), f32), jax.ShapeDtypeStruct((B, S * ROW_TILE, LANE), f32)],
        compiler_params=_params(2),
        name="cross_attention_ln",
    )(x, wq.astype(MXU), kv, kv, wo.astype(MXU), g.reshape(1, D), b.reshape(1, D))


_R_E0, _R_E1, _R_W0, _R_W1, _R_RANK0, _R_RANK1 = range(6)
_GRP_LANE0 = N_EXPERTS


def _router_kernel(x_ref, w_ref, b_ref, tri_ref, r_ref, cnt_ref):
    i = pl.program_id(0)
    tm = x_ref.shape[0]
    logits = _dot(x_ref[...].astype(MXU), w_ref[...]) + b_ref[...]
    lane = lax.broadcasted_iota(jnp.int32, (tm, LANE), 1)
    lane_f = lane.astype(f32)
    big = float(LANE)

    def rmax(t):
        return jnp.max(t, axis=1, keepdims=True)

    def first_lane(cond):
        return jnp.min(jnp.where(cond, lane_f, big), axis=1, keepdims=True)

    def softmax_on(mask):
        lm = jnp.where(mask, logits, NEG)
        e = jnp.where(mask, jnp.exp(lm - rmax(lm)), 0.0)
        return e / jnp.sum(e, axis=1, keepdims=True)

    is_g = (lane >= _GRP_LANE0) & (lane < _GRP_LANE0 + N_GROUPS)
    p_grp = softmax_on(is_g)
    p_g = rmax(p_grp)
    grp = first_lane(is_g & (p_grp == p_g)) - float(_GRP_LANE0)
    in_grp = (lane < N_EXPERTS) & (jnp.right_shift(lane, 3).astype(f32) == grp)
    p_e = softmax_on(in_grp)
    p1 = rmax(jnp.where(in_grp, p_e, -1.0))
    e1 = first_lane(in_grp & (p_e == p1))
    rest = in_grp & (lane_f != e1)
    p2 = rmax(jnp.where(rest, p_e, -1.0))
    e2 = first_lane(rest & (p_e == p2))
    w1 = p_g * p1 / (p1 + p2)
    w2 = p_g * p2 / (p1 + p2)

    @pl.when(i == 0)
    def _():
        cnt_ref[...] = jnp.zeros_like(cnt_ref)

    oh1 = lane_f == e1
    oh2 = lane_f == e2
    both = (oh1 | oh2).astype(MXU)
    before = _dot(tri_ref[...], both) + cnt_ref[0:1, :]
    rank1 = jnp.sum(jnp.where(oh1, before, 0.0), axis=1, keepdims=True)
    rank2 = jnp.sum(jnp.where(oh2, before, 0.0), axis=1, keepdims=True)
    cnt_ref[...] = cnt_ref[...] + jnp.sum(both.astype(f32), axis=0, keepdims=True)

    out = jnp.zeros((tm, LANE), f32)
    for slot, val in ((_R_E0, e1), (_R_E1, e2), (_R_W0, w1), (_R_W1, w2), (_R_RANK0, rank1), (_R_RANK1, rank2)):
        out = jnp.where(lane == slot, val, out)
    r_ref[...] = out


def moe_router(x2, rg_w, rg_b, re_w, re_b, tm=256):
    N, D = x2.shape
    assert EXPERTS_PER_GROUP == 8 and N_EXPERTS + N_GROUPS <= LANE
    w = jnp.pad(jnp.concatenate([re_w, rg_w], axis=1), ((0, 0), (0, LANE - N_EXPERTS - N_GROUPS))).astype(MXU)
    b = jnp.pad(jnp.concatenate([re_b, rg_b]), (0, LANE - N_EXPERTS - N_GROUPS)).reshape(1, LANE)
    tri = jnp.asarray(np.tril(np.ones((tm, tm), np.float32), -1), MXU)
    return pl.pallas_call(
        _router_kernel,
        grid=(N // tm,),
        in_specs=[pl.BlockSpec((tm, D), lambda i: (i, 0)), pl.BlockSpec((D, LANE), lambda i: (0, 0)),
                  pl.BlockSpec((1, LANE), lambda i: (0, 0)), pl.BlockSpec((tm, tm), lambda i: (0, 0))],
        out_specs=[pl.BlockSpec((tm, LANE), lambda i: (i, 0)), pl.BlockSpec((8, LANE), lambda i: (0, 0))],
        out_shape=[jax.ShapeDtypeStruct((N, LANE), f32), jax.ShapeDtypeStruct((8, LANE), f32)],
        compiler_params=_params(1),
        name="moe_router",
    )(x2, w, b, tri)


def _ffn_kernel(ce_ref, nu_ref, x_ref, wgu_ref, wd_ref, o_ref):
    c = pl.program_id(0)

    @pl.when(c < nu_ref[0])
    def _():
        x = _from_token_tiles(x_ref, EXPERT_CHUNK)
        gu = _dot(x.astype(MXU), wgu_ref[0, 0].astype(MXU))
        hid = jax.nn.silu(gu[:, :D_EXPERT]) * gu[:, D_EXPERT:]
        _to_token_tiles(o_ref, _dot(hid.astype(MXU), wd_ref[0, 0].astype(MXU)))

    @pl.when(c >= nu_ref[0])
    def _():
        o_ref[...] = jnp.zeros_like(o_ref)


def expert_ffn(xb8, chunk_e, n_used, w_gu, w_down, layer):
    C = EXPERT_CHUNK
    n_chunks = xb8.shape[0] // (C * ROW_TILE)
    D = ROW_TILE * LANE
    tile_spec = pl.BlockSpec((C * ROW_TILE, LANE), lambda c, ce, nu: (c, 0))
    grid_spec = pltpu.PrefetchScalarGridSpec(
        num_scalar_prefetch=2,
        grid=(n_chunks,),
        in_specs=[tile_spec,
                  pl.BlockSpec((1, 1, D, 2 * D_EXPERT), lambda c, ce, nu: (layer, ce[c], 0, 0)),
                  pl.BlockSpec((1, 1, D_EXPERT, D), lambda c, ce, nu: (layer, ce[c], 0, 0))],
        out_specs=tile_spec,
    )
    return pl.pallas_call(
        _ffn_kernel,
        grid_spec=grid_spec,
        out_shape=jax.ShapeDtypeStruct(xb8.shape, f32),
        compiler_params=_params(1),
        name="expert_ffn",
    )(chunk_e, n_used, xb8, w_gu, w_down)


_COPY_WINDOW = 256


def _windowed_copies(n, copy, wait):
    assert _COPY_WINDOW & (_COPY_WINDOW - 1) == 0 and n >= _COPY_WINDOW

    def fill(j, carry):
        copy(j, j).start()
        return carry

    def steady(j, carry):
        s = jnp.bitwise_and(j, _COPY_WINDOW - 1)
        wait(s)
        copy(j, s).start()
        return carry

    def drain(s, carry):
        wait(s)
        return carry

    lax.fori_loop(0, _COPY_WINDOW, fill, 0, unroll=8)
    lax.fori_loop(_COPY_WINDOW, n, steady, 0, unroll=8)
    lax.fori_loop(0, _COPY_WINDOW, drain, 0, unroll=8)


def _dispatch_kernel(dest_ref, x8_ref, init_hbm, xb_hbm, sems, *, tm):
    n = tm * TOPK_IN_GROUP
    base = pl.program_id(0) * n

    def copy(j, s):
        t = pl.multiple_of(jnp.right_shift(j, 1) * ROW_TILE, ROW_TILE)
        d = pl.multiple_of(dest_ref[base + j] * ROW_TILE, ROW_TILE)
        return pltpu.make_async_copy(x8_ref.at[pl.ds(t, ROW_TILE)], xb_hbm.at[pl.ds(d, ROW_TILE)], sems.at[s])

    def wait(s):
        pltpu.make_async_copy(x8_ref.at[pl.ds(0, ROW_TILE)], xb_hbm.at[pl.ds(0, ROW_TILE)], sems.at[s]).wait()

    _windowed_copies(n, copy, wait)


def moe_dispatch(x8, dest, n_slots, tm=256):
    n_tok = x8.shape[0] // ROW_TILE
    assert n_tok % tm == 0 and tm * TOPK_IN_GROUP >= _COPY_WINDOW
    grid_spec = pltpu.PrefetchScalarGridSpec(
        num_scalar_prefetch=1,
        grid=(n_tok // tm,),
        in_specs=[pl.BlockSpec((tm * ROW_TILE, LANE), lambda i, dest: (i, 0)), pl.BlockSpec(memory_space=pl.ANY)],
        out_specs=pl.BlockSpec(memory_space=pl.ANY),
        scratch_shapes=[pltpu.SemaphoreType.DMA((_COPY_WINDOW,))],
    )
    return pl.pallas_call(
        functools.partial(_dispatch_kernel, tm=tm),
        grid_spec=grid_spec,
        out_shape=jax.ShapeDtypeStruct((n_slots * ROW_TILE, LANE), x8.dtype),
        input_output_aliases={2: 0},
        compiler_params=_params(1),
        name="moe_dispatch",
    )(dest, x8, jnp.zeros((n_slots * ROW_TILE, LANE), x8.dtype))


def _combine_kernel(dest_ref, x_ref, r_ref, g_ref, b_ref, yb_hbm, o_ref, buf, sems):
    tm = x_ref.shape[0]
    n = tm * TOPK_IN_GROUP
    base = pl.program_id(0) * n

    def copy(j, s):
        src = pl.multiple_of(dest_ref[base + j] * ROW_TILE, ROW_TILE)
        dst = pl.multiple_of(j * ROW_TILE, ROW_TILE)
        return pltpu.make_async_copy(yb_hbm.at[pl.ds(src, ROW_TILE)], buf.at[pl.ds(dst, ROW_TILE)], sems.at[s])

    def wait(s):
        pltpu.make_async_copy(yb_hbm.at[pl.ds(0, ROW_TILE)], buf.at[pl.ds(0, ROW_TILE)], sems.at[s]).wait()

    _windowed_copies(n, copy, wait)
    r = r_ref[...]
    lane = lax.broadcasted_iota(jnp.int32, r.shape, 1)
    w0 = jnp.sum(jnp.where(lane == _R_W0, r, 0.0), axis=1, keepdims=True)
    w1 = jnp.sum(jnp.where(lane == _R_W1, r, 0.0), axis=1, keepdims=True)
    pair = TOPK_IN_GROUP * ROW_TILE
    y = _from_token_tiles(buf, tm, 0, pair) * w0 + _from_token_tiles(buf, tm, ROW_TILE, pair) * w1
    o_ref[...] = _ln(DN_ALPHA * x_ref[...] + y, g_ref[...], b_ref[...])


def combine_ln(x2, yb8, dest, r, g, b, tm=256):
    N, D = x2.shape
    assert N % tm == 0 and tm * TOPK_IN_GROUP >= _COPY_WINDOW
    row = lambda w: pl.BlockSpec((tm, w), lambda i, dest: (i, 0))
    vec = pl.BlockSpec((1, D), lambda i, dest: (0, 0))
    grid_spec = pltpu.PrefetchScalarGridSpec(
        num_scalar_prefetch=1,
        grid=(N // tm,),
        in_specs=[row(D), row(LANE), vec, vec, pl.BlockSpec(memory_space=pl.ANY)],
        out_specs=row(D),
        scratch_shapes=[pltpu.VMEM((tm * TOPK_IN_GROUP * ROW_TILE, LANE), f32),
                        pltpu.SemaphoreType.DMA((_COPY_WINDOW,))],
    )
    return pl.pallas_call(
        _combine_kernel,
        grid_spec=grid_spec,
        out_shape=jax.ShapeDtypeStruct((N, D), f32),
        compiler_params=_params(1),
        name="moe_combine_ln",
    )(dest, x2, r, g.reshape(1, D), b.reshape(1, D), yb8)


def hier_moe_ln(x2, x8, rg_w, rg_b, re_w, re_b, w_gu, w_down, layer, g, b):
    N, D = x2.shape
    E, C, K = N_EXPERTS, EXPERT_CHUNK, TOPK_IN_GROUP
    A = N * K
    r, cnt = moe_router(x2, rg_w, rg_b, re_w, re_b)
    e = r[:, _R_E0:_R_E1 + 1].astype(jnp.int32)
    rank = r[:, _R_RANK0:_R_RANK1 + 1].astype(jnp.int32)
    counts = cnt[0, :E].astype(jnp.int32)
    padded = (counts + C - 1) // C * C
    pad_end = jnp.cumsum(padded)
    pad_start = pad_end - padded
    dest = pad_start[e] + rank
    n_chunks = -(-(A + E * (C - 1)) // C)
    P = n_chunks * C
    chunk_start = jnp.arange(n_chunks, dtype=jnp.int32) * C
    chunk_e = jnp.minimum(jnp.sum((pad_end[None, :] <= chunk_start[:, None]).astype(jnp.int32), axis=1), E - 1)
    n_used = (pad_end[-1] // C).reshape(1).astype(jnp.int32)
    dest = dest.reshape(A).astype(jnp.int32)
    xb8 = moe_dispatch(x8, dest, P)
    yb8 = expert_ffn(xb8, chunk_e.astype(jnp.int32), n_used, w_gu, w_down, layer)
    return combine_ln(x2, yb8, dest, r, g, b)


def kernel(x, mem, w_in, nsa_cmp_pos, nsa_cmp_w1, nsa_cmp_w2, rnn_conv_w, rnn_conv_b, rnn_ga_w, rnn_ga_b,
           rnn_gx_w, rnn_gx_b, rnn_lambda, mla_q_norm, mla_kv_norm, mla_w_uq, mla_w_ukv, w_branch, w_out,
           ln1_g, ln1_b, x_wq, x_wkv, x_wo, ln2_g, ln2_b, moe_rg_w, moe_rg_b, moe_re_w, moe_re_b,
           moe_w_gu, moe_w_down, ln3_g, ln3_b):
    B, S, D = x.shape
    N = B * S
    x2 = x.reshape(N, D)
    for l in range(DEPTH):
        h2 = matmul(x2, _cat_w_in(w_in[l]), 512, _IN_TILE)
        h = h2.reshape(B, S, N_CAT)
        o_a = nsa_attention(h, nsa_cmp_pos[l], nsa_cmp_w1[l], nsa_cmp_w2[l])
        o_b = stick_breaking_attention(h)
        o_c = rglru_block(h, rnn_conv_w[l], rnn_conv_b[l], rnn_ga_w[l], rnn_ga_b[l], rnn_gx_w[l], rnn_gx_b[l],
                          rnn_lambda[l])
        o_d = mla_attention(h, mla_q_norm[l], mla_kv_norm[l], mla_w_uq[l], mla_w_ukv[l])
        branches = [o.reshape(N, MIX_W) for o in (o_a, o_b, o_c, o_d)]
        x2 = merge_ln(h2, branches, w_branch[l], w_out[l], x2, ln1_g[l], ln1_b[l])
        x3, x8 = cross_attention_ln(x2.reshape(B, S, D), mem, x_wq[l], x_wkv[l], x_wo[l], ln2_g[l], ln2_b[l])
        x2 = hier_moe_ln(x3.reshape(N, D), x8.reshape(N * ROW_TILE, LANE), moe_rg_w[l], moe_rg_b[l], moe_re_w[l],
                         moe_re_b[l], moe_w_gu, moe_w_down, l, ln3_g[l], ln3_b[l])
    return x2.reshape(B, S, D)
```

```python
import functools

import numpy as np
import jax
import jax.numpy as jnp
from jax import lax
from jax.experimental import pallas as pl
from jax.experimental.pallas import tpu as pltpu

D_MODEL = 1024
DEPTH = 4
HEAD_DIM = 64
N_MIXERS = 4
MIX_W = 256
NSA_HEADS = 4
NSA_KV_HEADS = 2
NSA_GROUP = NSA_HEADS // NSA_KV_HEADS
CMP_LEN = 32
CMP_STRIDE = 16
CMP_HID = 256
SEL_BLOCK = 64
SEL_TOPK = 8
WINDOW = 512
FORCE_SCORE = 1e4
SB_HEADS = 4
RNN_W = 256
CONV_W = 4
LRU_C = 8.0
MLA_HEADS = 4
MLA_Q_RANK = 192
MLA_KV_RANK = 128
MLA_NOPE = 64
MLA_ROPE = 32
MLA_V = 64
ROPE_THETA = 10000.0
X_HEADS = 4
X_HEAD_DIM = 128
N_GROUPS = 4
EXPERTS_PER_GROUP = 8
N_EXPERTS = N_GROUPS * EXPERTS_PER_GROUP
TOPK_IN_GROUP = 2
D_EXPERT = 512
EXPERT_CHUNK = 256
DN_ALPHA = (2.0 * DEPTH) ** 0.25
LN_EPS = 1e-5
RMS_EPS = 1e-6

IN_SPLITS = ((NSA_HEADS * HEAD_DIM,) + (NSA_KV_HEADS * HEAD_DIM,) * 6 + (NSA_HEADS * 3,)
             + (SB_HEADS * HEAD_DIM,) * 3
             + (RNN_W, RNN_W)
             + (MLA_Q_RANK, MLA_KV_RANK, MLA_ROPE)
             + (N_MIXERS * D_MODEL,))
IN_OFFSETS = tuple(int(o) for o in np.concatenate([[0], np.cumsum(IN_SPLITS)[:-1]]))

LANE = 128
VMEM_LIMIT = 48 * 1024 * 1024
NEG = -1e30
BIG_NEG = -2.0 ** 100

f32 = jnp.float32
MXU = jnp.bfloat16

_SECTIONS = (("mg", 4096), ("aq", 256), ("ks", 256), ("vs", 256), ("kw", 256), ("vw", 256), ("bq", 256),
             ("bk", 256), ("bv", 256), ("cx", 256), ("cg", 256), ("dcq", 256), ("kc", 128), ("vc", 128),
             ("ag", 128), ("dckv", 128), ("dkr", 128), ("dkrs", 128))
_IN_TILE = 3840
OFF = {}
_o = 0
for _n, _w in _SECTIONS:
    OFF[_n] = _o
    _o += _w
N_CAT = -(-_o // _IN_TILE) * _IN_TILE

_AUG_SEL = HEAD_DIM
_AUG_POS = HEAD_DIM + 32
_POS_SPLIT = 256


def _rot_half(t):
    d = t.shape[-1]
    return jnp.concatenate([-t[..., d // 2:], t[..., :d // 2]], axis=-1)


def _cat_w_in(w):
    def sec(i):
        return w[:, IN_OFFSETS[i]:IN_OFFSETS[i] + IN_SPLITS[i]]

    def spread(t):
        z = jnp.zeros_like(t[:, :HEAD_DIM])
        return jnp.concatenate([t[:, :HEAD_DIM], z, t[:, HEAD_DIM:], z], axis=1)

    def padc(t, n):
        return jnp.pad(t, ((0, 0), (0, n - t.shape[1])))

    def rope_slot(t):
        return jnp.pad(t, ((0, 0), (MLA_NOPE, LANE - MLA_NOPE - MLA_ROPE)))

    parts = {"mg": sec(16), "aq": sec(0), "kc": sec(1), "vc": sec(2), "ks": spread(sec(3)), "vs": spread(sec(4)),
             "kw": spread(sec(5)), "vw": spread(sec(6)), "ag": padc(sec(7), 128), "bq": sec(8), "bk": sec(9),
             "bv": sec(10), "cx": sec(11), "cg": sec(12), "dcq": padc(sec(13), 256), "dckv": sec(14),
             "dkr": rope_slot(sec(15)), "dkrs": rope_slot(_rot_half(sec(15)))}
    cat = jnp.concatenate([parts[n] for n, _ in _SECTIONS], axis=1)
    return padc(cat, N_CAT).astype(MXU)


def _dot(a, b):
    return jnp.dot(a, b, preferred_element_type=f32)


def _dot_nt(a, b):
    return lax.dot_general(a, b, (((1,), (1,)), ((), ())), preferred_element_type=f32)


def _params(n_axes):
    return pltpu.CompilerParams(dimension_semantics=("arbitrary",) * n_axes, vmem_limit_bytes=VMEM_LIMIT)


def _mm_kernel(a_ref, b_ref, o_ref):
    o_ref[...] = _dot(a_ref[...].astype(MXU), b_ref[...]).astype(o_ref.dtype)


def matmul(a, b, tm, tn, out_dtype=f32):
    M, K = a.shape
    _, N = b.shape
    assert M % tm == 0 and N % tn == 0
    return pl.pallas_call(
        _mm_kernel,
        grid=(N // tn, M // tm),
        in_specs=[pl.BlockSpec((tm, K), lambda j, i: (i, 0)),
                  pl.BlockSpec((K, tn), lambda j, i: (0, j))],
        out_specs=pl.BlockSpec((tm, tn), lambda j, i: (i, j)),
        out_shape=jax.ShapeDtypeStruct((M, N), out_dtype),
        compiler_params=_params(2),
        name="matmul",
    )(a, b)


def _cmp_kernel(t_ref, w1_ref, pos_ref, w2_ref, o_ref):
    half = t_ref.shape[3]
    t = t_ref[0, 0].astype(MXU)
    y1 = _dot(t, w1_ref[0, :half, :])
    y2 = _dot(t, w1_ref[0, half:, :])
    pos = jnp.broadcast_to(pos_ref[0], (8, 2 * half)).astype(MXU)
    pc = _dot(pos, w1_ref[0])[0:1]
    nc = y2.shape[0]
    hid = y1 + pltpu.roll(y2, nc - 1, 0) + pc
    o_ref[0, 0] = _dot(jax.nn.gelu(hid).astype(MXU), w2_ref[0])


def nsa_compress(t, cmp_w1, cmp_pos, cmp_w2):
    _, B, NC, F = t.shape
    G = NSA_KV_HEADS
    eye = jnp.eye(G, dtype=f32)
    w1 = jnp.einsum('jldh,gk->jlgdkh', cmp_w1.reshape(2, CMP_LEN, HEAD_DIM, CMP_HID), eye)
    w1 = w1.reshape(2, 2 * F, G * CMP_HID).astype(MXU)
    pos = jnp.broadcast_to(cmp_pos[:, :, None, :], (2, CMP_LEN, G, HEAD_DIM)).reshape(2, 1, 2 * F)
    w2 = jnp.einsum('jhc,gk->jghkc', jnp.concatenate([cmp_w2, cmp_w2], axis=-1), eye)
    w2 = w2.reshape(2, G * CMP_HID, G * LANE).astype(MXU)
    return pl.pallas_call(
        _cmp_kernel,
        grid=(2, B),
        in_specs=[pl.BlockSpec((1, 1, NC, F), lambda j, i: (j, i, 0, 0)),
                  pl.BlockSpec((1, 2 * F, G * CMP_HID), lambda j, i: (j, 0, 0)),
                  pl.BlockSpec((1, 1, 2 * F), lambda j, i: (j, 0, 0)),
                  pl.BlockSpec((1, G * CMP_HID, G * LANE), lambda j, i: (j, 0, 0))],
        out_specs=pl.BlockSpec((1, 1, NC, G * LANE), lambda j, i: (j, i, 0, 0)),
        out_shape=jax.ShapeDtypeStruct((2, B, NC, G * LANE), f32),
        compiler_params=_params(2),
        name="nsa_compress",
    )(t, w1, pos, w2)


def _nsa_kernel(q_ref, kc_ref, vc_ref, ks_ref, vs_ref, kw_ref, vw_ref, gl_ref, cover_ref, aug_ref, cb_ref, wb_ref,
                o_ref, ksa, vsa, kwa, vwa, *, tq, n_cmp, n_sel, n_top):
    tk = tq
    g = pl.program_id(1)
    qi = pl.program_id(2)
    q0 = pl.multiple_of(qi * tq, tq)
    lane = lax.broadcasted_iota(jnp.int32, (tq, LANE), 1)
    lo_half = lane < HEAD_DIM

    @pl.when(qi == 0)
    def _():
        real = lax.broadcasted_iota(jnp.int32, ksa.shape, 1) < HEAD_DIM
        aug = aug_ref[...]
        ones = jnp.ones(ksa.shape, MXU)
        ksa[...] = jnp.where(real, ks_ref[0].astype(MXU), aug)
        kwa[...] = jnp.where(real, kw_ref[0].astype(MXU), aug)
        vsa[...] = jnp.where(real, vs_ref[0].astype(MXU), ones)
        vwa[...] = jnp.where(real, vw_ref[0].astype(MXU), ones)

    q = q_ref[0] * (HEAD_DIM ** -0.5)
    q_heads = (jnp.where(lo_half, q, 0.0), jnp.where(lo_half, pltpu.roll(q, HEAD_DIM, 1), 0.0))
    alibi = [2.0 ** (-8.0 * (h + 1) / NSA_HEADS) for h in range(NSA_HEADS)]
    slopes = [jnp.where(g == 0, alibi[n], alibi[NSA_GROUP + n]) for n in range(NSA_GROUP)]
    pos_cols = [jnp.where(lane == _AUG_POS, slopes[n] * _POS_SPLIT, jnp.where(lane == _AUG_POS + 1, slopes[n], 0.0))
                for n in range(NSA_GROUP)]
    tpos = q0 + lax.broadcasted_iota(jnp.int32, (tq, 1), 0)

    nc = kc_ref.shape[2]
    cidx = lax.broadcasted_iota(jnp.int32, (1, nc), 1)
    dist_c = tpos - (cidx * CMP_STRIDE + (CMP_LEN - 1))
    mask_c = (dist_c >= 0) & (cidx < n_cmp)
    dist_cf = dist_c.astype(f32)
    kc = kc_ref[0, 0].astype(MXU)
    vc = vc_ref[0, 0].astype(MXU)
    o_cmp = []
    imp_t = jnp.zeros((LANE, tq), f32)
    for n in range(NSA_GROUP):
        s = _dot_nt(q_heads[n].astype(MXU), kc) - slopes[n] * dist_cf
        sm = jnp.where(mask_c, s, NEG)
        m = jnp.max(sm, axis=1, keepdims=True)
        p = jnp.where(mask_c, jnp.exp(sm - m), 0.0)
        p = (p / jnp.maximum(jnp.sum(p, axis=1, keepdims=True), 1e-30)).astype(MXU)
        o_cmp.append(_dot(p, vc))
        imp_t = imp_t + _dot_nt(cover_ref[...], p)

    rows = 32
    imp = imp_t[:rows]
    blk = lax.broadcasted_iota(jnp.int32, (rows, tq), 0)
    blk_f = blk.astype(f32)
    tpos_t = q0 + lax.broadcasted_iota(jnp.int32, (1, tq), 1)
    forced = (blk == 0) | (blk == jnp.right_shift(tpos_t, 6))
    valid = blk * SEL_BLOCK <= tpos_t
    imp = jnp.where(forced, FORCE_SCORE, jnp.where(valid, imp, -1.0))
    imp = jnp.where(blk < n_sel, imp, NEG)
    sel_t = jnp.zeros((rows, tq), f32)
    for _ in range(n_top):
        m = jnp.max(imp, axis=0, keepdims=True)
        first = jnp.min(jnp.where(imp == m, blk_f, float(LANE)), axis=0, keepdims=True)
        pick = blk_f == first
        sel_t = jnp.where(pick, 1.0, sel_t)
        imp = jnp.where(pick, 2 * NEG, imp)
    sel = jnp.concatenate([sel_t, jnp.zeros((LANE - rows, tq), f32)], axis=0).T
    sel_bias = pltpu.roll(jnp.where(sel > 0.5, 0.0, BIG_NEG), _AUG_SEL, 1)
    sel_cols = jnp.where((lane >= _AUG_SEL) & (lane < _AUG_SEL + rows), sel_bias, 0.0)
    q_sel = jnp.concatenate([q_heads[n] + sel_cols + pos_cols[n] for n in range(NSA_GROUP)], axis=0).astype(MXU)
    q_win = jnp.concatenate([q_heads[n] + pos_cols[n] for n in range(NSA_GROUP)], axis=0).astype(MXU)
    rows_q = NSA_GROUP * tq
    denom_lane = lax.broadcasted_iota(jnp.int32, (rows_q, LANE), 1) == HEAD_DIM

    def update(carry, qs, k, v, bias):
        m, acc = carry
        s = _dot_nt(qs, k)
        if bias is not None:
            s = s + bias
        m_new = jnp.maximum(m, jnp.max(s, axis=1, keepdims=True))
        p = jnp.exp(s - m_new)
        return m_new, jnp.exp(m - m_new) * acc + _dot(p.astype(MXU), v)

    def finish(carry):
        _, acc = carry
        denom = jnp.sum(jnp.where(denom_lane, acc, 0.0), axis=1, keepdims=True)
        o = acc / jnp.maximum(denom, 1e-30)
        return [o[n * tq:(n + 1) * tq] for n in range(NSA_GROUP)]

    init = (jnp.full((rows_q, 1), NEG, f32), jnp.zeros((rows_q, LANE), f32))
    causal = cb_ref[...]

    def sel_body(kt, carry):
        k0 = pl.multiple_of(kt * tk, tk)
        return update(carry, q_sel, ksa[pl.ds(k0, tk), :], vsa[pl.ds(k0, tk), :], None)

    carry = lax.fori_loop(0, qi, sel_body, init)
    o_sel = finish(update(carry, q_sel, ksa[pl.ds(q0, tk), :], vsa[pl.ds(q0, tk), :], causal))

    carry = init
    n_back = WINDOW // tk
    for back in range(n_back, -1, -1):
        k0 = pl.multiple_of(jnp.maximum(qi - back, 0) * tk, tk)
        if back:
            off = jnp.where(qi >= back, 0.0, BIG_NEG)
            bias = wb_ref[...] + off if back == n_back else off
        else:
            bias = causal
        carry = update(carry, q_win, kwa[pl.ds(k0, tk), :], vwa[pl.ds(k0, tk), :], bias)
    o_win = finish(carry)

    sig = jax.nn.sigmoid(gl_ref[0])

    def gate(n, j):
        col = 3 * (NSA_GROUP * g + n) + j
        return jnp.sum(jnp.where(lane == col, sig, 0.0), axis=1, keepdims=True)

    o = [gate(n, 0) * o_cmp[n] + gate(n, 1) * o_sel[n] + gate(n, 2) * o_win[n] for n in range(NSA_GROUP)]
    o_ref[0] = jnp.where(lo_half, o[0], pltpu.roll(o[1], HEAD_DIM, 1))


def nsa_attention(h, cmp_pos, cmp_w1, cmp_w2, tq=512):
    B, S, _ = h.shape
    G = NSA_KV_HEADS
    NC = S // CMP_STRIDE
    n_cmp = (S - CMP_LEN) // CMP_STRIDE + 1
    n_sel = S // SEL_BLOCK
    n_top = min(SEL_TOPK, n_sel)
    F = CMP_STRIDE * HEAD_DIM
    assert S % tq == 0 and WINDOW % tq == 0 and tq % SEL_BLOCK == 0 and n_sel <= 32 and S <= _POS_SPLIT ** 2

    def chunks(off):
        return h[:, :, off:off + G * HEAD_DIM].reshape(B, NC, CMP_STRIDE * G * HEAD_DIM)

    kvc = nsa_compress(jnp.stack([chunks(OFF["kc"]), chunks(OFF["vc"])]), cmp_w1, cmp_pos, cmp_w2)

    c0 = np.arange(n_cmp)[:, None] * CMP_STRIDE
    j0 = np.arange(n_sel)[None, :] * SEL_BLOCK
    cover = np.clip(np.minimum(c0 + CMP_LEN, j0 + SEL_BLOCK) - np.maximum(c0, j0), 0, None) / CMP_LEN
    cover_t = np.zeros((LANE, NC), np.float32)
    cover_t[:n_sel, :n_cmp] = cover.T
    pos = np.arange(S)
    aug = np.zeros((S, LANE), np.float32)
    aug[pos, _AUG_SEL + pos // SEL_BLOCK] = 1.0
    aug[:, _AUG_POS] = pos // _POS_SPLIT
    aug[:, _AUG_POS + 1] = pos % _POS_SPLIT
    rel = np.arange(tq)[:, None] - np.arange(tq)[None, :]
    rel = np.tile(rel, (NSA_GROUP, 1))
    causal = np.where(rel >= 0, 0.0, BIG_NEG).astype(np.float32)
    win_lo = np.where(rel < 0, 0.0, BIG_NEG).astype(np.float32)

    col = lambda name: OFF[name] // LANE
    full2 = lambda shape: pl.BlockSpec(shape, lambda b, g, i: (0, 0))
    q_spec = pl.BlockSpec((1, tq, LANE), lambda b, g, i: (b, i, col("aq") + g))
    c_specs = [pl.BlockSpec((1, 1, NC, LANE), lambda b, g, i, j=j: (j, b, 0, g)) for j in range(2)]
    kv_specs = [pl.BlockSpec((1, S, LANE), lambda b, g, i, c=col(n): (b, 0, c + g))
                for n in ("ks", "vs", "kw", "vw")]
    gl_spec = pl.BlockSpec((1, tq, LANE), lambda b, g, i: (b, i, col("ag")))
    return pl.pallas_call(
        functools.partial(_nsa_kernel, tq=tq, n_cmp=n_cmp, n_sel=n_sel, n_top=n_top),
        grid=(B, G, S // tq),
        in_specs=[q_spec] + c_specs + kv_specs + [gl_spec, full2((LANE, NC)), full2((S, LANE)),
                                                  full2((NSA_GROUP * tq, tq)), full2((NSA_GROUP * tq, tq))],
        out_specs=pl.BlockSpec((1, tq, LANE), lambda b, g, i: (b, i, g)),
        out_shape=jax.ShapeDtypeStruct((B, S, NSA_HEADS * HEAD_DIM), f32),
        scratch_shapes=[pltpu.VMEM((S, LANE), MXU)] * 4,
        compiler_params=_params(3),
        name="nsa_attention",
    )(h, kvc, kvc, h, h, h, h, h, jnp.asarray(cover_t, MXU), jnp.asarray(aug, MXU), jnp.asarray(causal),
      jnp.asarray(win_lo))


_SB_DEAD = -104.0


def _log_sigmoid(z):
    return jnp.minimum(z, 0.0) - jnp.log1p(jnp.exp(-jnp.abs(z)))


def _sb_kernel(q_ref, k_ref, v_ref, u_ref, o_ref, kb, vb, *, tq):
    tk = tq
    qi = pl.program_id(2)
    q0 = pl.multiple_of(qi * tq, tq)
    lane = lax.broadcasted_iota(jnp.int32, (tq, LANE), 1)
    lo_half = lane < HEAD_DIM

    @pl.when(qi == 0)
    def _():
        kb[...] = k_ref[0].astype(MXU)
        vb[...] = v_ref[0].astype(MXU)

    q = q_ref[0] * (HEAD_DIM ** -0.5)
    q2 = jnp.concatenate([jnp.where(lo_half, q, 0.0), jnp.where(lo_half, 0.0, q)], axis=0).astype(MXU)
    u = u_ref[...]

    def tile(carry, k, v, strict):
        c, acc = carry
        z = _dot_nt(q2, k)
        ls = _log_sigmoid(z)
        log_1m = ls - z
        if strict is not None:
            log_1m = jnp.where(strict, log_1m, 0.0)
        hi = log_1m.astype(MXU)
        lo = (log_1m - hi.astype(f32)).astype(MXU)
        tail = _dot(hi, u) + _dot(lo, u) + c
        a = jnp.exp(ls + tail)
        if strict is not None:
            a = jnp.where(strict, a, 0.0)
        return c + jnp.sum(log_1m, axis=1, keepdims=True), acc + _dot(a.astype(MXU), v)

    row = lax.broadcasted_iota(jnp.int32, (2 * tq, tk), 0)
    rel = jnp.where(row >= tq, row - tq, row) - lax.broadcasted_iota(jnp.int32, (2 * tq, tk), 1)
    zero = (jnp.zeros((2 * tq, 1), f32), jnp.zeros((2 * tq, LANE), f32))
    state = tile(zero, kb[pl.ds(q0, tk), :], vb[pl.ds(q0, tk), :], rel > 0)

    def alive(state):
        return (jnp.max(state[0]) > _SB_DEAD).astype(jnp.int32)

    def cond(loop):
        kt, live, _ = loop
        return (kt >= 0) & (live > 0)

    def body(loop):
        kt, _, state = loop
        k0 = pl.multiple_of(kt * tk, tk)
        state = tile(state, kb[pl.ds(k0, tk), :], vb[pl.ds(k0, tk), :], None)
        return kt - 1, alive(state), state

    _, _, (_, acc) = lax.while_loop(cond, body, (qi - 1, alive(state), state))
    o_ref[0] = jnp.where(lo_half, acc[:tq], acc[tq:])


def stick_breaking_attention(h, tq=256):
    B, S, _ = h.shape
    assert S % tq == 0
    tri = jnp.asarray(np.tril(np.ones((tq, tq), np.float32), -1), MXU)
    cq, ck, cv = (OFF[n] // LANE for n in ("bq", "bk", "bv"))
    return pl.pallas_call(
        functools.partial(_sb_kernel, tq=tq),
        grid=(B, SB_HEADS // 2, S // tq),
        in_specs=[pl.BlockSpec((1, tq, LANE), lambda b, p, i: (b, i, cq + p)),
                  pl.BlockSpec((1, S, LANE), lambda b, p, i: (b, 0, ck + p)),
                  pl.BlockSpec((1, S, LANE), lambda b, p, i: (b, 0, cv + p)),
                  pl.BlockSpec((tq, tq), lambda b, p, i: (0, 0))],
        out_specs=pl.BlockSpec((1, tq, LANE), lambda b, p, i: (b, i, p)),
        out_shape=jax.ShapeDtypeStruct((B, S, SB_HEADS * HEAD_DIM), f32),
        scratch_shapes=[pltpu.VMEM((S, LANE), MXU)] * 2,
        compiler_params=_params(3),
        name="sb_attention",
    )(h, h, h, tri)


def _neg_expm1(y):
    series = -y * (1.0 + y * (1.0 / 2 + y * (1.0 / 6 + y * (1.0 / 24 + y * (1.0 / 120)))))
    return jnp.where(y > -0.1, series, 1.0 - jnp.exp(y))


def _rglru_kernel(x_ref, xg_ref, cw_ref, cb_ref, gaw_ref, gab_ref, gxw_ref, gxb_ref, lam_ref, o_ref):
    x = x_ref[0]
    S = x.shape[0]
    row = lax.broadcasted_iota(jnp.int32, (S, 1), 0)

    def shifted(t, d, fill):
        return jnp.where(row >= d, pltpu.roll(t, d, 0), fill)

    u = cb_ref[...] + x * cw_ref[CONV_W - 1:CONV_W, :]
    for d in range(1, CONV_W):
        u = u + shifted(x, d, 0.0) * cw_ref[CONV_W - 1 - d:CONV_W - d, :]
    ub = u.astype(MXU)
    r = jax.nn.sigmoid(_dot(ub, gaw_ref[...]) + gab_ref[...])
    i = jax.nn.sigmoid(_dot(ub, gxw_ref[...]) + gxb_ref[...])
    lam = lam_ref[...]
    softplus_neg = jnp.maximum(-lam, 0.0) + jnp.log1p(jnp.exp(-jnp.abs(lam)))
    log_a = -LRU_C * r * softplus_neg
    a = jnp.exp(log_a)
    b = jnp.sqrt(_neg_expm1(2.0 * log_a)) * (i * u)
    d = 1
    while d < S:
        b = a * shifted(b, d, 0.0) + b
        a = a * shifted(a, d, 1.0)
        d *= 2
    o_ref[0] = b * jax.nn.gelu(xg_ref[0])


def _block_diag(w):
    n, c, _ = w.shape
    out = jnp.zeros((n * c, n * c), w.dtype)
    for j in range(n):
        out = out.at[j * c:(j + 1) * c, j * c:(j + 1) * c].set(w[j])
    return out


def rglru_block(h, conv_w, conv_b, ga_w, ga_b, gx_w, gx_b, lru_lambda):
    B, S, _ = h.shape
    W = RNN_W
    cx, cg = OFF["cx"] // W, OFF["cg"] // W
    vec = pl.BlockSpec((1, W), lambda b: (0, 0))
    mat = pl.BlockSpec((W, W), lambda b: (0, 0))
    return pl.pallas_call(
        _rglru_kernel,
        grid=(B,),
        in_specs=[pl.BlockSpec((1, S, W), lambda b: (b, 0, cx)), pl.BlockSpec((1, S, W), lambda b: (b, 0, cg)),
                  pl.BlockSpec((CONV_W, W), lambda b: (0, 0)), vec, mat, vec, mat, vec, vec],
        out_specs=pl.BlockSpec((1, S, W), lambda b: (b, 0, 0)),
        out_shape=jax.ShapeDtypeStruct((B, S, W), f32),
        compiler_params=_params(1),
        name="rglru",
    )(h, h, conv_w, conv_b.reshape(1, W), _block_diag(ga_w).astype(MXU), ga_b.reshape(1, W),
      _block_diag(gx_w).astype(MXU), gx_b.reshape(1, W), lru_lambda.reshape(1, W))


def _rms(x, g, width):
    return x * lax.rsqrt(jnp.sum(x * x, axis=-1, keepdims=True) * (1.0 / width) + RMS_EPS) * g


def _mla_prep_kernel(cq_ref, ckv_ref, kr_ref, krs_ref, gq_ref, gkv_ref, wq_ref, wqs_ref, wk_ref, wv_ref,
                     cosq_ref, sinq_ref, cosk_ref, sink_ref, vone_ref, q_ref, k_ref, v_ref):
    cq = _rms(cq_ref[0], gq_ref[...], MLA_Q_RANK).astype(MXU)
    ckv = _rms(ckv_ref[0], gkv_ref[...], MLA_KV_RANK).astype(MXU)
    scale = (MLA_NOPE + MLA_ROPE) ** -0.5
    q = _dot(cq, wq_ref[...]) * cosq_ref[...] + _dot(cq, wqs_ref[...]) * sinq_ref[...]
    q_ref[0] = (q * scale).astype(q_ref.dtype)
    k_rope = kr_ref[0] * cosk_ref[...] + krs_ref[0] * sink_ref[...]
    k = _dot(ckv, wk_ref[...])
    k_ref[0] = (k + jnp.concatenate([k_rope] * MLA_HEADS, axis=1)).astype(k_ref.dtype)
    v_ref[0] = (_dot(ckv, wv_ref[...]) + vone_ref[...]).astype(v_ref.dtype)


def _mla_attn_kernel(q_ref, k_ref, v_ref, cb_ref, o_ref, *, tq):
    tk = tq
    qi = pl.program_id(2)
    q0 = pl.multiple_of(qi * tq, tq)
    q = q_ref[0]
    lane = lax.broadcasted_iota(jnp.int32, (tq, LANE), 1)

    def update(carry, n, k, v, bias):
        m, acc = carry
        sl = slice(n * LANE, (n + 1) * LANE)
        s = _dot_nt(q[:, sl], k[:, sl])
        if bias is not None:
            s = s + bias
        m_new = jnp.maximum(m, jnp.max(s, axis=1, keepdims=True))
        p = jnp.exp(s - m_new)
        return m_new, jnp.exp(m - m_new) * acc + _dot(p.astype(MXU), v[:, sl])

    def body(kt, carry):
        k0 = pl.multiple_of(kt * tk, tk)
        k = k_ref[0, pl.ds(k0, tk), :]
        v = v_ref[0, pl.ds(k0, tk), :]
        return tuple(update(carry[n], n, k, v, None) for n in range(2))

    init = tuple((jnp.full((tq, 1), NEG, f32), jnp.zeros((tq, LANE), f32)) for _ in range(2))
    carry = lax.fori_loop(0, qi, body, init)
    k = k_ref[0, pl.ds(q0, tk), :]
    v = v_ref[0, pl.ds(q0, tk), :]
    o = []
    for n in range(2):
        _, acc = update(carry[n], n, k, v, cb_ref[...])
        denom = jnp.sum(jnp.where(lane == MLA_V, acc, 0.0), axis=1, keepdims=True)
        o.append(acc / jnp.maximum(denom, 1e-30))
    o_ref[0] = jnp.where(lane < MLA_V, o[0], pltpu.roll(o[1], MLA_V, 1))


def mla_attention(h, q_norm, kv_norm, w_uq, w_ukv, tr=512, tq=512):
    B, S, _ = h.shape
    H = MLA_HEADS
    dq = MLA_NOPE + MLA_ROPE
    HW = H * LANE
    wq3 = w_uq.reshape(MLA_Q_RANK, H, dq)
    wq_rot = jnp.concatenate([jnp.zeros_like(wq3[..., :MLA_NOPE]), _rot_half(wq3[..., MLA_NOPE:])], axis=-1)

    def pad_q(w3):
        w3 = jnp.pad(w3, ((0, 256 - MLA_Q_RANK), (0, 0), (0, LANE - dq)))
        return w3.reshape(256, HW).astype(MXU)

    wkv3 = w_ukv.reshape(MLA_KV_RANK, H, MLA_NOPE + MLA_V)
    wk = jnp.pad(wkv3[..., :MLA_NOPE], ((0, 0), (0, 0), (0, LANE - MLA_NOPE))).reshape(MLA_KV_RANK, HW).astype(MXU)
    wv = jnp.pad(wkv3[..., MLA_NOPE:], ((0, 0), (0, 0), (0, LANE - MLA_V))).reshape(MLA_KV_RANK, HW).astype(MXU)
    v_one = jnp.tile(jnp.concatenate([jnp.zeros((1, MLA_V), f32), jnp.ones((1, LANE - MLA_V), f32)], axis=1), (1, H))
    gq = jnp.pad(q_norm, (0, 256 - MLA_Q_RANK)).reshape(1, 256)
    gkv = kv_norm.reshape(1, MLA_KV_RANK)
    inv = ROPE_THETA ** (-jnp.arange(0, MLA_ROPE, 2, dtype=f32) / MLA_ROPE)
    ang = jnp.arange(S, dtype=f32)[:, None] * inv[None, :]
    cos2 = jnp.concatenate([jnp.cos(ang), jnp.cos(ang)], axis=1)
    sin2 = jnp.concatenate([jnp.sin(ang), jnp.sin(ang)], axis=1)
    tail = LANE - dq
    cos_k = jnp.concatenate([jnp.zeros((S, MLA_NOPE), f32), cos2, jnp.zeros((S, tail), f32)], axis=1)
    sin_k = jnp.concatenate([jnp.zeros((S, MLA_NOPE), f32), sin2, jnp.zeros((S, tail), f32)], axis=1)
    cos_q = jnp.tile(jnp.concatenate([jnp.ones((S, MLA_NOPE), f32), cos2, jnp.zeros((S, tail), f32)], axis=1), (1, H))
    sin_q = jnp.tile(sin_k, (1, H))
    rel = np.arange(tq)[:, None] - np.arange(tq)[None, :]
    causal = jnp.asarray(np.where(rel >= 0, 0.0, BIG_NEG).astype(np.float32))

    c_cq, c_ckv, c_kr, c_krs = OFF["dcq"] // 256, OFF["dckv"] // LANE, OFF["dkr"] // LANE, OFF["dkrs"] // LANE
    full = lambda shape: pl.BlockSpec(shape, lambda b, i: (0, 0))
    tab = lambda w: pl.BlockSpec((tr, w), lambda b, i: (i, 0))
    out3 = pl.BlockSpec((1, tr, HW), lambda b, i: (b, i, 0))
    q, k, v = pl.pallas_call(
        _mla_prep_kernel,
        grid=(B, S // tr),
        in_specs=[pl.BlockSpec((1, tr, 256), lambda b, i: (b, i, c_cq)),
                  pl.BlockSpec((1, tr, LANE), lambda b, i: (b, i, c_ckv)),
                  pl.BlockSpec((1, tr, LANE), lambda b, i: (b, i, c_kr)),
                  pl.BlockSpec((1, tr, LANE), lambda b, i: (b, i, c_krs)),
                  full((1, 256)), full((1, MLA_KV_RANK)), full((256, HW)), full((256, HW)),
                  full((MLA_KV_RANK, HW)), full((MLA_KV_RANK, HW)),
                  tab(HW), tab(HW), tab(LANE), tab(LANE), full((1, HW))],
        out_specs=[out3, out3, out3],
        out_shape=[jax.ShapeDtypeStruct((B, S, HW), MXU)] * 3,
        compiler_params=_params(2),
        name="mla_prep",
    )(h, h, h, h, gq, gkv, pad_q(wq3), pad_q(wq_rot), wk, wv, cos_q, sin_q, cos_k, sin_k, v_one)
    pair = lambda rows: pl.BlockSpec((1, rows, 2 * LANE), lambda b, p, i: (b, i if rows == tq else 0, p))
    return pl.pallas_call(
        functools.partial(_mla_attn_kernel, tq=tq),
        grid=(B, H // 2, S // tq),
        in_specs=[pair(tq), pair(S), pair(S), pl.BlockSpec((tq, tq), lambda b, p, i: (0, 0))],
        out_specs=pl.BlockSpec((1, tq, LANE), lambda b, p, i: (b, i, p)),
        out_shape=jax.ShapeDtypeStruct((B, S, H * MLA_V), f32),
        compiler_params=_params(3),
        name="mla_attention",
    )(q, k, v, causal)


def _ln(z, g, b):
    mu = jnp.mean(z, axis=-1, keepdims=True)
    zc = z - mu
    var = jnp.mean(zc * zc, axis=-1, keepdims=True)
    return zc * lax.rsqrt(var + LN_EPS) * g + b


def _merge_kernel(mg_ref, oa_ref, ob_ref, oc_ref, od_ref, wb_ref, wo_ref, x_ref, g_ref, b_ref, o_ref):
    acc = None
    for n, br in enumerate((oa_ref, ob_ref, oc_ref, od_ref)):
        up = _dot(br[...].astype(MXU), wb_ref[n])
        term = jax.nn.sigmoid(mg_ref[:, n * D_MODEL:(n + 1) * D_MODEL]) * up
        acc = term if acc is None else acc + term
    y = _dot(acc.astype(MXU), wo_ref[...])
    o_ref[...] = _ln(DN_ALPHA * x_ref[...] + y, g_ref[...], b_ref[...])


def merge_ln(h2, branches, w_branch, w_out, x2, g, b, tm=256):
    N, D = x2.shape
    assert OFF["mg"] == 0 and N % tm == 0
    row = lambda w: pl.BlockSpec((tm, w), lambda i: (i, 0))
    return pl.pallas_call(
        _merge_kernel,
        grid=(N // tm,),
        in_specs=[row(N_MIXERS * D)] + [row(MIX_W)] * N_MIXERS
        + [pl.BlockSpec((N_MIXERS, MIX_W, D), lambda i: (0, 0, 0)), pl.BlockSpec((D, D), lambda i: (0, 0)),
           row(D), pl.BlockSpec((1, D), lambda i: (0, 0)), pl.BlockSpec((1, D), lambda i: (0, 0))],
        out_specs=row(D),
        out_shape=jax.ShapeDtypeStruct((N, D), f32),
        compiler_params=_params(1),
        name="merge_ln",
    )(h2, *branches, w_branch.astype(MXU), w_out.astype(MXU), x2, g.reshape(1, D), b.reshape(1, D))


ROW_TILE = 8


def _to_token_tiles(ref, val):
    rows = val.shape[0]
    for j in range(ROW_TILE):
        ref[pl.ds(j, rows, stride=ROW_TILE), :] = val[:, j * LANE:(j + 1) * LANE]


def _from_token_tiles(ref, rows, first=0, stride=ROW_TILE):
    return jnp.concatenate([ref[pl.ds(first + j, rows, stride=stride), :] for j in range(ROW_TILE)], axis=1)


def _xattn_kernel(x_ref, wq_ref, k_ref, v_ref, wo_ref, g_ref, b_ref, o_ref, o8_ref):
    x = x_ref[0]
    q = _dot(x.astype(MXU), wq_ref[...]).astype(MXU)
    k = k_ref[0]
    v = v_ref[0]
    heads = []
    for hd in range(X_HEADS):
        sl = slice(hd * X_HEAD_DIM, (hd + 1) * X_HEAD_DIM)
        s = _dot_nt(q[:, sl], k[:, sl]) * (X_HEAD_DIM ** -0.5)
        e = jnp.exp(s - jnp.max(s, axis=1, keepdims=True))
        p = e / jnp.sum(e, axis=1, keepdims=True)
        heads.append(_dot(p.astype(MXU), v[:, sl]).astype(MXU))
    y = _dot(jnp.concatenate(heads, axis=1), wo_ref[...])
    out = _ln(DN_ALPHA * x + y, g_ref[...], b_ref[...])
    o_ref[0] = out
    _to_token_tiles(o8_ref.at[0], out)


def cross_attention_ln(x, mem, wq, wkv, wo, g, b, tq=256):
    B, S, D = x.shape
    assert D == ROW_TILE * LANE
    M = mem.shape[1]
    F = X_HEADS * X_HEAD_DIM
    kv = matmul(mem.reshape(B * M, D), wkv.astype(MXU), 512, 2 * F, out_dtype=MXU).reshape(B, M, 2 * F)
    full = lambda shape: pl.BlockSpec(shape, lambda bi, i: (0,) * len(shape))
    return pl.pallas_call(
        _xattn_kernel,
        grid=(B, S // tq),
        in_specs=[pl.BlockSpec((1, tq, D), lambda bi, i: (bi, i, 0)), full((D, F)),
                  pl.BlockSpec((1, M, F), lambda bi, i: (bi, 0, 0)), pl.BlockSpec((1, M, F), lambda bi, i: (bi, 0, 1)),
                  full((F, D)), full((1, D)), full((1, D))],
        out_specs=[pl.BlockSpec((1, tq, D), lambda bi, i: (bi, i, 0)),
                   pl.BlockSpec((1, tq * ROW_TILE, LANE), lambda bi, i: (bi, i, 0))],
        out_shape=[jax.ShapeDtypeStruct((B, S, D), f32), jax.ShapeDtypeStruct((B, S * ROW_TILE, LANE), f32)],
        compiler_params=_params(2),
        name="cross_attention_ln",
    )(x, wq.astype(MXU), kv, kv, wo.astype(MXU), g.reshape(1, D), b.reshape(1, D))


_R_E0, _R_E1, _R_W0, _R_W1, _R_RANK0, _R_RANK1 = range(6)
_GRP_LANE0 = N_EXPERTS


def _router_kernel(x_ref, w_ref, b_ref, tri_ref, r_ref, cnt_ref):
    i = pl.program_id(0)
    tm = x_ref.shape[0]
    logits = _dot_nt(w_ref[...], x_ref[...].astype(MXU)) + b_ref[...]
    row = lax.broadcasted_iota(jnp.int32, (LANE, tm), 0)
    row_f = row.astype(f32)
    big = float(LANE)

    def cmax(t):
        return jnp.max(t, axis=0, keepdims=True)

    def first_row(cond):
        return jnp.min(jnp.where(cond, row_f, big), axis=0, keepdims=True)

    def softmax_on(mask):
        lm = jnp.where(mask, logits, NEG)
        e = jnp.where(mask, jnp.exp(lm - cmax(lm)), 0.0)
        return e / jnp.sum(e, axis=0, keepdims=True)

    is_g = (row >= _GRP_LANE0) & (row < _GRP_LANE0 + N_GROUPS)
    p_grp = softmax_on(is_g)
    p_g = cmax(p_grp)
    grp = first_row(is_g & (p_grp == p_g)) - float(_GRP_LANE0)
    in_grp = (row < N_EXPERTS) & (jnp.right_shift(row, 3).astype(f32) == grp)
    p_e = softmax_on(in_grp)
    p1 = cmax(jnp.where(in_grp, p_e, -1.0))
    e1 = first_row(in_grp & (p_e == p1))
    rest = in_grp & (row_f != e1)
    p2 = cmax(jnp.where(rest, p_e, -1.0))
    e2 = first_row(rest & (p_e == p2))
    w1 = p_g * p1 / (p1 + p2)
    w2 = p_g * p2 / (p1 + p2)

    @pl.when(i == 0)
    def _():
        cnt_ref[...] = jnp.zeros_like(cnt_ref)

    oh1 = row_f == e1
    oh2 = row_f == e2
    both = (oh1 | oh2).astype(MXU)
    before = _dot(both, tri_ref[...]) + cnt_ref[:, 0:1]
    rank1 = jnp.sum(jnp.where(oh1, before, 0.0), axis=0, keepdims=True)
    rank2 = jnp.sum(jnp.where(oh2, before, 0.0), axis=0, keepdims=True)
    cnt_ref[...] = cnt_ref[...] + jnp.sum(both.astype(f32), axis=1, keepdims=True)

    out = jnp.zeros((LANE, tm), f32)
    for slot, val in ((_R_E0, e1), (_R_E1, e2), (_R_W0, w1), (_R_W1, w2), (_R_RANK0, rank1), (_R_RANK1, rank2)):
        out = jnp.where(row == slot, val, out)
    r_ref[...] = out.T


def moe_router(x2, rg_w, rg_b, re_w, re_b, tm=256):
    N, D = x2.shape
    assert EXPERTS_PER_GROUP == 8 and N_EXPERTS + N_GROUPS <= LANE
    w = jnp.pad(jnp.concatenate([re_w, rg_w], axis=1), ((0, 0), (0, LANE - N_EXPERTS - N_GROUPS))).T.astype(MXU)
    b = jnp.pad(jnp.concatenate([re_b, rg_b]), (0, LANE - N_EXPERTS - N_GROUPS)).reshape(LANE, 1)
    tri = jnp.asarray(np.triu(np.ones((tm, tm), np.float32), 1), MXU)
    return pl.pallas_call(
        _router_kernel,
        grid=(N // tm,),
        in_specs=[pl.BlockSpec((tm, D), lambda i: (i, 0)), pl.BlockSpec((LANE, D), lambda i: (0, 0)),
                  pl.BlockSpec((LANE, 1), lambda i: (0, 0)), pl.BlockSpec((tm, tm), lambda i: (0, 0))],
        out_specs=[pl.BlockSpec((tm, LANE), lambda i: (i, 0)), pl.BlockSpec((LANE, LANE), lambda i: (0, 0))],
        out_shape=[jax.ShapeDtypeStruct((N, LANE), f32), jax.ShapeDtypeStruct((LANE, LANE), f32)],
        compiler_params=_params(1),
        name="moe_router",
    )(x2, w, b, tri)


def _ffn_kernel(ce_ref, nu_ref, x_ref, wgu_ref, wd_ref, o_ref):
    c = pl.program_id(0)

    @pl.when(c < nu_ref[0])
    def _():
        x = _from_token_tiles(x_ref, EXPERT_CHUNK)
        gu = _dot(x.astype(MXU), wgu_ref[0, 0].astype(MXU))
        hid = jax.nn.silu(gu[:, :D_EXPERT]) * gu[:, D_EXPERT:]
        _to_token_tiles(o_ref, _dot(hid.astype(MXU), wd_ref[0, 0].astype(MXU)))

    @pl.when(c >= nu_ref[0])
    def _():
        o_ref[...] = jnp.zeros_like(o_ref)


def expert_ffn(xb8, chunk_e, n_used, w_gu, w_down, layer):
    C = EXPERT_CHUNK
    n_chunks = xb8.shape[0] // (C * ROW_TILE)
    D = ROW_TILE * LANE
    tile_spec = pl.BlockSpec((C * ROW_TILE, LANE), lambda c, ce, nu: (c, 0))
    grid_spec = pltpu.PrefetchScalarGridSpec(
        num_scalar_prefetch=2,
        grid=(n_chunks,),
        in_specs=[tile_spec,
                  pl.BlockSpec((1, 1, D, 2 * D_EXPERT), lambda c, ce, nu: (layer, ce[c], 0, 0)),
                  pl.BlockSpec((1, 1, D_EXPERT, D), lambda c, ce, nu: (layer, ce[c], 0, 0))],
        out_specs=tile_spec,
    )
    return pl.pallas_call(
        _ffn_kernel,
        grid_spec=grid_spec,
        out_shape=jax.ShapeDtypeStruct(xb8.shape, f32),
        compiler_params=_params(1),
        name="expert_ffn",
    )(chunk_e, n_used, xb8, w_gu, w_down)


_COPY_WINDOW = 512


def _windowed_copies(n, copy, wait):
    assert _COPY_WINDOW & (_COPY_WINDOW - 1) == 0 and n >= _COPY_WINDOW

    def fill(j, carry):
        copy(j, j).start()
        return carry

    def steady(j, carry):
        s = jnp.bitwise_and(j, _COPY_WINDOW - 1)
        wait(s)
        copy(j, s).start()
        return carry

    def drain(s, carry):
        wait(s)
        return carry

    lax.fori_loop(0, _COPY_WINDOW, fill, 0, unroll=8)
    lax.fori_loop(_COPY_WINDOW, n, steady, 0, unroll=8)
    lax.fori_loop(0, _COPY_WINDOW, drain, 0, unroll=8)


def _dispatch_kernel(dest_ref, x8_ref, init_hbm, xb_hbm, sems, *, tm):
    n = tm * TOPK_IN_GROUP
    base = pl.program_id(0) * n

    def copy(j, s):
        t = pl.multiple_of(jnp.right_shift(j, 1) * ROW_TILE, ROW_TILE)
        d = pl.multiple_of(dest_ref[base + j] * ROW_TILE, ROW_TILE)
        return pltpu.make_async_copy(x8_ref.at[pl.ds(t, ROW_TILE)], xb_hbm.at[pl.ds(d, ROW_TILE)], sems.at[s])

    def wait(s):
        pltpu.make_async_copy(x8_ref.at[pl.ds(0, ROW_TILE)], xb_hbm.at[pl.ds(0, ROW_TILE)], sems.at[s]).wait()

    _windowed_copies(n, copy, wait)


def moe_dispatch(x8, dest, n_slots, tm=256):
    n_tok = x8.shape[0] // ROW_TILE
    assert n_tok % tm == 0 and tm * TOPK_IN_GROUP >= _COPY_WINDOW
    grid_spec = pltpu.PrefetchScalarGridSpec(
        num_scalar_prefetch=1,
        grid=(n_tok // tm,),
        in_specs=[pl.BlockSpec((tm * ROW_TILE, LANE), lambda i, dest: (i, 0)), pl.BlockSpec(memory_space=pl.ANY)],
        out_specs=pl.BlockSpec(memory_space=pl.ANY),
        scratch_shapes=[pltpu.SemaphoreType.DMA((_COPY_WINDOW,))],
    )
    return pl.pallas_call(
        functools.partial(_dispatch_kernel, tm=tm),
        grid_spec=grid_spec,
        out_shape=jax.ShapeDtypeStruct((n_slots * ROW_TILE, LANE), x8.dtype),
        input_output_aliases={2: 0},
        compiler_params=_params(1),
        name="moe_dispatch",
    )(dest, x8, jnp.zeros((n_slots * ROW_TILE, LANE), x8.dtype))


def _combine_kernel(dest_ref, x_ref, r_ref, g_ref, b_ref, yb_hbm, o_ref, buf, sems):
    tm = x_ref.shape[0]
    n = tm * TOPK_IN_GROUP
    base = pl.program_id(0) * n

    def copy(j, s):
        src = pl.multiple_of(dest_ref[base + j] * ROW_TILE, ROW_TILE)
        dst = pl.multiple_of(j * ROW_TILE, ROW_TILE)
        return pltpu.make_async_copy(yb_hbm.at[pl.ds(src, ROW_TILE)], buf.at[pl.ds(dst, ROW_TILE)], sems.at[s])

    def wait(s):
        pltpu.make_async_copy(yb_hbm.at[pl.ds(0, ROW_TILE)], buf.at[pl.ds(0, ROW_TILE)], sems.at[s]).wait()

    _windowed_copies(n, copy, wait)
    r = r_ref[...]
    lane = lax.broadcasted_iota(jnp.int32, r.shape, 1)
    w0 = jnp.sum(jnp.where(lane == _R_W0, r, 0.0), axis=1, keepdims=True)
    w1 = jnp.sum(jnp.where(lane == _R_W1, r, 0.0), axis=1, keepdims=True)
    pair = TOPK_IN_GROUP * ROW_TILE
    y = _from_token_tiles(buf, tm, 0, pair) * w0 + _from_token_tiles(buf, tm, ROW_TILE, pair) * w1
    o_ref[...] = _ln(DN_ALPHA * x_ref[...] + y, g_ref[...], b_ref[...])


def combine_ln(x2, yb8, dest, r, g, b, tm=256):
    N, D = x2.shape
    assert N % tm == 0 and tm * TOPK_IN_GROUP >= _COPY_WINDOW
    row = lambda w: pl.BlockSpec((tm, w), lambda i, dest: (i, 0))
    vec = pl.BlockSpec((1, D), lambda i, dest: (0, 0))
    grid_spec = pltpu.PrefetchScalarGridSpec(
        num_scalar_prefetch=1,
        grid=(N // tm,),
        in_specs=[row(D), row(LANE), vec, vec, pl.BlockSpec(memory_space=pl.ANY)],
        out_specs=row(D),
        scratch_shapes=[pltpu.VMEM((tm * TOPK_IN_GROUP * ROW_TILE, LANE), f32),
                        pltpu.SemaphoreType.DMA((_COPY_WINDOW,))],
    )
    return pl.pallas_call(
        _combine_kernel,
        grid_spec=grid_spec,
        out_shape=jax.ShapeDtypeStruct((N, D), f32),
        compiler_params=_params(1),
        name="moe_combine_ln",
    )(dest, x2, r, g.reshape(1, D), b.reshape(1, D), yb8)


def hier_moe_ln(x2, x8, rg_w, rg_b, re_w, re_b, w_gu, w_down, layer, g, b):
    N, D = x2.shape
    E, C, K = N_EXPERTS, EXPERT_CHUNK, TOPK_IN_GROUP
    A = N * K
    r, cnt = moe_router(x2, rg_w, rg_b, re_w, re_b)
    e = r[:, _R_E0:_R_E1 + 1].astype(jnp.int32)
    rank = r[:, _R_RANK0:_R_RANK1 + 1].astype(jnp.int32)
    counts = cnt[:E, 0].astype(jnp.int32)
    padded = (counts + C - 1) // C * C
    pad_end = jnp.cumsum(padded)
    pad_start = pad_end - padded
    dest = pad_start[e] + rank
    n_chunks = -(-(A + E * (C - 1)) // C)
    P = n_chunks * C
    chunk_start = jnp.arange(n_chunks, dtype=jnp.int32) * C
    chunk_e = jnp.minimum(jnp.sum((pad_end[None, :] <= chunk_start[:, None]).astype(jnp.int32), axis=1), E - 1)
    n_used = (pad_end[-1] // C).reshape(1).astype(jnp.int32)
    dest = dest.reshape(A).astype(jnp.int32)
    xb8 = moe_dispatch(x8, dest, P)
    yb8 = expert_ffn(xb8, chunk_e.astype(jnp.int32), n_used, w_gu, w_down, layer)
    return combine_ln(x2, yb8, dest, r, g, b)


def kernel(x, mem, w_in, nsa_cmp_pos, nsa_cmp_w1, nsa_cmp_w2, rnn_conv_w, rnn_conv_b, rnn_ga_w, rnn_ga_b,
           rnn_gx_w, rnn_gx_b, rnn_lambda, mla_q_norm, mla_kv_norm, mla_w_uq, mla_w_ukv, w_branch, w_out,
           ln1_g, ln1_b, x_wq, x_wkv, x_wo, ln2_g, ln2_b, moe_rg_w, moe_rg_b, moe_re_w, moe_re_b,
           moe_w_gu, moe_w_down, ln3_g, ln3_b):
    B, S, D = x.shape
    N = B * S
    x2 = x.reshape(N, D)
    for l in range(DEPTH):
        h2 = matmul(x2, _cat_w_in(w_in[l]), 512, _IN_TILE)
        h = h2.reshape(B, S, N_CAT)
        o_a = nsa_attention(h, nsa_cmp_pos[l], nsa_cmp_w1[l], nsa_cmp_w2[l])
        o_b = stick_breaking_attention(h)
        o_c = rglru_block(h, rnn_conv_w[l], rnn_conv_b[l], rnn_ga_w[l], rnn_ga_b[l], rnn_gx_w[l], rnn_gx_b[l],
                          rnn_lambda[l])
        o_d = mla_attention(h, mla_q_norm[l], mla_kv_norm[l], mla_w_uq[l], mla_w_ukv[l])
        branches = [o.reshape(N, MIX_W) for o in (o_a, o_b, o_c, o_d)]
        x2 = merge_ln(h2, branches, w_branch[l], w_out[l], x2, ln1_g[l], ln1_b[l])
        x3, x8 = cross_attention_ln(x2.reshape(B, S, D), mem, x_wq[l], x_wkv[l], x_wo[l], ln2_g[l], ln2_b[l])
        x2 = hier_moe_ln(x3.reshape(N, D), x8.reshape(N * ROW_TILE, LANE), moe_rg_w[l], moe_rg_b[l], moe_re_w[l],
                         moe_re_b[l], moe_w_gu, moe_w_down, l, ln3_g[l], ln3_b[l])
    return x2.reshape(B, S, D)
```

```python
import functools

import numpy as np
import jax
import jax.numpy as jnp
from jax import lax
from jax.experimental import pallas as pl
from jax.experimental.pallas import tpu as pltpu

D_MODEL = 1024
DEPTH = 4
HEAD_DIM = 64
N_MIXERS = 4
MIX_W = 256
NSA_HEADS = 4
NSA_KV_HEADS = 2
NSA_GROUP = NSA_HEADS // NSA_KV_HEADS
CMP_LEN = 32
CMP_STRIDE = 16
CMP_HID = 256
SEL_BLOCK = 64
SEL_TOPK = 8
WINDOW = 512
FORCE_SCORE = 1e4
SB_HEADS = 4
RNN_W = 256
CONV_W = 4
LRU_C = 8.0
MLA_HEADS = 4
MLA_Q_RANK = 192
MLA_KV_RANK = 128
MLA_NOPE = 64
MLA_ROPE = 32
MLA_V = 64
ROPE_THETA = 10000.0
X_HEADS = 4
X_HEAD_DIM = 128
N_GROUPS = 4
EXPERTS_PER_GROUP = 8
N_EXPERTS = N_GROUPS * EXPERTS_PER_GROUP
TOPK_IN_GROUP = 2
D_EXPERT = 512
EXPERT_CHUNK = 256
DN_ALPHA = (2.0 * DEPTH) ** 0.25
LN_EPS = 1e-5
RMS_EPS = 1e-6

IN_SPLITS = ((NSA_HEADS * HEAD_DIM,) + (NSA_KV_HEADS * HEAD_DIM,) * 6 + (NSA_HEADS * 3,)
             + (SB_HEADS * HEAD_DIM,) * 3
             + (RNN_W, RNN_W)
             + (MLA_Q_RANK, MLA_KV_RANK, MLA_ROPE)
             + (N_MIXERS * D_MODEL,))
IN_OFFSETS = tuple(int(o) for o in np.concatenate([[0], np.cumsum(IN_SPLITS)[:-1]]))

LANE = 128
VMEM_LIMIT = 48 * 1024 * 1024
NEG = -1e30
BIG_NEG = -2.0 ** 100

f32 = jnp.float32
MXU = jnp.bfloat16

_SECTIONS = (("mg", 4096), ("aq", 256), ("ks", 256), ("vs", 256), ("kw", 256), ("vw", 256), ("bq", 256),
             ("bk", 256), ("bv", 256), ("cx", 256), ("cg", 256), ("dcq", 256), ("kc", 128), ("vc", 128),
             ("ag", 128), ("dckv", 128), ("dkr", 128), ("dkrs", 128))
_IN_TILE = 3840
OFF = {}
_o = 0
for _n, _w in _SECTIONS:
    OFF[_n] = _o
    _o += _w
N_CAT = -(-_o // _IN_TILE) * _IN_TILE

_AUG_SEL = HEAD_DIM
_AUG_POS = HEAD_DIM + 32
_POS_SPLIT = 256


def _rot_half(t):
    d = t.shape[-1]
    return jnp.concatenate([-t[..., d // 2:], t[..., :d // 2]], axis=-1)


def _cat_w_in(w):
    def sec(i):
        return w[:, IN_OFFSETS[i]:IN_OFFSETS[i] + IN_SPLITS[i]]

    def spread(t):
        z = jnp.zeros_like(t[:, :HEAD_DIM])
        return jnp.concatenate([t[:, :HEAD_DIM], z, t[:, HEAD_DIM:], z], axis=1)

    def padc(t, n):
        return jnp.pad(t, ((0, 0), (0, n - t.shape[1])))

    def rope_slot(t):
        return jnp.pad(t, ((0, 0), (MLA_NOPE, LANE - MLA_NOPE - MLA_ROPE)))

    parts = {"mg": sec(16), "aq": sec(0), "kc": sec(1), "vc": sec(2), "ks": spread(sec(3)), "vs": spread(sec(4)),
             "kw": spread(sec(5)), "vw": spread(sec(6)), "ag": padc(sec(7), 128), "bq": sec(8), "bk": sec(9),
             "bv": sec(10), "cx": sec(11), "cg": sec(12), "dcq": padc(sec(13), 256), "dckv": sec(14),
             "dkr": rope_slot(sec(15)), "dkrs": rope_slot(_rot_half(sec(15)))}
    cat = jnp.concatenate([parts[n] for n, _ in _SECTIONS], axis=1)
    return padc(cat, N_CAT).astype(MXU)


def _dot(a, b):
    return jnp.dot(a, b, preferred_element_type=f32)


def _dot_nt(a, b):
    return lax.dot_general(a, b, (((1,), (1,)), ((), ())), preferred_element_type=f32)


def _params(n_axes):
    return pltpu.CompilerParams(dimension_semantics=("arbitrary",) * n_axes, vmem_limit_bytes=VMEM_LIMIT)


def _mm_kernel(a_ref, b_ref, o_ref):
    o_ref[...] = _dot(a_ref[...].astype(MXU), b_ref[...]).astype(o_ref.dtype)


def matmul(a, b, tm, tn, out_dtype=f32):
    M, K = a.shape
    _, N = b.shape
    assert M % tm == 0 and N % tn == 0
    return pl.pallas_call(
        _mm_kernel,
        grid=(N // tn, M // tm),
        in_specs=[pl.BlockSpec((tm, K), lambda j, i: (i, 0)),
                  pl.BlockSpec((K, tn), lambda j, i: (0, j))],
        out_specs=pl.BlockSpec((tm, tn), lambda j, i: (i, j)),
        out_shape=jax.ShapeDtypeStruct((M, N), out_dtype),
        compiler_params=_params(2),
        name="matmul",
    )(a, b)


def _cmp_kernel(t_ref, w1_ref, pos_ref, w2_ref, o_ref):
    half = t_ref.shape[3]
    t = t_ref[0, 0].astype(MXU)
    y1 = _dot(t, w1_ref[0, :half, :])
    y2 = _dot(t, w1_ref[0, half:, :])
    pos = jnp.broadcast_to(pos_ref[0], (8, 2 * half)).astype(MXU)
    pc = _dot(pos, w1_ref[0])[0:1]
    nc = y2.shape[0]
    hid = y1 + pltpu.roll(y2, nc - 1, 0) + pc
    o_ref[0, 0] = _dot(jax.nn.gelu(hid).astype(MXU), w2_ref[0])


def nsa_compress(t, cmp_w1, cmp_pos, cmp_w2):
    _, B, NC, F = t.shape
    G = NSA_KV_HEADS
    eye = jnp.eye(G, dtype=f32)
    w1 = jnp.einsum('jldh,gk->jlgdkh', cmp_w1.reshape(2, CMP_LEN, HEAD_DIM, CMP_HID), eye)
    w1 = w1.reshape(2, 2 * F, G * CMP_HID).astype(MXU)
    pos = jnp.broadcast_to(cmp_pos[:, :, None, :], (2, CMP_LEN, G, HEAD_DIM)).reshape(2, 1, 2 * F)
    w2 = jnp.einsum('jhc,gk->jghkc', jnp.concatenate([cmp_w2, cmp_w2], axis=-1), eye)
    w2 = w2.reshape(2, G * CMP_HID, G * LANE).astype(MXU)
    return pl.pallas_call(
        _cmp_kernel,
        grid=(2, B),
        in_specs=[pl.BlockSpec((1, 1, NC, F), lambda j, i: (j, i, 0, 0)),
                  pl.BlockSpec((1, 2 * F, G * CMP_HID), lambda j, i: (j, 0, 0)),
                  pl.BlockSpec((1, 1, 2 * F), lambda j, i: (j, 0, 0)),
                  pl.BlockSpec((1, G * CMP_HID, G * LANE), lambda j, i: (j, 0, 0))],
        out_specs=pl.BlockSpec((1, 1, NC, G * LANE), lambda j, i: (j, i, 0, 0)),
        out_shape=jax.ShapeDtypeStruct((2, B, NC, G * LANE), f32),
        compiler_params=_params(2),
        name="nsa_compress",
    )(t, w1, pos, w2)


def _nsa_kernel(q_ref, kc_ref, vc_ref, ks_ref, vs_ref, kw_ref, vw_ref, gl_ref, cover_ref, aug_ref, cb_ref, wb_ref,
                o_ref, ksa, vsa, kwa, vwa, *, tq, n_cmp, n_sel, n_top):
    tk = tq
    g = pl.program_id(1)
    qi = pl.program_id(2)
    q0 = pl.multiple_of(qi * tq, tq)
    lane = lax.broadcasted_iota(jnp.int32, (tq, LANE), 1)
    lo_half = lane < HEAD_DIM

    @pl.when(qi == 0)
    def _():
        real = lax.broadcasted_iota(jnp.int32, ksa.shape, 1) < HEAD_DIM
        aug = aug_ref[...]
        ones = jnp.ones(ksa.shape, MXU)
        ksa[...] = jnp.where(real, ks_ref[0].astype(MXU), aug)
        kwa[...] = jnp.where(real, kw_ref[0].astype(MXU), aug)
        vsa[...] = jnp.where(real, vs_ref[0].astype(MXU), ones)
        vwa[...] = jnp.where(real, vw_ref[0].astype(MXU), ones)

    q = q_ref[0] * (HEAD_DIM ** -0.5)
    q_heads = (jnp.where(lo_half, q, 0.0), jnp.where(lo_half, pltpu.roll(q, HEAD_DIM, 1), 0.0))
    alibi = [2.0 ** (-8.0 * (h + 1) / NSA_HEADS) for h in range(NSA_HEADS)]
    slopes = [jnp.where(g == 0, alibi[n], alibi[NSA_GROUP + n]) for n in range(NSA_GROUP)]
    pos_cols = [jnp.where(lane == _AUG_POS, slopes[n] * _POS_SPLIT, jnp.where(lane == _AUG_POS + 1, slopes[n], 0.0))
                for n in range(NSA_GROUP)]
    tpos = q0 + lax.broadcasted_iota(jnp.int32, (tq, 1), 0)

    nc = kc_ref.shape[2]
    cidx = lax.broadcasted_iota(jnp.int32, (1, nc), 1)
    dist_c = tpos - (cidx * CMP_STRIDE + (CMP_LEN - 1))
    mask_c = (dist_c >= 0) & (cidx < n_cmp)
    dist_cf = dist_c.astype(f32)
    kc = kc_ref[0, 0].astype(MXU)
    vc = vc_ref[0, 0].astype(MXU)
    o_cmp = []
    imp_t = jnp.zeros((LANE, tq), f32)
    for n in range(NSA_GROUP):
        s = _dot_nt(q_heads[n].astype(MXU), kc) - slopes[n] * dist_cf
        sm = jnp.where(mask_c, s, NEG)
        m = jnp.max(sm, axis=1, keepdims=True)
        p = jnp.where(mask_c, jnp.exp(sm - m), 0.0)
        p = (p / jnp.maximum(jnp.sum(p, axis=1, keepdims=True), 1e-30)).astype(MXU)
        o_cmp.append(_dot(p, vc))
        imp_t = imp_t + _dot_nt(cover_ref[...], p)

    rows = 32
    imp = imp_t[:rows]
    blk = lax.broadcasted_iota(jnp.int32, (rows, tq), 0)
    blk_f = blk.astype(f32)
    tpos_t = q0 + lax.broadcasted_iota(jnp.int32, (1, tq), 1)
    forced = (blk == 0) | (blk == jnp.right_shift(tpos_t, 6))
    valid = blk * SEL_BLOCK <= tpos_t
    imp = jnp.where(forced, FORCE_SCORE, jnp.where(valid, imp, -1.0))
    imp = jnp.where(blk < n_sel, imp, NEG)
    sel_t = jnp.zeros((rows, tq), f32)
    for _ in range(n_top):
        m = jnp.max(imp, axis=0, keepdims=True)
        first = jnp.min(jnp.where(imp == m, blk_f, float(LANE)), axis=0, keepdims=True)
        pick = blk_f == first
        sel_t = jnp.where(pick, 1.0, sel_t)
        imp = jnp.where(pick, 2 * NEG, imp)
    sel = jnp.concatenate([sel_t, jnp.zeros((LANE - rows, tq), f32)], axis=0).T
    sel_bias = pltpu.roll(jnp.where(sel > 0.5, 0.0, BIG_NEG), _AUG_SEL, 1)
    sel_cols = jnp.where((lane >= _AUG_SEL) & (lane < _AUG_SEL + rows), sel_bias, 0.0)
    q_sel = jnp.concatenate([q_heads[n] + sel_cols + pos_cols[n] for n in range(NSA_GROUP)], axis=0).astype(MXU)
    q_win = jnp.concatenate([q_heads[n] + pos_cols[n] for n in range(NSA_GROUP)], axis=0).astype(MXU)
    rows_q = NSA_GROUP * tq
    denom_lane = lax.broadcasted_iota(jnp.int32, (rows_q, LANE), 1) == HEAD_DIM

    def update(carry, qs, k, v, bias):
        m, acc = carry
        s = _dot_nt(qs, k)
        if bias is not None:
            s = s + bias
        m_new = jnp.maximum(m, jnp.max(s, axis=1, keepdims=True))
        p = jnp.exp(s - m_new)
        return m_new, jnp.exp(m - m_new) * acc + _dot(p.astype(MXU), v)

    def finish(carry):
        _, acc = carry
        denom = jnp.sum(jnp.where(denom_lane, acc, 0.0), axis=1, keepdims=True)
        o = acc / jnp.maximum(denom, 1e-30)
        return [o[n * tq:(n + 1) * tq] for n in range(NSA_GROUP)]

    init = (jnp.full((rows_q, 1), NEG, f32), jnp.zeros((rows_q, LANE), f32))
    causal = cb_ref[...]

    def sel_body(kt, carry):
        k0 = pl.multiple_of(kt * tk, tk)
        return update(carry, q_sel, ksa[pl.ds(k0, tk), :], vsa[pl.ds(k0, tk), :], None)

    carry = lax.fori_loop(0, qi, sel_body, init)
    o_sel = finish(update(carry, q_sel, ksa[pl.ds(q0, tk), :], vsa[pl.ds(q0, tk), :], causal))

    carry = init
    n_back = WINDOW // tk
    for back in range(n_back, -1, -1):
        k0 = pl.multiple_of(jnp.maximum(qi - back, 0) * tk, tk)
        if back:
            off = jnp.where(qi >= back, 0.0, BIG_NEG)
            bias = wb_ref[...] + off if back == n_back else off
        else:
            bias = causal
        carry = update(carry, q_win, kwa[pl.ds(k0, tk), :], vwa[pl.ds(k0, tk), :], bias)
    o_win = finish(carry)

    sig = jax.nn.sigmoid(gl_ref[0])

    def gate(n, j):
        col = 3 * (NSA_GROUP * g + n) + j
        return jnp.sum(jnp.where(lane == col, sig, 0.0), axis=1, keepdims=True)

    o = [gate(n, 0) * o_cmp[n] + gate(n, 1) * o_sel[n] + gate(n, 2) * o_win[n] for n in range(NSA_GROUP)]
    o_ref[0] = jnp.where(lo_half, o[0], pltpu.roll(o[1], HEAD_DIM, 1))


def nsa_attention(h, cmp_pos, cmp_w1, cmp_w2, tq=512):
    B, S, _ = h.shape
    G = NSA_KV_HEADS
    NC = S // CMP_STRIDE
    n_cmp = (S - CMP_LEN) // CMP_STRIDE + 1
    n_sel = S // SEL_BLOCK
    n_top = min(SEL_TOPK, n_sel)
    F = CMP_STRIDE * HEAD_DIM
    assert S % tq == 0 and WINDOW % tq == 0 and tq % SEL_BLOCK == 0 and n_sel <= 32 and S <= _POS_SPLIT ** 2

    def chunks(off):
        return h[:, :, off:off + G * HEAD_DIM].reshape(B, NC, CMP_STRIDE * G * HEAD_DIM)

    kvc = nsa_compress(jnp.stack([chunks(OFF["kc"]), chunks(OFF["vc"])]), cmp_w1, cmp_pos, cmp_w2)

    c0 = np.arange(n_cmp)[:, None] * CMP_STRIDE
    j0 = np.arange(n_sel)[None, :] * SEL_BLOCK
    cover = np.clip(np.minimum(c0 + CMP_LEN, j0 + SEL_BLOCK) - np.maximum(c0, j0), 0, None) / CMP_LEN
    cover_t = np.zeros((LANE, NC), np.float32)
    cover_t[:n_sel, :n_cmp] = cover.T
    pos = np.arange(S)
    aug = np.zeros((S, LANE), np.float32)
    aug[pos, _AUG_SEL + pos // SEL_BLOCK] = 1.0
    aug[:, _AUG_POS] = pos // _POS_SPLIT
    aug[:, _AUG_POS + 1] = pos % _POS_SPLIT
    rel = np.arange(tq)[:, None] - np.arange(tq)[None, :]
    rel = np.tile(rel, (NSA_GROUP, 1))
    causal = np.where(rel >= 0, 0.0, BIG_NEG).astype(np.float32)
    win_lo = np.where(rel < 0, 0.0, BIG_NEG).astype(np.float32)

    col = lambda name: OFF[name] // LANE
    full2 = lambda shape: pl.BlockSpec(shape, lambda b, g, i: (0, 0))
    q_spec = pl.BlockSpec((1, tq, LANE), lambda b, g, i: (b, i, col("aq") + g))
    c_specs = [pl.BlockSpec((1, 1, NC, LANE), lambda b, g, i, j=j: (j, b, 0, g)) for j in range(2)]
    kv_specs = [pl.BlockSpec((1, S, LANE), lambda b, g, i, c=col(n): (b, 0, c + g))
                for n in ("ks", "vs", "kw", "vw")]
    gl_spec = pl.BlockSpec((1, tq, LANE), lambda b, g, i: (b, i, col("ag")))
    return pl.pallas_call(
        functools.partial(_nsa_kernel, tq=tq, n_cmp=n_cmp, n_sel=n_sel, n_top=n_top),
        grid=(B, G, S // tq),
        in_specs=[q_spec] + c_specs + kv_specs + [gl_spec, full2((LANE, NC)), full2((S, LANE)),
                                                  full2((NSA_GROUP * tq, tq)), full2((NSA_GROUP * tq, tq))],
        out_specs=pl.BlockSpec((1, tq, LANE), lambda b, g, i: (b, i, g)),
        out_shape=jax.ShapeDtypeStruct((B, S, NSA_HEADS * HEAD_DIM), f32),
        scratch_shapes=[pltpu.VMEM((S, LANE), MXU)] * 4,
        compiler_params=_params(3),
        name="nsa_attention",
    )(h, kvc, kvc, h, h, h, h, h, jnp.asarray(cover_t, MXU), jnp.asarray(aug, MXU), jnp.asarray(causal),
      jnp.asarray(win_lo))


_SB_DEAD = -104.0


def _log_sigmoid(z):
    return jnp.minimum(z, 0.0) - jnp.log1p(jnp.exp(-jnp.abs(z)))


def _sb_kernel(q_ref, k_ref, v_ref, u_ref, o_ref, kb, vb, *, tq):
    tk = tq
    qi = pl.program_id(2)
    q0 = pl.multiple_of(qi * tq, tq)
    lane = lax.broadcasted_iota(jnp.int32, (tq, LANE), 1)
    lo_half = lane < HEAD_DIM

    @pl.when(qi == 0)
    def _():
        kb[...] = k_ref[0].astype(MXU)
        vb[...] = v_ref[0].astype(MXU)

    q = q_ref[0] * (HEAD_DIM ** -0.5)
    q2 = jnp.concatenate([jnp.where(lo_half, q, 0.0), jnp.where(lo_half, 0.0, q)], axis=0).astype(MXU)
    u = u_ref[...]

    def tile(carry, k, v, strict):
        c, acc = carry
        z = _dot_nt(q2, k)
        ls = _log_sigmoid(z)
        log_1m = ls - z
        if strict is not None:
            log_1m = jnp.where(strict, log_1m, 0.0)
        hi = log_1m.astype(MXU)
        lo = (log_1m - hi.astype(f32)).astype(MXU)
        tail = _dot(hi, u) + _dot(lo, u) + c
        a = jnp.exp(ls + tail)
        if strict is not None:
            a = jnp.where(strict, a, 0.0)
        return c + jnp.sum(log_1m, axis=1, keepdims=True), acc + _dot(a.astype(MXU), v)

    row = lax.broadcasted_iota(jnp.int32, (2 * tq, tk), 0)
    rel = jnp.where(row >= tq, row - tq, row) - lax.broadcasted_iota(jnp.int32, (2 * tq, tk), 1)
    zero = (jnp.zeros((2 * tq, 1), f32), jnp.zeros((2 * tq, LANE), f32))
    state = tile(zero, kb[pl.ds(q0, tk), :], vb[pl.ds(q0, tk), :], rel > 0)

    def alive(state):
        return (jnp.max(state[0]) > _SB_DEAD).astype(jnp.int32)

    def cond(loop):
        kt, live, _ = loop
        return (kt >= 0) & (live > 0)

    def body(loop):
        kt, _, state = loop
        k0 = pl.multiple_of(kt * tk, tk)
        state = tile(state, kb[pl.ds(k0, tk), :], vb[pl.ds(k0, tk), :], None)
        return kt - 1, alive(state), state

    _, _, (_, acc) = lax.while_loop(cond, body, (qi - 1, alive(state), state))
    o_ref[0] = jnp.where(lo_half, acc[:tq], acc[tq:])


def stick_breaking_attention(h, tq=256):
    B, S, _ = h.shape
    assert S % tq == 0
    tri = jnp.asarray(np.tril(np.ones((tq, tq), np.float32), -1), MXU)
    cq, ck, cv = (OFF[n] // LANE for n in ("bq", "bk", "bv"))
    return pl.pallas_call(
        functools.partial(_sb_kernel, tq=tq),
        grid=(B, SB_HEADS // 2, S // tq),
        in_specs=[pl.BlockSpec((1, tq, LANE), lambda b, p, i: (b, i, cq + p)),
                  pl.BlockSpec((1, S, LANE), lambda b, p, i: (b, 0, ck + p)),
                  pl.BlockSpec((1, S, LANE), lambda b, p, i: (b, 0, cv + p)),
                  pl.BlockSpec((tq, tq), lambda b, p, i: (0, 0))],
        out_specs=pl.BlockSpec((1, tq, LANE), lambda b, p, i: (b, i, p)),
        out_shape=jax.ShapeDtypeStruct((B, S, SB_HEADS * HEAD_DIM), f32),
        scratch_shapes=[pltpu.VMEM((S, LANE), MXU)] * 2,
        compiler_params=_params(3),
        name="sb_attention",
    )(h, h, h, tri)


def _neg_expm1(y):
    series = -y * (1.0 + y * (1.0 / 2 + y * (1.0 / 6 + y * (1.0 / 24 + y * (1.0 / 120)))))
    return jnp.where(y > -0.1, series, 1.0 - jnp.exp(y))


def _rglru_kernel(x_ref, xg_ref, cw_ref, cb_ref, gaw_ref, gab_ref, gxw_ref, gxb_ref, lam_ref, o_ref):
    x = x_ref[0]
    S = x.shape[0]
    row = lax.broadcasted_iota(jnp.int32, (S, 1), 0)

    def shifted(t, d, fill):
        return jnp.where(row >= d, pltpu.roll(t, d, 0), fill)

    u = cb_ref[...] + x * cw_ref[CONV_W - 1:CONV_W, :]
    for d in range(1, CONV_W):
        u = u + shifted(x, d, 0.0) * cw_ref[CONV_W - 1 - d:CONV_W - d, :]
    ub = u.astype(MXU)
    r = jax.nn.sigmoid(_dot(ub, gaw_ref[...]) + gab_ref[...])
    i = jax.nn.sigmoid(_dot(ub, gxw_ref[...]) + gxb_ref[...])
    lam = lam_ref[...]
    softplus_neg = jnp.maximum(-lam, 0.0) + jnp.log1p(jnp.exp(-jnp.abs(lam)))
    log_a = -LRU_C * r * softplus_neg
    a = jnp.exp(log_a)
    b = jnp.sqrt(_neg_expm1(2.0 * log_a)) * (i * u)
    d = 1
    while d < S:
        b = a * shifted(b, d, 0.0) + b
        a = a * shifted(a, d, 1.0)
        d *= 2
    o_ref[0] = b * jax.nn.gelu(xg_ref[0])


def _block_diag(w):
    n, c, _ = w.shape
    out = jnp.zeros((n * c, n * c), w.dtype)
    for j in range(n):
        out = out.at[j * c:(j + 1) * c, j * c:(j + 1) * c].set(w[j])
    return out


def rglru_block(h, conv_w, conv_b, ga_w, ga_b, gx_w, gx_b, lru_lambda):
    B, S, _ = h.shape
    W = RNN_W
    cx, cg = OFF["cx"] // W, OFF["cg"] // W
    vec = pl.BlockSpec((1, W), lambda b: (0, 0))
    mat = pl.BlockSpec((W, W), lambda b: (0, 0))
    return pl.pallas_call(
        _rglru_kernel,
        grid=(B,),
        in_specs=[pl.BlockSpec((1, S, W), lambda b: (b, 0, cx)), pl.BlockSpec((1, S, W), lambda b: (b, 0, cg)),
                  pl.BlockSpec((CONV_W, W), lambda b: (0, 0)), vec, mat, vec, mat, vec, vec],
        out_specs=pl.BlockSpec((1, S, W), lambda b: (b, 0, 0)),
        out_shape=jax.ShapeDtypeStruct((B, S, W), f32),
        compiler_params=_params(1),
        name="rglru",
    )(h, h, conv_w, conv_b.reshape(1, W), _block_diag(ga_w).astype(MXU), ga_b.reshape(1, W),
      _block_diag(gx_w).astype(MXU), gx_b.reshape(1, W), lru_lambda.reshape(1, W))


def _rms(x, g, width):
    return x * lax.rsqrt(jnp.sum(x * x, axis=-1, keepdims=True) * (1.0 / width) + RMS_EPS) * g


def _mla_prep_kernel(cq_ref, ckv_ref, kr_ref, krs_ref, gq_ref, gkv_ref, wq_ref, wqs_ref, wk_ref, wv_ref,
                     cosq_ref, sinq_ref, cosk_ref, sink_ref, vone_ref, q_ref, k_ref, v_ref):
    cq = _rms(cq_ref[0], gq_ref[...], MLA_Q_RANK).astype(MXU)
    ckv = _rms(ckv_ref[0], gkv_ref[...], MLA_KV_RANK).astype(MXU)
    scale = (MLA_NOPE + MLA_ROPE) ** -0.5
    q = _dot(cq, wq_ref[...]) * cosq_ref[...] + _dot(cq, wqs_ref[...]) * sinq_ref[...]
    q_ref[0] = (q * scale).astype(q_ref.dtype)
    k_rope = kr_ref[0] * cosk_ref[...] + krs_ref[0] * sink_ref[...]
    k = _dot(ckv, wk_ref[...])
    k_ref[0] = (k + jnp.concatenate([k_rope] * MLA_HEADS, axis=1)).astype(k_ref.dtype)
    v_ref[0] = (_dot(ckv, wv_ref[...]) + vone_ref[...]).astype(v_ref.dtype)


def _mla_attn_kernel(q_ref, k_ref, v_ref, cb_ref, o_ref, *, tq):
    tk = tq
    qi = pl.program_id(2)
    q0 = pl.multiple_of(qi * tq, tq)
    q = q_ref[0]
    lane = lax.broadcasted_iota(jnp.int32, (tq, LANE), 1)

    def update(carry, n, k, v, bias):
        m, acc = carry
        sl = slice(n * LANE, (n + 1) * LANE)
        s = _dot_nt(q[:, sl], k[:, sl])
        if bias is not None:
            s = s + bias
        m_new = jnp.maximum(m, jnp.max(s, axis=1, keepdims=True))
        p = jnp.exp(s - m_new)
        return m_new, jnp.exp(m - m_new) * acc + _dot(p.astype(MXU), v[:, sl])

    def body(kt, carry):
        k0 = pl.multiple_of(kt * tk, tk)
        k = k_ref[0, pl.ds(k0, tk), :]
        v = v_ref[0, pl.ds(k0, tk), :]
        return tuple(update(carry[n], n, k, v, None) for n in range(2))

    init = tuple((jnp.full((tq, 1), NEG, f32), jnp.zeros((tq, LANE), f32)) for _ in range(2))
    carry = lax.fori_loop(0, qi, body, init)
    k = k_ref[0, pl.ds(q0, tk), :]
    v = v_ref[0, pl.ds(q0, tk), :]
    o = []
    for n in range(2):
        _, acc = update(carry[n], n, k, v, cb_ref[...])
        denom = jnp.sum(jnp.where(lane == MLA_V, acc, 0.0), axis=1, keepdims=True)
        o.append(acc / jnp.maximum(denom, 1e-30))
    o_ref[0] = jnp.where(lane < MLA_V, o[0], pltpu.roll(o[1], MLA_V, 1))


def mla_attention(h, q_norm, kv_norm, w_uq, w_ukv, tr=512, tq=512):
    B, S, _ = h.shape
    H = MLA_HEADS
    dq = MLA_NOPE + MLA_ROPE
    HW = H * LANE
    wq3 = w_uq.reshape(MLA_Q_RANK, H, dq)
    wq_rot = jnp.concatenate([jnp.zeros_like(wq3[..., :MLA_NOPE]), _rot_half(wq3[..., MLA_NOPE:])], axis=-1)

    def pad_q(w3):
        w3 = jnp.pad(w3, ((0, 256 - MLA_Q_RANK), (0, 0), (0, LANE - dq)))
        return w3.reshape(256, HW).astype(MXU)

    wkv3 = w_ukv.reshape(MLA_KV_RANK, H, MLA_NOPE + MLA_V)
    wk = jnp.pad(wkv3[..., :MLA_NOPE], ((0, 0), (0, 0), (0, LANE - MLA_NOPE))).reshape(MLA_KV_RANK, HW).astype(MXU)
    wv = jnp.pad(wkv3[..., MLA_NOPE:], ((0, 0), (0, 0), (0, LANE - MLA_V))).reshape(MLA_KV_RANK, HW).astype(MXU)
    v_one = jnp.tile(jnp.concatenate([jnp.zeros((1, MLA_V), f32), jnp.ones((1, LANE - MLA_V), f32)], axis=1), (1, H))
    gq = jnp.pad(q_norm, (0, 256 - MLA_Q_RANK)).reshape(1, 256)
    gkv = kv_norm.reshape(1, MLA_KV_RANK)
    inv = ROPE_THETA ** (-jnp.arange(0, MLA_ROPE, 2, dtype=f32) / MLA_ROPE)
    ang = jnp.arange(S, dtype=f32)[:, None] * inv[None, :]
    cos2 = jnp.concatenate([jnp.cos(ang), jnp.cos(ang)], axis=1)
    sin2 = jnp.concatenate([jnp.sin(ang), jnp.sin(ang)], axis=1)
    tail = LANE - dq
    cos_k = jnp.concatenate([jnp.zeros((S, MLA_NOPE), f32), cos2, jnp.zeros((S, tail), f32)], axis=1)
    sin_k = jnp.concatenate([jnp.zeros((S, MLA_NOPE), f32), sin2, jnp.zeros((S, tail), f32)], axis=1)
    cos_q = jnp.tile(jnp.concatenate([jnp.ones((S, MLA_NOPE), f32), cos2, jnp.zeros((S, tail), f32)], axis=1), (1, H))
    sin_q = jnp.tile(sin_k, (1, H))
    rel = np.arange(tq)[:, None] - np.arange(tq)[None, :]
    causal = jnp.asarray(np.where(rel >= 0, 0.0, BIG_NEG).astype(np.float32))

    c_cq, c_ckv, c_kr, c_krs = OFF["dcq"] // 256, OFF["dckv"] // LANE, OFF["dkr"] // LANE, OFF["dkrs"] // LANE
    full = lambda shape: pl.BlockSpec(shape, lambda b, i: (0, 0))
    tab = lambda w: pl.BlockSpec((tr, w), lambda b, i: (i, 0))
    out3 = pl.BlockSpec((1, tr, HW), lambda b, i: (b, i, 0))
    q, k, v = pl.pallas_call(
        _mla_prep_kernel,
        grid=(B, S // tr),
        in_specs=[pl.BlockSpec((1, tr, 256), lambda b, i: (b, i, c_cq)),
                  pl.BlockSpec((1, tr, LANE), lambda b, i: (b, i, c_ckv)),
                  pl.BlockSpec((1, tr, LANE), lambda b, i: (b, i, c_kr)),
                  pl.BlockSpec((1, tr, LANE), lambda b, i: (b, i, c_krs)),
                  full((1, 256)), full((1, MLA_KV_RANK)), full((256, HW)), full((256, HW)),
                  full((MLA_KV_RANK, HW)), full((MLA_KV_RANK, HW)),
                  tab(HW), tab(HW), tab(LANE), tab(LANE), full((1, HW))],
        out_specs=[out3, out3, out3],
        out_shape=[jax.ShapeDtypeStruct((B, S, HW), MXU)] * 3,
        compiler_params=_params(2),
        name="mla_prep",
    )(h, h, h, h, gq, gkv, pad_q(wq3), pad_q(wq_rot), wk, wv, cos_q, sin_q, cos_k, sin_k, v_one)
    pair = lambda rows: pl.BlockSpec((1, rows, 2 * LANE), lambda b, p, i: (b, i if rows == tq else 0, p))
    return pl.pallas_call(
        functools.partial(_mla_attn_kernel, tq=tq),
        grid=(B, H // 2, S // tq),
        in_specs=[pair(tq), pair(S), pair(S), pl.BlockSpec((tq, tq), lambda b, p, i: (0, 0))],
        out_specs=pl.BlockSpec((1, tq, LANE), lambda b, p, i: (b, i, p)),
        out_shape=jax.ShapeDtypeStruct((B, S, H * MLA_V), f32),
        compiler_params=_params(3),
        name="mla_attention",
    )(q, k, v, causal)


def _ln(z, g, b):
    mu = jnp.mean(z, axis=-1, keepdims=True)
    zc = z - mu
    var = jnp.mean(zc * zc, axis=-1, keepdims=True)
    return zc * lax.rsqrt(var + LN_EPS) * g + b


def _merge_kernel(mg_ref, oa_ref, ob_ref, oc_ref, od_ref, wb_ref, wo_ref, x_ref, g_ref, b_ref, o_ref):
    acc = None
    for n, br in enumerate((oa_ref, ob_ref, oc_ref, od_ref)):
        up = _dot(br[...].astype(MXU), wb_ref[n])
        term = jax.nn.sigmoid(mg_ref[:, n * D_MODEL:(n + 1) * D_MODEL]) * up
        acc = term if acc is None else acc + term
    y = _dot(acc.astype(MXU), wo_ref[...])
    o_ref[...] = _ln(DN_ALPHA * x_ref[...] + y, g_ref[...], b_ref[...])


def merge_ln(h2, branches, w_branch, w_out, x2, g, b, tm=256):
    N, D = x2.shape
    assert OFF["mg"] == 0 and N % tm == 0
    row = lambda w: pl.BlockSpec((tm, w), lambda i: (i, 0))
    return pl.pallas_call(
        _merge_kernel,
        grid=(N // tm,),
        in_specs=[row(N_MIXERS * D)] + [row(MIX_W)] * N_MIXERS
        + [pl.BlockSpec((N_MIXERS, MIX_W, D), lambda i: (0, 0, 0)), pl.BlockSpec((D, D), lambda i: (0, 0)),
           row(D), pl.BlockSpec((1, D), lambda i: (0, 0)), pl.BlockSpec((1, D), lambda i: (0, 0))],
        out_specs=row(D),
        out_shape=jax.ShapeDtypeStruct((N, D), f32),
        compiler_params=_params(1),
        name="merge_ln",
    )(h2, *branches, w_branch.astype(MXU), w_out.astype(MXU), x2, g.reshape(1, D), b.reshape(1, D))


ROW_TILE = 8


def _to_token_tiles(ref, val):
    rows = val.shape[0]
    for j in range(ROW_TILE):
        ref[pl.ds(j, rows, stride=ROW_TILE), :] = val[:, j * LANE:(j + 1) * LANE]


def _from_token_tiles(ref, rows, first=0, stride=ROW_TILE):
    return jnp.concatenate([ref[pl.ds(first + j, rows, stride=stride), :] for j in range(ROW_TILE)], axis=1)


def _xattn_kernel(x_ref, wq_ref, k_ref, v_ref, wo_ref, g_ref, b_ref, o_ref, o8_ref):
    x = x_ref[0]
    q = _dot(x.astype(MXU), wq_ref[...]).astype(MXU)
    k = k_ref[0]
    v = v_ref[0]
    heads = []
    for hd in range(X_HEADS):
        sl = slice(hd * X_HEAD_DIM, (hd + 1) * X_HEAD_DIM)
        s = _dot_nt(q[:, sl], k[:, sl]) * (X_HEAD_DIM ** -0.5)
        e = jnp.exp(s - jnp.max(s, axis=1, keepdims=True))
        p = e / jnp.sum(e, axis=1, keepdims=True)
        heads.append(_dot(p.astype(MXU), v[:, sl]).astype(MXU))
    y = _dot(jnp.concatenate(heads, axis=1), wo_ref[...])
    out = _ln(DN_ALPHA * x + y, g_ref[...], b_ref[...])
    o_ref[0] = out
    _to_token_tiles(o8_ref.at[0], out)


def cross_attention_ln(x, mem, wq, wkv, wo, g, b, tq=256):
    B, S, D = x.shape
    assert D == ROW_TILE * LANE
    M = mem.shape[1]
    F = X_HEADS * X_HEAD_DIM
    kv = matmul(mem.reshape(B * M, D), wkv.astype(MXU), 512, 2 * F, out_dtype=MXU).reshape(B, M, 2 * F)
    full = lambda shape: pl.BlockSpec(shape, lambda bi, i: (0,) * len(shape))
    return pl.pallas_call(
        _xattn_kernel,
        grid=(B, S // tq),
        in_specs=[pl.BlockSpec((1, tq, D), lambda bi, i: (bi, i, 0)), full((D, F)),
                  pl.BlockSpec((1, M, F), lambda bi, i: (bi, 0, 0)), pl.BlockSpec((1, M, F), lambda bi, i: (bi, 0, 1)),
                  full((F, D)), full((1, D)), full((1, D))],
        out_specs=[pl.BlockSpec((1, tq, D), lambda bi, i: (bi, i, 0)),
                   pl.BlockSpec((1, tq * ROW_TILE, LANE), lambda bi, i: (bi, i, 0))],
        out_shape=[jax.ShapeDtypeStruct((B, S, D), f32), jax.ShapeDtypeStruct((B, S * ROW_TILE, LANE), f32)],
        compiler_params=_params(2),
        name="cross_attention_ln",
    )(x, wq.astype(MXU), kv, kv, wo.astype(MXU), g.reshape(1, D), b.reshape(1, D))


_R_E0, _R_E1, _R_W0, _R_W1, _R_RANK0, _R_RANK1 = range(6)
_GRP_LANE0 = N_EXPERTS


def _router_kernel(x_ref, w_ref, b_ref, tri_ref, r_ref, cnt_ref):
    i = pl.program_id(0)
    tm = x_ref.shape[0]
    logits = _dot_nt(w_ref[...], x_ref[...].astype(MXU)) + b_ref[...]
    row = lax.broadcasted_iota(jnp.int32, (LANE, tm), 0)
    row_f = row.astype(f32)
    big = float(LANE)

    def cmax(t):
        return jnp.max(t, axis=0, keepdims=True)

    def first_row(cond):
        return jnp.min(jnp.where(cond, row_f, big), axis=0, keepdims=True)

    def softmax_on(mask):
        lm = jnp.where(mask, logits, NEG)
        e = jnp.where(mask, jnp.exp(lm - cmax(lm)), 0.0)
        return e / jnp.sum(e, axis=0, keepdims=True)

    is_g = (row >= _GRP_LANE0) & (row < _GRP_LANE0 + N_GROUPS)
    p_grp = softmax_on(is_g)
    p_g = cmax(p_grp)
    grp = first_row(is_g & (p_grp == p_g)) - float(_GRP_LANE0)
    in_grp = (row < N_EXPERTS) & (jnp.right_shift(row, 3).astype(f32) == grp)
    p_e = softmax_on(in_grp)
    p1 = cmax(jnp.where(in_grp, p_e, -1.0))
    e1 = first_row(in_grp & (p_e == p1))
    rest = in_grp & (row_f != e1)
    p2 = cmax(jnp.where(rest, p_e, -1.0))
    e2 = first_row(rest & (p_e == p2))
    w1 = p_g * p1 / (p1 + p2)
    w2 = p_g * p2 / (p1 + p2)

    @pl.when(i == 0)
    def _():
        cnt_ref[...] = jnp.zeros_like(cnt_ref)

    oh1 = row_f == e1
    oh2 = row_f == e2
    both = (oh1 | oh2).astype(MXU)
    before = _dot(both, tri_ref[...]) + cnt_ref[:, 0:1]
    rank1 = jnp.sum(jnp.where(oh1, before, 0.0), axis=0, keepdims=True)
    rank2 = jnp.sum(jnp.where(oh2, before, 0.0), axis=0, keepdims=True)
    cnt_ref[...] = cnt_ref[...] + jnp.sum(both.astype(f32), axis=1, keepdims=True)

    out = jnp.zeros((LANE, tm), f32)
    for slot, val in ((_R_E0, e1), (_R_E1, e2), (_R_W0, w1), (_R_W1, w2), (_R_RANK0, rank1), (_R_RANK1, rank2)):
        out = jnp.where(row == slot, val, out)
    r_ref[...] = out.T


def moe_router(x2, rg_w, rg_b, re_w, re_b, tm=256):
    N, D = x2.shape
    assert EXPERTS_PER_GROUP == 8 and N_EXPERTS + N_GROUPS <= LANE
    w = jnp.pad(jnp.concatenate([re_w, rg_w], axis=1), ((0, 0), (0, LANE - N_EXPERTS - N_GROUPS))).T.astype(MXU)
    b = jnp.pad(jnp.concatenate([re_b, rg_b]), (0, LANE - N_EXPERTS - N_GROUPS)).reshape(LANE, 1)
    tri = jnp.asarray(np.triu(np.ones((tm, tm), np.float32), 1), MXU)
    return pl.pallas_call(
        _router_kernel,
        grid=(N // tm,),
        in_specs=[pl.BlockSpec((tm, D), lambda i: (i, 0)), pl.BlockSpec((LANE, D), lambda i: (0, 0)),
                  pl.BlockSpec((LANE, 1), lambda i: (0, 0)), pl.BlockSpec((tm, tm), lambda i: (0, 0))],
        out_specs=[pl.BlockSpec((tm, LANE), lambda i: (i, 0)), pl.BlockSpec((LANE, LANE), lambda i: (0, 0))],
        out_shape=[jax.ShapeDtypeStruct((N, LANE), f32), jax.ShapeDtypeStruct((LANE, LANE), f32)],
        compiler_params=_params(1),
        name="moe_router",
    )(x2, w, b, tri)


def _ffn_kernel(ce_ref, first_ref, slot_ref, next_ref, nu_ref, x_ref, wgu_hbm, wd_hbm, o_ref,
                wgu_f32, wd_f32, wgu_b, wd_b, sems, *, layer):
    c = pl.program_id(0)
    used = c < nu_ref[0]

    def fetch(e, slot):
        return (pltpu.make_async_copy(wgu_hbm.at[layer, e], wgu_f32.at[slot], sems.at[slot, 0]),
                pltpu.make_async_copy(wd_hbm.at[layer, e], wd_f32.at[slot], sems.at[slot, 1]))

    @pl.when(used & (c == 0))
    def _():
        for cp in fetch(ce_ref[0], 0):
            cp.start()

    @pl.when(used & (first_ref[c] == 1))
    def _():
        slot = slot_ref[c]
        for cp in fetch(ce_ref[c], slot):
            cp.wait()

        @pl.when(next_ref[c] >= 0)
        def _():
            for cp in fetch(next_ref[c], 1 - slot):
                cp.start()

        wgu_b[...] = wgu_f32[slot].astype(MXU)
        wd_b[...] = wd_f32[slot].astype(MXU)

    @pl.when(used)
    def _():
        x = _from_token_tiles(x_ref, EXPERT_CHUNK)
        gu = _dot(x.astype(MXU), wgu_b[...])
        hid = jax.nn.silu(gu[:, :D_EXPERT]) * gu[:, D_EXPERT:]
        _to_token_tiles(o_ref, _dot(hid.astype(MXU), wd_b[...]))

    @pl.when(jnp.logical_not(used))
    def _():
        o_ref[...] = jnp.zeros_like(o_ref)


def expert_ffn(xb8, chunk_e, n_used, w_gu, w_down, layer):
    C = EXPERT_CHUNK
    n_chunks = xb8.shape[0] // (C * ROW_TILE)
    D = ROW_TILE * LANE
    idx = jnp.arange(n_chunks, dtype=jnp.int32)
    used = idx < n_used[0]
    first = used & ((idx == 0) | (chunk_e != jnp.roll(chunk_e, 1)))
    slot = (jnp.cumsum(first.astype(jnp.int32)) - 1) % 2
    none = jnp.int32(N_EXPERTS)
    later = jnp.roll(jnp.where(first, chunk_e, none), -1).at[-1].set(none)
    next_e = lax.cummin(later, axis=0, reverse=True)
    next_e = jnp.where(next_e == none, -1, next_e)
    tile_spec = pl.BlockSpec((C * ROW_TILE, LANE), lambda c, *_: (c, 0))
    any_spec = pl.BlockSpec(memory_space=pl.ANY)
    grid_spec = pltpu.PrefetchScalarGridSpec(
        num_scalar_prefetch=5,
        grid=(n_chunks,),
        in_specs=[tile_spec, any_spec, any_spec],
        out_specs=tile_spec,
        scratch_shapes=[pltpu.VMEM((2, D, 2 * D_EXPERT), w_gu.dtype), pltpu.VMEM((2, D_EXPERT, D), w_down.dtype),
                        pltpu.VMEM((D, 2 * D_EXPERT), MXU), pltpu.VMEM((D_EXPERT, D), MXU),
                        pltpu.SemaphoreType.DMA((2, 2))],
    )
    return pl.pallas_call(
        functools.partial(_ffn_kernel, layer=layer),
        grid_spec=grid_spec,
        out_shape=jax.ShapeDtypeStruct(xb8.shape, f32),
        compiler_params=_params(1),
        name="expert_ffn",
    )(chunk_e, first.astype(jnp.int32), slot.astype(jnp.int32), next_e.astype(jnp.int32), n_used, xb8, w_gu, w_down)


_COPY_WINDOW = 512


def _windowed_copies(n, copy, wait):
    assert _COPY_WINDOW & (_COPY_WINDOW - 1) == 0 and n >= _COPY_WINDOW

    def fill(j, carry):
        copy(j, j).start()
        return carry

    def steady(j, carry):
        s = jnp.bitwise_and(j, _COPY_WINDOW - 1)
        wait(s)
        copy(j, s).start()
        return carry

    def drain(s, carry):
        wait(s)
        return carry

    lax.fori_loop(0, _COPY_WINDOW, fill, 0, unroll=8)
    lax.fori_loop(_COPY_WINDOW, n, steady, 0, unroll=8)
    lax.fori_loop(0, _COPY_WINDOW, drain, 0, unroll=8)


def _dispatch_kernel(dest_ref, x8_ref, init_hbm, xb_hbm, sems, *, tm):
    n = tm * TOPK_IN_GROUP
    base = pl.program_id(0) * n

    def copy(j, s):
        t = pl.multiple_of(jnp.right_shift(j, 1) * ROW_TILE, ROW_TILE)
        d = pl.multiple_of(dest_ref[base + j] * ROW_TILE, ROW_TILE)
        return pltpu.make_async_copy(x8_ref.at[pl.ds(t, ROW_TILE)], xb_hbm.at[pl.ds(d, ROW_TILE)], sems.at[s])

    def wait(s):
        pltpu.make_async_copy(x8_ref.at[pl.ds(0, ROW_TILE)], xb_hbm.at[pl.ds(0, ROW_TILE)], sems.at[s]).wait()

    _windowed_copies(n, copy, wait)


def moe_dispatch(x8, dest, n_slots, tm=256):
    n_tok = x8.shape[0] // ROW_TILE
    assert n_tok % tm == 0 and tm * TOPK_IN_GROUP >= _COPY_WINDOW
    grid_spec = pltpu.PrefetchScalarGridSpec(
        num_scalar_prefetch=1,
        grid=(n_tok // tm,),
        in_specs=[pl.BlockSpec((tm * ROW_TILE, LANE), lambda i, dest: (i, 0)), pl.BlockSpec(memory_space=pl.ANY)],
        out_specs=pl.BlockSpec(memory_space=pl.ANY),
        scratch_shapes=[pltpu.SemaphoreType.DMA((_COPY_WINDOW,))],
    )
    return pl.pallas_call(
        functools.partial(_dispatch_kernel, tm=tm),
        grid_spec=grid_spec,
        out_shape=jax.ShapeDtypeStruct((n_slots * ROW_TILE, LANE), x8.dtype),
        input_output_aliases={2: 0},
        compiler_params=_params(1),
        name="moe_dispatch",
    )(dest, x8, jnp.zeros((n_slots * ROW_TILE, LANE), x8.dtype))


def _combine_kernel(dest_ref, x_ref, r_ref, g_ref, b_ref, yb_hbm, o_ref, buf, sems):
    tm = x_ref.shape[0]
    n = tm * TOPK_IN_GROUP
    base = pl.program_id(0) * n

    def copy(j, s):
        src = pl.multiple_of(dest_ref[base + j] * ROW_TILE, ROW_TILE)
        dst = pl.multiple_of(j * ROW_TILE, ROW_TILE)
        return pltpu.make_async_copy(yb_hbm.at[pl.ds(src, ROW_TILE)], buf.at[pl.ds(dst, ROW_TILE)], sems.at[s])

    def wait(s):
        pltpu.make_async_copy(yb_hbm.at[pl.ds(0, ROW_TILE)], buf.at[pl.ds(0, ROW_TILE)], sems.at[s]).wait()

    _windowed_copies(n, copy, wait)
    r = r_ref[...]
    lane = lax.broadcasted_iota(jnp.int32, r.shape, 1)
    w0 = jnp.sum(jnp.where(lane == _R_W0, r, 0.0), axis=1, keepdims=True)
    w1 = jnp.sum(jnp.where(lane == _R_W1, r, 0.0), axis=1, keepdims=True)
    pair = TOPK_IN_GROUP * ROW_TILE
    y = _from_token_tiles(buf, tm, 0, pair) * w0 + _from_token_tiles(buf, tm, ROW_TILE, pair) * w1
    o_ref[...] = _ln(DN_ALPHA * x_ref[...] + y, g_ref[...], b_ref[...])


def combine_ln(x2, yb8, dest, r, g, b, tm=256):
    N, D = x2.shape
    assert N % tm == 0 and tm * TOPK_IN_GROUP >= _COPY_WINDOW
    row = lambda w: pl.BlockSpec((tm, w), lambda i, dest: (i, 0))
    vec = pl.BlockSpec((1, D), lambda i, dest: (0, 0))
    grid_spec = pltpu.PrefetchScalarGridSpec(
        num_scalar_prefetch=1,
        grid=(N // tm,),
        in_specs=[row(D), row(LANE), vec, vec, pl.BlockSpec(memory_space=pl.ANY)],
        out_specs=row(D),
        scratch_shapes=[pltpu.VMEM((tm * TOPK_IN_GROUP * ROW_TILE, LANE), f32),
                        pltpu.SemaphoreType.DMA((_COPY_WINDOW,))],
    )
    return pl.pallas_call(
        _combine_kernel,
        grid_spec=grid_spec,
        out_shape=jax.ShapeDtypeStruct((N, D), f32),
        compiler_params=_params(1),
        name="moe_combine_ln",
    )(dest, x2, r, g.reshape(1, D), b.reshape(1, D), yb8)


def hier_moe_ln(x2, x8, rg_w, rg_b, re_w, re_b, w_gu, w_down, layer, g, b):
    N, D = x2.shape
    E, C, K = N_EXPERTS, EXPERT_CHUNK, TOPK_IN_GROUP
    A = N * K
    r, cnt = moe_router(x2, rg_w, rg_b, re_w, re_b)
    e = r[:, _R_E0:_R_E1 + 1].astype(jnp.int32)
    rank = r[:, _R_RANK0:_R_RANK1 + 1].astype(jnp.int32)
    counts = cnt[:E, 0].astype(jnp.int32)
    padded = (counts + C - 1) // C * C
    pad_end = jnp.cumsum(padded)
    pad_start = pad_end - padded
    dest = pad_start[e] + rank
    n_chunks = -(-(A + E * (C - 1)) // C)
    P = n_chunks * C
    chunk_start = jnp.arange(n_chunks, dtype=jnp.int32) * C
    chunk_e = jnp.minimum(jnp.sum((pad_end[None, :] <= chunk_start[:, None]).astype(jnp.int32), axis=1), E - 1)
    n_used = (pad_end[-1] // C).reshape(1).astype(jnp.int32)
    dest = dest.reshape(A).astype(jnp.int32)
    xb8 = moe_dispatch(x8, dest, P)
    yb8 = expert_ffn(xb8, chunk_e.astype(jnp.int32), n_used, w_gu, w_down, layer)
    return combine_ln(x2, yb8, dest, r, g, b)


def kernel(x, mem, w_in, nsa_cmp_pos, nsa_cmp_w1, nsa_cmp_w2, rnn_conv_w, rnn_conv_b, rnn_ga_w, rnn_ga_b,
           rnn_gx_w, rnn_gx_b, rnn_lambda, mla_q_norm, mla_kv_norm, mla_w_uq, mla_w_ukv, w_branch, w_out,
           ln1_g, ln1_b, x_wq, x_wkv, x_wo, ln2_g, ln2_b, moe_rg_w, moe_rg_b, moe_re_w, moe_re_b,
           moe_w_gu, moe_w_down, ln3_g, ln3_b):
    B, S, D = x.shape
    N = B * S
    x2 = x.reshape(N, D)
    for l in range(DEPTH):
        h2 = matmul(x2, _cat_w_in(w_in[l]), 512, _IN_TILE)
        h = h2.reshape(B, S, N_CAT)
        o_a = nsa_attention(h, nsa_cmp_pos[l], nsa_cmp_w1[l], nsa_cmp_w2[l])
        o_b = stick_breaking_attention(h)
        o_c = rglru_block(h, rnn_conv_w[l], rnn_conv_b[l], rnn_ga_w[l], rnn_ga_b[l], rnn_gx_w[l], rnn_gx_b[l],
                          rnn_lambda[l])
        o_d = mla_attention(h, mla_q_norm[l], mla_kv_norm[l], mla_w_uq[l], mla_w_ukv[l])
        branches = [o.reshape(N, MIX_W) for o in (o_a, o_b, o_c, o_d)]
        x2 = merge_ln(h2, branches, w_branch[l], w_out[l], x2, ln1_g[l], ln1_b[l])
        x3, x8 = cross_attention_ln(x2.reshape(B, S, D), mem, x_wq[l], x_wkv[l], x_wo[l], ln2_g[l], ln2_b[l])
        x2 = hier_moe_ln(x3.reshape(N, D), x8.reshape(N * ROW_TILE, LANE), moe_rg_w[l], moe_rg_b[l], moe_re_w[l],
                         moe_re_b[l], moe_w_gu, moe_w_down, l, ln3_g[l], ln3_b[l])
    return x2.reshape(B, S, D)
```

```python
import functools

import numpy as np
import jax
import jax.numpy as jnp
from jax import lax
from jax.experimental import pallas as pl
from jax.experimental.pallas import tpu as pltpu

D_MODEL = 1024
DEPTH = 4
HEAD_DIM = 64
N_MIXERS = 4
MIX_W = 256
NSA_HEADS = 4
NSA_KV_HEADS = 2
NSA_GROUP = NSA_HEADS // NSA_KV_HEADS
CMP_LEN = 32
CMP_STRIDE = 16
CMP_HID = 256
SEL_BLOCK = 64
SEL_TOPK = 8
WINDOW = 512
FORCE_SCORE = 1e4
SB_HEADS = 4
RNN_W = 256
CONV_W = 4
LRU_C = 8.0
MLA_HEADS = 4
MLA_Q_RANK = 192
MLA_KV_RANK = 128
MLA_NOPE = 64
MLA_ROPE = 32
MLA_V = 64
ROPE_THETA = 10000.0
X_HEADS = 4
X_HEAD_DIM = 128
N_GROUPS = 4
EXPERTS_PER_GROUP = 8
N_EXPERTS = N_GROUPS * EXPERTS_PER_GROUP
TOPK_IN_GROUP = 2
D_EXPERT = 512
EXPERT_CHUNK = 256
DN_ALPHA = (2.0 * DEPTH) ** 0.25
LN_EPS = 1e-5
RMS_EPS = 1e-6

IN_SPLITS = ((NSA_HEADS * HEAD_DIM,) + (NSA_KV_HEADS * HEAD_DIM,) * 6 + (NSA_HEADS * 3,)
             + (SB_HEADS * HEAD_DIM,) * 3
             + (RNN_W, RNN_W)
             + (MLA_Q_RANK, MLA_KV_RANK, MLA_ROPE)
             + (N_MIXERS * D_MODEL,))
IN_OFFSETS = tuple(int(o) for o in np.concatenate([[0], np.cumsum(IN_SPLITS)[:-1]]))

LANE = 128
VMEM_LIMIT = 48 * 1024 * 1024
NEG = -1e30
BIG_NEG = -2.0 ** 100

f32 = jnp.float32
MXU = jnp.bfloat16

_CQ_PAD = 2 * LANE
_W_Q, _W_KV, _W_KVT = NSA_HEADS * HEAD_DIM, NSA_KV_HEADS * HEAD_DIM, NSA_KV_HEADS * LANE
_SECTIONS = (("mg", N_MIXERS * D_MODEL), ("aq", _W_Q), ("ks", _W_KVT), ("vs", _W_KVT), ("kw", _W_KVT),
             ("vw", _W_KVT), ("bq", SB_HEADS * HEAD_DIM), ("bk", SB_HEADS * HEAD_DIM), ("bv", SB_HEADS * HEAD_DIM),
             ("cx", RNN_W), ("cg", RNN_W), ("dcq", _CQ_PAD), ("kc", _W_KV), ("vc", _W_KV),
             ("ag", LANE), ("dckv", MLA_KV_RANK), ("dkr", LANE), ("dkrs", LANE))
_IN_TILE = 3840
OFF = {}
_o = 0
for _n, _w in _SECTIONS:
    OFF[_n] = _o
    _o += _w
N_CAT = -(-_o // _IN_TILE) * _IN_TILE

_SEL_ROWS = 32
_AUG_SEL = HEAD_DIM
_AUG_POS = HEAD_DIM + _SEL_ROWS
_POS_SPLIT = 256

_TQ_NSA = 512
_TQ_MLA = 512
_TQ_SB = 256
_TM_ROWS = 256
_TR_MLA_PREP = 512


def _rot_half(t):
    d = t.shape[-1]
    return jnp.concatenate([-t[..., d // 2:], t[..., :d // 2]], axis=-1)


def _cat_w_in(w):
    def sec(i):
        return w[:, IN_OFFSETS[i]:IN_OFFSETS[i] + IN_SPLITS[i]]

    def spread(t):
        z = jnp.zeros_like(t[:, :HEAD_DIM])
        return jnp.concatenate([t[:, :HEAD_DIM], z, t[:, HEAD_DIM:], z], axis=1)

    def padc(t, n):
        return jnp.pad(t, ((0, 0), (0, n - t.shape[1])))

    def rope_slot(t):
        return jnp.pad(t, ((0, 0), (MLA_NOPE, LANE - MLA_NOPE - MLA_ROPE)))

    parts = {"mg": sec(16), "aq": sec(0), "kc": sec(1), "vc": sec(2), "ks": spread(sec(3)), "vs": spread(sec(4)),
             "kw": spread(sec(5)), "vw": spread(sec(6)), "ag": padc(sec(7), LANE), "bq": sec(8), "bk": sec(9),
             "bv": sec(10), "cx": sec(11), "cg": sec(12), "dcq": padc(sec(13), _CQ_PAD), "dckv": sec(14),
             "dkr": rope_slot(sec(15)), "dkrs": rope_slot(_rot_half(sec(15)))}
    cat = jnp.concatenate([parts[n] for n, _ in _SECTIONS], axis=1)
    return padc(cat, N_CAT).astype(MXU)


def _dot(a, b):
    return jnp.dot(a, b, preferred_element_type=f32)


def _dot_nt(a, b):
    return lax.dot_general(a, b, (((1,), (1,)), ((), ())), preferred_element_type=f32)


def _params(n_axes):
    return pltpu.CompilerParams(dimension_semantics=("arbitrary",) * n_axes, vmem_limit_bytes=VMEM_LIMIT)


def _mm_kernel(a_ref, b_ref, o_ref):
    o_ref[...] = _dot(a_ref[...].astype(MXU), b_ref[...]).astype(o_ref.dtype)


def matmul(a, b, tm, tn, out_dtype=f32):
    M, K = a.shape
    _, N = b.shape
    assert M % tm == 0 and N % tn == 0
    return pl.pallas_call(
        _mm_kernel,
        grid=(N // tn, M // tm),
        in_specs=[pl.BlockSpec((tm, K), lambda j, i: (i, 0)),
                  pl.BlockSpec((K, tn), lambda j, i: (0, j))],
        out_specs=pl.BlockSpec((tm, tn), lambda j, i: (i, j)),
        out_shape=jax.ShapeDtypeStruct((M, N), out_dtype),
        compiler_params=_params(2),
        name="matmul",
    )(a, b)


def _cmp_kernel(t_ref, w1_ref, pos_ref, w2_ref, o_ref):
    half = t_ref.shape[3]
    t = t_ref[0, 0].astype(MXU)
    y1 = _dot(t, w1_ref[0, :half, :])
    y2 = _dot(t, w1_ref[0, half:, :])
    pos = jnp.broadcast_to(pos_ref[0], (8, 2 * half)).astype(MXU)
    pc = _dot(pos, w1_ref[0])[0:1]
    nc = y2.shape[0]
    hid = y1 + pltpu.roll(y2, nc - 1, 0) + pc
    o_ref[0, 0] = _dot(jax.nn.gelu(hid).astype(MXU), w2_ref[0])


def nsa_compress(t, cmp_w1, cmp_pos, cmp_w2):
    _, B, NC, F = t.shape
    G = NSA_KV_HEADS
    eye = jnp.eye(G, dtype=f32)
    w1 = jnp.einsum('jldh,gk->jlgdkh', cmp_w1.reshape(2, CMP_LEN, HEAD_DIM, CMP_HID), eye)
    w1 = w1.reshape(2, 2 * F, G * CMP_HID).astype(MXU)
    pos = jnp.broadcast_to(cmp_pos[:, :, None, :], (2, CMP_LEN, G, HEAD_DIM)).reshape(2, 1, 2 * F)
    w2 = jnp.einsum('jhc,gk->jghkc', jnp.concatenate([cmp_w2, cmp_w2], axis=-1), eye)
    w2 = w2.reshape(2, G * CMP_HID, G * LANE).astype(MXU)
    return pl.pallas_call(
        _cmp_kernel,
        grid=(2, B),
        in_specs=[pl.BlockSpec((1, 1, NC, F), lambda j, i: (j, i, 0, 0)),
                  pl.BlockSpec((1, 2 * F, G * CMP_HID), lambda j, i: (j, 0, 0)),
                  pl.BlockSpec((1, 1, 2 * F), lambda j, i: (j, 0, 0)),
                  pl.BlockSpec((1, G * CMP_HID, G * LANE), lambda j, i: (j, 0, 0))],
        out_specs=pl.BlockSpec((1, 1, NC, G * LANE), lambda j, i: (j, i, 0, 0)),
        out_shape=jax.ShapeDtypeStruct((2, B, NC, G * LANE), f32),
        compiler_params=_params(2),
        name="nsa_compress",
    )(t, w1, pos, w2)


def _nsa_kernel(q_ref, kc_ref, vc_ref, ks_ref, vs_ref, kw_ref, vw_ref, gl_ref, cover_ref, aug_ref, cb_ref, wb_ref,
                o_ref, ksa, vsa, kwa, vwa, *, tq, n_cmp, n_sel, n_top):
    tk = tq
    g = pl.program_id(1)
    qi = pl.program_id(2)
    q0 = pl.multiple_of(qi * tq, tq)
    lane = lax.broadcasted_iota(jnp.int32, (tq, LANE), 1)
    lo_half = lane < HEAD_DIM

    @pl.when(qi == 0)
    def _():
        real = lax.broadcasted_iota(jnp.int32, ksa.shape, 1) < HEAD_DIM
        aug = aug_ref[...]
        ones = jnp.ones(ksa.shape, MXU)
        ksa[...] = jnp.where(real, ks_ref[0].astype(MXU), aug)
        kwa[...] = jnp.where(real, kw_ref[0].astype(MXU), aug)
        vsa[...] = jnp.where(real, vs_ref[0].astype(MXU), ones)
        vwa[...] = jnp.where(real, vw_ref[0].astype(MXU), ones)

    q = q_ref[0] * (HEAD_DIM ** -0.5)
    q_heads = (jnp.where(lo_half, q, 0.0), jnp.where(lo_half, pltpu.roll(q, HEAD_DIM, 1), 0.0))
    alibi = [2.0 ** (-8.0 * (h + 1) / NSA_HEADS) for h in range(NSA_HEADS)]
    slopes = [jnp.where(g == 0, alibi[n], alibi[NSA_GROUP + n]) for n in range(NSA_GROUP)]
    pos_cols = [jnp.where(lane == _AUG_POS, slopes[n] * _POS_SPLIT, jnp.where(lane == _AUG_POS + 1, slopes[n], 0.0))
                for n in range(NSA_GROUP)]
    tpos = q0 + lax.broadcasted_iota(jnp.int32, (tq, 1), 0)

    nc = kc_ref.shape[2]
    cidx = lax.broadcasted_iota(jnp.int32, (1, nc), 1)
    dist_c = tpos - (cidx * CMP_STRIDE + (CMP_LEN - 1))
    mask_c = (dist_c >= 0) & (cidx < n_cmp)
    dist_cf = dist_c.astype(f32)
    kc = kc_ref[0, 0].astype(MXU)
    vc = vc_ref[0, 0].astype(MXU)
    o_cmp = []
    imp_t = jnp.zeros((LANE, tq), f32)
    for n in range(NSA_GROUP):
        s = _dot_nt(q_heads[n].astype(MXU), kc) - slopes[n] * dist_cf
        sm = jnp.where(mask_c, s, NEG)
        m = jnp.max(sm, axis=1, keepdims=True)
        p = jnp.where(mask_c, jnp.exp(sm - m), 0.0)
        p = (p / jnp.maximum(jnp.sum(p, axis=1, keepdims=True), 1e-30)).astype(MXU)
        o_cmp.append(_dot(p, vc))
        imp_t = imp_t + _dot_nt(cover_ref[...], p)

    rows = _SEL_ROWS
    imp = imp_t[:rows]
    blk = lax.broadcasted_iota(jnp.int32, (rows, tq), 0)
    blk_f = blk.astype(f32)
    tpos_t = q0 + lax.broadcasted_iota(jnp.int32, (1, tq), 1)
    forced = (blk == 0) | (blk == jnp.right_shift(tpos_t, SEL_BLOCK.bit_length() - 1))
    valid = blk * SEL_BLOCK <= tpos_t
    imp = jnp.where(forced, FORCE_SCORE, jnp.where(valid, imp, -1.0))
    imp = jnp.where(blk < n_sel, imp, NEG)
    sel_t = jnp.zeros((rows, tq), f32)
    for _ in range(n_top):
        m = jnp.max(imp, axis=0, keepdims=True)
        first = jnp.min(jnp.where(imp == m, blk_f, float(LANE)), axis=0, keepdims=True)
        pick = blk_f == first
        sel_t = jnp.where(pick, 1.0, sel_t)
        imp = jnp.where(pick, 2 * NEG, imp)
    sel = jnp.concatenate([sel_t, jnp.zeros((LANE - rows, tq), f32)], axis=0).T
    sel_bias = pltpu.roll(jnp.where(sel > 0.5, 0.0, BIG_NEG), _AUG_SEL, 1)
    sel_cols = jnp.where((lane >= _AUG_SEL) & (lane < _AUG_SEL + rows), sel_bias, 0.0)
    q_sel = jnp.concatenate([q_heads[n] + sel_cols + pos_cols[n] for n in range(NSA_GROUP)], axis=0).astype(MXU)
    q_win = jnp.concatenate([q_heads[n] + pos_cols[n] for n in range(NSA_GROUP)], axis=0).astype(MXU)
    rows_q = NSA_GROUP * tq
    denom_lane = lax.broadcasted_iota(jnp.int32, (rows_q, LANE), 1) == HEAD_DIM

    def update(carry, qs, k, v, bias):
        m, acc = carry
        s = _dot_nt(qs, k)
        if bias is not None:
            s = s + bias
        m_new = jnp.maximum(m, jnp.max(s, axis=1, keepdims=True))
        p = jnp.exp(s - m_new)
        return m_new, jnp.exp(m - m_new) * acc + _dot(p.astype(MXU), v)

    def finish(carry):
        _, acc = carry
        denom = jnp.sum(jnp.where(denom_lane, acc, 0.0), axis=1, keepdims=True)
        o = acc / jnp.maximum(denom, 1e-30)
        return [o[n * tq:(n + 1) * tq] for n in range(NSA_GROUP)]

    init = (jnp.full((rows_q, 1), NEG, f32), jnp.zeros((rows_q, LANE), f32))
    causal = cb_ref[...]

    def sel_body(kt, carry):
        k0 = pl.multiple_of(kt * tk, tk)
        return update(carry, q_sel, ksa[pl.ds(k0, tk), :], vsa[pl.ds(k0, tk), :], None)

    carry = lax.fori_loop(0, qi, sel_body, init)
    o_sel = finish(update(carry, q_sel, ksa[pl.ds(q0, tk), :], vsa[pl.ds(q0, tk), :], causal))

    carry = init
    n_back = WINDOW // tk
    for back in range(n_back, -1, -1):
        k0 = pl.multiple_of(jnp.maximum(qi - back, 0) * tk, tk)
        if back:
            off = jnp.where(qi >= back, 0.0, BIG_NEG)
            bias = wb_ref[...] + off if back == n_back else off
        else:
            bias = causal
        carry = update(carry, q_win, kwa[pl.ds(k0, tk), :], vwa[pl.ds(k0, tk), :], bias)
    o_win = finish(carry)

    sig = jax.nn.sigmoid(gl_ref[0])

    def gate(n, j):
        col = 3 * (NSA_GROUP * g + n) + j
        return jnp.sum(jnp.where(lane == col, sig, 0.0), axis=1, keepdims=True)

    o = [gate(n, 0) * o_cmp[n] + gate(n, 1) * o_sel[n] + gate(n, 2) * o_win[n] for n in range(NSA_GROUP)]
    o_ref[0] = jnp.where(lo_half, o[0], pltpu.roll(o[1], HEAD_DIM, 1))


def nsa_attention(h, cmp_pos, cmp_w1, cmp_w2, tq=_TQ_NSA):
    B, S, _ = h.shape
    G = NSA_KV_HEADS
    NC = S // CMP_STRIDE
    n_cmp = (S - CMP_LEN) // CMP_STRIDE + 1
    n_sel = S // SEL_BLOCK
    n_top = min(SEL_TOPK, n_sel)
    F = CMP_STRIDE * HEAD_DIM
    assert S % tq == 0 and WINDOW % tq == 0 and tq % SEL_BLOCK == 0 and n_sel <= _SEL_ROWS and S <= _POS_SPLIT ** 2

    def chunks(off):
        return h[:, :, off:off + G * HEAD_DIM].reshape(B, NC, CMP_STRIDE * G * HEAD_DIM)

    kvc = nsa_compress(jnp.stack([chunks(OFF["kc"]), chunks(OFF["vc"])]), cmp_w1, cmp_pos, cmp_w2)

    c0 = np.arange(n_cmp)[:, None] * CMP_STRIDE
    j0 = np.arange(n_sel)[None, :] * SEL_BLOCK
    cover = np.clip(np.minimum(c0 + CMP_LEN, j0 + SEL_BLOCK) - np.maximum(c0, j0), 0, None) / CMP_LEN
    cover_t = np.zeros((LANE, NC), np.float32)
    cover_t[:n_sel, :n_cmp] = cover.T
    pos = np.arange(S)
    aug = np.zeros((S, LANE), np.float32)
    aug[pos, _AUG_SEL + pos // SEL_BLOCK] = 1.0
    aug[:, _AUG_POS] = pos // _POS_SPLIT
    aug[:, _AUG_POS + 1] = pos % _POS_SPLIT
    rel = np.arange(tq)[:, None] - np.arange(tq)[None, :]
    rel = np.tile(rel, (NSA_GROUP, 1))
    causal = np.where(rel >= 0, 0.0, BIG_NEG).astype(np.float32)
    win_lo = np.where(rel < 0, 0.0, BIG_NEG).astype(np.float32)

    col = lambda name: OFF[name] // LANE
    full2 = lambda shape: pl.BlockSpec(shape, lambda b, g, i: (0, 0))
    q_spec = pl.BlockSpec((1, tq, LANE), lambda b, g, i: (b, i, col("aq") + g))
    c_specs = [pl.BlockSpec((1, 1, NC, LANE), lambda b, g, i, j=j: (j, b, 0, g)) for j in range(2)]
    kv_specs = [pl.BlockSpec((1, S, LANE), lambda b, g, i, c=col(n): (b, 0, c + g))
                for n in ("ks", "vs", "kw", "vw")]
    gl_spec = pl.BlockSpec((1, tq, LANE), lambda b, g, i: (b, i, col("ag")))
    return pl.pallas_call(
        functools.partial(_nsa_kernel, tq=tq, n_cmp=n_cmp, n_sel=n_sel, n_top=n_top),
        grid=(B, G, S // tq),
        in_specs=[q_spec] + c_specs + kv_specs + [gl_spec, full2((LANE, NC)), full2((S, LANE)),
                                                  full2((NSA_GROUP * tq, tq)), full2((NSA_GROUP * tq, tq))],
        out_specs=pl.BlockSpec((1, tq, LANE), lambda b, g, i: (b, i, g)),
        out_shape=jax.ShapeDtypeStruct((B, S, NSA_HEADS * HEAD_DIM), f32),
        scratch_shapes=[pltpu.VMEM((S, LANE), MXU)] * 4,
        compiler_params=_params(3),
        name="nsa_attention",
    )(h, kvc, kvc, h, h, h, h, h, jnp.asarray(cover_t, MXU), jnp.asarray(aug, MXU), jnp.asarray(causal),
      jnp.asarray(win_lo))


_SB_DEAD = -104.0


def _log_sigmoid(z):
    return jnp.minimum(z, 0.0) - jnp.log1p(jnp.exp(-jnp.abs(z)))


def _sb_kernel(q_ref, k_ref, v_ref, u_ref, o_ref, kb, vb, *, tq):
    tk = tq
    qi = pl.program_id(2)
    q0 = pl.multiple_of(qi * tq, tq)
    lane = lax.broadcasted_iota(jnp.int32, (tq, LANE), 1)
    lo_half = lane < HEAD_DIM

    @pl.when(qi == 0)
    def _():
        kb[...] = k_ref[0].astype(MXU)
        vb[...] = v_ref[0].astype(MXU)

    q = q_ref[0] * (HEAD_DIM ** -0.5)
    q2 = jnp.concatenate([jnp.where(lo_half, q, 0.0), jnp.where(lo_half, 0.0, q)], axis=0).astype(MXU)
    u = u_ref[...]

    def tile(carry, k, v, strict):
        c, acc = carry
        z = _dot_nt(q2, k)
        ls = _log_sigmoid(z)
        log_1m = ls - z
        if strict is not None:
            log_1m = jnp.where(strict, log_1m, 0.0)
        hi = log_1m.astype(MXU)
        lo = (log_1m - hi.astype(f32)).astype(MXU)
        tail = _dot(hi, u) + _dot(lo, u) + c
        a = jnp.exp(ls + tail)
        if strict is not None:
            a = jnp.where(strict, a, 0.0)
        return c + jnp.sum(log_1m, axis=1, keepdims=True), acc + _dot(a.astype(MXU), v)

    row = lax.broadcasted_iota(jnp.int32, (2 * tq, tk), 0)
    rel = jnp.where(row >= tq, row - tq, row) - lax.broadcasted_iota(jnp.int32, (2 * tq, tk), 1)
    zero = (jnp.zeros((2 * tq, 1), f32), jnp.zeros((2 * tq, LANE), f32))
    state = tile(zero, kb[pl.ds(q0, tk), :], vb[pl.ds(q0, tk), :], rel > 0)

    def alive(state):
        return (jnp.max(state[0]) > _SB_DEAD).astype(jnp.int32)

    def cond(loop):
        kt, live, _ = loop
        return (kt >= 0) & (live > 0)

    def body(loop):
        kt, _, state = loop
        k0 = pl.multiple_of(kt * tk, tk)
        state = tile(state, kb[pl.ds(k0, tk), :], vb[pl.ds(k0, tk), :], None)
        return kt - 1, alive(state), state

    _, _, (_, acc) = lax.while_loop(cond, body, (qi - 1, alive(state), state))
    o_ref[0] = jnp.where(lo_half, acc[:tq], acc[tq:])


def stick_breaking_attention(h, tq=_TQ_SB):
    B, S, _ = h.shape
    assert S % tq == 0
    tri = jnp.asarray(np.tril(np.ones((tq, tq), np.float32), -1), MXU)
    cq, ck, cv = (OFF[n] // LANE for n in ("bq", "bk", "bv"))
    return pl.pallas_call(
        functools.partial(_sb_kernel, tq=tq),
        grid=(B, SB_HEADS // 2, S // tq),
        in_specs=[pl.BlockSpec((1, tq, LANE), lambda b, p, i: (b, i, cq + p)),
                  pl.BlockSpec((1, S, LANE), lambda b, p, i: (b, 0, ck + p)),
                  pl.BlockSpec((1, S, LANE), lambda b, p, i: (b, 0, cv + p)),
                  pl.BlockSpec((tq, tq), lambda b, p, i: (0, 0))],
        out_specs=pl.BlockSpec((1, tq, LANE), lambda b, p, i: (b, i, p)),
        out_shape=jax.ShapeDtypeStruct((B, S, SB_HEADS * HEAD_DIM), f32),
        scratch_shapes=[pltpu.VMEM((S, LANE), MXU)] * 2,
        compiler_params=_params(3),
        name="sb_attention",
    )(h, h, h, tri)


def _neg_expm1(y):
    series = -y * (1.0 + y * (1.0 / 2 + y * (1.0 / 6 + y * (1.0 / 24 + y * (1.0 / 120)))))
    return jnp.where(y > -0.1, series, 1.0 - jnp.exp(y))


def _rglru_kernel(x_ref, xg_ref, cw_ref, cb_ref, gaw_ref, gab_ref, gxw_ref, gxb_ref, lam_ref, o_ref):
    x = x_ref[0]
    S = x.shape[0]
    row = lax.broadcasted_iota(jnp.int32, (S, 1), 0)

    def shifted(t, d, fill):
        return jnp.where(row >= d, pltpu.roll(t, d, 0), fill)

    u = cb_ref[...] + x * cw_ref[CONV_W - 1:CONV_W, :]
    for d in range(1, CONV_W):
        u = u + shifted(x, d, 0.0) * cw_ref[CONV_W - 1 - d:CONV_W - d, :]
    ub = u.astype(MXU)
    r = jax.nn.sigmoid(_dot(ub, gaw_ref[...]) + gab_ref[...])
    i = jax.nn.sigmoid(_dot(ub, gxw_ref[...]) + gxb_ref[...])
    lam = lam_ref[...]
    softplus_neg = jnp.maximum(-lam, 0.0) + jnp.log1p(jnp.exp(-jnp.abs(lam)))
    log_a = -LRU_C * r * softplus_neg
    a = jnp.exp(log_a)
    b = jnp.sqrt(_neg_expm1(2.0 * log_a)) * (i * u)
    d = 1
    while d < S:
        b = a * shifted(b, d, 0.0) + b
        a = a * shifted(a, d, 1.0)
        d *= 2
    o_ref[0] = b * jax.nn.gelu(xg_ref[0])


def _block_diag(w):
    n, c, _ = w.shape
    out = jnp.zeros((n * c, n * c), w.dtype)
    for j in range(n):
        out = out.at[j * c:(j + 1) * c, j * c:(j + 1) * c].set(w[j])
    return out


def rglru_block(h, conv_w, conv_b, ga_w, ga_b, gx_w, gx_b, lru_lambda):
    B, S, _ = h.shape
    W = RNN_W
    cx, cg = OFF["cx"] // W, OFF["cg"] // W
    vec = pl.BlockSpec((1, W), lambda b: (0, 0))
    mat = pl.BlockSpec((W, W), lambda b: (0, 0))
    return pl.pallas_call(
        _rglru_kernel,
        grid=(B,),
        in_specs=[pl.BlockSpec((1, S, W), lambda b: (b, 0, cx)), pl.BlockSpec((1, S, W), lambda b: (b, 0, cg)),
                  pl.BlockSpec((CONV_W, W), lambda b: (0, 0)), vec, mat, vec, mat, vec, vec],
        out_specs=pl.BlockSpec((1, S, W), lambda b: (b, 0, 0)),
        out_shape=jax.ShapeDtypeStruct((B, S, W), f32),
        compiler_params=_params(1),
        name="rglru",
    )(h, h, conv_w, conv_b.reshape(1, W), _block_diag(ga_w).astype(MXU), ga_b.reshape(1, W),
      _block_diag(gx_w).astype(MXU), gx_b.reshape(1, W), lru_lambda.reshape(1, W))


def _rms(x, g, width):
    return x * lax.rsqrt(jnp.sum(x * x, axis=-1, keepdims=True) * (1.0 / width) + RMS_EPS) * g


def _mla_prep_kernel(cq_ref, ckv_ref, kr_ref, krs_ref, gq_ref, gkv_ref, wq_ref, wqs_ref, wk_ref, wv_ref,
                     cosq_ref, sinq_ref, cosk_ref, sink_ref, vone_ref, q_ref, k_ref, v_ref):
    cq = _rms(cq_ref[0], gq_ref[...], MLA_Q_RANK).astype(MXU)
    ckv = _rms(ckv_ref[0], gkv_ref[...], MLA_KV_RANK).astype(MXU)
    scale = (MLA_NOPE + MLA_ROPE) ** -0.5
    q = _dot(cq, wq_ref[...]) * cosq_ref[...] + _dot(cq, wqs_ref[...]) * sinq_ref[...]
    q_ref[0] = (q * scale).astype(q_ref.dtype)
    k_rope = kr_ref[0] * cosk_ref[...] + krs_ref[0] * sink_ref[...]
    k = _dot(ckv, wk_ref[...])
    k_ref[0] = (k + jnp.concatenate([k_rope] * MLA_HEADS, axis=1)).astype(k_ref.dtype)
    v_ref[0] = (_dot(ckv, wv_ref[...]) + vone_ref[...]).astype(v_ref.dtype)


def _mla_attn_kernel(q_ref, k_ref, v_ref, cb_ref, o_ref, *, tq):
    tk = tq
    qi = pl.program_id(2)
    q0 = pl.multiple_of(qi * tq, tq)
    q = q_ref[0]
    lane = lax.broadcasted_iota(jnp.int32, (tq, LANE), 1)

    def update(carry, n, k, v, bias):
        m, acc = carry
        sl = slice(n * LANE, (n + 1) * LANE)
        s = _dot_nt(q[:, sl], k[:, sl])
        if bias is not None:
            s = s + bias
        m_new = jnp.maximum(m, jnp.max(s, axis=1, keepdims=True))
        p = jnp.exp(s - m_new)
        return m_new, jnp.exp(m - m_new) * acc + _dot(p.astype(MXU), v[:, sl])

    def body(kt, carry):
        k0 = pl.multiple_of(kt * tk, tk)
        k = k_ref[0, pl.ds(k0, tk), :]
        v = v_ref[0, pl.ds(k0, tk), :]
        return tuple(update(carry[n], n, k, v, None) for n in range(2))

    init = tuple((jnp.full((tq, 1), NEG, f32), jnp.zeros((tq, LANE), f32)) for _ in range(2))
    carry = lax.fori_loop(0, qi, body, init)
    k = k_ref[0, pl.ds(q0, tk), :]
    v = v_ref[0, pl.ds(q0, tk), :]
    o = []
    for n in range(2):
        _, acc = update(carry[n], n, k, v, cb_ref[...])
        denom = jnp.sum(jnp.where(lane == MLA_V, acc, 0.0), axis=1, keepdims=True)
        o.append(acc / jnp.maximum(denom, 1e-30))
    o_ref[0] = jnp.where(lane < MLA_V, o[0], pltpu.roll(o[1], MLA_V, 1))


def mla_attention(h, q_norm, kv_norm, w_uq, w_ukv, tr=_TR_MLA_PREP, tq=_TQ_MLA):
    B, S, _ = h.shape
    H = MLA_HEADS
    dq = MLA_NOPE + MLA_ROPE
    HW = H * LANE
    wq3 = w_uq.reshape(MLA_Q_RANK, H, dq)
    wq_rot = jnp.concatenate([jnp.zeros_like(wq3[..., :MLA_NOPE]), _rot_half(wq3[..., MLA_NOPE:])], axis=-1)

    def pad_q(w3):
        w3 = jnp.pad(w3, ((0, _CQ_PAD - MLA_Q_RANK), (0, 0), (0, LANE - dq)))
        return w3.reshape(_CQ_PAD, HW).astype(MXU)

    wkv3 = w_ukv.reshape(MLA_KV_RANK, H, MLA_NOPE + MLA_V)
    wk = jnp.pad(wkv3[..., :MLA_NOPE], ((0, 0), (0, 0), (0, LANE - MLA_NOPE))).reshape(MLA_KV_RANK, HW).astype(MXU)
    wv = jnp.pad(wkv3[..., MLA_NOPE:], ((0, 0), (0, 0), (0, LANE - MLA_V))).reshape(MLA_KV_RANK, HW).astype(MXU)
    v_one = jnp.tile(jnp.concatenate([jnp.zeros((1, MLA_V), f32), jnp.ones((1, LANE - MLA_V), f32)], axis=1), (1, H))
    gq = jnp.pad(q_norm, (0, _CQ_PAD - MLA_Q_RANK)).reshape(1, _CQ_PAD)
    gkv = kv_norm.reshape(1, MLA_KV_RANK)
    inv = ROPE_THETA ** (-jnp.arange(0, MLA_ROPE, 2, dtype=f32) / MLA_ROPE)
    ang = jnp.arange(S, dtype=f32)[:, None] * inv[None, :]
    cos2 = jnp.concatenate([jnp.cos(ang), jnp.cos(ang)], axis=1)
    sin2 = jnp.concatenate([jnp.sin(ang), jnp.sin(ang)], axis=1)
    tail = LANE - dq
    cos_k = jnp.concatenate([jnp.zeros((S, MLA_NOPE), f32), cos2, jnp.zeros((S, tail), f32)], axis=1)
    sin_k = jnp.concatenate([jnp.zeros((S, MLA_NOPE), f32), sin2, jnp.zeros((S, tail), f32)], axis=1)
    cos_q = jnp.tile(jnp.concatenate([jnp.ones((S, MLA_NOPE), f32), cos2, jnp.zeros((S, tail), f32)], axis=1), (1, H))
    sin_q = jnp.tile(sin_k, (1, H))
    rel = np.arange(tq)[:, None] - np.arange(tq)[None, :]
    causal = jnp.asarray(np.where(rel >= 0, 0.0, BIG_NEG).astype(np.float32))

    c_cq, c_ckv, c_kr, c_krs = OFF["dcq"] // _CQ_PAD, OFF["dckv"] // LANE, OFF["dkr"] // LANE, OFF["dkrs"] // LANE
    full = lambda shape: pl.BlockSpec(shape, lambda b, i: (0, 0))
    tab = lambda w: pl.BlockSpec((tr, w), lambda b, i: (i, 0))
    out3 = pl.BlockSpec((1, tr, HW), lambda b, i: (b, i, 0))
    q, k, v = pl.pallas_call(
        _mla_prep_kernel,
        grid=(B, S // tr),
        in_specs=[pl.BlockSpec((1, tr, _CQ_PAD), lambda b, i: (b, i, c_cq)),
                  pl.BlockSpec((1, tr, LANE), lambda b, i: (b, i, c_ckv)),
                  pl.BlockSpec((1, tr, LANE), lambda b, i: (b, i, c_kr)),
                  pl.BlockSpec((1, tr, LANE), lambda b, i: (b, i, c_krs)),
                  full((1, _CQ_PAD)), full((1, MLA_KV_RANK)), full((_CQ_PAD, HW)), full((_CQ_PAD, HW)),
                  full((MLA_KV_RANK, HW)), full((MLA_KV_RANK, HW)),
                  tab(HW), tab(HW), tab(LANE), tab(LANE), full((1, HW))],
        out_specs=[out3, out3, out3],
        out_shape=[jax.ShapeDtypeStruct((B, S, HW), MXU)] * 3,
        compiler_params=_params(2),
        name="mla_prep",
    )(h, h, h, h, gq, gkv, pad_q(wq3), pad_q(wq_rot), wk, wv, cos_q, sin_q, cos_k, sin_k, v_one)
    pair = lambda rows: pl.BlockSpec((1, rows, 2 * LANE), lambda b, p, i: (b, i if rows == tq else 0, p))
    return pl.pallas_call(
        functools.partial(_mla_attn_kernel, tq=tq),
        grid=(B, H // 2, S // tq),
        in_specs=[pair(tq), pair(S), pair(S), pl.BlockSpec((tq, tq), lambda b, p, i: (0, 0))],
        out_specs=pl.BlockSpec((1, tq, LANE), lambda b, p, i: (b, i, p)),
        out_shape=jax.ShapeDtypeStruct((B, S, H * MLA_V), f32),
        compiler_params=_params(3),
        name="mla_attention",
    )(q, k, v, causal)


def _ln(z, g, b):
    mu = jnp.mean(z, axis=-1, keepdims=True)
    zc = z - mu
    var = jnp.mean(zc * zc, axis=-1, keepdims=True)
    return zc * lax.rsqrt(var + LN_EPS) * g + b


def _merge_kernel(mg_ref, oa_ref, ob_ref, oc_ref, od_ref, wb_ref, wo_ref, x_ref, g_ref, b_ref, o_ref):
    acc = None
    for n, br in enumerate((oa_ref, ob_ref, oc_ref, od_ref)):
        up = _dot(br[...].astype(MXU), wb_ref[n])
        term = jax.nn.sigmoid(mg_ref[:, n * D_MODEL:(n + 1) * D_MODEL]) * up
        acc = term if acc is None else acc + term
    y = _dot(acc.astype(MXU), wo_ref[...])
    o_ref[...] = _ln(DN_ALPHA * x_ref[...] + y, g_ref[...], b_ref[...])


def merge_ln(h2, branches, w_branch, w_out, x2, g, b, tm=_TM_ROWS):
    N, D = x2.shape
    assert OFF["mg"] == 0 and N % tm == 0
    row = lambda w: pl.BlockSpec((tm, w), lambda i: (i, 0))
    return pl.pallas_call(
        _merge_kernel,
        grid=(N // tm,),
        in_specs=[row(N_MIXERS * D)] + [row(MIX_W)] * N_MIXERS
        + [pl.BlockSpec((N_MIXERS, MIX_W, D), lambda i: (0, 0, 0)), pl.BlockSpec((D, D), lambda i: (0, 0)),
           row(D), pl.BlockSpec((1, D), lambda i: (0, 0)), pl.BlockSpec((1, D), lambda i: (0, 0))],
        out_specs=row(D),
        out_shape=jax.ShapeDtypeStruct((N, D), f32),
        compiler_params=_params(1),
        name="merge_ln",
    )(h2, *branches, w_branch.astype(MXU), w_out.astype(MXU), x2, g.reshape(1, D), b.reshape(1, D))


ROW_TILE = 8


def _to_token_tiles(ref, val):
    rows = val.shape[0]
    for j in range(ROW_TILE):
        ref[pl.ds(j, rows, stride=ROW_TILE), :] = val[:, j * LANE:(j + 1) * LANE]


def _from_token_tiles(ref, rows, first=0, stride=ROW_TILE):
    return jnp.concatenate([ref[pl.ds(first + j, rows, stride=stride), :] for j in range(ROW_TILE)], axis=1)


def _xattn_kernel(x_ref, wq_ref, k_ref, v_ref, wo_ref, g_ref, b_ref, o_ref, o8_ref):
    x = x_ref[0]
    q = _dot(x.astype(MXU), wq_ref[...]).astype(MXU)
    k = k_ref[0]
    v = v_ref[0]
    heads = []
    for hd in range(X_HEADS):
        sl = slice(hd * X_HEAD_DIM, (hd + 1) * X_HEAD_DIM)
        s = _dot_nt(q[:, sl], k[:, sl]) * (X_HEAD_DIM ** -0.5)
        e = jnp.exp(s - jnp.max(s, axis=1, keepdims=True))
        p = e / jnp.sum(e, axis=1, keepdims=True)
        heads.append(_dot(p.astype(MXU), v[:, sl]).astype(MXU))
    y = _dot(jnp.concatenate(heads, axis=1), wo_ref[...])
    out = _ln(DN_ALPHA * x + y, g_ref[...], b_ref[...])
    o_ref[0] = out
    _to_token_tiles(o8_ref.at[0], out)


def cross_attention_ln(x, mem, wq, wkv, wo, g, b, tq=_TM_ROWS):
    B, S, D = x.shape
    assert D == ROW_TILE * LANE
    M = mem.shape[1]
    F = X_HEADS * X_HEAD_DIM
    kv = matmul(mem.reshape(B * M, D), wkv.astype(MXU), 512, 2 * F, out_dtype=MXU).reshape(B, M, 2 * F)
    full = lambda shape: pl.BlockSpec(shape, lambda bi, i: (0,) * len(shape))
    return pl.pallas_call(
        _xattn_kernel,
        grid=(B, S // tq),
        in_specs=[pl.BlockSpec((1, tq, D), lambda bi, i: (bi, i, 0)), full((D, F)),
                  pl.BlockSpec((1, M, F), lambda bi, i: (bi, 0, 0)), pl.BlockSpec((1, M, F), lambda bi, i: (bi, 0, 1)),
                  full((F, D)), full((1, D)), full((1, D))],
        out_specs=[pl.BlockSpec((1, tq, D), lambda bi, i: (bi, i, 0)),
                   pl.BlockSpec((1, tq * ROW_TILE, LANE), lambda bi, i: (bi, i, 0))],
        out_shape=[jax.ShapeDtypeStruct((B, S, D), f32), jax.ShapeDtypeStruct((B, S * ROW_TILE, LANE), f32)],
        compiler_params=_params(2),
        name="cross_attention_ln",
    )(x, wq.astype(MXU), kv, kv, wo.astype(MXU), g.reshape(1, D), b.reshape(1, D))


_R_E0, _R_E1, _R_W0, _R_W1, _R_RANK0, _R_RANK1 = range(6)
_GRP_LANE0 = N_EXPERTS


def _router_kernel(x_ref, w_ref, b_ref, tri_ref, r_ref, cnt_ref):
    i = pl.program_id(0)
    tm = x_ref.shape[0]
    logits = _dot_nt(w_ref[...], x_ref[...].astype(MXU)) + b_ref[...]
    row = lax.broadcasted_iota(jnp.int32, (LANE, tm), 0)
    row_f = row.astype(f32)
    big = float(LANE)

    def cmax(t):
        return jnp.max(t, axis=0, keepdims=True)

    def first_row(cond):
        return jnp.min(jnp.where(cond, row_f, big), axis=0, keepdims=True)

    def softmax_on(mask):
        lm = jnp.where(mask, logits, NEG)
        e = jnp.where(mask, jnp.exp(lm - cmax(lm)), 0.0)
        return e / jnp.sum(e, axis=0, keepdims=True)

    is_g = (row >= _GRP_LANE0) & (row < _GRP_LANE0 + N_GROUPS)
    p_grp = softmax_on(is_g)
    p_g = cmax(p_grp)
    grp = first_row(is_g & (p_grp == p_g)) - float(_GRP_LANE0)
    grp_of_row = jnp.right_shift(row, EXPERTS_PER_GROUP.bit_length() - 1)
    in_grp = (row < N_EXPERTS) & (grp_of_row.astype(f32) == grp)
    p_e = softmax_on(in_grp)
    p1 = cmax(jnp.where(in_grp, p_e, -1.0))
    e1 = first_row(in_grp & (p_e == p1))
    rest = in_grp & (row_f != e1)
    p2 = cmax(jnp.where(rest, p_e, -1.0))
    e2 = first_row(rest & (p_e == p2))
    w1 = p_g * p1 / (p1 + p2)
    w2 = p_g * p2 / (p1 + p2)

    @pl.when(i == 0)
    def _():
        cnt_ref[...] = jnp.zeros_like(cnt_ref)

    oh1 = row_f == e1
    oh2 = row_f == e2
    both = (oh1 | oh2).astype(MXU)
    before = _dot(both, tri_ref[...]) + cnt_ref[:, 0:1]
    rank1 = jnp.sum(jnp.where(oh1, before, 0.0), axis=0, keepdims=True)
    rank2 = jnp.sum(jnp.where(oh2, before, 0.0), axis=0, keepdims=True)
    cnt_ref[...] = cnt_ref[...] + jnp.sum(both.astype(f32), axis=1, keepdims=True)

    out = jnp.zeros((LANE, tm), f32)
    for slot, val in ((_R_E0, e1), (_R_E1, e2), (_R_W0, w1), (_R_W1, w2), (_R_RANK0, rank1), (_R_RANK1, rank2)):
        out = jnp.where(row == slot, val, out)
    r_ref[...] = out.T


def moe_router(x2, rg_w, rg_b, re_w, re_b, tm=_TM_ROWS):
    N, D = x2.shape
    assert EXPERTS_PER_GROUP & (EXPERTS_PER_GROUP - 1) == 0 and N_EXPERTS + N_GROUPS <= LANE
    w = jnp.pad(jnp.concatenate([re_w, rg_w], axis=1), ((0, 0), (0, LANE - N_EXPERTS - N_GROUPS))).T.astype(MXU)
    b = jnp.pad(jnp.concatenate([re_b, rg_b]), (0, LANE - N_EXPERTS - N_GROUPS)).reshape(LANE, 1)
    tri = jnp.asarray(np.triu(np.ones((tm, tm), np.float32), 1), MXU)
    return pl.pallas_call(
        _router_kernel,
        grid=(N // tm,),
        in_specs=[pl.BlockSpec((tm, D), lambda i: (i, 0)), pl.BlockSpec((LANE, D), lambda i: (0, 0)),
                  pl.BlockSpec((LANE, 1), lambda i: (0, 0)), pl.BlockSpec((tm, tm), lambda i: (0, 0))],
        out_specs=[pl.BlockSpec((tm, LANE), lambda i: (i, 0)), pl.BlockSpec((LANE, LANE), lambda i: (0, 0))],
        out_shape=[jax.ShapeDtypeStruct((N, LANE), f32), jax.ShapeDtypeStruct((LANE, LANE), f32)],
        compiler_params=_params(1),
        name="moe_router",
    )(x2, w, b, tri)


def _ffn_kernel(ce_ref, first_ref, slot_ref, next_ref, nu_ref, x_ref, wgu_hbm, wd_hbm, o_ref,
                wgu_f32, wd_f32, wgu_b, wd_b, sems, *, layer):
    c = pl.program_id(0)
    used = c < nu_ref[0]

    def fetch(e, slot):
        return (pltpu.make_async_copy(wgu_hbm.at[layer, e], wgu_f32.at[slot], sems.at[slot, 0]),
                pltpu.make_async_copy(wd_hbm.at[layer, e], wd_f32.at[slot], sems.at[slot, 1]))

    @pl.when(used & (c == 0))
    def _():
        for cp in fetch(ce_ref[0], 0):
            cp.start()

    @pl.when(used & (first_ref[c] == 1))
    def _():
        slot = slot_ref[c]
        for cp in fetch(ce_ref[c], slot):
            cp.wait()

        @pl.when(next_ref[c] >= 0)
        def _():
            for cp in fetch(next_ref[c], 1 - slot):
                cp.start()

        wgu_b[...] = wgu_f32[slot].astype(MXU)
        wd_b[...] = wd_f32[slot].astype(MXU)

    @pl.when(used)
    def _():
        x = _from_token_tiles(x_ref, EXPERT_CHUNK)
        gu = _dot(x.astype(MXU), wgu_b[...])
        hid = jax.nn.silu(gu[:, :D_EXPERT]) * gu[:, D_EXPERT:]
        _to_token_tiles(o_ref, _dot(hid.astype(MXU), wd_b[...]))

    @pl.when(jnp.logical_not(used))
    def _():
        o_ref[...] = jnp.zeros_like(o_ref)


def expert_ffn(xb8, chunk_e, n_used, w_gu, w_down, layer):
    C = EXPERT_CHUNK
    n_chunks = xb8.shape[0] // (C * ROW_TILE)
    D = ROW_TILE * LANE
    idx = jnp.arange(n_chunks, dtype=jnp.int32)
    used = idx < n_used[0]
    first = used & ((idx == 0) | (chunk_e != jnp.roll(chunk_e, 1)))
    slot = (jnp.cumsum(first.astype(jnp.int32)) - 1) % 2
    none = jnp.int32(N_EXPERTS)
    later = jnp.roll(jnp.where(first, chunk_e, none), -1).at[-1].set(none)
    next_e = lax.cummin(later, axis=0, reverse=True)
    next_e = jnp.where(next_e == none, -1, next_e)
    tile_spec = pl.BlockSpec((C * ROW_TILE, LANE), lambda c, *_: (c, 0))
    any_spec = pl.BlockSpec(memory_space=pl.ANY)
    grid_spec = pltpu.PrefetchScalarGridSpec(
        num_scalar_prefetch=5,
        grid=(n_chunks,),
        in_specs=[tile_spec, any_spec, any_spec],
        out_specs=tile_spec,
        scratch_shapes=[pltpu.VMEM((2, D, 2 * D_EXPERT), w_gu.dtype), pltpu.VMEM((2, D_EXPERT, D), w_down.dtype),
                        pltpu.VMEM((D, 2 * D_EXPERT), MXU), pltpu.VMEM((D_EXPERT, D), MXU),
                        pltpu.SemaphoreType.DMA((2, 2))],
    )
    return pl.pallas_call(
        functools.partial(_ffn_kernel, layer=layer),
        grid_spec=grid_spec,
        out_shape=jax.ShapeDtypeStruct(xb8.shape, f32),
        compiler_params=_params(1),
        name="expert_ffn",
    )(chunk_e, first.astype(jnp.int32), slot.astype(jnp.int32), next_e.astype(jnp.int32), n_used, xb8, w_gu, w_down)


_COPY_WINDOW = 512


def _windowed_copies(n, copy, wait):
    assert _COPY_WINDOW & (_COPY_WINDOW - 1) == 0 and n >= _COPY_WINDOW

    def fill(j, carry):
        copy(j, j).start()
        return carry

    def steady(j, carry):
        s = jnp.bitwise_and(j, _COPY_WINDOW - 1)
        wait(s)
        copy(j, s).start()
        return carry

    def drain(s, carry):
        wait(s)
        return carry

    lax.fori_loop(0, _COPY_WINDOW, fill, 0, unroll=8)
    lax.fori_loop(_COPY_WINDOW, n, steady, 0, unroll=8)
    lax.fori_loop(0, _COPY_WINDOW, drain, 0, unroll=8)


def _dispatch_kernel(dest_ref, x8_ref, init_hbm, xb_hbm, sems, *, tm):
    n = tm * TOPK_IN_GROUP
    base = pl.program_id(0) * n

    def copy(j, s):
        t = pl.multiple_of(jnp.right_shift(j, 1) * ROW_TILE, ROW_TILE)
        d = pl.multiple_of(dest_ref[base + j] * ROW_TILE, ROW_TILE)
        return pltpu.make_async_copy(x8_ref.at[pl.ds(t, ROW_TILE)], xb_hbm.at[pl.ds(d, ROW_TILE)], sems.at[s])

    def wait(s):
        pltpu.make_async_copy(x8_ref.at[pl.ds(0, ROW_TILE)], xb_hbm.at[pl.ds(0, ROW_TILE)], sems.at[s]).wait()

    _windowed_copies(n, copy, wait)


def moe_dispatch(x8, dest, n_slots, tm=_TM_ROWS):
    n_tok = x8.shape[0] // ROW_TILE
    assert n_tok % tm == 0 and tm * TOPK_IN_GROUP >= _COPY_WINDOW
    grid_spec = pltpu.PrefetchScalarGridSpec(
        num_scalar_prefetch=1,
        grid=(n_tok // tm,),
        in_specs=[pl.BlockSpec((tm * ROW_TILE, LANE), lambda i, dest: (i, 0)), pl.BlockSpec(memory_space=pl.ANY)],
        out_specs=pl.BlockSpec(memory_space=pl.ANY),
        scratch_shapes=[pltpu.SemaphoreType.DMA((_COPY_WINDOW,))],
    )
    return pl.pallas_call(
        functools.partial(_dispatch_kernel, tm=tm),
        grid_spec=grid_spec,
        out_shape=jax.ShapeDtypeStruct((n_slots * ROW_TILE, LANE), x8.dtype),
        input_output_aliases={2: 0},
        compiler_params=_params(1),
        name="moe_dispatch",
    )(dest, x8, jnp.zeros((n_slots * ROW_TILE, LANE), x8.dtype))


def _combine_kernel(dest_ref, x_ref, r_ref, g_ref, b_ref, yb_hbm, o_ref, buf, sems):
    tm = x_ref.shape[0]
    n = tm * TOPK_IN_GROUP
    i = pl.program_id(0)

    def issue(tile):
        dst_buf = buf.at[jnp.bitwise_and(tile, 1)]

        def body(j, carry):
            src = pl.multiple_of(dest_ref[tile * n + j] * ROW_TILE, ROW_TILE)
            dst = pl.multiple_of(j * ROW_TILE, ROW_TILE)
            pltpu.make_async_copy(yb_hbm.at[pl.ds(src, ROW_TILE)], dst_buf.at[pl.ds(dst, ROW_TILE)],
                                  sems.at[j]).start()
            return carry

        lax.fori_loop(0, n, body, 0, unroll=8)

    def drain(j, carry):
        pltpu.make_async_copy(yb_hbm.at[pl.ds(0, ROW_TILE)], buf.at[0, pl.ds(0, ROW_TILE)], sems.at[j]).wait()
        return carry

    @pl.when(i == 0)
    def _():
        issue(i)

    lax.fori_loop(0, n, drain, 0, unroll=8)

    @pl.when(i + 1 < pl.num_programs(0))
    def _():
        issue(i + 1)

    cur = buf.at[jnp.bitwise_and(i, 1)]
    r = r_ref[...]
    lane = lax.broadcasted_iota(jnp.int32, r.shape, 1)
    w0 = jnp.sum(jnp.where(lane == _R_W0, r, 0.0), axis=1, keepdims=True)
    w1 = jnp.sum(jnp.where(lane == _R_W1, r, 0.0), axis=1, keepdims=True)
    pair = TOPK_IN_GROUP * ROW_TILE
    y = _from_token_tiles(cur, tm, 0, pair) * w0 + _from_token_tiles(cur, tm, ROW_TILE, pair) * w1
    o_ref[...] = _ln(DN_ALPHA * x_ref[...] + y, g_ref[...], b_ref[...])


def combine_ln(x2, yb8, dest, r, g, b, tm=_TM_ROWS):
    N, D = x2.shape
    assert N % tm == 0
    row = lambda w: pl.BlockSpec((tm, w), lambda i, dest: (i, 0))
    vec = pl.BlockSpec((1, D), lambda i, dest: (0, 0))
    grid_spec = pltpu.PrefetchScalarGridSpec(
        num_scalar_prefetch=1,
        grid=(N // tm,),
        in_specs=[row(D), row(LANE), vec, vec, pl.BlockSpec(memory_space=pl.ANY)],
        out_specs=row(D),
        scratch_shapes=[pltpu.VMEM((2, tm * TOPK_IN_GROUP * ROW_TILE, LANE), f32),
                        pltpu.SemaphoreType.DMA((tm * TOPK_IN_GROUP,))],
    )
    return pl.pallas_call(
        _combine_kernel,
        grid_spec=grid_spec,
        out_shape=jax.ShapeDtypeStruct((N, D), f32),
        compiler_params=_params(1),
        name="moe_combine_ln",
    )(dest, x2, r, g.reshape(1, D), b.reshape(1, D), yb8)


def hier_moe_ln(x2, x8, rg_w, rg_b, re_w, re_b, w_gu, w_down, layer, g, b):
    N, D = x2.shape
    E, C, K = N_EXPERTS, EXPERT_CHUNK, TOPK_IN_GROUP
    A = N * K
    r, cnt = moe_router(x2, rg_w, rg_b, re_w, re_b)
    e = r[:, _R_E0:_R_E1 + 1].astype(jnp.int32)
    rank = r[:, _R_RANK0:_R_RANK1 + 1].astype(jnp.int32)
    counts = cnt[:E, 0].astype(jnp.int32)
    padded = (counts + C - 1) // C * C
    pad_end = jnp.cumsum(padded)
    pad_start = pad_end - padded
    dest = pad_start[e] + rank
    n_chunks = -(-(A + E * (C - 1)) // C)
    P = n_chunks * C
    chunk_start = jnp.arange(n_chunks, dtype=jnp.int32) * C
    chunk_e = jnp.minimum(jnp.sum((pad_end[None, :] <= chunk_start[:, None]).astype(jnp.int32), axis=1), E - 1)
    n_used = (pad_end[-1] // C).reshape(1).astype(jnp.int32)
    dest = dest.reshape(A).astype(jnp.int32)
    xb8 = moe_dispatch(x8, dest, P)
    yb8 = expert_ffn(xb8, chunk_e.astype(jnp.int32), n_used, w_gu, w_down, layer)
    return combine_ln(x2, yb8, dest, r, g, b)


def kernel(x, mem, w_in, nsa_cmp_pos, nsa_cmp_w1, nsa_cmp_w2, rnn_conv_w, rnn_conv_b, rnn_ga_w, rnn_ga_b,
           rnn_gx_w, rnn_gx_b, rnn_lambda, mla_q_norm, mla_kv_norm, mla_w_uq, mla_w_ukv, w_branch, w_out,
           ln1_g, ln1_b, x_wq, x_wkv, x_wo, ln2_g, ln2_b, moe_rg_w, moe_rg_b, moe_re_w, moe_re_b,
           moe_w_gu, moe_w_down, ln3_g, ln3_b):
    B, S, D = x.shape
    N = B * S
    x2 = x.reshape(N, D)
    for l in range(DEPTH):
        h2 = matmul(x2, _cat_w_in(w_in[l]), 512, _IN_TILE)
        h = h2.reshape(B, S, N_CAT)
        o_a = nsa_attention(h, nsa_cmp_pos[l], nsa_cmp_w1[l], nsa_cmp_w2[l])
        o_b = stick_breaking_attention(h)
        o_c = rglru_block(h, rnn_conv_w[l], rnn_conv_b[l], rnn_ga_w[l], rnn_ga_b[l], rnn_gx_w[l], rnn_gx_b[l],
                          rnn_lambda[l])
        o_d = mla_attention(h, mla_q_norm[l], mla_kv_norm[l], mla_w_uq[l], mla_w_ukv[l])
        branches = [o.reshape(N, MIX_W) for o in (o_a, o_b, o_c, o_d)]
        x2 = merge_ln(h2, branches, w_branch[l], w_out[l], x2, ln1_g[l], ln1_b[l])
        x3, x8 = cross_attention_ln(x2.reshape(B, S, D), mem, x_wq[l], x_wkv[l], x_wo[l], ln2_g[l], ln2_b[l])
        x2 = hier_moe_ln(x3.reshape(N, D), x8.reshape(N * ROW_TILE, LANE), moe_rg_w[l], moe_rg_b[l], moe_re_w[l],
                         moe_re_b[l], moe_w_gu, moe_w_down, l, ln3_g[l], ln3_b[l])
    return x2.reshape(B, S, D)
```

```python
import functools

import numpy as np
import jax
import jax.numpy as jnp
from jax import lax
from jax.experimental import pallas as pl
from jax.experimental.pallas import tpu as pltpu

D_MODEL = 1024
DEPTH = 4
HEAD_DIM = 64
N_MIXERS = 4
MIX_W = 256
NSA_HEADS = 4
NSA_KV_HEADS = 2
NSA_GROUP = NSA_HEADS // NSA_KV_HEADS
CMP_LEN = 32
CMP_STRIDE = 16
CMP_HID = 256
SEL_BLOCK = 64
SEL_TOPK = 8
WINDOW = 512
FORCE_SCORE = 1e4
SB_HEADS = 4
RNN_W = 256
CONV_W = 4
LRU_C = 8.0
MLA_HEADS = 4
MLA_Q_RANK = 192
MLA_KV_RANK = 128
MLA_NOPE = 64
MLA_ROPE = 32
MLA_V = 64
ROPE_THETA = 10000.0
X_HEADS = 4
X_HEAD_DIM = 128
N_GROUPS = 4
EXPERTS_PER_GROUP = 8
N_EXPERTS = N_GROUPS * EXPERTS_PER_GROUP
TOPK_IN_GROUP = 2
D_EXPERT = 512
EXPERT_CHUNK = 256
DN_ALPHA = (2.0 * DEPTH) ** 0.25
LN_EPS = 1e-5
RMS_EPS = 1e-6

IN_SPLITS = ((NSA_HEADS * HEAD_DIM,) + (NSA_KV_HEADS * HEAD_DIM,) * 6 + (NSA_HEADS * 3,)
             + (SB_HEADS * HEAD_DIM,) * 3
             + (RNN_W, RNN_W)
             + (MLA_Q_RANK, MLA_KV_RANK, MLA_ROPE)
             + (N_MIXERS * D_MODEL,))
IN_OFFSETS = tuple(int(o) for o in np.concatenate([[0], np.cumsum(IN_SPLITS)[:-1]]))

LANE = 128
VMEM_LIMIT = 48 * 1024 * 1024
NEG = -1e30
BIG_NEG = -2.0 ** 100

f32 = jnp.float32
MXU = jnp.bfloat16

_CQ_PAD = 2 * LANE
_W_Q, _W_KV, _W_KVT = NSA_HEADS * HEAD_DIM, NSA_KV_HEADS * HEAD_DIM, NSA_KV_HEADS * LANE
_SECTIONS = (("mg", N_MIXERS * D_MODEL), ("aq", _W_Q), ("kvs", _W_KVT), ("kvw", _W_KVT),
             ("bq", SB_HEADS * HEAD_DIM), ("bk", SB_HEADS * HEAD_DIM), ("bv", SB_HEADS * HEAD_DIM),
             ("cx", RNN_W), ("cg", RNN_W), ("dcq", _CQ_PAD), ("kc", _W_KV), ("vc", _W_KV),
             ("ag", LANE), ("dckv", MLA_KV_RANK), ("dkr", LANE), ("dkrs", LANE))
_IN_TILE = 3584
OFF = {}
_o = 0
for _n, _w in _SECTIONS:
    OFF[_n] = _o
    _o += _w
N_CAT = -(-_o // _IN_TILE) * _IN_TILE

_SEL_ROWS = 32
_AUG_SEL = HEAD_DIM
_AUG_POS = HEAD_DIM + _SEL_ROWS
_POS_SPLIT = 256

_TQ_NSA = 512
_TQ_MLA = 512
_TQ_SB = 256
_TM_ROWS = 256
_TM_DISPATCH = 1024
_TR_MLA_PREP = 512


def _rot_half(t):
    d = t.shape[-1]
    return jnp.concatenate([-t[..., d // 2:], t[..., :d // 2]], axis=-1)


def _cat_w_in(w):
    def sec(i):
        return w[:, IN_OFFSETS[i]:IN_OFFSETS[i] + IN_SPLITS[i]]

    def pair(k, v):
        return jnp.concatenate([k[:, :HEAD_DIM], v[:, :HEAD_DIM], k[:, HEAD_DIM:], v[:, HEAD_DIM:]], axis=1)

    def padc(t, n):
        return jnp.pad(t, ((0, 0), (0, n - t.shape[1])))

    def rope_slot(t):
        return jnp.pad(t, ((0, 0), (MLA_NOPE, LANE - MLA_NOPE - MLA_ROPE)))

    parts = {"mg": sec(16), "aq": sec(0), "kc": sec(1), "vc": sec(2), "kvs": pair(sec(3), sec(4)),
             "kvw": pair(sec(5), sec(6)), "ag": padc(sec(7), LANE), "bq": sec(8), "bk": sec(9),
             "bv": sec(10), "cx": sec(11), "cg": sec(12), "dcq": padc(sec(13), _CQ_PAD), "dckv": sec(14),
             "dkr": rope_slot(sec(15)), "dkrs": rope_slot(_rot_half(sec(15)))}
    cat = jnp.concatenate([parts[n] for n, _ in _SECTIONS], axis=1)
    return padc(cat, N_CAT).astype(MXU)


def _dot(a, b):
    return jnp.dot(a, b, preferred_element_type=f32)


def _dot_nt(a, b):
    return lax.dot_general(a, b, (((1,), (1,)), ((), ())), preferred_element_type=f32)


def _params(n_axes):
    return pltpu.CompilerParams(dimension_semantics=("arbitrary",) * n_axes, vmem_limit_bytes=VMEM_LIMIT)


def _mm_kernel(a_ref, b_ref, o_ref):
    o_ref[...] = _dot(a_ref[...].astype(MXU), b_ref[...]).astype(o_ref.dtype)


def matmul(a, b, tm, tn, out_dtype=f32):
    M, K = a.shape
    _, N = b.shape
    assert M % tm == 0 and N % tn == 0
    return pl.pallas_call(
        _mm_kernel,
        grid=(N // tn, M // tm),
        in_specs=[pl.BlockSpec((tm, K), lambda j, i: (i, 0)),
                  pl.BlockSpec((K, tn), lambda j, i: (0, j))],
        out_specs=pl.BlockSpec((tm, tn), lambda j, i: (i, j)),
        out_shape=jax.ShapeDtypeStruct((M, N), out_dtype),
        compiler_params=_params(2),
        name="matmul",
    )(a, b)


def _cmp_kernel(t_ref, w1_ref, pos_ref, w2_ref, o_ref):
    half = t_ref.shape[3]
    t = t_ref[0, 0].astype(MXU)
    y1 = _dot(t, w1_ref[0, :half, :])
    y2 = _dot(t, w1_ref[0, half:, :])
    pos = jnp.broadcast_to(pos_ref[0], (8, 2 * half)).astype(MXU)
    pc = _dot(pos, w1_ref[0])[0:1]
    nc = y2.shape[0]
    hid = y1 + pltpu.roll(y2, nc - 1, 0) + pc
    o_ref[0, 0] = _dot(jax.nn.gelu(hid).astype(MXU), w2_ref[0])


def nsa_compress(t, cmp_w1, cmp_pos, cmp_w2):
    _, B, NC, F = t.shape
    G = NSA_KV_HEADS
    eye = jnp.eye(G, dtype=f32)
    w1 = jnp.einsum('jldh,gk->jlgdkh', cmp_w1.reshape(2, CMP_LEN, HEAD_DIM, CMP_HID), eye)
    w1 = w1.reshape(2, 2 * F, G * CMP_HID).astype(MXU)
    pos = jnp.broadcast_to(cmp_pos[:, :, None, :], (2, CMP_LEN, G, HEAD_DIM)).reshape(2, 1, 2 * F)
    w2 = jnp.einsum('jhc,gk->jghkc', jnp.concatenate([cmp_w2, cmp_w2], axis=-1), eye)
    w2 = w2.reshape(2, G * CMP_HID, G * LANE).astype(MXU)
    return pl.pallas_call(
        _cmp_kernel,
        grid=(2, B),
        in_specs=[pl.BlockSpec((1, 1, NC, F), lambda j, i: (j, i, 0, 0)),
                  pl.BlockSpec((1, 2 * F, G * CMP_HID), lambda j, i: (j, 0, 0)),
                  pl.BlockSpec((1, 1, 2 * F), lambda j, i: (j, 0, 0)),
                  pl.BlockSpec((1, G * CMP_HID, G * LANE), lambda j, i: (j, 0, 0))],
        out_specs=pl.BlockSpec((1, 1, NC, G * LANE), lambda j, i: (j, i, 0, 0)),
        out_shape=jax.ShapeDtypeStruct((2, B, NC, G * LANE), f32),
        compiler_params=_params(2),
        name="nsa_compress",
    )(t, w1, pos, w2)


def _nsa_kernel(q_ref, kc_ref, vc_ref, kvs_ref, kvw_ref, gl_ref, cover_ref, aug_ref, cb_ref, wb_ref,
                o_ref, ksa, vsa, kwa, vwa, *, tq, n_cmp, n_sel, n_top):
    tk = tq
    g = pl.program_id(1)
    qi = pl.program_id(2)
    q0 = pl.multiple_of(qi * tq, tq)
    lane = lax.broadcasted_iota(jnp.int32, (tq, LANE), 1)
    lo_half = lane < HEAD_DIM

    @pl.when(qi == 0)
    def _():
        real = lax.broadcasted_iota(jnp.int32, ksa.shape, 1) < HEAD_DIM
        aug = aug_ref[...]
        ones = jnp.ones(ksa.shape, MXU)
        for kv_ref, k_out, v_out in ((kvs_ref, ksa, vsa), (kvw_ref, kwa, vwa)):
            kv = kv_ref[0]
            k_out[...] = jnp.where(real, kv.astype(MXU), aug)
            v_out[...] = jnp.where(real, pltpu.roll(kv, HEAD_DIM, 1).astype(MXU), ones)

    q = q_ref[0] * (HEAD_DIM ** -0.5)
    q_heads = (jnp.where(lo_half, q, 0.0), jnp.where(lo_half, pltpu.roll(q, HEAD_DIM, 1), 0.0))
    alibi = [2.0 ** (-8.0 * (h + 1) / NSA_HEADS) for h in range(NSA_HEADS)]
    slopes = [jnp.where(g == 0, alibi[n], alibi[NSA_GROUP + n]) for n in range(NSA_GROUP)]
    pos_cols = [jnp.where(lane == _AUG_POS, slopes[n] * _POS_SPLIT, jnp.where(lane == _AUG_POS + 1, slopes[n], 0.0))
                for n in range(NSA_GROUP)]
    tpos = q0 + lax.broadcasted_iota(jnp.int32, (tq, 1), 0)

    nc = kc_ref.shape[2]
    cidx = lax.broadcasted_iota(jnp.int32, (1, nc), 1)
    dist_c = tpos - (cidx * CMP_STRIDE + (CMP_LEN - 1))
    mask_c = (dist_c >= 0) & (cidx < n_cmp)
    dist_cf = dist_c.astype(f32)
    kc = kc_ref[0, 0].astype(MXU)
    vc = vc_ref[0, 0].astype(MXU)
    o_cmp = []
    imp_t = jnp.zeros((LANE, tq), f32)
    for n in range(NSA_GROUP):
        s = _dot_nt(q_heads[n].astype(MXU), kc) - slopes[n] * dist_cf
        sm = jnp.where(mask_c, s, NEG)
        m = jnp.max(sm, axis=1, keepdims=True)
        p = jnp.where(mask_c, jnp.exp(sm - m), 0.0)
        p = (p / jnp.maximum(jnp.sum(p, axis=1, keepdims=True), 1e-30)).astype(MXU)
        o_cmp.append(_dot(p, vc))
        imp_t = imp_t + _dot_nt(cover_ref[...], p)

    rows = _SEL_ROWS
    imp = imp_t[:rows]
    blk = lax.broadcasted_iota(jnp.int32, (rows, tq), 0)
    blk_f = blk.astype(f32)
    tpos_t = q0 + lax.broadcasted_iota(jnp.int32, (1, tq), 1)
    forced = (blk == 0) | (blk == jnp.right_shift(tpos_t, SEL_BLOCK.bit_length() - 1))
    valid = blk * SEL_BLOCK <= tpos_t
    imp = jnp.where(forced, FORCE_SCORE, jnp.where(valid, imp, -1.0))
    imp = jnp.where(blk < n_sel, imp, NEG)
    sel_t = jnp.zeros((rows, tq), f32)
    for _ in range(n_top):
        m = jnp.max(imp, axis=0, keepdims=True)
        first = jnp.min(jnp.where(imp == m, blk_f, float(LANE)), axis=0, keepdims=True)
        pick = blk_f == first
        sel_t = jnp.where(pick, 1.0, sel_t)
        imp = jnp.where(pick, 2 * NEG, imp)
    sel = jnp.concatenate([sel_t, jnp.zeros((LANE - rows, tq), f32)], axis=0).T
    sel_bias = pltpu.roll(jnp.where(sel > 0.5, 0.0, BIG_NEG), _AUG_SEL, 1)
    sel_cols = jnp.where((lane >= _AUG_SEL) & (lane < _AUG_SEL + rows), sel_bias, 0.0)
    q_sel = jnp.concatenate([q_heads[n] + sel_cols + pos_cols[n] for n in range(NSA_GROUP)], axis=0).astype(MXU)
    q_win = jnp.concatenate([q_heads[n] + pos_cols[n] for n in range(NSA_GROUP)], axis=0).astype(MXU)
    rows_q = NSA_GROUP * tq
    denom_lane = lax.broadcasted_iota(jnp.int32, (rows_q, LANE), 1) == HEAD_DIM

    def update(carry, qs, k, v, bias):
        m, acc = carry
        s = _dot_nt(qs, k)
        if bias is not None:
            s = s + bias
        m_new = jnp.maximum(m, jnp.max(s, axis=1, keepdims=True))
        p = jnp.exp(s - m_new)
        return m_new, jnp.exp(m - m_new) * acc + _dot(p.astype(MXU), v)

    def finish(carry):
        _, acc = carry
        denom = jnp.sum(jnp.where(denom_lane, acc, 0.0), axis=1, keepdims=True)
        o = acc / jnp.maximum(denom, 1e-30)
        return [o[n * tq:(n + 1) * tq] for n in range(NSA_GROUP)]

    init = (jnp.full((rows_q, 1), NEG, f32), jnp.zeros((rows_q, LANE), f32))
    causal = cb_ref[...]

    def sel_body(kt, carry):
        k0 = pl.multiple_of(kt * tk, tk)
        return update(carry, q_sel, ksa[pl.ds(k0, tk), :], vsa[pl.ds(k0, tk), :], None)

    carry = lax.fori_loop(0, qi, sel_body, init)
    o_sel = finish(update(carry, q_sel, ksa[pl.ds(q0, tk), :], vsa[pl.ds(q0, tk), :], causal))

    carry = init
    n_back = WINDOW // tk
    for back in range(n_back, -1, -1):
        k0 = pl.multiple_of(jnp.maximum(qi - back, 0) * tk, tk)
        if back:
            off = jnp.where(qi >= back, 0.0, BIG_NEG)
            bias = wb_ref[...] + off if back == n_back else off
        else:
            bias = causal
        carry = update(carry, q_win, kwa[pl.ds(k0, tk), :], vwa[pl.ds(k0, tk), :], bias)
    o_win = finish(carry)

    sig = jax.nn.sigmoid(gl_ref[0])

    def gate(n, j):
        col = 3 * (NSA_GROUP * g + n) + j
        return jnp.sum(jnp.where(lane == col, sig, 0.0), axis=1, keepdims=True)

    o = [gate(n, 0) * o_cmp[n] + gate(n, 1) * o_sel[n] + gate(n, 2) * o_win[n] for n in range(NSA_GROUP)]
    o_ref[0] = jnp.where(lo_half, o[0], pltpu.roll(o[1], HEAD_DIM, 1))


def nsa_attention(h, cmp_pos, cmp_w1, cmp_w2, tq=_TQ_NSA):
    B, S, _ = h.shape
    G = NSA_KV_HEADS
    NC = S // CMP_STRIDE
    n_cmp = (S - CMP_LEN) // CMP_STRIDE + 1
    n_sel = S // SEL_BLOCK
    n_top = min(SEL_TOPK, n_sel)
    F = CMP_STRIDE * HEAD_DIM
    assert S % tq == 0 and WINDOW % tq == 0 and tq % SEL_BLOCK == 0 and n_sel <= _SEL_ROWS and S <= _POS_SPLIT ** 2

    def chunks(off):
        return h[:, :, off:off + G * HEAD_DIM].reshape(B, NC, CMP_STRIDE * G * HEAD_DIM)

    kvc = nsa_compress(jnp.stack([chunks(OFF["kc"]), chunks(OFF["vc"])]), cmp_w1, cmp_pos, cmp_w2)

    c0 = np.arange(n_cmp)[:, None] * CMP_STRIDE
    j0 = np.arange(n_sel)[None, :] * SEL_BLOCK
    cover = np.clip(np.minimum(c0 + CMP_LEN, j0 + SEL_BLOCK) - np.maximum(c0, j0), 0, None) / CMP_LEN
    cover_t = np.zeros((LANE, NC), np.float32)
    cover_t[:n_sel, :n_cmp] = cover.T
    pos = np.arange(S)
    aug = np.zeros((S, LANE), np.float32)
    aug[pos, _AUG_SEL + pos // SEL_BLOCK] = 1.0
    aug[:, _AUG_POS] = pos // _POS_SPLIT
    aug[:, _AUG_POS + 1] = pos % _POS_SPLIT
    rel = np.arange(tq)[:, None] - np.arange(tq)[None, :]
    rel = np.tile(rel, (NSA_GROUP, 1))
    causal = np.where(rel >= 0, 0.0, BIG_NEG).astype(np.float32)
    win_lo = np.where(rel < 0, 0.0, BIG_NEG).astype(np.float32)

    col = lambda name: OFF[name] // LANE
    full2 = lambda shape: pl.BlockSpec(shape, lambda b, g, i: (0, 0))
    q_spec = pl.BlockSpec((1, tq, LANE), lambda b, g, i: (b, i, col("aq") + g))
    c_specs = [pl.BlockSpec((1, 1, NC, LANE), lambda b, g, i, j=j: (j, b, 0, g)) for j in range(2)]
    kv_specs = [pl.BlockSpec((1, S, LANE), lambda b, g, i, c=col(n): (b, 0, c + g))
                for n in ("kvs", "kvw")]
    gl_spec = pl.BlockSpec((1, tq, LANE), lambda b, g, i: (b, i, col("ag")))
    return pl.pallas_call(
        functools.partial(_nsa_kernel, tq=tq, n_cmp=n_cmp, n_sel=n_sel, n_top=n_top),
        grid=(B, G, S // tq),
        in_specs=[q_spec] + c_specs + kv_specs + [gl_spec, full2((LANE, NC)), full2((S, LANE)),
                                                  full2((NSA_GROUP * tq, tq)), full2((NSA_GROUP * tq, tq))],
        out_specs=pl.BlockSpec((1, tq, LANE), lambda b, g, i: (b, i, g)),
        out_shape=jax.ShapeDtypeStruct((B, S, NSA_HEADS * HEAD_DIM), f32),
        scratch_shapes=[pltpu.VMEM((S, LANE), MXU)] * 4,
        compiler_params=_params(3),
        name="nsa_attention",
    )(h, kvc, kvc, h, h, h, jnp.asarray(cover_t, MXU), jnp.asarray(aug, MXU), jnp.asarray(causal),
      jnp.asarray(win_lo))


_SB_DEAD = -104.0


def _log_sigmoid(z):
    return jnp.minimum(z, 0.0) - jnp.log1p(jnp.exp(-jnp.abs(z)))


def _sb_kernel(q_ref, k_ref, v_ref, u_ref, o_ref, kb, vb, *, tq):
    tk = tq
    qi = pl.program_id(2)
    q0 = pl.multiple_of(qi * tq, tq)
    lane = lax.broadcasted_iota(jnp.int32, (tq, LANE), 1)
    lo_half = lane < HEAD_DIM

    @pl.when(qi == 0)
    def _():
        kb[...] = k_ref[0].astype(MXU)
        vb[...] = v_ref[0].astype(MXU)

    q = q_ref[0] * (HEAD_DIM ** -0.5)
    q2 = jnp.concatenate([jnp.where(lo_half, q, 0.0), jnp.where(lo_half, 0.0, q)], axis=0).astype(MXU)
    u = u_ref[...]

    def tile(carry, k, v, strict):
        c, acc = carry
        z = _dot_nt(q2, k)
        ls = _log_sigmoid(z)
        log_1m = ls - z
        if strict is not None:
            log_1m = jnp.where(strict, log_1m, 0.0)
        hi = log_1m.astype(MXU)
        lo = (log_1m - hi.astype(f32)).astype(MXU)
        tail = _dot(hi, u) + _dot(lo, u) + c
        a = jnp.exp(ls + tail)
        if strict is not None:
            a = jnp.where(strict, a, 0.0)
        return c + jnp.sum(log_1m, axis=1, keepdims=True), acc + _dot(a.astype(MXU), v)

    row = lax.broadcasted_iota(jnp.int32, (2 * tq, tk), 0)
    rel = jnp.where(row >= tq, row - tq, row) - lax.broadcasted_iota(jnp.int32, (2 * tq, tk), 1)
    zero = (jnp.zeros((2 * tq, 1), f32), jnp.zeros((2 * tq, LANE), f32))
    state = tile(zero, kb[pl.ds(q0, tk), :], vb[pl.ds(q0, tk), :], rel > 0)

    def alive(state):
        return (jnp.max(state[0]) > _SB_DEAD).astype(jnp.int32)

    def cond(loop):
        kt, live, _ = loop
        return (kt >= 0) & (live > 0)

    def body(loop):
        kt, _, state = loop
        k0 = pl.multiple_of(kt * tk, tk)
        state = tile(state, kb[pl.ds(k0, tk), :], vb[pl.ds(k0, tk), :], None)
        return kt - 1, alive(state), state

    _, _, (_, acc) = lax.while_loop(cond, body, (qi - 1, alive(state), state))
    o_ref[0] = jnp.where(lo_half, acc[:tq], acc[tq:])


def stick_breaking_attention(h, tq=_TQ_SB):
    B, S, _ = h.shape
    assert S % tq == 0
    tri = jnp.asarray(np.tril(np.ones((tq, tq), np.float32), -1), MXU)
    cq, ck, cv = (OFF[n] // LANE for n in ("bq", "bk", "bv"))
    return pl.pallas_call(
        functools.partial(_sb_kernel, tq=tq),
        grid=(B, SB_HEADS // 2, S // tq),
        in_specs=[pl.BlockSpec((1, tq, LANE), lambda b, p, i: (b, i, cq + p)),
                  pl.BlockSpec((1, S, LANE), lambda b, p, i: (b, 0, ck + p)),
                  pl.BlockSpec((1, S, LANE), lambda b, p, i: (b, 0, cv + p)),
                  pl.BlockSpec((tq, tq), lambda b, p, i: (0, 0))],
        out_specs=pl.BlockSpec((1, tq, LANE), lambda b, p, i: (b, i, p)),
        out_shape=jax.ShapeDtypeStruct((B, S, SB_HEADS * HEAD_DIM), f32),
        scratch_shapes=[pltpu.VMEM((S, LANE), MXU)] * 2,
        compiler_params=_params(3),
        name="sb_attention",
    )(h, h, h, tri)


def _neg_expm1(y):
    series = -y * (1.0 + y * (1.0 / 2 + y * (1.0 / 6 + y * (1.0 / 24 + y * (1.0 / 120)))))
    return jnp.where(y > -0.1, series, 1.0 - jnp.exp(y))


def _rglru_kernel(x_ref, xg_ref, cw_ref, cb_ref, gaw_ref, gab_ref, gxw_ref, gxb_ref, lam_ref, o_ref):
    x = x_ref[0]
    S = x.shape[0]
    row = lax.broadcasted_iota(jnp.int32, (S, 1), 0)

    def shifted(t, d, fill):
        return jnp.where(row >= d, pltpu.roll(t, d, 0), fill)

    u = cb_ref[...] + x * cw_ref[CONV_W - 1:CONV_W, :]
    for d in range(1, CONV_W):
        u = u + shifted(x, d, 0.0) * cw_ref[CONV_W - 1 - d:CONV_W - d, :]
    ub = u.astype(MXU)
    r = jax.nn.sigmoid(_dot(ub, gaw_ref[...]) + gab_ref[...])
    i = jax.nn.sigmoid(_dot(ub, gxw_ref[...]) + gxb_ref[...])
    lam = lam_ref[...]
    softplus_neg = jnp.maximum(-lam, 0.0) + jnp.log1p(jnp.exp(-jnp.abs(lam)))
    log_a = -LRU_C * r * softplus_neg
    a = jnp.exp(log_a)
    b = jnp.sqrt(_neg_expm1(2.0 * log_a)) * (i * u)
    d = 1
    while d < S:
        b = a * shifted(b, d, 0.0) + b
        a = a * shifted(a, d, 1.0)
        d *= 2
    o_ref[0] = b * jax.nn.gelu(xg_ref[0])


def _block_diag(w):
    n, c, _ = w.shape
    out = jnp.zeros((n * c, n * c), w.dtype)
    for j in range(n):
        out = out.at[j * c:(j + 1) * c, j * c:(j + 1) * c].set(w[j])
    return out


def rglru_block(h, conv_w, conv_b, ga_w, ga_b, gx_w, gx_b, lru_lambda):
    B, S, _ = h.shape
    W = RNN_W
    cx, cg = OFF["cx"] // W, OFF["cg"] // W
    vec = pl.BlockSpec((1, W), lambda b: (0, 0))
    mat = pl.BlockSpec((W, W), lambda b: (0, 0))
    return pl.pallas_call(
        _rglru_kernel,
        grid=(B,),
        in_specs=[pl.BlockSpec((1, S, W), lambda b: (b, 0, cx)), pl.BlockSpec((1, S, W), lambda b: (b, 0, cg)),
                  pl.BlockSpec((CONV_W, W), lambda b: (0, 0)), vec, mat, vec, mat, vec, vec],
        out_specs=pl.BlockSpec((1, S, W), lambda b: (b, 0, 0)),
        out_shape=jax.ShapeDtypeStruct((B, S, W), f32),
        compiler_params=_params(1),
        name="rglru",
    )(h, h, conv_w, conv_b.reshape(1, W), _block_diag(ga_w).astype(MXU), ga_b.reshape(1, W),
      _block_diag(gx_w).astype(MXU), gx_b.reshape(1, W), lru_lambda.reshape(1, W))


def _rms(x, g, width):
    return x * lax.rsqrt(jnp.sum(x * x, axis=-1, keepdims=True) * (1.0 / width) + RMS_EPS) * g


def _mla_prep_kernel(cq_ref, ckv_ref, kr_ref, krs_ref, gq_ref, gkv_ref, wq_ref, wqs_ref, wk_ref, wv_ref,
                     cosq_ref, sinq_ref, cosk_ref, sink_ref, vone_ref, q_ref, k_ref, v_ref):
    cq = _rms(cq_ref[0], gq_ref[...], MLA_Q_RANK).astype(MXU)
    ckv = _rms(ckv_ref[0], gkv_ref[...], MLA_KV_RANK).astype(MXU)
    scale = (MLA_NOPE + MLA_ROPE) ** -0.5
    q = _dot(cq, wq_ref[...]) * cosq_ref[...] + _dot(cq, wqs_ref[...]) * sinq_ref[...]
    q_ref[0] = (q * scale).astype(q_ref.dtype)
    k_rope = kr_ref[0] * cosk_ref[...] + krs_ref[0] * sink_ref[...]
    k = _dot(ckv, wk_ref[...])
    k_ref[0] = (k + jnp.concatenate([k_rope] * MLA_HEADS, axis=1)).astype(k_ref.dtype)
    v_ref[0] = (_dot(ckv, wv_ref[...]) + vone_ref[...]).astype(v_ref.dtype)


def _mla_attn_kernel(q_ref, k_ref, v_ref, cb_ref, o_ref, *, tq):
    tk = tq
    qi = pl.program_id(2)
    q0 = pl.multiple_of(qi * tq, tq)
    q = q_ref[0]
    lane = lax.broadcasted_iota(jnp.int32, (tq, LANE), 1)

    def update(carry, n, k, v, bias):
        m, acc = carry
        sl = slice(n * LANE, (n + 1) * LANE)
        s = _dot_nt(q[:, sl], k[:, sl])
        if bias is not None:
            s = s + bias
        m_new = jnp.maximum(m, jnp.max(s, axis=1, keepdims=True))
        p = jnp.exp(s - m_new)
        return m_new, jnp.exp(m - m_new) * acc + _dot(p.astype(MXU), v[:, sl])

    def body(kt, carry):
        k0 = pl.multiple_of(kt * tk, tk)
        k = k_ref[0, pl.ds(k0, tk), :]
        v = v_ref[0, pl.ds(k0, tk), :]
        return tuple(update(carry[n], n, k, v, None) for n in range(2))

    init = tuple((jnp.full((tq, 1), NEG, f32), jnp.zeros((tq, LANE), f32)) for _ in range(2))
    carry = lax.fori_loop(0, qi, body, init)
    k = k_ref[0, pl.ds(q0, tk), :]
    v = v_ref[0, pl.ds(q0, tk), :]
    o = []
    for n in range(2):
        _, acc = update(carry[n], n, k, v, cb_ref[...])
        denom = jnp.sum(jnp.where(lane == MLA_V, acc, 0.0), axis=1, keepdims=True)
        o.append(acc / jnp.maximum(denom, 1e-30))
    o_ref[0] = jnp.where(lane < MLA_V, o[0], pltpu.roll(o[1], MLA_V, 1))


def mla_attention(h, q_norm, kv_norm, w_uq, w_ukv, tr=_TR_MLA_PREP, tq=_TQ_MLA):
    B, S, _ = h.shape
    H = MLA_HEADS
    dq = MLA_NOPE + MLA_ROPE
    HW = H * LANE
    wq3 = w_uq.reshape(MLA_Q_RANK, H, dq)
    wq_rot = jnp.concatenate([jnp.zeros_like(wq3[..., :MLA_NOPE]), _rot_half(wq3[..., MLA_NOPE:])], axis=-1)

    def pad_q(w3):
        w3 = jnp.pad(w3, ((0, _CQ_PAD - MLA_Q_RANK), (0, 0), (0, LANE - dq)))
        return w3.reshape(_CQ_PAD, HW).astype(MXU)

    wkv3 = w_ukv.reshape(MLA_KV_RANK, H, MLA_NOPE + MLA_V)
    wk = jnp.pad(wkv3[..., :MLA_NOPE], ((0, 0), (0, 0), (0, LANE - MLA_NOPE))).reshape(MLA_KV_RANK, HW).astype(MXU)
    wv = jnp.pad(wkv3[..., MLA_NOPE:], ((0, 0), (0, 0), (0, LANE - MLA_V))).reshape(MLA_KV_RANK, HW).astype(MXU)
    v_one = jnp.tile(jnp.concatenate([jnp.zeros((1, MLA_V), f32), jnp.ones((1, LANE - MLA_V), f32)], axis=1), (1, H))
    gq = jnp.pad(q_norm, (0, _CQ_PAD - MLA_Q_RANK)).reshape(1, _CQ_PAD)
    gkv = kv_norm.reshape(1, MLA_KV_RANK)
    inv = ROPE_THETA ** (-jnp.arange(0, MLA_ROPE, 2, dtype=f32) / MLA_ROPE)
    ang = jnp.arange(S, dtype=f32)[:, None] * inv[None, :]
    cos2 = jnp.concatenate([jnp.cos(ang), jnp.cos(ang)], axis=1)
    sin2 = jnp.concatenate([jnp.sin(ang), jnp.sin(ang)], axis=1)
    tail = LANE - dq
    cos_k = jnp.concatenate([jnp.zeros((S, MLA_NOPE), f32), cos2, jnp.zeros((S, tail), f32)], axis=1)
    sin_k = jnp.concatenate([jnp.zeros((S, MLA_NOPE), f32), sin2, jnp.zeros((S, tail), f32)], axis=1)
    cos_q = jnp.tile(jnp.concatenate([jnp.ones((S, MLA_NOPE), f32), cos2, jnp.zeros((S, tail), f32)], axis=1), (1, H))
    sin_q = jnp.tile(sin_k, (1, H))
    rel = np.arange(tq)[:, None] - np.arange(tq)[None, :]
    causal = jnp.asarray(np.where(rel >= 0, 0.0, BIG_NEG).astype(np.float32))

    c_cq, c_ckv, c_kr, c_krs = OFF["dcq"] // _CQ_PAD, OFF["dckv"] // LANE, OFF["dkr"] // LANE, OFF["dkrs"] // LANE
    full = lambda shape: pl.BlockSpec(shape, lambda b, i: (0, 0))
    tab = lambda w: pl.BlockSpec((tr, w), lambda b, i: (i, 0))
    out3 = pl.BlockSpec((1, tr, HW), lambda b, i: (b, i, 0))
    q, k, v = pl.pallas_call(
        _mla_prep_kernel,
        grid=(B, S // tr),
        in_specs=[pl.BlockSpec((1, tr, _CQ_PAD), lambda b, i: (b, i, c_cq)),
                  pl.BlockSpec((1, tr, LANE), lambda b, i: (b, i, c_ckv)),
                  pl.BlockSpec((1, tr, LANE), lambda b, i: (b, i, c_kr)),
                  pl.BlockSpec((1, tr, LANE), lambda b, i: (b, i, c_krs)),
                  full((1, _CQ_PAD)), full((1, MLA_KV_RANK)), full((_CQ_PAD, HW)), full((_CQ_PAD, HW)),
                  full((MLA_KV_RANK, HW)), full((MLA_KV_RANK, HW)),
                  tab(HW), tab(HW), tab(LANE), tab(LANE), full((1, HW))],
        out_specs=[out3, out3, out3],
        out_shape=[jax.ShapeDtypeStruct((B, S, HW), MXU)] * 3,
        compiler_params=_params(2),
        name="mla_prep",
    )(h, h, h, h, gq, gkv, pad_q(wq3), pad_q(wq_rot), wk, wv, cos_q, sin_q, cos_k, sin_k, v_one)
    pair = lambda rows: pl.BlockSpec((1, rows, 2 * LANE), lambda b, p, i: (b, i if rows == tq else 0, p))
    return pl.pallas_call(
        functools.partial(_mla_attn_kernel, tq=tq),
        grid=(B, H // 2, S // tq),
        in_specs=[pair(tq), pair(S), pair(S), pl.BlockSpec((tq, tq), lambda b, p, i: (0, 0))],
        out_specs=pl.BlockSpec((1, tq, LANE), lambda b, p, i: (b, i, p)),
        out_shape=jax.ShapeDtypeStruct((B, S, H * MLA_V), f32),
        compiler_params=_params(3),
        name="mla_attention",
    )(q, k, v, causal)


def _ln(z, g, b):
    mu = jnp.mean(z, axis=-1, keepdims=True)
    zc = z - mu
    var = jnp.mean(zc * zc, axis=-1, keepdims=True)
    return zc * lax.rsqrt(var + LN_EPS) * g + b


def _merge_kernel(mg_ref, oa_ref, ob_ref, oc_ref, od_ref, wb_ref, wo_ref, x_ref, g_ref, b_ref, o_ref):
    acc = None
    for n, br in enumerate((oa_ref, ob_ref, oc_ref, od_ref)):
        up = _dot(br[...].astype(MXU), wb_ref[n])
        term = jax.nn.sigmoid(mg_ref[:, n * D_MODEL:(n + 1) * D_MODEL]) * up
        acc = term if acc is None else acc + term
    y = _dot(acc.astype(MXU), wo_ref[...])
    o_ref[...] = _ln(DN_ALPHA * x_ref[...] + y, g_ref[...], b_ref[...])


def merge_ln(h2, branches, w_branch, w_out, x2, g, b, tm=_TM_ROWS):
    N, D = x2.shape
    assert OFF["mg"] == 0 and N % tm == 0
    row = lambda w: pl.BlockSpec((tm, w), lambda i: (i, 0))
    return pl.pallas_call(
        _merge_kernel,
        grid=(N // tm,),
        in_specs=[row(N_MIXERS * D)] + [row(MIX_W)] * N_MIXERS
        + [pl.BlockSpec((N_MIXERS, MIX_W, D), lambda i: (0, 0, 0)), pl.BlockSpec((D, D), lambda i: (0, 0)),
           row(D), pl.BlockSpec((1, D), lambda i: (0, 0)), pl.BlockSpec((1, D), lambda i: (0, 0))],
        out_specs=row(D),
        out_shape=jax.ShapeDtypeStruct((N, D), f32),
        compiler_params=_params(1),
        name="merge_ln",
    )(h2, *branches, w_branch.astype(MXU), w_out.astype(MXU), x2, g.reshape(1, D), b.reshape(1, D))


ROW_TILE = 8


def _to_token_tiles(ref, val):
    rows = val.shape[0]
    for j in range(ROW_TILE):
        ref[pl.ds(j, rows, stride=ROW_TILE), :] = val[:, j * LANE:(j + 1) * LANE]


def _from_token_tiles(ref, rows, first=0, stride=ROW_TILE):
    return jnp.concatenate([ref[pl.ds(first + j, rows, stride=stride), :] for j in range(ROW_TILE)], axis=1)


def _xattn_kernel(x_ref, wq_ref, k_ref, v_ref, wo_ref, g_ref, b_ref, o_ref, o8_ref):
    x = x_ref[0]
    q = _dot(x.astype(MXU), wq_ref[...]).astype(MXU)
    k = k_ref[0]
    v = v_ref[0]
    heads = []
    for hd in range(X_HEADS):
        sl = slice(hd * X_HEAD_DIM, (hd + 1) * X_HEAD_DIM)
        s = _dot_nt(q[:, sl], k[:, sl]) * (X_HEAD_DIM ** -0.5)
        e = jnp.exp(s - jnp.max(s, axis=1, keepdims=True))
        p = e / jnp.sum(e, axis=1, keepdims=True)
        heads.append(_dot(p.astype(MXU), v[:, sl]).astype(MXU))
    y = _dot(jnp.concatenate(heads, axis=1), wo_ref[...])
    out = _ln(DN_ALPHA * x + y, g_ref[...], b_ref[...])
    o_ref[0] = out
    _to_token_tiles(o8_ref.at[0], out)


def cross_attention_ln(x, mem, wq, wkv, wo, g, b, tq=_TM_ROWS):
    B, S, D = x.shape
    assert D == ROW_TILE * LANE
    M = mem.shape[1]
    F = X_HEADS * X_HEAD_DIM
    kv = matmul(mem.reshape(B * M, D), wkv.astype(MXU), 512, 2 * F, out_dtype=MXU).reshape(B, M, 2 * F)
    full = lambda shape: pl.BlockSpec(shape, lambda bi, i: (0,) * len(shape))
    return pl.pallas_call(
        _xattn_kernel,
        grid=(B, S // tq),
        in_specs=[pl.BlockSpec((1, tq, D), lambda bi, i: (bi, i, 0)), full((D, F)),
                  pl.BlockSpec((1, M, F), lambda bi, i: (bi, 0, 0)), pl.BlockSpec((1, M, F), lambda bi, i: (bi, 0, 1)),
                  full((F, D)), full((1, D)), full((1, D))],
        out_specs=[pl.BlockSpec((1, tq, D), lambda bi, i: (bi, i, 0)),
                   pl.BlockSpec((1, tq * ROW_TILE, LANE), lambda bi, i: (bi, i, 0))],
        out_shape=[jax.ShapeDtypeStruct((B, S, D), f32), jax.ShapeDtypeStruct((B, S * ROW_TILE, LANE), f32)],
        compiler_params=_params(2),
        name="cross_attention_ln",
    )(x, wq.astype(MXU), kv, kv, wo.astype(MXU), g.reshape(1, D), b.reshape(1, D))


_R_E0, _R_E1, _R_W0, _R_W1, _R_RANK0, _R_RANK1 = range(6)
_GRP_LANE0 = N_EXPERTS


def _router_kernel(x_ref, w_ref, b_ref, tri_ref, r_ref, cnt_ref):
    i = pl.program_id(0)
    tm = x_ref.shape[0]
    logits = _dot_nt(w_ref[...], x_ref[...].astype(MXU)) + b_ref[...]
    row = lax.broadcasted_iota(jnp.int32, (LANE, tm), 0)
    row_f = row.astype(f32)
    big = float(LANE)

    def cmax(t):
        return jnp.max(t, axis=0, keepdims=True)

    def first_row(cond):
        return jnp.min(jnp.where(cond, row_f, big), axis=0, keepdims=True)

    def softmax_on(mask):
        lm = jnp.where(mask, logits, NEG)
        e = jnp.where(mask, jnp.exp(lm - cmax(lm)), 0.0)
        return e / jnp.sum(e, axis=0, keepdims=True)

    is_g = (row >= _GRP_LANE0) & (row < _GRP_LANE0 + N_GROUPS)
    p_grp = softmax_on(is_g)
    p_g = cmax(p_grp)
    grp = first_row(is_g & (p_grp == p_g)) - float(_GRP_LANE0)
    grp_of_row = jnp.right_shift(row, EXPERTS_PER_GROUP.bit_length() - 1)
    in_grp = (row < N_EXPERTS) & (grp_of_row.astype(f32) == grp)
    p_e = softmax_on(in_grp)
    p1 = cmax(jnp.where(in_grp, p_e, -1.0))
    e1 = first_row(in_grp & (p_e == p1))
    rest = in_grp & (row_f != e1)
    p2 = cmax(jnp.where(rest, p_e, -1.0))
    e2 = first_row(rest & (p_e == p2))
    w1 = p_g * p1 / (p1 + p2)
    w2 = p_g * p2 / (p1 + p2)

    @pl.when(i == 0)
    def _():
        cnt_ref[...] = jnp.zeros_like(cnt_ref)

    oh1 = row_f == e1
    oh2 = row_f == e2
    both = (oh1 | oh2).astype(MXU)
    before = _dot(both, tri_ref[...]) + cnt_ref[:, 0:1]
    rank1 = jnp.sum(jnp.where(oh1, before, 0.0), axis=0, keepdims=True)
    rank2 = jnp.sum(jnp.where(oh2, before, 0.0), axis=0, keepdims=True)
    cnt_ref[...] = cnt_ref[...] + jnp.sum(both.astype(f32), axis=1, keepdims=True)

    out = jnp.zeros((LANE, tm), f32)
    for slot, val in ((_R_E0, e1), (_R_E1, e2), (_R_W0, w1), (_R_W1, w2), (_R_RANK0, rank1), (_R_RANK1, rank2)):
        out = jnp.where(row == slot, val, out)
    r_ref[...] = out.T


def moe_router(x2, rg_w, rg_b, re_w, re_b, tm=_TM_ROWS):
    N, D = x2.shape
    assert EXPERTS_PER_GROUP & (EXPERTS_PER_GROUP - 1) == 0 and N_EXPERTS + N_GROUPS <= LANE
    w = jnp.pad(jnp.concatenate([re_w, rg_w], axis=1), ((0, 0), (0, LANE - N_EXPERTS - N_GROUPS))).T.astype(MXU)
    b = jnp.pad(jnp.concatenate([re_b, rg_b]), (0, LANE - N_EXPERTS - N_GROUPS)).reshape(LANE, 1)
    tri = jnp.asarray(np.triu(np.ones((tm, tm), np.float32), 1), MXU)
    return pl.pallas_call(
        _router_kernel,
        grid=(N // tm,),
        in_specs=[pl.BlockSpec((tm, D), lambda i: (i, 0)), pl.BlockSpec((LANE, D), lambda i: (0, 0)),
                  pl.BlockSpec((LANE, 1), lambda i: (0, 0)), pl.BlockSpec((tm, tm), lambda i: (0, 0))],
        out_specs=[pl.BlockSpec((tm, LANE), lambda i: (i, 0)), pl.BlockSpec((LANE, LANE), lambda i: (0, 0))],
        out_shape=[jax.ShapeDtypeStruct((N, LANE), f32), jax.ShapeDtypeStruct((LANE, LANE), f32)],
        compiler_params=_params(1),
        name="moe_router",
    )(x2, w, b, tri)


def _ffn_kernel(ce_ref, first_ref, slot_ref, next_ref, nu_ref, x_ref, wgu_hbm, wd_hbm, o_ref,
                wgu_f32, wd_f32, wgu_b, wd_b, sems, *, layer):
    c = pl.program_id(0)
    used = c < nu_ref[0]

    def fetch(e, slot):
        return (pltpu.make_async_copy(wgu_hbm.at[layer, e], wgu_f32.at[slot], sems.at[slot, 0]),
                pltpu.make_async_copy(wd_hbm.at[layer, e], wd_f32.at[slot], sems.at[slot, 1]))

    @pl.when(used & (c == 0))
    def _():
        for cp in fetch(ce_ref[0], 0):
            cp.start()

    @pl.when(used & (first_ref[c] == 1))
    def _():
        slot = slot_ref[c]
        for cp in fetch(ce_ref[c], slot):
            cp.wait()

        @pl.when(next_ref[c] >= 0)
        def _():
            for cp in fetch(next_ref[c], 1 - slot):
                cp.start()

        wgu_b[...] = wgu_f32[slot].astype(MXU)
        wd_b[...] = wd_f32[slot].astype(MXU)

    @pl.when(used)
    def _():
        x = _from_token_tiles(x_ref, EXPERT_CHUNK)
        gu = _dot(x.astype(MXU), wgu_b[...])
        hid = jax.nn.silu(gu[:, :D_EXPERT]) * gu[:, D_EXPERT:]
        _to_token_tiles(o_ref, _dot(hid.astype(MXU), wd_b[...]))

    @pl.when(jnp.logical_not(used))
    def _():
        o_ref[...] = jnp.zeros_like(o_ref)


def expert_ffn(xb8, chunk_e, n_used, w_gu, w_down, layer):
    C = EXPERT_CHUNK
    n_chunks = xb8.shape[0] // (C * ROW_TILE)
    D = ROW_TILE * LANE
    idx = jnp.arange(n_chunks, dtype=jnp.int32)
    used = idx < n_used[0]
    first = used & ((idx == 0) | (chunk_e != jnp.roll(chunk_e, 1)))
    slot = (jnp.cumsum(first.astype(jnp.int32)) - 1) % 2
    none = jnp.int32(N_EXPERTS)
    later = jnp.roll(jnp.where(first, chunk_e, none), -1).at[-1].set(none)
    next_e = lax.cummin(later, axis=0, reverse=True)
    next_e = jnp.where(next_e == none, -1, next_e)
    tile_spec = pl.BlockSpec((C * ROW_TILE, LANE), lambda c, *_: (c, 0))
    any_spec = pl.BlockSpec(memory_space=pl.ANY)
    grid_spec = pltpu.PrefetchScalarGridSpec(
        num_scalar_prefetch=5,
        grid=(n_chunks,),
        in_specs=[tile_spec, any_spec, any_spec],
        out_specs=tile_spec,
        scratch_shapes=[pltpu.VMEM((2, D, 2 * D_EXPERT), w_gu.dtype), pltpu.VMEM((2, D_EXPERT, D), w_down.dtype),
                        pltpu.VMEM((D, 2 * D_EXPERT), MXU), pltpu.VMEM((D_EXPERT, D), MXU),
                        pltpu.SemaphoreType.DMA((2, 2))],
    )
    return pl.pallas_call(
        functools.partial(_ffn_kernel, layer=layer),
        grid_spec=grid_spec,
        out_shape=jax.ShapeDtypeStruct(xb8.shape, f32),
        compiler_params=_params(1),
        name="expert_ffn",
    )(chunk_e, first.astype(jnp.int32), slot.astype(jnp.int32), next_e.astype(jnp.int32), n_used, xb8, w_gu, w_down)


_COPY_WINDOW = 512


def _windowed_copies(n, copy, wait):
    assert _COPY_WINDOW & (_COPY_WINDOW - 1) == 0 and n >= _COPY_WINDOW

    def fill(j, carry):
        copy(j, j).start()
        return carry

    def steady(j, carry):
        s = jnp.bitwise_and(j, _COPY_WINDOW - 1)
        wait(s)
        copy(j, s).start()
        return carry

    def drain(s, carry):
        wait(s)
        return carry

    lax.fori_loop(0, _COPY_WINDOW, fill, 0, unroll=8)
    lax.fori_loop(_COPY_WINDOW, n, steady, 0, unroll=8)
    lax.fori_loop(0, _COPY_WINDOW, drain, 0, unroll=8)


def _dispatch_kernel(dest_ref, x8_ref, init_hbm, xb_hbm, sems, *, tm):
    n = tm * TOPK_IN_GROUP
    base = pl.program_id(0) * n

    def copy(j, s):
        t = pl.multiple_of(jnp.right_shift(j, 1) * ROW_TILE, ROW_TILE)
        d = pl.multiple_of(dest_ref[base + j] * ROW_TILE, ROW_TILE)
        return pltpu.make_async_copy(x8_ref.at[pl.ds(t, ROW_TILE)], xb_hbm.at[pl.ds(d, ROW_TILE)], sems.at[s])

    def wait(s):
        pltpu.make_async_copy(x8_ref.at[pl.ds(0, ROW_TILE)], xb_hbm.at[pl.ds(0, ROW_TILE)], sems.at[s]).wait()

    _windowed_copies(n, copy, wait)


def moe_dispatch(x8, dest, n_slots, tm=_TM_DISPATCH):
    n_tok = x8.shape[0] // ROW_TILE
    assert n_tok % tm == 0 and tm * TOPK_IN_GROUP >= _COPY_WINDOW
    grid_spec = pltpu.PrefetchScalarGridSpec(
        num_scalar_prefetch=1,
        grid=(n_tok // tm,),
        in_specs=[pl.BlockSpec((tm * ROW_TILE, LANE), lambda i, dest: (i, 0)), pl.BlockSpec(memory_space=pl.ANY)],
        out_specs=pl.BlockSpec(memory_space=pl.ANY),
        scratch_shapes=[pltpu.SemaphoreType.DMA((_COPY_WINDOW,))],
    )
    return pl.pallas_call(
        functools.partial(_dispatch_kernel, tm=tm),
        grid_spec=grid_spec,
        out_shape=jax.ShapeDtypeStruct((n_slots * ROW_TILE, LANE), x8.dtype),
        input_output_aliases={2: 0},
        compiler_params=_params(1),
        name="moe_dispatch",
    )(dest, x8, jnp.zeros((n_slots * ROW_TILE, LANE), x8.dtype))


def _combine_kernel(dest_ref, x_ref, r_ref, g_ref, b_ref, yb_hbm, o_ref, buf, sems):
    tm = x_ref.shape[0]
    n = tm * TOPK_IN_GROUP
    i = pl.program_id(0)

    def issue(tile):
        dst_buf = buf.at[jnp.bitwise_and(tile, 1)]

        def body(j, carry):
            src = pl.multiple_of(dest_ref[tile * n + j] * ROW_TILE, ROW_TILE)
            dst = pl.multiple_of(j * ROW_TILE, ROW_TILE)
            pltpu.make_async_copy(yb_hbm.at[pl.ds(src, ROW_TILE)], dst_buf.at[pl.ds(dst, ROW_TILE)],
                                  sems.at[j]).start()
            return carry

        lax.fori_loop(0, n, body, 0, unroll=8)

    def drain(j, carry):
        pltpu.make_async_copy(yb_hbm.at[pl.ds(0, ROW_TILE)], buf.at[0, pl.ds(0, ROW_TILE)], sems.at[j]).wait()
        return carry

    @pl.when(i == 0)
    def _():
        issue(i)

    lax.fori_loop(0, n, drain, 0, unroll=8)

    @pl.when(i + 1 < pl.num_programs(0))
    def _():
        issue(i + 1)

    cur = buf.at[jnp.bitwise_and(i, 1)]
    r = r_ref[...]
    lane = lax.broadcasted_iota(jnp.int32, r.shape, 1)
    w0 = jnp.sum(jnp.where(lane == _R_W0, r, 0.0), axis=1, keepdims=True)
    w1 = jnp.sum(jnp.where(lane == _R_W1, r, 0.0), axis=1, keepdims=True)
    pair = TOPK_IN_GROUP * ROW_TILE
    y = _from_token_tiles(cur, tm, 0, pair) * w0 + _from_token_tiles(cur, tm, ROW_TILE, pair) * w1
    o_ref[...] = _ln(DN_ALPHA * x_ref[...] + y, g_ref[...], b_ref[...])


def combine_ln(x2, yb8, dest, r, g, b, tm=_TM_ROWS):
    N, D = x2.shape
    assert N % tm == 0
    row = lambda w: pl.BlockSpec((tm, w), lambda i, dest: (i, 0))
    vec = pl.BlockSpec((1, D), lambda i, dest: (0, 0))
    grid_spec = pltpu.PrefetchScalarGridSpec(
        num_scalar_prefetch=1,
        grid=(N // tm,),
        in_specs=[row(D), row(LANE), vec, vec, pl.BlockSpec(memory_space=pl.ANY)],
        out_specs=row(D),
        scratch_shapes=[pltpu.VMEM((2, tm * TOPK_IN_GROUP * ROW_TILE, LANE), f32),
                        pltpu.SemaphoreType.DMA((tm * TOPK_IN_GROUP,))],
    )
    return pl.pallas_call(
        _combine_kernel,
        grid_spec=grid_spec,
        out_shape=jax.ShapeDtypeStruct((N, D), f32),
        compiler_params=_params(1),
        name="moe_combine_ln",
    )(dest, x2, r, g.reshape(1, D), b.reshape(1, D), yb8)


def hier_moe_ln(x2, x8, rg_w, rg_b, re_w, re_b, w_gu, w_down, layer, g, b):
    N, D = x2.shape
    E, C, K = N_EXPERTS, EXPERT_CHUNK, TOPK_IN_GROUP
    A = N * K
    r, cnt = moe_router(x2, rg_w, rg_b, re_w, re_b)
    e = r[:, _R_E0:_R_E1 + 1].astype(jnp.int32)
    rank = r[:, _R_RANK0:_R_RANK1 + 1].astype(jnp.int32)
    counts = cnt[:E, 0].astype(jnp.int32)
    padded = (counts + C - 1) // C * C
    pad_end = jnp.cumsum(padded)
    pad_start = pad_end - padded
    dest = pad_start[e] + rank
    n_chunks = -(-(A + E * (C - 1)) // C)
    P = n_chunks * C
    chunk_start = jnp.arange(n_chunks, dtype=jnp.int32) * C
    chunk_e = jnp.minimum(jnp.sum((pad_end[None, :] <= chunk_start[:, None]).astype(jnp.int32), axis=1), E - 1)
    n_used = (pad_end[-1] // C).reshape(1).astype(jnp.int32)
    dest = dest.reshape(A).astype(jnp.int32)
    xb8 = moe_dispatch(x8, dest, P)
    yb8 = expert_ffn(xb8, chunk_e.astype(jnp.int32), n_used, w_gu, w_down, layer)
    return combine_ln(x2, yb8, dest, r, g, b)


def kernel(x, mem, w_in, nsa_cmp_pos, nsa_cmp_w1, nsa_cmp_w2, rnn_conv_w, rnn_conv_b, rnn_ga_w, rnn_ga_b,
           rnn_gx_w, rnn_gx_b, rnn_lambda, mla_q_norm, mla_kv_norm, mla_w_uq, mla_w_ukv, w_branch, w_out,
           ln1_g, ln1_b, x_wq, x_wkv, x_wo, ln2_g, ln2_b, moe_rg_w, moe_rg_b, moe_re_w, moe_re_b,
           moe_w_gu, moe_w_down, ln3_g, ln3_b):
    B, S, D = x.shape
    N = B * S
    x2 = x.reshape(N, D)
    for l in range(DEPTH):
        h2 = matmul(x2, _cat_w_in(w_in[l]), 512, _IN_TILE)
        h = h2.reshape(B, S, N_CAT)
        o_a = nsa_attention(h, nsa_cmp_pos[l], nsa_cmp_w1[l], nsa_cmp_w2[l])
        o_b = stick_breaking_attention(h)
        o_c = rglru_block(h, rnn_conv_w[l], rnn_conv_b[l], rnn_ga_w[l], rnn_ga_b[l], rnn_gx_w[l], rnn_gx_b[l],
                          rnn_lambda[l])
        o_d = mla_attention(h, mla_q_norm[l], mla_kv_norm[l], mla_w_uq[l], mla_w_ukv[l])
        branches = [o.reshape(N, MIX_W) for o in (o_a, o_b, o_c, o_d)]
        x2 = merge_ln(h2, branches, w_branch[l], w_out[l], x2, ln1_g[l], ln1_b[l])
        x3, x8 = cross_attention_ln(x2.reshape(B, S, D), mem, x_wq[l], x_wkv[l], x_wo[l], ln2_g[l], ln2_b[l])
        x2 = hier_moe_ln(x3.reshape(N, D), x8.reshape(N * ROW_TILE, LANE), moe_rg_w[l], moe_rg_b[l], moe_re_w[l],
                         moe_re_b[l], moe_w_gu, moe_w_down, l, ln3_g[l], ln3_b[l])
    return x2.reshape(B, S, D)
```

```python
import functools

import numpy as np
import jax
import jax.numpy as jnp
from jax import lax
from jax.experimental import pallas as pl
from jax.experimental.pallas import tpu as pltpu

D_MODEL = 1024
DEPTH = 4
HEAD_DIM = 64
N_MIXERS = 4
MIX_W = 256
NSA_HEADS = 4
NSA_KV_HEADS = 2
NSA_GROUP = NSA_HEADS // NSA_KV_HEADS
CMP_LEN = 32
CMP_STRIDE = 16
CMP_HID = 256
SEL_BLOCK = 64
SEL_TOPK = 8
WINDOW = 512
FORCE_SCORE = 1e4
SB_HEADS = 4
RNN_W = 256
CONV_W = 4
LRU_C = 8.0
MLA_HEADS = 4
MLA_Q_RANK = 192
MLA_KV_RANK = 128
MLA_NOPE = 64
MLA_ROPE = 32
MLA_V = 64
ROPE_THETA = 10000.0
X_HEADS = 4
X_HEAD_DIM = 128
N_GROUPS = 4
EXPERTS_PER_GROUP = 8
N_EXPERTS = N_GROUPS * EXPERTS_PER_GROUP
TOPK_IN_GROUP = 2
D_EXPERT = 512
EXPERT_CHUNK = 256
DN_ALPHA = (2.0 * DEPTH) ** 0.25
LN_EPS = 1e-5
RMS_EPS = 1e-6

IN_SPLITS = ((NSA_HEADS * HEAD_DIM,) + (NSA_KV_HEADS * HEAD_DIM,) * 6 + (NSA_HEADS * 3,)
             + (SB_HEADS * HEAD_DIM,) * 3
             + (RNN_W, RNN_W)
             + (MLA_Q_RANK, MLA_KV_RANK, MLA_ROPE)
             + (N_MIXERS * D_MODEL,))
IN_OFFSETS = tuple(int(o) for o in np.concatenate([[0], np.cumsum(IN_SPLITS)[:-1]]))

LANE = 128
VMEM_LIMIT = 48 * 1024 * 1024
NEG = -1e30
BIG_NEG = -2.0 ** 100

f32 = jnp.float32
MXU = jnp.bfloat16

_CQ_PAD = 2 * LANE
_W_Q, _W_KV, _W_KVT = NSA_HEADS * HEAD_DIM, NSA_KV_HEADS * HEAD_DIM, NSA_KV_HEADS * LANE
_SECTIONS = (("mg", N_MIXERS * D_MODEL), ("aq", _W_Q), ("kvs", _W_KVT), ("kvw", _W_KVT),
             ("bq", SB_HEADS * HEAD_DIM), ("bk", SB_HEADS * HEAD_DIM), ("bv", SB_HEADS * HEAD_DIM),
             ("cx", RNN_W), ("cg", RNN_W), ("dcq", _CQ_PAD), ("kc", _W_KV), ("vc", _W_KV),
             ("ag", LANE), ("dckv", MLA_KV_RANK), ("dkr", LANE), ("dkrs", LANE))
_IN_TILE = 3584
OFF = {}
_o = 0
for _n, _w in _SECTIONS:
    OFF[_n] = _o
    _o += _w
N_CAT = -(-_o // _IN_TILE) * _IN_TILE

_SEL_ROWS = 32
_AUG_SEL = HEAD_DIM
_AUG_POS = HEAD_DIM + _SEL_ROWS
_POS_SPLIT = 256

_TQ_NSA = 512
_TQ_MLA = 512
_TQ_SB = 256
_TM_ROWS = 256
_TM_DISPATCH = 256
_TR_MLA_PREP = 512


def _rot_half(t):
    d = t.shape[-1]
    return jnp.concatenate([-t[..., d // 2:], t[..., :d // 2]], axis=-1)


def _cat_w_in(w):
    def sec(i):
        return w[:, IN_OFFSETS[i]:IN_OFFSETS[i] + IN_SPLITS[i]]

    def pair(k, v):
        return jnp.concatenate([k[:, :HEAD_DIM], v[:, :HEAD_DIM], k[:, HEAD_DIM:], v[:, HEAD_DIM:]], axis=1)

    def padc(t, n):
        return jnp.pad(t, ((0, 0), (0, n - t.shape[1])))

    def rope_slot(t):
        return jnp.pad(t, ((0, 0), (MLA_NOPE, LANE - MLA_NOPE - MLA_ROPE)))

    parts = {"mg": sec(16), "aq": sec(0), "kc": sec(1), "vc": sec(2), "kvs": pair(sec(3), sec(4)),
             "kvw": pair(sec(5), sec(6)), "ag": padc(sec(7), LANE), "bq": sec(8), "bk": sec(9),
             "bv": sec(10), "cx": sec(11), "cg": sec(12), "dcq": padc(sec(13), _CQ_PAD), "dckv": sec(14),
             "dkr": rope_slot(sec(15)), "dkrs": rope_slot(_rot_half(sec(15)))}
    cat = jnp.concatenate([parts[n] for n, _ in _SECTIONS], axis=1)
    return padc(cat, N_CAT).astype(MXU)


def _dot(a, b):
    return jnp.dot(a, b, preferred_element_type=f32)


def _dot_nt(a, b):
    return lax.dot_general(a, b, (((1,), (1,)), ((), ())), preferred_element_type=f32)


def _params(n_axes):
    return pltpu.CompilerParams(dimension_semantics=("arbitrary",) * n_axes, vmem_limit_bytes=VMEM_LIMIT)


def _mm_kernel(a_ref, b_ref, o_ref):
    o_ref[...] = _dot(a_ref[...].astype(MXU), b_ref[...]).astype(o_ref.dtype)


def matmul(a, b, tm, tn, out_dtype=f32):
    M, K = a.shape
    _, N = b.shape
    assert M % tm == 0 and N % tn == 0
    return pl.pallas_call(
        _mm_kernel,
        grid=(N // tn, M // tm),
        in_specs=[pl.BlockSpec((tm, K), lambda j, i: (i, 0)),
                  pl.BlockSpec((K, tn), lambda j, i: (0, j))],
        out_specs=pl.BlockSpec((tm, tn), lambda j, i: (i, j)),
        out_shape=jax.ShapeDtypeStruct((M, N), out_dtype),
        compiler_params=_params(2),
        name="matmul",
    )(a, b)


def _cmp_kernel(t_ref, w1_ref, pos_ref, w2_ref, o_ref):
    half = t_ref.shape[3]
    t = t_ref[0, 0].astype(MXU)
    y1 = _dot(t, w1_ref[0, :half, :])
    y2 = _dot(t, w1_ref[0, half:, :])
    pos = jnp.broadcast_to(pos_ref[0], (8, 2 * half)).astype(MXU)
    pc = _dot(pos, w1_ref[0])[0:1]
    nc = y2.shape[0]
    hid = y1 + pltpu.roll(y2, nc - 1, 0) + pc
    o_ref[0, 0] = _dot(jax.nn.gelu(hid).astype(MXU), w2_ref[0])


def nsa_compress(t, cmp_w1, cmp_pos, cmp_w2):
    _, B, NC, F = t.shape
    G = NSA_KV_HEADS
    eye = jnp.eye(G, dtype=f32)
    w1 = jnp.einsum('jldh,gk->jlgdkh', cmp_w1.reshape(2, CMP_LEN, HEAD_DIM, CMP_HID), eye)
    w1 = w1.reshape(2, 2 * F, G * CMP_HID).astype(MXU)
    pos = jnp.broadcast_to(cmp_pos[:, :, None, :], (2, CMP_LEN, G, HEAD_DIM)).reshape(2, 1, 2 * F)
    w2 = jnp.einsum('jhc,gk->jghkc', jnp.concatenate([cmp_w2, cmp_w2], axis=-1), eye)
    w2 = w2.reshape(2, G * CMP_HID, G * LANE).astype(MXU)
    return pl.pallas_call(
        _cmp_kernel,
        grid=(2, B),
        in_specs=[pl.BlockSpec((1, 1, NC, F), lambda j, i: (j, i, 0, 0)),
                  pl.BlockSpec((1, 2 * F, G * CMP_HID), lambda j, i: (j, 0, 0)),
                  pl.BlockSpec((1, 1, 2 * F), lambda j, i: (j, 0, 0)),
                  pl.BlockSpec((1, G * CMP_HID, G * LANE), lambda j, i: (j, 0, 0))],
        out_specs=pl.BlockSpec((1, 1, NC, G * LANE), lambda j, i: (j, i, 0, 0)),
        out_shape=jax.ShapeDtypeStruct((2, B, NC, G * LANE), f32),
        compiler_params=_params(2),
        name="nsa_compress",
    )(t, w1, pos, w2)


def _nsa_kernel(q_ref, kc_ref, vc_ref, kvs_ref, kvw_ref, gl_ref, cover_ref, aug_ref, cb_ref, wb_ref,
                o_ref, ksa, vsa, kwa, vwa, *, tq, n_cmp, n_sel, n_top):
    tk = tq
    g = pl.program_id(1)
    qi = pl.program_id(2)
    q0 = pl.multiple_of(qi * tq, tq)
    lane = lax.broadcasted_iota(jnp.int32, (tq, LANE), 1)
    lo_half = lane < HEAD_DIM

    @pl.when(qi == 0)
    def _():
        real = lax.broadcasted_iota(jnp.int32, ksa.shape, 1) < HEAD_DIM
        aug = aug_ref[...]
        ones = jnp.ones(ksa.shape, MXU)
        for kv_ref, k_out, v_out in ((kvs_ref, ksa, vsa), (kvw_ref, kwa, vwa)):
            kv = kv_ref[0]
            k_out[...] = jnp.where(real, kv.astype(MXU), aug)
            v_out[...] = jnp.where(real, pltpu.roll(kv, HEAD_DIM, 1).astype(MXU), ones)

    q = q_ref[0] * (HEAD_DIM ** -0.5)
    q_heads = (jnp.where(lo_half, q, 0.0), jnp.where(lo_half, pltpu.roll(q, HEAD_DIM, 1), 0.0))
    alibi = [2.0 ** (-8.0 * (h + 1) / NSA_HEADS) for h in range(NSA_HEADS)]
    slopes = [jnp.where(g == 0, alibi[n], alibi[NSA_GROUP + n]) for n in range(NSA_GROUP)]
    pos_cols = [jnp.where(lane == _AUG_POS, slopes[n] * _POS_SPLIT, jnp.where(lane == _AUG_POS + 1, slopes[n], 0.0))
                for n in range(NSA_GROUP)]
    tpos = q0 + lax.broadcasted_iota(jnp.int32, (tq, 1), 0)

    nc = kc_ref.shape[2]
    cidx = lax.broadcasted_iota(jnp.int32, (1, nc), 1)
    dist_c = tpos - (cidx * CMP_STRIDE + (CMP_LEN - 1))
    mask_c = (dist_c >= 0) & (cidx < n_cmp)
    dist_cf = dist_c.astype(f32)
    kc = kc_ref[0, 0].astype(MXU)
    vc = vc_ref[0, 0].astype(MXU)
    o_cmp = []
    imp_t = jnp.zeros((LANE, tq), f32)
    for n in range(NSA_GROUP):
        s = _dot_nt(q_heads[n].astype(MXU), kc) - slopes[n] * dist_cf
        sm = jnp.where(mask_c, s, NEG)
        m = jnp.max(sm, axis=1, keepdims=True)
        p = jnp.where(mask_c, jnp.exp(sm - m), 0.0)
        p = (p / jnp.maximum(jnp.sum(p, axis=1, keepdims=True), 1e-30)).astype(MXU)
        o_cmp.append(_dot(p, vc))
        imp_t = imp_t + _dot_nt(cover_ref[...], p)

    rows = _SEL_ROWS
    imp = imp_t[:rows]
    blk = lax.broadcasted_iota(jnp.int32, (rows, tq), 0)
    blk_f = blk.astype(f32)
    tpos_t = q0 + lax.broadcasted_iota(jnp.int32, (1, tq), 1)
    forced = (blk == 0) | (blk == jnp.right_shift(tpos_t, SEL_BLOCK.bit_length() - 1))
    valid = blk * SEL_BLOCK <= tpos_t
    imp = jnp.where(forced, FORCE_SCORE, jnp.where(valid, imp, -1.0))
    imp = jnp.where(blk < n_sel, imp, NEG)
    sel_t = jnp.zeros((rows, tq), f32)
    for _ in range(n_top):
        m = jnp.max(imp, axis=0, keepdims=True)
        first = jnp.min(jnp.where(imp == m, blk_f, float(LANE)), axis=0, keepdims=True)
        pick = blk_f == first
        sel_t = jnp.where(pick, 1.0, sel_t)
        imp = jnp.where(pick, 2 * NEG, imp)
    sel = jnp.concatenate([sel_t, jnp.zeros((LANE - rows, tq), f32)], axis=0).T
    sel_bias = pltpu.roll(jnp.where(sel > 0.5, 0.0, BIG_NEG), _AUG_SEL, 1)
    sel_cols = jnp.where((lane >= _AUG_SEL) & (lane < _AUG_SEL + rows), sel_bias, 0.0)
    q_sel = jnp.concatenate([q_heads[n] + sel_cols + pos_cols[n] for n in range(NSA_GROUP)], axis=0).astype(MXU)
    q_win = jnp.concatenate([q_heads[n] + pos_cols[n] for n in range(NSA_GROUP)], axis=0).astype(MXU)
    rows_q = NSA_GROUP * tq
    denom_lane = lax.broadcasted_iota(jnp.int32, (rows_q, LANE), 1) == HEAD_DIM

    def update(carry, qs, k, v, bias):
        m, acc = carry
        s = _dot_nt(qs, k)
        if bias is not None:
            s = s + bias
        m_new = jnp.maximum(m, jnp.max(s, axis=1, keepdims=True))
        p = jnp.exp(s - m_new)
        return m_new, jnp.exp(m - m_new) * acc + _dot(p.astype(MXU), v)

    def finish(carry):
        _, acc = carry
        denom = jnp.sum(jnp.where(denom_lane, acc, 0.0), axis=1, keepdims=True)
        o = acc / jnp.maximum(denom, 1e-30)
        return [o[n * tq:(n + 1) * tq] for n in range(NSA_GROUP)]

    init = (jnp.full((rows_q, 1), NEG, f32), jnp.zeros((rows_q, LANE), f32))
    causal = cb_ref[...]

    def sel_body(kt, carry):
        k0 = pl.multiple_of(kt * tk, tk)
        return update(carry, q_sel, ksa[pl.ds(k0, tk), :], vsa[pl.ds(k0, tk), :], None)

    carry = lax.fori_loop(0, qi, sel_body, init)
    o_sel = finish(update(carry, q_sel, ksa[pl.ds(q0, tk), :], vsa[pl.ds(q0, tk), :], causal))

    carry = init
    n_back = WINDOW // tk
    for back in range(n_back, -1, -1):
        k0 = pl.multiple_of(jnp.maximum(qi - back, 0) * tk, tk)
        if back:
            off = jnp.where(qi >= back, 0.0, BIG_NEG)
            bias = wb_ref[...] + off if back == n_back else off
        else:
            bias = causal
        carry = update(carry, q_win, kwa[pl.ds(k0, tk), :], vwa[pl.ds(k0, tk), :], bias)
    o_win = finish(carry)

    sig = jax.nn.sigmoid(gl_ref[0])

    def gate(n, j):
        col = 3 * (NSA_GROUP * g + n) + j
        return jnp.sum(jnp.where(lane == col, sig, 0.0), axis=1, keepdims=True)

    o = [gate(n, 0) * o_cmp[n] + gate(n, 1) * o_sel[n] + gate(n, 2) * o_win[n] for n in range(NSA_GROUP)]
    o_ref[0] = jnp.where(lo_half, o[0], pltpu.roll(o[1], HEAD_DIM, 1))


def nsa_attention(h, cmp_pos, cmp_w1, cmp_w2, tq=_TQ_NSA):
    B, S, _ = h.shape
    G = NSA_KV_HEADS
    NC = S // CMP_STRIDE
    n_cmp = (S - CMP_LEN) // CMP_STRIDE + 1
    n_sel = S // SEL_BLOCK
    n_top = min(SEL_TOPK, n_sel)
    F = CMP_STRIDE * HEAD_DIM
    assert S % tq == 0 and WINDOW % tq == 0 and tq % SEL_BLOCK == 0 and n_sel <= _SEL_ROWS and S <= _POS_SPLIT ** 2

    def chunks(off):
        return h[:, :, off:off + G * HEAD_DIM].reshape(B, NC, CMP_STRIDE * G * HEAD_DIM)

    kvc = nsa_compress(jnp.stack([chunks(OFF["kc"]), chunks(OFF["vc"])]), cmp_w1, cmp_pos, cmp_w2)

    c0 = np.arange(n_cmp)[:, None] * CMP_STRIDE
    j0 = np.arange(n_sel)[None, :] * SEL_BLOCK
    cover = np.clip(np.minimum(c0 + CMP_LEN, j0 + SEL_BLOCK) - np.maximum(c0, j0), 0, None) / CMP_LEN
    cover_t = np.zeros((LANE, NC), np.float32)
    cover_t[:n_sel, :n_cmp] = cover.T
    pos = np.arange(S)
    aug = np.zeros((S, LANE), np.float32)
    aug[pos, _AUG_SEL + pos // SEL_BLOCK] = 1.0
    aug[:, _AUG_POS] = pos // _POS_SPLIT
    aug[:, _AUG_POS + 1] = pos % _POS_SPLIT
    rel = np.arange(tq)[:, None] - np.arange(tq)[None, :]
    rel = np.tile(rel, (NSA_GROUP, 1))
    causal = np.where(rel >= 0, 0.0, BIG_NEG).astype(np.float32)
    win_lo = np.where(rel < 0, 0.0, BIG_NEG).astype(np.float32)

    col = lambda name: OFF[name] // LANE
    full2 = lambda shape: pl.BlockSpec(shape, lambda b, g, i: (0, 0))
    q_spec = pl.BlockSpec((1, tq, LANE), lambda b, g, i: (b, i, col("aq") + g))
    c_specs = [pl.BlockSpec((1, 1, NC, LANE), lambda b, g, i, j=j: (j, b, 0, g)) for j in range(2)]
    kv_specs = [pl.BlockSpec((1, S, LANE), lambda b, g, i, c=col(n): (b, 0, c + g))
                for n in ("kvs", "kvw")]
    gl_spec = pl.BlockSpec((1, tq, LANE), lambda b, g, i: (b, i, col("ag")))
    return pl.pallas_call(
        functools.partial(_nsa_kernel, tq=tq, n_cmp=n_cmp, n_sel=n_sel, n_top=n_top),
        grid=(B, G, S // tq),
        in_specs=[q_spec] + c_specs + kv_specs + [gl_spec, full2((LANE, NC)), full2((S, LANE)),
                                                  full2((NSA_GROUP * tq, tq)), full2((NSA_GROUP * tq, tq))],
        out_specs=pl.BlockSpec((1, tq, LANE), lambda b, g, i: (b, i, g)),
        out_shape=jax.ShapeDtypeStruct((B, S, NSA_HEADS * HEAD_DIM), f32),
        scratch_shapes=[pltpu.VMEM((S, LANE), MXU)] * 4,
        compiler_params=_params(3),
        name="nsa_attention",
    )(h, kvc, kvc, h, h, h, jnp.asarray(cover_t, MXU), jnp.asarray(aug, MXU), jnp.asarray(causal),
      jnp.asarray(win_lo))


_SB_DEAD = -104.0


def _log_sigmoid(z):
    return jnp.minimum(z, 0.0) - jnp.log1p(jnp.exp(-jnp.abs(z)))


def _sb_kernel(q_ref, k_ref, v_ref, u_ref, o_ref, kb, vb, *, tq):
    tk = tq
    qi = pl.program_id(2)
    q0 = pl.multiple_of(qi * tq, tq)
    lane = lax.broadcasted_iota(jnp.int32, (tq, LANE), 1)
    lo_half = lane < HEAD_DIM

    @pl.when(qi == 0)
    def _():
        kb[...] = k_ref[0].astype(MXU)
        vb[...] = v_ref[0].astype(MXU)

    q = q_ref[0] * (HEAD_DIM ** -0.5)
    q2 = jnp.concatenate([jnp.where(lo_half, q, 0.0), jnp.where(lo_half, 0.0, q)], axis=0).astype(MXU)
    u = u_ref[...]

    def tile(carry, k, v, strict):
        c, acc = carry
        z = _dot_nt(q2, k)
        ls = _log_sigmoid(z)
        log_1m = ls - z
        if strict is not None:
            log_1m = jnp.where(strict, log_1m, 0.0)
        hi = log_1m.astype(MXU)
        lo = (log_1m - hi.astype(f32)).astype(MXU)
        tail = _dot(hi, u) + _dot(lo, u) + c
        a = jnp.exp(ls + tail)
        if strict is not None:
            a = jnp.where(strict, a, 0.0)
        return c + jnp.sum(log_1m, axis=1, keepdims=True), acc + _dot(a.astype(MXU), v)

    row = lax.broadcasted_iota(jnp.int32, (2 * tq, tk), 0)
    rel = jnp.where(row >= tq, row - tq, row) - lax.broadcasted_iota(jnp.int32, (2 * tq, tk), 1)
    zero = (jnp.zeros((2 * tq, 1), f32), jnp.zeros((2 * tq, LANE), f32))
    state = tile(zero, kb[pl.ds(q0, tk), :], vb[pl.ds(q0, tk), :], rel > 0)

    def alive(state):
        return (jnp.max(state[0]) > _SB_DEAD).astype(jnp.int32)

    def cond(loop):
        kt, live, _ = loop
        return (kt >= 0) & (live > 0)

    def body(loop):
        kt, _, state = loop
        k0 = pl.multiple_of(kt * tk, tk)
        state = tile(state, kb[pl.ds(k0, tk), :], vb[pl.ds(k0, tk), :], None)
        return kt - 1, alive(state), state

    _, _, (_, acc) = lax.while_loop(cond, body, (qi - 1, alive(state), state))
    o_ref[0] = jnp.where(lo_half, acc[:tq], acc[tq:])


def stick_breaking_attention(h, tq=_TQ_SB):
    B, S, _ = h.shape
    assert S % tq == 0
    tri = jnp.asarray(np.tril(np.ones((tq, tq), np.float32), -1), MXU)
    cq, ck, cv = (OFF[n] // LANE for n in ("bq", "bk", "bv"))
    return pl.pallas_call(
        functools.partial(_sb_kernel, tq=tq),
        grid=(B, SB_HEADS // 2, S // tq),
        in_specs=[pl.BlockSpec((1, tq, LANE), lambda b, p, i: (b, i, cq + p)),
                  pl.BlockSpec((1, S, LANE), lambda b, p, i: (b, 0, ck + p)),
                  pl.BlockSpec((1, S, LANE), lambda b, p, i: (b, 0, cv + p)),
                  pl.BlockSpec((tq, tq), lambda b, p, i: (0, 0))],
        out_specs=pl.BlockSpec((1, tq, LANE), lambda b, p, i: (b, i, p)),
        out_shape=jax.ShapeDtypeStruct((B, S, SB_HEADS * HEAD_DIM), f32),
        scratch_shapes=[pltpu.VMEM((S, LANE), MXU)] * 2,
        compiler_params=_params(3),
        name="sb_attention",
    )(h, h, h, tri)


def _neg_expm1(y):
    series = -y * (1.0 + y * (1.0 / 2 + y * (1.0 / 6 + y * (1.0 / 24 + y * (1.0 / 120)))))
    return jnp.where(y > -0.1, series, 1.0 - jnp.exp(y))


def _rglru_kernel(x_ref, xg_ref, cw_ref, cb_ref, gaw_ref, gab_ref, gxw_ref, gxb_ref, lam_ref, o_ref):
    x = x_ref[0]
    S = x.shape[0]
    row = lax.broadcasted_iota(jnp.int32, (S, 1), 0)

    def shifted(t, d, fill):
        return jnp.where(row >= d, pltpu.roll(t, d, 0), fill)

    u = cb_ref[...] + x * cw_ref[CONV_W - 1:CONV_W, :]
    for d in range(1, CONV_W):
        u = u + shifted(x, d, 0.0) * cw_ref[CONV_W - 1 - d:CONV_W - d, :]
    ub = u.astype(MXU)
    r = jax.nn.sigmoid(_dot(ub, gaw_ref[...]) + gab_ref[...])
    i = jax.nn.sigmoid(_dot(ub, gxw_ref[...]) + gxb_ref[...])
    lam = lam_ref[...]
    softplus_neg = jnp.maximum(-lam, 0.0) + jnp.log1p(jnp.exp(-jnp.abs(lam)))
    log_a = -LRU_C * r * softplus_neg
    a = jnp.exp(log_a)
    b = jnp.sqrt(_neg_expm1(2.0 * log_a)) * (i * u)
    d = 1
    while d < S:
        b = a * shifted(b, d, 0.0) + b
        a = a * shifted(a, d, 1.0)
        d *= 2
    o_ref[0] = b * jax.nn.gelu(xg_ref[0])


def _block_diag(w):
    n, c, _ = w.shape
    out = jnp.zeros((n * c, n * c), w.dtype)
    for j in range(n):
        out = out.at[j * c:(j + 1) * c, j * c:(j + 1) * c].set(w[j])
    return out


def rglru_block(h, conv_w, conv_b, ga_w, ga_b, gx_w, gx_b, lru_lambda):
    B, S, _ = h.shape
    W = RNN_W
    cx, cg = OFF["cx"] // W, OFF["cg"] // W
    vec = pl.BlockSpec((1, W), lambda b: (0, 0))
    mat = pl.BlockSpec((W, W), lambda b: (0, 0))
    return pl.pallas_call(
        _rglru_kernel,
        grid=(B,),
        in_specs=[pl.BlockSpec((1, S, W), lambda b: (b, 0, cx)), pl.BlockSpec((1, S, W), lambda b: (b, 0, cg)),
                  pl.BlockSpec((CONV_W, W), lambda b: (0, 0)), vec, mat, vec, mat, vec, vec],
        out_specs=pl.BlockSpec((1, S, W), lambda b: (b, 0, 0)),
        out_shape=jax.ShapeDtypeStruct((B, S, W), f32),
        compiler_params=_params(1),
        name="rglru",
    )(h, h, conv_w, conv_b.reshape(1, W), _block_diag(ga_w).astype(MXU), ga_b.reshape(1, W),
      _block_diag(gx_w).astype(MXU), gx_b.reshape(1, W), lru_lambda.reshape(1, W))


def _rms(x, g, width):
    return x * lax.rsqrt(jnp.sum(x * x, axis=-1, keepdims=True) * (1.0 / width) + RMS_EPS) * g


def _mla_prep_kernel(cq_ref, ckv_ref, kr_ref, krs_ref, gq_ref, gkv_ref, wq_ref, wqs_ref, wk_ref, wv_ref,
                     cosq_ref, sinq_ref, cosk_ref, sink_ref, vone_ref, q_ref, k_ref, v_ref):
    cq = _rms(cq_ref[0], gq_ref[...], MLA_Q_RANK).astype(MXU)
    ckv = _rms(ckv_ref[0], gkv_ref[...], MLA_KV_RANK).astype(MXU)
    scale = (MLA_NOPE + MLA_ROPE) ** -0.5
    q = _dot(cq, wq_ref[...]) * cosq_ref[...] + _dot(cq, wqs_ref[...]) * sinq_ref[...]
    q_ref[0] = (q * scale).astype(q_ref.dtype)
    k_rope = kr_ref[0] * cosk_ref[...] + krs_ref[0] * sink_ref[...]
    k = _dot(ckv, wk_ref[...])
    k_ref[0] = (k + jnp.concatenate([k_rope] * MLA_HEADS, axis=1)).astype(k_ref.dtype)
    v_ref[0] = (_dot(ckv, wv_ref[...]) + vone_ref[...]).astype(v_ref.dtype)


def _mla_attn_kernel(q_ref, k_ref, v_ref, cb_ref, o_ref, *, tq):
    tk = tq
    qi = pl.program_id(2)
    q0 = pl.multiple_of(qi * tq, tq)
    q = q_ref[0]
    lane = lax.broadcasted_iota(jnp.int32, (tq, LANE), 1)

    def update(carry, n, k, v, bias):
        m, acc = carry
        sl = slice(n * LANE, (n + 1) * LANE)
        s = _dot_nt(q[:, sl], k[:, sl])
        if bias is not None:
            s = s + bias
        m_new = jnp.maximum(m, jnp.max(s, axis=1, keepdims=True))
        p = jnp.exp(s - m_new)
        return m_new, jnp.exp(m - m_new) * acc + _dot(p.astype(MXU), v[:, sl])

    def body(kt, carry):
        k0 = pl.multiple_of(kt * tk, tk)
        k = k_ref[0, pl.ds(k0, tk), :]
        v = v_ref[0, pl.ds(k0, tk), :]
        return tuple(update(carry[n], n, k, v, None) for n in range(2))

    init = tuple((jnp.full((tq, 1), NEG, f32), jnp.zeros((tq, LANE), f32)) for _ in range(2))
    carry = lax.fori_loop(0, qi, body, init)
    k = k_ref[0, pl.ds(q0, tk), :]
    v = v_ref[0, pl.ds(q0, tk), :]
    o = []
    for n in range(2):
        _, acc = update(carry[n], n, k, v, cb_ref[...])
        denom = jnp.sum(jnp.where(lane == MLA_V, acc, 0.0), axis=1, keepdims=True)
        o.append(acc / jnp.maximum(denom, 1e-30))
    o_ref[0] = jnp.where(lane < MLA_V, o[0], pltpu.roll(o[1], MLA_V, 1))


def mla_attention(h, q_norm, kv_norm, w_uq, w_ukv, tr=_TR_MLA_PREP, tq=_TQ_MLA):
    B, S, _ = h.shape
    H = MLA_HEADS
    dq = MLA_NOPE + MLA_ROPE
    HW = H * LANE
    wq3 = w_uq.reshape(MLA_Q_RANK, H, dq)
    wq_rot = jnp.concatenate([jnp.zeros_like(wq3[..., :MLA_NOPE]), _rot_half(wq3[..., MLA_NOPE:])], axis=-1)

    def pad_q(w3):
        w3 = jnp.pad(w3, ((0, _CQ_PAD - MLA_Q_RANK), (0, 0), (0, LANE - dq)))
        return w3.reshape(_CQ_PAD, HW).astype(MXU)

    wkv3 = w_ukv.reshape(MLA_KV_RANK, H, MLA_NOPE + MLA_V)
    wk = jnp.pad(wkv3[..., :MLA_NOPE], ((0, 0), (0, 0), (0, LANE - MLA_NOPE))).reshape(MLA_KV_RANK, HW).astype(MXU)
    wv = jnp.pad(wkv3[..., MLA_NOPE:], ((0, 0), (0, 0), (0, LANE - MLA_V))).reshape(MLA_KV_RANK, HW).astype(MXU)
    v_one = jnp.tile(jnp.concatenate([jnp.zeros((1, MLA_V), f32), jnp.ones((1, LANE - MLA_V), f32)], axis=1), (1, H))
    gq = jnp.pad(q_norm, (0, _CQ_PAD - MLA_Q_RANK)).reshape(1, _CQ_PAD)
    gkv = kv_norm.reshape(1, MLA_KV_RANK)
    inv = ROPE_THETA ** (-jnp.arange(0, MLA_ROPE, 2, dtype=f32) / MLA_ROPE)
    ang = jnp.arange(S, dtype=f32)[:, None] * inv[None, :]
    cos2 = jnp.concatenate([jnp.cos(ang), jnp.cos(ang)], axis=1)
    sin2 = jnp.concatenate([jnp.sin(ang), jnp.sin(ang)], axis=1)
    tail = LANE - dq
    cos_k = jnp.concatenate([jnp.zeros((S, MLA_NOPE), f32), cos2, jnp.zeros((S, tail), f32)], axis=1)
    sin_k = jnp.concatenate([jnp.zeros((S, MLA_NOPE), f32), sin2, jnp.zeros((S, tail), f32)], axis=1)
    cos_q = jnp.tile(jnp.concatenate([jnp.ones((S, MLA_NOPE), f32), cos2, jnp.zeros((S, tail), f32)], axis=1), (1, H))
    sin_q = jnp.tile(sin_k, (1, H))
    rel = np.arange(tq)[:, None] - np.arange(tq)[None, :]
    causal = jnp.asarray(np.where(rel >= 0, 0.0, BIG_NEG).astype(np.float32))

    c_cq, c_ckv, c_kr, c_krs = OFF["dcq"] // _CQ_PAD, OFF["dckv"] // LANE, OFF["dkr"] // LANE, OFF["dkrs"] // LANE
    full = lambda shape: pl.BlockSpec(shape, lambda b, i: (0, 0))
    tab = lambda w: pl.BlockSpec((tr, w), lambda b, i: (i, 0))
    out3 = pl.BlockSpec((1, tr, HW), lambda b, i: (b, i, 0))
    q, k, v = pl.pallas_call(
        _mla_prep_kernel,
        grid=(B, S // tr),
        in_specs=[pl.BlockSpec((1, tr, _CQ_PAD), lambda b, i: (b, i, c_cq)),
                  pl.BlockSpec((1, tr, LANE), lambda b, i: (b, i, c_ckv)),
                  pl.BlockSpec((1, tr, LANE), lambda b, i: (b, i, c_kr)),
                  pl.BlockSpec((1, tr, LANE), lambda b, i: (b, i, c_krs)),
                  full((1, _CQ_PAD)), full((1, MLA_KV_RANK)), full((_CQ_PAD, HW)), full((_CQ_PAD, HW)),
                  full((MLA_KV_RANK, HW)), full((MLA_KV_RANK, HW)),
                  tab(HW), tab(HW), tab(LANE), tab(LANE), full((1, HW))],
        out_specs=[out3, out3, out3],
        out_shape=[jax.ShapeDtypeStruct((B, S, HW), MXU)] * 3,
        compiler_params=_params(2),
        name="mla_prep",
    )(h, h, h, h, gq, gkv, pad_q(wq3), pad_q(wq_rot), wk, wv, cos_q, sin_q, cos_k, sin_k, v_one)
    pair = lambda rows: pl.BlockSpec((1, rows, 2 * LANE), lambda b, p, i: (b, i if rows == tq else 0, p))
    return pl.pallas_call(
        functools.partial(_mla_attn_kernel, tq=tq),
        grid=(B, H // 2, S // tq),
        in_specs=[pair(tq), pair(S), pair(S), pl.BlockSpec((tq, tq), lambda b, p, i: (0, 0))],
        out_specs=pl.BlockSpec((1, tq, LANE), lambda b, p, i: (b, i, p)),
        out_shape=jax.ShapeDtypeStruct((B, S, H * MLA_V), f32),
        compiler_params=_params(3),
        name="mla_attention",
    )(q, k, v, causal)


def _ln(z, g, b):
    mu = jnp.mean(z, axis=-1, keepdims=True)
    zc = z - mu
    var = jnp.mean(zc * zc, axis=-1, keepdims=True)
    return zc * lax.rsqrt(var + LN_EPS) * g + b


def _merge_kernel(mg_ref, oa_ref, ob_ref, oc_ref, od_ref, wb_ref, wo_ref, x_ref, g_ref, b_ref, o_ref):
    acc = None
    for n, br in enumerate((oa_ref, ob_ref, oc_ref, od_ref)):
        up = _dot(br[...].astype(MXU), wb_ref[n])
        term = jax.nn.sigmoid(mg_ref[:, n * D_MODEL:(n + 1) * D_MODEL]) * up
        acc = term if acc is None else acc + term
    y = _dot(acc.astype(MXU), wo_ref[...])
    o_ref[...] = _ln(DN_ALPHA * x_ref[...] + y, g_ref[...], b_ref[...])


def merge_ln(h2, branches, w_branch, w_out, x2, g, b, tm=_TM_ROWS):
    N, D = x2.shape
    assert OFF["mg"] == 0 and N % tm == 0
    row = lambda w: pl.BlockSpec((tm, w), lambda i: (i, 0))
    return pl.pallas_call(
        _merge_kernel,
        grid=(N // tm,),
        in_specs=[row(N_MIXERS * D)] + [row(MIX_W)] * N_MIXERS
        + [pl.BlockSpec((N_MIXERS, MIX_W, D), lambda i: (0, 0, 0)), pl.BlockSpec((D, D), lambda i: (0, 0)),
           row(D), pl.BlockSpec((1, D), lambda i: (0, 0)), pl.BlockSpec((1, D), lambda i: (0, 0))],
        out_specs=row(D),
        out_shape=jax.ShapeDtypeStruct((N, D), f32),
        compiler_params=_params(1),
        name="merge_ln",
    )(h2, *branches, w_branch.astype(MXU), w_out.astype(MXU), x2, g.reshape(1, D), b.reshape(1, D))


ROW_TILE = 8


def _to_token_tiles(ref, val):
    rows = val.shape[0]
    for j in range(ROW_TILE):
        ref[pl.ds(j, rows, stride=ROW_TILE), :] = val[:, j * LANE:(j + 1) * LANE]


def _from_token_tiles(ref, rows, first=0, stride=ROW_TILE):
    return jnp.concatenate([ref[pl.ds(first + j, rows, stride=stride), :] for j in range(ROW_TILE)], axis=1)


def _xattn_kernel(x_ref, wq_ref, k_ref, v_ref, wo_ref, g_ref, b_ref, o_ref, o8_ref):
    x = x_ref[0]
    q = _dot(x.astype(MXU), wq_ref[...]).astype(MXU)
    k = k_ref[0]
    v = v_ref[0]
    heads = []
    for hd in range(X_HEADS):
        sl = slice(hd * X_HEAD_DIM, (hd + 1) * X_HEAD_DIM)
        s = _dot_nt(q[:, sl], k[:, sl]) * (X_HEAD_DIM ** -0.5)
        e = jnp.exp(s - jnp.max(s, axis=1, keepdims=True))
        p = e / jnp.sum(e, axis=1, keepdims=True)
        heads.append(_dot(p.astype(MXU), v[:, sl]).astype(MXU))
    y = _dot(jnp.concatenate(heads, axis=1), wo_ref[...])
    out = _ln(DN_ALPHA * x + y, g_ref[...], b_ref[...])
    o_ref[0] = out
    _to_token_tiles(o8_ref.at[0], out)


def cross_attention_ln(x, mem, wq, wkv, wo, g, b, tq=_TM_ROWS):
    B, S, D = x.shape
    assert D == ROW_TILE * LANE
    M = mem.shape[1]
    F = X_HEADS * X_HEAD_DIM
    kv = matmul(mem.reshape(B * M, D), wkv.astype(MXU), 512, 2 * F, out_dtype=MXU).reshape(B, M, 2 * F)
    full = lambda shape: pl.BlockSpec(shape, lambda bi, i: (0,) * len(shape))
    return pl.pallas_call(
        _xattn_kernel,
        grid=(B, S // tq),
        in_specs=[pl.BlockSpec((1, tq, D), lambda bi, i: (bi, i, 0)), full((D, F)),
                  pl.BlockSpec((1, M, F), lambda bi, i: (bi, 0, 0)), pl.BlockSpec((1, M, F), lambda bi, i: (bi, 0, 1)),
                  full((F, D)), full((1, D)), full((1, D))],
        out_specs=[pl.BlockSpec((1, tq, D), lambda bi, i: (bi, i, 0)),
                   pl.BlockSpec((1, tq * ROW_TILE, LANE), lambda bi, i: (bi, i, 0))],
        out_shape=[jax.ShapeDtypeStruct((B, S, D), f32), jax.ShapeDtypeStruct((B, S * ROW_TILE, LANE), f32)],
        compiler_params=_params(2),
        name="cross_attention_ln",
    )(x, wq.astype(MXU), kv, kv, wo.astype(MXU), g.reshape(1, D), b.reshape(1, D))


_R_E0, _R_E1, _R_W0, _R_W1, _R_RANK0, _R_RANK1 = range(6)
_GRP_LANE0 = N_EXPERTS


def _router_kernel(x_ref, w_ref, b_ref, tri_ref, r_ref, cnt_ref):
    i = pl.program_id(0)
    tm = x_ref.shape[0]
    logits = _dot_nt(w_ref[...], x_ref[...].astype(MXU)) + b_ref[...]
    row = lax.broadcasted_iota(jnp.int32, (LANE, tm), 0)
    row_f = row.astype(f32)
    big = float(LANE)

    def cmax(t):
        return jnp.max(t, axis=0, keepdims=True)

    def first_row(cond):
        return jnp.min(jnp.where(cond, row_f, big), axis=0, keepdims=True)

    def softmax_on(mask):
        lm = jnp.where(mask, logits, NEG)
        e = jnp.where(mask, jnp.exp(lm - cmax(lm)), 0.0)
        return e / jnp.sum(e, axis=0, keepdims=True)

    is_g = (row >= _GRP_LANE0) & (row < _GRP_LANE0 + N_GROUPS)
    p_grp = softmax_on(is_g)
    p_g = cmax(p_grp)
    grp = first_row(is_g & (p_grp == p_g)) - float(_GRP_LANE0)
    grp_of_row = jnp.right_shift(row, EXPERTS_PER_GROUP.bit_length() - 1)
    in_grp = (row < N_EXPERTS) & (grp_of_row.astype(f32) == grp)
    p_e = softmax_on(in_grp)
    p1 = cmax(jnp.where(in_grp, p_e, -1.0))
    e1 = first_row(in_grp & (p_e == p1))
    rest = in_grp & (row_f != e1)
    p2 = cmax(jnp.where(rest, p_e, -1.0))
    e2 = first_row(rest & (p_e == p2))
    w1 = p_g * p1 / (p1 + p2)
    w2 = p_g * p2 / (p1 + p2)

    @pl.when(i == 0)
    def _():
        cnt_ref[...] = jnp.zeros_like(cnt_ref)

    oh1 = row_f == e1
    oh2 = row_f == e2
    both = (oh1 | oh2).astype(MXU)
    before = _dot(both, tri_ref[...]) + cnt_ref[:, 0:1]
    rank1 = jnp.sum(jnp.where(oh1, before, 0.0), axis=0, keepdims=True)
    rank2 = jnp.sum(jnp.where(oh2, before, 0.0), axis=0, keepdims=True)
    cnt_ref[...] = cnt_ref[...] + jnp.sum(both.astype(f32), axis=1, keepdims=True)

    out = jnp.zeros((LANE, tm), f32)
    for slot, val in ((_R_E0, e1), (_R_E1, e2), (_R_W0, w1), (_R_W1, w2), (_R_RANK0, rank1), (_R_RANK1, rank2)):
        out = jnp.where(row == slot, val, out)
    r_ref[...] = out.T


def moe_router(x2, rg_w, rg_b, re_w, re_b, tm=_TM_ROWS):
    N, D = x2.shape
    assert EXPERTS_PER_GROUP & (EXPERTS_PER_GROUP - 1) == 0 and N_EXPERTS + N_GROUPS <= LANE
    w = jnp.pad(jnp.concatenate([re_w, rg_w], axis=1), ((0, 0), (0, LANE - N_EXPERTS - N_GROUPS))).T.astype(MXU)
    b = jnp.pad(jnp.concatenate([re_b, rg_b]), (0, LANE - N_EXPERTS - N_GROUPS)).reshape(LANE, 1)
    tri = jnp.asarray(np.triu(np.ones((tm, tm), np.float32), 1), MXU)
    return pl.pallas_call(
        _router_kernel,
        grid=(N // tm,),
        in_specs=[pl.BlockSpec((tm, D), lambda i: (i, 0)), pl.BlockSpec((LANE, D), lambda i: (0, 0)),
                  pl.BlockSpec((LANE, 1), lambda i: (0, 0)), pl.BlockSpec((tm, tm), lambda i: (0, 0))],
        out_specs=[pl.BlockSpec((tm, LANE), lambda i: (i, 0)), pl.BlockSpec((LANE, LANE), lambda i: (0, 0))],
        out_shape=[jax.ShapeDtypeStruct((N, LANE), f32), jax.ShapeDtypeStruct((LANE, LANE), f32)],
        compiler_params=_params(1),
        name="moe_router",
    )(x2, w, b, tri)


def _ffn_kernel(ce_ref, first_ref, slot_ref, next_ref, nu_ref, x_ref, wgu_hbm, wd_hbm, o_ref,
                wgu_f32, wd_f32, wgu_b, wd_b, sems, *, layer):
    c = pl.program_id(0)
    used = c < nu_ref[0]

    def fetch(e, slot):
        return (pltpu.make_async_copy(wgu_hbm.at[layer, e], wgu_f32.at[slot], sems.at[slot, 0]),
                pltpu.make_async_copy(wd_hbm.at[layer, e], wd_f32.at[slot], sems.at[slot, 1]))

    @pl.when(used & (c == 0))
    def _():
        for cp in fetch(ce_ref[0], 0):
            cp.start()

    @pl.when(used & (first_ref[c] == 1))
    def _():
        slot = slot_ref[c]
        for cp in fetch(ce_ref[c], slot):
            cp.wait()

        @pl.when(next_ref[c] >= 0)
        def _():
            for cp in fetch(next_ref[c], 1 - slot):
                cp.start()

        wgu_b[...] = wgu_f32[slot].astype(MXU)
        wd_b[...] = wd_f32[slot].astype(MXU)

    @pl.when(used)
    def _():
        x = _from_token_tiles(x_ref, EXPERT_CHUNK)
        gu = _dot(x.astype(MXU), wgu_b[...])
        hid = jax.nn.silu(gu[:, :D_EXPERT]) * gu[:, D_EXPERT:]
        _to_token_tiles(o_ref, _dot(hid.astype(MXU), wd_b[...]))

    @pl.when(jnp.logical_not(used))
    def _():
        o_ref[...] = jnp.zeros_like(o_ref)


def expert_ffn(xb8, chunk_e, n_used, w_gu, w_down, layer):
    C = EXPERT_CHUNK
    n_chunks = xb8.shape[0] // (C * ROW_TILE)
    D = ROW_TILE * LANE
    idx = jnp.arange(n_chunks, dtype=jnp.int32)
    used = idx < n_used[0]
    first = used & ((idx == 0) | (chunk_e != jnp.roll(chunk_e, 1)))
    slot = (jnp.cumsum(first.astype(jnp.int32)) - 1) % 2
    none = jnp.int32(N_EXPERTS)
    later = jnp.roll(jnp.where(first, chunk_e, none), -1).at[-1].set(none)
    next_e = lax.cummin(later, axis=0, reverse=True)
    next_e = jnp.where(next_e == none, -1, next_e)
    tile_spec = pl.BlockSpec((C * ROW_TILE, LANE), lambda c, *_: (c, 0))
    any_spec = pl.BlockSpec(memory_space=pl.ANY)
    grid_spec = pltpu.PrefetchScalarGridSpec(
        num_scalar_prefetch=5,
        grid=(n_chunks,),
        in_specs=[tile_spec, any_spec, any_spec],
        out_specs=tile_spec,
        scratch_shapes=[pltpu.VMEM((2, D, 2 * D_EXPERT), w_gu.dtype), pltpu.VMEM((2, D_EXPERT, D), w_down.dtype),
                        pltpu.VMEM((D, 2 * D_EXPERT), MXU), pltpu.VMEM((D_EXPERT, D), MXU),
                        pltpu.SemaphoreType.DMA((2, 2))],
    )
    return pl.pallas_call(
        functools.partial(_ffn_kernel, layer=layer),
        grid_spec=grid_spec,
        out_shape=jax.ShapeDtypeStruct(xb8.shape, f32),
        compiler_params=_params(1),
        name="expert_ffn",
    )(chunk_e, first.astype(jnp.int32), slot.astype(jnp.int32), next_e.astype(jnp.int32), n_used, xb8, w_gu, w_down)


_COPY_WINDOW = 512


def _windowed_copies(n, copy, wait):
    assert _COPY_WINDOW & (_COPY_WINDOW - 1) == 0 and n >= _COPY_WINDOW

    def fill(j, carry):
        copy(j, j).start()
        return carry

    def steady(j, carry):
        s = jnp.bitwise_and(j, _COPY_WINDOW - 1)
        wait(s)
        copy(j, s).start()
        return carry

    def drain(s, carry):
        wait(s)
        return carry

    lax.fori_loop(0, _COPY_WINDOW, fill, 0, unroll=8)
    lax.fori_loop(_COPY_WINDOW, n, steady, 0, unroll=8)
    lax.fori_loop(0, _COPY_WINDOW, drain, 0, unroll=8)


def _dispatch_kernel(dest_ref, x8_ref, init_hbm, xb_hbm, sems, *, tm):
    n = tm * TOPK_IN_GROUP
    base = pl.program_id(0) * n

    def copy(j, s):
        t = pl.multiple_of(jnp.right_shift(j, 1) * ROW_TILE, ROW_TILE)
        d = pl.multiple_of(dest_ref[base + j] * ROW_TILE, ROW_TILE)
        return pltpu.make_async_copy(x8_ref.at[pl.ds(t, ROW_TILE)], xb_hbm.at[pl.ds(d, ROW_TILE)], sems.at[s])

    def wait(s):
        pltpu.make_async_copy(x8_ref.at[pl.ds(0, ROW_TILE)], xb_hbm.at[pl.ds(0, ROW_TILE)], sems.at[s]).wait()

    _windowed_copies(n, copy, wait)


def moe_dispatch(x8, dest, n_slots, tm=_TM_DISPATCH):
    n_tok = x8.shape[0] // ROW_TILE
    assert n_tok % tm == 0 and tm * TOPK_IN_GROUP >= _COPY_WINDOW
    grid_spec = pltpu.PrefetchScalarGridSpec(
        num_scalar_prefetch=1,
        grid=(n_tok // tm,),
        in_specs=[pl.BlockSpec((tm * ROW_TILE, LANE), lambda i, dest: (i, 0)), pl.BlockSpec(memory_space=pl.ANY)],
        out_specs=pl.BlockSpec(memory_space=pl.ANY),
        scratch_shapes=[pltpu.SemaphoreType.DMA((_COPY_WINDOW,))],
    )
    return pl.pallas_call(
        functools.partial(_dispatch_kernel, tm=tm),
        grid_spec=grid_spec,
        out_shape=jax.ShapeDtypeStruct((n_slots * ROW_TILE, LANE), x8.dtype),
        input_output_aliases={2: 0},
        compiler_params=_params(1),
        name="moe_dispatch",
    )(dest, x8, jnp.zeros((n_slots * ROW_TILE, LANE), x8.dtype))


def _combine_kernel(dest_ref, x_ref, r_ref, g_ref, b_ref, yb_hbm, o_ref, buf, sems):
    tm = x_ref.shape[0]
    n = tm * TOPK_IN_GROUP
    i = pl.program_id(0)

    def issue(tile):
        dst_buf = buf.at[jnp.bitwise_and(tile, 1)]

        def body(j, carry):
            src = pl.multiple_of(dest_ref[tile * n + j] * ROW_TILE, ROW_TILE)
            dst = pl.multiple_of(j * ROW_TILE, ROW_TILE)
            pltpu.make_async_copy(yb_hbm.at[pl.ds(src, ROW_TILE)], dst_buf.at[pl.ds(dst, ROW_TILE)],
                                  sems.at[j]).start()
            return carry

        lax.fori_loop(0, n, body, 0, unroll=8)

    def drain(j, carry):
        pltpu.make_async_copy(yb_hbm.at[pl.ds(0, ROW_TILE)], buf.at[0, pl.ds(0, ROW_TILE)], sems.at[j]).wait()
        return carry

    @pl.when(i == 0)
    def _():
        issue(i)

    lax.fori_loop(0, n, drain, 0, unroll=8)

    @pl.when(i + 1 < pl.num_programs(0))
    def _():
        issue(i + 1)

    cur = buf.at[jnp.bitwise_and(i, 1)]
    r = r_ref[...]
    lane = lax.broadcasted_iota(jnp.int32, r.shape, 1)
    w0 = jnp.sum(jnp.where(lane == _R_W0, r, 0.0), axis=1, keepdims=True)
    w1 = jnp.sum(jnp.where(lane == _R_W1, r, 0.0), axis=1, keepdims=True)
    pair = TOPK_IN_GROUP * ROW_TILE
    y = _from_token_tiles(cur, tm, 0, pair) * w0 + _from_token_tiles(cur, tm, ROW_TILE, pair) * w1
    o_ref[...] = _ln(DN_ALPHA * x_ref[...] + y, g_ref[...], b_ref[...])


def combine_ln(x2, yb8, dest, r, g, b, tm=_TM_ROWS):
    N, D = x2.shape
    assert N % tm == 0
    row = lambda w: pl.BlockSpec((tm, w), lambda i, dest: (i, 0))
    vec = pl.BlockSpec((1, D), lambda i, dest: (0, 0))
    grid_spec = pltpu.PrefetchScalarGridSpec(
        num_scalar_prefetch=1,
        grid=(N // tm,),
        in_specs=[row(D), row(LANE), vec, vec, pl.BlockSpec(memory_space=pl.ANY)],
        out_specs=row(D),
        scratch_shapes=[pltpu.VMEM((2, tm * TOPK_IN_GROUP * ROW_TILE, LANE), f32),
                        pltpu.SemaphoreType.DMA((tm * TOPK_IN_GROUP,))],
    )
    return pl.pallas_call(
        _combine_kernel,
        grid_spec=grid_spec,
        out_shape=jax.ShapeDtypeStruct((N, D), f32),
        compiler_params=_params(1),
        name="moe_combine_ln",
    )(dest, x2, r, g.reshape(1, D), b.reshape(1, D), yb8)


def hier_moe_ln(x2, x8, rg_w, rg_b, re_w, re_b, w_gu, w_down, layer, g, b):
    N, D = x2.shape
    E, C, K = N_EXPERTS, EXPERT_CHUNK, TOPK_IN_GROUP
    A = N * K
    r, cnt = moe_router(x2, rg_w, rg_b, re_w, re_b)
    e = r[:, _R_E0:_R_E1 + 1].astype(jnp.int32)
    rank = r[:, _R_RANK0:_R_RANK1 + 1].astype(jnp.int32)
    counts = cnt[:E, 0].astype(jnp.int32)
    padded = (counts + C - 1) // C * C
    pad_end = jnp.cumsum(padded)
    pad_start = pad_end - padded
    dest = pad_start[e] + rank
    n_chunks = -(-(A + E * (C - 1)) // C)
    P = n_chunks * C
    chunk_start = jnp.arange(n_chunks, dtype=jnp.int32) * C
    chunk_e = jnp.minimum(jnp.sum((pad_end[None, :] <= chunk_start[:, None]).astype(jnp.int32), axis=1), E - 1)
    n_used = (pad_end[-1] // C).reshape(1).astype(jnp.int32)
    dest = dest.reshape(A).astype(jnp.int32)
    xb8 = moe_dispatch(x8, dest, P)
    yb8 = expert_ffn(xb8, chunk_e.astype(jnp.int32), n_used, w_gu, w_down, layer)
    return combine_ln(x2, yb8, dest, r, g, b)


def kernel(x, mem, w_in, nsa_cmp_pos, nsa_cmp_w1, nsa_cmp_w2, rnn_conv_w, rnn_conv_b, rnn_ga_w, rnn_ga_b,
           rnn_gx_w, rnn_gx_b, rnn_lambda, mla_q_norm, mla_kv_norm, mla_w_uq, mla_w_ukv, w_branch, w_out,
           ln1_g, ln1_b, x_wq, x_wkv, x_wo, ln2_g, ln2_b, moe_rg_w, moe_rg_b, moe_re_w, moe_re_b,
           moe_w_gu, moe_w_down, ln3_g, ln3_b):
    B, S, D = x.shape
    N = B * S
    x2 = x.reshape(N, D)
    for l in range(DEPTH):
        h2 = matmul(x2, _cat_w_in(w_in[l]), 512, _IN_TILE)
        h = h2.reshape(B, S, N_CAT)
        o_a = nsa_attention(h, nsa_cmp_pos[l], nsa_cmp_w1[l], nsa_cmp_w2[l])
        o_b = stick_breaking_attention(h)
        o_c = rglru_block(h, rnn_conv_w[l], rnn_conv_b[l], rnn_ga_w[l], rnn_ga_b[l], rnn_gx_w[l], rnn_gx_b[l],
                          rnn_lambda[l])
        o_d = mla_attention(h, mla_q_norm[l], mla_kv_norm[l], mla_w_uq[l], mla_w_ukv[l])
        branches = [o.reshape(N, MIX_W) for o in (o_a, o_b, o_c, o_d)]
        x2 = merge_ln(h2, branches, w_branch[l], w_out[l], x2, ln1_g[l], ln1_b[l])
        x3, x8 = cross_attention_ln(x2.reshape(B, S, D), mem, x_wq[l], x_wkv[l], x_wo[l], ln2_g[l], ln2_b[l])
        x2 = hier_moe_ln(x3.reshape(N, D), x8.reshape(N * ROW_TILE, LANE), moe_rg_w[l], moe_rg_b[l], moe_re_w[l],
                         moe_re_b[l], moe_w_gu, moe_w_down, l, ln3_g[l], ln3_b[l])
    return x2.reshape(B, S, D)
```

```python
import functools

import numpy as np
import jax
import jax.numpy as jnp
from jax import lax
from jax.experimental import pallas as pl
from jax.experimental.pallas import tpu as pltpu

D_MODEL = 1024
DEPTH = 4
HEAD_DIM = 64
N_MIXERS = 4
MIX_W = 256
NSA_HEADS = 4
NSA_KV_HEADS = 2
NSA_GROUP = NSA_HEADS // NSA_KV_HEADS
CMP_LEN = 32
CMP_STRIDE = 16
CMP_HID = 256
SEL_BLOCK = 64
SEL_TOPK = 8
WINDOW = 512
FORCE_SCORE = 1e4
SB_HEADS = 4
RNN_W = 256
CONV_W = 4
LRU_C = 8.0
MLA_HEADS = 4
MLA_Q_RANK = 192
MLA_KV_RANK = 128
MLA_NOPE = 64
MLA_ROPE = 32
MLA_V = 64
ROPE_THETA = 10000.0
X_HEADS = 4
X_HEAD_DIM = 128
N_GROUPS = 4
EXPERTS_PER_GROUP = 8
N_EXPERTS = N_GROUPS * EXPERTS_PER_GROUP
TOPK_IN_GROUP = 2
D_EXPERT = 512
EXPERT_CHUNK = 256
DN_ALPHA = (2.0 * DEPTH) ** 0.25
LN_EPS = 1e-5
RMS_EPS = 1e-6

IN_SPLITS = ((NSA_HEADS * HEAD_DIM,) + (NSA_KV_HEADS * HEAD_DIM,) * 6 + (NSA_HEADS * 3,)
             + (SB_HEADS * HEAD_DIM,) * 3
             + (RNN_W, RNN_W)
             + (MLA_Q_RANK, MLA_KV_RANK, MLA_ROPE)
             + (N_MIXERS * D_MODEL,))
IN_OFFSETS = tuple(int(o) for o in np.concatenate([[0], np.cumsum(IN_SPLITS)[:-1]]))

LANE = 128
VMEM_LIMIT = 48 * 1024 * 1024
NEG = -1e30
BIG_NEG = -2.0 ** 100

f32 = jnp.float32
MXU = jnp.bfloat16

_CQ_PAD = 2 * LANE
_W_Q, _W_KV, _W_KVT = NSA_HEADS * HEAD_DIM, NSA_KV_HEADS * HEAD_DIM, NSA_KV_HEADS * LANE
_SECTIONS = (("mg", N_MIXERS * D_MODEL), ("aq", _W_Q), ("kvs", _W_KVT), ("kvw", _W_KVT),
             ("bq", SB_HEADS * HEAD_DIM), ("bk", SB_HEADS * HEAD_DIM), ("bv", SB_HEADS * HEAD_DIM),
             ("cx", RNN_W), ("cg", RNN_W), ("dcq", _CQ_PAD), ("kc", _W_KV), ("vc", _W_KV),
             ("ag", LANE), ("dckv", MLA_KV_RANK), ("dkr", LANE), ("dkrs", LANE))
_IN_TILE = 3584
OFF = {}
_o = 0
for _n, _w in _SECTIONS:
    OFF[_n] = _o
    _o += _w
N_CAT = -(-_o // _IN_TILE) * _IN_TILE

_SEL_ROWS = 32
_AUG_SEL = HEAD_DIM
_AUG_POS = HEAD_DIM + _SEL_ROWS
_POS_SPLIT = 256

_TQ_NSA = 512
_TQ_MLA = 512
_TQ_SB = 256
_TM_ROWS = 256
_TM_DENSE = 512
_TM_DISPATCH = 256
_TR_MLA_PREP = 512


def _rot_half(t):
    d = t.shape[-1]
    return jnp.concatenate([-t[..., d // 2:], t[..., :d // 2]], axis=-1)


def _cat_w_in(w):
    def sec(i):
        return w[:, IN_OFFSETS[i]:IN_OFFSETS[i] + IN_SPLITS[i]]

    def pair(k, v):
        return jnp.concatenate([k[:, :HEAD_DIM], v[:, :HEAD_DIM], k[:, HEAD_DIM:], v[:, HEAD_DIM:]], axis=1)

    def padc(t, n):
        return jnp.pad(t, ((0, 0), (0, n - t.shape[1])))

    def rope_slot(t):
        return jnp.pad(t, ((0, 0), (MLA_NOPE, LANE - MLA_NOPE - MLA_ROPE)))

    parts = {"mg": sec(16), "aq": sec(0), "kc": sec(1), "vc": sec(2), "kvs": pair(sec(3), sec(4)),
             "kvw": pair(sec(5), sec(6)), "ag": padc(sec(7), LANE), "bq": sec(8), "bk": sec(9),
             "bv": sec(10), "cx": sec(11), "cg": sec(12), "dcq": padc(sec(13), _CQ_PAD), "dckv": sec(14),
             "dkr": rope_slot(sec(15)), "dkrs": rope_slot(_rot_half(sec(15)))}
    cat = jnp.concatenate([parts[n] for n, _ in _SECTIONS], axis=1)
    return padc(cat, N_CAT).astype(MXU)


def _dot(a, b):
    return jnp.dot(a, b, preferred_element_type=f32)


def _dot_nt(a, b):
    return lax.dot_general(a, b, (((1,), (1,)), ((), ())), preferred_element_type=f32)


def _params(n_axes):
    return pltpu.CompilerParams(dimension_semantics=("arbitrary",) * n_axes, vmem_limit_bytes=VMEM_LIMIT)


def _mm_kernel(a_ref, b_ref, o_ref):
    o_ref[...] = _dot(a_ref[...].astype(MXU), b_ref[...]).astype(o_ref.dtype)


def matmul(a, b, tm, tn, out_dtype=f32):
    M, K = a.shape
    _, N = b.shape
    assert M % tm == 0 and N % tn == 0
    return pl.pallas_call(
        _mm_kernel,
        grid=(N // tn, M // tm),
        in_specs=[pl.BlockSpec((tm, K), lambda j, i: (i, 0)),
                  pl.BlockSpec((K, tn), lambda j, i: (0, j))],
        out_specs=pl.BlockSpec((tm, tn), lambda j, i: (i, j)),
        out_shape=jax.ShapeDtypeStruct((M, N), out_dtype),
        compiler_params=_params(2),
        name="matmul",
    )(a, b)


def _cmp_kernel(t_ref, w1_ref, pos_ref, w2_ref, o_ref):
    nc = t_ref.shape[1] // CMP_STRIDE
    t = jnp.concatenate([t_ref[0, pl.ds(l, nc, stride=CMP_STRIDE), :] for l in range(CMP_STRIDE)], axis=1)
    half = t.shape[1]
    t = t.astype(MXU)
    y1 = _dot(t, w1_ref[0, :half, :])
    y2 = _dot(t, w1_ref[0, half:, :])
    pos = jnp.broadcast_to(pos_ref[0], (8, 2 * half)).astype(MXU)
    pc = _dot(pos, w1_ref[0])[0:1]
    nc = y2.shape[0]
    hid = y1 + pltpu.roll(y2, nc - 1, 0) + pc
    o_ref[0, 0] = _dot(jax.nn.gelu(hid).astype(MXU), w2_ref[0])


def nsa_compress(h, cmp_w1, cmp_pos, cmp_w2):
    B, S, _ = h.shape
    G = NSA_KV_HEADS
    NC = S // CMP_STRIDE
    F = CMP_STRIDE * G * HEAD_DIM
    assert G * HEAD_DIM == LANE and OFF["vc"] == OFF["kc"] + LANE
    col0 = OFF["kc"] // LANE
    eye = jnp.eye(G, dtype=f32)
    w1 = jnp.einsum('jldh,gk->jlgdkh', cmp_w1.reshape(2, CMP_LEN, HEAD_DIM, CMP_HID), eye)
    w1 = w1.reshape(2, 2 * F, G * CMP_HID).astype(MXU)
    pos = jnp.broadcast_to(cmp_pos[:, :, None, :], (2, CMP_LEN, G, HEAD_DIM)).reshape(2, 1, 2 * F)
    w2 = jnp.einsum('jhc,gk->jghkc', jnp.concatenate([cmp_w2, cmp_w2], axis=-1), eye)
    w2 = w2.reshape(2, G * CMP_HID, G * LANE).astype(MXU)
    return pl.pallas_call(
        _cmp_kernel,
        grid=(2, B),
        in_specs=[pl.BlockSpec((1, S, LANE), lambda j, i: (i, 0, col0 + j)),
                  pl.BlockSpec((1, 2 * F, G * CMP_HID), lambda j, i: (j, 0, 0)),
                  pl.BlockSpec((1, 1, 2 * F), lambda j, i: (j, 0, 0)),
                  pl.BlockSpec((1, G * CMP_HID, G * LANE), lambda j, i: (j, 0, 0))],
        out_specs=pl.BlockSpec((1, 1, NC, G * LANE), lambda j, i: (j, i, 0, 0)),
        out_shape=jax.ShapeDtypeStruct((2, B, NC, G * LANE), f32),
        compiler_params=_params(2),
        name="nsa_compress",
    )(h, w1, pos, w2)


def _nsa_kernel(q_ref, kc_ref, vc_ref, kvs_ref, kvw_ref, gl_ref, cover_ref, aug_ref, cb_ref, wb_ref,
                o_ref, ksa, vsa, kwa, vwa, *, tq, n_cmp, n_sel, n_top):
    tk = tq
    g = pl.program_id(1)
    qi = pl.program_id(2)
    q0 = pl.multiple_of(qi * tq, tq)
    lane = lax.broadcasted_iota(jnp.int32, (tq, LANE), 1)
    lo_half = lane < HEAD_DIM

    @pl.when(qi == 0)
    def _():
        real = lax.broadcasted_iota(jnp.int32, ksa.shape, 1) < HEAD_DIM
        aug = aug_ref[...]
        ones = jnp.ones(ksa.shape, MXU)
        for kv_ref, k_out, v_out in ((kvs_ref, ksa, vsa), (kvw_ref, kwa, vwa)):
            kv = kv_ref[0]
            k_out[...] = jnp.where(real, kv.astype(MXU), aug)
            v_out[...] = jnp.where(real, pltpu.roll(kv, HEAD_DIM, 1).astype(MXU), ones)

    q = q_ref[0] * (HEAD_DIM ** -0.5)
    q_heads = (jnp.where(lo_half, q, 0.0), jnp.where(lo_half, pltpu.roll(q, HEAD_DIM, 1), 0.0))
    alibi = [2.0 ** (-8.0 * (h + 1) / NSA_HEADS) for h in range(NSA_HEADS)]
    slopes = [jnp.where(g == 0, alibi[n], alibi[NSA_GROUP + n]) for n in range(NSA_GROUP)]
    pos_cols = [jnp.where(lane == _AUG_POS, slopes[n] * _POS_SPLIT, jnp.where(lane == _AUG_POS + 1, slopes[n], 0.0))
                for n in range(NSA_GROUP)]
    tpos = q0 + lax.broadcasted_iota(jnp.int32, (tq, 1), 0)

    nc = kc_ref.shape[2]
    cidx = lax.broadcasted_iota(jnp.int32, (1, nc), 1)
    dist_c = tpos - (cidx * CMP_STRIDE + (CMP_LEN - 1))
    mask_c = (dist_c >= 0) & (cidx < n_cmp)
    dist_cf = dist_c.astype(f32)
    kc = kc_ref[0, 0].astype(MXU)
    vc = vc_ref[0, 0].astype(MXU)
    o_cmp = []
    imp_t = jnp.zeros((LANE, tq), f32)
    for n in range(NSA_GROUP):
        s = _dot_nt(q_heads[n].astype(MXU), kc) - slopes[n] * dist_cf
        sm = jnp.where(mask_c, s, NEG)
        m = jnp.max(sm, axis=1, keepdims=True)
        p = jnp.where(mask_c, jnp.exp(sm - m), 0.0)
        p = (p / jnp.maximum(jnp.sum(p, axis=1, keepdims=True), 1e-30)).astype(MXU)
        o_cmp.append(_dot(p, vc))
        imp_t = imp_t + _dot_nt(cover_ref[...], p)

    rows = _SEL_ROWS
    imp = imp_t[:rows]
    blk = lax.broadcasted_iota(jnp.int32, (rows, tq), 0)
    blk_f = blk.astype(f32)
    tpos_t = q0 + lax.broadcasted_iota(jnp.int32, (1, tq), 1)
    forced = (blk == 0) | (blk == jnp.right_shift(tpos_t, SEL_BLOCK.bit_length() - 1))
    valid = blk * SEL_BLOCK <= tpos_t
    imp = jnp.where(forced, FORCE_SCORE, jnp.where(valid, imp, -1.0))
    imp = jnp.where(blk < n_sel, imp, NEG)
    sel_t = jnp.zeros((rows, tq), f32)
    for _ in range(n_top):
        m = jnp.max(imp, axis=0, keepdims=True)
        first = jnp.min(jnp.where(imp == m, blk_f, float(LANE)), axis=0, keepdims=True)
        pick = blk_f == first
        sel_t = jnp.where(pick, 1.0, sel_t)
        imp = jnp.where(pick, 2 * NEG, imp)
    sel = jnp.concatenate([sel_t, jnp.zeros((LANE - rows, tq), f32)], axis=0).T
    sel_bias = pltpu.roll(jnp.where(sel > 0.5, 0.0, BIG_NEG), _AUG_SEL, 1)
    sel_cols = jnp.where((lane >= _AUG_SEL) & (lane < _AUG_SEL + rows), sel_bias, 0.0)
    q_sel = jnp.concatenate([q_heads[n] + sel_cols + pos_cols[n] for n in range(NSA_GROUP)], axis=0).astype(MXU)
    q_win = jnp.concatenate([q_heads[n] + pos_cols[n] for n in range(NSA_GROUP)], axis=0).astype(MXU)
    rows_q = NSA_GROUP * tq
    denom_lane = lax.broadcasted_iota(jnp.int32, (rows_q, LANE), 1) == HEAD_DIM

    def update(carry, qs, k, v, bias):
        m, acc = carry
        s = _dot_nt(qs, k)
        if bias is not None:
            s = s + bias
        m_new = jnp.maximum(m, jnp.max(s, axis=1, keepdims=True))
        p = jnp.exp(s - m_new)
        return m_new, jnp.exp(m - m_new) * acc + _dot(p.astype(MXU), v)

    def finish(carry):
        _, acc = carry
        denom = jnp.sum(jnp.where(denom_lane, acc, 0.0), axis=1, keepdims=True)
        o = acc / jnp.maximum(denom, 1e-30)
        return [o[n * tq:(n + 1) * tq] for n in range(NSA_GROUP)]

    init = (jnp.full((rows_q, 1), NEG, f32), jnp.zeros((rows_q, LANE), f32))
    causal = cb_ref[...]

    def sel_body(kt, carry):
        k0 = pl.multiple_of(kt * tk, tk)
        return update(carry, q_sel, ksa[pl.ds(k0, tk), :], vsa[pl.ds(k0, tk), :], None)

    carry = lax.fori_loop(0, qi, sel_body, init)
    o_sel = finish(update(carry, q_sel, ksa[pl.ds(q0, tk), :], vsa[pl.ds(q0, tk), :], causal))

    carry = init
    n_back = WINDOW // tk
    for back in range(n_back, -1, -1):
        k0 = pl.multiple_of(jnp.maximum(qi - back, 0) * tk, tk)
        if back:
            off = jnp.where(qi >= back, 0.0, BIG_NEG)
            bias = wb_ref[...] + off if back == n_back else off
        else:
            bias = causal
        carry = update(carry, q_win, kwa[pl.ds(k0, tk), :], vwa[pl.ds(k0, tk), :], bias)
    o_win = finish(carry)

    sig = jax.nn.sigmoid(gl_ref[0])

    def gate(n, j):
        col = 3 * (NSA_GROUP * g + n) + j
        return jnp.sum(jnp.where(lane == col, sig, 0.0), axis=1, keepdims=True)

    o = [gate(n, 0) * o_cmp[n] + gate(n, 1) * o_sel[n] + gate(n, 2) * o_win[n] for n in range(NSA_GROUP)]
    o_ref[0] = jnp.where(lo_half, o[0], pltpu.roll(o[1], HEAD_DIM, 1))


def nsa_attention(h, cmp_pos, cmp_w1, cmp_w2, tq=_TQ_NSA):
    B, S, _ = h.shape
    G = NSA_KV_HEADS
    NC = S // CMP_STRIDE
    n_cmp = (S - CMP_LEN) // CMP_STRIDE + 1
    n_sel = S // SEL_BLOCK
    n_top = min(SEL_TOPK, n_sel)
    F = CMP_STRIDE * HEAD_DIM
    assert S % tq == 0 and WINDOW % tq == 0 and tq % SEL_BLOCK == 0 and n_sel <= _SEL_ROWS and S <= _POS_SPLIT ** 2

    kvc = nsa_compress(h, cmp_w1, cmp_pos, cmp_w2)

    c0 = np.arange(n_cmp)[:, None] * CMP_STRIDE
    j0 = np.arange(n_sel)[None, :] * SEL_BLOCK
    cover = np.clip(np.minimum(c0 + CMP_LEN, j0 + SEL_BLOCK) - np.maximum(c0, j0), 0, None) / CMP_LEN
    cover_t = np.zeros((LANE, NC), np.float32)
    cover_t[:n_sel, :n_cmp] = cover.T
    pos = np.arange(S)
    aug = np.zeros((S, LANE), np.float32)
    aug[pos, _AUG_SEL + pos // SEL_BLOCK] = 1.0
    aug[:, _AUG_POS] = pos // _POS_SPLIT
    aug[:, _AUG_POS + 1] = pos % _POS_SPLIT
    rel = np.arange(tq)[:, None] - np.arange(tq)[None, :]
    rel = np.tile(rel, (NSA_GROUP, 1))
    causal = np.where(rel >= 0, 0.0, BIG_NEG).astype(np.float32)
    win_lo = np.where(rel < 0, 0.0, BIG_NEG).astype(np.float32)

    col = lambda name: OFF[name] // LANE
    full2 = lambda shape: pl.BlockSpec(shape, lambda b, g, i: (0, 0))
    q_spec = pl.BlockSpec((1, tq, LANE), lambda b, g, i: (b, i, col("aq") + g))
    c_specs = [pl.BlockSpec((1, 1, NC, LANE), lambda b, g, i, j=j: (j, b, 0, g)) for j in range(2)]
    kv_specs = [pl.BlockSpec((1, S, LANE), lambda b, g, i, c=col(n): (b, 0, c + g))
                for n in ("kvs", "kvw")]
    gl_spec = pl.BlockSpec((1, tq, LANE), lambda b, g, i: (b, i, col("ag")))
    return pl.pallas_call(
        functools.partial(_nsa_kernel, tq=tq, n_cmp=n_cmp, n_sel=n_sel, n_top=n_top),
        grid=(B, G, S // tq),
        in_specs=[q_spec] + c_specs + kv_specs + [gl_spec, full2((LANE, NC)), full2((S, LANE)),
                                                  full2((NSA_GROUP * tq, tq)), full2((NSA_GROUP * tq, tq))],
        out_specs=pl.BlockSpec((1, tq, LANE), lambda b, g, i: (b, i, g)),
        out_shape=jax.ShapeDtypeStruct((B, S, NSA_HEADS * HEAD_DIM), f32),
        scratch_shapes=[pltpu.VMEM((S, LANE), MXU)] * 4,
        compiler_params=_params(3),
        name="nsa_attention",
    )(h, kvc, kvc, h, h, h, jnp.asarray(cover_t, MXU), jnp.asarray(aug, MXU), jnp.asarray(causal),
      jnp.asarray(win_lo))


_SB_DEAD = -104.0


def _log_sigmoid(z):
    return jnp.minimum(z, 0.0) - jnp.log1p(jnp.exp(-jnp.abs(z)))


def _sb_kernel(q_ref, k_ref, v_ref, u_ref, o_ref, kb, vb, *, tq):
    tk = tq
    qi = pl.program_id(2)
    q0 = pl.multiple_of(qi * tq, tq)
    lane = lax.broadcasted_iota(jnp.int32, (tq, LANE), 1)
    lo_half = lane < HEAD_DIM

    @pl.when(qi == 0)
    def _():
        kb[...] = k_ref[0].astype(MXU)
        vb[...] = v_ref[0].astype(MXU)

    q = q_ref[0] * (HEAD_DIM ** -0.5)
    q2 = jnp.concatenate([jnp.where(lo_half, q, 0.0), jnp.where(lo_half, 0.0, q)], axis=0).astype(MXU)
    u = u_ref[...]

    def tile(carry, k, v, strict):
        c, acc = carry
        z = _dot_nt(q2, k)
        ls = _log_sigmoid(z)
        log_1m = ls - z
        if strict is not None:
            log_1m = jnp.where(strict, log_1m, 0.0)
        hi = log_1m.astype(MXU)
        lo = (log_1m - hi.astype(f32)).astype(MXU)
        tail = _dot(hi, u) + _dot(lo, u) + c
        a = jnp.exp(ls + tail)
        if strict is not None:
            a = jnp.where(strict, a, 0.0)
        return c + jnp.sum(log_1m, axis=1, keepdims=True), acc + _dot(a.astype(MXU), v)

    row = lax.broadcasted_iota(jnp.int32, (2 * tq, tk), 0)
    rel = jnp.where(row >= tq, row - tq, row) - lax.broadcasted_iota(jnp.int32, (2 * tq, tk), 1)
    zero = (jnp.zeros((2 * tq, 1), f32), jnp.zeros((2 * tq, LANE), f32))
    state = tile(zero, kb[pl.ds(q0, tk), :], vb[pl.ds(q0, tk), :], rel > 0)

    def alive(state):
        return (jnp.max(state[0]) > _SB_DEAD).astype(jnp.int32)

    def cond(loop):
        kt, live, _ = loop
        return (kt >= 0) & (live > 0)

    def body(loop):
        kt, _, state = loop
        k0 = pl.multiple_of(kt * tk, tk)
        state = tile(state, kb[pl.ds(k0, tk), :], vb[pl.ds(k0, tk), :], None)
        return kt - 1, alive(state), state

    _, _, (_, acc) = lax.while_loop(cond, body, (qi - 1, alive(state), state))
    o_ref[0] = jnp.where(lo_half, acc[:tq], acc[tq:])


def stick_breaking_attention(h, tq=_TQ_SB):
    B, S, _ = h.shape
    assert S % tq == 0
    tri = jnp.asarray(np.tril(np.ones((tq, tq), np.float32), -1), MXU)
    cq, ck, cv = (OFF[n] // LANE for n in ("bq", "bk", "bv"))
    return pl.pallas_call(
        functools.partial(_sb_kernel, tq=tq),
        grid=(B, SB_HEADS // 2, S // tq),
        in_specs=[pl.BlockSpec((1, tq, LANE), lambda b, p, i: (b, i, cq + p)),
                  pl.BlockSpec((1, S, LANE), lambda b, p, i: (b, 0, ck + p)),
                  pl.BlockSpec((1, S, LANE), lambda b, p, i: (b, 0, cv + p)),
                  pl.BlockSpec((tq, tq), lambda b, p, i: (0, 0))],
        out_specs=pl.BlockSpec((1, tq, LANE), lambda b, p, i: (b, i, p)),
        out_shape=jax.ShapeDtypeStruct((B, S, SB_HEADS * HEAD_DIM), f32),
        scratch_shapes=[pltpu.VMEM((S, LANE), MXU)] * 2,
        compiler_params=_params(3),
        name="sb_attention",
    )(h, h, h, tri)


def _neg_expm1(y):
    series = -y * (1.0 + y * (1.0 / 2 + y * (1.0 / 6 + y * (1.0 / 24 + y * (1.0 / 120)))))
    return jnp.where(y > -0.1, series, 1.0 - jnp.exp(y))


def _rglru_kernel(x_ref, xg_ref, cw_ref, cb_ref, gaw_ref, gab_ref, gxw_ref, gxb_ref, lam_ref, o_ref):
    x = x_ref[0]
    S = x.shape[0]
    row = lax.broadcasted_iota(jnp.int32, (S, 1), 0)

    def shifted(t, d, fill):
        return jnp.where(row >= d, pltpu.roll(t, d, 0), fill)

    u = cb_ref[...] + x * cw_ref[CONV_W - 1:CONV_W, :]
    for d in range(1, CONV_W):
        u = u + shifted(x, d, 0.0) * cw_ref[CONV_W - 1 - d:CONV_W - d, :]
    ub = u.astype(MXU)
    r = jax.nn.sigmoid(_dot(ub, gaw_ref[...]) + gab_ref[...])
    i = jax.nn.sigmoid(_dot(ub, gxw_ref[...]) + gxb_ref[...])
    lam = lam_ref[...]
    softplus_neg = jnp.maximum(-lam, 0.0) + jnp.log1p(jnp.exp(-jnp.abs(lam)))
    log_a = -LRU_C * r * softplus_neg
    a = jnp.exp(log_a)
    b = jnp.sqrt(_neg_expm1(2.0 * log_a)) * (i * u)
    d = 1
    while d < S:
        b = a * shifted(b, d, 0.0) + b
        a = a * shifted(a, d, 1.0)
        d *= 2
    o_ref[0] = b * jax.nn.gelu(xg_ref[0])


def _block_diag(w):
    n, c, _ = w.shape
    out = jnp.zeros((n * c, n * c), w.dtype)
    for j in range(n):
        out = out.at[j * c:(j + 1) * c, j * c:(j + 1) * c].set(w[j])
    return out


def rglru_block(h, conv_w, conv_b, ga_w, ga_b, gx_w, gx_b, lru_lambda):
    B, S, _ = h.shape
    W = RNN_W
    cx, cg = OFF["cx"] // W, OFF["cg"] // W
    vec = pl.BlockSpec((1, W), lambda b: (0, 0))
    mat = pl.BlockSpec((W, W), lambda b: (0, 0))
    return pl.pallas_call(
        _rglru_kernel,
        grid=(B,),
        in_specs=[pl.BlockSpec((1, S, W), lambda b: (b, 0, cx)), pl.BlockSpec((1, S, W), lambda b: (b, 0, cg)),
                  pl.BlockSpec((CONV_W, W), lambda b: (0, 0)), vec, mat, vec, mat, vec, vec],
        out_specs=pl.BlockSpec((1, S, W), lambda b: (b, 0, 0)),
        out_shape=jax.ShapeDtypeStruct((B, S, W), f32),
        compiler_params=_params(1),
        name="rglru",
    )(h, h, conv_w, conv_b.reshape(1, W), _block_diag(ga_w).astype(MXU), ga_b.reshape(1, W),
      _block_diag(gx_w).astype(MXU), gx_b.reshape(1, W), lru_lambda.reshape(1, W))


def _rms(x, g, width):
    return x * lax.rsqrt(jnp.sum(x * x, axis=-1, keepdims=True) * (1.0 / width) + RMS_EPS) * g


def _mla_prep_kernel(cq_ref, ckv_ref, kr_ref, krs_ref, gq_ref, gkv_ref, wq_ref, wqs_ref, wk_ref, wv_ref,
                     cosq_ref, sinq_ref, cosk_ref, sink_ref, vone_ref, q_ref, k_ref, v_ref):
    cq = _rms(cq_ref[0], gq_ref[...], MLA_Q_RANK).astype(MXU)
    ckv = _rms(ckv_ref[0], gkv_ref[...], MLA_KV_RANK).astype(MXU)
    scale = (MLA_NOPE + MLA_ROPE) ** -0.5
    q = _dot(cq, wq_ref[...]) * cosq_ref[...] + _dot(cq, wqs_ref[...]) * sinq_ref[...]
    q_ref[0] = (q * scale).astype(q_ref.dtype)
    k_rope = kr_ref[0] * cosk_ref[...] + krs_ref[0] * sink_ref[...]
    k = _dot(ckv, wk_ref[...])
    k_ref[0] = (k + jnp.concatenate([k_rope] * MLA_HEADS, axis=1)).astype(k_ref.dtype)
    v_ref[0] = (_dot(ckv, wv_ref[...]) + vone_ref[...]).astype(v_ref.dtype)


def _mla_attn_kernel(q_ref, k_ref, v_ref, cb_ref, o_ref, *, tq):
    tk = tq
    qi = pl.program_id(2)
    q0 = pl.multiple_of(qi * tq, tq)
    q = q_ref[0]
    lane = lax.broadcasted_iota(jnp.int32, (tq, LANE), 1)

    def update(carry, n, k, v, bias):
        m, acc = carry
        sl = slice(n * LANE, (n + 1) * LANE)
        s = _dot_nt(q[:, sl], k[:, sl])
        if bias is not None:
            s = s + bias
        m_new = jnp.maximum(m, jnp.max(s, axis=1, keepdims=True))
        p = jnp.exp(s - m_new)
        return m_new, jnp.exp(m - m_new) * acc + _dot(p.astype(MXU), v[:, sl])

    def body(kt, carry):
        k0 = pl.multiple_of(kt * tk, tk)
        k = k_ref[0, pl.ds(k0, tk), :]
        v = v_ref[0, pl.ds(k0, tk), :]
        return tuple(update(carry[n], n, k, v, None) for n in range(2))

    init = tuple((jnp.full((tq, 1), NEG, f32), jnp.zeros((tq, LANE), f32)) for _ in range(2))
    carry = lax.fori_loop(0, qi, body, init)
    k = k_ref[0, pl.ds(q0, tk), :]
    v = v_ref[0, pl.ds(q0, tk), :]
    o = []
    for n in range(2):
        _, acc = update(carry[n], n, k, v, cb_ref[...])
        denom = jnp.sum(jnp.where(lane == MLA_V, acc, 0.0), axis=1, keepdims=True)
        o.append(acc / jnp.maximum(denom, 1e-30))
    o_ref[0] = jnp.where(lane < MLA_V, o[0], pltpu.roll(o[1], MLA_V, 1))


def mla_attention(h, q_norm, kv_norm, w_uq, w_ukv, tr=_TR_MLA_PREP, tq=_TQ_MLA):
    B, S, _ = h.shape
    H = MLA_HEADS
    dq = MLA_NOPE + MLA_ROPE
    HW = H * LANE
    wq3 = w_uq.reshape(MLA_Q_RANK, H, dq)
    wq_rot = jnp.concatenate([jnp.zeros_like(wq3[..., :MLA_NOPE]), _rot_half(wq3[..., MLA_NOPE:])], axis=-1)

    def pad_q(w3):
        w3 = jnp.pad(w3, ((0, _CQ_PAD - MLA_Q_RANK), (0, 0), (0, LANE - dq)))
        return w3.reshape(_CQ_PAD, HW).astype(MXU)

    wkv3 = w_ukv.reshape(MLA_KV_RANK, H, MLA_NOPE + MLA_V)
    wk = jnp.pad(wkv3[..., :MLA_NOPE], ((0, 0), (0, 0), (0, LANE - MLA_NOPE))).reshape(MLA_KV_RANK, HW).astype(MXU)
    wv = jnp.pad(wkv3[..., MLA_NOPE:], ((0, 0), (0, 0), (0, LANE - MLA_V))).reshape(MLA_KV_RANK, HW).astype(MXU)
    v_one = jnp.tile(jnp.concatenate([jnp.zeros((1, MLA_V), f32), jnp.ones((1, LANE - MLA_V), f32)], axis=1), (1, H))
    gq = jnp.pad(q_norm, (0, _CQ_PAD - MLA_Q_RANK)).reshape(1, _CQ_PAD)
    gkv = kv_norm.reshape(1, MLA_KV_RANK)
    inv = ROPE_THETA ** (-jnp.arange(0, MLA_ROPE, 2, dtype=f32) / MLA_ROPE)
    ang = jnp.arange(S, dtype=f32)[:, None] * inv[None, :]
    cos2 = jnp.concatenate([jnp.cos(ang), jnp.cos(ang)], axis=1)
    sin2 = jnp.concatenate([jnp.sin(ang), jnp.sin(ang)], axis=1)
    tail = LANE - dq
    cos_k = jnp.concatenate([jnp.zeros((S, MLA_NOPE), f32), cos2, jnp.zeros((S, tail), f32)], axis=1)
    sin_k = jnp.concatenate([jnp.zeros((S, MLA_NOPE), f32), sin2, jnp.zeros((S, tail), f32)], axis=1)
    cos_q = jnp.tile(jnp.concatenate([jnp.ones((S, MLA_NOPE), f32), cos2, jnp.zeros((S, tail), f32)], axis=1), (1, H))
    sin_q = jnp.tile(sin_k, (1, H))
    rel = np.arange(tq)[:, None] - np.arange(tq)[None, :]
    causal = jnp.asarray(np.where(rel >= 0, 0.0, BIG_NEG).astype(np.float32))

    c_cq, c_ckv, c_kr, c_krs = OFF["dcq"] // _CQ_PAD, OFF["dckv"] // LANE, OFF["dkr"] // LANE, OFF["dkrs"] // LANE
    full = lambda shape: pl.BlockSpec(shape, lambda b, i: (0, 0))
    tab = lambda w: pl.BlockSpec((tr, w), lambda b, i: (i, 0))
    out3 = pl.BlockSpec((1, tr, HW), lambda b, i: (b, i, 0))
    q, k, v = pl.pallas_call(
        _mla_prep_kernel,
        grid=(B, S // tr),
        in_specs=[pl.BlockSpec((1, tr, _CQ_PAD), lambda b, i: (b, i, c_cq)),
                  pl.BlockSpec((1, tr, LANE), lambda b, i: (b, i, c_ckv)),
                  pl.BlockSpec((1, tr, LANE), lambda b, i: (b, i, c_kr)),
                  pl.BlockSpec((1, tr, LANE), lambda b, i: (b, i, c_krs)),
                  full((1, _CQ_PAD)), full((1, MLA_KV_RANK)), full((_CQ_PAD, HW)), full((_CQ_PAD, HW)),
                  full((MLA_KV_RANK, HW)), full((MLA_KV_RANK, HW)),
                  tab(HW), tab(HW), tab(LANE), tab(LANE), full((1, HW))],
        out_specs=[out3, out3, out3],
        out_shape=[jax.ShapeDtypeStruct((B, S, HW), MXU)] * 3,
        compiler_params=_params(2),
        name="mla_prep",
    )(h, h, h, h, gq, gkv, pad_q(wq3), pad_q(wq_rot), wk, wv, cos_q, sin_q, cos_k, sin_k, v_one)
    pair = lambda rows: pl.BlockSpec((1, rows, 2 * LANE), lambda b, p, i: (b, i if rows == tq else 0, p))
    return pl.pallas_call(
        functools.partial(_mla_attn_kernel, tq=tq),
        grid=(B, H // 2, S // tq),
        in_specs=[pair(tq), pair(S), pair(S), pl.BlockSpec((tq, tq), lambda b, p, i: (0, 0))],
        out_specs=pl.BlockSpec((1, tq, LANE), lambda b, p, i: (b, i, p)),
        out_shape=jax.ShapeDtypeStruct((B, S, H * MLA_V), f32),
        compiler_params=_params(3),
        name="mla_attention",
    )(q, k, v, causal)


def _ln(z, g, b):
    mu = jnp.mean(z, axis=-1, keepdims=True)
    zc = z - mu
    var = jnp.mean(zc * zc, axis=-1, keepdims=True)
    return zc * lax.rsqrt(var + LN_EPS) * g + b


def _merge_kernel(mg_ref, oa_ref, ob_ref, oc_ref, od_ref, wb_ref, wo_ref, x_ref, g_ref, b_ref, o_ref):
    acc = None
    for n, br in enumerate((oa_ref, ob_ref, oc_ref, od_ref)):
        up = _dot(br[...].astype(MXU), wb_ref[n])
        term = jax.nn.sigmoid(mg_ref[:, n * D_MODEL:(n + 1) * D_MODEL]) * up
        acc = term if acc is None else acc + term
    y = _dot(acc.astype(MXU), wo_ref[...])
    o_ref[...] = _ln(DN_ALPHA * x_ref[...] + y, g_ref[...], b_ref[...])


def merge_ln(h2, branches, w_branch, w_out, x2, g, b, tm=_TM_DENSE):
    N, D = x2.shape
    assert OFF["mg"] == 0 and N % tm == 0
    row = lambda w: pl.BlockSpec((tm, w), lambda i: (i, 0))
    return pl.pallas_call(
        _merge_kernel,
        grid=(N // tm,),
        in_specs=[row(N_MIXERS * D)] + [row(MIX_W)] * N_MIXERS
        + [pl.BlockSpec((N_MIXERS, MIX_W, D), lambda i: (0, 0, 0)), pl.BlockSpec((D, D), lambda i: (0, 0)),
           row(D), pl.BlockSpec((1, D), lambda i: (0, 0)), pl.BlockSpec((1, D), lambda i: (0, 0))],
        out_specs=row(D),
        out_shape=jax.ShapeDtypeStruct((N, D), f32),
        compiler_params=_params(1),
        name="merge_ln",
    )(h2, *branches, w_branch.astype(MXU), w_out.astype(MXU), x2, g.reshape(1, D), b.reshape(1, D))


ROW_TILE = 8


def _to_token_tiles(ref, val):
    rows = val.shape[0]
    for j in range(ROW_TILE):
        ref[pl.ds(j, rows, stride=ROW_TILE), :] = val[:, j * LANE:(j + 1) * LANE]


def _from_token_tiles(ref, rows, first=0, stride=ROW_TILE):
    return jnp.concatenate([ref[pl.ds(first + j, rows, stride=stride), :] for j in range(ROW_TILE)], axis=1)


def _xattn_kernel(x_ref, wq_ref, k_ref, v_ref, wo_ref, g_ref, b_ref, o_ref, o8_ref):
    x = x_ref[0]
    q = _dot(x.astype(MXU), wq_ref[...]).astype(MXU)
    k = k_ref[0]
    v = v_ref[0]
    heads = []
    for hd in range(X_HEADS):
        sl = slice(hd * X_HEAD_DIM, (hd + 1) * X_HEAD_DIM)
        s = _dot_nt(q[:, sl], k[:, sl]) * (X_HEAD_DIM ** -0.5)
        e = jnp.exp(s - jnp.max(s, axis=1, keepdims=True))
        p = e / jnp.sum(e, axis=1, keepdims=True)
        heads.append(_dot(p.astype(MXU), v[:, sl]).astype(MXU))
    y = _dot(jnp.concatenate(heads, axis=1), wo_ref[...])
    out = _ln(DN_ALPHA * x + y, g_ref[...], b_ref[...])
    o_ref[0] = out
    _to_token_tiles(o8_ref.at[0], out)


def cross_attention_ln(x, mem, wq, wkv, wo, g, b, tq=_TM_DENSE):
    B, S, D = x.shape
    assert D == ROW_TILE * LANE
    M = mem.shape[1]
    F = X_HEADS * X_HEAD_DIM
    kv = matmul(mem.reshape(B * M, D), wkv.astype(MXU), 512, 2 * F, out_dtype=MXU).reshape(B, M, 2 * F)
    full = lambda shape: pl.BlockSpec(shape, lambda bi, i: (0,) * len(shape))
    return pl.pallas_call(
        _xattn_kernel,
        grid=(B, S // tq),
        in_specs=[pl.BlockSpec((1, tq, D), lambda bi, i: (bi, i, 0)), full((D, F)),
                  pl.BlockSpec((1, M, F), lambda bi, i: (bi, 0, 0)), pl.BlockSpec((1, M, F), lambda bi, i: (bi, 0, 1)),
                  full((F, D)), full((1, D)), full((1, D))],
        out_specs=[pl.BlockSpec((1, tq, D), lambda bi, i: (bi, i, 0)),
                   pl.BlockSpec((1, tq * ROW_TILE, LANE), lambda bi, i: (bi, i, 0))],
        out_shape=[jax.ShapeDtypeStruct((B, S, D), f32), jax.ShapeDtypeStruct((B, S * ROW_TILE, LANE), f32)],
        compiler_params=_params(2),
        name="cross_attention_ln",
    )(x, wq.astype(MXU), kv, kv, wo.astype(MXU), g.reshape(1, D), b.reshape(1, D))


_R_E0, _R_E1, _R_W0, _R_W1, _R_RANK0, _R_RANK1 = range(6)
_GRP_LANE0 = N_EXPERTS


def _router_kernel(x_ref, w_ref, b_ref, tri_ref, r_ref, cnt_ref):
    i = pl.program_id(0)
    tm = x_ref.shape[0]
    logits = _dot_nt(w_ref[...], x_ref[...].astype(MXU)) + b_ref[...]
    row = lax.broadcasted_iota(jnp.int32, (LANE, tm), 0)
    row_f = row.astype(f32)
    big = float(LANE)

    def cmax(t):
        return jnp.max(t, axis=0, keepdims=True)

    def first_row(cond):
        return jnp.min(jnp.where(cond, row_f, big), axis=0, keepdims=True)

    def softmax_on(mask):
        lm = jnp.where(mask, logits, NEG)
        e = jnp.where(mask, jnp.exp(lm - cmax(lm)), 0.0)
        return e / jnp.sum(e, axis=0, keepdims=True)

    is_g = (row >= _GRP_LANE0) & (row < _GRP_LANE0 + N_GROUPS)
    p_grp = softmax_on(is_g)
    p_g = cmax(p_grp)
    grp = first_row(is_g & (p_grp == p_g)) - float(_GRP_LANE0)
    grp_of_row = jnp.right_shift(row, EXPERTS_PER_GROUP.bit_length() - 1)
    in_grp = (row < N_EXPERTS) & (grp_of_row.astype(f32) == grp)
    p_e = softmax_on(in_grp)
    p1 = cmax(jnp.where(in_grp, p_e, -1.0))
    e1 = first_row(in_grp & (p_e == p1))
    rest = in_grp & (row_f != e1)
    p2 = cmax(jnp.where(rest, p_e, -1.0))
    e2 = first_row(rest & (p_e == p2))
    w1 = p_g * p1 / (p1 + p2)
    w2 = p_g * p2 / (p1 + p2)

    @pl.when(i == 0)
    def _():
        cnt_ref[...] = jnp.zeros_like(cnt_ref)

    oh1 = row_f == e1
    oh2 = row_f == e2
    both = (oh1 | oh2).astype(MXU)
    before = _dot(both, tri_ref[...]) + cnt_ref[:, 0:1]
    rank1 = jnp.sum(jnp.where(oh1, before, 0.0), axis=0, keepdims=True)
    rank2 = jnp.sum(jnp.where(oh2, before, 0.0), axis=0, keepdims=True)
    cnt_ref[...] = cnt_ref[...] + jnp.sum(both.astype(f32), axis=1, keepdims=True)

    out = jnp.zeros((LANE, tm), f32)
    for slot, val in ((_R_E0, e1), (_R_E1, e2), (_R_W0, w1), (_R_W1, w2), (_R_RANK0, rank1), (_R_RANK1, rank2)):
        out = jnp.where(row == slot, val, out)
    r_ref[...] = out.T


def moe_router(x2, rg_w, rg_b, re_w, re_b, tm=_TM_ROWS):
    N, D = x2.shape
    assert EXPERTS_PER_GROUP & (EXPERTS_PER_GROUP - 1) == 0 and N_EXPERTS + N_GROUPS <= LANE
    w = jnp.pad(jnp.concatenate([re_w, rg_w], axis=1), ((0, 0), (0, LANE - N_EXPERTS - N_GROUPS))).T.astype(MXU)
    b = jnp.pad(jnp.concatenate([re_b, rg_b]), (0, LANE - N_EXPERTS - N_GROUPS)).reshape(LANE, 1)
    tri = jnp.asarray(np.triu(np.ones((tm, tm), np.float32), 1), MXU)
    return pl.pallas_call(
        _router_kernel,
        grid=(N // tm,),
        in_specs=[pl.BlockSpec((tm, D), lambda i: (i, 0)), pl.BlockSpec((LANE, D), lambda i: (0, 0)),
                  pl.BlockSpec((LANE, 1), lambda i: (0, 0)), pl.BlockSpec((tm, tm), lambda i: (0, 0))],
        out_specs=[pl.BlockSpec((tm, LANE), lambda i: (i, 0)), pl.BlockSpec((LANE, LANE), lambda i: (0, 0))],
        out_shape=[jax.ShapeDtypeStruct((N, LANE), f32), jax.ShapeDtypeStruct((LANE, LANE), f32)],
        compiler_params=_params(1),
        name="moe_router",
    )(x2, w, b, tri)


def _ffn_kernel(ce_ref, first_ref, slot_ref, next_ref, nu_ref, x_ref, wgu_hbm, wd_hbm, o_ref,
                wgu_f32, wd_f32, wgu_b, wd_b, sems, *, layer):
    c = pl.program_id(0)
    used = c < nu_ref[0]

    def fetch(e, slot):
        return (pltpu.make_async_copy(wgu_hbm.at[layer, e], wgu_f32.at[slot], sems.at[slot, 0]),
                pltpu.make_async_copy(wd_hbm.at[layer, e], wd_f32.at[slot], sems.at[slot, 1]))

    @pl.when(used & (c == 0))
    def _():
        for cp in fetch(ce_ref[0], 0):
            cp.start()

    @pl.when(used & (first_ref[c] == 1))
    def _():
        slot = slot_ref[c]
        for cp in fetch(ce_ref[c], slot):
            cp.wait()

        @pl.when(next_ref[c] >= 0)
        def _():
            for cp in fetch(next_ref[c], 1 - slot):
                cp.start()

        wgu_b[...] = wgu_f32[slot].astype(MXU)
        wd_b[...] = wd_f32[slot].astype(MXU)

    @pl.when(used)
    def _():
        x = _from_token_tiles(x_ref, EXPERT_CHUNK)
        gu = _dot(x.astype(MXU), wgu_b[...])
        hid = jax.nn.silu(gu[:, :D_EXPERT]) * gu[:, D_EXPERT:]
        _to_token_tiles(o_ref, _dot(hid.astype(MXU), wd_b[...]))

    @pl.when(jnp.logical_not(used))
    def _():
        o_ref[...] = jnp.zeros_like(o_ref)


def expert_ffn(xb8, chunk_e, n_used, w_gu, w_down, layer):
    C = EXPERT_CHUNK
    n_chunks = xb8.shape[0] // (C * ROW_TILE)
    D = ROW_TILE * LANE
    idx = jnp.arange(n_chunks, dtype=jnp.int32)
    used = idx < n_used[0]
    first = used & ((idx == 0) | (chunk_e != jnp.roll(chunk_e, 1)))
    slot = (jnp.cumsum(first.astype(jnp.int32)) - 1) % 2
    none = jnp.int32(N_EXPERTS)
    later = jnp.roll(jnp.where(first, chunk_e, none), -1).at[-1].set(none)
    next_e = lax.cummin(later, axis=0, reverse=True)
    next_e = jnp.where(next_e == none, -1, next_e)
    tile_spec = pl.BlockSpec((C * ROW_TILE, LANE), lambda c, *_: (c, 0))
    any_spec = pl.BlockSpec(memory_space=pl.ANY)
    grid_spec = pltpu.PrefetchScalarGridSpec(
        num_scalar_prefetch=5,
        grid=(n_chunks,),
        in_specs=[tile_spec, any_spec, any_spec],
        out_specs=tile_spec,
        scratch_shapes=[pltpu.VMEM((2, D, 2 * D_EXPERT), w_gu.dtype), pltpu.VMEM((2, D_EXPERT, D), w_down.dtype),
                        pltpu.VMEM((D, 2 * D_EXPERT), MXU), pltpu.VMEM((D_EXPERT, D), MXU),
                        pltpu.SemaphoreType.DMA((2, 2))],
    )
    return pl.pallas_call(
        functools.partial(_ffn_kernel, layer=layer),
        grid_spec=grid_spec,
        out_shape=jax.ShapeDtypeStruct(xb8.shape, f32),
        compiler_params=_params(1),
        name="expert_ffn",
    )(chunk_e, first.astype(jnp.int32), slot.astype(jnp.int32), next_e.astype(jnp.int32), n_used, xb8, w_gu, w_down)


_COPY_WINDOW = 512


def _windowed_copies(n, copy, wait):
    assert _COPY_WINDOW & (_COPY_WINDOW - 1) == 0 and n >= _COPY_WINDOW

    def fill(j, carry):
        copy(j, j).start()
        return carry

    def steady(j, carry):
        s = jnp.bitwise_and(j, _COPY_WINDOW - 1)
        wait(s)
        copy(j, s).start()
        return carry

    def drain(s, carry):
        wait(s)
        return carry

    lax.fori_loop(0, _COPY_WINDOW, fill, 0, unroll=8)
    lax.fori_loop(_COPY_WINDOW, n, steady, 0, unroll=8)
    lax.fori_loop(0, _COPY_WINDOW, drain, 0, unroll=8)


def _dispatch_kernel(dest_ref, x8_ref, init_hbm, xb_hbm, sems, *, tm):
    n = tm * TOPK_IN_GROUP
    base = pl.program_id(0) * n

    def copy(j, s):
        t = pl.multiple_of(jnp.right_shift(j, 1) * ROW_TILE, ROW_TILE)
        d = pl.multiple_of(dest_ref[base + j] * ROW_TILE, ROW_TILE)
        return pltpu.make_async_copy(x8_ref.at[pl.ds(t, ROW_TILE)], xb_hbm.at[pl.ds(d, ROW_TILE)], sems.at[s])

    def wait(s):
        pltpu.make_async_copy(x8_ref.at[pl.ds(0, ROW_TILE)], xb_hbm.at[pl.ds(0, ROW_TILE)], sems.at[s]).wait()

    _windowed_copies(n, copy, wait)


def moe_dispatch(x8, dest, n_slots, tm=_TM_DISPATCH):
    n_tok = x8.shape[0] // ROW_TILE
    assert n_tok % tm == 0 and tm * TOPK_IN_GROUP >= _COPY_WINDOW
    grid_spec = pltpu.PrefetchScalarGridSpec(
        num_scalar_prefetch=1,
        grid=(n_tok // tm,),
        in_specs=[pl.BlockSpec((tm * ROW_TILE, LANE), lambda i, dest: (i, 0)), pl.BlockSpec(memory_space=pl.ANY)],
        out_specs=pl.BlockSpec(memory_space=pl.ANY),
        scratch_shapes=[pltpu.SemaphoreType.DMA((_COPY_WINDOW,))],
    )
    return pl.pallas_call(
        functools.partial(_dispatch_kernel, tm=tm),
        grid_spec=grid_spec,
        out_shape=jax.ShapeDtypeStruct((n_slots * ROW_TILE, LANE), x8.dtype),
        input_output_aliases={2: 0},
        compiler_params=_params(1),
        name="moe_dispatch",
    )(dest, x8, jnp.zeros((n_slots * ROW_TILE, LANE), x8.dtype))


def _combine_kernel(dest_ref, x_ref, r_ref, g_ref, b_ref, yb_hbm, o_ref, buf, sems):
    tm = x_ref.shape[0]
    n = tm * TOPK_IN_GROUP
    i = pl.program_id(0)

    def issue(tile):
        dst_buf = buf.at[jnp.bitwise_and(tile, 1)]

        def body(j, carry):
            src = pl.multiple_of(dest_ref[tile * n + j] * ROW_TILE, ROW_TILE)
            dst = pl.multiple_of(j * ROW_TILE, ROW_TILE)
            pltpu.make_async_copy(yb_hbm.at[pl.ds(src, ROW_TILE)], dst_buf.at[pl.ds(dst, ROW_TILE)],
                                  sems.at[j]).start()
            return carry

        lax.fori_loop(0, n, body, 0, unroll=8)

    def drain(j, carry):
        pltpu.make_async_copy(yb_hbm.at[pl.ds(0, ROW_TILE)], buf.at[0, pl.ds(0, ROW_TILE)], sems.at[j]).wait()
        return carry

    @pl.when(i == 0)
    def _():
        issue(i)

    lax.fori_loop(0, n, drain, 0, unroll=8)

    @pl.when(i + 1 < pl.num_programs(0))
    def _():
        issue(i + 1)

    cur = buf.at[jnp.bitwise_and(i, 1)]
    r = r_ref[...]
    lane = lax.broadcasted_iota(jnp.int32, r.shape, 1)
    w0 = jnp.sum(jnp.where(lane == _R_W0, r, 0.0), axis=1, keepdims=True)
    w1 = jnp.sum(jnp.where(lane == _R_W1, r, 0.0), axis=1, keepdims=True)
    pair = TOPK_IN_GROUP * ROW_TILE
    y = _from_token_tiles(cur, tm, 0, pair) * w0 + _from_token_tiles(cur, tm, ROW_TILE, pair) * w1
    o_ref[...] = _ln(DN_ALPHA * x_ref[...] + y, g_ref[...], b_ref[...])


def combine_ln(x2, yb8, dest, r, g, b, tm=_TM_ROWS):
    N, D = x2.shape
    assert N % tm == 0
    row = lambda w: pl.BlockSpec((tm, w), lambda i, dest: (i, 0))
    vec = pl.BlockSpec((1, D), lambda i, dest: (0, 0))
    grid_spec = pltpu.PrefetchScalarGridSpec(
        num_scalar_prefetch=1,
        grid=(N // tm,),
        in_specs=[row(D), row(LANE), vec, vec, pl.BlockSpec(memory_space=pl.ANY)],
        out_specs=row(D),
        scratch_shapes=[pltpu.VMEM((2, tm * TOPK_IN_GROUP * ROW_TILE, LANE), f32),
                        pltpu.SemaphoreType.DMA((tm * TOPK_IN_GROUP,))],
    )
    return pl.pallas_call(
        _combine_kernel,
        grid_spec=grid_spec,
        out_shape=jax.ShapeDtypeStruct((N, D), f32),
        compiler_params=_params(1),
        name="moe_combine_ln",
    )(dest, x2, r, g.reshape(1, D), b.reshape(1, D), yb8)


def hier_moe_ln(x2, x8, rg_w, rg_b, re_w, re_b, w_gu, w_down, layer, g, b):
    N, D = x2.shape
    E, C, K = N_EXPERTS, EXPERT_CHUNK, TOPK_IN_GROUP
    A = N * K
    r, cnt = moe_router(x2, rg_w, rg_b, re_w, re_b)
    e = r[:, _R_E0:_R_E1 + 1].astype(jnp.int32)
    rank = r[:, _R_RANK0:_R_RANK1 + 1].astype(jnp.int32)
    counts = cnt[:E, 0].astype(jnp.int32)
    padded = (counts + C - 1) // C * C
    pad_end = jnp.cumsum(padded)
    below = jnp.arange(E, dtype=jnp.int32)[None, None, :] < e[..., None]
    dest = rank + jnp.sum(jnp.where(below, padded[None, None, :], 0), axis=-1)
    n_chunks = -(-(A + E * (C - 1)) // C)
    P = n_chunks * C
    chunk_start = jnp.arange(n_chunks, dtype=jnp.int32) * C
    chunk_e = jnp.minimum(jnp.sum((pad_end[None, :] <= chunk_start[:, None]).astype(jnp.int32), axis=1), E - 1)
    n_used = (pad_end[-1] // C).reshape(1).astype(jnp.int32)
    dest = dest.reshape(A).astype(jnp.int32)
    xb8 = moe_dispatch(x8, dest, P)
    yb8 = expert_ffn(xb8, chunk_e.astype(jnp.int32), n_used, w_gu, w_down, layer)
    return combine_ln(x2, yb8, dest, r, g, b)


def kernel(x, mem, w_in, nsa_cmp_pos, nsa_cmp_w1, nsa_cmp_w2, rnn_conv_w, rnn_conv_b, rnn_ga_w, rnn_ga_b,
           rnn_gx_w, rnn_gx_b, rnn_lambda, mla_q_norm, mla_kv_norm, mla_w_uq, mla_w_ukv, w_branch, w_out,
           ln1_g, ln1_b, x_wq, x_wkv, x_wo, ln2_g, ln2_b, moe_rg_w, moe_rg_b, moe_re_w, moe_re_b,
           moe_w_gu, moe_w_down, ln3_g, ln3_b):
    B, S, D = x.shape
    N = B * S
    x2 = x.reshape(N, D)
    for l in range(DEPTH):
        h2 = matmul(x2, _cat_w_in(w_in[l]), 512, _IN_TILE)
        h = h2.reshape(B, S, N_CAT)
        o_a = nsa_attention(h, nsa_cmp_pos[l], nsa_cmp_w1[l], nsa_cmp_w2[l])
        o_b = stick_breaking_attention(h)
        o_c = rglru_block(h, rnn_conv_w[l], rnn_conv_b[l], rnn_ga_w[l], rnn_ga_b[l], rnn_gx_w[l], rnn_gx_b[l],
                          rnn_lambda[l])
        o_d = mla_attention(h, mla_q_norm[l], mla_kv_norm[l], mla_w_uq[l], mla_w_ukv[l])
        branches = [o.reshape(N, MIX_W) for o in (o_a, o_b, o_c, o_d)]
        x2 = merge_ln(h2, branches, w_branch[l], w_out[l], x2, ln1_g[l], ln1_b[l])
        x3, x8 = cross_attention_ln(x2.reshape(B, S, D), mem, x_wq[l], x_wkv[l], x_wo[l], ln2_g[l], ln2_b[l])
        x2 = hier_moe_ln(x3.reshape(N, D), x8.reshape(N * ROW_TILE, LANE), moe_rg_w[l], moe_rg_b[l], moe_re_w[l],
                         moe_re_b[l], moe_w_gu, moe_w_down, l, ln3_g[l], ln3_b[l])
    return x2.reshape(B, S, D)
```

```python
import functools

import numpy as np
import jax
import jax.numpy as jnp
from jax import lax
from jax.experimental import pallas as pl
from jax.experimental.pallas import tpu as pltpu

D_MODEL = 1024
DEPTH = 4
HEAD_DIM = 64
N_MIXERS = 4
MIX_W = 256
NSA_HEADS = 4
NSA_KV_HEADS = 2
NSA_GROUP = NSA_HEADS // NSA_KV_HEADS
CMP_LEN = 32
CMP_STRIDE = 16
CMP_HID = 256
SEL_BLOCK = 64
SEL_TOPK = 8
WINDOW = 512
FORCE_SCORE = 1e4
SB_HEADS = 4
RNN_W = 256
CONV_W = 4
LRU_C = 8.0
MLA_HEADS = 4
MLA_Q_RANK = 192
MLA_KV_RANK = 128
MLA_NOPE = 64
MLA_ROPE = 32
MLA_V = 64
ROPE_THETA = 10000.0
X_HEADS = 4
X_HEAD_DIM = 128
N_GROUPS = 4
EXPERTS_PER_GROUP = 8
N_EXPERTS = N_GROUPS * EXPERTS_PER_GROUP
TOPK_IN_GROUP = 2
D_EXPERT = 512
EXPERT_CHUNK = 256
DN_ALPHA = (2.0 * DEPTH) ** 0.25
LN_EPS = 1e-5
RMS_EPS = 1e-6

IN_SPLITS = ((NSA_HEADS * HEAD_DIM,) + (NSA_KV_HEADS * HEAD_DIM,) * 6 + (NSA_HEADS * 3,)
             + (SB_HEADS * HEAD_DIM,) * 3
             + (RNN_W, RNN_W)
             + (MLA_Q_RANK, MLA_KV_RANK, MLA_ROPE)
             + (N_MIXERS * D_MODEL,))
IN_OFFSETS = tuple(int(o) for o in np.concatenate([[0], np.cumsum(IN_SPLITS)[:-1]]))

LANE = 128
VMEM_LIMIT = 48 * 1024 * 1024
NEG = -1e30
BIG_NEG = -2.0 ** 100

f32 = jnp.float32
MXU = jnp.bfloat16

_CQ_PAD = 2 * LANE
_W_Q, _W_KV, _W_KVT = NSA_HEADS * HEAD_DIM, NSA_KV_HEADS * HEAD_DIM, NSA_KV_HEADS * LANE
_SECTIONS = (("mg", N_MIXERS * D_MODEL), ("aq", _W_Q), ("kvs", _W_KVT), ("kvw", _W_KVT),
             ("bq", SB_HEADS * HEAD_DIM), ("bk", SB_HEADS * HEAD_DIM), ("bv", SB_HEADS * HEAD_DIM),
             ("cx", RNN_W), ("cg", RNN_W), ("dcq", _CQ_PAD), ("kc", _W_KV), ("vc", _W_KV),
             ("ag", LANE), ("dckv", MLA_KV_RANK), ("dkr", LANE), ("dkrs", LANE))
_IN_TILE = 3584
OFF = {}
_o = 0
for _n, _w in _SECTIONS:
    OFF[_n] = _o
    _o += _w
N_CAT = -(-_o // _IN_TILE) * _IN_TILE

_SEL_ROWS = 32
_AUG_SEL = HEAD_DIM
_AUG_POS = HEAD_DIM + _SEL_ROWS
_POS_SPLIT = 256

_TQ_NSA = 512
_TQ_MLA = 512
_TQ_SB = 256
_TM_ROWS = 256
_TM_DENSE = 512
_TM_DISPATCH = 256
_TR_MLA_PREP = 512


def _rot_half(t):
    d = t.shape[-1]
    return jnp.concatenate([-t[..., d // 2:], t[..., :d // 2]], axis=-1)


def _cat_w_in(w):
    def sec(i):
        return w[..., IN_OFFSETS[i]:IN_OFFSETS[i] + IN_SPLITS[i]]

    def pair(k, v):
        return jnp.concatenate([k[..., :HEAD_DIM], v[..., :HEAD_DIM], k[..., HEAD_DIM:], v[..., HEAD_DIM:]], axis=-1)

    def pad_cols(t, before, after):
        return jnp.pad(t, ((0, 0),) * (t.ndim - 1) + ((before, after),))

    def padc(t, n):
        return pad_cols(t, 0, n - t.shape[-1])

    def rope_slot(t):
        return pad_cols(t, MLA_NOPE, LANE - MLA_NOPE - MLA_ROPE)

    parts = {"mg": sec(16), "aq": sec(0), "kc": sec(1), "vc": sec(2), "kvs": pair(sec(3), sec(4)),
             "kvw": pair(sec(5), sec(6)), "ag": padc(sec(7), LANE), "bq": sec(8), "bk": sec(9),
             "bv": sec(10), "cx": sec(11), "cg": sec(12), "dcq": padc(sec(13), _CQ_PAD), "dckv": sec(14),
             "dkr": rope_slot(sec(15)), "dkrs": rope_slot(_rot_half(sec(15)))}
    cat = jnp.concatenate([parts[n] for n, _ in _SECTIONS], axis=-1)
    return padc(cat, N_CAT).astype(MXU)


def _dot(a, b):
    return jnp.dot(a, b, preferred_element_type=f32)


def _dot_nt(a, b):
    return lax.dot_general(a, b, (((1,), (1,)), ((), ())), preferred_element_type=f32)


def _params(n_axes):
    return pltpu.CompilerParams(dimension_semantics=("arbitrary",) * n_axes, vmem_limit_bytes=VMEM_LIMIT)


def _mm_kernel(a_ref, b_ref, o_ref):
    o_ref[...] = _dot(a_ref[...].astype(MXU), b_ref[0]).astype(o_ref.dtype)


def matmul(a, b, layer, tm, tn, out_dtype=f32):
    M, K = a.shape
    _, _, N = b.shape
    assert M % tm == 0 and N % tn == 0
    return pl.pallas_call(
        _mm_kernel,
        grid=(N // tn, M // tm),
        in_specs=[pl.BlockSpec((tm, K), lambda j, i: (i, 0)),
                  pl.BlockSpec((1, K, tn), lambda j, i: (layer, 0, j))],
        out_specs=pl.BlockSpec((tm, tn), lambda j, i: (i, j)),
        out_shape=jax.ShapeDtypeStruct((M, N), out_dtype),
        compiler_params=_params(2),
        name="matmul",
    )(a, b)


def _cmp_kernel(t_ref, w1_ref, pos_ref, w2_ref, o_ref):
    nc = t_ref.shape[1] // CMP_STRIDE
    t = jnp.concatenate([t_ref[0, pl.ds(l, nc, stride=CMP_STRIDE), :] for l in range(CMP_STRIDE)], axis=1)
    half = t.shape[1]
    t = t.astype(MXU)
    y1 = _dot(t, w1_ref[0, :half, :])
    y2 = _dot(t, w1_ref[0, half:, :])
    pos = jnp.broadcast_to(pos_ref[0], (8, 2 * half)).astype(MXU)
    pc = _dot(pos, w1_ref[0])[0:1]
    nc = y2.shape[0]
    hid = y1 + pltpu.roll(y2, nc - 1, 0) + pc
    o_ref[0, 0] = _dot(jax.nn.gelu(hid).astype(MXU), w2_ref[0])


def nsa_compress(h, cmp_w1, cmp_pos, cmp_w2):
    B, S, _ = h.shape
    G = NSA_KV_HEADS
    NC = S // CMP_STRIDE
    F = CMP_STRIDE * G * HEAD_DIM
    assert G * HEAD_DIM == LANE and OFF["vc"] == OFF["kc"] + LANE
    col0 = OFF["kc"] // LANE
    def group_diag(w, axis):
        blocks = [jnp.concatenate([w if k == g else jnp.zeros_like(w) for k in range(G)], axis=-1) for g in range(G)]
        return jnp.stack(blocks, axis=axis)

    w1 = group_diag(cmp_w1.reshape(2, CMP_LEN, HEAD_DIM, CMP_HID).astype(MXU), 2)
    w1 = w1.reshape(2, 2 * F, G * CMP_HID)
    pos = jnp.broadcast_to(cmp_pos[:, :, None, :], (2, CMP_LEN, G, HEAD_DIM)).reshape(2, 1, 2 * F)
    w2 = group_diag(jnp.concatenate([cmp_w2, cmp_w2], axis=-1).astype(MXU), 1)
    w2 = w2.reshape(2, G * CMP_HID, G * LANE)
    return pl.pallas_call(
        _cmp_kernel,
        grid=(2, B),
        in_specs=[pl.BlockSpec((1, S, LANE), lambda j, i: (i, 0, col0 + j)),
                  pl.BlockSpec((1, 2 * F, G * CMP_HID), lambda j, i: (j, 0, 0)),
                  pl.BlockSpec((1, 1, 2 * F), lambda j, i: (j, 0, 0)),
                  pl.BlockSpec((1, G * CMP_HID, G * LANE), lambda j, i: (j, 0, 0))],
        out_specs=pl.BlockSpec((1, 1, NC, G * LANE), lambda j, i: (j, i, 0, 0)),
        out_shape=jax.ShapeDtypeStruct((2, B, NC, G * LANE), f32),
        compiler_params=_params(2),
        name="nsa_compress",
    )(h, w1, pos, w2)


def _nsa_kernel(q_ref, kc_ref, vc_ref, kvs_ref, kvw_ref, gl_ref, cover_ref, aug_ref, cb_ref, wb_ref,
                o_ref, ksa, vsa, kwa, vwa, *, tq, n_cmp, n_sel, n_top):
    tk = tq
    g = pl.program_id(1)
    qi = pl.program_id(2)
    q0 = pl.multiple_of(qi * tq, tq)
    lane = lax.broadcasted_iota(jnp.int32, (tq, LANE), 1)
    lo_half = lane < HEAD_DIM

    @pl.when(qi == 0)
    def _():
        real = lax.broadcasted_iota(jnp.int32, ksa.shape, 1) < HEAD_DIM
        aug = aug_ref[...]
        ones = jnp.ones(ksa.shape, MXU)
        for kv_ref, k_out, v_out in ((kvs_ref, ksa, vsa), (kvw_ref, kwa, vwa)):
            kv = kv_ref[0]
            k_out[...] = jnp.where(real, kv.astype(MXU), aug)
            v_out[...] = jnp.where(real, pltpu.roll(kv, HEAD_DIM, 1).astype(MXU), ones)

    q = q_ref[0] * (HEAD_DIM ** -0.5)
    q_heads = (jnp.where(lo_half, q, 0.0), jnp.where(lo_half, pltpu.roll(q, HEAD_DIM, 1), 0.0))
    alibi = [2.0 ** (-8.0 * (h + 1) / NSA_HEADS) for h in range(NSA_HEADS)]
    slopes = [jnp.where(g == 0, alibi[n], alibi[NSA_GROUP + n]) for n in range(NSA_GROUP)]
    pos_cols = [jnp.where(lane == _AUG_POS, slopes[n] * _POS_SPLIT, jnp.where(lane == _AUG_POS + 1, slopes[n], 0.0))
                for n in range(NSA_GROUP)]
    tpos = q0 + lax.broadcasted_iota(jnp.int32, (tq, 1), 0)

    nc = kc_ref.shape[2]
    cidx = lax.broadcasted_iota(jnp.int32, (1, nc), 1)
    dist_c = tpos - (cidx * CMP_STRIDE + (CMP_LEN - 1))
    mask_c = (dist_c >= 0) & (cidx < n_cmp)
    dist_cf = dist_c.astype(f32)
    kc = kc_ref[0, 0].astype(MXU)
    vc = vc_ref[0, 0].astype(MXU)
    o_cmp = []
    imp_t = jnp.zeros((LANE, tq), f32)
    for n in range(NSA_GROUP):
        s = _dot_nt(q_heads[n].astype(MXU), kc) - slopes[n] * dist_cf
        sm = jnp.where(mask_c, s, NEG)
        m = jnp.max(sm, axis=1, keepdims=True)
        p = jnp.where(mask_c, jnp.exp(sm - m), 0.0)
        p = (p / jnp.maximum(jnp.sum(p, axis=1, keepdims=True), 1e-30)).astype(MXU)
        o_cmp.append(_dot(p, vc))
        imp_t = imp_t + _dot_nt(cover_ref[...], p)

    rows = _SEL_ROWS
    imp = imp_t[:rows]
    blk = lax.broadcasted_iota(jnp.int32, (rows, tq), 0)
    blk_f = blk.astype(f32)
    tpos_t = q0 + lax.broadcasted_iota(jnp.int32, (1, tq), 1)
    forced = (blk == 0) | (blk == jnp.right_shift(tpos_t, SEL_BLOCK.bit_length() - 1))
    valid = blk * SEL_BLOCK <= tpos_t
    imp = jnp.where(forced, FORCE_SCORE, jnp.where(valid, imp, -1.0))
    imp = jnp.where(blk < n_sel, imp, NEG)
    sel_t = jnp.zeros((rows, tq), f32)
    for _ in range(n_top):
        m = jnp.max(imp, axis=0, keepdims=True)
        first = jnp.min(jnp.where(imp == m, blk_f, float(LANE)), axis=0, keepdims=True)
        pick = blk_f == first
        sel_t = jnp.where(pick, 1.0, sel_t)
        imp = jnp.where(pick, 2 * NEG, imp)
    sel = jnp.concatenate([sel_t, jnp.zeros((LANE - rows, tq), f32)], axis=0).T
    sel_bias = pltpu.roll(jnp.where(sel > 0.5, 0.0, BIG_NEG), _AUG_SEL, 1)
    sel_cols = jnp.where((lane >= _AUG_SEL) & (lane < _AUG_SEL + rows), sel_bias, 0.0)
    q_sel = jnp.concatenate([q_heads[n] + sel_cols + pos_cols[n] for n in range(NSA_GROUP)], axis=0).astype(MXU)
    q_win = jnp.concatenate([q_heads[n] + pos_cols[n] for n in range(NSA_GROUP)], axis=0).astype(MXU)
    rows_q = NSA_GROUP * tq
    denom_lane = lax.broadcasted_iota(jnp.int32, (rows_q, LANE), 1) == HEAD_DIM

    def update(carry, qs, k, v, bias):
        m, acc = carry
        s = _dot_nt(qs, k)
        if bias is not None:
            s = s + bias
        m_new = jnp.maximum(m, jnp.max(s, axis=1, keepdims=True))
        p = jnp.exp(s - m_new)
        return m_new, jnp.exp(m - m_new) * acc + _dot(p.astype(MXU), v)

    def finish(carry):
        _, acc = carry
        denom = jnp.sum(jnp.where(denom_lane, acc, 0.0), axis=1, keepdims=True)
        o = acc / jnp.maximum(denom, 1e-30)
        return [o[n * tq:(n + 1) * tq] for n in range(NSA_GROUP)]

    init = (jnp.full((rows_q, 1), NEG, f32), jnp.zeros((rows_q, LANE), f32))
    causal = cb_ref[...]

    def sel_body(kt, carry):
        k0 = pl.multiple_of(kt * tk, tk)
        return update(carry, q_sel, ksa[pl.ds(k0, tk), :], vsa[pl.ds(k0, tk), :], None)

    carry = lax.fori_loop(0, qi, sel_body, init)
    o_sel = finish(update(carry, q_sel, ksa[pl.ds(q0, tk), :], vsa[pl.ds(q0, tk), :], causal))

    carry = init
    n_back = WINDOW // tk
    for back in range(n_back, -1, -1):
        k0 = pl.multiple_of(jnp.maximum(qi - back, 0) * tk, tk)
        if back:
            off = jnp.where(qi >= back, 0.0, BIG_NEG)
            bias = wb_ref[...] + off if back == n_back else off
        else:
            bias = causal
        carry = update(carry, q_win, kwa[pl.ds(k0, tk), :], vwa[pl.ds(k0, tk), :], bias)
    o_win = finish(carry)

    sig = jax.nn.sigmoid(gl_ref[0])

    def gate(n, j):
        col = 3 * (NSA_GROUP * g + n) + j
        return jnp.sum(jnp.where(lane == col, sig, 0.0), axis=1, keepdims=True)

    o = [gate(n, 0) * o_cmp[n] + gate(n, 1) * o_sel[n] + gate(n, 2) * o_win[n] for n in range(NSA_GROUP)]
    o_ref[0] = jnp.where(lo_half, o[0], pltpu.roll(o[1], HEAD_DIM, 1))


def nsa_attention(h, cmp_pos, cmp_w1, cmp_w2, tq=_TQ_NSA):
    B, S, _ = h.shape
    G = NSA_KV_HEADS
    NC = S // CMP_STRIDE
    n_cmp = (S - CMP_LEN) // CMP_STRIDE + 1
    n_sel = S // SEL_BLOCK
    n_top = min(SEL_TOPK, n_sel)
    F = CMP_STRIDE * HEAD_DIM
    assert S % tq == 0 and WINDOW % tq == 0 and tq % SEL_BLOCK == 0 and n_sel <= _SEL_ROWS and S <= _POS_SPLIT ** 2

    kvc = nsa_compress(h, cmp_w1, cmp_pos, cmp_w2)

    c0 = np.arange(n_cmp)[:, None] * CMP_STRIDE
    j0 = np.arange(n_sel)[None, :] * SEL_BLOCK
    cover = np.clip(np.minimum(c0 + CMP_LEN, j0 + SEL_BLOCK) - np.maximum(c0, j0), 0, None) / CMP_LEN
    cover_t = np.zeros((LANE, NC), np.float32)
    cover_t[:n_sel, :n_cmp] = cover.T
    pos = np.arange(S)
    aug = np.zeros((S, LANE), np.float32)
    aug[pos, _AUG_SEL + pos // SEL_BLOCK] = 1.0
    aug[:, _AUG_POS] = pos // _POS_SPLIT
    aug[:, _AUG_POS + 1] = pos % _POS_SPLIT
    rel = np.arange(tq)[:, None] - np.arange(tq)[None, :]
    rel = np.tile(rel, (NSA_GROUP, 1))
    causal = np.where(rel >= 0, 0.0, BIG_NEG).astype(np.float32)
    win_lo = np.where(rel < 0, 0.0, BIG_NEG).astype(np.float32)

    col = lambda name: OFF[name] // LANE
    full2 = lambda shape: pl.BlockSpec(shape, lambda b, g, i: (0, 0))
    q_spec = pl.BlockSpec((1, tq, LANE), lambda b, g, i: (b, i, col("aq") + g))
    c_specs = [pl.BlockSpec((1, 1, NC, LANE), lambda b, g, i, j=j: (j, b, 0, g)) for j in range(2)]
    kv_specs = [pl.BlockSpec((1, S, LANE), lambda b, g, i, c=col(n): (b, 0, c + g))
                for n in ("kvs", "kvw")]
    gl_spec = pl.BlockSpec((1, tq, LANE), lambda b, g, i: (b, i, col("ag")))
    return pl.pallas_call(
        functools.partial(_nsa_kernel, tq=tq, n_cmp=n_cmp, n_sel=n_sel, n_top=n_top),
        grid=(B, G, S // tq),
        in_specs=[q_spec] + c_specs + kv_specs + [gl_spec, full2((LANE, NC)), full2((S, LANE)),
                                                  full2((NSA_GROUP * tq, tq)), full2((NSA_GROUP * tq, tq))],
        out_specs=pl.BlockSpec((1, tq, LANE), lambda b, g, i: (b, i, g)),
        out_shape=jax.ShapeDtypeStruct((B, S, NSA_HEADS * HEAD_DIM), f32),
        scratch_shapes=[pltpu.VMEM((S, LANE), MXU)] * 4,
        compiler_params=_params(3),
        name="nsa_attention",
    )(h, kvc, kvc, h, h, h, jnp.asarray(cover_t, MXU), jnp.asarray(aug, MXU), jnp.asarray(causal),
      jnp.asarray(win_lo))


_SB_DEAD = -104.0


def _log_sigmoid(z):
    return jnp.minimum(z, 0.0) - jnp.log1p(jnp.exp(-jnp.abs(z)))


def _sb_kernel(q_ref, k_ref, v_ref, u_ref, o_ref, kb, vb, *, tq):
    tk = tq
    qi = pl.program_id(2)
    q0 = pl.multiple_of(qi * tq, tq)
    lane = lax.broadcasted_iota(jnp.int32, (tq, LANE), 1)
    lo_half = lane < HEAD_DIM

    @pl.when(qi == 0)
    def _():
        kb[...] = k_ref[0].astype(MXU)
        vb[...] = v_ref[0].astype(MXU)

    q = q_ref[0] * (HEAD_DIM ** -0.5)
    q2 = jnp.concatenate([jnp.where(lo_half, q, 0.0), jnp.where(lo_half, 0.0, q)], axis=0).astype(MXU)
    u = u_ref[...]

    def tile(carry, k, v, strict):
        c, acc = carry
        z = _dot_nt(q2, k)
        ls = _log_sigmoid(z)
        log_1m = ls - z
        if strict is not None:
            log_1m = jnp.where(strict, log_1m, 0.0)
        hi = log_1m.astype(MXU)
        lo = (log_1m - hi.astype(f32)).astype(MXU)
        tail = _dot(hi, u) + _dot(lo, u) + c
        a = jnp.exp(ls + tail)
        if strict is not None:
            a = jnp.where(strict, a, 0.0)
        return c + jnp.sum(log_1m, axis=1, keepdims=True), acc + _dot(a.astype(MXU), v)

    row = lax.broadcasted_iota(jnp.int32, (2 * tq, tk), 0)
    rel = jnp.where(row >= tq, row - tq, row) - lax.broadcasted_iota(jnp.int32, (2 * tq, tk), 1)
    zero = (jnp.zeros((2 * tq, 1), f32), jnp.zeros((2 * tq, LANE), f32))
    state = tile(zero, kb[pl.ds(q0, tk), :], vb[pl.ds(q0, tk), :], rel > 0)

    def alive(state):
        return (jnp.max(state[0]) > _SB_DEAD).astype(jnp.int32)

    def cond(loop):
        kt, live, _ = loop
        return (kt >= 0) & (live > 0)

    def body(loop):
        kt, _, state = loop
        k0 = pl.multiple_of(kt * tk, tk)
        state = tile(state, kb[pl.ds(k0, tk), :], vb[pl.ds(k0, tk), :], None)
        return kt - 1, alive(state), state

    _, _, (_, acc) = lax.while_loop(cond, body, (qi - 1, alive(state), state))
    o_ref[0] = jnp.where(lo_half, acc[:tq], acc[tq:])


def stick_breaking_attention(h, tq=_TQ_SB):
    B, S, _ = h.shape
    assert S % tq == 0
    tri = jnp.asarray(np.tril(np.ones((tq, tq), np.float32), -1), MXU)
    cq, ck, cv = (OFF[n] // LANE for n in ("bq", "bk", "bv"))
    return pl.pallas_call(
        functools.partial(_sb_kernel, tq=tq),
        grid=(B, SB_HEADS // 2, S // tq),
        in_specs=[pl.BlockSpec((1, tq, LANE), lambda b, p, i: (b, i, cq + p)),
                  pl.BlockSpec((1, S, LANE), lambda b, p, i: (b, 0, ck + p)),
                  pl.BlockSpec((1, S, LANE), lambda b, p, i: (b, 0, cv + p)),
                  pl.BlockSpec((tq, tq), lambda b, p, i: (0, 0))],
        out_specs=pl.BlockSpec((1, tq, LANE), lambda b, p, i: (b, i, p)),
        out_shape=jax.ShapeDtypeStruct((B, S, SB_HEADS * HEAD_DIM), f32),
        scratch_shapes=[pltpu.VMEM((S, LANE), MXU)] * 2,
        compiler_params=_params(3),
        name="sb_attention",
    )(h, h, h, tri)


def _neg_expm1(y):
    series = -y * (1.0 + y * (1.0 / 2 + y * (1.0 / 6 + y * (1.0 / 24 + y * (1.0 / 120)))))
    return jnp.where(y > -0.1, series, 1.0 - jnp.exp(y))


def _rglru_kernel(x_ref, xg_ref, cw_ref, cb_ref, gaw_ref, gab_ref, gxw_ref, gxb_ref, lam_ref, o_ref):
    x = x_ref[0]
    S = x.shape[0]
    row = lax.broadcasted_iota(jnp.int32, (S, 1), 0)

    def shifted(t, d, fill):
        return jnp.where(row >= d, pltpu.roll(t, d, 0), fill)

    u = cb_ref[...] + x * cw_ref[CONV_W - 1:CONV_W, :]
    for d in range(1, CONV_W):
        u = u + shifted(x, d, 0.0) * cw_ref[CONV_W - 1 - d:CONV_W - d, :]
    ub = u.astype(MXU)
    r = jax.nn.sigmoid(_dot(ub, gaw_ref[...]) + gab_ref[...])
    i = jax.nn.sigmoid(_dot(ub, gxw_ref[...]) + gxb_ref[...])
    lam = lam_ref[...]
    softplus_neg = jnp.maximum(-lam, 0.0) + jnp.log1p(jnp.exp(-jnp.abs(lam)))
    log_a = -LRU_C * r * softplus_neg
    a = jnp.exp(log_a)
    b = jnp.sqrt(_neg_expm1(2.0 * log_a)) * (i * u)
    d = 1
    while d < S:
        b = a * shifted(b, d, 0.0) + b
        a = a * shifted(a, d, 1.0)
        d *= 2
    o_ref[0] = b * jax.nn.gelu(xg_ref[0])


def _block_diag(w):
    n, c, _ = w.shape
    out = jnp.zeros((n * c, n * c), w.dtype)
    for j in range(n):
        out = out.at[j * c:(j + 1) * c, j * c:(j + 1) * c].set(w[j])
    return out


def rglru_block(h, conv_w, conv_b, ga_w, ga_b, gx_w, gx_b, lru_lambda):
    B, S, _ = h.shape
    W = RNN_W
    cx, cg = OFF["cx"] // W, OFF["cg"] // W
    vec = pl.BlockSpec((1, W), lambda b: (0, 0))
    mat = pl.BlockSpec((W, W), lambda b: (0, 0))
    return pl.pallas_call(
        _rglru_kernel,
        grid=(B,),
        in_specs=[pl.BlockSpec((1, S, W), lambda b: (b, 0, cx)), pl.BlockSpec((1, S, W), lambda b: (b, 0, cg)),
                  pl.BlockSpec((CONV_W, W), lambda b: (0, 0)), vec, mat, vec, mat, vec, vec],
        out_specs=pl.BlockSpec((1, S, W), lambda b: (b, 0, 0)),
        out_shape=jax.ShapeDtypeStruct((B, S, W), f32),
        compiler_params=_params(1),
        name="rglru",
    )(h, h, conv_w, conv_b.reshape(1, W), _block_diag(ga_w).astype(MXU), ga_b.reshape(1, W),
      _block_diag(gx_w).astype(MXU), gx_b.reshape(1, W), lru_lambda.reshape(1, W))


def _rms(x, g, width):
    return x * lax.rsqrt(jnp.sum(x * x, axis=-1, keepdims=True) * (1.0 / width) + RMS_EPS) * g


def _mla_prep_kernel(cq_ref, ckv_ref, kr_ref, krs_ref, gq_ref, gkv_ref, wq_ref, wqs_ref, wk_ref, wv_ref,
                     cosq_ref, sinq_ref, cosk_ref, sink_ref, vone_ref, q_ref, k_ref, v_ref):
    cq = _rms(cq_ref[0], gq_ref[...], MLA_Q_RANK).astype(MXU)
    ckv = _rms(ckv_ref[0], gkv_ref[...], MLA_KV_RANK).astype(MXU)
    scale = (MLA_NOPE + MLA_ROPE) ** -0.5
    q = _dot(cq, wq_ref[...]) * cosq_ref[...] + _dot(cq, wqs_ref[...]) * sinq_ref[...]
    q_ref[0] = (q * scale).astype(q_ref.dtype)
    k_rope = kr_ref[0] * cosk_ref[...] + krs_ref[0] * sink_ref[...]
    k = _dot(ckv, wk_ref[...])
    k_ref[0] = (k + jnp.concatenate([k_rope] * MLA_HEADS, axis=1)).astype(k_ref.dtype)
    v_ref[0] = (_dot(ckv, wv_ref[...]) + vone_ref[...]).astype(v_ref.dtype)


def _mla_attn_kernel(q_ref, k_ref, v_ref, cb_ref, o_ref, *, tq):
    tk = tq
    qi = pl.program_id(2)
    q0 = pl.multiple_of(qi * tq, tq)
    q = q_ref[0]
    lane = lax.broadcasted_iota(jnp.int32, (tq, LANE), 1)

    def update(carry, n, k, v, bias):
        m, acc = carry
        sl = slice(n * LANE, (n + 1) * LANE)
        s = _dot_nt(q[:, sl], k[:, sl])
        if bias is not None:
            s = s + bias
        m_new = jnp.maximum(m, jnp.max(s, axis=1, keepdims=True))
        p = jnp.exp(s - m_new)
        return m_new, jnp.exp(m - m_new) * acc + _dot(p.astype(MXU), v[:, sl])

    def body(kt, carry):
        k0 = pl.multiple_of(kt * tk, tk)
        k = k_ref[0, pl.ds(k0, tk), :]
        v = v_ref[0, pl.ds(k0, tk), :]
        return tuple(update(carry[n], n, k, v, None) for n in range(2))

    init = tuple((jnp.full((tq, 1), NEG, f32), jnp.zeros((tq, LANE), f32)) for _ in range(2))
    carry = lax.fori_loop(0, qi, body, init)
    k = k_ref[0, pl.ds(q0, tk), :]
    v = v_ref[0, pl.ds(q0, tk), :]
    o = []
    for n in range(2):
        _, acc = update(carry[n], n, k, v, cb_ref[...])
        denom = jnp.sum(jnp.where(lane == MLA_V, acc, 0.0), axis=1, keepdims=True)
        o.append(acc / jnp.maximum(denom, 1e-30))
    o_ref[0] = jnp.where(lane < MLA_V, o[0], pltpu.roll(o[1], MLA_V, 1))


def mla_attention(h, q_norm, kv_norm, w_uq, w_ukv, tr=_TR_MLA_PREP, tq=_TQ_MLA):
    B, S, _ = h.shape
    H = MLA_HEADS
    dq = MLA_NOPE + MLA_ROPE
    HW = H * LANE
    wq3 = w_uq.reshape(MLA_Q_RANK, H, dq)
    wq_rot = jnp.concatenate([jnp.zeros_like(wq3[..., :MLA_NOPE]), _rot_half(wq3[..., MLA_NOPE:])], axis=-1)

    def pad_q(w3):
        w3 = jnp.pad(w3, ((0, _CQ_PAD - MLA_Q_RANK), (0, 0), (0, LANE - dq)))
        return w3.reshape(_CQ_PAD, HW).astype(MXU)

    wkv3 = w_ukv.reshape(MLA_KV_RANK, H, MLA_NOPE + MLA_V)
    wk = jnp.pad(wkv3[..., :MLA_NOPE], ((0, 0), (0, 0), (0, LANE - MLA_NOPE))).reshape(MLA_KV_RANK, HW).astype(MXU)
    wv = jnp.pad(wkv3[..., MLA_NOPE:], ((0, 0), (0, 0), (0, LANE - MLA_V))).reshape(MLA_KV_RANK, HW).astype(MXU)
    v_one = jnp.tile(jnp.concatenate([jnp.zeros((1, MLA_V), f32), jnp.ones((1, LANE - MLA_V), f32)], axis=1), (1, H))
    gq = jnp.pad(q_norm, (0, _CQ_PAD - MLA_Q_RANK)).reshape(1, _CQ_PAD)
    gkv = kv_norm.reshape(1, MLA_KV_RANK)
    inv = ROPE_THETA ** (-jnp.arange(0, MLA_ROPE, 2, dtype=f32) / MLA_ROPE)
    ang = jnp.arange(S, dtype=f32)[:, None] * inv[None, :]
    cos2 = jnp.concatenate([jnp.cos(ang), jnp.cos(ang)], axis=1)
    sin2 = jnp.concatenate([jnp.sin(ang), jnp.sin(ang)], axis=1)
    tail = LANE - dq
    cos_k = jnp.concatenate([jnp.zeros((S, MLA_NOPE), f32), cos2, jnp.zeros((S, tail), f32)], axis=1)
    sin_k = jnp.concatenate([jnp.zeros((S, MLA_NOPE), f32), sin2, jnp.zeros((S, tail), f32)], axis=1)
    cos_q = jnp.tile(jnp.concatenate([jnp.ones((S, MLA_NOPE), f32), cos2, jnp.zeros((S, tail), f32)], axis=1), (1, H))
    sin_q = jnp.tile(sin_k, (1, H))
    rel = np.arange(tq)[:, None] - np.arange(tq)[None, :]
    causal = jnp.asarray(np.where(rel >= 0, 0.0, BIG_NEG).astype(np.float32))

    c_cq, c_ckv, c_kr, c_krs = OFF["dcq"] // _CQ_PAD, OFF["dckv"] // LANE, OFF["dkr"] // LANE, OFF["dkrs"] // LANE
    full = lambda shape: pl.BlockSpec(shape, lambda b, i: (0, 0))
    tab = lambda w: pl.BlockSpec((tr, w), lambda b, i: (i, 0))
    out3 = pl.BlockSpec((1, tr, HW), lambda b, i: (b, i, 0))
    q, k, v = pl.pallas_call(
        _mla_prep_kernel,
        grid=(B, S // tr),
        in_specs=[pl.BlockSpec((1, tr, _CQ_PAD), lambda b, i: (b, i, c_cq)),
                  pl.BlockSpec((1, tr, LANE), lambda b, i: (b, i, c_ckv)),
                  pl.BlockSpec((1, tr, LANE), lambda b, i: (b, i, c_kr)),
                  pl.BlockSpec((1, tr, LANE), lambda b, i: (b, i, c_krs)),
                  full((1, _CQ_PAD)), full((1, MLA_KV_RANK)), full((_CQ_PAD, HW)), full((_CQ_PAD, HW)),
                  full((MLA_KV_RANK, HW)), full((MLA_KV_RANK, HW)),
                  tab(HW), tab(HW), tab(LANE), tab(LANE), full((1, HW))],
        out_specs=[out3, out3, out3],
        out_shape=[jax.ShapeDtypeStruct((B, S, HW), MXU)] * 3,
        compiler_params=_params(2),
        name="mla_prep",
    )(h, h, h, h, gq, gkv, pad_q(wq3), pad_q(wq_rot), wk, wv, cos_q, sin_q, cos_k, sin_k, v_one)
    pair = lambda rows: pl.BlockSpec((1, rows, 2 * LANE), lambda b, p, i: (b, i if rows == tq else 0, p))
    return pl.pallas_call(
        functools.partial(_mla_attn_kernel, tq=tq),
        grid=(B, H // 2, S // tq),
        in_specs=[pair(tq), pair(S), pair(S), pl.BlockSpec((tq, tq), lambda b, p, i: (0, 0))],
        out_specs=pl.BlockSpec((1, tq, LANE), lambda b, p, i: (b, i, p)),
        out_shape=jax.ShapeDtypeStruct((B, S, H * MLA_V), f32),
        compiler_params=_params(3),
        name="mla_attention",
    )(q, k, v, causal)


def _ln(z, g, b):
    mu = jnp.mean(z, axis=-1, keepdims=True)
    zc = z - mu
    var = jnp.mean(zc * zc, axis=-1, keepdims=True)
    return zc * lax.rsqrt(var + LN_EPS) * g + b


def _merge_kernel(mg_ref, oa_ref, ob_ref, oc_ref, od_ref, wb_ref, wo_ref, x_ref, g_ref, b_ref, o_ref):
    acc = None
    for n, br in enumerate((oa_ref, ob_ref, oc_ref, od_ref)):
        up = _dot(br[...].astype(MXU), wb_ref[n])
        term = jax.nn.sigmoid(mg_ref[:, n * D_MODEL:(n + 1) * D_MODEL]) * up
        acc = term if acc is None else acc + term
    y = _dot(acc.astype(MXU), wo_ref[...])
    o_ref[...] = _ln(DN_ALPHA * x_ref[...] + y, g_ref[...], b_ref[...])


def merge_ln(h2, branches, w_branch, w_out, x2, g, b, tm=_TM_DENSE):
    N, D = x2.shape
    assert OFF["mg"] == 0 and N % tm == 0
    row = lambda w: pl.BlockSpec((tm, w), lambda i: (i, 0))
    return pl.pallas_call(
        _merge_kernel,
        grid=(N // tm,),
        in_specs=[row(N_MIXERS * D)] + [row(MIX_W)] * N_MIXERS
        + [pl.BlockSpec((N_MIXERS, MIX_W, D), lambda i: (0, 0, 0)), pl.BlockSpec((D, D), lambda i: (0, 0)),
           row(D), pl.BlockSpec((1, D), lambda i: (0, 0)), pl.BlockSpec((1, D), lambda i: (0, 0))],
        out_specs=row(D),
        out_shape=jax.ShapeDtypeStruct((N, D), f32),
        compiler_params=_params(1),
        name="merge_ln",
    )(h2, *branches, w_branch.astype(MXU), w_out.astype(MXU), x2, g.reshape(1, D), b.reshape(1, D))


ROW_TILE = 8


def _to_token_tiles(ref, val):
    rows = val.shape[0]
    for j in range(ROW_TILE):
        ref[pl.ds(j, rows, stride=ROW_TILE), :] = val[:, j * LANE:(j + 1) * LANE]


def _from_token_tiles(ref, rows, first=0, stride=ROW_TILE):
    return jnp.concatenate([ref[pl.ds(first + j, rows, stride=stride), :] for j in range(ROW_TILE)], axis=1)


def _xattn_kernel(x_ref, wq_ref, k_ref, v_ref, wo_ref, g_ref, b_ref, o_ref, o8_ref):
    x = x_ref[0]
    q = _dot(x.astype(MXU), wq_ref[...]).astype(MXU)
    k = k_ref[0]
    v = v_ref[0]
    heads = []
    for hd in range(X_HEADS):
        sl = slice(hd * X_HEAD_DIM, (hd + 1) * X_HEAD_DIM)
        s = _dot_nt(q[:, sl], k[:, sl]) * (X_HEAD_DIM ** -0.5)
        e = jnp.exp(s - jnp.max(s, axis=1, keepdims=True))
        p = e / jnp.sum(e, axis=1, keepdims=True)
        heads.append(_dot(p.astype(MXU), v[:, sl]).astype(MXU))
    y = _dot(jnp.concatenate(heads, axis=1), wo_ref[...])
    out = _ln(DN_ALPHA * x + y, g_ref[...], b_ref[...])
    o_ref[0] = out
    _to_token_tiles(o8_ref.at[0], out)


def cross_attention_ln(x, mem, wq, wkv, wo, g, b, tq=_TM_DENSE):
    B, S, D = x.shape
    assert D == ROW_TILE * LANE
    M = mem.shape[1]
    F = X_HEADS * X_HEAD_DIM
    kv = matmul(mem.reshape(B * M, D), wkv.astype(MXU)[None], 0, 512, 2 * F, out_dtype=MXU).reshape(B, M, 2 * F)
    full = lambda shape: pl.BlockSpec(shape, lambda bi, i: (0,) * len(shape))
    return pl.pallas_call(
        _xattn_kernel,
        grid=(B, S // tq),
        in_specs=[pl.BlockSpec((1, tq, D), lambda bi, i: (bi, i, 0)), full((D, F)),
                  pl.BlockSpec((1, M, F), lambda bi, i: (bi, 0, 0)), pl.BlockSpec((1, M, F), lambda bi, i: (bi, 0, 1)),
                  full((F, D)), full((1, D)), full((1, D))],
        out_specs=[pl.BlockSpec((1, tq, D), lambda bi, i: (bi, i, 0)),
                   pl.BlockSpec((1, tq * ROW_TILE, LANE), lambda bi, i: (bi, i, 0))],
        out_shape=[jax.ShapeDtypeStruct((B, S, D), f32), jax.ShapeDtypeStruct((B, S * ROW_TILE, LANE), f32)],
        compiler_params=_params(2),
        name="cross_attention_ln",
    )(x, wq.astype(MXU), kv, kv, wo.astype(MXU), g.reshape(1, D), b.reshape(1, D))


_R_E0, _R_E1, _R_W0, _R_W1, _R_RANK0, _R_RANK1 = range(6)
_GRP_LANE0 = N_EXPERTS


def _router_kernel(x_ref, w_ref, b_ref, tri_ref, r_ref, cnt_ref):
    i = pl.program_id(0)
    tm = x_ref.shape[0]
    logits = _dot_nt(w_ref[...], x_ref[...].astype(MXU)) + b_ref[...]
    row = lax.broadcasted_iota(jnp.int32, (LANE, tm), 0)
    row_f = row.astype(f32)
    big = float(LANE)

    def cmax(t):
        return jnp.max(t, axis=0, keepdims=True)

    def first_row(cond):
        return jnp.min(jnp.where(cond, row_f, big), axis=0, keepdims=True)

    def softmax_on(mask):
        lm = jnp.where(mask, logits, NEG)
        e = jnp.where(mask, jnp.exp(lm - cmax(lm)), 0.0)
        return e / jnp.sum(e, axis=0, keepdims=True)

    is_g = (row >= _GRP_LANE0) & (row < _GRP_LANE0 + N_GROUPS)
    p_grp = softmax_on(is_g)
    p_g = cmax(p_grp)
    grp = first_row(is_g & (p_grp == p_g)) - float(_GRP_LANE0)
    grp_of_row = jnp.right_shift(row, EXPERTS_PER_GROUP.bit_length() - 1)
    in_grp = (row < N_EXPERTS) & (grp_of_row.astype(f32) == grp)
    p_e = softmax_on(in_grp)
    p1 = cmax(jnp.where(in_grp, p_e, -1.0))
    e1 = first_row(in_grp & (p_e == p1))
    rest = in_grp & (row_f != e1)
    p2 = cmax(jnp.where(rest, p_e, -1.0))
    e2 = first_row(rest & (p_e == p2))
    w1 = p_g * p1 / (p1 + p2)
    w2 = p_g * p2 / (p1 + p2)

    @pl.when(i == 0)
    def _():
        cnt_ref[...] = jnp.zeros_like(cnt_ref)

    oh1 = row_f == e1
    oh2 = row_f == e2
    both = (oh1 | oh2).astype(MXU)
    before = _dot(both, tri_ref[...]) + cnt_ref[:, 0:1]
    rank1 = jnp.sum(jnp.where(oh1, before, 0.0), axis=0, keepdims=True)
    rank2 = jnp.sum(jnp.where(oh2, before, 0.0), axis=0, keepdims=True)
    cnt_ref[...] = cnt_ref[...] + jnp.sum(both.astype(f32), axis=1, keepdims=True)

    out = jnp.zeros((LANE, tm), f32)
    for slot, val in ((_R_E0, e1), (_R_E1, e2), (_R_W0, w1), (_R_W1, w2), (_R_RANK0, rank1), (_R_RANK1, rank2)):
        out = jnp.where(row == slot, val, out)
    r_ref[...] = out.T


def moe_router(x2, rg_w, rg_b, re_w, re_b, tm=_TM_ROWS):
    N, D = x2.shape
    assert EXPERTS_PER_GROUP & (EXPERTS_PER_GROUP - 1) == 0 and N_EXPERTS + N_GROUPS <= LANE
    w = jnp.pad(jnp.concatenate([re_w, rg_w], axis=1), ((0, 0), (0, LANE - N_EXPERTS - N_GROUPS))).T.astype(MXU)
    b = jnp.pad(jnp.concatenate([re_b, rg_b]), (0, LANE - N_EXPERTS - N_GROUPS)).reshape(LANE, 1)
    tri = jnp.asarray(np.triu(np.ones((tm, tm), np.float32), 1), MXU)
    return pl.pallas_call(
        _router_kernel,
        grid=(N // tm,),
        in_specs=[pl.BlockSpec((tm, D), lambda i: (i, 0)), pl.BlockSpec((LANE, D), lambda i: (0, 0)),
                  pl.BlockSpec((LANE, 1), lambda i: (0, 0)), pl.BlockSpec((tm, tm), lambda i: (0, 0))],
        out_specs=[pl.BlockSpec((tm, LANE), lambda i: (i, 0)), pl.BlockSpec((LANE, LANE), lambda i: (0, 0))],
        out_shape=[jax.ShapeDtypeStruct((N, LANE), f32), jax.ShapeDtypeStruct((LANE, LANE), f32)],
        compiler_params=_params(1),
        name="moe_router",
    )(x2, w, b, tri)


def _ffn_kernel(ce_ref, first_ref, slot_ref, next_ref, nu_ref, x_ref, wgu_hbm, wd_hbm, o_ref,
                wgu_f32, wd_f32, wgu_b, wd_b, sems, *, layer):
    c = pl.program_id(0)
    used = c < nu_ref[0]

    def fetch(e, slot):
        return (pltpu.make_async_copy(wgu_hbm.at[layer, e], wgu_f32.at[slot], sems.at[slot, 0]),
                pltpu.make_async_copy(wd_hbm.at[layer, e], wd_f32.at[slot], sems.at[slot, 1]))

    @pl.when(used & (c == 0))
    def _():
        for cp in fetch(ce_ref[0], 0):
            cp.start()

    @pl.when(used & (first_ref[c] == 1))
    def _():
        slot = slot_ref[c]
        for cp in fetch(ce_ref[c], slot):
            cp.wait()

        @pl.when(next_ref[c] >= 0)
        def _():
            for cp in fetch(next_ref[c], 1 - slot):
                cp.start()

        wgu_b[...] = wgu_f32[slot].astype(MXU)
        wd_b[...] = wd_f32[slot].astype(MXU)

    @pl.when(used)
    def _():
        x = _from_token_tiles(x_ref, EXPERT_CHUNK)
        gu = _dot(x.astype(MXU), wgu_b[...])
        hid = jax.nn.silu(gu[:, :D_EXPERT]) * gu[:, D_EXPERT:]
        _to_token_tiles(o_ref, _dot(hid.astype(MXU), wd_b[...]))

    @pl.when(jnp.logical_not(used))
    def _():
        o_ref[...] = jnp.zeros_like(o_ref)


def expert_ffn(xb8, chunk_e, n_used, w_gu, w_down, layer):
    C = EXPERT_CHUNK
    n_chunks = xb8.shape[0] // (C * ROW_TILE)
    D = ROW_TILE * LANE
    idx = jnp.arange(n_chunks, dtype=jnp.int32)
    used = idx < n_used[0]
    first = used & ((idx == 0) | (chunk_e != jnp.roll(chunk_e, 1)))
    slot = (jnp.cumsum(first.astype(jnp.int32)) - 1) % 2
    none = jnp.int32(N_EXPERTS)
    later = jnp.roll(jnp.where(first, chunk_e, none), -1).at[-1].set(none)
    next_e = lax.cummin(later, axis=0, reverse=True)
    next_e = jnp.where(next_e == none, -1, next_e)
    tile_spec = pl.BlockSpec((C * ROW_TILE, LANE), lambda c, *_: (c, 0))
    any_spec = pl.BlockSpec(memory_space=pl.ANY)
    grid_spec = pltpu.PrefetchScalarGridSpec(
        num_scalar_prefetch=5,
        grid=(n_chunks,),
        in_specs=[tile_spec, any_spec, any_spec],
        out_specs=tile_spec,
        scratch_shapes=[pltpu.VMEM((2, D, 2 * D_EXPERT), w_gu.dtype), pltpu.VMEM((2, D_EXPERT, D), w_down.dtype),
                        pltpu.VMEM((D, 2 * D_EXPERT), MXU), pltpu.VMEM((D_EXPERT, D), MXU),
                        pltpu.SemaphoreType.DMA((2, 2))],
    )
    return pl.pallas_call(
        functools.partial(_ffn_kernel, layer=layer),
        grid_spec=grid_spec,
        out_shape=jax.ShapeDtypeStruct(xb8.shape, f32),
        compiler_params=_params(1),
        name="expert_ffn",
    )(chunk_e, first.astype(jnp.int32), slot.astype(jnp.int32), next_e.astype(jnp.int32), n_used, xb8, w_gu, w_down)


_COPY_WINDOW = 512


def _windowed_copies(n, copy, wait):
    assert _COPY_WINDOW & (_COPY_WINDOW - 1) == 0 and n >= _COPY_WINDOW

    def fill(j, carry):
        copy(j, j).start()
        return carry

    def steady(j, carry):
        s = jnp.bitwise_and(j, _COPY_WINDOW - 1)
        wait(s)
        copy(j, s).start()
        return carry

    def drain(s, carry):
        wait(s)
        return carry

    lax.fori_loop(0, _COPY_WINDOW, fill, 0, unroll=8)
    lax.fori_loop(_COPY_WINDOW, n, steady, 0, unroll=8)
    lax.fori_loop(0, _COPY_WINDOW, drain, 0, unroll=8)


def _dispatch_kernel(dest_ref, x8_ref, init_hbm, xb_hbm, sems, *, tm):
    n = tm * TOPK_IN_GROUP
    base = pl.program_id(0) * n

    def copy(j, s):
        t = pl.multiple_of(jnp.right_shift(j, 1) * ROW_TILE, ROW_TILE)
        d = pl.multiple_of(dest_ref[base + j] * ROW_TILE, ROW_TILE)
        return pltpu.make_async_copy(x8_ref.at[pl.ds(t, ROW_TILE)], xb_hbm.at[pl.ds(d, ROW_TILE)], sems.at[s])

    def wait(s):
        pltpu.make_async_copy(x8_ref.at[pl.ds(0, ROW_TILE)], xb_hbm.at[pl.ds(0, ROW_TILE)], sems.at[s]).wait()

    _windowed_copies(n, copy, wait)


def moe_dispatch(x8, dest, n_slots, tm=_TM_DISPATCH):
    n_tok = x8.shape[0] // ROW_TILE
    assert n_tok % tm == 0 and tm * TOPK_IN_GROUP >= _COPY_WINDOW
    grid_spec = pltpu.PrefetchScalarGridSpec(
        num_scalar_prefetch=1,
        grid=(n_tok // tm,),
        in_specs=[pl.BlockSpec((tm * ROW_TILE, LANE), lambda i, dest: (i, 0)), pl.BlockSpec(memory_space=pl.ANY)],
        out_specs=pl.BlockSpec(memory_space=pl.ANY),
        scratch_shapes=[pltpu.SemaphoreType.DMA((_COPY_WINDOW,))],
    )
    return pl.pallas_call(
        functools.partial(_dispatch_kernel, tm=tm),
        grid_spec=grid_spec,
        out_shape=jax.ShapeDtypeStruct((n_slots * ROW_TILE, LANE), x8.dtype),
        input_output_aliases={2: 0},
        compiler_params=_params(1),
        name="moe_dispatch",
    )(dest, x8, jnp.zeros((n_slots * ROW_TILE, LANE), x8.dtype))


def _combine_kernel(dest_ref, x_ref, r_ref, g_ref, b_ref, yb_hbm, o_ref, buf, sems):
    tm = x_ref.shape[0]
    n = tm * TOPK_IN_GROUP
    i = pl.program_id(0)

    def issue(tile):
        dst_buf = buf.at[jnp.bitwise_and(tile, 1)]

        def body(j, carry):
            src = pl.multiple_of(dest_ref[tile * n + j] * ROW_TILE, ROW_TILE)
            dst = pl.multiple_of(j * ROW_TILE, ROW_TILE)
            pltpu.make_async_copy(yb_hbm.at[pl.ds(src, ROW_TILE)], dst_buf.at[pl.ds(dst, ROW_TILE)],
                                  sems.at[j]).start()
            return carry

        lax.fori_loop(0, n, body, 0, unroll=8)

    def drain(j, carry):
        pltpu.make_async_copy(yb_hbm.at[pl.ds(0, ROW_TILE)], buf.at[0, pl.ds(0, ROW_TILE)], sems.at[j]).wait()
        return carry

    @pl.when(i == 0)
    def _():
        issue(i)

    lax.fori_loop(0, n, drain, 0, unroll=8)

    @pl.when(i + 1 < pl.num_programs(0))
    def _():
        issue(i + 1)

    cur = buf.at[jnp.bitwise_and(i, 1)]
    r = r_ref[...]
    lane = lax.broadcasted_iota(jnp.int32, r.shape, 1)
    w0 = jnp.sum(jnp.where(lane == _R_W0, r, 0.0), axis=1, keepdims=True)
    w1 = jnp.sum(jnp.where(lane == _R_W1, r, 0.0), axis=1, keepdims=True)
    pair = TOPK_IN_GROUP * ROW_TILE
    y = _from_token_tiles(cur, tm, 0, pair) * w0 + _from_token_tiles(cur, tm, ROW_TILE, pair) * w1
    o_ref[...] = _ln(DN_ALPHA * x_ref[...] + y, g_ref[...], b_ref[...])


def combine_ln(x2, yb8, dest, r, g, b, tm=_TM_ROWS):
    N, D = x2.shape
    assert N % tm == 0
    row = lambda w: pl.BlockSpec((tm, w), lambda i, dest: (i, 0))
    vec = pl.BlockSpec((1, D), lambda i, dest: (0, 0))
    grid_spec = pltpu.PrefetchScalarGridSpec(
        num_scalar_prefetch=1,
        grid=(N // tm,),
        in_specs=[row(D), row(LANE), vec, vec, pl.BlockSpec(memory_space=pl.ANY)],
        out_specs=row(D),
        scratch_shapes=[pltpu.VMEM((2, tm * TOPK_IN_GROUP * ROW_TILE, LANE), f32),
                        pltpu.SemaphoreType.DMA((tm * TOPK_IN_GROUP,))],
    )
    return pl.pallas_call(
        _combine_kernel,
        grid_spec=grid_spec,
        out_shape=jax.ShapeDtypeStruct((N, D), f32),
        compiler_params=_params(1),
        name="moe_combine_ln",
    )(dest, x2, r, g.reshape(1, D), b.reshape(1, D), yb8)


def hier_moe_ln(x2, x8, rg_w, rg_b, re_w, re_b, w_gu, w_down, layer, g, b):
    N, D = x2.shape
    E, C, K = N_EXPERTS, EXPERT_CHUNK, TOPK_IN_GROUP
    A = N * K
    r, cnt = moe_router(x2, rg_w, rg_b, re_w, re_b)
    e = r[:, _R_E0:_R_E1 + 1].astype(jnp.int32)
    rank = r[:, _R_RANK0:_R_RANK1 + 1].astype(jnp.int32)
    counts = cnt[:E, 0].astype(jnp.int32)
    padded = (counts + C - 1) // C * C
    pad_end = jnp.cumsum(padded)
    below = jnp.arange(E, dtype=jnp.int32)[None, None, :] < e[..., None]
    dest = rank + jnp.sum(jnp.where(below, padded[None, None, :], 0), axis=-1)
    n_chunks = -(-(A + E * (C - 1)) // C)
    P = n_chunks * C
    chunk_start = jnp.arange(n_chunks, dtype=jnp.int32) * C
    chunk_e = jnp.minimum(jnp.sum((pad_end[None, :] <= chunk_start[:, None]).astype(jnp.int32), axis=1), E - 1)
    n_used = (pad_end[-1] // C).reshape(1).astype(jnp.int32)
    dest = dest.reshape(A).astype(jnp.int32)
    xb8 = moe_dispatch(x8, dest, P)
    yb8 = expert_ffn(xb8, chunk_e.astype(jnp.int32), n_used, w_gu, w_down, layer)
    return combine_ln(x2, yb8, dest, r, g, b)


def kernel(x, mem, w_in, nsa_cmp_pos, nsa_cmp_w1, nsa_cmp_w2, rnn_conv_w, rnn_conv_b, rnn_ga_w, rnn_ga_b,
           rnn_gx_w, rnn_gx_b, rnn_lambda, mla_q_norm, mla_kv_norm, mla_w_uq, mla_w_ukv, w_branch, w_out,
           ln1_g, ln1_b, x_wq, x_wkv, x_wo, ln2_g, ln2_b, moe_rg_w, moe_rg_b, moe_re_w, moe_re_b,
           moe_w_gu, moe_w_down, ln3_g, ln3_b):
    B, S, D = x.shape
    N = B * S
    x2 = x.reshape(N, D)
    w_cat = _cat_w_in(w_in)
    for l in range(DEPTH):
        h2 = matmul(x2, w_cat, l, 512, _IN_TILE)
        h = h2.reshape(B, S, N_CAT)
        o_a = nsa_attention(h, nsa_cmp_pos[l], nsa_cmp_w1[l], nsa_cmp_w2[l])
        o_b = stick_breaking_attention(h)
        o_c = rglru_block(h, rnn_conv_w[l], rnn_conv_b[l], rnn_ga_w[l], rnn_ga_b[l], rnn_gx_w[l], rnn_gx_b[l],
                          rnn_lambda[l])
        o_d = mla_attention(h, mla_q_norm[l], mla_kv_norm[l], mla_w_uq[l], mla_w_ukv[l])
        branches = [o.reshape(N, MIX_W) for o in (o_a, o_b, o_c, o_d)]
        x2 = merge_ln(h2, branches, w_branch[l], w_out[l], x2, ln1_g[l], ln1_b[l])
        x3, x8 = cross_attention_ln(x2.reshape(B, S, D), mem, x_wq[l], x_wkv[l], x_wo[l], ln2_g[l], ln2_b[l])
        x2 = hier_moe_ln(x3.reshape(N, D), x8.reshape(N * ROW_TILE, LANE), moe_rg_w[l], moe_rg_b[l], moe_re_w[l],
                         moe_re_b[l], moe_w_gu, moe_w_down, l, ln3_g[l], ln3_b[l])
    return x2.reshape(B, S, D)
```

```python
import functools

import numpy as np
import jax
import jax.numpy as jnp
from jax import lax
from jax.experimental import pallas as pl
from jax.experimental.pallas import tpu as pltpu

D_MODEL = 1024
DEPTH = 4
HEAD_DIM = 64
N_MIXERS = 4
MIX_W = 256
NSA_HEADS = 4
NSA_KV_HEADS = 2
NSA_GROUP = NSA_HEADS // NSA_KV_HEADS
CMP_LEN = 32
CMP_STRIDE = 16
CMP_HID = 256
SEL_BLOCK = 64
SEL_TOPK = 8
WINDOW = 512
FORCE_SCORE = 1e4
SB_HEADS = 4
RNN_W = 256
CONV_W = 4
LRU_C = 8.0
MLA_HEADS = 4
MLA_Q_RANK = 192
MLA_KV_RANK = 128
MLA_NOPE = 64
MLA_ROPE = 32
MLA_V = 64
ROPE_THETA = 10000.0
X_HEADS = 4
X_HEAD_DIM = 128
N_GROUPS = 4
EXPERTS_PER_GROUP = 8
N_EXPERTS = N_GROUPS * EXPERTS_PER_GROUP
TOPK_IN_GROUP = 2
D_EXPERT = 512
EXPERT_CHUNK = 256
DN_ALPHA = (2.0 * DEPTH) ** 0.25
LN_EPS = 1e-5
RMS_EPS = 1e-6

IN_SPLITS = ((NSA_HEADS * HEAD_DIM,) + (NSA_KV_HEADS * HEAD_DIM,) * 6 + (NSA_HEADS * 3,)
             + (SB_HEADS * HEAD_DIM,) * 3
             + (RNN_W, RNN_W)
             + (MLA_Q_RANK, MLA_KV_RANK, MLA_ROPE)
             + (N_MIXERS * D_MODEL,))
IN_OFFSETS = tuple(int(o) for o in np.concatenate([[0], np.cumsum(IN_SPLITS)[:-1]]))

LANE = 128
VMEM_LIMIT = 48 * 1024 * 1024
NEG = -1e30
BIG_NEG = -2.0 ** 100

f32 = jnp.float32
MXU = jnp.bfloat16

_CQ_PAD = 2 * LANE
_W_Q, _W_KV, _W_KVT = NSA_HEADS * HEAD_DIM, NSA_KV_HEADS * HEAD_DIM, NSA_KV_HEADS * LANE
_SECTIONS = (("mg", N_MIXERS * D_MODEL), ("aq", _W_Q), ("kvs", _W_KVT), ("kvw", _W_KVT),
             ("bq", SB_HEADS * HEAD_DIM), ("bk", SB_HEADS * HEAD_DIM), ("bv", SB_HEADS * HEAD_DIM),
             ("cx", RNN_W), ("cg", RNN_W), ("dcq", _CQ_PAD), ("kc", _W_KV), ("vc", _W_KV),
             ("ag", LANE), ("dckv", MLA_KV_RANK), ("dkr", LANE), ("dkrs", LANE))
_IN_TILE = 3584
OFF = {}
_o = 0
for _n, _w in _SECTIONS:
    OFF[_n] = _o
    _o += _w
N_CAT = -(-_o // _IN_TILE) * _IN_TILE

_SEL_ROWS = 32
_AUG_SEL = HEAD_DIM
_AUG_POS = HEAD_DIM + _SEL_ROWS
_POS_SPLIT = 256

_TQ_NSA = 512
_TQ_MLA = 512
_TQ_SB = 256
_TM_ROWS = 256
_TM_DENSE = 512
_TM_DISPATCH = 256
_TR_MLA_PREP = 512


def _rot_half(t):
    d = t.shape[-1]
    return jnp.concatenate([-t[..., d // 2:], t[..., :d // 2]], axis=-1)


def _cat_w_in(w):
    def sec(i):
        return w[..., IN_OFFSETS[i]:IN_OFFSETS[i] + IN_SPLITS[i]]

    def pair(k, v):
        return jnp.concatenate([k[..., :HEAD_DIM], v[..., :HEAD_DIM], k[..., HEAD_DIM:], v[..., HEAD_DIM:]], axis=-1)

    def pad_cols(t, before, after):
        return jnp.pad(t, ((0, 0),) * (t.ndim - 1) + ((before, after),))

    def padc(t, n):
        return pad_cols(t, 0, n - t.shape[-1])

    def rope_slot(t):
        return pad_cols(t, MLA_NOPE, LANE - MLA_NOPE - MLA_ROPE)

    parts = {"mg": sec(16), "aq": sec(0), "kc": sec(1), "vc": sec(2), "kvs": pair(sec(3), sec(4)),
             "kvw": pair(sec(5), sec(6)), "ag": padc(sec(7), LANE), "bq": sec(8), "bk": sec(9),
             "bv": sec(10), "cx": sec(11), "cg": sec(12), "dcq": padc(sec(13), _CQ_PAD), "dckv": sec(14),
             "dkr": rope_slot(sec(15)), "dkrs": rope_slot(_rot_half(sec(15)))}
    cat = jnp.concatenate([parts[n] for n, _ in _SECTIONS], axis=-1)
    return padc(cat, N_CAT).astype(MXU)


def _dot(a, b):
    return jnp.dot(a, b, preferred_element_type=f32)


def _dot_nt(a, b):
    return lax.dot_general(a, b, (((1,), (1,)), ((), ())), preferred_element_type=f32)


def _params(n_axes):
    return pltpu.CompilerParams(dimension_semantics=("arbitrary",) * n_axes, vmem_limit_bytes=VMEM_LIMIT)


def _mm_kernel(a_ref, b_ref, o_ref):
    o_ref[...] = _dot(a_ref[...].astype(MXU), b_ref[0]).astype(o_ref.dtype)


def matmul(a, b, layer, tm, tn, out_dtype=f32):
    M, K = a.shape
    _, _, N = b.shape
    assert M % tm == 0 and N % tn == 0
    return pl.pallas_call(
        _mm_kernel,
        grid=(N // tn, M // tm),
        in_specs=[pl.BlockSpec((tm, K), lambda j, i: (i, 0)),
                  pl.BlockSpec((1, K, tn), lambda j, i: (layer, 0, j))],
        out_specs=pl.BlockSpec((tm, tn), lambda j, i: (i, j)),
        out_shape=jax.ShapeDtypeStruct((M, N), out_dtype),
        compiler_params=_params(2),
        name="matmul",
    )(a, b)


def _cmp_kernel(t_ref, w1_ref, pos_ref, w2_ref, o_ref):
    nc = t_ref.shape[1] // CMP_STRIDE
    t = jnp.concatenate([t_ref[0, pl.ds(l, nc, stride=CMP_STRIDE), :] for l in range(CMP_STRIDE)], axis=1)
    half = t.shape[1]
    t = t.astype(MXU)
    y1 = _dot(t, w1_ref[0, :half, :])
    y2 = _dot(t, w1_ref[0, half:, :])
    pos = jnp.broadcast_to(pos_ref[0], (8, 2 * half)).astype(MXU)
    pc = _dot(pos, w1_ref[0])[0:1]
    nc = y2.shape[0]
    hid = y1 + pltpu.roll(y2, nc - 1, 0) + pc
    o_ref[0, 0] = _dot(jax.nn.gelu(hid).astype(MXU), w2_ref[0])


def nsa_compress(h, cmp_w1, cmp_pos, cmp_w2):
    B, S, _ = h.shape
    G = NSA_KV_HEADS
    NC = S // CMP_STRIDE
    F = CMP_STRIDE * G * HEAD_DIM
    assert G * HEAD_DIM == LANE and OFF["vc"] == OFF["kc"] + LANE
    col0 = OFF["kc"] // LANE
    def group_diag(w, axis):
        blocks = [jnp.concatenate([w if k == g else jnp.zeros_like(w) for k in range(G)], axis=-1) for g in range(G)]
        return jnp.stack(blocks, axis=axis)

    w1 = group_diag(cmp_w1.reshape(2, CMP_LEN, HEAD_DIM, CMP_HID).astype(MXU), 2)
    w1 = w1.reshape(2, 2 * F, G * CMP_HID)
    pos = jnp.broadcast_to(cmp_pos[:, :, None, :], (2, CMP_LEN, G, HEAD_DIM)).reshape(2, 1, 2 * F)
    w2 = group_diag(jnp.concatenate([cmp_w2, cmp_w2], axis=-1).astype(MXU), 1)
    w2 = w2.reshape(2, G * CMP_HID, G * LANE)
    return pl.pallas_call(
        _cmp_kernel,
        grid=(2, B),
        in_specs=[pl.BlockSpec((1, S, LANE), lambda j, i: (i, 0, col0 + j)),
                  pl.BlockSpec((1, 2 * F, G * CMP_HID), lambda j, i: (j, 0, 0)),
                  pl.BlockSpec((1, 1, 2 * F), lambda j, i: (j, 0, 0)),
                  pl.BlockSpec((1, G * CMP_HID, G * LANE), lambda j, i: (j, 0, 0))],
        out_specs=pl.BlockSpec((1, 1, NC, G * LANE), lambda j, i: (j, i, 0, 0)),
        out_shape=jax.ShapeDtypeStruct((2, B, NC, G * LANE), f32),
        compiler_params=_params(2),
        name="nsa_compress",
    )(h, w1, pos, w2)


def _nsa_kernel(q_ref, kc_ref, vc_ref, kvs_ref, kvw_ref, gl_ref, cover_ref, aug_ref, cb_ref, wb_ref,
                o_ref, ksa, vsa, kwa, vwa, *, tq, n_cmp, n_sel, n_top):
    tk = tq
    g = pl.program_id(1)
    qi = pl.program_id(2)
    q0 = pl.multiple_of(qi * tq, tq)
    lane = lax.broadcasted_iota(jnp.int32, (tq, LANE), 1)
    lo_half = lane < HEAD_DIM

    @pl.when(qi == 0)
    def _():
        real = lax.broadcasted_iota(jnp.int32, ksa.shape, 1) < HEAD_DIM
        aug = aug_ref[...]
        ones = jnp.ones(ksa.shape, MXU)
        for kv_ref, k_out, v_out in ((kvs_ref, ksa, vsa), (kvw_ref, kwa, vwa)):
            kv = kv_ref[0]
            k_out[...] = jnp.where(real, kv.astype(MXU), aug)
            v_out[...] = jnp.where(real, pltpu.roll(kv, HEAD_DIM, 1).astype(MXU), ones)

    q = q_ref[0] * (HEAD_DIM ** -0.5)
    q_heads = (jnp.where(lo_half, q, 0.0), jnp.where(lo_half, pltpu.roll(q, HEAD_DIM, 1), 0.0))
    alibi = [2.0 ** (-8.0 * (h + 1) / NSA_HEADS) for h in range(NSA_HEADS)]
    slopes = [jnp.where(g == 0, alibi[n], alibi[NSA_GROUP + n]) for n in range(NSA_GROUP)]
    pos_cols = [jnp.where(lane == _AUG_POS, slopes[n] * _POS_SPLIT, jnp.where(lane == _AUG_POS + 1, slopes[n], 0.0))
                for n in range(NSA_GROUP)]
    tpos = q0 + lax.broadcasted_iota(jnp.int32, (tq, 1), 0)

    nc = kc_ref.shape[2]
    cidx = lax.broadcasted_iota(jnp.int32, (1, nc), 1)
    dist_c = tpos - (cidx * CMP_STRIDE + (CMP_LEN - 1))
    mask_c = (dist_c >= 0) & (cidx < n_cmp)
    dist_cf = dist_c.astype(f32)
    kc = kc_ref[0, 0].astype(MXU)
    vc = vc_ref[0, 0].astype(MXU)
    o_cmp = []
    imp_t = jnp.zeros((LANE, tq), f32)
    for n in range(NSA_GROUP):
        s = _dot_nt(q_heads[n].astype(MXU), kc) - slopes[n] * dist_cf
        sm = jnp.where(mask_c, s, NEG)
        m = jnp.max(sm, axis=1, keepdims=True)
        p = jnp.where(mask_c, jnp.exp(sm - m), 0.0)
        p = (p / jnp.maximum(jnp.sum(p, axis=1, keepdims=True), 1e-30)).astype(MXU)
        o_cmp.append(_dot(p, vc))
        imp_t = imp_t + _dot_nt(cover_ref[...], p)

    rows = _SEL_ROWS
    imp = imp_t[:rows]
    blk = lax.broadcasted_iota(jnp.int32, (rows, tq), 0)
    blk_f = blk.astype(f32)
    tpos_t = q0 + lax.broadcasted_iota(jnp.int32, (1, tq), 1)
    forced = (blk == 0) | (blk == jnp.right_shift(tpos_t, SEL_BLOCK.bit_length() - 1))
    valid = blk * SEL_BLOCK <= tpos_t
    imp = jnp.where(forced, FORCE_SCORE, jnp.where(valid, imp, -1.0))
    imp = jnp.where(blk < n_sel, imp, NEG)
    sel_t = jnp.zeros((rows, tq), f32)
    for _ in range(n_top):
        m = jnp.max(imp, axis=0, keepdims=True)
        first = jnp.min(jnp.where(imp == m, blk_f, float(LANE)), axis=0, keepdims=True)
        pick = blk_f == first
        sel_t = jnp.where(pick, 1.0, sel_t)
        imp = jnp.where(pick, 2 * NEG, imp)
    sel = jnp.concatenate([sel_t, jnp.zeros((LANE - rows, tq), f32)], axis=0).T
    sel_bias = pltpu.roll(jnp.where(sel > 0.5, 0.0, BIG_NEG), _AUG_SEL, 1)
    sel_cols = jnp.where((lane >= _AUG_SEL) & (lane < _AUG_SEL + rows), sel_bias, 0.0)
    q_sel = jnp.concatenate([q_heads[n] + sel_cols + pos_cols[n] for n in range(NSA_GROUP)], axis=0).astype(MXU)
    q_win = jnp.concatenate([q_heads[n] + pos_cols[n] for n in range(NSA_GROUP)], axis=0).astype(MXU)
    rows_q = NSA_GROUP * tq
    denom_lane = lax.broadcasted_iota(jnp.int32, (rows_q, LANE), 1) == HEAD_DIM

    def update(carry, qs, k, v, bias):
        m, acc = carry
        s = _dot_nt(qs, k)
        if bias is not None:
            s = s + bias
        m_new = jnp.maximum(m, jnp.max(s, axis=1, keepdims=True))
        p = jnp.exp(s - m_new)
        return m_new, jnp.exp(m - m_new) * acc + _dot(p.astype(MXU), v)

    def finish(carry):
        _, acc = carry
        denom = jnp.sum(jnp.where(denom_lane, acc, 0.0), axis=1, keepdims=True)
        o = acc / jnp.maximum(denom, 1e-30)
        return [o[n * tq:(n + 1) * tq] for n in range(NSA_GROUP)]

    init = (jnp.full((rows_q, 1), NEG, f32), jnp.zeros((rows_q, LANE), f32))
    causal = cb_ref[...]

    def sel_body(kt, carry):
        k0 = pl.multiple_of(kt * tk, tk)
        return update(carry, q_sel, ksa[pl.ds(k0, tk), :], vsa[pl.ds(k0, tk), :], None)

    carry = lax.fori_loop(0, qi, sel_body, init)
    o_sel = finish(update(carry, q_sel, ksa[pl.ds(q0, tk), :], vsa[pl.ds(q0, tk), :], causal))

    carry = init
    n_back = WINDOW // tk
    for back in range(n_back, -1, -1):
        k0 = pl.multiple_of(jnp.maximum(qi - back, 0) * tk, tk)
        if back:
            off = jnp.where(qi >= back, 0.0, BIG_NEG)
            bias = wb_ref[...] + off if back == n_back else off
        else:
            bias = causal
        carry = update(carry, q_win, kwa[pl.ds(k0, tk), :], vwa[pl.ds(k0, tk), :], bias)
    o_win = finish(carry)

    sig = jax.nn.sigmoid(gl_ref[0])

    def gate(n, j):
        col = 3 * (NSA_GROUP * g + n) + j
        return jnp.sum(jnp.where(lane == col, sig, 0.0), axis=1, keepdims=True)

    o = [gate(n, 0) * o_cmp[n] + gate(n, 1) * o_sel[n] + gate(n, 2) * o_win[n] for n in range(NSA_GROUP)]
    o_ref[0] = jnp.where(lo_half, o[0], pltpu.roll(o[1], HEAD_DIM, 1))


def nsa_attention(h, cmp_pos, cmp_w1, cmp_w2, tq=_TQ_NSA):
    B, S, _ = h.shape
    G = NSA_KV_HEADS
    NC = S // CMP_STRIDE
    n_cmp = (S - CMP_LEN) // CMP_STRIDE + 1
    n_sel = S // SEL_BLOCK
    n_top = min(SEL_TOPK, n_sel)
    F = CMP_STRIDE * HEAD_DIM
    assert S % tq == 0 and WINDOW % tq == 0 and tq % SEL_BLOCK == 0 and n_sel <= _SEL_ROWS and S <= _POS_SPLIT ** 2

    kvc = nsa_compress(h, cmp_w1, cmp_pos, cmp_w2)

    c0 = np.arange(n_cmp)[:, None] * CMP_STRIDE
    j0 = np.arange(n_sel)[None, :] * SEL_BLOCK
    cover = np.clip(np.minimum(c0 + CMP_LEN, j0 + SEL_BLOCK) - np.maximum(c0, j0), 0, None) / CMP_LEN
    cover_t = np.zeros((LANE, NC), np.float32)
    cover_t[:n_sel, :n_cmp] = cover.T
    pos = np.arange(S)
    aug = np.zeros((S, LANE), np.float32)
    aug[pos, _AUG_SEL + pos // SEL_BLOCK] = 1.0
    aug[:, _AUG_POS] = pos // _POS_SPLIT
    aug[:, _AUG_POS + 1] = pos % _POS_SPLIT
    rel = np.arange(tq)[:, None] - np.arange(tq)[None, :]
    rel = np.tile(rel, (NSA_GROUP, 1))
    causal = np.where(rel >= 0, 0.0, BIG_NEG).astype(np.float32)
    win_lo = np.where(rel < 0, 0.0, BIG_NEG).astype(np.float32)

    col = lambda name: OFF[name] // LANE
    full2 = lambda shape: pl.BlockSpec(shape, lambda b, g, i: (0, 0))
    q_spec = pl.BlockSpec((1, tq, LANE), lambda b, g, i: (b, i, col("aq") + g))
    c_specs = [pl.BlockSpec((1, 1, NC, LANE), lambda b, g, i, j=j: (j, b, 0, g)) for j in range(2)]
    kv_specs = [pl.BlockSpec((1, S, LANE), lambda b, g, i, c=col(n): (b, 0, c + g))
                for n in ("kvs", "kvw")]
    gl_spec = pl.BlockSpec((1, tq, LANE), lambda b, g, i: (b, i, col("ag")))
    return pl.pallas_call(
        functools.partial(_nsa_kernel, tq=tq, n_cmp=n_cmp, n_sel=n_sel, n_top=n_top),
        grid=(B, G, S // tq),
        in_specs=[q_spec] + c_specs + kv_specs + [gl_spec, full2((LANE, NC)), full2((S, LANE)),
                                                  full2((NSA_GROUP * tq, tq)), full2((NSA_GROUP * tq, tq))],
        out_specs=pl.BlockSpec((1, tq, LANE), lambda b, g, i: (b, i, g)),
        out_shape=jax.ShapeDtypeStruct((B, S, NSA_HEADS * HEAD_DIM), f32),
        scratch_shapes=[pltpu.VMEM((S, LANE), MXU)] * 4,
        compiler_params=_params(3),
        name="nsa_attention",
    )(h, kvc, kvc, h, h, h, jnp.asarray(cover_t, MXU), jnp.asarray(aug, MXU), jnp.asarray(causal),
      jnp.asarray(win_lo))


_SB_DEAD = -104.0


def _log_sigmoid(z):
    return jnp.minimum(z, 0.0) - jnp.log1p(jnp.exp(-jnp.abs(z)))


def _sb_kernel(q_ref, k_ref, v_ref, u_ref, o_ref, kb, vb, *, tq):
    tk = tq
    qi = pl.program_id(2)
    q0 = pl.multiple_of(qi * tq, tq)
    lane = lax.broadcasted_iota(jnp.int32, (tq, LANE), 1)
    lo_half = lane < HEAD_DIM

    @pl.when(qi == 0)
    def _():
        kb[...] = k_ref[0].astype(MXU)
        vb[...] = v_ref[0].astype(MXU)

    q = q_ref[0] * (HEAD_DIM ** -0.5)
    q2 = jnp.concatenate([jnp.where(lo_half, q, 0.0), jnp.where(lo_half, 0.0, q)], axis=0).astype(MXU)
    u = u_ref[...]

    def tile(carry, k, v, strict):
        c, acc = carry
        z = _dot_nt(q2, k)
        ls = _log_sigmoid(z)
        log_1m = ls - z
        if strict is not None:
            log_1m = jnp.where(strict, log_1m, 0.0)
        hi = log_1m.astype(MXU)
        lo = (log_1m - hi.astype(f32)).astype(MXU)
        tail = _dot(hi, u) + _dot(lo, u) + c
        a = jnp.exp(ls + tail)
        if strict is not None:
            a = jnp.where(strict, a, 0.0)
        return c + jnp.sum(log_1m, axis=1, keepdims=True), acc + _dot(a.astype(MXU), v)

    row = lax.broadcasted_iota(jnp.int32, (2 * tq, tk), 0)
    rel = jnp.where(row >= tq, row - tq, row) - lax.broadcasted_iota(jnp.int32, (2 * tq, tk), 1)
    zero = (jnp.zeros((2 * tq, 1), f32), jnp.zeros((2 * tq, LANE), f32))
    state = tile(zero, kb[pl.ds(q0, tk), :], vb[pl.ds(q0, tk), :], rel > 0)

    def alive(state):
        return (jnp.max(state[0]) > _SB_DEAD).astype(jnp.int32)

    def cond(loop):
        kt, live, _ = loop
        return (kt >= 0) & (live > 0)

    def body(loop):
        kt, _, state = loop
        k0 = pl.multiple_of(kt * tk, tk)
        state = tile(state, kb[pl.ds(k0, tk), :], vb[pl.ds(k0, tk), :], None)
        return kt - 1, alive(state), state

    _, _, (_, acc) = lax.while_loop(cond, body, (qi - 1, alive(state), state))
    o_ref[0] = jnp.where(lo_half, acc[:tq], acc[tq:])


def stick_breaking_attention(h, tq=_TQ_SB):
    B, S, _ = h.shape
    assert S % tq == 0
    tri = jnp.asarray(np.tril(np.ones((tq, tq), np.float32), -1), MXU)
    cq, ck, cv = (OFF[n] // LANE for n in ("bq", "bk", "bv"))
    return pl.pallas_call(
        functools.partial(_sb_kernel, tq=tq),
        grid=(B, SB_HEADS // 2, S // tq),
        in_specs=[pl.BlockSpec((1, tq, LANE), lambda b, p, i: (b, i, cq + p)),
                  pl.BlockSpec((1, S, LANE), lambda b, p, i: (b, 0, ck + p)),
                  pl.BlockSpec((1, S, LANE), lambda b, p, i: (b, 0, cv + p)),
                  pl.BlockSpec((tq, tq), lambda b, p, i: (0, 0))],
        out_specs=pl.BlockSpec((1, tq, LANE), lambda b, p, i: (b, i, p)),
        out_shape=jax.ShapeDtypeStruct((B, S, SB_HEADS * HEAD_DIM), f32),
        scratch_shapes=[pltpu.VMEM((S, LANE), MXU)] * 2,
        compiler_params=_params(3),
        name="sb_attention",
    )(h, h, h, tri)


def _neg_expm1(y):
    series = -y * (1.0 + y * (1.0 / 2 + y * (1.0 / 6 + y * (1.0 / 24 + y * (1.0 / 120)))))
    return jnp.where(y > -0.1, series, 1.0 - jnp.exp(y))


def _rglru_kernel(x_ref, xg_ref, cw_ref, cb_ref, gaw_ref, gab_ref, gxw_ref, gxb_ref, lam_ref, o_ref):
    x = x_ref[0]
    S = x.shape[0]
    row = lax.broadcasted_iota(jnp.int32, (S, 1), 0)

    def shifted(t, d, fill):
        return jnp.where(row >= d, pltpu.roll(t, d, 0), fill)

    u = cb_ref[...] + x * cw_ref[CONV_W - 1:CONV_W, :]
    for d in range(1, CONV_W):
        u = u + shifted(x, d, 0.0) * cw_ref[CONV_W - 1 - d:CONV_W - d, :]
    ub = u.astype(MXU)
    r = jax.nn.sigmoid(_dot(ub, gaw_ref[...]) + gab_ref[...])
    i = jax.nn.sigmoid(_dot(ub, gxw_ref[...]) + gxb_ref[...])
    lam = lam_ref[...]
    softplus_neg = jnp.maximum(-lam, 0.0) + jnp.log1p(jnp.exp(-jnp.abs(lam)))
    log_a = -LRU_C * r * softplus_neg
    a = jnp.exp(log_a)
    b = jnp.sqrt(_neg_expm1(2.0 * log_a)) * (i * u)
    d = 1
    while d < S:
        b = a * shifted(b, d, 0.0) + b
        a = a * shifted(a, d, 1.0)
        d *= 2
    o_ref[0] = b * jax.nn.gelu(xg_ref[0])


def _block_diag(w):
    n, c, _ = w.shape
    out = jnp.zeros((n * c, n * c), w.dtype)
    for j in range(n):
        out = out.at[j * c:(j + 1) * c, j * c:(j + 1) * c].set(w[j])
    return out


def rglru_block(h, conv_w, conv_b, ga_w, ga_b, gx_w, gx_b, lru_lambda):
    B, S, _ = h.shape
    W = RNN_W
    cx, cg = OFF["cx"] // W, OFF["cg"] // W
    vec = pl.BlockSpec((1, W), lambda b: (0, 0))
    mat = pl.BlockSpec((W, W), lambda b: (0, 0))
    return pl.pallas_call(
        _rglru_kernel,
        grid=(B,),
        in_specs=[pl.BlockSpec((1, S, W), lambda b: (b, 0, cx)), pl.BlockSpec((1, S, W), lambda b: (b, 0, cg)),
                  pl.BlockSpec((CONV_W, W), lambda b: (0, 0)), vec, mat, vec, mat, vec, vec],
        out_specs=pl.BlockSpec((1, S, W), lambda b: (b, 0, 0)),
        out_shape=jax.ShapeDtypeStruct((B, S, W), f32),
        compiler_params=_params(1),
        name="rglru",
    )(h, h, conv_w, conv_b.reshape(1, W), _block_diag(ga_w).astype(MXU), ga_b.reshape(1, W),
      _block_diag(gx_w).astype(MXU), gx_b.reshape(1, W), lru_lambda.reshape(1, W))


def _rms(x, g, width):
    return x * lax.rsqrt(jnp.sum(x * x, axis=-1, keepdims=True) * (1.0 / width) + RMS_EPS) * g


def _mla_prep_kernel(cq_ref, ckv_ref, kr_ref, krs_ref, gq_ref, gkv_ref, wq_ref, wqs_ref, wk_ref, wv_ref,
                     cosq_ref, sinq_ref, cosk_ref, sink_ref, vone_ref, q_ref, k_ref, v_ref):
    cq = _rms(cq_ref[0], gq_ref[...], MLA_Q_RANK).astype(MXU)
    ckv = _rms(ckv_ref[0], gkv_ref[...], MLA_KV_RANK).astype(MXU)
    scale = (MLA_NOPE + MLA_ROPE) ** -0.5
    q = _dot(cq, wq_ref[...]) * cosq_ref[...] + _dot(cq, wqs_ref[...]) * sinq_ref[...]
    q_ref[0] = (q * scale).astype(q_ref.dtype)
    k_rope = kr_ref[0] * cosk_ref[...] + krs_ref[0] * sink_ref[...]
    k = _dot(ckv, wk_ref[...])
    k_ref[0] = (k + jnp.concatenate([k_rope] * MLA_HEADS, axis=1)).astype(k_ref.dtype)
    v_ref[0] = (_dot(ckv, wv_ref[...]) + vone_ref[...]).astype(v_ref.dtype)


def _mla_attn_kernel(q_ref, k_ref, v_ref, cb_ref, o_ref, *, tq):
    tk = tq
    qi = pl.program_id(2)
    q0 = pl.multiple_of(qi * tq, tq)
    q = q_ref[0]
    lane = lax.broadcasted_iota(jnp.int32, (tq, LANE), 1)

    def update(carry, n, k, v, bias):
        m, acc = carry
        sl = slice(n * LANE, (n + 1) * LANE)
        s = _dot_nt(q[:, sl], k[:, sl])
        if bias is not None:
            s = s + bias
        m_new = jnp.maximum(m, jnp.max(s, axis=1, keepdims=True))
        p = jnp.exp(s - m_new)
        return m_new, jnp.exp(m - m_new) * acc + _dot(p.astype(MXU), v[:, sl])

    def body(kt, carry):
        k0 = pl.multiple_of(kt * tk, tk)
        k = k_ref[0, pl.ds(k0, tk), :]
        v = v_ref[0, pl.ds(k0, tk), :]
        return tuple(update(carry[n], n, k, v, None) for n in range(2))

    init = tuple((jnp.full((tq, 1), NEG, f32), jnp.zeros((tq, LANE), f32)) for _ in range(2))
    carry = lax.fori_loop(0, qi, body, init)
    k = k_ref[0, pl.ds(q0, tk), :]
    v = v_ref[0, pl.ds(q0, tk), :]
    o = []
    for n in range(2):
        _, acc = update(carry[n], n, k, v, cb_ref[...])
        denom = jnp.sum(jnp.where(lane == MLA_V, acc, 0.0), axis=1, keepdims=True)
        o.append(acc / jnp.maximum(denom, 1e-30))
    o_ref[0] = jnp.where(lane < MLA_V, o[0], pltpu.roll(o[1], MLA_V, 1))


def mla_attention(h, q_norm, kv_norm, w_uq, w_ukv, tr=_TR_MLA_PREP, tq=_TQ_MLA):
    B, S, _ = h.shape
    H = MLA_HEADS
    dq = MLA_NOPE + MLA_ROPE
    HW = H * LANE
    wq3 = w_uq.reshape(MLA_Q_RANK, H, dq)
    wq_rot = jnp.concatenate([jnp.zeros_like(wq3[..., :MLA_NOPE]), _rot_half(wq3[..., MLA_NOPE:])], axis=-1)

    def pad_q(w3):
        w3 = jnp.pad(w3, ((0, _CQ_PAD - MLA_Q_RANK), (0, 0), (0, LANE - dq)))
        return w3.reshape(_CQ_PAD, HW).astype(MXU)

    wkv3 = w_ukv.reshape(MLA_KV_RANK, H, MLA_NOPE + MLA_V)
    wk = jnp.pad(wkv3[..., :MLA_NOPE], ((0, 0), (0, 0), (0, LANE - MLA_NOPE))).reshape(MLA_KV_RANK, HW).astype(MXU)
    wv = jnp.pad(wkv3[..., MLA_NOPE:], ((0, 0), (0, 0), (0, LANE - MLA_V))).reshape(MLA_KV_RANK, HW).astype(MXU)
    v_one = jnp.tile(jnp.concatenate([jnp.zeros((1, MLA_V), f32), jnp.ones((1, LANE - MLA_V), f32)], axis=1), (1, H))
    gq = jnp.pad(q_norm, (0, _CQ_PAD - MLA_Q_RANK)).reshape(1, _CQ_PAD)
    gkv = kv_norm.reshape(1, MLA_KV_RANK)
    inv = ROPE_THETA ** (-jnp.arange(0, MLA_ROPE, 2, dtype=f32) / MLA_ROPE)
    ang = jnp.arange(S, dtype=f32)[:, None] * inv[None, :]
    cos2 = jnp.concatenate([jnp.cos(ang), jnp.cos(ang)], axis=1)
    sin2 = jnp.concatenate([jnp.sin(ang), jnp.sin(ang)], axis=1)
    tail = LANE - dq
    cos_k = jnp.concatenate([jnp.zeros((S, MLA_NOPE), f32), cos2, jnp.zeros((S, tail), f32)], axis=1)
    sin_k = jnp.concatenate([jnp.zeros((S, MLA_NOPE), f32), sin2, jnp.zeros((S, tail), f32)], axis=1)
    cos_q = jnp.tile(jnp.concatenate([jnp.ones((S, MLA_NOPE), f32), cos2, jnp.zeros((S, tail), f32)], axis=1), (1, H))
    sin_q = jnp.tile(sin_k, (1, H))
    rel = np.arange(tq)[:, None] - np.arange(tq)[None, :]
    causal = jnp.asarray(np.where(rel >= 0, 0.0, BIG_NEG).astype(np.float32))

    c_cq, c_ckv, c_kr, c_krs = OFF["dcq"] // _CQ_PAD, OFF["dckv"] // LANE, OFF["dkr"] // LANE, OFF["dkrs"] // LANE
    full = lambda shape: pl.BlockSpec(shape, lambda b, i: (0, 0))
    tab = lambda w: pl.BlockSpec((tr, w), lambda b, i: (i, 0))
    out3 = pl.BlockSpec((1, tr, HW), lambda b, i: (b, i, 0))
    q, k, v = pl.pallas_call(
        _mla_prep_kernel,
        grid=(B, S // tr),
        in_specs=[pl.BlockSpec((1, tr, _CQ_PAD), lambda b, i: (b, i, c_cq)),
                  pl.BlockSpec((1, tr, LANE), lambda b, i: (b, i, c_ckv)),
                  pl.BlockSpec((1, tr, LANE), lambda b, i: (b, i, c_kr)),
                  pl.BlockSpec((1, tr, LANE), lambda b, i: (b, i, c_krs)),
                  full((1, _CQ_PAD)), full((1, MLA_KV_RANK)), full((_CQ_PAD, HW)), full((_CQ_PAD, HW)),
                  full((MLA_KV_RANK, HW)), full((MLA_KV_RANK, HW)),
                  tab(HW), tab(HW), tab(LANE), tab(LANE), full((1, HW))],
        out_specs=[out3, out3, out3],
        out_shape=[jax.ShapeDtypeStruct((B, S, HW), MXU)] * 3,
        compiler_params=_params(2),
        name="mla_prep",
    )(h, h, h, h, gq, gkv, pad_q(wq3), pad_q(wq_rot), wk, wv, cos_q, sin_q, cos_k, sin_k, v_one)
    pair = lambda rows: pl.BlockSpec((1, rows, 2 * LANE), lambda b, p, i: (b, i if rows == tq else 0, p))
    return pl.pallas_call(
        functools.partial(_mla_attn_kernel, tq=tq),
        grid=(B, H // 2, S // tq),
        in_specs=[pair(tq), pair(S), pair(S), pl.BlockSpec((tq, tq), lambda b, p, i: (0, 0))],
        out_specs=pl.BlockSpec((1, tq, LANE), lambda b, p, i: (b, i, p)),
        out_shape=jax.ShapeDtypeStruct((B, S, H * MLA_V), f32),
        compiler_params=_params(3),
        name="mla_attention",
    )(q, k, v, causal)


def _ln(z, g, b):
    mu = jnp.mean(z, axis=-1, keepdims=True)
    zc = z - mu
    var = jnp.mean(zc * zc, axis=-1, keepdims=True)
    return zc * lax.rsqrt(var + LN_EPS) * g + b


def _merge_kernel(mg_ref, oa_ref, ob_ref, oc_ref, od_ref, wb_ref, wo_ref, x_ref, g_ref, b_ref, o_ref):
    acc = None
    for n, br in enumerate((oa_ref, ob_ref, oc_ref, od_ref)):
        up = _dot(br[...].astype(MXU), wb_ref[n])
        term = jax.nn.sigmoid(mg_ref[:, n * D_MODEL:(n + 1) * D_MODEL]) * up
        acc = term if acc is None else acc + term
    y = _dot(acc.astype(MXU), wo_ref[...])
    o_ref[...] = _ln(DN_ALPHA * x_ref[...] + y, g_ref[...], b_ref[...])


def merge_ln(h2, branches, w_branch, w_out, x2, g, b, tm=_TM_DENSE):
    N, D = x2.shape
    assert OFF["mg"] == 0 and N % tm == 0
    row = lambda w: pl.BlockSpec((tm, w), lambda i: (i, 0))
    return pl.pallas_call(
        _merge_kernel,
        grid=(N // tm,),
        in_specs=[row(N_MIXERS * D)] + [row(MIX_W)] * N_MIXERS
        + [pl.BlockSpec((N_MIXERS, MIX_W, D), lambda i: (0, 0, 0)), pl.BlockSpec((D, D), lambda i: (0, 0)),
           row(D), pl.BlockSpec((1, D), lambda i: (0, 0)), pl.BlockSpec((1, D), lambda i: (0, 0))],
        out_specs=row(D),
        out_shape=jax.ShapeDtypeStruct((N, D), f32),
        compiler_params=_params(1),
        name="merge_ln",
    )(h2, *branches, w_branch.astype(MXU), w_out.astype(MXU), x2, g.reshape(1, D), b.reshape(1, D))


ROW_TILE = 8


def _to_token_tiles(ref, val):
    rows = val.shape[0]
    for j in range(ROW_TILE):
        ref[pl.ds(j, rows, stride=ROW_TILE), :] = val[:, j * LANE:(j + 1) * LANE]


def _from_token_tiles(ref, rows, first=0, stride=ROW_TILE):
    return jnp.concatenate([ref[pl.ds(first + j, rows, stride=stride), :] for j in range(ROW_TILE)], axis=1)


def _xattn_kernel(x_ref, wq_ref, k_ref, v_ref, wo_ref, g_ref, b_ref, o_ref, o8_ref):
    x = x_ref[0]
    q = _dot(x.astype(MXU), wq_ref[...]).astype(MXU)
    k = k_ref[0]
    v = v_ref[0]
    heads = []
    for hd in range(X_HEADS):
        sl = slice(hd * X_HEAD_DIM, (hd + 1) * X_HEAD_DIM)
        s = _dot_nt(q[:, sl], k[:, sl]) * (X_HEAD_DIM ** -0.5)
        e = jnp.exp(s - jnp.max(s, axis=1, keepdims=True))
        p = e / jnp.sum(e, axis=1, keepdims=True)
        heads.append(_dot(p.astype(MXU), v[:, sl]).astype(MXU))
    y = _dot(jnp.concatenate(heads, axis=1), wo_ref[...])
    out = _ln(DN_ALPHA * x + y, g_ref[...], b_ref[...])
    o_ref[0] = out
    _to_token_tiles(o8_ref.at[0], out)


def cross_attention_ln(x, mem, wq, wkv, wo, g, b, tq=_TM_DENSE):
    B, S, D = x.shape
    assert D == ROW_TILE * LANE
    M = mem.shape[1]
    F = X_HEADS * X_HEAD_DIM
    kv = matmul(mem.reshape(B * M, D), wkv.astype(MXU)[None], 0, 512, 2 * F, out_dtype=MXU).reshape(B, M, 2 * F)
    full = lambda shape: pl.BlockSpec(shape, lambda bi, i: (0,) * len(shape))
    return pl.pallas_call(
        _xattn_kernel,
        grid=(B, S // tq),
        in_specs=[pl.BlockSpec((1, tq, D), lambda bi, i: (bi, i, 0)), full((D, F)),
                  pl.BlockSpec((1, M, F), lambda bi, i: (bi, 0, 0)), pl.BlockSpec((1, M, F), lambda bi, i: (bi, 0, 1)),
                  full((F, D)), full((1, D)), full((1, D))],
        out_specs=[pl.BlockSpec((1, tq, D), lambda bi, i: (bi, i, 0)),
                   pl.BlockSpec((1, tq * ROW_TILE, LANE), lambda bi, i: (bi, i, 0))],
        out_shape=[jax.ShapeDtypeStruct((B, S, D), f32), jax.ShapeDtypeStruct((B, S * ROW_TILE, LANE), f32)],
        compiler_params=_params(2),
        name="cross_attention_ln",
    )(x, wq.astype(MXU), kv, kv, wo.astype(MXU), g.reshape(1, D), b.reshape(1, D))


_R_E0, _R_E1, _R_W0, _R_W1, _R_RANK0, _R_RANK1 = range(6)
_GRP_LANE0 = N_EXPERTS


def _router_kernel(x_ref, w_ref, b_ref, tri_ref, r_ref, cnt_ref):
    i = pl.program_id(0)
    tm = x_ref.shape[0]
    logits = _dot_nt(w_ref[...], x_ref[...].astype(MXU)) + b_ref[...]
    row = lax.broadcasted_iota(jnp.int32, (LANE, tm), 0)
    row_f = row.astype(f32)
    big = float(LANE)

    def cmax(t):
        return jnp.max(t, axis=0, keepdims=True)

    def first_row(cond):
        return jnp.min(jnp.where(cond, row_f, big), axis=0, keepdims=True)

    def softmax_on(mask):
        lm = jnp.where(mask, logits, NEG)
        e = jnp.where(mask, jnp.exp(lm - cmax(lm)), 0.0)
        return e / jnp.sum(e, axis=0, keepdims=True)

    is_g = (row >= _GRP_LANE0) & (row < _GRP_LANE0 + N_GROUPS)
    p_grp = softmax_on(is_g)
    p_g = cmax(p_grp)
    grp = first_row(is_g & (p_grp == p_g)) - float(_GRP_LANE0)
    grp_of_row = jnp.right_shift(row, EXPERTS_PER_GROUP.bit_length() - 1)
    in_grp = (row < N_EXPERTS) & (grp_of_row.astype(f32) == grp)
    p_e = softmax_on(in_grp)
    p1 = cmax(jnp.where(in_grp, p_e, -1.0))
    e1 = first_row(in_grp & (p_e == p1))
    rest = in_grp & (row_f != e1)
    p2 = cmax(jnp.where(rest, p_e, -1.0))
    e2 = first_row(rest & (p_e == p2))
    w1 = p_g * p1 / (p1 + p2)
    w2 = p_g * p2 / (p1 + p2)

    @pl.when(i == 0)
    def _():
        cnt_ref[...] = jnp.zeros_like(cnt_ref)

    oh1 = row_f == e1
    oh2 = row_f == e2
    both = (oh1 | oh2).astype(MXU)
    before = _dot(both, tri_ref[...]) + cnt_ref[:, 0:1]
    rank1 = jnp.sum(jnp.where(oh1, before, 0.0), axis=0, keepdims=True)
    rank2 = jnp.sum(jnp.where(oh2, before, 0.0), axis=0, keepdims=True)
    cnt_ref[...] = cnt_ref[...] + jnp.sum(both.astype(f32), axis=1, keepdims=True)

    out = jnp.zeros((LANE, tm), f32)
    for slot, val in ((_R_E0, e1), (_R_E1, e2), (_R_W0, w1), (_R_W1, w2), (_R_RANK0, rank1), (_R_RANK1, rank2)):
        out = jnp.where(row == slot, val, out)
    r_ref[...] = out.T


def moe_router(x2, rg_w, rg_b, re_w, re_b, tm=_TM_ROWS):
    N, D = x2.shape
    assert EXPERTS_PER_GROUP & (EXPERTS_PER_GROUP - 1) == 0 and N_EXPERTS + N_GROUPS <= LANE
    w = jnp.pad(jnp.concatenate([re_w, rg_w], axis=1), ((0, 0), (0, LANE - N_EXPERTS - N_GROUPS))).T.astype(MXU)
    b = jnp.pad(jnp.concatenate([re_b, rg_b]), (0, LANE - N_EXPERTS - N_GROUPS)).reshape(LANE, 1)
    tri = jnp.asarray(np.triu(np.ones((tm, tm), np.float32), 1), MXU)
    return pl.pallas_call(
        _router_kernel,
        grid=(N // tm,),
        in_specs=[pl.BlockSpec((tm, D), lambda i: (i, 0)), pl.BlockSpec((LANE, D), lambda i: (0, 0)),
                  pl.BlockSpec((LANE, 1), lambda i: (0, 0)), pl.BlockSpec((tm, tm), lambda i: (0, 0))],
        out_specs=[pl.BlockSpec((tm, LANE), lambda i: (i, 0)), pl.BlockSpec((LANE, LANE), lambda i: (0, 0))],
        out_shape=[jax.ShapeDtypeStruct((N, LANE), f32), jax.ShapeDtypeStruct((LANE, LANE), f32)],
        compiler_params=_params(1),
        name="moe_router",
    )(x2, w, b, tri)


def _ffn_kernel(ce_ref, first_ref, slot_ref, next_ref, nu_ref, x_ref, wgu_hbm, wd_hbm, o_ref,
                wgu_f32, wd_f32, wgu_b, wd_b, sems, *, layer):
    c = pl.program_id(0)
    used = c < nu_ref[0]

    def fetch(e, slot):
        return (pltpu.make_async_copy(wgu_hbm.at[layer, e], wgu_f32.at[slot], sems.at[slot, 0]),
                pltpu.make_async_copy(wd_hbm.at[layer, e], wd_f32.at[slot], sems.at[slot, 1]))

    @pl.when(used & (c == 0))
    def _():
        for cp in fetch(ce_ref[0], 0):
            cp.start()

    @pl.when(used & (first_ref[c] == 1))
    def _():
        slot = slot_ref[c]
        for cp in fetch(ce_ref[c], slot):
            cp.wait()

        @pl.when(next_ref[c] >= 0)
        def _():
            for cp in fetch(next_ref[c], 1 - slot):
                cp.start()

        wgu_b[...] = wgu_f32[slot].astype(MXU)
        wd_b[...] = wd_f32[slot].astype(MXU)

    @pl.when(used)
    def _():
        x = _from_token_tiles(x_ref, EXPERT_CHUNK)
        gu = _dot(x.astype(MXU), wgu_b[...])
        hid = jax.nn.silu(gu[:, :D_EXPERT]) * gu[:, D_EXPERT:]
        _to_token_tiles(o_ref, _dot(hid.astype(MXU), wd_b[...]))

    @pl.when(jnp.logical_not(used))
    def _():
        o_ref[...] = jnp.zeros_like(o_ref)


def expert_ffn(xb8, chunk_e, n_used, w_gu, w_down, layer):
    C = EXPERT_CHUNK
    n_chunks = xb8.shape[0] // (C * ROW_TILE)
    D = ROW_TILE * LANE
    idx = jnp.arange(n_chunks, dtype=jnp.int32)
    used = idx < n_used[0]
    first = used & ((idx == 0) | (chunk_e != jnp.roll(chunk_e, 1)))
    slot = (jnp.cumsum(first.astype(jnp.int32)) - 1) % 2
    none = jnp.int32(N_EXPERTS)
    later = jnp.roll(jnp.where(first, chunk_e, none), -1).at[-1].set(none)
    next_e = lax.cummin(later, axis=0, reverse=True)
    next_e = jnp.where(next_e == none, -1, next_e)
    tile_spec = pl.BlockSpec((C * ROW_TILE, LANE), lambda c, *_: (c, 0))
    any_spec = pl.BlockSpec(memory_space=pl.ANY)
    grid_spec = pltpu.PrefetchScalarGridSpec(
        num_scalar_prefetch=5,
        grid=(n_chunks,),
        in_specs=[tile_spec, any_spec, any_spec],
        out_specs=tile_spec,
        scratch_shapes=[pltpu.VMEM((2, D, 2 * D_EXPERT), w_gu.dtype), pltpu.VMEM((2, D_EXPERT, D), w_down.dtype),
                        pltpu.VMEM((D, 2 * D_EXPERT), MXU), pltpu.VMEM((D_EXPERT, D), MXU),
                        pltpu.SemaphoreType.DMA((2, 2))],
    )
    return pl.pallas_call(
        functools.partial(_ffn_kernel, layer=layer),
        grid_spec=grid_spec,
        out_shape=jax.ShapeDtypeStruct(xb8.shape, f32),
        compiler_params=_params(1),
        name="expert_ffn",
    )(chunk_e, first.astype(jnp.int32), slot.astype(jnp.int32), next_e.astype(jnp.int32), n_used, xb8, w_gu, w_down)


_COPY_WINDOW = 512


def _windowed_copies(n, copy, wait):
    assert _COPY_WINDOW & (_COPY_WINDOW - 1) == 0 and n >= _COPY_WINDOW

    def fill(j, carry):
        copy(j, j).start()
        return carry

    def steady(j, carry):
        s = jnp.bitwise_and(j, _COPY_WINDOW - 1)
        wait(s)
        copy(j, s).start()
        return carry

    def drain(s, carry):
        wait(s)
        return carry

    lax.fori_loop(0, _COPY_WINDOW, fill, 0, unroll=8)
    lax.fori_loop(_COPY_WINDOW, n, steady, 0, unroll=8)
    lax.fori_loop(0, _COPY_WINDOW, drain, 0, unroll=8)


def _dispatch_kernel(dest_ref, tail_ref, nu_ref, x8_ref, xb_hbm, sems, zeros, *, tm):
    n = tm * TOPK_IN_GROUP
    base = pl.program_id(0) * n

    @pl.when(pl.program_id(0) == 0)
    def _():
        zeros[...] = jnp.zeros_like(zeros)
        chunk_rows = zeros.shape[0]
        n_chunks = xb_hbm.shape[0] // chunk_rows

        def fill(row, s):
            return pltpu.make_async_copy(zeros, xb_hbm.at[pl.ds(pl.multiple_of(row, ROW_TILE), chunk_rows)], sems.at[s])

        def tail_fill(e):
            return fill(jnp.maximum(tail_ref[e], 0) * ROW_TILE, e)

        def free_fill(c):
            return fill(c * chunk_rows, N_EXPERTS + c - nu_ref[0])

        for e in range(N_EXPERTS):
            @pl.when(tail_ref[e] >= 0)
            def _():
                tail_fill(e).start()

        lax.fori_loop(nu_ref[0], n_chunks, lambda c, carry: (free_fill(c).start(), carry)[1], 0)
        for e in range(N_EXPERTS):
            @pl.when(tail_ref[e] >= 0)
            def _():
                tail_fill(e).wait()

        lax.fori_loop(nu_ref[0], n_chunks, lambda c, carry: (free_fill(c).wait(), carry)[1], 0)

    def copy(j, s):
        t = pl.multiple_of(jnp.right_shift(j, 1) * ROW_TILE, ROW_TILE)
        d = pl.multiple_of(dest_ref[base + j] * ROW_TILE, ROW_TILE)
        return pltpu.make_async_copy(x8_ref.at[pl.ds(t, ROW_TILE)], xb_hbm.at[pl.ds(d, ROW_TILE)], sems.at[s])

    def wait(s):
        pltpu.make_async_copy(x8_ref.at[pl.ds(0, ROW_TILE)], xb_hbm.at[pl.ds(0, ROW_TILE)], sems.at[s]).wait()

    _windowed_copies(n, copy, wait)


def moe_dispatch(x8, dest, tail, n_used, n_slots, tm=_TM_DISPATCH):
    n_tok = x8.shape[0] // ROW_TILE
    n_chunks = n_slots // EXPERT_CHUNK
    assert n_tok % tm == 0 and tm * TOPK_IN_GROUP >= _COPY_WINDOW
    assert _COPY_WINDOW >= N_EXPERTS + n_chunks - n_tok * TOPK_IN_GROUP // EXPERT_CHUNK
    grid_spec = pltpu.PrefetchScalarGridSpec(
        num_scalar_prefetch=3,
        grid=(n_tok // tm,),
        in_specs=[pl.BlockSpec((tm * ROW_TILE, LANE), lambda i, *_: (i, 0))],
        out_specs=pl.BlockSpec(memory_space=pl.ANY),
        scratch_shapes=[pltpu.SemaphoreType.DMA((_COPY_WINDOW,)),
                        pltpu.VMEM((EXPERT_CHUNK * ROW_TILE, LANE), x8.dtype)],
    )
    return pl.pallas_call(
        functools.partial(_dispatch_kernel, tm=tm),
        grid_spec=grid_spec,
        out_shape=jax.ShapeDtypeStruct((n_slots * ROW_TILE, LANE), x8.dtype),
        compiler_params=_params(1),
        name="moe_dispatch",
    )(dest, tail, n_used, x8)


def _combine_kernel(dest_ref, x_ref, r_ref, g_ref, b_ref, yb_hbm, o_ref, buf, sems):
    tm = x_ref.shape[0]
    n = tm * TOPK_IN_GROUP
    i = pl.program_id(0)

    def issue(tile):
        dst_buf = buf.at[jnp.bitwise_and(tile, 1)]

        def body(j, carry):
            src = pl.multiple_of(dest_ref[tile * n + j] * ROW_TILE, ROW_TILE)
            dst = pl.multiple_of(j * ROW_TILE, ROW_TILE)
            pltpu.make_async_copy(yb_hbm.at[pl.ds(src, ROW_TILE)], dst_buf.at[pl.ds(dst, ROW_TILE)],
                                  sems.at[j]).start()
            return carry

        lax.fori_loop(0, n, body, 0, unroll=8)

    def drain(j, carry):
        pltpu.make_async_copy(yb_hbm.at[pl.ds(0, ROW_TILE)], buf.at[0, pl.ds(0, ROW_TILE)], sems.at[j]).wait()
        return carry

    @pl.when(i == 0)
    def _():
        issue(i)

    lax.fori_loop(0, n, drain, 0, unroll=8)

    @pl.when(i + 1 < pl.num_programs(0))
    def _():
        issue(i + 1)

    cur = buf.at[jnp.bitwise_and(i, 1)]
    r = r_ref[...]
    lane = lax.broadcasted_iota(jnp.int32, r.shape, 1)
    w0 = jnp.sum(jnp.where(lane == _R_W0, r, 0.0), axis=1, keepdims=True)
    w1 = jnp.sum(jnp.where(lane == _R_W1, r, 0.0), axis=1, keepdims=True)
    pair = TOPK_IN_GROUP * ROW_TILE
    y = _from_token_tiles(cur, tm, 0, pair) * w0 + _from_token_tiles(cur, tm, ROW_TILE, pair) * w1
    o_ref[...] = _ln(DN_ALPHA * x_ref[...] + y, g_ref[...], b_ref[...])


def combine_ln(x2, yb8, dest, r, g, b, tm=_TM_ROWS):
    N, D = x2.shape
    assert N % tm == 0
    row = lambda w: pl.BlockSpec((tm, w), lambda i, dest: (i, 0))
    vec = pl.BlockSpec((1, D), lambda i, dest: (0, 0))
    grid_spec = pltpu.PrefetchScalarGridSpec(
        num_scalar_prefetch=1,
        grid=(N // tm,),
        in_specs=[row(D), row(LANE), vec, vec, pl.BlockSpec(memory_space=pl.ANY)],
        out_specs=row(D),
        scratch_shapes=[pltpu.VMEM((2, tm * TOPK_IN_GROUP * ROW_TILE, LANE), f32),
                        pltpu.SemaphoreType.DMA((tm * TOPK_IN_GROUP,))],
    )
    return pl.pallas_call(
        _combine_kernel,
        grid_spec=grid_spec,
        out_shape=jax.ShapeDtypeStruct((N, D), f32),
        compiler_params=_params(1),
        name="moe_combine_ln",
    )(dest, x2, r, g.reshape(1, D), b.reshape(1, D), yb8)


def hier_moe_ln(x2, x8, rg_w, rg_b, re_w, re_b, w_gu, w_down, layer, g, b):
    N, D = x2.shape
    E, C, K = N_EXPERTS, EXPERT_CHUNK, TOPK_IN_GROUP
    A = N * K
    r, cnt = moe_router(x2, rg_w, rg_b, re_w, re_b)
    e = r[:, _R_E0:_R_E1 + 1].astype(jnp.int32)
    rank = r[:, _R_RANK0:_R_RANK1 + 1].astype(jnp.int32)
    counts = cnt[:E, 0].astype(jnp.int32)
    padded = (counts + C - 1) // C * C
    pad_end = jnp.cumsum(padded)
    below = jnp.arange(E, dtype=jnp.int32)[None, None, :] < e[..., None]
    dest = rank + jnp.sum(jnp.where(below, padded[None, None, :], 0), axis=-1)
    n_chunks = -(-(A + E * (C - 1)) // C)
    P = n_chunks * C
    chunk_start = jnp.arange(n_chunks, dtype=jnp.int32) * C
    chunk_e = jnp.minimum(jnp.sum((pad_end[None, :] <= chunk_start[:, None]).astype(jnp.int32), axis=1), E - 1)
    n_used = (pad_end[-1] // C).reshape(1).astype(jnp.int32)
    dest = dest.reshape(A).astype(jnp.int32)
    tail = jnp.where(padded > 0, pad_end - C, -1).astype(jnp.int32)
    xb8 = moe_dispatch(x8, dest, tail, n_used, P)
    yb8 = expert_ffn(xb8, chunk_e.astype(jnp.int32), n_used, w_gu, w_down, layer)
    return combine_ln(x2, yb8, dest, r, g, b)


def kernel(x, mem, w_in, nsa_cmp_pos, nsa_cmp_w1, nsa_cmp_w2, rnn_conv_w, rnn_conv_b, rnn_ga_w, rnn_ga_b,
           rnn_gx_w, rnn_gx_b, rnn_lambda, mla_q_norm, mla_kv_norm, mla_w_uq, mla_w_ukv, w_branch, w_out,
           ln1_g, ln1_b, x_wq, x_wkv, x_wo, ln2_g, ln2_b, moe_rg_w, moe_rg_b, moe_re_w, moe_re_b,
           moe_w_gu, moe_w_down, ln3_g, ln3_b):
    B, S, D = x.shape
    N = B * S
    x2 = x.reshape(N, D)
    w_cat = _cat_w_in(w_in)
    for l in range(DEPTH):
        h2 = matmul(x2, w_cat, l, 512, _IN_TILE)
        h = h2.reshape(B, S, N_CAT)
        o_a = nsa_attention(h, nsa_cmp_pos[l], nsa_cmp_w1[l], nsa_cmp_w2[l])
        o_b = stick_breaking_attention(h)
        o_c = rglru_block(h, rnn_conv_w[l], rnn_conv_b[l], rnn_ga_w[l], rnn_ga_b[l], rnn_gx_w[l], rnn_gx_b[l],
                          rnn_lambda[l])
        o_d = mla_attention(h, mla_q_norm[l], mla_kv_norm[l], mla_w_uq[l], mla_w_ukv[l])
        branches = [o.reshape(N, MIX_W) for o in (o_a, o_b, o_c, o_d)]
        x2 = merge_ln(h2, branches, w_branch[l], w_out[l], x2, ln1_g[l], ln1_b[l])
        x3, x8 = cross_attention_ln(x2.reshape(B, S, D), mem, x_wq[l], x_wkv[l], x_wo[l], ln2_g[l], ln2_b[l])
        x2 = hier_moe_ln(x3.reshape(N, D), x8.reshape(N * ROW_TILE, LANE), moe_rg_w[l], moe_rg_b[l], moe_re_w[l],
                         moe_re_b[l], moe_w_gu, moe_w_down, l, ln3_g[l], ln3_b[l])
    return x2.reshape(B, S, D)
```

```python
import functools

import numpy as np
import jax
import jax.numpy as jnp
from jax import lax
from jax.experimental import pallas as pl
from jax.experimental.pallas import tpu as pltpu

D_MODEL = 1024
DEPTH = 4
HEAD_DIM = 64
N_MIXERS = 4
MIX_W = 256
NSA_HEADS = 4
NSA_KV_HEADS = 2
NSA_GROUP = NSA_HEADS // NSA_KV_HEADS
CMP_LEN = 32
CMP_STRIDE = 16
CMP_HID = 256
SEL_BLOCK = 64
SEL_TOPK = 8
WINDOW = 512
FORCE_SCORE = 1e4
SB_HEADS = 4
RNN_W = 256
CONV_W = 4
LRU_C = 8.0
MLA_HEADS = 4
MLA_Q_RANK = 192
MLA_KV_RANK = 128
MLA_NOPE = 64
MLA_ROPE = 32
MLA_V = 64
ROPE_THETA = 10000.0
X_HEADS = 4
X_HEAD_DIM = 128
N_GROUPS = 4
EXPERTS_PER_GROUP = 8
N_EXPERTS = N_GROUPS * EXPERTS_PER_GROUP
TOPK_IN_GROUP = 2
D_EXPERT = 512
EXPERT_CHUNK = 256
DN_ALPHA = (2.0 * DEPTH) ** 0.25
LN_EPS = 1e-5
RMS_EPS = 1e-6

IN_SPLITS = ((NSA_HEADS * HEAD_DIM,) + (NSA_KV_HEADS * HEAD_DIM,) * 6 + (NSA_HEADS * 3,)
             + (SB_HEADS * HEAD_DIM,) * 3
             + (RNN_W, RNN_W)
             + (MLA_Q_RANK, MLA_KV_RANK, MLA_ROPE)
             + (N_MIXERS * D_MODEL,))
IN_OFFSETS = tuple(int(o) for o in np.concatenate([[0], np.cumsum(IN_SPLITS)[:-1]]))

LANE = 128
VMEM_LIMIT = 48 * 1024 * 1024
NEG = -1e30
BIG_NEG = -2.0 ** 100

f32 = jnp.float32
MXU = jnp.bfloat16

_CQ_PAD = 2 * LANE
_W_Q, _W_KV, _W_KVT = NSA_HEADS * HEAD_DIM, NSA_KV_HEADS * HEAD_DIM, NSA_KV_HEADS * LANE
_SECTIONS = (("mg", N_MIXERS * D_MODEL), ("aq", _W_Q), ("kvs", _W_KVT), ("kvw", _W_KVT),
             ("bq", SB_HEADS * HEAD_DIM), ("bk", SB_HEADS * HEAD_DIM), ("bv", SB_HEADS * HEAD_DIM),
             ("cx", RNN_W), ("cg", RNN_W), ("dcq", _CQ_PAD), ("kc", _W_KV), ("vc", _W_KV),
             ("ag", LANE), ("dckv", MLA_KV_RANK), ("dkr", LANE), ("dkrs", LANE))
_IN_TILE = 3584
OFF = {}
_o = 0
for _n, _w in _SECTIONS:
    OFF[_n] = _o
    _o += _w
N_CAT = -(-_o // _IN_TILE) * _IN_TILE

_SEL_ROWS = 32
_AUG_SEL = HEAD_DIM
_AUG_POS = HEAD_DIM + _SEL_ROWS
_POS_SPLIT = 256

_TQ_NSA = 512
_TQ_MLA = 512
_TQ_SB = 256
_TM_ROWS = 256
_TM_DENSE = 512
_TM_DISPATCH = 256
_TR_MLA_PREP = 1024


def _rot_half(t):
    d = t.shape[-1]
    return jnp.concatenate([-t[..., d // 2:], t[..., :d // 2]], axis=-1)


def _cat_w_in(w):
    def sec(i):
        return w[..., IN_OFFSETS[i]:IN_OFFSETS[i] + IN_SPLITS[i]]

    def pair(k, v):
        return jnp.concatenate([k[..., :HEAD_DIM], v[..., :HEAD_DIM], k[..., HEAD_DIM:], v[..., HEAD_DIM:]], axis=-1)

    def pad_cols(t, before, after):
        return jnp.pad(t, ((0, 0),) * (t.ndim - 1) + ((before, after),))

    def padc(t, n):
        return pad_cols(t, 0, n - t.shape[-1])

    def rope_slot(t):
        return pad_cols(t, MLA_NOPE, LANE - MLA_NOPE - MLA_ROPE)

    parts = {"mg": sec(16), "aq": sec(0), "kc": sec(1), "vc": sec(2), "kvs": pair(sec(3), sec(4)),
             "kvw": pair(sec(5), sec(6)), "ag": padc(sec(7), LANE), "bq": sec(8), "bk": sec(9),
             "bv": sec(10), "cx": sec(11), "cg": sec(12), "dcq": padc(sec(13), _CQ_PAD), "dckv": sec(14),
             "dkr": rope_slot(sec(15)), "dkrs": rope_slot(_rot_half(sec(15)))}
    cat = jnp.concatenate([parts[n] for n, _ in _SECTIONS], axis=-1)
    return padc(cat, N_CAT).astype(MXU)


def _dot(a, b):
    return jnp.dot(a, b, preferred_element_type=f32)


def _dot_nt(a, b):
    return lax.dot_general(a, b, (((1,), (1,)), ((), ())), preferred_element_type=f32)


def _params(n_axes):
    return pltpu.CompilerParams(dimension_semantics=("arbitrary",) * n_axes, vmem_limit_bytes=VMEM_LIMIT)


def _mm_kernel(a_ref, b_ref, o_ref):
    o_ref[...] = _dot(a_ref[...].astype(MXU), b_ref[0]).astype(o_ref.dtype)


def matmul(a, b, layer, tm, tn, out_dtype=f32):
    M, K = a.shape
    _, _, N = b.shape
    assert M % tm == 0 and N % tn == 0
    return pl.pallas_call(
        _mm_kernel,
        grid=(N // tn, M // tm),
        in_specs=[pl.BlockSpec((tm, K), lambda j, i: (i, 0)),
                  pl.BlockSpec((1, K, tn), lambda j, i: (layer, 0, j))],
        out_specs=pl.BlockSpec((tm, tn), lambda j, i: (i, j)),
        out_shape=jax.ShapeDtypeStruct((M, N), out_dtype),
        compiler_params=_params(2),
        name="matmul",
    )(a, b)


def _cmp_kernel(t_ref, w1_ref, pos_ref, w2_ref, o_ref):
    nc = t_ref.shape[1] // CMP_STRIDE
    t = jnp.concatenate([t_ref[0, pl.ds(l, nc, stride=CMP_STRIDE), :] for l in range(CMP_STRIDE)], axis=1)
    half = t.shape[1]
    t = t.astype(MXU)
    y1 = _dot(t, w1_ref[0, :half, :])
    y2 = _dot(t, w1_ref[0, half:, :])
    pos = jnp.broadcast_to(pos_ref[0], (8, 2 * half)).astype(MXU)
    pc = _dot(pos, w1_ref[0])[0:1]
    nc = y2.shape[0]
    hid = y1 + pltpu.roll(y2, nc - 1, 0) + pc
    o_ref[0, 0] = _dot(jax.nn.gelu(hid).astype(MXU), w2_ref[0])


def nsa_compress(h, cmp_w1, cmp_pos, cmp_w2):
    B, S, _ = h.shape
    G = NSA_KV_HEADS
    NC = S // CMP_STRIDE
    F = CMP_STRIDE * G * HEAD_DIM
    assert G * HEAD_DIM == LANE and OFF["vc"] == OFF["kc"] + LANE
    col0 = OFF["kc"] // LANE
    def group_diag(w, axis):
        blocks = [jnp.concatenate([w if k == g else jnp.zeros_like(w) for k in range(G)], axis=-1) for g in range(G)]
        return jnp.stack(blocks, axis=axis)

    w1 = group_diag(cmp_w1.reshape(2, CMP_LEN, HEAD_DIM, CMP_HID).astype(MXU), 2)
    w1 = w1.reshape(2, 2 * F, G * CMP_HID)
    pos = jnp.broadcast_to(cmp_pos[:, :, None, :], (2, CMP_LEN, G, HEAD_DIM)).reshape(2, 1, 2 * F)
    w2 = group_diag(jnp.concatenate([cmp_w2, cmp_w2], axis=-1).astype(MXU), 1)
    w2 = w2.reshape(2, G * CMP_HID, G * LANE)
    return pl.pallas_call(
        _cmp_kernel,
        grid=(2, B),
        in_specs=[pl.BlockSpec((1, S, LANE), lambda j, i: (i, 0, col0 + j)),
                  pl.BlockSpec((1, 2 * F, G * CMP_HID), lambda j, i: (j, 0, 0)),
                  pl.BlockSpec((1, 1, 2 * F), lambda j, i: (j, 0, 0)),
                  pl.BlockSpec((1, G * CMP_HID, G * LANE), lambda j, i: (j, 0, 0))],
        out_specs=pl.BlockSpec((1, 1, NC, G * LANE), lambda j, i: (j, i, 0, 0)),
        out_shape=jax.ShapeDtypeStruct((2, B, NC, G * LANE), f32),
        compiler_params=_params(2),
        name="nsa_compress",
    )(h, w1, pos, w2)


def _nsa_kernel(q_ref, kc_ref, vc_ref, kvs_ref, kvw_ref, gl_ref, cover_ref, aug_ref, cb_ref, wb_ref,
                o_ref, ksa, vsa, kwa, vwa, *, tq, n_cmp, n_sel, n_top):
    tk = tq
    g = pl.program_id(1)
    qi = pl.program_id(2)
    q0 = pl.multiple_of(qi * tq, tq)
    lane = lax.broadcasted_iota(jnp.int32, (tq, LANE), 1)
    lo_half = lane < HEAD_DIM

    @pl.when(qi == 0)
    def _():
        real = lax.broadcasted_iota(jnp.int32, ksa.shape, 1) < HEAD_DIM
        aug = aug_ref[...]
        ones = jnp.ones(ksa.shape, MXU)
        for kv_ref, k_out, v_out in ((kvs_ref, ksa, vsa), (kvw_ref, kwa, vwa)):
            kv = kv_ref[0]
            k_out[...] = jnp.where(real, kv.astype(MXU), aug)
            v_out[...] = jnp.where(real, pltpu.roll(kv, HEAD_DIM, 1).astype(MXU), ones)

    q = q_ref[0] * (HEAD_DIM ** -0.5)
    q_heads = (jnp.where(lo_half, q, 0.0), jnp.where(lo_half, pltpu.roll(q, HEAD_DIM, 1), 0.0))
    alibi = [2.0 ** (-8.0 * (h + 1) / NSA_HEADS) for h in range(NSA_HEADS)]
    slopes = [jnp.where(g == 0, alibi[n], alibi[NSA_GROUP + n]) for n in range(NSA_GROUP)]
    pos_cols = [jnp.where(lane == _AUG_POS, slopes[n] * _POS_SPLIT, jnp.where(lane == _AUG_POS + 1, slopes[n], 0.0))
                for n in range(NSA_GROUP)]
    tpos = q0 + lax.broadcasted_iota(jnp.int32, (tq, 1), 0)

    nc = kc_ref.shape[2]
    cidx = lax.broadcasted_iota(jnp.int32, (1, nc), 1)
    dist_c = tpos - (cidx * CMP_STRIDE + (CMP_LEN - 1))
    mask_c = (dist_c >= 0) & (cidx < n_cmp)
    dist_cf = dist_c.astype(f32)
    kc = kc_ref[0, 0].astype(MXU)
    vc = vc_ref[0, 0].astype(MXU)
    o_cmp = []
    imp_t = jnp.zeros((LANE, tq), f32)
    for n in range(NSA_GROUP):
        s = _dot_nt(q_heads[n].astype(MXU), kc) - slopes[n] * dist_cf
        sm = jnp.where(mask_c, s, NEG)
        m = jnp.max(sm, axis=1, keepdims=True)
        p = jnp.where(mask_c, jnp.exp(sm - m), 0.0)
        p = (p / jnp.maximum(jnp.sum(p, axis=1, keepdims=True), 1e-30)).astype(MXU)
        o_cmp.append(_dot(p, vc))
        imp_t = imp_t + _dot_nt(cover_ref[...], p)

    rows = _SEL_ROWS
    imp = imp_t[:rows]
    blk = lax.broadcasted_iota(jnp.int32, (rows, tq), 0)
    blk_f = blk.astype(f32)
    tpos_t = q0 + lax.broadcasted_iota(jnp.int32, (1, tq), 1)
    forced = (blk == 0) | (blk == jnp.right_shift(tpos_t, SEL_BLOCK.bit_length() - 1))
    valid = blk * SEL_BLOCK <= tpos_t
    imp = jnp.where(forced, FORCE_SCORE, jnp.where(valid, imp, -1.0))
    imp = jnp.where(blk < n_sel, imp, NEG)
    sel_t = jnp.zeros((rows, tq), f32)
    for _ in range(n_top):
        m = jnp.max(imp, axis=0, keepdims=True)
        first = jnp.min(jnp.where(imp == m, blk_f, float(LANE)), axis=0, keepdims=True)
        pick = blk_f == first
        sel_t = jnp.where(pick, 1.0, sel_t)
        imp = jnp.where(pick, 2 * NEG, imp)
    sel = jnp.concatenate([sel_t, jnp.zeros((LANE - rows, tq), f32)], axis=0).T
    sel_bias = pltpu.roll(jnp.where(sel > 0.5, 0.0, BIG_NEG), _AUG_SEL, 1)
    sel_cols = jnp.where((lane >= _AUG_SEL) & (lane < _AUG_SEL + rows), sel_bias, 0.0)
    q_sel = jnp.concatenate([q_heads[n] + sel_cols + pos_cols[n] for n in range(NSA_GROUP)], axis=0).astype(MXU)
    q_win = jnp.concatenate([q_heads[n] + pos_cols[n] for n in range(NSA_GROUP)], axis=0).astype(MXU)
    rows_q = NSA_GROUP * tq
    denom_lane = lax.broadcasted_iota(jnp.int32, (rows_q, LANE), 1) == HEAD_DIM

    def update(carry, qs, k, v, bias):
        m, acc = carry
        s = _dot_nt(qs, k)
        if bias is not None:
            s = s + bias
        m_new = jnp.maximum(m, jnp.max(s, axis=1, keepdims=True))
        p = jnp.exp(s - m_new)
        return m_new, jnp.exp(m - m_new) * acc + _dot(p.astype(MXU), v)

    def finish(carry):
        _, acc = carry
        denom = jnp.sum(jnp.where(denom_lane, acc, 0.0), axis=1, keepdims=True)
        o = acc / jnp.maximum(denom, 1e-30)
        return [o[n * tq:(n + 1) * tq] for n in range(NSA_GROUP)]

    init = (jnp.full((rows_q, 1), NEG, f32), jnp.zeros((rows_q, LANE), f32))
    causal = cb_ref[...]

    def sel_body(kt, carry):
        k0 = pl.multiple_of(kt * tk, tk)
        return update(carry, q_sel, ksa[pl.ds(k0, tk), :], vsa[pl.ds(k0, tk), :], None)

    carry = lax.fori_loop(0, qi, sel_body, init)
    o_sel = finish(update(carry, q_sel, ksa[pl.ds(q0, tk), :], vsa[pl.ds(q0, tk), :], causal))

    carry = init
    n_back = WINDOW // tk
    for back in range(n_back, -1, -1):
        k0 = pl.multiple_of(jnp.maximum(qi - back, 0) * tk, tk)
        if back:
            off = jnp.where(qi >= back, 0.0, BIG_NEG)
            bias = wb_ref[...] + off if back == n_back else off
        else:
            bias = causal
        carry = update(carry, q_win, kwa[pl.ds(k0, tk), :], vwa[pl.ds(k0, tk), :], bias)
    o_win = finish(carry)

    sig = jax.nn.sigmoid(gl_ref[0])

    def gate(n, j):
        col = 3 * (NSA_GROUP * g + n) + j
        return jnp.sum(jnp.where(lane == col, sig, 0.0), axis=1, keepdims=True)

    o = [gate(n, 0) * o_cmp[n] + gate(n, 1) * o_sel[n] + gate(n, 2) * o_win[n] for n in range(NSA_GROUP)]
    o_ref[0] = jnp.where(lo_half, o[0], pltpu.roll(o[1], HEAD_DIM, 1))


def nsa_attention(h, cmp_pos, cmp_w1, cmp_w2, tq=_TQ_NSA):
    B, S, _ = h.shape
    G = NSA_KV_HEADS
    NC = S // CMP_STRIDE
    n_cmp = (S - CMP_LEN) // CMP_STRIDE + 1
    n_sel = S // SEL_BLOCK
    n_top = min(SEL_TOPK, n_sel)
    F = CMP_STRIDE * HEAD_DIM
    assert S % tq == 0 and WINDOW % tq == 0 and tq % SEL_BLOCK == 0 and n_sel <= _SEL_ROWS and S <= _POS_SPLIT ** 2

    kvc = nsa_compress(h, cmp_w1, cmp_pos, cmp_w2)

    c0 = np.arange(n_cmp)[:, None] * CMP_STRIDE
    j0 = np.arange(n_sel)[None, :] * SEL_BLOCK
    cover = np.clip(np.minimum(c0 + CMP_LEN, j0 + SEL_BLOCK) - np.maximum(c0, j0), 0, None) / CMP_LEN
    cover_t = np.zeros((LANE, NC), np.float32)
    cover_t[:n_sel, :n_cmp] = cover.T
    pos = np.arange(S)
    aug = np.zeros((S, LANE), np.float32)
    aug[pos, _AUG_SEL + pos // SEL_BLOCK] = 1.0
    aug[:, _AUG_POS] = pos // _POS_SPLIT
    aug[:, _AUG_POS + 1] = pos % _POS_SPLIT
    rel = np.arange(tq)[:, None] - np.arange(tq)[None, :]
    rel = np.tile(rel, (NSA_GROUP, 1))
    causal = np.where(rel >= 0, 0.0, BIG_NEG).astype(np.float32)
    win_lo = np.where(rel < 0, 0.0, BIG_NEG).astype(np.float32)

    col = lambda name: OFF[name] // LANE
    full2 = lambda shape: pl.BlockSpec(shape, lambda b, g, i: (0, 0))
    q_spec = pl.BlockSpec((1, tq, LANE), lambda b, g, i: (b, i, col("aq") + g))
    c_specs = [pl.BlockSpec((1, 1, NC, LANE), lambda b, g, i, j=j: (j, b, 0, g)) for j in range(2)]
    kv_specs = [pl.BlockSpec((1, S, LANE), lambda b, g, i, c=col(n): (b, 0, c + g))
                for n in ("kvs", "kvw")]
    gl_spec = pl.BlockSpec((1, tq, LANE), lambda b, g, i: (b, i, col("ag")))
    return pl.pallas_call(
        functools.partial(_nsa_kernel, tq=tq, n_cmp=n_cmp, n_sel=n_sel, n_top=n_top),
        grid=(B, G, S // tq),
        in_specs=[q_spec] + c_specs + kv_specs + [gl_spec, full2((LANE, NC)), full2((S, LANE)),
                                                  full2((NSA_GROUP * tq, tq)), full2((NSA_GROUP * tq, tq))],
        out_specs=pl.BlockSpec((1, tq, LANE), lambda b, g, i: (b, i, g)),
        out_shape=jax.ShapeDtypeStruct((B, S, NSA_HEADS * HEAD_DIM), f32),
        scratch_shapes=[pltpu.VMEM((S, LANE), MXU)] * 4,
        compiler_params=_params(3),
        name="nsa_attention",
    )(h, kvc, kvc, h, h, h, jnp.asarray(cover_t, MXU), jnp.asarray(aug, MXU), jnp.asarray(causal),
      jnp.asarray(win_lo))


_SB_DEAD = -104.0


def _log_sigmoid(z):
    return jnp.minimum(z, 0.0) - jnp.log1p(jnp.exp(-jnp.abs(z)))


def _sb_kernel(q_ref, k_ref, v_ref, u_ref, o_ref, kb, vb, *, tq):
    tk = tq
    qi = pl.program_id(2)
    q0 = pl.multiple_of(qi * tq, tq)
    lane = lax.broadcasted_iota(jnp.int32, (tq, LANE), 1)
    lo_half = lane < HEAD_DIM

    @pl.when(qi == 0)
    def _():
        kb[...] = k_ref[0].astype(MXU)
        vb[...] = v_ref[0].astype(MXU)

    q = q_ref[0] * (HEAD_DIM ** -0.5)
    q2 = jnp.concatenate([jnp.where(lo_half, q, 0.0), jnp.where(lo_half, 0.0, q)], axis=0).astype(MXU)
    u = u_ref[...]

    def tile(carry, k, v, strict):
        c, acc = carry
        z = _dot_nt(q2, k)
        ls = _log_sigmoid(z)
        log_1m = ls - z
        if strict is not None:
            log_1m = jnp.where(strict, log_1m, 0.0)
        hi = log_1m.astype(MXU)
        lo = (log_1m - hi.astype(f32)).astype(MXU)
        tail = _dot(hi, u) + _dot(lo, u) + c
        a = jnp.exp(ls + tail)
        if strict is not None:
            a = jnp.where(strict, a, 0.0)
        return c + jnp.sum(log_1m, axis=1, keepdims=True), acc + _dot(a.astype(MXU), v)

    row = lax.broadcasted_iota(jnp.int32, (2 * tq, tk), 0)
    rel = jnp.where(row >= tq, row - tq, row) - lax.broadcasted_iota(jnp.int32, (2 * tq, tk), 1)
    zero = (jnp.zeros((2 * tq, 1), f32), jnp.zeros((2 * tq, LANE), f32))
    state = tile(zero, kb[pl.ds(q0, tk), :], vb[pl.ds(q0, tk), :], rel > 0)

    def alive(state):
        return (jnp.max(state[0]) > _SB_DEAD).astype(jnp.int32)

    def cond(loop):
        kt, live, _ = loop
        return (kt >= 0) & (live > 0)

    def body(loop):
        kt, _, state = loop
        k0 = pl.multiple_of(kt * tk, tk)
        state = tile(state, kb[pl.ds(k0, tk), :], vb[pl.ds(k0, tk), :], None)
        return kt - 1, alive(state), state

    _, _, (_, acc) = lax.while_loop(cond, body, (qi - 1, alive(state), state))
    o_ref[0] = jnp.where(lo_half, acc[:tq], acc[tq:])


def stick_breaking_attention(h, tq=_TQ_SB):
    B, S, _ = h.shape
    assert S % tq == 0
    tri = jnp.asarray(np.tril(np.ones((tq, tq), np.float32), -1), MXU)
    cq, ck, cv = (OFF[n] // LANE for n in ("bq", "bk", "bv"))
    return pl.pallas_call(
        functools.partial(_sb_kernel, tq=tq),
        grid=(B, SB_HEADS // 2, S // tq),
        in_specs=[pl.BlockSpec((1, tq, LANE), lambda b, p, i: (b, i, cq + p)),
                  pl.BlockSpec((1, S, LANE), lambda b, p, i: (b, 0, ck + p)),
                  pl.BlockSpec((1, S, LANE), lambda b, p, i: (b, 0, cv + p)),
                  pl.BlockSpec((tq, tq), lambda b, p, i: (0, 0))],
        out_specs=pl.BlockSpec((1, tq, LANE), lambda b, p, i: (b, i, p)),
        out_shape=jax.ShapeDtypeStruct((B, S, SB_HEADS * HEAD_DIM), f32),
        scratch_shapes=[pltpu.VMEM((S, LANE), MXU)] * 2,
        compiler_params=_params(3),
        name="sb_attention",
    )(h, h, h, tri)


def _neg_expm1(y):
    series = -y * (1.0 + y * (1.0 / 2 + y * (1.0 / 6 + y * (1.0 / 24 + y * (1.0 / 120)))))
    return jnp.where(y > -0.1, series, 1.0 - jnp.exp(y))


def _rglru_kernel(x_ref, xg_ref, cw_ref, cb_ref, gaw_ref, gab_ref, gxw_ref, gxb_ref, lam_ref, o_ref):
    x = x_ref[0]
    S = x.shape[0]
    row = lax.broadcasted_iota(jnp.int32, (S, 1), 0)

    def shifted(t, d, fill):
        return jnp.where(row >= d, pltpu.roll(t, d, 0), fill)

    u = cb_ref[...] + x * cw_ref[CONV_W - 1:CONV_W, :]
    for d in range(1, CONV_W):
        u = u + shifted(x, d, 0.0) * cw_ref[CONV_W - 1 - d:CONV_W - d, :]
    ub = u.astype(MXU)
    r = jax.nn.sigmoid(_dot(ub, gaw_ref[...]) + gab_ref[...])
    i = jax.nn.sigmoid(_dot(ub, gxw_ref[...]) + gxb_ref[...])
    lam = lam_ref[...]
    softplus_neg = jnp.maximum(-lam, 0.0) + jnp.log1p(jnp.exp(-jnp.abs(lam)))
    log_a = -LRU_C * r * softplus_neg
    a = jnp.exp(log_a)
    b = jnp.sqrt(_neg_expm1(2.0 * log_a)) * (i * u)
    d = 1
    while d < S:
        b = a * shifted(b, d, 0.0) + b
        a = a * shifted(a, d, 1.0)
        d *= 2
    o_ref[0] = b * jax.nn.gelu(xg_ref[0])


def _block_diag(w):
    n, c, _ = w.shape
    out = jnp.zeros((n * c, n * c), w.dtype)
    for j in range(n):
        out = out.at[j * c:(j + 1) * c, j * c:(j + 1) * c].set(w[j])
    return out


def rglru_block(h, conv_w, conv_b, ga_w, ga_b, gx_w, gx_b, lru_lambda):
    B, S, _ = h.shape
    W = RNN_W
    cx, cg = OFF["cx"] // W, OFF["cg"] // W
    vec = pl.BlockSpec((1, W), lambda b: (0, 0))
    mat = pl.BlockSpec((W, W), lambda b: (0, 0))
    return pl.pallas_call(
        _rglru_kernel,
        grid=(B,),
        in_specs=[pl.BlockSpec((1, S, W), lambda b: (b, 0, cx)), pl.BlockSpec((1, S, W), lambda b: (b, 0, cg)),
                  pl.BlockSpec((CONV_W, W), lambda b: (0, 0)), vec, mat, vec, mat, vec, vec],
        out_specs=pl.BlockSpec((1, S, W), lambda b: (b, 0, 0)),
        out_shape=jax.ShapeDtypeStruct((B, S, W), f32),
        compiler_params=_params(1),
        name="rglru",
    )(h, h, conv_w, conv_b.reshape(1, W), _block_diag(ga_w).astype(MXU), ga_b.reshape(1, W),
      _block_diag(gx_w).astype(MXU), gx_b.reshape(1, W), lru_lambda.reshape(1, W))


def _rms(x, g, width):
    return x * lax.rsqrt(jnp.sum(x * x, axis=-1, keepdims=True) * (1.0 / width) + RMS_EPS) * g


def _mla_prep_kernel(cq_ref, ckv_ref, kr_ref, krs_ref, gq_ref, gkv_ref, wq_ref, wqs_ref, wk_ref, wv_ref,
                     cosq_ref, sinq_ref, cosk_ref, sink_ref, vone_ref, q_ref, k_ref, v_ref):
    cq = _rms(cq_ref[0], gq_ref[...], MLA_Q_RANK).astype(MXU)
    ckv = _rms(ckv_ref[0], gkv_ref[...], MLA_KV_RANK).astype(MXU)
    scale = (MLA_NOPE + MLA_ROPE) ** -0.5
    q = _dot(cq, wq_ref[...]) * cosq_ref[...] + _dot(cq, wqs_ref[...]) * sinq_ref[...]
    q_ref[0] = (q * scale).astype(q_ref.dtype)
    k_rope = kr_ref[0] * cosk_ref[...] + krs_ref[0] * sink_ref[...]
    k = _dot(ckv, wk_ref[...])
    k_ref[0] = (k + jnp.concatenate([k_rope] * MLA_HEADS, axis=1)).astype(k_ref.dtype)
    v_ref[0] = (_dot(ckv, wv_ref[...]) + vone_ref[...]).astype(v_ref.dtype)


def _mla_attn_kernel(q_ref, k_ref, v_ref, cb_ref, o_ref, *, tq):
    tk = tq
    qi = pl.program_id(2)
    q0 = pl.multiple_of(qi * tq, tq)
    q = q_ref[0]
    lane = lax.broadcasted_iota(jnp.int32, (tq, LANE), 1)

    def update(carry, n, k, v, bias):
        m, acc = carry
        sl = slice(n * LANE, (n + 1) * LANE)
        s = _dot_nt(q[:, sl], k[:, sl])
        if bias is not None:
            s = s + bias
        m_new = jnp.maximum(m, jnp.max(s, axis=1, keepdims=True))
        p = jnp.exp(s - m_new)
        return m_new, jnp.exp(m - m_new) * acc + _dot(p.astype(MXU), v[:, sl])

    def body(kt, carry):
        k0 = pl.multiple_of(kt * tk, tk)
        k = k_ref[0, pl.ds(k0, tk), :]
        v = v_ref[0, pl.ds(k0, tk), :]
        return tuple(update(carry[n], n, k, v, None) for n in range(2))

    init = tuple((jnp.full((tq, 1), NEG, f32), jnp.zeros((tq, LANE), f32)) for _ in range(2))
    carry = lax.fori_loop(0, qi, body, init)
    k = k_ref[0, pl.ds(q0, tk), :]
    v = v_ref[0, pl.ds(q0, tk), :]
    o = []
    for n in range(2):
        _, acc = update(carry[n], n, k, v, cb_ref[...])
        denom = jnp.sum(jnp.where(lane == MLA_V, acc, 0.0), axis=1, keepdims=True)
        o.append(acc / jnp.maximum(denom, 1e-30))
    o_ref[0] = jnp.where(lane < MLA_V, o[0], pltpu.roll(o[1], MLA_V, 1))


def mla_attention(h, q_norm, kv_norm, w_uq, w_ukv, tr=_TR_MLA_PREP, tq=_TQ_MLA):
    B, S, _ = h.shape
    H = MLA_HEADS
    dq = MLA_NOPE + MLA_ROPE
    HW = H * LANE
    wq3 = w_uq.reshape(MLA_Q_RANK, H, dq)
    wq_rot = jnp.concatenate([jnp.zeros_like(wq3[..., :MLA_NOPE]), _rot_half(wq3[..., MLA_NOPE:])], axis=-1)

    def pad_q(w3):
        w3 = jnp.pad(w3, ((0, _CQ_PAD - MLA_Q_RANK), (0, 0), (0, LANE - dq)))
        return w3.reshape(_CQ_PAD, HW).astype(MXU)

    wkv3 = w_ukv.reshape(MLA_KV_RANK, H, MLA_NOPE + MLA_V)
    wk = jnp.pad(wkv3[..., :MLA_NOPE], ((0, 0), (0, 0), (0, LANE - MLA_NOPE))).reshape(MLA_KV_RANK, HW).astype(MXU)
    wv = jnp.pad(wkv3[..., MLA_NOPE:], ((0, 0), (0, 0), (0, LANE - MLA_V))).reshape(MLA_KV_RANK, HW).astype(MXU)
    v_one = jnp.tile(jnp.concatenate([jnp.zeros((1, MLA_V), f32), jnp.ones((1, LANE - MLA_V), f32)], axis=1), (1, H))
    gq = jnp.pad(q_norm, (0, _CQ_PAD - MLA_Q_RANK)).reshape(1, _CQ_PAD)
    gkv = kv_norm.reshape(1, MLA_KV_RANK)
    inv = ROPE_THETA ** (-jnp.arange(0, MLA_ROPE, 2, dtype=f32) / MLA_ROPE)
    ang = jnp.arange(S, dtype=f32)[:, None] * inv[None, :]
    cos2 = jnp.concatenate([jnp.cos(ang), jnp.cos(ang)], axis=1)
    sin2 = jnp.concatenate([jnp.sin(ang), jnp.sin(ang)], axis=1)
    tail = LANE - dq
    cos_k = jnp.concatenate([jnp.zeros((S, MLA_NOPE), f32), cos2, jnp.zeros((S, tail), f32)], axis=1)
    sin_k = jnp.concatenate([jnp.zeros((S, MLA_NOPE), f32), sin2, jnp.zeros((S, tail), f32)], axis=1)
    cos_q = jnp.tile(jnp.concatenate([jnp.ones((S, MLA_NOPE), f32), cos2, jnp.zeros((S, tail), f32)], axis=1), (1, H))
    sin_q = jnp.tile(sin_k, (1, H))
    rel = np.arange(tq)[:, None] - np.arange(tq)[None, :]
    causal = jnp.asarray(np.where(rel >= 0, 0.0, BIG_NEG).astype(np.float32))

    c_cq, c_ckv, c_kr, c_krs = OFF["dcq"] // _CQ_PAD, OFF["dckv"] // LANE, OFF["dkr"] // LANE, OFF["dkrs"] // LANE
    full = lambda shape: pl.BlockSpec(shape, lambda b, i: (0, 0))
    tab = lambda w: pl.BlockSpec((tr, w), lambda b, i: (i, 0))
    out3 = pl.BlockSpec((1, tr, HW), lambda b, i: (b, i, 0))
    q, k, v = pl.pallas_call(
        _mla_prep_kernel,
        grid=(B, S // tr),
        in_specs=[pl.BlockSpec((1, tr, _CQ_PAD), lambda b, i: (b, i, c_cq)),
                  pl.BlockSpec((1, tr, LANE), lambda b, i: (b, i, c_ckv)),
                  pl.BlockSpec((1, tr, LANE), lambda b, i: (b, i, c_kr)),
                  pl.BlockSpec((1, tr, LANE), lambda b, i: (b, i, c_krs)),
                  full((1, _CQ_PAD)), full((1, MLA_KV_RANK)), full((_CQ_PAD, HW)), full((_CQ_PAD, HW)),
                  full((MLA_KV_RANK, HW)), full((MLA_KV_RANK, HW)),
                  tab(HW), tab(HW), tab(LANE), tab(LANE), full((1, HW))],
        out_specs=[out3, out3, out3],
        out_shape=[jax.ShapeDtypeStruct((B, S, HW), MXU)] * 3,
        compiler_params=_params(2),
        name="mla_prep",
    )(h, h, h, h, gq, gkv, pad_q(wq3), pad_q(wq_rot), wk, wv, cos_q, sin_q, cos_k, sin_k, v_one)
    pair = lambda rows: pl.BlockSpec((1, rows, 2 * LANE), lambda b, p, i: (b, i if rows == tq else 0, p))
    return pl.pallas_call(
        functools.partial(_mla_attn_kernel, tq=tq),
        grid=(B, H // 2, S // tq),
        in_specs=[pair(tq), pair(S), pair(S), pl.BlockSpec((tq, tq), lambda b, p, i: (0, 0))],
        out_specs=pl.BlockSpec((1, tq, LANE), lambda b, p, i: (b, i, p)),
        out_shape=jax.ShapeDtypeStruct((B, S, H * MLA_V), f32),
        compiler_params=_params(3),
        name="mla_attention",
    )(q, k, v, causal)


def _ln(z, g, b):
    mu = jnp.mean(z, axis=-1, keepdims=True)
    zc = z - mu
    var = jnp.mean(zc * zc, axis=-1, keepdims=True)
    return zc * lax.rsqrt(var + LN_EPS) * g + b


def _merge_kernel(mg_ref, oa_ref, ob_ref, oc_ref, od_ref, wb_ref, wo_ref, x_ref, g_ref, b_ref, o_ref):
    acc = None
    for n, br in enumerate((oa_ref, ob_ref, oc_ref, od_ref)):
        up = _dot(br[...].astype(MXU), wb_ref[n])
        term = jax.nn.sigmoid(mg_ref[:, n * D_MODEL:(n + 1) * D_MODEL]) * up
        acc = term if acc is None else acc + term
    y = _dot(acc.astype(MXU), wo_ref[...])
    o_ref[...] = _ln(DN_ALPHA * x_ref[...] + y, g_ref[...], b_ref[...])


def merge_ln(h2, branches, w_branch, w_out, x2, g, b, tm=_TM_DENSE):
    N, D = x2.shape
    assert OFF["mg"] == 0 and N % tm == 0
    row = lambda w: pl.BlockSpec((tm, w), lambda i: (i, 0))
    return pl.pallas_call(
        _merge_kernel,
        grid=(N // tm,),
        in_specs=[row(N_MIXERS * D)] + [row(MIX_W)] * N_MIXERS
        + [pl.BlockSpec((N_MIXERS, MIX_W, D), lambda i: (0, 0, 0)), pl.BlockSpec((D, D), lambda i: (0, 0)),
           row(D), pl.BlockSpec((1, D), lambda i: (0, 0)), pl.BlockSpec((1, D), lambda i: (0, 0))],
        out_specs=row(D),
        out_shape=jax.ShapeDtypeStruct((N, D), f32),
        compiler_params=_params(1),
        name="merge_ln",
    )(h2, *branches, w_branch.astype(MXU), w_out.astype(MXU), x2, g.reshape(1, D), b.reshape(1, D))


ROW_TILE = 8


def _to_token_tiles(ref, val):
    rows = val.shape[0]
    for j in range(ROW_TILE):
        ref[pl.ds(j, rows, stride=ROW_TILE), :] = val[:, j * LANE:(j + 1) * LANE]


def _from_token_tiles(ref, rows, first=0, stride=ROW_TILE):
    return jnp.concatenate([ref[pl.ds(first + j, rows, stride=stride), :] for j in range(ROW_TILE)], axis=1)


def _xattn_kernel(x_ref, wq_ref, k_ref, v_ref, wo_ref, g_ref, b_ref, o_ref, o8_ref):
    x = x_ref[0]
    q = _dot(x.astype(MXU), wq_ref[...]).astype(MXU)
    k = k_ref[0]
    v = v_ref[0]
    heads = []
    for hd in range(X_HEADS):
        sl = slice(hd * X_HEAD_DIM, (hd + 1) * X_HEAD_DIM)
        s = _dot_nt(q[:, sl], k[:, sl]) * (X_HEAD_DIM ** -0.5)
        e = jnp.exp(s - jnp.max(s, axis=1, keepdims=True))
        p = e / jnp.sum(e, axis=1, keepdims=True)
        heads.append(_dot(p.astype(MXU), v[:, sl]).astype(MXU))
    y = _dot(jnp.concatenate(heads, axis=1), wo_ref[...])
    out = _ln(DN_ALPHA * x + y, g_ref[...], b_ref[...])
    o_ref[0] = out
    _to_token_tiles(o8_ref.at[0], out)


def cross_attention_ln(x, mem, wq, wkv, wo, g, b, tq=_TM_DENSE):
    B, S, D = x.shape
    assert D == ROW_TILE * LANE
    M = mem.shape[1]
    F = X_HEADS * X_HEAD_DIM
    kv = matmul(mem.reshape(B * M, D), wkv.astype(MXU)[None], 0, 512, 2 * F, out_dtype=MXU).reshape(B, M, 2 * F)
    full = lambda shape: pl.BlockSpec(shape, lambda bi, i: (0,) * len(shape))
    return pl.pallas_call(
        _xattn_kernel,
        grid=(B, S // tq),
        in_specs=[pl.BlockSpec((1, tq, D), lambda bi, i: (bi, i, 0)), full((D, F)),
                  pl.BlockSpec((1, M, F), lambda bi, i: (bi, 0, 0)), pl.BlockSpec((1, M, F), lambda bi, i: (bi, 0, 1)),
                  full((F, D)), full((1, D)), full((1, D))],
        out_specs=[pl.BlockSpec((1, tq, D), lambda bi, i: (bi, i, 0)),
                   pl.BlockSpec((1, tq * ROW_TILE, LANE), lambda bi, i: (bi, i, 0))],
        out_shape=[jax.ShapeDtypeStruct((B, S, D), f32), jax.ShapeDtypeStruct((B, S * ROW_TILE, LANE), f32)],
        compiler_params=_params(2),
        name="cross_attention_ln",
    )(x, wq.astype(MXU), kv, kv, wo.astype(MXU), g.reshape(1, D), b.reshape(1, D))


_R_E0, _R_E1, _R_W0, _R_W1, _R_RANK0, _R_RANK1 = range(6)
_GRP_LANE0 = N_EXPERTS


def _router_kernel(x_ref, w_ref, b_ref, tri_ref, r_ref, cnt_ref):
    i = pl.program_id(0)
    tm = x_ref.shape[0]
    logits = _dot_nt(w_ref[...], x_ref[...].astype(MXU)) + b_ref[...]
    row = lax.broadcasted_iota(jnp.int32, (LANE, tm), 0)
    row_f = row.astype(f32)
    big = float(LANE)

    def cmax(t):
        return jnp.max(t, axis=0, keepdims=True)

    def first_row(cond):
        return jnp.min(jnp.where(cond, row_f, big), axis=0, keepdims=True)

    def softmax_on(mask):
        lm = jnp.where(mask, logits, NEG)
        e = jnp.where(mask, jnp.exp(lm - cmax(lm)), 0.0)
        return e / jnp.sum(e, axis=0, keepdims=True)

    is_g = (row >= _GRP_LANE0) & (row < _GRP_LANE0 + N_GROUPS)
    p_grp = softmax_on(is_g)
    p_g = cmax(p_grp)
    grp = first_row(is_g & (p_grp == p_g)) - float(_GRP_LANE0)
    grp_of_row = jnp.right_shift(row, EXPERTS_PER_GROUP.bit_length() - 1)
    in_grp = (row < N_EXPERTS) & (grp_of_row.astype(f32) == grp)
    p_e = softmax_on(in_grp)
    p1 = cmax(jnp.where(in_grp, p_e, -1.0))
    e1 = first_row(in_grp & (p_e == p1))
    rest = in_grp & (row_f != e1)
    p2 = cmax(jnp.where(rest, p_e, -1.0))
    e2 = first_row(rest & (p_e == p2))
    w1 = p_g * p1 / (p1 + p2)
    w2 = p_g * p2 / (p1 + p2)

    @pl.when(i == 0)
    def _():
        cnt_ref[...] = jnp.zeros_like(cnt_ref)

    oh1 = row_f == e1
    oh2 = row_f == e2
    both = (oh1 | oh2).astype(MXU)
    before = _dot(both, tri_ref[...]) + cnt_ref[:, 0:1]
    rank1 = jnp.sum(jnp.where(oh1, before, 0.0), axis=0, keepdims=True)
    rank2 = jnp.sum(jnp.where(oh2, before, 0.0), axis=0, keepdims=True)
    cnt_ref[...] = cnt_ref[...] + jnp.sum(both.astype(f32), axis=1, keepdims=True)

    out = jnp.zeros((LANE, tm), f32)
    for slot, val in ((_R_E0, e1), (_R_E1, e2), (_R_W0, w1), (_R_W1, w2), (_R_RANK0, rank1), (_R_RANK1, rank2)):
        out = jnp.where(row == slot, val, out)
    r_ref[...] = out.T


def moe_router(x2, rg_w, rg_b, re_w, re_b, tm=_TM_DENSE):
    N, D = x2.shape
    assert EXPERTS_PER_GROUP & (EXPERTS_PER_GROUP - 1) == 0 and N_EXPERTS + N_GROUPS <= LANE
    w = jnp.pad(jnp.concatenate([re_w, rg_w], axis=1), ((0, 0), (0, LANE - N_EXPERTS - N_GROUPS))).T.astype(MXU)
    b = jnp.pad(jnp.concatenate([re_b, rg_b]), (0, LANE - N_EXPERTS - N_GROUPS)).reshape(LANE, 1)
    tri = jnp.asarray(np.triu(np.ones((tm, tm), np.float32), 1), MXU)
    return pl.pallas_call(
        _router_kernel,
        grid=(N // tm,),
        in_specs=[pl.BlockSpec((tm, D), lambda i: (i, 0)), pl.BlockSpec((LANE, D), lambda i: (0, 0)),
                  pl.BlockSpec((LANE, 1), lambda i: (0, 0)), pl.BlockSpec((tm, tm), lambda i: (0, 0))],
        out_specs=[pl.BlockSpec((tm, LANE), lambda i: (i, 0)), pl.BlockSpec((LANE, LANE), lambda i: (0, 0))],
        out_shape=[jax.ShapeDtypeStruct((N, LANE), f32), jax.ShapeDtypeStruct((LANE, LANE), f32)],
        compiler_params=_params(1),
        name="moe_router",
    )(x2, w, b, tri)


def _ffn_kernel(ce_ref, first_ref, slot_ref, next_ref, nu_ref, x_ref, wgu_hbm, wd_hbm, o_ref,
                wgu_f32, wd_f32, wgu_b, wd_b, sems, *, layer):
    c = pl.program_id(0)
    used = c < nu_ref[0]

    def fetch(e, slot):
        return (pltpu.make_async_copy(wgu_hbm.at[layer, e], wgu_f32.at[slot], sems.at[slot, 0]),
                pltpu.make_async_copy(wd_hbm.at[layer, e], wd_f32.at[slot], sems.at[slot, 1]))

    @pl.when(used & (c == 0))
    def _():
        for cp in fetch(ce_ref[0], 0):
            cp.start()

    @pl.when(used & (first_ref[c] == 1))
    def _():
        slot = slot_ref[c]
        for cp in fetch(ce_ref[c], slot):
            cp.wait()

        @pl.when(next_ref[c] >= 0)
        def _():
            for cp in fetch(next_ref[c], 1 - slot):
                cp.start()

        wgu_b[...] = wgu_f32[slot].astype(MXU)
        wd_b[...] = wd_f32[slot].astype(MXU)

    @pl.when(used)
    def _():
        x = _from_token_tiles(x_ref, EXPERT_CHUNK)
        gu = _dot(x.astype(MXU), wgu_b[...])
        hid = jax.nn.silu(gu[:, :D_EXPERT]) * gu[:, D_EXPERT:]
        _to_token_tiles(o_ref, _dot(hid.astype(MXU), wd_b[...]))

    @pl.when(jnp.logical_not(used))
    def _():
        o_ref[...] = jnp.zeros_like(o_ref)


def expert_ffn(xb8, chunk_e, n_used, w_gu, w_down, layer):
    C = EXPERT_CHUNK
    n_chunks = xb8.shape[0] // (C * ROW_TILE)
    D = ROW_TILE * LANE
    idx = jnp.arange(n_chunks, dtype=jnp.int32)
    used = idx < n_used[0]
    first = used & ((idx == 0) | (chunk_e != jnp.roll(chunk_e, 1)))
    slot = (jnp.cumsum(first.astype(jnp.int32)) - 1) % 2
    none = jnp.int32(N_EXPERTS)
    later = jnp.roll(jnp.where(first, chunk_e, none), -1).at[-1].set(none)
    next_e = lax.cummin(later, axis=0, reverse=True)
    next_e = jnp.where(next_e == none, -1, next_e)
    tile_spec = pl.BlockSpec((C * ROW_TILE, LANE), lambda c, *_: (c, 0))
    any_spec = pl.BlockSpec(memory_space=pl.ANY)
    grid_spec = pltpu.PrefetchScalarGridSpec(
        num_scalar_prefetch=5,
        grid=(n_chunks,),
        in_specs=[tile_spec, any_spec, any_spec],
        out_specs=tile_spec,
        scratch_shapes=[pltpu.VMEM((2, D, 2 * D_EXPERT), w_gu.dtype), pltpu.VMEM((2, D_EXPERT, D), w_down.dtype),
                        pltpu.VMEM((D, 2 * D_EXPERT), MXU), pltpu.VMEM((D_EXPERT, D), MXU),
                        pltpu.SemaphoreType.DMA((2, 2))],
    )
    return pl.pallas_call(
        functools.partial(_ffn_kernel, layer=layer),
        grid_spec=grid_spec,
        out_shape=jax.ShapeDtypeStruct(xb8.shape, f32),
        compiler_params=_params(1),
        name="expert_ffn",
    )(chunk_e, first.astype(jnp.int32), slot.astype(jnp.int32), next_e.astype(jnp.int32), n_used, xb8, w_gu, w_down)


_COPY_WINDOW = 512


def _windowed_copies(n, copy, wait):
    assert _COPY_WINDOW & (_COPY_WINDOW - 1) == 0 and n >= _COPY_WINDOW

    def fill(j, carry):
        copy(j, j).start()
        return carry

    def steady(j, carry):
        s = jnp.bitwise_and(j, _COPY_WINDOW - 1)
        wait(s)
        copy(j, s).start()
        return carry

    def drain(s, carry):
        wait(s)
        return carry

    lax.fori_loop(0, _COPY_WINDOW, fill, 0, unroll=8)
    lax.fori_loop(_COPY_WINDOW, n, steady, 0, unroll=8)
    lax.fori_loop(0, _COPY_WINDOW, drain, 0, unroll=8)


def _dispatch_kernel(dest_ref, tail_ref, nu_ref, x8_ref, xb_hbm, sems, zeros, *, tm):
    n = tm * TOPK_IN_GROUP
    base = pl.program_id(0) * n

    @pl.when(pl.program_id(0) == 0)
    def _():
        zeros[...] = jnp.zeros_like(zeros)
        chunk_rows = zeros.shape[0]
        n_chunks = xb_hbm.shape[0] // chunk_rows

        def fill(row, s):
            return pltpu.make_async_copy(zeros, xb_hbm.at[pl.ds(pl.multiple_of(row, ROW_TILE), chunk_rows)], sems.at[s])

        def tail_fill(e):
            return fill(jnp.maximum(tail_ref[e], 0) * ROW_TILE, e)

        def free_fill(c):
            return fill(c * chunk_rows, N_EXPERTS + c - nu_ref[0])

        for e in range(N_EXPERTS):
            @pl.when(tail_ref[e] >= 0)
            def _():
                tail_fill(e).start()

        lax.fori_loop(nu_ref[0], n_chunks, lambda c, carry: (free_fill(c).start(), carry)[1], 0)
        for e in range(N_EXPERTS):
            @pl.when(tail_ref[e] >= 0)
            def _():
                tail_fill(e).wait()

        lax.fori_loop(nu_ref[0], n_chunks, lambda c, carry: (free_fill(c).wait(), carry)[1], 0)

    def copy(j, s):
        t = pl.multiple_of(jnp.right_shift(j, 1) * ROW_TILE, ROW_TILE)
        d = pl.multiple_of(dest_ref[base + j] * ROW_TILE, ROW_TILE)
        return pltpu.make_async_copy(x8_ref.at[pl.ds(t, ROW_TILE)], xb_hbm.at[pl.ds(d, ROW_TILE)], sems.at[s])

    def wait(s):
        pltpu.make_async_copy(x8_ref.at[pl.ds(0, ROW_TILE)], xb_hbm.at[pl.ds(0, ROW_TILE)], sems.at[s]).wait()

    _windowed_copies(n, copy, wait)


def moe_dispatch(x8, dest, tail, n_used, n_slots, tm=_TM_DISPATCH):
    n_tok = x8.shape[0] // ROW_TILE
    n_chunks = n_slots // EXPERT_CHUNK
    assert n_tok % tm == 0 and tm * TOPK_IN_GROUP >= _COPY_WINDOW
    assert _COPY_WINDOW >= N_EXPERTS + n_chunks - n_tok * TOPK_IN_GROUP // EXPERT_CHUNK
    grid_spec = pltpu.PrefetchScalarGridSpec(
        num_scalar_prefetch=3,
        grid=(n_tok // tm,),
        in_specs=[pl.BlockSpec((tm * ROW_TILE, LANE), lambda i, *_: (i, 0))],
        out_specs=pl.BlockSpec(memory_space=pl.ANY),
        scratch_shapes=[pltpu.SemaphoreType.DMA((_COPY_WINDOW,)),
                        pltpu.VMEM((EXPERT_CHUNK * ROW_TILE, LANE), x8.dtype)],
    )
    return pl.pallas_call(
        functools.partial(_dispatch_kernel, tm=tm),
        grid_spec=grid_spec,
        out_shape=jax.ShapeDtypeStruct((n_slots * ROW_TILE, LANE), x8.dtype),
        compiler_params=_params(1),
        name="moe_dispatch",
    )(dest, tail, n_used, x8)


def _combine_kernel(dest_ref, x_ref, r_ref, g_ref, b_ref, yb_hbm, o_ref, buf, sems):
    tm = x_ref.shape[0]
    n = tm * TOPK_IN_GROUP
    i = pl.program_id(0)

    def issue(tile):
        dst_buf = buf.at[jnp.bitwise_and(tile, 1)]

        def body(j, carry):
            src = pl.multiple_of(dest_ref[tile * n + j] * ROW_TILE, ROW_TILE)
            dst = pl.multiple_of(j * ROW_TILE, ROW_TILE)
            pltpu.make_async_copy(yb_hbm.at[pl.ds(src, ROW_TILE)], dst_buf.at[pl.ds(dst, ROW_TILE)],
                                  sems.at[j]).start()
            return carry

        lax.fori_loop(0, n, body, 0, unroll=8)

    def drain(j, carry):
        pltpu.make_async_copy(yb_hbm.at[pl.ds(0, ROW_TILE)], buf.at[0, pl.ds(0, ROW_TILE)], sems.at[j]).wait()
        return carry

    @pl.when(i == 0)
    def _():
        issue(i)

    lax.fori_loop(0, n, drain, 0, unroll=8)

    @pl.when(i + 1 < pl.num_programs(0))
    def _():
        issue(i + 1)

    cur = buf.at[jnp.bitwise_and(i, 1)]
    r = r_ref[...]
    lane = lax.broadcasted_iota(jnp.int32, r.shape, 1)
    w0 = jnp.sum(jnp.where(lane == _R_W0, r, 0.0), axis=1, keepdims=True)
    w1 = jnp.sum(jnp.where(lane == _R_W1, r, 0.0), axis=1, keepdims=True)
    pair = TOPK_IN_GROUP * ROW_TILE
    y = _from_token_tiles(cur, tm, 0, pair) * w0 + _from_token_tiles(cur, tm, ROW_TILE, pair) * w1
    o_ref[...] = _ln(DN_ALPHA * x_ref[...] + y, g_ref[...], b_ref[...])


def combine_ln(x2, yb8, dest, r, g, b, tm=_TM_ROWS):
    N, D = x2.shape
    assert N % tm == 0
    row = lambda w: pl.BlockSpec((tm, w), lambda i, dest: (i, 0))
    vec = pl.BlockSpec((1, D), lambda i, dest: (0, 0))
    grid_spec = pltpu.PrefetchScalarGridSpec(
        num_scalar_prefetch=1,
        grid=(N // tm,),
        in_specs=[row(D), row(LANE), vec, vec, pl.BlockSpec(memory_space=pl.ANY)],
        out_specs=row(D),
        scratch_shapes=[pltpu.VMEM((2, tm * TOPK_IN_GROUP * ROW_TILE, LANE), f32),
                        pltpu.SemaphoreType.DMA((tm * TOPK_IN_GROUP,))],
    )
    return pl.pallas_call(
        _combine_kernel,
        grid_spec=grid_spec,
        out_shape=jax.ShapeDtypeStruct((N, D), f32),
        compiler_params=_params(1),
        name="moe_combine_ln",
    )(dest, x2, r, g.reshape(1, D), b.reshape(1, D), yb8)


def hier_moe_ln(x2, x8, rg_w, rg_b, re_w, re_b, w_gu, w_down, layer, g, b):
    N, D = x2.shape
    E, C, K = N_EXPERTS, EXPERT_CHUNK, TOPK_IN_GROUP
    A = N * K
    r, cnt = moe_router(x2, rg_w, rg_b, re_w, re_b)
    e = r[:, _R_E0:_R_E1 + 1].astype(jnp.int32)
    rank = r[:, _R_RANK0:_R_RANK1 + 1].astype(jnp.int32)
    counts = cnt[:E, 0].astype(jnp.int32)
    padded = (counts + C - 1) // C * C
    pad_end = jnp.cumsum(padded)
    below = jnp.arange(E, dtype=jnp.int32)[None, None, :] < e[..., None]
    dest = rank + jnp.sum(jnp.where(below, padded[None, None, :], 0), axis=-1)
    n_chunks = -(-(A + E * (C - 1)) // C)
    P = n_chunks * C
    chunk_start = jnp.arange(n_chunks, dtype=jnp.int32) * C
    chunk_e = jnp.minimum(jnp.sum((pad_end[None, :] <= chunk_start[:, None]).astype(jnp.int32), axis=1), E - 1)
    n_used = (pad_end[-1] // C).reshape(1).astype(jnp.int32)
    dest = dest.reshape(A).astype(jnp.int32)
    tail = jnp.where(padded > 0, pad_end - C, -1).astype(jnp.int32)
    xb8 = moe_dispatch(x8, dest, tail, n_used, P)
    yb8 = expert_ffn(xb8, chunk_e.astype(jnp.int32), n_used, w_gu, w_down, layer)
    return combine_ln(x2, yb8, dest, r, g, b)


def kernel(x, mem, w_in, nsa_cmp_pos, nsa_cmp_w1, nsa_cmp_w2, rnn_conv_w, rnn_conv_b, rnn_ga_w, rnn_ga_b,
           rnn_gx_w, rnn_gx_b, rnn_lambda, mla_q_norm, mla_kv_norm, mla_w_uq, mla_w_ukv, w_branch, w_out,
           ln1_g, ln1_b, x_wq, x_wkv, x_wo, ln2_g, ln2_b, moe_rg_w, moe_rg_b, moe_re_w, moe_re_b,
           moe_w_gu, moe_w_down, ln3_g, ln3_b):
    B, S, D = x.shape
    N = B * S
    x2 = x.reshape(N, D)
    w_cat = _cat_w_in(w_in)
    for l in range(DEPTH):
        h2 = matmul(x2, w_cat, l, 512, _IN_TILE)
        h = h2.reshape(B, S, N_CAT)
        o_a = nsa_attention(h, nsa_cmp_pos[l], nsa_cmp_w1[l], nsa_cmp_w2[l])
        o_b = stick_breaking_attention(h)
        o_c = rglru_block(h, rnn_conv_w[l], rnn_conv_b[l], rnn_ga_w[l], rnn_ga_b[l], rnn_gx_w[l], rnn_gx_b[l],
                          rnn_lambda[l])
        o_d = mla_attention(h, mla_q_norm[l], mla_kv_norm[l], mla_w_uq[l], mla_w_ukv[l])
        branches = [o.reshape(N, MIX_W) for o in (o_a, o_b, o_c, o_d)]
        x2 = merge_ln(h2, branches, w_branch[l], w_out[l], x2, ln1_g[l], ln1_b[l])
        x3, x8 = cross_attention_ln(x2.reshape(B, S, D), mem, x_wq[l], x_wkv[l], x_wo[l], ln2_g[l], ln2_b[l])
        x2 = hier_moe_ln(x3.reshape(N, D), x8.reshape(N * ROW_TILE, LANE), moe_rg_w[l], moe_rg_b[l], moe_re_w[l],
                         moe_re_b[l], moe_w_gu, moe_w_down, l, ln3_g[l], ln3_b[l])
    return x2.reshape(B, S, D)
```

```python
import functools

import numpy as np
import jax
import jax.numpy as jnp
from jax import lax
from jax.experimental import pallas as pl
from jax.experimental.pallas import tpu as pltpu

D_MODEL = 1024
DEPTH = 4
HEAD_DIM = 64
N_MIXERS = 4
MIX_W = 256
NSA_HEADS = 4
NSA_KV_HEADS = 2
NSA_GROUP = NSA_HEADS // NSA_KV_HEADS
CMP_LEN = 32
CMP_STRIDE = 16
CMP_HID = 256
SEL_BLOCK = 64
SEL_TOPK = 8
WINDOW = 512
FORCE_SCORE = 1e4
SB_HEADS = 4
RNN_W = 256
CONV_W = 4
LRU_C = 8.0
MLA_HEADS = 4
MLA_Q_RANK = 192
MLA_KV_RANK = 128
MLA_NOPE = 64
MLA_ROPE = 32
MLA_V = 64
ROPE_THETA = 10000.0
X_HEADS = 4
X_HEAD_DIM = 128
N_GROUPS = 4
EXPERTS_PER_GROUP = 8
N_EXPERTS = N_GROUPS * EXPERTS_PER_GROUP
TOPK_IN_GROUP = 2
D_EXPERT = 512
EXPERT_CHUNK = 256
DN_ALPHA = (2.0 * DEPTH) ** 0.25
LN_EPS = 1e-5
RMS_EPS = 1e-6

IN_SPLITS = ((NSA_HEADS * HEAD_DIM,) + (NSA_KV_HEADS * HEAD_DIM,) * 6 + (NSA_HEADS * 3,)
             + (SB_HEADS * HEAD_DIM,) * 3
             + (RNN_W, RNN_W)
             + (MLA_Q_RANK, MLA_KV_RANK, MLA_ROPE)
             + (N_MIXERS * D_MODEL,))
IN_OFFSETS = tuple(int(o) for o in np.concatenate([[0], np.cumsum(IN_SPLITS)[:-1]]))

LANE = 128
VMEM_LIMIT = 48 * 1024 * 1024
NEG = -1e30
BIG_NEG = -2.0 ** 100

f32 = jnp.float32
MXU = jnp.bfloat16

_CQ_PAD = 2 * LANE
_W_Q, _W_KV, _W_KVT = NSA_HEADS * HEAD_DIM, NSA_KV_HEADS * HEAD_DIM, NSA_KV_HEADS * LANE
_SECTIONS = (("mg", N_MIXERS * D_MODEL), ("aq", _W_Q), ("kvs", _W_KVT), ("kvw", _W_KVT),
             ("bq", SB_HEADS * HEAD_DIM), ("bk", SB_HEADS * HEAD_DIM), ("bv", SB_HEADS * HEAD_DIM),
             ("cx", RNN_W), ("cg", RNN_W), ("dcq", _CQ_PAD), ("kc", _W_KV), ("vc", _W_KV),
             ("ag", LANE), ("dckv", MLA_KV_RANK), ("dkr", LANE), ("dkrs", LANE))
_IN_TILE = 3584
OFF = {}
_o = 0
for _n, _w in _SECTIONS:
    OFF[_n] = _o
    _o += _w
N_CAT = -(-_o // _IN_TILE) * _IN_TILE

_SEL_ROWS = 32
_AUG_SEL = HEAD_DIM
_AUG_POS = HEAD_DIM + _SEL_ROWS
_POS_SPLIT = 256

_TQ_NSA = 512
_TQ_MLA = 1024
_TQ_SB = 256
_TM_ROWS = 256
_TM_DENSE = 512
_TM_DISPATCH = 256
_TR_MLA_PREP = 1024


def _rot_half(t):
    d = t.shape[-1]
    return jnp.concatenate([-t[..., d // 2:], t[..., :d // 2]], axis=-1)


def _cat_w_in(w):
    def sec(i):
        return w[..., IN_OFFSETS[i]:IN_OFFSETS[i] + IN_SPLITS[i]]

    def pair(k, v):
        return jnp.concatenate([k[..., :HEAD_DIM], v[..., :HEAD_DIM], k[..., HEAD_DIM:], v[..., HEAD_DIM:]], axis=-1)

    def pad_cols(t, before, after):
        return jnp.pad(t, ((0, 0),) * (t.ndim - 1) + ((before, after),))

    def padc(t, n):
        return pad_cols(t, 0, n - t.shape[-1])

    def rope_slot(t):
        return pad_cols(t, MLA_NOPE, LANE - MLA_NOPE - MLA_ROPE)

    parts = {"mg": sec(16), "aq": sec(0), "kc": sec(1), "vc": sec(2), "kvs": pair(sec(3), sec(4)),
             "kvw": pair(sec(5), sec(6)), "ag": padc(sec(7), LANE), "bq": sec(8), "bk": sec(9),
             "bv": sec(10), "cx": sec(11), "cg": sec(12), "dcq": padc(sec(13), _CQ_PAD), "dckv": sec(14),
             "dkr": rope_slot(sec(15)), "dkrs": rope_slot(_rot_half(sec(15)))}
    cat = jnp.concatenate([parts[n] for n, _ in _SECTIONS], axis=-1)
    return padc(cat, N_CAT).astype(MXU)


def _dot(a, b):
    return jnp.dot(a, b, preferred_element_type=f32)


def _dot_nt(a, b):
    return lax.dot_general(a, b, (((1,), (1,)), ((), ())), preferred_element_type=f32)


def _params(n_axes):
    return pltpu.CompilerParams(dimension_semantics=("arbitrary",) * n_axes, vmem_limit_bytes=VMEM_LIMIT)


def _mm_kernel(a_ref, b_ref, o_ref):
    o_ref[...] = _dot(a_ref[...].astype(MXU), b_ref[0]).astype(o_ref.dtype)


def matmul(a, b, layer, tm, tn, out_dtype=f32):
    M, K = a.shape
    _, _, N = b.shape
    assert M % tm == 0 and N % tn == 0
    return pl.pallas_call(
        _mm_kernel,
        grid=(N // tn, M // tm),
        in_specs=[pl.BlockSpec((tm, K), lambda j, i: (i, 0)),
                  pl.BlockSpec((1, K, tn), lambda j, i: (layer, 0, j))],
        out_specs=pl.BlockSpec((tm, tn), lambda j, i: (i, j)),
        out_shape=jax.ShapeDtypeStruct((M, N), out_dtype),
        compiler_params=_params(2),
        name="matmul",
    )(a, b)


def _cmp_kernel(t_ref, w1_ref, pos_ref, w2_ref, o_ref):
    nc = t_ref.shape[1] // CMP_STRIDE
    t = jnp.concatenate([t_ref[0, pl.ds(l, nc, stride=CMP_STRIDE), :] for l in range(CMP_STRIDE)], axis=1)
    half = t.shape[1]
    t = t.astype(MXU)
    y1 = _dot(t, w1_ref[0, :half, :])
    y2 = _dot(t, w1_ref[0, half:, :])
    pos = jnp.broadcast_to(pos_ref[0], (8, 2 * half)).astype(MXU)
    pc = _dot(pos, w1_ref[0])[0:1]
    nc = y2.shape[0]
    hid = y1 + pltpu.roll(y2, nc - 1, 0) + pc
    o_ref[0, 0] = _dot(jax.nn.gelu(hid).astype(MXU), w2_ref[0])


def nsa_compress(h, cmp_w1, cmp_pos, cmp_w2):
    B, S, _ = h.shape
    G = NSA_KV_HEADS
    NC = S // CMP_STRIDE
    F = CMP_STRIDE * G * HEAD_DIM
    assert G * HEAD_DIM == LANE and OFF["vc"] == OFF["kc"] + LANE
    col0 = OFF["kc"] // LANE
    def group_diag(w, axis):
        blocks = [jnp.concatenate([w if k == g else jnp.zeros_like(w) for k in range(G)], axis=-1) for g in range(G)]
        return jnp.stack(blocks, axis=axis)

    w1 = group_diag(cmp_w1.reshape(2, CMP_LEN, HEAD_DIM, CMP_HID).astype(MXU), 2)
    w1 = w1.reshape(2, 2 * F, G * CMP_HID)
    pos = jnp.broadcast_to(cmp_pos[:, :, None, :], (2, CMP_LEN, G, HEAD_DIM)).reshape(2, 1, 2 * F)
    w2 = group_diag(jnp.concatenate([cmp_w2, cmp_w2], axis=-1).astype(MXU), 1)
    w2 = w2.reshape(2, G * CMP_HID, G * LANE)
    return pl.pallas_call(
        _cmp_kernel,
        grid=(2, B),
        in_specs=[pl.BlockSpec((1, S, LANE), lambda j, i: (i, 0, col0 + j)),
                  pl.BlockSpec((1, 2 * F, G * CMP_HID), lambda j, i: (j, 0, 0)),
                  pl.BlockSpec((1, 1, 2 * F), lambda j, i: (j, 0, 0)),
                  pl.BlockSpec((1, G * CMP_HID, G * LANE), lambda j, i: (j, 0, 0))],
        out_specs=pl.BlockSpec((1, 1, NC, G * LANE), lambda j, i: (j, i, 0, 0)),
        out_shape=jax.ShapeDtypeStruct((2, B, NC, G * LANE), f32),
        compiler_params=_params(2),
        name="nsa_compress",
    )(h, w1, pos, w2)


def _nsa_kernel(q_ref, kc_ref, vc_ref, kvs_ref, kvw_ref, gl_ref, cover_ref, aug_ref, cb_ref, wb_ref,
                o_ref, ksa, vsa, kwa, vwa, *, tq, n_cmp, n_sel, n_top):
    tk = tq
    g = pl.program_id(1)
    qi = pl.program_id(2)
    q0 = pl.multiple_of(qi * tq, tq)
    lane = lax.broadcasted_iota(jnp.int32, (tq, LANE), 1)
    lo_half = lane < HEAD_DIM

    @pl.when(qi == 0)
    def _():
        real = lax.broadcasted_iota(jnp.int32, ksa.shape, 1) < HEAD_DIM
        aug = aug_ref[...]
        ones = jnp.ones(ksa.shape, MXU)
        for kv_ref, k_out, v_out in ((kvs_ref, ksa, vsa), (kvw_ref, kwa, vwa)):
            kv = kv_ref[0]
            k_out[...] = jnp.where(real, kv.astype(MXU), aug)
            v_out[...] = jnp.where(real, pltpu.roll(kv, HEAD_DIM, 1).astype(MXU), ones)

    q = q_ref[0] * (HEAD_DIM ** -0.5)
    q_heads = (jnp.where(lo_half, q, 0.0), jnp.where(lo_half, pltpu.roll(q, HEAD_DIM, 1), 0.0))
    alibi = [2.0 ** (-8.0 * (h + 1) / NSA_HEADS) for h in range(NSA_HEADS)]
    slopes = [jnp.where(g == 0, alibi[n], alibi[NSA_GROUP + n]) for n in range(NSA_GROUP)]
    pos_cols = [jnp.where(lane == _AUG_POS, slopes[n] * _POS_SPLIT, jnp.where(lane == _AUG_POS + 1, slopes[n], 0.0))
                for n in range(NSA_GROUP)]
    tpos = q0 + lax.broadcasted_iota(jnp.int32, (tq, 1), 0)

    nc = kc_ref.shape[2]
    cidx = lax.broadcasted_iota(jnp.int32, (1, nc), 1)
    dist_c = tpos - (cidx * CMP_STRIDE + (CMP_LEN - 1))
    mask_c = (dist_c >= 0) & (cidx < n_cmp)
    dist_cf = dist_c.astype(f32)
    kc = kc_ref[0, 0].astype(MXU)
    vc = vc_ref[0, 0].astype(MXU)
    o_cmp = []
    imp_t = jnp.zeros((LANE, tq), f32)
    for n in range(NSA_GROUP):
        s = _dot_nt(q_heads[n].astype(MXU), kc) - slopes[n] * dist_cf
        sm = jnp.where(mask_c, s, NEG)
        m = jnp.max(sm, axis=1, keepdims=True)
        p = jnp.where(mask_c, jnp.exp(sm - m), 0.0)
        p = (p / jnp.maximum(jnp.sum(p, axis=1, keepdims=True), 1e-30)).astype(MXU)
        o_cmp.append(_dot(p, vc))
        imp_t = imp_t + _dot_nt(cover_ref[...], p)

    rows = _SEL_ROWS
    imp = imp_t[:rows]
    blk = lax.broadcasted_iota(jnp.int32, (rows, tq), 0)
    blk_f = blk.astype(f32)
    tpos_t = q0 + lax.broadcasted_iota(jnp.int32, (1, tq), 1)
    forced = (blk == 0) | (blk == jnp.right_shift(tpos_t, SEL_BLOCK.bit_length() - 1))
    valid = blk * SEL_BLOCK <= tpos_t
    imp = jnp.where(forced, FORCE_SCORE, jnp.where(valid, imp, -1.0))
    imp = jnp.where(blk < n_sel, imp, NEG)
    sel_t = jnp.zeros((rows, tq), f32)
    for _ in range(n_top):
        m = jnp.max(imp, axis=0, keepdims=True)
        first = jnp.min(jnp.where(imp == m, blk_f, float(LANE)), axis=0, keepdims=True)
        pick = blk_f == first
        sel_t = jnp.where(pick, 1.0, sel_t)
        imp = jnp.where(pick, 2 * NEG, imp)
    sel = jnp.concatenate([sel_t, jnp.zeros((LANE - rows, tq), f32)], axis=0).T
    sel_bias = pltpu.roll(jnp.where(sel > 0.5, 0.0, BIG_NEG), _AUG_SEL, 1)
    sel_cols = jnp.where((lane >= _AUG_SEL) & (lane < _AUG_SEL + rows), sel_bias, 0.0)
    q_sel = jnp.concatenate([q_heads[n] + sel_cols + pos_cols[n] for n in range(NSA_GROUP)], axis=0).astype(MXU)
    q_win = jnp.concatenate([q_heads[n] + pos_cols[n] for n in range(NSA_GROUP)], axis=0).astype(MXU)
    rows_q = NSA_GROUP * tq
    denom_lane = lax.broadcasted_iota(jnp.int32, (rows_q, LANE), 1) == HEAD_DIM

    def update(carry, qs, k, v, bias):
        m, acc = carry
        s = _dot_nt(qs, k)
        if bias is not None:
            s = s + bias
        m_new = jnp.maximum(m, jnp.max(s, axis=1, keepdims=True))
        p = jnp.exp(s - m_new)
        return m_new, jnp.exp(m - m_new) * acc + _dot(p.astype(MXU), v)

    def finish(carry):
        _, acc = carry
        denom = jnp.sum(jnp.where(denom_lane, acc, 0.0), axis=1, keepdims=True)
        o = acc / jnp.maximum(denom, 1e-30)
        return [o[n * tq:(n + 1) * tq] for n in range(NSA_GROUP)]

    init = (jnp.full((rows_q, 1), NEG, f32), jnp.zeros((rows_q, LANE), f32))
    causal = cb_ref[...]

    def sel_body(kt, carry):
        k0 = pl.multiple_of(kt * tk, tk)
        return update(carry, q_sel, ksa[pl.ds(k0, tk), :], vsa[pl.ds(k0, tk), :], None)

    carry = lax.fori_loop(0, qi, sel_body, init)
    o_sel = finish(update(carry, q_sel, ksa[pl.ds(q0, tk), :], vsa[pl.ds(q0, tk), :], causal))

    carry = init
    n_back = WINDOW // tk
    for back in range(n_back, -1, -1):
        k0 = pl.multiple_of(jnp.maximum(qi - back, 0) * tk, tk)
        if back:
            off = jnp.where(qi >= back, 0.0, BIG_NEG)
            bias = wb_ref[...] + off if back == n_back else off
        else:
            bias = causal
        carry = update(carry, q_win, kwa[pl.ds(k0, tk), :], vwa[pl.ds(k0, tk), :], bias)
    o_win = finish(carry)

    sig = jax.nn.sigmoid(gl_ref[0])

    def gate(n, j):
        col = 3 * (NSA_GROUP * g + n) + j
        return jnp.sum(jnp.where(lane == col, sig, 0.0), axis=1, keepdims=True)

    o = [gate(n, 0) * o_cmp[n] + gate(n, 1) * o_sel[n] + gate(n, 2) * o_win[n] for n in range(NSA_GROUP)]
    o_ref[0] = jnp.where(lo_half, o[0], pltpu.roll(o[1], HEAD_DIM, 1))


def nsa_attention(h, cmp_pos, cmp_w1, cmp_w2, tq=_TQ_NSA):
    B, S, _ = h.shape
    G = NSA_KV_HEADS
    NC = S // CMP_STRIDE
    n_cmp = (S - CMP_LEN) // CMP_STRIDE + 1
    n_sel = S // SEL_BLOCK
    n_top = min(SEL_TOPK, n_sel)
    F = CMP_STRIDE * HEAD_DIM
    assert S % tq == 0 and WINDOW % tq == 0 and tq % SEL_BLOCK == 0 and n_sel <= _SEL_ROWS and S <= _POS_SPLIT ** 2

    kvc = nsa_compress(h, cmp_w1, cmp_pos, cmp_w2)

    c0 = np.arange(n_cmp)[:, None] * CMP_STRIDE
    j0 = np.arange(n_sel)[None, :] * SEL_BLOCK
    cover = np.clip(np.minimum(c0 + CMP_LEN, j0 + SEL_BLOCK) - np.maximum(c0, j0), 0, None) / CMP_LEN
    cover_t = np.zeros((LANE, NC), np.float32)
    cover_t[:n_sel, :n_cmp] = cover.T
    pos = np.arange(S)
    aug = np.zeros((S, LANE), np.float32)
    aug[pos, _AUG_SEL + pos // SEL_BLOCK] = 1.0
    aug[:, _AUG_POS] = pos // _POS_SPLIT
    aug[:, _AUG_POS + 1] = pos % _POS_SPLIT
    rel = np.arange(tq)[:, None] - np.arange(tq)[None, :]
    rel = np.tile(rel, (NSA_GROUP, 1))
    causal = np.where(rel >= 0, 0.0, BIG_NEG).astype(np.float32)
    win_lo = np.where(rel < 0, 0.0, BIG_NEG).astype(np.float32)

    col = lambda name: OFF[name] // LANE
    full2 = lambda shape: pl.BlockSpec(shape, lambda b, g, i: (0, 0))
    q_spec = pl.BlockSpec((1, tq, LANE), lambda b, g, i: (b, i, col("aq") + g))
    c_specs = [pl.BlockSpec((1, 1, NC, LANE), lambda b, g, i, j=j: (j, b, 0, g)) for j in range(2)]
    kv_specs = [pl.BlockSpec((1, S, LANE), lambda b, g, i, c=col(n): (b, 0, c + g))
                for n in ("kvs", "kvw")]
    gl_spec = pl.BlockSpec((1, tq, LANE), lambda b, g, i: (b, i, col("ag")))
    return pl.pallas_call(
        functools.partial(_nsa_kernel, tq=tq, n_cmp=n_cmp, n_sel=n_sel, n_top=n_top),
        grid=(B, G, S // tq),
        in_specs=[q_spec] + c_specs + kv_specs + [gl_spec, full2((LANE, NC)), full2((S, LANE)),
                                                  full2((NSA_GROUP * tq, tq)), full2((NSA_GROUP * tq, tq))],
        out_specs=pl.BlockSpec((1, tq, LANE), lambda b, g, i: (b, i, g)),
        out_shape=jax.ShapeDtypeStruct((B, S, NSA_HEADS * HEAD_DIM), f32),
        scratch_shapes=[pltpu.VMEM((S, LANE), MXU)] * 4,
        compiler_params=_params(3),
        name="nsa_attention",
    )(h, kvc, kvc, h, h, h, jnp.asarray(cover_t, MXU), jnp.asarray(aug, MXU), jnp.asarray(causal),
      jnp.asarray(win_lo))


_SB_DEAD = -104.0


def _log_sigmoid(z):
    return jnp.minimum(z, 0.0) - jnp.log1p(jnp.exp(-jnp.abs(z)))


def _sb_kernel(q_ref, k_ref, v_ref, u_ref, o_ref, kb, vb, *, tq):
    tk = tq
    qi = pl.program_id(2)
    q0 = pl.multiple_of(qi * tq, tq)
    lane = lax.broadcasted_iota(jnp.int32, (tq, LANE), 1)
    lo_half = lane < HEAD_DIM

    @pl.when(qi == 0)
    def _():
        kb[...] = k_ref[0].astype(MXU)
        vb[...] = v_ref[0].astype(MXU)

    q = q_ref[0] * (HEAD_DIM ** -0.5)
    q2 = jnp.concatenate([jnp.where(lo_half, q, 0.0), jnp.where(lo_half, 0.0, q)], axis=0).astype(MXU)
    u = u_ref[...]

    def tile(carry, k, v, strict):
        c, acc = carry
        z = _dot_nt(q2, k)
        ls = _log_sigmoid(z)
        log_1m = ls - z
        if strict is not None:
            log_1m = jnp.where(strict, log_1m, 0.0)
        hi = log_1m.astype(MXU)
        lo = (log_1m - hi.astype(f32)).astype(MXU)
        tail = _dot(hi, u) + _dot(lo, u) + c
        a = jnp.exp(ls + tail)
        if strict is not None:
            a = jnp.where(strict, a, 0.0)
        return c + jnp.sum(log_1m, axis=1, keepdims=True), acc + _dot(a.astype(MXU), v)

    row = lax.broadcasted_iota(jnp.int32, (2 * tq, tk), 0)
    rel = jnp.where(row >= tq, row - tq, row) - lax.broadcasted_iota(jnp.int32, (2 * tq, tk), 1)
    zero = (jnp.zeros((2 * tq, 1), f32), jnp.zeros((2 * tq, LANE), f32))
    state = tile(zero, kb[pl.ds(q0, tk), :], vb[pl.ds(q0, tk), :], rel > 0)

    def alive(state):
        return (jnp.max(state[0]) > _SB_DEAD).astype(jnp.int32)

    def cond(loop):
        kt, live, _ = loop
        return (kt >= 0) & (live > 0)

    def body(loop):
        kt, _, state = loop
        k0 = pl.multiple_of(kt * tk, tk)
        state = tile(state, kb[pl.ds(k0, tk), :], vb[pl.ds(k0, tk), :], None)
        return kt - 1, alive(state), state

    _, _, (_, acc) = lax.while_loop(cond, body, (qi - 1, alive(state), state))
    o_ref[0] = jnp.where(lo_half, acc[:tq], acc[tq:])


def stick_breaking_attention(h, tq=_TQ_SB):
    B, S, _ = h.shape
    assert S % tq == 0
    tri = jnp.asarray(np.tril(np.ones((tq, tq), np.float32), -1), MXU)
    cq, ck, cv = (OFF[n] // LANE for n in ("bq", "bk", "bv"))
    return pl.pallas_call(
        functools.partial(_sb_kernel, tq=tq),
        grid=(B, SB_HEADS // 2, S // tq),
        in_specs=[pl.BlockSpec((1, tq, LANE), lambda b, p, i: (b, i, cq + p)),
                  pl.BlockSpec((1, S, LANE), lambda b, p, i: (b, 0, ck + p)),
                  pl.BlockSpec((1, S, LANE), lambda b, p, i: (b, 0, cv + p)),
                  pl.BlockSpec((tq, tq), lambda b, p, i: (0, 0))],
        out_specs=pl.BlockSpec((1, tq, LANE), lambda b, p, i: (b, i, p)),
        out_shape=jax.ShapeDtypeStruct((B, S, SB_HEADS * HEAD_DIM), f32),
        scratch_shapes=[pltpu.VMEM((S, LANE), MXU)] * 2,
        compiler_params=_params(3),
        name="sb_attention",
    )(h, h, h, tri)


def _neg_expm1(y):
    series = -y * (1.0 + y * (1.0 / 2 + y * (1.0 / 6 + y * (1.0 / 24 + y * (1.0 / 120)))))
    return jnp.where(y > -0.1, series, 1.0 - jnp.exp(y))


def _rglru_kernel(x_ref, xg_ref, cw_ref, cb_ref, gaw_ref, gab_ref, gxw_ref, gxb_ref, lam_ref, o_ref):
    x = x_ref[0]
    S = x.shape[0]
    row = lax.broadcasted_iota(jnp.int32, (S, 1), 0)

    def shifted(t, d, fill):
        return jnp.where(row >= d, pltpu.roll(t, d, 0), fill)

    u = cb_ref[...] + x * cw_ref[CONV_W - 1:CONV_W, :]
    for d in range(1, CONV_W):
        u = u + shifted(x, d, 0.0) * cw_ref[CONV_W - 1 - d:CONV_W - d, :]
    ub = u.astype(MXU)
    r = jax.nn.sigmoid(_dot(ub, gaw_ref[...]) + gab_ref[...])
    i = jax.nn.sigmoid(_dot(ub, gxw_ref[...]) + gxb_ref[...])
    lam = lam_ref[...]
    softplus_neg = jnp.maximum(-lam, 0.0) + jnp.log1p(jnp.exp(-jnp.abs(lam)))
    log_a = -LRU_C * r * softplus_neg
    a = jnp.exp(log_a)
    b = jnp.sqrt(_neg_expm1(2.0 * log_a)) * (i * u)
    d = 1
    while d < S:
        b = a * shifted(b, d, 0.0) + b
        a = a * shifted(a, d, 1.0)
        d *= 2
    o_ref[0] = b * jax.nn.gelu(xg_ref[0])


def _block_diag(w):
    n, c, _ = w.shape
    out = jnp.zeros((n * c, n * c), w.dtype)
    for j in range(n):
        out = out.at[j * c:(j + 1) * c, j * c:(j + 1) * c].set(w[j])
    return out


def rglru_block(h, conv_w, conv_b, ga_w, ga_b, gx_w, gx_b, lru_lambda):
    B, S, _ = h.shape
    W = RNN_W
    cx, cg = OFF["cx"] // W, OFF["cg"] // W
    vec = pl.BlockSpec((1, W), lambda b: (0, 0))
    mat = pl.BlockSpec((W, W), lambda b: (0, 0))
    return pl.pallas_call(
        _rglru_kernel,
        grid=(B,),
        in_specs=[pl.BlockSpec((1, S, W), lambda b: (b, 0, cx)), pl.BlockSpec((1, S, W), lambda b: (b, 0, cg)),
                  pl.BlockSpec((CONV_W, W), lambda b: (0, 0)), vec, mat, vec, mat, vec, vec],
        out_specs=pl.BlockSpec((1, S, W), lambda b: (b, 0, 0)),
        out_shape=jax.ShapeDtypeStruct((B, S, W), f32),
        compiler_params=_params(1),
        name="rglru",
    )(h, h, conv_w, conv_b.reshape(1, W), _block_diag(ga_w).astype(MXU), ga_b.reshape(1, W),
      _block_diag(gx_w).astype(MXU), gx_b.reshape(1, W), lru_lambda.reshape(1, W))


def _rms(x, g, width):
    return x * lax.rsqrt(jnp.sum(x * x, axis=-1, keepdims=True) * (1.0 / width) + RMS_EPS) * g


def _mla_prep_kernel(cq_ref, ckv_ref, kr_ref, krs_ref, gq_ref, gkv_ref, wq_ref, wqs_ref, wk_ref, wv_ref,
                     cosq_ref, sinq_ref, cosk_ref, sink_ref, vone_ref, q_ref, k_ref, v_ref):
    cq = _rms(cq_ref[0], gq_ref[...], MLA_Q_RANK).astype(MXU)
    ckv = _rms(ckv_ref[0], gkv_ref[...], MLA_KV_RANK).astype(MXU)
    scale = (MLA_NOPE + MLA_ROPE) ** -0.5
    q = _dot(cq, wq_ref[...]) * cosq_ref[...] + _dot(cq, wqs_ref[...]) * sinq_ref[...]
    q_ref[0] = (q * scale).astype(q_ref.dtype)
    k_rope = kr_ref[0] * cosk_ref[...] + krs_ref[0] * sink_ref[...]
    k = _dot(ckv, wk_ref[...])
    k_ref[0] = (k + jnp.concatenate([k_rope] * MLA_HEADS, axis=1)).astype(k_ref.dtype)
    v_ref[0] = (_dot(ckv, wv_ref[...]) + vone_ref[...]).astype(v_ref.dtype)


def _mla_attn_kernel(q_ref, k_ref, v_ref, cb_ref, o_ref, *, tq):
    tk = tq
    qi = pl.program_id(2)
    q0 = pl.multiple_of(qi * tq, tq)
    q = q_ref[0]
    lane = lax.broadcasted_iota(jnp.int32, (tq, LANE), 1)

    def update(carry, n, k, v, bias):
        m, acc = carry
        sl = slice(n * LANE, (n + 1) * LANE)
        s = _dot_nt(q[:, sl], k[:, sl])
        if bias is not None:
            s = s + bias
        m_new = jnp.maximum(m, jnp.max(s, axis=1, keepdims=True))
        p = jnp.exp(s - m_new)
        return m_new, jnp.exp(m - m_new) * acc + _dot(p.astype(MXU), v[:, sl])

    def body(kt, carry):
        k0 = pl.multiple_of(kt * tk, tk)
        k = k_ref[0, pl.ds(k0, tk), :]
        v = v_ref[0, pl.ds(k0, tk), :]
        return tuple(update(carry[n], n, k, v, None) for n in range(2))

    init = tuple((jnp.full((tq, 1), NEG, f32), jnp.zeros((tq, LANE), f32)) for _ in range(2))
    carry = lax.fori_loop(0, qi, body, init)
    k = k_ref[0, pl.ds(q0, tk), :]
    v = v_ref[0, pl.ds(q0, tk), :]
    o = []
    for n in range(2):
        _, acc = update(carry[n], n, k, v, cb_ref[...])
        denom = jnp.sum(jnp.where(lane == MLA_V, acc, 0.0), axis=1, keepdims=True)
        o.append(acc / jnp.maximum(denom, 1e-30))
    o_ref[0] = jnp.where(lane < MLA_V, o[0], pltpu.roll(o[1], MLA_V, 1))


def mla_attention(h, q_norm, kv_norm, w_uq, w_ukv, tr=_TR_MLA_PREP, tq=_TQ_MLA):
    B, S, _ = h.shape
    H = MLA_HEADS
    dq = MLA_NOPE + MLA_ROPE
    HW = H * LANE
    wq3 = w_uq.reshape(MLA_Q_RANK, H, dq)
    wq_rot = jnp.concatenate([jnp.zeros_like(wq3[..., :MLA_NOPE]), _rot_half(wq3[..., MLA_NOPE:])], axis=-1)

    def pad_q(w3):
        w3 = jnp.pad(w3, ((0, _CQ_PAD - MLA_Q_RANK), (0, 0), (0, LANE - dq)))
        return w3.reshape(_CQ_PAD, HW).astype(MXU)

    wkv3 = w_ukv.reshape(MLA_KV_RANK, H, MLA_NOPE + MLA_V)
    wk = jnp.pad(wkv3[..., :MLA_NOPE], ((0, 0), (0, 0), (0, LANE - MLA_NOPE))).reshape(MLA_KV_RANK, HW).astype(MXU)
    wv = jnp.pad(wkv3[..., MLA_NOPE:], ((0, 0), (0, 0), (0, LANE - MLA_V))).reshape(MLA_KV_RANK, HW).astype(MXU)
    v_one = jnp.tile(jnp.concatenate([jnp.zeros((1, MLA_V), f32), jnp.ones((1, LANE - MLA_V), f32)], axis=1), (1, H))
    gq = jnp.pad(q_norm, (0, _CQ_PAD - MLA_Q_RANK)).reshape(1, _CQ_PAD)
    gkv = kv_norm.reshape(1, MLA_KV_RANK)
    inv = ROPE_THETA ** (-jnp.arange(0, MLA_ROPE, 2, dtype=f32) / MLA_ROPE)
    ang = jnp.arange(S, dtype=f32)[:, None] * inv[None, :]
    cos2 = jnp.concatenate([jnp.cos(ang), jnp.cos(ang)], axis=1)
    sin2 = jnp.concatenate([jnp.sin(ang), jnp.sin(ang)], axis=1)
    tail = LANE - dq
    cos_k = jnp.concatenate([jnp.zeros((S, MLA_NOPE), f32), cos2, jnp.zeros((S, tail), f32)], axis=1)
    sin_k = jnp.concatenate([jnp.zeros((S, MLA_NOPE), f32), sin2, jnp.zeros((S, tail), f32)], axis=1)
    cos_q = jnp.tile(jnp.concatenate([jnp.ones((S, MLA_NOPE), f32), cos2, jnp.zeros((S, tail), f32)], axis=1), (1, H))
    sin_q = jnp.tile(sin_k, (1, H))
    rel = np.arange(tq)[:, None] - np.arange(tq)[None, :]
    causal = jnp.asarray(np.where(rel >= 0, 0.0, BIG_NEG).astype(np.float32))

    c_cq, c_ckv, c_kr, c_krs = OFF["dcq"] // _CQ_PAD, OFF["dckv"] // LANE, OFF["dkr"] // LANE, OFF["dkrs"] // LANE
    full = lambda shape: pl.BlockSpec(shape, lambda b, i: (0, 0))
    tab = lambda w: pl.BlockSpec((tr, w), lambda b, i: (i, 0))
    out3 = pl.BlockSpec((1, tr, HW), lambda b, i: (b, i, 0))
    q, k, v = pl.pallas_call(
        _mla_prep_kernel,
        grid=(B, S // tr),
        in_specs=[pl.BlockSpec((1, tr, _CQ_PAD), lambda b, i: (b, i, c_cq)),
                  pl.BlockSpec((1, tr, LANE), lambda b, i: (b, i, c_ckv)),
                  pl.BlockSpec((1, tr, LANE), lambda b, i: (b, i, c_kr)),
                  pl.BlockSpec((1, tr, LANE), lambda b, i: (b, i, c_krs)),
                  full((1, _CQ_PAD)), full((1, MLA_KV_RANK)), full((_CQ_PAD, HW)), full((_CQ_PAD, HW)),
                  full((MLA_KV_RANK, HW)), full((MLA_KV_RANK, HW)),
                  tab(HW), tab(HW), tab(LANE), tab(LANE), full((1, HW))],
        out_specs=[out3, out3, out3],
        out_shape=[jax.ShapeDtypeStruct((B, S, HW), MXU)] * 3,
        compiler_params=_params(2),
        name="mla_prep",
    )(h, h, h, h, gq, gkv, pad_q(wq3), pad_q(wq_rot), wk, wv, cos_q, sin_q, cos_k, sin_k, v_one)
    pair = lambda rows: pl.BlockSpec((1, rows, 2 * LANE), lambda b, p, i: (b, i if rows == tq else 0, p))
    return pl.pallas_call(
        functools.partial(_mla_attn_kernel, tq=tq),
        grid=(B, H // 2, S // tq),
        in_specs=[pair(tq), pair(S), pair(S), pl.BlockSpec((tq, tq), lambda b, p, i: (0, 0))],
        out_specs=pl.BlockSpec((1, tq, LANE), lambda b, p, i: (b, i, p)),
        out_shape=jax.ShapeDtypeStruct((B, S, H * MLA_V), f32),
        compiler_params=_params(3),
        name="mla_attention",
    )(q, k, v, causal)


def _ln(z, g, b):
    mu = jnp.mean(z, axis=-1, keepdims=True)
    zc = z - mu
    var = jnp.mean(zc * zc, axis=-1, keepdims=True)
    return zc * lax.rsqrt(var + LN_EPS) * g + b


def _merge_kernel(mg_ref, oa_ref, ob_ref, oc_ref, od_ref, wb_ref, wo_ref, x_ref, g_ref, b_ref, o_ref):
    acc = None
    for n, br in enumerate((oa_ref, ob_ref, oc_ref, od_ref)):
        up = _dot(br[...].astype(MXU), wb_ref[n])
        term = jax.nn.sigmoid(mg_ref[:, n * D_MODEL:(n + 1) * D_MODEL]) * up
        acc = term if acc is None else acc + term
    y = _dot(acc.astype(MXU), wo_ref[...])
    o_ref[...] = _ln(DN_ALPHA * x_ref[...] + y, g_ref[...], b_ref[...])


def merge_ln(h2, branches, w_branch, w_out, x2, g, b, tm=_TM_DENSE):
    N, D = x2.shape
    assert OFF["mg"] == 0 and N % tm == 0
    row = lambda w: pl.BlockSpec((tm, w), lambda i: (i, 0))
    return pl.pallas_call(
        _merge_kernel,
        grid=(N // tm,),
        in_specs=[row(N_MIXERS * D)] + [row(MIX_W)] * N_MIXERS
        + [pl.BlockSpec((N_MIXERS, MIX_W, D), lambda i: (0, 0, 0)), pl.BlockSpec((D, D), lambda i: (0, 0)),
           row(D), pl.BlockSpec((1, D), lambda i: (0, 0)), pl.BlockSpec((1, D), lambda i: (0, 0))],
        out_specs=row(D),
        out_shape=jax.ShapeDtypeStruct((N, D), f32),
        compiler_params=_params(1),
        name="merge_ln",
    )(h2, *branches, w_branch.astype(MXU), w_out.astype(MXU), x2, g.reshape(1, D), b.reshape(1, D))


ROW_TILE = 8


def _to_token_tiles(ref, val):
    rows = val.shape[0]
    for j in range(ROW_TILE):
        ref[pl.ds(j, rows, stride=ROW_TILE), :] = val[:, j * LANE:(j + 1) * LANE]


def _from_token_tiles(ref, rows, first=0, stride=ROW_TILE):
    return jnp.concatenate([ref[pl.ds(first + j, rows, stride=stride), :] for j in range(ROW_TILE)], axis=1)


def _xattn_kernel(x_ref, wq_ref, k_ref, v_ref, wo_ref, g_ref, b_ref, o_ref, o8_ref):
    x = x_ref[0]
    q = _dot(x.astype(MXU), wq_ref[...]).astype(MXU)
    k = k_ref[0]
    v = v_ref[0]
    heads = []
    for hd in range(X_HEADS):
        sl = slice(hd * X_HEAD_DIM, (hd + 1) * X_HEAD_DIM)
        s = _dot_nt(q[:, sl], k[:, sl]) * (X_HEAD_DIM ** -0.5)
        e = jnp.exp(s - jnp.max(s, axis=1, keepdims=True))
        p = e / jnp.sum(e, axis=1, keepdims=True)
        heads.append(_dot(p.astype(MXU), v[:, sl]).astype(MXU))
    y = _dot(jnp.concatenate(heads, axis=1), wo_ref[...])
    out = _ln(DN_ALPHA * x + y, g_ref[...], b_ref[...])
    o_ref[0] = out
    _to_token_tiles(o8_ref.at[0], out)


def cross_attention_ln(x, mem, wq, wkv, wo, g, b, tq=_TM_DENSE):
    B, S, D = x.shape
    assert D == ROW_TILE * LANE
    M = mem.shape[1]
    F = X_HEADS * X_HEAD_DIM
    kv = matmul(mem.reshape(B * M, D), wkv.astype(MXU)[None], 0, 512, 2 * F, out_dtype=MXU).reshape(B, M, 2 * F)
    full = lambda shape: pl.BlockSpec(shape, lambda bi, i: (0,) * len(shape))
    return pl.pallas_call(
        _xattn_kernel,
        grid=(B, S // tq),
        in_specs=[pl.BlockSpec((1, tq, D), lambda bi, i: (bi, i, 0)), full((D, F)),
                  pl.BlockSpec((1, M, F), lambda bi, i: (bi, 0, 0)), pl.BlockSpec((1, M, F), lambda bi, i: (bi, 0, 1)),
                  full((F, D)), full((1, D)), full((1, D))],
        out_specs=[pl.BlockSpec((1, tq, D), lambda bi, i: (bi, i, 0)),
                   pl.BlockSpec((1, tq * ROW_TILE, LANE), lambda bi, i: (bi, i, 0))],
        out_shape=[jax.ShapeDtypeStruct((B, S, D), f32), jax.ShapeDtypeStruct((B, S * ROW_TILE, LANE), f32)],
        compiler_params=_params(2),
        name="cross_attention_ln",
    )(x, wq.astype(MXU), kv, kv, wo.astype(MXU), g.reshape(1, D), b.reshape(1, D))


_R_E0, _R_E1, _R_W0, _R_W1, _R_RANK0, _R_RANK1 = range(6)
_GRP_LANE0 = N_EXPERTS


def _router_kernel(x_ref, w_ref, b_ref, tri_ref, r_ref, cnt_ref):
    i = pl.program_id(0)
    tm = x_ref.shape[0]
    logits = _dot_nt(w_ref[...], x_ref[...].astype(MXU)) + b_ref[...]
    row = lax.broadcasted_iota(jnp.int32, (LANE, tm), 0)
    row_f = row.astype(f32)
    big = float(LANE)

    def cmax(t):
        return jnp.max(t, axis=0, keepdims=True)

    def first_row(cond):
        return jnp.min(jnp.where(cond, row_f, big), axis=0, keepdims=True)

    def softmax_on(mask):
        lm = jnp.where(mask, logits, NEG)
        e = jnp.where(mask, jnp.exp(lm - cmax(lm)), 0.0)
        return e / jnp.sum(e, axis=0, keepdims=True)

    is_g = (row >= _GRP_LANE0) & (row < _GRP_LANE0 + N_GROUPS)
    p_grp = softmax_on(is_g)
    p_g = cmax(p_grp)
    grp = first_row(is_g & (p_grp == p_g)) - float(_GRP_LANE0)
    grp_of_row = jnp.right_shift(row, EXPERTS_PER_GROUP.bit_length() - 1)
    in_grp = (row < N_EXPERTS) & (grp_of_row.astype(f32) == grp)
    p_e = softmax_on(in_grp)
    p1 = cmax(jnp.where(in_grp, p_e, -1.0))
    e1 = first_row(in_grp & (p_e == p1))
    rest = in_grp & (row_f != e1)
    p2 = cmax(jnp.where(rest, p_e, -1.0))
    e2 = first_row(rest & (p_e == p2))
    w1 = p_g * p1 / (p1 + p2)
    w2 = p_g * p2 / (p1 + p2)

    @pl.when(i == 0)
    def _():
        cnt_ref[...] = jnp.zeros_like(cnt_ref)

    oh1 = row_f == e1
    oh2 = row_f == e2
    both = (oh1 | oh2).astype(MXU)
    before = _dot(both, tri_ref[...]) + cnt_ref[:, 0:1]
    rank1 = jnp.sum(jnp.where(oh1, before, 0.0), axis=0, keepdims=True)
    rank2 = jnp.sum(jnp.where(oh2, before, 0.0), axis=0, keepdims=True)
    cnt_ref[...] = cnt_ref[...] + jnp.sum(both.astype(f32), axis=1, keepdims=True)

    out = jnp.zeros((LANE, tm), f32)
    for slot, val in ((_R_E0, e1), (_R_E1, e2), (_R_W0, w1), (_R_W1, w2), (_R_RANK0, rank1), (_R_RANK1, rank2)):
        out = jnp.where(row == slot, val, out)
    r_ref[...] = out.T


def moe_router(x2, rg_w, rg_b, re_w, re_b, tm=_TM_DENSE):
    N, D = x2.shape
    assert EXPERTS_PER_GROUP & (EXPERTS_PER_GROUP - 1) == 0 and N_EXPERTS + N_GROUPS <= LANE
    w = jnp.pad(jnp.concatenate([re_w, rg_w], axis=1), ((0, 0), (0, LANE - N_EXPERTS - N_GROUPS))).T.astype(MXU)
    b = jnp.pad(jnp.concatenate([re_b, rg_b]), (0, LANE - N_EXPERTS - N_GROUPS)).reshape(LANE, 1)
    tri = jnp.asarray(np.triu(np.ones((tm, tm), np.float32), 1), MXU)
    return pl.pallas_call(
        _router_kernel,
        grid=(N // tm,),
        in_specs=[pl.BlockSpec((tm, D), lambda i: (i, 0)), pl.BlockSpec((LANE, D), lambda i: (0, 0)),
                  pl.BlockSpec((LANE, 1), lambda i: (0, 0)), pl.BlockSpec((tm, tm), lambda i: (0, 0))],
        out_specs=[pl.BlockSpec((tm, LANE), lambda i: (i, 0)), pl.BlockSpec((LANE, LANE), lambda i: (0, 0))],
        out_shape=[jax.ShapeDtypeStruct((N, LANE), f32), jax.ShapeDtypeStruct((LANE, LANE), f32)],
        compiler_params=_params(1),
        name="moe_router",
    )(x2, w, b, tri)


def _ffn_kernel(ce_ref, first_ref, slot_ref, next_ref, nu_ref, x_ref, wgu_hbm, wd_hbm, o_ref,
                wgu_f32, wd_f32, wgu_b, wd_b, sems, *, layer):
    c = pl.program_id(0)
    used = c < nu_ref[0]

    def fetch(e, slot):
        return (pltpu.make_async_copy(wgu_hbm.at[layer, e], wgu_f32.at[slot], sems.at[slot, 0]),
                pltpu.make_async_copy(wd_hbm.at[layer, e], wd_f32.at[slot], sems.at[slot, 1]))

    @pl.when(used & (c == 0))
    def _():
        for cp in fetch(ce_ref[0], 0):
            cp.start()

    @pl.when(used & (first_ref[c] == 1))
    def _():
        slot = slot_ref[c]
        for cp in fetch(ce_ref[c], slot):
            cp.wait()

        @pl.when(next_ref[c] >= 0)
        def _():
            for cp in fetch(next_ref[c], 1 - slot):
                cp.start()

        wgu_b[...] = wgu_f32[slot].astype(MXU)
        wd_b[...] = wd_f32[slot].astype(MXU)

    @pl.when(used)
    def _():
        x = _from_token_tiles(x_ref, EXPERT_CHUNK)
        gu = _dot(x.astype(MXU), wgu_b[...])
        hid = jax.nn.silu(gu[:, :D_EXPERT]) * gu[:, D_EXPERT:]
        _to_token_tiles(o_ref, _dot(hid.astype(MXU), wd_b[...]))

    @pl.when(jnp.logical_not(used))
    def _():
        o_ref[...] = jnp.zeros_like(o_ref)


def expert_ffn(xb8, chunk_e, n_used, w_gu, w_down, layer):
    C = EXPERT_CHUNK
    n_chunks = xb8.shape[0] // (C * ROW_TILE)
    D = ROW_TILE * LANE
    idx = jnp.arange(n_chunks, dtype=jnp.int32)
    used = idx < n_used[0]
    first = used & ((idx == 0) | (chunk_e != jnp.roll(chunk_e, 1)))
    slot = (jnp.cumsum(first.astype(jnp.int32)) - 1) % 2
    none = jnp.int32(N_EXPERTS)
    later = jnp.roll(jnp.where(first, chunk_e, none), -1).at[-1].set(none)
    next_e = lax.cummin(later, axis=0, reverse=True)
    next_e = jnp.where(next_e == none, -1, next_e)
    tile_spec = pl.BlockSpec((C * ROW_TILE, LANE), lambda c, *_: (c, 0))
    any_spec = pl.BlockSpec(memory_space=pl.ANY)
    grid_spec = pltpu.PrefetchScalarGridSpec(
        num_scalar_prefetch=5,
        grid=(n_chunks,),
        in_specs=[tile_spec, any_spec, any_spec],
        out_specs=tile_spec,
        scratch_shapes=[pltpu.VMEM((2, D, 2 * D_EXPERT), w_gu.dtype), pltpu.VMEM((2, D_EXPERT, D), w_down.dtype),
                        pltpu.VMEM((D, 2 * D_EXPERT), MXU), pltpu.VMEM((D_EXPERT, D), MXU),
                        pltpu.SemaphoreType.DMA((2, 2))],
    )
    return pl.pallas_call(
        functools.partial(_ffn_kernel, layer=layer),
        grid_spec=grid_spec,
        out_shape=jax.ShapeDtypeStruct(xb8.shape, f32),
        compiler_params=_params(1),
        name="expert_ffn",
    )(chunk_e, first.astype(jnp.int32), slot.astype(jnp.int32), next_e.astype(jnp.int32), n_used, xb8, w_gu, w_down)


_COPY_WINDOW = 512


def _windowed_copies(n, copy, wait):
    assert _COPY_WINDOW & (_COPY_WINDOW - 1) == 0 and n >= _COPY_WINDOW

    def fill(j, carry):
        copy(j, j).start()
        return carry

    def steady(j, carry):
        s = jnp.bitwise_and(j, _COPY_WINDOW - 1)
        wait(s)
        copy(j, s).start()
        return carry

    def drain(s, carry):
        wait(s)
        return carry

    lax.fori_loop(0, _COPY_WINDOW, fill, 0, unroll=8)
    lax.fori_loop(_COPY_WINDOW, n, steady, 0, unroll=8)
    lax.fori_loop(0, _COPY_WINDOW, drain, 0, unroll=8)


def _dispatch_kernel(dest_ref, tail_ref, nu_ref, x8_ref, xb_hbm, sems, zeros, *, tm):
    n = tm * TOPK_IN_GROUP
    base = pl.program_id(0) * n

    @pl.when(pl.program_id(0) == 0)
    def _():
        zeros[...] = jnp.zeros_like(zeros)
        chunk_rows = zeros.shape[0]
        n_chunks = xb_hbm.shape[0] // chunk_rows

        def fill(row, s):
            return pltpu.make_async_copy(zeros, xb_hbm.at[pl.ds(pl.multiple_of(row, ROW_TILE), chunk_rows)], sems.at[s])

        def tail_fill(e):
            return fill(jnp.maximum(tail_ref[e], 0) * ROW_TILE, e)

        def free_fill(c):
            return fill(c * chunk_rows, N_EXPERTS + c - nu_ref[0])

        for e in range(N_EXPERTS):
            @pl.when(tail_ref[e] >= 0)
            def _():
                tail_fill(e).start()

        lax.fori_loop(nu_ref[0], n_chunks, lambda c, carry: (free_fill(c).start(), carry)[1], 0)
        for e in range(N_EXPERTS):
            @pl.when(tail_ref[e] >= 0)
            def _():
                tail_fill(e).wait()

        lax.fori_loop(nu_ref[0], n_chunks, lambda c, carry: (free_fill(c).wait(), carry)[1], 0)

    def copy(j, s):
        t = pl.multiple_of(jnp.right_shift(j, 1) * ROW_TILE, ROW_TILE)
        d = pl.multiple_of(dest_ref[base + j] * ROW_TILE, ROW_TILE)
        return pltpu.make_async_copy(x8_ref.at[pl.ds(t, ROW_TILE)], xb_hbm.at[pl.ds(d, ROW_TILE)], sems.at[s])

    def wait(s):
        pltpu.make_async_copy(x8_ref.at[pl.ds(0, ROW_TILE)], xb_hbm.at[pl.ds(0, ROW_TILE)], sems.at[s]).wait()

    _windowed_copies(n, copy, wait)


def moe_dispatch(x8, dest, tail, n_used, n_slots, tm=_TM_DISPATCH):
    n_tok = x8.shape[0] // ROW_TILE
    n_chunks = n_slots // EXPERT_CHUNK
    assert n_tok % tm == 0 and tm * TOPK_IN_GROUP >= _COPY_WINDOW
    assert _COPY_WINDOW >= N_EXPERTS + n_chunks - n_tok * TOPK_IN_GROUP // EXPERT_CHUNK
    grid_spec = pltpu.PrefetchScalarGridSpec(
        num_scalar_prefetch=3,
        grid=(n_tok // tm,),
        in_specs=[pl.BlockSpec((tm * ROW_TILE, LANE), lambda i, *_: (i, 0))],
        out_specs=pl.BlockSpec(memory_space=pl.ANY),
        scratch_shapes=[pltpu.SemaphoreType.DMA((_COPY_WINDOW,)),
                        pltpu.VMEM((EXPERT_CHUNK * ROW_TILE, LANE), x8.dtype)],
    )
    return pl.pallas_call(
        functools.partial(_dispatch_kernel, tm=tm),
        grid_spec=grid_spec,
        out_shape=jax.ShapeDtypeStruct((n_slots * ROW_TILE, LANE), x8.dtype),
        compiler_params=_params(1),
        name="moe_dispatch",
    )(dest, tail, n_used, x8)


def _combine_kernel(dest_ref, x_ref, r_ref, g_ref, b_ref, yb_hbm, o_ref, buf, sems):
    tm = x_ref.shape[0]
    n = tm * TOPK_IN_GROUP
    i = pl.program_id(0)

    def issue(tile):
        dst_buf = buf.at[jnp.bitwise_and(tile, 1)]

        def body(j, carry):
            src = pl.multiple_of(dest_ref[tile * n + j] * ROW_TILE, ROW_TILE)
            dst = pl.multiple_of(j * ROW_TILE, ROW_TILE)
            pltpu.make_async_copy(yb_hbm.at[pl.ds(src, ROW_TILE)], dst_buf.at[pl.ds(dst, ROW_TILE)],
                                  sems.at[j]).start()
            return carry

        lax.fori_loop(0, n, body, 0, unroll=8)

    def drain(j, carry):
        pltpu.make_async_copy(yb_hbm.at[pl.ds(0, ROW_TILE)], buf.at[0, pl.ds(0, ROW_TILE)], sems.at[j]).wait()
        return carry

    @pl.when(i == 0)
    def _():
        issue(i)

    lax.fori_loop(0, n, drain, 0, unroll=8)

    @pl.when(i + 1 < pl.num_programs(0))
    def _():
        issue(i + 1)

    cur = buf.at[jnp.bitwise_and(i, 1)]
    r = r_ref[...]
    lane = lax.broadcasted_iota(jnp.int32, r.shape, 1)
    w0 = jnp.sum(jnp.where(lane == _R_W0, r, 0.0), axis=1, keepdims=True)
    w1 = jnp.sum(jnp.where(lane == _R_W1, r, 0.0), axis=1, keepdims=True)
    pair = TOPK_IN_GROUP * ROW_TILE
    y = _from_token_tiles(cur, tm, 0, pair) * w0 + _from_token_tiles(cur, tm, ROW_TILE, pair) * w1
    o_ref[...] = _ln(DN_ALPHA * x_ref[...] + y, g_ref[...], b_ref[...])


def combine_ln(x2, yb8, dest, r, g, b, tm=_TM_ROWS):
    N, D = x2.shape
    assert N % tm == 0
    row = lambda w: pl.BlockSpec((tm, w), lambda i, dest: (i, 0))
    vec = pl.BlockSpec((1, D), lambda i, dest: (0, 0))
    grid_spec = pltpu.PrefetchScalarGridSpec(
        num_scalar_prefetch=1,
        grid=(N // tm,),
        in_specs=[row(D), row(LANE), vec, vec, pl.BlockSpec(memory_space=pl.ANY)],
        out_specs=row(D),
        scratch_shapes=[pltpu.VMEM((2, tm * TOPK_IN_GROUP * ROW_TILE, LANE), f32),
                        pltpu.SemaphoreType.DMA((tm * TOPK_IN_GROUP,))],
    )
    return pl.pallas_call(
        _combine_kernel,
        grid_spec=grid_spec,
        out_shape=jax.ShapeDtypeStruct((N, D), f32),
        compiler_params=_params(1),
        name="moe_combine_ln",
    )(dest, x2, r, g.reshape(1, D), b.reshape(1, D), yb8)


def hier_moe_ln(x2, x8, rg_w, rg_b, re_w, re_b, w_gu, w_down, layer, g, b):
    N, D = x2.shape
    E, C, K = N_EXPERTS, EXPERT_CHUNK, TOPK_IN_GROUP
    A = N * K
    r, cnt = moe_router(x2, rg_w, rg_b, re_w, re_b)
    e = r[:, _R_E0:_R_E1 + 1].astype(jnp.int32)
    rank = r[:, _R_RANK0:_R_RANK1 + 1].astype(jnp.int32)
    counts = cnt[:E, 0].astype(jnp.int32)
    padded = (counts + C - 1) // C * C
    pad_end = jnp.cumsum(padded)
    below = jnp.arange(E, dtype=jnp.int32)[None, None, :] < e[..., None]
    dest = rank + jnp.sum(jnp.where(below, padded[None, None, :], 0), axis=-1)
    n_chunks = -(-(A + E * (C - 1)) // C)
    P = n_chunks * C
    chunk_start = jnp.arange(n_chunks, dtype=jnp.int32) * C
    chunk_e = jnp.minimum(jnp.sum((pad_end[None, :] <= chunk_start[:, None]).astype(jnp.int32), axis=1), E - 1)
    n_used = (pad_end[-1] // C).reshape(1).astype(jnp.int32)
    dest = dest.reshape(A).astype(jnp.int32)
    tail = jnp.where(padded > 0, pad_end - C, -1).astype(jnp.int32)
    xb8 = moe_dispatch(x8, dest, tail, n_used, P)
    yb8 = expert_ffn(xb8, chunk_e.astype(jnp.int32), n_used, w_gu, w_down, layer)
    return combine_ln(x2, yb8, dest, r, g, b)


def kernel(x, mem, w_in, nsa_cmp_pos, nsa_cmp_w1, nsa_cmp_w2, rnn_conv_w, rnn_conv_b, rnn_ga_w, rnn_ga_b,
           rnn_gx_w, rnn_gx_b, rnn_lambda, mla_q_norm, mla_kv_norm, mla_w_uq, mla_w_ukv, w_branch, w_out,
           ln1_g, ln1_b, x_wq, x_wkv, x_wo, ln2_g, ln2_b, moe_rg_w, moe_rg_b, moe_re_w, moe_re_b,
           moe_w_gu, moe_w_down, ln3_g, ln3_b):
    B, S, D = x.shape
    N = B * S
    x2 = x.reshape(N, D)
    w_cat = _cat_w_in(w_in)
    for l in range(DEPTH):
        h2 = matmul(x2, w_cat, l, 512, _IN_TILE)
        h = h2.reshape(B, S, N_CAT)
        o_a = nsa_attention(h, nsa_cmp_pos[l], nsa_cmp_w1[l], nsa_cmp_w2[l])
        o_b = stick_breaking_attention(h)
        o_c = rglru_block(h, rnn_conv_w[l], rnn_conv_b[l], rnn_ga_w[l], rnn_ga_b[l], rnn_gx_w[l], rnn_gx_b[l],
                          rnn_lambda[l])
        o_d = mla_attention(h, mla_q_norm[l], mla_kv_norm[l], mla_w_uq[l], mla_w_ukv[l])
        branches = [o.reshape(N, MIX_W) for o in (o_a, o_b, o_c, o_d)]
        x2 = merge_ln(h2, branches, w_branch[l], w_out[l], x2, ln1_g[l], ln1_b[l])
        x3, x8 = cross_attention_ln(x2.reshape(B, S, D), mem, x_wq[l], x_wkv[l], x_wo[l], ln2_g[l], ln2_b[l])
        x2 = hier_moe_ln(x3.reshape(N, D), x8.reshape(N * ROW_TILE, LANE), moe_rg_w[l], moe_rg_b[l], moe_re_w[l],
                         moe_re_b[l], moe_w_gu, moe_w_down, l, ln3_g[l], ln3_b[l])
    return x2.reshape(B, S, D)
```

```python
import functools

import numpy as np
import jax
import jax.numpy as jnp
from jax import lax
from jax.experimental import pallas as pl
from jax.experimental.pallas import tpu as pltpu

D_MODEL = 1024
DEPTH = 4
HEAD_DIM = 64
N_MIXERS = 4
MIX_W = 256
NSA_HEADS = 4
NSA_KV_HEADS = 2
NSA_GROUP = NSA_HEADS // NSA_KV_HEADS
CMP_LEN = 32
CMP_STRIDE = 16
CMP_HID = 256
SEL_BLOCK = 64
SEL_TOPK = 8
WINDOW = 512
FORCE_SCORE = 1e4
SB_HEADS = 4
RNN_W = 256
CONV_W = 4
LRU_C = 8.0
MLA_HEADS = 4
MLA_Q_RANK = 192
MLA_KV_RANK = 128
MLA_NOPE = 64
MLA_ROPE = 32
MLA_V = 64
ROPE_THETA = 10000.0
X_HEADS = 4
X_HEAD_DIM = 128
N_GROUPS = 4
EXPERTS_PER_GROUP = 8
N_EXPERTS = N_GROUPS * EXPERTS_PER_GROUP
TOPK_IN_GROUP = 2
D_EXPERT = 512
EXPERT_CHUNK = 256
DN_ALPHA = (2.0 * DEPTH) ** 0.25
LN_EPS = 1e-5
RMS_EPS = 1e-6

IN_SPLITS = ((NSA_HEADS * HEAD_DIM,) + (NSA_KV_HEADS * HEAD_DIM,) * 6 + (NSA_HEADS * 3,)
             + (SB_HEADS * HEAD_DIM,) * 3
             + (RNN_W, RNN_W)
             + (MLA_Q_RANK, MLA_KV_RANK, MLA_ROPE)
             + (N_MIXERS * D_MODEL,))
IN_OFFSETS = tuple(int(o) for o in np.concatenate([[0], np.cumsum(IN_SPLITS)[:-1]]))

LANE = 128
VMEM_LIMIT = 48 * 1024 * 1024
NEG = -1e30
BIG_NEG = -2.0 ** 100

f32 = jnp.float32
MXU = jnp.bfloat16

_CQ_PAD = 2 * LANE
_W_Q, _W_KV, _W_KVT = NSA_HEADS * HEAD_DIM, NSA_KV_HEADS * HEAD_DIM, NSA_KV_HEADS * LANE
_SECTIONS = (("mg", N_MIXERS * D_MODEL), ("aq", _W_Q), ("kvs", _W_KVT), ("kvw", _W_KVT),
             ("bq", SB_HEADS * HEAD_DIM), ("bk", SB_HEADS * HEAD_DIM), ("bv", SB_HEADS * HEAD_DIM),
             ("cx", RNN_W), ("cg", RNN_W), ("dcq", _CQ_PAD), ("kc", _W_KV), ("vc", _W_KV),
             ("ag", LANE), ("dckv", MLA_KV_RANK), ("dkr", LANE), ("dkrs", LANE))
_IN_TILE = 3584
OFF = {}
_o = 0
for _n, _w in _SECTIONS:
    OFF[_n] = _o
    _o += _w
N_CAT = -(-_o // _IN_TILE) * _IN_TILE

_SEL_ROWS = 32
_AUG_SEL = HEAD_DIM
_AUG_POS = HEAD_DIM + _SEL_ROWS
_POS_SPLIT = 256

_TQ_NSA = 512
_TQ_MLA = 1024
_TQ_SB = 256
_TM_ROWS = 256
_TM_DENSE = 512
_TM_DISPATCH = 256
_TR_MLA_PREP = 1024


def _rot_half(t):
    d = t.shape[-1]
    return jnp.concatenate([-t[..., d // 2:], t[..., :d // 2]], axis=-1)


def _cat_w_in(w):
    def sec(i):
        return w[..., IN_OFFSETS[i]:IN_OFFSETS[i] + IN_SPLITS[i]]

    def pair(k, v):
        return jnp.concatenate([k[..., :HEAD_DIM], v[..., :HEAD_DIM], k[..., HEAD_DIM:], v[..., HEAD_DIM:]], axis=-1)

    def pad_cols(t, before, after):
        return jnp.pad(t, ((0, 0),) * (t.ndim - 1) + ((before, after),))

    def padc(t, n):
        return pad_cols(t, 0, n - t.shape[-1])

    def rope_slot(t):
        return pad_cols(t, MLA_NOPE, LANE - MLA_NOPE - MLA_ROPE)

    parts = {"mg": sec(16), "aq": sec(0), "kc": sec(1), "vc": sec(2), "kvs": pair(sec(3), sec(4)),
             "kvw": pair(sec(5), sec(6)), "ag": padc(sec(7), LANE), "bq": sec(8), "bk": sec(9),
             "bv": sec(10), "cx": sec(11), "cg": sec(12), "dcq": padc(sec(13), _CQ_PAD), "dckv": sec(14),
             "dkr": rope_slot(sec(15)), "dkrs": rope_slot(_rot_half(sec(15)))}
    cat = jnp.concatenate([parts[n] for n, _ in _SECTIONS], axis=-1)
    return padc(cat, N_CAT).astype(MXU)


def _dot(a, b):
    return jnp.dot(a, b, preferred_element_type=f32)


def _dot_nt(a, b):
    return lax.dot_general(a, b, (((1,), (1,)), ((), ())), preferred_element_type=f32)


def _params(n_axes):
    return pltpu.CompilerParams(dimension_semantics=("arbitrary",) * n_axes, vmem_limit_bytes=VMEM_LIMIT)


def _mm_kernel(a_ref, b_ref, o_ref):
    o_ref[...] = _dot(a_ref[...].astype(MXU), b_ref[0]).astype(o_ref.dtype)


def matmul(a, b, layer, tm, tn, out_dtype=f32):
    M, K = a.shape
    _, _, N = b.shape
    assert M % tm == 0 and N % tn == 0
    return pl.pallas_call(
        _mm_kernel,
        grid=(N // tn, M // tm),
        in_specs=[pl.BlockSpec((tm, K), lambda j, i: (i, 0)),
                  pl.BlockSpec((1, K, tn), lambda j, i: (layer, 0, j))],
        out_specs=pl.BlockSpec((tm, tn), lambda j, i: (i, j)),
        out_shape=jax.ShapeDtypeStruct((M, N), out_dtype),
        compiler_params=_params(2),
        name="matmul",
    )(a, b)


def _cmp_kernel(t_ref, w1_ref, pos_ref, w2_ref, o_ref):
    nc = t_ref.shape[1] // CMP_STRIDE
    t = jnp.concatenate([t_ref[0, pl.ds(l, nc, stride=CMP_STRIDE), :] for l in range(CMP_STRIDE)], axis=1)
    half = t.shape[1]
    t = t.astype(MXU)
    y1 = _dot(t, w1_ref[0, :half, :])
    y2 = _dot(t, w1_ref[0, half:, :])
    pos = jnp.broadcast_to(pos_ref[0], (8, 2 * half)).astype(MXU)
    pc = _dot(pos, w1_ref[0])[0:1]
    nc = y2.shape[0]
    hid = y1 + pltpu.roll(y2, nc - 1, 0) + pc
    o_ref[0, 0] = _dot(jax.nn.gelu(hid).astype(MXU), w2_ref[0])


def nsa_compress(h, cmp_w1, cmp_pos, cmp_w2):
    B, S, _ = h.shape
    G = NSA_KV_HEADS
    NC = S // CMP_STRIDE
    F = CMP_STRIDE * G * HEAD_DIM
    assert G * HEAD_DIM == LANE and OFF["vc"] == OFF["kc"] + LANE
    col0 = OFF["kc"] // LANE
    def group_diag(w, axis):
        blocks = [jnp.concatenate([w if k == g else jnp.zeros_like(w) for k in range(G)], axis=-1) for g in range(G)]
        return jnp.stack(blocks, axis=axis)

    w1 = group_diag(cmp_w1.reshape(2, CMP_LEN, HEAD_DIM, CMP_HID).astype(MXU), 2)
    w1 = w1.reshape(2, 2 * F, G * CMP_HID)
    pos = jnp.broadcast_to(cmp_pos[:, :, None, :], (2, CMP_LEN, G, HEAD_DIM)).reshape(2, 1, 2 * F)
    w2 = group_diag(jnp.concatenate([cmp_w2, cmp_w2], axis=-1).astype(MXU), 1)
    w2 = w2.reshape(2, G * CMP_HID, G * LANE)
    return pl.pallas_call(
        _cmp_kernel,
        grid=(2, B),
        in_specs=[pl.BlockSpec((1, S, LANE), lambda j, i: (i, 0, col0 + j)),
                  pl.BlockSpec((1, 2 * F, G * CMP_HID), lambda j, i: (j, 0, 0)),
                  pl.BlockSpec((1, 1, 2 * F), lambda j, i: (j, 0, 0)),
                  pl.BlockSpec((1, G * CMP_HID, G * LANE), lambda j, i: (j, 0, 0))],
        out_specs=pl.BlockSpec((1, 1, NC, G * LANE), lambda j, i: (j, i, 0, 0)),
        out_shape=jax.ShapeDtypeStruct((2, B, NC, G * LANE), f32),
        compiler_params=_params(2),
        name="nsa_compress",
    )(h, w1, pos, w2)


def _nsa_kernel(q_ref, kc_ref, vc_ref, kvs_ref, kvw_ref, gl_ref, cover_ref, aug_ref, cb_ref, wb_ref,
                o_ref, ksa, vsa, kwa, vwa, *, tq, n_cmp, n_sel, n_top):
    tk = tq
    g = pl.program_id(1)
    qi = pl.program_id(2)
    q0 = pl.multiple_of(qi * tq, tq)
    lane = lax.broadcasted_iota(jnp.int32, (tq, LANE), 1)
    lo_half = lane < HEAD_DIM

    @pl.when(qi == 0)
    def _():
        real = lax.broadcasted_iota(jnp.int32, ksa.shape, 1) < HEAD_DIM
        aug = aug_ref[...]
        ones = jnp.ones(ksa.shape, MXU)
        for kv_ref, k_out, v_out in ((kvs_ref, ksa, vsa), (kvw_ref, kwa, vwa)):
            kv = kv_ref[0]
            k_out[...] = jnp.where(real, kv.astype(MXU), aug)
            v_out[...] = jnp.where(real, pltpu.roll(kv, HEAD_DIM, 1).astype(MXU), ones)

    q = q_ref[0] * (HEAD_DIM ** -0.5)
    q_heads = (jnp.where(lo_half, q, 0.0), jnp.where(lo_half, pltpu.roll(q, HEAD_DIM, 1), 0.0))
    alibi = [2.0 ** (-8.0 * (h + 1) / NSA_HEADS) for h in range(NSA_HEADS)]
    slopes = [jnp.where(g == 0, alibi[n], alibi[NSA_GROUP + n]) for n in range(NSA_GROUP)]
    pos_cols = [jnp.where(lane == _AUG_POS, slopes[n] * _POS_SPLIT, jnp.where(lane == _AUG_POS + 1, slopes[n], 0.0))
                for n in range(NSA_GROUP)]
    tpos = q0 + lax.broadcasted_iota(jnp.int32, (tq, 1), 0)

    nc = kc_ref.shape[2]
    cidx = lax.broadcasted_iota(jnp.int32, (1, nc), 1)
    dist_c = tpos - (cidx * CMP_STRIDE + (CMP_LEN - 1))
    mask_c = (dist_c >= 0) & (cidx < n_cmp)
    dist_cf = dist_c.astype(f32)
    kc = kc_ref[0, 0].astype(MXU)
    vc = vc_ref[0, 0].astype(MXU)
    o_cmp = []
    imp_t = jnp.zeros((LANE, tq), f32)
    for n in range(NSA_GROUP):
        s = _dot_nt(q_heads[n].astype(MXU), kc) - slopes[n] * dist_cf
        sm = jnp.where(mask_c, s, NEG)
        m = jnp.max(sm, axis=1, keepdims=True)
        p = jnp.where(mask_c, jnp.exp(sm - m), 0.0)
        p = (p / jnp.maximum(jnp.sum(p, axis=1, keepdims=True), 1e-30)).astype(MXU)
        o_cmp.append(_dot(p, vc))
        imp_t = imp_t + _dot_nt(cover_ref[...], p)

    rows = _SEL_ROWS
    imp = imp_t[:rows]
    blk = lax.broadcasted_iota(jnp.int32, (rows, tq), 0)
    blk_f = blk.astype(f32)
    tpos_t = q0 + lax.broadcasted_iota(jnp.int32, (1, tq), 1)
    forced = (blk == 0) | (blk == jnp.right_shift(tpos_t, SEL_BLOCK.bit_length() - 1))
    valid = blk * SEL_BLOCK <= tpos_t
    imp = jnp.where(forced, FORCE_SCORE, jnp.where(valid, imp, -1.0))
    imp = jnp.where(blk < n_sel, imp, NEG)
    sel_t = jnp.zeros((rows, tq), f32)
    for _ in range(n_top):
        m = jnp.max(imp, axis=0, keepdims=True)
        first = jnp.min(jnp.where(imp == m, blk_f, float(LANE)), axis=0, keepdims=True)
        pick = blk_f == first
        sel_t = jnp.where(pick, 1.0, sel_t)
        imp = jnp.where(pick, 2 * NEG, imp)
    sel = jnp.concatenate([sel_t, jnp.zeros((LANE - rows, tq), f32)], axis=0).T
    sel_bias = pltpu.roll(jnp.where(sel > 0.5, 0.0, BIG_NEG), _AUG_SEL, 1)
    sel_cols = jnp.where((lane >= _AUG_SEL) & (lane < _AUG_SEL + rows), sel_bias, 0.0)
    q_sel = jnp.concatenate([q_heads[n] + sel_cols + pos_cols[n] for n in range(NSA_GROUP)], axis=0).astype(MXU)
    q_win = jnp.concatenate([q_heads[n] + pos_cols[n] for n in range(NSA_GROUP)], axis=0).astype(MXU)
    rows_q = NSA_GROUP * tq
    denom_lane = lax.broadcasted_iota(jnp.int32, (rows_q, LANE), 1) == HEAD_DIM

    def update(carry, qs, k, v, bias):
        m, acc = carry
        s = _dot_nt(qs, k)
        if bias is not None:
            s = s + bias
        m_new = jnp.maximum(m, jnp.max(s, axis=1, keepdims=True))
        p = jnp.exp(s - m_new)
        return m_new, jnp.exp(m - m_new) * acc + _dot(p.astype(MXU), v)

    def finish(carry):
        _, acc = carry
        denom = jnp.sum(jnp.where(denom_lane, acc, 0.0), axis=1, keepdims=True)
        o = acc / jnp.maximum(denom, 1e-30)
        return [o[n * tq:(n + 1) * tq] for n in range(NSA_GROUP)]

    init = (jnp.full((rows_q, 1), NEG, f32), jnp.zeros((rows_q, LANE), f32))
    causal = cb_ref[...]

    def sel_body(kt, carry):
        k0 = pl.multiple_of(kt * tk, tk)
        return update(carry, q_sel, ksa[pl.ds(k0, tk), :], vsa[pl.ds(k0, tk), :], None)

    carry = lax.fori_loop(0, qi, sel_body, init)
    o_sel = finish(update(carry, q_sel, ksa[pl.ds(q0, tk), :], vsa[pl.ds(q0, tk), :], causal))

    carry = init
    n_back = WINDOW // tk
    for back in range(n_back, -1, -1):
        k0 = pl.multiple_of(jnp.maximum(qi - back, 0) * tk, tk)
        if back:
            off = jnp.where(qi >= back, 0.0, BIG_NEG)
            bias = wb_ref[...] + off if back == n_back else off
        else:
            bias = causal
        carry = update(carry, q_win, kwa[pl.ds(k0, tk), :], vwa[pl.ds(k0, tk), :], bias)
    o_win = finish(carry)

    sig = jax.nn.sigmoid(gl_ref[0])

    def gate(n, j):
        col = 3 * (NSA_GROUP * g + n) + j
        return jnp.sum(jnp.where(lane == col, sig, 0.0), axis=1, keepdims=True)

    o = [gate(n, 0) * o_cmp[n] + gate(n, 1) * o_sel[n] + gate(n, 2) * o_win[n] for n in range(NSA_GROUP)]
    o_ref[0] = jnp.where(lo_half, o[0], pltpu.roll(o[1], HEAD_DIM, 1))


def nsa_attention(h, cmp_pos, cmp_w1, cmp_w2, tq=_TQ_NSA):
    B, S, _ = h.shape
    G = NSA_KV_HEADS
    NC = S // CMP_STRIDE
    n_cmp = (S - CMP_LEN) // CMP_STRIDE + 1
    n_sel = S // SEL_BLOCK
    n_top = min(SEL_TOPK, n_sel)
    F = CMP_STRIDE * HEAD_DIM
    assert S % tq == 0 and WINDOW % tq == 0 and tq % SEL_BLOCK == 0 and n_sel <= _SEL_ROWS and S <= _POS_SPLIT ** 2

    kvc = nsa_compress(h, cmp_w1, cmp_pos, cmp_w2)

    c0 = np.arange(n_cmp)[:, None] * CMP_STRIDE
    j0 = np.arange(n_sel)[None, :] * SEL_BLOCK
    cover = np.clip(np.minimum(c0 + CMP_LEN, j0 + SEL_BLOCK) - np.maximum(c0, j0), 0, None) / CMP_LEN
    cover_t = np.zeros((LANE, NC), np.float32)
    cover_t[:n_sel, :n_cmp] = cover.T
    pos = np.arange(S)
    aug = np.zeros((S, LANE), np.float32)
    aug[pos, _AUG_SEL + pos // SEL_BLOCK] = 1.0
    aug[:, _AUG_POS] = pos // _POS_SPLIT
    aug[:, _AUG_POS + 1] = pos % _POS_SPLIT
    rel = np.arange(tq)[:, None] - np.arange(tq)[None, :]
    rel = np.tile(rel, (NSA_GROUP, 1))
    causal = np.where(rel >= 0, 0.0, BIG_NEG).astype(np.float32)
    win_lo = np.where(rel < 0, 0.0, BIG_NEG).astype(np.float32)

    col = lambda name: OFF[name] // LANE
    full2 = lambda shape: pl.BlockSpec(shape, lambda b, g, i: (0, 0))
    q_spec = pl.BlockSpec((1, tq, LANE), lambda b, g, i: (b, i, col("aq") + g))
    c_specs = [pl.BlockSpec((1, 1, NC, LANE), lambda b, g, i, j=j: (j, b, 0, g)) for j in range(2)]
    kv_specs = [pl.BlockSpec((1, S, LANE), lambda b, g, i, c=col(n): (b, 0, c + g))
                for n in ("kvs", "kvw")]
    gl_spec = pl.BlockSpec((1, tq, LANE), lambda b, g, i: (b, i, col("ag")))
    return pl.pallas_call(
        functools.partial(_nsa_kernel, tq=tq, n_cmp=n_cmp, n_sel=n_sel, n_top=n_top),
        grid=(B, G, S // tq),
        in_specs=[q_spec] + c_specs + kv_specs + [gl_spec, full2((LANE, NC)), full2((S, LANE)),
                                                  full2((NSA_GROUP * tq, tq)), full2((NSA_GROUP * tq, tq))],
        out_specs=pl.BlockSpec((1, tq, LANE), lambda b, g, i: (b, i, g)),
        out_shape=jax.ShapeDtypeStruct((B, S, NSA_HEADS * HEAD_DIM), f32),
        scratch_shapes=[pltpu.VMEM((S, LANE), MXU)] * 4,
        compiler_params=_params(3),
        name="nsa_attention",
    )(h, kvc, kvc, h, h, h, jnp.asarray(cover_t, MXU), jnp.asarray(aug, MXU), jnp.asarray(causal),
      jnp.asarray(win_lo))


_SB_DEAD = -104.0


def _log_sigmoid(z):
    return jnp.minimum(z, 0.0) - jnp.log1p(jnp.exp(-jnp.abs(z)))


def _sb_kernel(q_ref, k_ref, v_ref, u_ref, o_ref, kb, vb, *, tq):
    tk = tq
    qi = pl.program_id(2)
    q0 = pl.multiple_of(qi * tq, tq)
    lane = lax.broadcasted_iota(jnp.int32, (tq, LANE), 1)
    lo_half = lane < HEAD_DIM

    @pl.when(qi == 0)
    def _():
        kb[...] = k_ref[0].astype(MXU)
        vb[...] = v_ref[0].astype(MXU)

    q = q_ref[0] * (HEAD_DIM ** -0.5)
    q2 = jnp.concatenate([jnp.where(lo_half, q, 0.0), jnp.where(lo_half, 0.0, q)], axis=0).astype(MXU)
    u = u_ref[...]

    def tile(carry, k, v, strict):
        c, acc = carry
        z = _dot_nt(q2, k)
        ls = _log_sigmoid(z)
        log_1m = ls - z
        if strict is not None:
            log_1m = jnp.where(strict, log_1m, 0.0)
        hi = log_1m.astype(MXU)
        lo = (log_1m - hi.astype(f32)).astype(MXU)
        tail = _dot(hi, u) + _dot(lo, u) + c
        a = jnp.exp(ls + tail)
        if strict is not None:
            a = jnp.where(strict, a, 0.0)
        return c + jnp.sum(log_1m, axis=1, keepdims=True), acc + _dot(a.astype(MXU), v)

    row = lax.broadcasted_iota(jnp.int32, (2 * tq, tk), 0)
    rel = jnp.where(row >= tq, row - tq, row) - lax.broadcasted_iota(jnp.int32, (2 * tq, tk), 1)
    zero = (jnp.zeros((2 * tq, 1), f32), jnp.zeros((2 * tq, LANE), f32))
    state = tile(zero, kb[pl.ds(q0, tk), :], vb[pl.ds(q0, tk), :], rel > 0)

    def alive(state):
        return (jnp.max(state[0]) > _SB_DEAD).astype(jnp.int32)

    def cond(loop):
        kt, live, _ = loop
        return (kt >= 0) & (live > 0)

    def body(loop):
        kt, _, state = loop
        k0 = pl.multiple_of(kt * tk, tk)
        state = tile(state, kb[pl.ds(k0, tk), :], vb[pl.ds(k0, tk), :], None)
        return kt - 1, alive(state), state

    _, _, (_, acc) = lax.while_loop(cond, body, (qi - 1, alive(state), state))
    o_ref[0] = jnp.where(lo_half, acc[:tq], acc[tq:])


def stick_breaking_attention(h, tq=_TQ_SB):
    B, S, _ = h.shape
    assert S % tq == 0
    tri = jnp.asarray(np.tril(np.ones((tq, tq), np.float32), -1), MXU)
    cq, ck, cv = (OFF[n] // LANE for n in ("bq", "bk", "bv"))
    return pl.pallas_call(
        functools.partial(_sb_kernel, tq=tq),
        grid=(B, SB_HEADS // 2, S // tq),
        in_specs=[pl.BlockSpec((1, tq, LANE), lambda b, p, i: (b, i, cq + p)),
                  pl.BlockSpec((1, S, LANE), lambda b, p, i: (b, 0, ck + p)),
                  pl.BlockSpec((1, S, LANE), lambda b, p, i: (b, 0, cv + p)),
                  pl.BlockSpec((tq, tq), lambda b, p, i: (0, 0))],
        out_specs=pl.BlockSpec((1, tq, LANE), lambda b, p, i: (b, i, p)),
        out_shape=jax.ShapeDtypeStruct((B, S, SB_HEADS * HEAD_DIM), f32),
        scratch_shapes=[pltpu.VMEM((S, LANE), MXU)] * 2,
        compiler_params=_params(3),
        name="sb_attention",
    )(h, h, h, tri)


def _neg_expm1(y):
    series = -y * (1.0 + y * (1.0 / 2 + y * (1.0 / 6 + y * (1.0 / 24 + y * (1.0 / 120)))))
    return jnp.where(y > -0.1, series, 1.0 - jnp.exp(y))


def _rglru_kernel(x_ref, xg_ref, cw_ref, cb_ref, gaw_ref, gab_ref, gxw_ref, gxb_ref, lam_ref, o_ref):
    x = x_ref[0]
    S = x.shape[0]
    row = lax.broadcasted_iota(jnp.int32, (S, 1), 0)

    def shifted(t, d, fill):
        return jnp.where(row >= d, pltpu.roll(t, d, 0), fill)

    u = cb_ref[...] + x * cw_ref[CONV_W - 1:CONV_W, :]
    for d in range(1, CONV_W):
        u = u + shifted(x, d, 0.0) * cw_ref[CONV_W - 1 - d:CONV_W - d, :]
    ub = u.astype(MXU)
    r = jax.nn.sigmoid(_dot(ub, gaw_ref[...]) + gab_ref[...])
    i = jax.nn.sigmoid(_dot(ub, gxw_ref[...]) + gxb_ref[...])
    lam = lam_ref[...]
    softplus_neg = jnp.maximum(-lam, 0.0) + jnp.log1p(jnp.exp(-jnp.abs(lam)))
    log_a = -LRU_C * r * softplus_neg
    a = jnp.exp(log_a)
    b = jnp.sqrt(_neg_expm1(2.0 * log_a)) * (i * u)
    d = 1
    while d < S:
        b = a * shifted(b, d, 0.0) + b
        a = a * shifted(a, d, 1.0)
        d *= 2
    o_ref[0] = b * jax.nn.gelu(xg_ref[0])


def _block_diag(w):
    n, c, _ = w.shape
    out = jnp.zeros((n * c, n * c), w.dtype)
    for j in range(n):
        out = out.at[j * c:(j + 1) * c, j * c:(j + 1) * c].set(w[j])
    return out


def rglru_block(h, conv_w, conv_b, ga_w, ga_b, gx_w, gx_b, lru_lambda):
    B, S, _ = h.shape
    W = RNN_W
    cx, cg = OFF["cx"] // W, OFF["cg"] // W
    vec = pl.BlockSpec((1, W), lambda b: (0, 0))
    mat = pl.BlockSpec((W, W), lambda b: (0, 0))
    return pl.pallas_call(
        _rglru_kernel,
        grid=(B,),
        in_specs=[pl.BlockSpec((1, S, W), lambda b: (b, 0, cx)), pl.BlockSpec((1, S, W), lambda b: (b, 0, cg)),
                  pl.BlockSpec((CONV_W, W), lambda b: (0, 0)), vec, mat, vec, mat, vec, vec],
        out_specs=pl.BlockSpec((1, S, W), lambda b: (b, 0, 0)),
        out_shape=jax.ShapeDtypeStruct((B, S, W), f32),
        compiler_params=_params(1),
        name="rglru",
    )(h, h, conv_w, conv_b.reshape(1, W), _block_diag(ga_w).astype(MXU), ga_b.reshape(1, W),
      _block_diag(gx_w).astype(MXU), gx_b.reshape(1, W), lru_lambda.reshape(1, W))


def _rms(x, g, width):
    return x * lax.rsqrt(jnp.sum(x * x, axis=-1, keepdims=True) * (1.0 / width) + RMS_EPS) * g


def _mla_prep_kernel(cq_ref, ckv_ref, kr_ref, krs_ref, gq_ref, gkv_ref, wq_ref, wqs_ref, wk_ref, wv_ref,
                     cosq_ref, sinq_ref, cosk_ref, sink_ref, vone_ref, q_ref, k_ref, v_ref):
    cq = _rms(cq_ref[0], gq_ref[...], MLA_Q_RANK).astype(MXU)
    ckv = _rms(ckv_ref[0], gkv_ref[...], MLA_KV_RANK).astype(MXU)
    scale = (MLA_NOPE + MLA_ROPE) ** -0.5
    q = _dot(cq, wq_ref[...]) * cosq_ref[...] + _dot(cq, wqs_ref[...]) * sinq_ref[...]
    q_ref[0] = (q * scale).astype(q_ref.dtype)
    k_rope = kr_ref[0] * cosk_ref[...] + krs_ref[0] * sink_ref[...]
    k = _dot(ckv, wk_ref[...])
    k_ref[0] = (k + jnp.concatenate([k_rope] * MLA_HEADS, axis=1)).astype(k_ref.dtype)
    v_ref[0] = (_dot(ckv, wv_ref[...]) + vone_ref[...]).astype(v_ref.dtype)


def _mla_attn_kernel(q_ref, k_ref, v_ref, cb_ref, o_ref, *, tq):
    tk = tq
    qi = pl.program_id(2)
    q0 = pl.multiple_of(qi * tq, tq)
    q = q_ref[0]
    lane = lax.broadcasted_iota(jnp.int32, (tq, LANE), 1)

    def update(carry, n, k, v, bias):
        m, acc = carry
        sl = slice(n * LANE, (n + 1) * LANE)
        s = _dot_nt(q[:, sl], k[:, sl])
        if bias is not None:
            s = s + bias
        m_new = jnp.maximum(m, jnp.max(s, axis=1, keepdims=True))
        p = jnp.exp(s - m_new)
        return m_new, jnp.exp(m - m_new) * acc + _dot(p.astype(MXU), v[:, sl])

    def body(kt, carry):
        k0 = pl.multiple_of(kt * tk, tk)
        k = k_ref[0, pl.ds(k0, tk), :]
        v = v_ref[0, pl.ds(k0, tk), :]
        return tuple(update(carry[n], n, k, v, None) for n in range(2))

    init = tuple((jnp.full((tq, 1), NEG, f32), jnp.zeros((tq, LANE), f32)) for _ in range(2))
    carry = lax.fori_loop(0, qi, body, init)
    k = k_ref[0, pl.ds(q0, tk), :]
    v = v_ref[0, pl.ds(q0, tk), :]
    o = []
    for n in range(2):
        _, acc = update(carry[n], n, k, v, cb_ref[...])
        denom = jnp.sum(jnp.where(lane == MLA_V, acc, 0.0), axis=1, keepdims=True)
        o.append(acc / jnp.maximum(denom, 1e-30))
    o_ref[0] = jnp.where(lane < MLA_V, o[0], pltpu.roll(o[1], MLA_V, 1))


def mla_attention(h, q_norm, kv_norm, w_uq, w_ukv, tr=_TR_MLA_PREP, tq=_TQ_MLA):
    B, S, _ = h.shape
    H = MLA_HEADS
    dq = MLA_NOPE + MLA_ROPE
    HW = H * LANE
    wq3 = w_uq.reshape(MLA_Q_RANK, H, dq)
    wq_rot = jnp.concatenate([jnp.zeros_like(wq3[..., :MLA_NOPE]), _rot_half(wq3[..., MLA_NOPE:])], axis=-1)

    def pad_q(w3):
        w3 = jnp.pad(w3, ((0, _CQ_PAD - MLA_Q_RANK), (0, 0), (0, LANE - dq)))
        return w3.reshape(_CQ_PAD, HW).astype(MXU)

    wkv3 = w_ukv.reshape(MLA_KV_RANK, H, MLA_NOPE + MLA_V)
    wk = jnp.pad(wkv3[..., :MLA_NOPE], ((0, 0), (0, 0), (0, LANE - MLA_NOPE))).reshape(MLA_KV_RANK, HW).astype(MXU)
    wv = jnp.pad(wkv3[..., MLA_NOPE:], ((0, 0), (0, 0), (0, LANE - MLA_V))).reshape(MLA_KV_RANK, HW).astype(MXU)
    v_one = jnp.tile(jnp.concatenate([jnp.zeros((1, MLA_V), f32), jnp.ones((1, LANE - MLA_V), f32)], axis=1), (1, H))
    gq = jnp.pad(q_norm, (0, _CQ_PAD - MLA_Q_RANK)).reshape(1, _CQ_PAD)
    gkv = kv_norm.reshape(1, MLA_KV_RANK)
    inv = ROPE_THETA ** (-jnp.arange(0, MLA_ROPE, 2, dtype=f32) / MLA_ROPE)
    ang = jnp.arange(S, dtype=f32)[:, None] * inv[None, :]
    cos2 = jnp.concatenate([jnp.cos(ang), jnp.cos(ang)], axis=1)
    sin2 = jnp.concatenate([jnp.sin(ang), jnp.sin(ang)], axis=1)
    tail = LANE - dq
    cos_k = jnp.concatenate([jnp.zeros((S, MLA_NOPE), f32), cos2, jnp.zeros((S, tail), f32)], axis=1)
    sin_k = jnp.concatenate([jnp.zeros((S, MLA_NOPE), f32), sin2, jnp.zeros((S, tail), f32)], axis=1)
    cos_q = jnp.tile(jnp.concatenate([jnp.ones((S, MLA_NOPE), f32), cos2, jnp.zeros((S, tail), f32)], axis=1), (1, H))
    sin_q = jnp.tile(sin_k, (1, H))
    rel = np.arange(tq)[:, None] - np.arange(tq)[None, :]
    causal = jnp.asarray(np.where(rel >= 0, 0.0, BIG_NEG).astype(np.float32))

    c_cq, c_ckv, c_kr, c_krs = OFF["dcq"] // _CQ_PAD, OFF["dckv"] // LANE, OFF["dkr"] // LANE, OFF["dkrs"] // LANE
    full = lambda shape: pl.BlockSpec(shape, lambda b, i: (0, 0))
    tab = lambda w: pl.BlockSpec((tr, w), lambda b, i: (i, 0))
    out3 = pl.BlockSpec((1, tr, HW), lambda b, i: (b, i, 0))
    q, k, v = pl.pallas_call(
        _mla_prep_kernel,
        grid=(B, S // tr),
        in_specs=[pl.BlockSpec((1, tr, _CQ_PAD), lambda b, i: (b, i, c_cq)),
                  pl.BlockSpec((1, tr, LANE), lambda b, i: (b, i, c_ckv)),
                  pl.BlockSpec((1, tr, LANE), lambda b, i: (b, i, c_kr)),
                  pl.BlockSpec((1, tr, LANE), lambda b, i: (b, i, c_krs)),
                  full((1, _CQ_PAD)), full((1, MLA_KV_RANK)), full((_CQ_PAD, HW)), full((_CQ_PAD, HW)),
                  full((MLA_KV_RANK, HW)), full((MLA_KV_RANK, HW)),
                  tab(HW), tab(HW), tab(LANE), tab(LANE), full((1, HW))],
        out_specs=[out3, out3, out3],
        out_shape=[jax.ShapeDtypeStruct((B, S, HW), MXU)] * 3,
        compiler_params=_params(2),
        name="mla_prep",
    )(h, h, h, h, gq, gkv, pad_q(wq3), pad_q(wq_rot), wk, wv, cos_q, sin_q, cos_k, sin_k, v_one)
    pair = lambda rows: pl.BlockSpec((1, rows, 2 * LANE), lambda b, p, i: (b, i if rows == tq else 0, p))
    return pl.pallas_call(
        functools.partial(_mla_attn_kernel, tq=tq),
        grid=(B, H // 2, S // tq),
        in_specs=[pair(tq), pair(S), pair(S), pl.BlockSpec((tq, tq), lambda b, p, i: (0, 0))],
        out_specs=pl.BlockSpec((1, tq, LANE), lambda b, p, i: (b, i, p)),
        out_shape=jax.ShapeDtypeStruct((B, S, H * MLA_V), f32),
        compiler_params=_params(3),
        name="mla_attention",
    )(q, k, v, causal)


def _ln(z, g, b):
    mu = jnp.mean(z, axis=-1, keepdims=True)
    zc = z - mu
    var = jnp.mean(zc * zc, axis=-1, keepdims=True)
    return zc * lax.rsqrt(var + LN_EPS) * g + b


def _merge_kernel(mg_ref, oa_ref, ob_ref, oc_ref, od_ref, wb_ref, wo_ref, x_ref, g_ref, b_ref, o_ref):
    acc = None
    for n, br in enumerate((oa_ref, ob_ref, oc_ref, od_ref)):
        up = _dot(br[...].astype(MXU), wb_ref[n])
        term = jax.nn.sigmoid(mg_ref[:, n * D_MODEL:(n + 1) * D_MODEL]) * up
        acc = term if acc is None else acc + term
    y = _dot(acc.astype(MXU), wo_ref[...])
    o_ref[...] = _ln(DN_ALPHA * x_ref[...] + y, g_ref[...], b_ref[...])


def merge_ln(h2, branches, w_branch, w_out, x2, g, b, tm=_TM_DENSE):
    N, D = x2.shape
    assert OFF["mg"] == 0 and N % tm == 0
    row = lambda w: pl.BlockSpec((tm, w), lambda i: (i, 0))
    return pl.pallas_call(
        _merge_kernel,
        grid=(N // tm,),
        in_specs=[row(N_MIXERS * D)] + [row(MIX_W)] * N_MIXERS
        + [pl.BlockSpec((N_MIXERS, MIX_W, D), lambda i: (0, 0, 0)), pl.BlockSpec((D, D), lambda i: (0, 0)),
           row(D), pl.BlockSpec((1, D), lambda i: (0, 0)), pl.BlockSpec((1, D), lambda i: (0, 0))],
        out_specs=row(D),
        out_shape=jax.ShapeDtypeStruct((N, D), f32),
        compiler_params=_params(1),
        name="merge_ln",
    )(h2, *branches, w_branch.astype(MXU), w_out.astype(MXU), x2, g.reshape(1, D), b.reshape(1, D))


ROW_TILE = 8


def _to_token_tiles(ref, val):
    rows = val.shape[0]
    for j in range(ROW_TILE):
        ref[pl.ds(j, rows, stride=ROW_TILE), :] = val[:, j * LANE:(j + 1) * LANE]


def _from_token_tiles(ref, rows, first=0, stride=ROW_TILE):
    return jnp.concatenate([ref[pl.ds(first + j, rows, stride=stride), :] for j in range(ROW_TILE)], axis=1)


def _xattn_kernel(x_ref, wq_ref, k_ref, v_ref, wo_ref, g_ref, b_ref, o_ref, o8_ref):
    x = x_ref[0]
    q = _dot(x.astype(MXU), wq_ref[...]).astype(MXU)
    k = k_ref[0]
    v = v_ref[0]
    heads = []
    for hd in range(X_HEADS):
        sl = slice(hd * X_HEAD_DIM, (hd + 1) * X_HEAD_DIM)
        s = _dot_nt(q[:, sl], k[:, sl]) * (X_HEAD_DIM ** -0.5)
        e = jnp.exp(s - jnp.max(s, axis=1, keepdims=True))
        p = e / jnp.sum(e, axis=1, keepdims=True)
        heads.append(_dot(p.astype(MXU), v[:, sl]).astype(MXU))
    y = _dot(jnp.concatenate(heads, axis=1), wo_ref[...])
    out = _ln(DN_ALPHA * x + y, g_ref[...], b_ref[...])
    o_ref[0] = out
    _to_token_tiles(o8_ref.at[0], out)


def cross_attention_ln(x, mem, wq, wkv, wo, g, b, tq=_TM_DENSE):
    B, S, D = x.shape
    assert D == ROW_TILE * LANE
    M = mem.shape[1]
    F = X_HEADS * X_HEAD_DIM
    kv = matmul(mem.reshape(B * M, D), wkv.astype(MXU)[None], 0, 512, 2 * F, out_dtype=MXU).reshape(B, M, 2 * F)
    full = lambda shape: pl.BlockSpec(shape, lambda bi, i: (0,) * len(shape))
    return pl.pallas_call(
        _xattn_kernel,
        grid=(B, S // tq),
        in_specs=[pl.BlockSpec((1, tq, D), lambda bi, i: (bi, i, 0)), full((D, F)),
                  pl.BlockSpec((1, M, F), lambda bi, i: (bi, 0, 0)), pl.BlockSpec((1, M, F), lambda bi, i: (bi, 0, 1)),
                  full((F, D)), full((1, D)), full((1, D))],
        out_specs=[pl.BlockSpec((1, tq, D), lambda bi, i: (bi, i, 0)),
                   pl.BlockSpec((1, tq * ROW_TILE, LANE), lambda bi, i: (bi, i, 0))],
        out_shape=[jax.ShapeDtypeStruct((B, S, D), f32), jax.ShapeDtypeStruct((B, S * ROW_TILE, LANE), f32)],
        compiler_params=_params(2),
        name="cross_attention_ln",
    )(x, wq.astype(MXU), kv, kv, wo.astype(MXU), g.reshape(1, D), b.reshape(1, D))


_R_E0, _R_E1, _R_W0, _R_W1, _R_RANK0, _R_RANK1 = range(6)
_GRP_LANE0 = N_EXPERTS


def _router_kernel(x_ref, w_ref, b_ref, tri_ref, r_ref, cnt_ref):
    i = pl.program_id(0)
    tm = x_ref.shape[0]
    logits = _dot_nt(w_ref[...], x_ref[...].astype(MXU)) + b_ref[...]
    row = lax.broadcasted_iota(jnp.int32, (LANE, tm), 0)
    row_f = row.astype(f32)
    big = float(LANE)

    def cmax(t):
        return jnp.max(t, axis=0, keepdims=True)

    def first_row(cond):
        return jnp.min(jnp.where(cond, row_f, big), axis=0, keepdims=True)

    def softmax_on(mask):
        lm = jnp.where(mask, logits, NEG)
        e = jnp.where(mask, jnp.exp(lm - cmax(lm)), 0.0)
        return e / jnp.sum(e, axis=0, keepdims=True)

    is_g = (row >= _GRP_LANE0) & (row < _GRP_LANE0 + N_GROUPS)
    p_grp = softmax_on(is_g)
    p_g = cmax(p_grp)
    grp = first_row(is_g & (p_grp == p_g)) - float(_GRP_LANE0)
    grp_of_row = jnp.right_shift(row, EXPERTS_PER_GROUP.bit_length() - 1)
    in_grp = (row < N_EXPERTS) & (grp_of_row.astype(f32) == grp)
    p_e = softmax_on(in_grp)
    p1 = cmax(jnp.where(in_grp, p_e, -1.0))
    e1 = first_row(in_grp & (p_e == p1))
    rest = in_grp & (row_f != e1)
    p2 = cmax(jnp.where(rest, p_e, -1.0))
    e2 = first_row(rest & (p_e == p2))
    w1 = p_g * p1 / (p1 + p2)
    w2 = p_g * p2 / (p1 + p2)

    @pl.when(i == 0)
    def _():
        cnt_ref[...] = jnp.zeros_like(cnt_ref)

    oh1 = row_f == e1
    oh2 = row_f == e2
    both = (oh1 | oh2).astype(MXU)
    before = _dot(both, tri_ref[...]) + cnt_ref[:, 0:1]
    rank1 = jnp.sum(jnp.where(oh1, before, 0.0), axis=0, keepdims=True)
    rank2 = jnp.sum(jnp.where(oh2, before, 0.0), axis=0, keepdims=True)
    cnt_ref[...] = cnt_ref[...] + jnp.sum(both.astype(f32), axis=1, keepdims=True)

    out = jnp.zeros((LANE, tm), f32)
    for slot, val in ((_R_E0, e1), (_R_E1, e2), (_R_W0, w1), (_R_W1, w2), (_R_RANK0, rank1), (_R_RANK1, rank2)):
        out = jnp.where(row == slot, val, out)
    r_ref[...] = out.T


def moe_router(x2, rg_w, rg_b, re_w, re_b, tm=_TM_DENSE):
    N, D = x2.shape
    assert EXPERTS_PER_GROUP & (EXPERTS_PER_GROUP - 1) == 0 and N_EXPERTS + N_GROUPS <= LANE
    w = jnp.pad(jnp.concatenate([re_w, rg_w], axis=1), ((0, 0), (0, LANE - N_EXPERTS - N_GROUPS))).T.astype(MXU)
    b = jnp.pad(jnp.concatenate([re_b, rg_b]), (0, LANE - N_EXPERTS - N_GROUPS)).reshape(LANE, 1)
    tri = jnp.asarray(np.triu(np.ones((tm, tm), np.float32), 1), MXU)
    return pl.pallas_call(
        _router_kernel,
        grid=(N // tm,),
        in_specs=[pl.BlockSpec((tm, D), lambda i: (i, 0)), pl.BlockSpec((LANE, D), lambda i: (0, 0)),
                  pl.BlockSpec((LANE, 1), lambda i: (0, 0)), pl.BlockSpec((tm, tm), lambda i: (0, 0))],
        out_specs=[pl.BlockSpec((tm, LANE), lambda i: (i, 0)), pl.BlockSpec((LANE, LANE), lambda i: (0, 0))],
        out_shape=[jax.ShapeDtypeStruct((N, LANE), f32), jax.ShapeDtypeStruct((LANE, LANE), f32)],
        compiler_params=_params(1),
        name="moe_router",
    )(x2, w, b, tri)


def _ffn_kernel(ce_ref, first_ref, slot_ref, next_ref, nu_ref, x_ref, wgu_hbm, wd_hbm, o_ref,
                wgu_f32, wd_f32, wgu_b, wd_b, sems, *, layer):
    c = pl.program_id(0)
    used = c < nu_ref[0]

    def fetch(e, slot):
        return (pltpu.make_async_copy(wgu_hbm.at[layer, e], wgu_f32.at[slot], sems.at[slot, 0]),
                pltpu.make_async_copy(wd_hbm.at[layer, e], wd_f32.at[slot], sems.at[slot, 1]))

    @pl.when(used & (c == 0))
    def _():
        for cp in fetch(ce_ref[0], 0):
            cp.start()

    @pl.when(used & (first_ref[c] == 1))
    def _():
        slot = slot_ref[c]
        for cp in fetch(ce_ref[c], slot):
            cp.wait()

        @pl.when(next_ref[c] >= 0)
        def _():
            for cp in fetch(next_ref[c], 1 - slot):
                cp.start()

        wgu_b[...] = wgu_f32[slot].astype(MXU)
        wd_b[...] = wd_f32[slot].astype(MXU)

    @pl.when(used)
    def _():
        x = _from_token_tiles(x_ref, EXPERT_CHUNK)
        gu = _dot(x.astype(MXU), wgu_b[...])
        hid = jax.nn.silu(gu[:, :D_EXPERT]) * gu[:, D_EXPERT:]
        _to_token_tiles(o_ref, _dot(hid.astype(MXU), wd_b[...]))

    @pl.when(jnp.logical_not(used))
    def _():
        o_ref[...] = jnp.zeros_like(o_ref)


def expert_ffn(xb8, chunk_e, n_used, w_gu, w_down, layer):
    C = EXPERT_CHUNK
    n_chunks = xb8.shape[0] // (C * ROW_TILE)
    D = ROW_TILE * LANE
    idx = jnp.arange(n_chunks, dtype=jnp.int32)
    used = idx < n_used[0]
    first = used & ((idx == 0) | (chunk_e != jnp.roll(chunk_e, 1)))
    slot = (jnp.cumsum(first.astype(jnp.int32)) - 1) % 2
    none = jnp.int32(N_EXPERTS)
    later = jnp.roll(jnp.where(first, chunk_e, none), -1).at[-1].set(none)
    next_e = lax.cummin(later, axis=0, reverse=True)
    next_e = jnp.where(next_e == none, -1, next_e)
    tile_spec = pl.BlockSpec((C * ROW_TILE, LANE), lambda c, *_: (c, 0))
    any_spec = pl.BlockSpec(memory_space=pl.ANY)
    grid_spec = pltpu.PrefetchScalarGridSpec(
        num_scalar_prefetch=5,
        grid=(n_chunks,),
        in_specs=[tile_spec, any_spec, any_spec],
        out_specs=tile_spec,
        scratch_shapes=[pltpu.VMEM((2, D, 2 * D_EXPERT), w_gu.dtype), pltpu.VMEM((2, D_EXPERT, D), w_down.dtype),
                        pltpu.VMEM((D, 2 * D_EXPERT), MXU), pltpu.VMEM((D_EXPERT, D), MXU),
                        pltpu.SemaphoreType.DMA((2, 2))],
    )
    return pl.pallas_call(
        functools.partial(_ffn_kernel, layer=layer),
        grid_spec=grid_spec,
        out_shape=jax.ShapeDtypeStruct(xb8.shape, f32),
        compiler_params=_params(1),
        name="expert_ffn",
    )(chunk_e, first.astype(jnp.int32), slot.astype(jnp.int32), next_e.astype(jnp.int32), n_used, xb8, w_gu, w_down)


_COPY_WINDOW = 512


def _windowed_copies(n, copy, wait):
    assert _COPY_WINDOW & (_COPY_WINDOW - 1) == 0 and n >= _COPY_WINDOW

    def fill(pair, carry):
        for prio in range(2):
            j = 2 * pair + prio
            copy(j, j).start(priority=prio)
        return carry

    def steady(pair, carry):
        for prio in range(2):
            j = 2 * pair + prio
            s = jnp.bitwise_and(j, _COPY_WINDOW - 1)
            wait(s)
            copy(j, s).start(priority=prio)
        return carry

    def drain(s, carry):
        wait(s)
        return carry

    assert n % 2 == 0
    lax.fori_loop(0, _COPY_WINDOW // 2, fill, 0, unroll=4)
    lax.fori_loop(_COPY_WINDOW // 2, n // 2, steady, 0, unroll=4)
    lax.fori_loop(0, _COPY_WINDOW, drain, 0, unroll=8)


def _dispatch_kernel(dest_ref, tail_ref, nu_ref, x8_ref, xb_hbm, sems, zeros, *, tm):
    n = tm * TOPK_IN_GROUP
    base = pl.program_id(0) * n

    @pl.when(pl.program_id(0) == 0)
    def _():
        zeros[...] = jnp.zeros_like(zeros)
        chunk_rows = zeros.shape[0]
        n_chunks = xb_hbm.shape[0] // chunk_rows

        def fill(row, s):
            return pltpu.make_async_copy(zeros, xb_hbm.at[pl.ds(pl.multiple_of(row, ROW_TILE), chunk_rows)], sems.at[s])

        def tail_fill(e):
            return fill(jnp.maximum(tail_ref[e], 0) * ROW_TILE, e)

        def free_fill(c):
            return fill(c * chunk_rows, N_EXPERTS + c - nu_ref[0])

        for e in range(N_EXPERTS):
            @pl.when(tail_ref[e] >= 0)
            def _():
                tail_fill(e).start()

        lax.fori_loop(nu_ref[0], n_chunks, lambda c, carry: (free_fill(c).start(), carry)[1], 0)
        for e in range(N_EXPERTS):
            @pl.when(tail_ref[e] >= 0)
            def _():
                tail_fill(e).wait()

        lax.fori_loop(nu_ref[0], n_chunks, lambda c, carry: (free_fill(c).wait(), carry)[1], 0)

    def copy(j, s):
        t = pl.multiple_of(jnp.right_shift(j, 1) * ROW_TILE, ROW_TILE)
        d = pl.multiple_of(dest_ref[base + j] * ROW_TILE, ROW_TILE)
        return pltpu.make_async_copy(x8_ref.at[pl.ds(t, ROW_TILE)], xb_hbm.at[pl.ds(d, ROW_TILE)], sems.at[s])

    def wait(s):
        pltpu.make_async_copy(x8_ref.at[pl.ds(0, ROW_TILE)], xb_hbm.at[pl.ds(0, ROW_TILE)], sems.at[s]).wait()

    _windowed_copies(n, copy, wait)


def moe_dispatch(x8, dest, tail, n_used, n_slots, tm=_TM_DISPATCH):
    n_tok = x8.shape[0] // ROW_TILE
    n_chunks = n_slots // EXPERT_CHUNK
    assert n_tok % tm == 0 and tm * TOPK_IN_GROUP >= _COPY_WINDOW
    assert _COPY_WINDOW >= N_EXPERTS + n_chunks - n_tok * TOPK_IN_GROUP // EXPERT_CHUNK
    grid_spec = pltpu.PrefetchScalarGridSpec(
        num_scalar_prefetch=3,
        grid=(n_tok // tm,),
        in_specs=[pl.BlockSpec((tm * ROW_TILE, LANE), lambda i, *_: (i, 0))],
        out_specs=pl.BlockSpec(memory_space=pl.ANY),
        scratch_shapes=[pltpu.SemaphoreType.DMA((_COPY_WINDOW,)),
                        pltpu.VMEM((EXPERT_CHUNK * ROW_TILE, LANE), x8.dtype)],
    )
    return pl.pallas_call(
        functools.partial(_dispatch_kernel, tm=tm),
        grid_spec=grid_spec,
        out_shape=jax.ShapeDtypeStruct((n_slots * ROW_TILE, LANE), x8.dtype),
        compiler_params=_params(1),
        name="moe_dispatch",
    )(dest, tail, n_used, x8)


def _combine_kernel(dest_ref, x_ref, r_ref, g_ref, b_ref, yb_hbm, o_ref, buf, sems):
    tm = x_ref.shape[0]
    n = tm * TOPK_IN_GROUP
    i = pl.program_id(0)

    def issue(tile):
        dst_buf = buf.at[jnp.bitwise_and(tile, 1)]

        def body(pair, carry):
            for prio in range(TOPK_IN_GROUP):
                j = TOPK_IN_GROUP * pair + prio
                src = pl.multiple_of(dest_ref[tile * n + j] * ROW_TILE, ROW_TILE)
                dst = pl.multiple_of(j * ROW_TILE, ROW_TILE)
                pltpu.make_async_copy(yb_hbm.at[pl.ds(src, ROW_TILE)], dst_buf.at[pl.ds(dst, ROW_TILE)],
                                      sems.at[j]).start(priority=prio)
            return carry

        lax.fori_loop(0, tm, body, 0, unroll=4)

    def drain(j, carry):
        pltpu.make_async_copy(yb_hbm.at[pl.ds(0, ROW_TILE)], buf.at[0, pl.ds(0, ROW_TILE)], sems.at[j]).wait()
        return carry

    @pl.when(i == 0)
    def _():
        issue(i)

    lax.fori_loop(0, n, drain, 0, unroll=8)

    @pl.when(i + 1 < pl.num_programs(0))
    def _():
        issue(i + 1)

    cur = buf.at[jnp.bitwise_and(i, 1)]
    r = r_ref[...]
    lane = lax.broadcasted_iota(jnp.int32, r.shape, 1)
    w0 = jnp.sum(jnp.where(lane == _R_W0, r, 0.0), axis=1, keepdims=True)
    w1 = jnp.sum(jnp.where(lane == _R_W1, r, 0.0), axis=1, keepdims=True)
    pair = TOPK_IN_GROUP * ROW_TILE
    y = _from_token_tiles(cur, tm, 0, pair) * w0 + _from_token_tiles(cur, tm, ROW_TILE, pair) * w1
    o_ref[...] = _ln(DN_ALPHA * x_ref[...] + y, g_ref[...], b_ref[...])


def combine_ln(x2, yb8, dest, r, g, b, tm=_TM_ROWS):
    N, D = x2.shape
    assert N % tm == 0
    row = lambda w: pl.BlockSpec((tm, w), lambda i, dest: (i, 0))
    vec = pl.BlockSpec((1, D), lambda i, dest: (0, 0))
    grid_spec = pltpu.PrefetchScalarGridSpec(
        num_scalar_prefetch=1,
        grid=(N // tm,),
        in_specs=[row(D), row(LANE), vec, vec, pl.BlockSpec(memory_space=pl.ANY)],
        out_specs=row(D),
        scratch_shapes=[pltpu.VMEM((2, tm * TOPK_IN_GROUP * ROW_TILE, LANE), f32),
                        pltpu.SemaphoreType.DMA((tm * TOPK_IN_GROUP,))],
    )
    return pl.pallas_call(
        _combine_kernel,
        grid_spec=grid_spec,
        out_shape=jax.ShapeDtypeStruct((N, D), f32),
        compiler_params=_params(1),
        name="moe_combine_ln",
    )(dest, x2, r, g.reshape(1, D), b.reshape(1, D), yb8)


def hier_moe_ln(x2, x8, rg_w, rg_b, re_w, re_b, w_gu, w_down, layer, g, b):
    N, D = x2.shape
    E, C, K = N_EXPERTS, EXPERT_CHUNK, TOPK_IN_GROUP
    A = N * K
    r, cnt = moe_router(x2, rg_w, rg_b, re_w, re_b)
    e = r[:, _R_E0:_R_E1 + 1].astype(jnp.int32)
    rank = r[:, _R_RANK0:_R_RANK1 + 1].astype(jnp.int32)
    counts = cnt[:E, 0].astype(jnp.int32)
    padded = (counts + C - 1) // C * C
    pad_end = jnp.cumsum(padded)
    below = jnp.arange(E, dtype=jnp.int32)[None, None, :] < e[..., None]
    dest = rank + jnp.sum(jnp.where(below, padded[None, None, :], 0), axis=-1)
    n_chunks = -(-(A + E * (C - 1)) // C)
    P = n_chunks * C
    chunk_start = jnp.arange(n_chunks, dtype=jnp.int32) * C
    chunk_e = jnp.minimum(jnp.sum((pad_end[None, :] <= chunk_start[:, None]).astype(jnp.int32), axis=1), E - 1)
    n_used = (pad_end[-1] // C).reshape(1).astype(jnp.int32)
    dest = dest.reshape(A).astype(jnp.int32)
    tail = jnp.where(padded > 0, pad_end - C, -1).astype(jnp.int32)
    xb8 = moe_dispatch(x8, dest, tail, n_used, P)
    yb8 = expert_ffn(xb8, chunk_e.astype(jnp.int32), n_used, w_gu, w_down, layer)
    return combine_ln(x2, yb8, dest, r, g, b)


def kernel(x, mem, w_in, nsa_cmp_pos, nsa_cmp_w1, nsa_cmp_w2, rnn_conv_w, rnn_conv_b, rnn_ga_w, rnn_ga_b,
           rnn_gx_w, rnn_gx_b, rnn_lambda, mla_q_norm, mla_kv_norm, mla_w_uq, mla_w_ukv, w_branch, w_out,
           ln1_g, ln1_b, x_wq, x_wkv, x_wo, ln2_g, ln2_b, moe_rg_w, moe_rg_b, moe_re_w, moe_re_b,
           moe_w_gu, moe_w_down, ln3_g, ln3_b):
    B, S, D = x.shape
    N = B * S
    x2 = x.reshape(N, D)
    w_cat = _cat_w_in(w_in)
    for l in range(DEPTH):
        h2 = matmul(x2, w_cat, l, 512, _IN_TILE)
        h = h2.reshape(B, S, N_CAT)
        o_a = nsa_attention(h, nsa_cmp_pos[l], nsa_cmp_w1[l], nsa_cmp_w2[l])
        o_b = stick_breaking_attention(h)
        o_c = rglru_block(h, rnn_conv_w[l], rnn_conv_b[l], rnn_ga_w[l], rnn_ga_b[l], rnn_gx_w[l], rnn_gx_b[l],
                          rnn_lambda[l])
        o_d = mla_attention(h, mla_q_norm[l], mla_kv_norm[l], mla_w_uq[l], mla_w_ukv[l])
        branches = [o.reshape(N, MIX_W) for o in (o_a, o_b, o_c, o_d)]
        x2 = merge_ln(h2, branches, w_branch[l], w_out[l], x2, ln1_g[l], ln1_b[l])
        x3, x8 = cross_attention_ln(x2.reshape(B, S, D), mem, x_wq[l], x_wkv[l], x_wo[l], ln2_g[l], ln2_b[l])
        x2 = hier_moe_ln(x3.reshape(N, D), x8.reshape(N * ROW_TILE, LANE), moe_rg_w[l], moe_rg_b[l], moe_re_w[l],
                         moe_re_b[l], moe_w_gu, moe_w_down, l, ln3_g[l], ln3_b[l])
    return x2.reshape(B, S, D)
```
